```python
import math
import jax
import jax.numpy as jnp
from jax import lax
import numpy as np

D_MODEL = 1024
BATCH = 8
SEQ = 4096
DEPTH = 4

N_META = 16
CHUNK = 64
META_PAD = CHUNK - N_META
CONV_K = 4
N_BRANCH = 3
BRANCH_WIDTH = 768

S5_WIDTH = BRANCH_WIDTH
S5_GROUP = 16
S5_GROUPS = S5_WIDTH // S5_GROUP
S5_STATE = 64
S5_STEP_MIN = 1e-3
S5_STEP_MAX = 1e-1

SSD_HEAD_DIM = 64
SSD_HEADS = BRANCH_WIDTH // SSD_HEAD_DIM
SSD_WIDTH = SSD_HEADS * SSD_HEAD_DIM
SSD_GROUPS = 2
SSD_STATE = 128
SSD_CONV_WIDTH = SSD_WIDTH + 2 * SSD_GROUPS * SSD_STATE

GDN_HEAD_DIM = 128
GDN_HEADS = BRANCH_WIDTH // GDN_HEAD_DIM
GDN_WIDTH = GDN_HEADS * GDN_HEAD_DIM

IN_SPLITS = (S5_WIDTH, S5_WIDTH,
             SSD_CONV_WIDTH, SSD_HEADS, SSD_WIDTH,
             3 * GDN_WIDTH, GDN_HEADS, GDN_HEADS, GDN_WIDTH,
             N_BRANCH * D_MODEL)
IN_WIDTH = sum(IN_SPLITS)

ALPHA = (2 * DEPTH) ** 0.25
BETA = (8 * DEPTH) ** -0.25
LN_EPS = 1e-5

kernel_name = 'hybrid_s5_ssd_gdn_gated_merge'


def _split_points():
    pts, acc = [], 0
    for w in IN_SPLITS[:-1]:
        acc += w
        pts.append(acc)
    return pts


def _layer_norm(z, g, b):
    zf = z.astype(jnp.float32)
    mu = jnp.mean(zf, axis=-1, keepdims=True)
    var = jnp.mean(jnp.square(zf - mu), axis=-1, keepdims=True)
    return ((zf - mu) * lax.rsqrt(var + LN_EPS) * g + b).astype(z.dtype)


def _rms_norm(z, g):
    zf = z.astype(jnp.float32)
    return zf * lax.rsqrt(jnp.mean(zf * zf, axis=-1, keepdims=True) + LN_EPS) * g.astype(jnp.float32)


def _l2norm(z):
    return z * lax.rsqrt(jnp.sum(z * z, axis=-1, keepdims=True) + 1e-6)


def _front_pad(z):
    return jnp.pad(z, [(0, 0), (META_PAD, 0)] + [(0, 0)] * (z.ndim - 2))


def _causal_dwconv(u, w):
    k, ch = w.shape
    return lax.conv_general_dilated(u, w.astype(u.dtype)[:, None, :], window_strides=(1,),
                                    padding=[(k - 1, 0)], dimension_numbers=('NWC', 'WIO', 'NWC'),
                                    feature_group_count=ch)


def _complex_affine_combine(earlier, later):
    a1r, a1i, b1r, b1i = earlier
    a2r, a2i, b2r, b2i = later
    return (a2r * a1r - a2i * a1i, a2r * a1i + a2i * a1r,
            a2r * b1r - a2i * b1i + b2r, a2r * b1i + a2i * b1r + b2i)


def _s5_branch(u, z, a_re, a_im, log_step, b_re, b_im, c_re, c_im, d, w_glu, b_glu):
    bsz, t, _ = u.shape
    uf = u.astype(jnp.float32)
    ug = uf.reshape(bsz, t, S5_GROUPS, S5_GROUP)
    lam_re = jnp.minimum(a_re.astype(jnp.float32), -1e-4)
    lam_im = a_im.astype(jnp.float32)
    step = jnp.exp(log_step.astype(jnp.float32))[:, None]
    mag = jnp.exp(lam_re * step)
    abar_re, abar_im = mag * jnp.cos(lam_im * step), mag * jnp.sin(lam_im * step)
    den = lam_re * lam_re + lam_im * lam_im
    nr, ni = abar_re - 1.0, abar_im
    coef_re = (nr * lam_re + ni * lam_im) / den
    coef_im = (ni * lam_re - nr * lam_im) / den
    bbar_re = coef_re[..., None] * b_re - coef_im[..., None] * b_im
    bbar_im = coef_re[..., None] * b_im + coef_im[..., None] * b_re
    bu_re = jnp.einsum('btgc,gpc->btgp', ug, bbar_re)
    bu_im = jnp.einsum('btgc,gpc->btgp', ug, bbar_im)
    ae_re = jnp.broadcast_to(abar_re, (1, t, S5_GROUPS, S5_STATE))
    ae_im = jnp.broadcast_to(abar_im, (1, t, S5_GROUPS, S5_STATE))
    _, _, s_re, s_im = lax.associative_scan(_complex_affine_combine, (ae_re, ae_im, bu_re, bu_im), axis=1)
    y = jnp.einsum('btgp,gcp->btgc', s_re, c_re) - jnp.einsum('btgp,gcp->btgc', s_im, c_im)
    y = y.reshape(bsz, t, S5_WIDTH) + d * uf
    v = jax.nn.gelu(y)
    v = v * jax.nn.sigmoid(v @ w_glu + b_glu)
    return (v * jax.nn.silu(z.astype(jnp.float32))).astype(u.dtype)


def _ssd_chunked(x, dt, a, b, c):
    bsz, t, h, p = x.shape
    g, n = b.shape[2], b.shape[3]
    r = h // g
    nc = t // CHUNK
    xd = (x * dt[..., None]).reshape(bsz, nc, CHUNK, g, r, p)
    acum = jnp.cumsum((dt * a).reshape(bsz, nc, CHUNK, g, r), axis=2)
    bc = b.reshape(bsz, nc, CHUNK, g, n)
    cc = c.reshape(bsz, nc, CHUNK, g, n)
    causal = jnp.tril(jnp.ones((CHUNK, CHUNK), dtype=bool))
    seg = acum[:, :, :, None] - acum[:, :, None]
    decay = jnp.exp(jnp.where(causal[:, :, None, None], seg, -jnp.inf))
    scores = jnp.einsum('bclgn,bcsgn->bclsg', cc, bc)[..., None] * decay
    y_diag = jnp.einsum('bclsgr,bcsgrp->bclgrp', scores, xd)
    to_end = jnp.exp(acum[:, :, -1:] - acum)
    states = jnp.einsum('bcsgn,bcsgr,bcsgrp->bcgrpn', bc, to_end, xd)
    chunk_decay = jnp.exp(acum[:, :, -1])

    def step(state, inp):
        st, dec = inp
        return state * dec[..., None, None] + st, state

    init = jnp.zeros((bsz, g, r, p, n), x.dtype)
    _, prev = lax.scan(step, init, (jnp.moveaxis(states, 1, 0), jnp.moveaxis(chunk_decay, 1, 0)))
    y_off = jnp.einsum('bclgn,cbgrpn,bclgr->bclgrp', cc, prev, jnp.exp(acum))
    return (y_diag + y_off).reshape(bsz, t, h, p)


def _ssd_branch(xbc, dt_raw, z, conv_w, conv_b, dt_bias, a_log, d, norm_g):
    bsz, t, _ = xbc.shape
    xbc = jax.nn.silu(_causal_dwconv(xbc, conv_w) + conv_b).astype(jnp.float32)
    xs, bs, cs = jnp.split(xbc, [SSD_WIDTH, SSD_WIDTH + SSD_GROUPS * SSD_STATE], axis=-1)
    xs = xs.reshape(bsz, t, SSD_HEADS, SSD_HEAD_DIM)
    bs = bs.reshape(bsz, t, SSD_GROUPS, SSD_STATE)
    cs = cs.reshape(bsz, t, SSD_GROUPS, SSD_STATE)
    dt = jax.nn.softplus(dt_raw.astype(jnp.float32) + dt_bias.astype(jnp.float32))
    a = -jnp.exp(a_log.astype(jnp.float32))
    y = _ssd_chunked(_front_pad(xs), _front_pad(dt), a, _front_pad(bs), _front_pad(cs))[:, META_PAD:]
    y = y + xs * d.astype(jnp.float32)[:, None]
    y = y.reshape(bsz, t, SSD_WIDTH) * jax.nn.silu(z.astype(jnp.float32))
    return _rms_norm(y, norm_g).astype(z.dtype)


def _gated_delta_chunked(q, k, v, beta, g):
    bsz, t, h, dk = k.shape
    dv = v.shape[-1]
    nc = t // CHUNK
    q, k, v, beta, g = (u.reshape(bsz, nc, CHUNK, *u.shape[2:]) for u in (q, k, v, beta, g))
    gcum = jnp.cumsum(g, axis=2)
    causal = jnp.tril(jnp.ones((CHUNK, CHUNK), dtype=bool))
    strict = jnp.tril(jnp.ones((CHUNK, CHUNK), dtype=bool), -1)
    seg = gcum[:, :, :, None, :] - gcum[:, :, None, :, :]
    gamma = jnp.exp(jnp.where(causal[:, :, None], seg, -jnp.inf))
    kk = jnp.einsum('bclhd,bcshd->bclsh', k, k)
    a_mat = jnp.where(strict[:, :, None], kk * gamma * beta[:, :, :, None, :], 0.0)
    rhs = jnp.concatenate([v * beta[..., None], k * (beta * jnp.exp(gcum))[..., None]], axis=-1)
    sol = lax.linalg.triangular_solve(a_mat.transpose(0, 1, 4, 2, 3), rhs.transpose(0, 1, 3, 2, 4),
                                      left_side=True, lower=True, unit_diagonal=True)
    u_c, w_c = sol[..., :dv], sol[..., dv:]
    attn = jnp.einsum('bclhd,bcshd->bchls', q, k) * gamma.transpose(0, 1, 4, 2, 3)
    qg = (q * jnp.exp(gcum)[..., None]).transpose(0, 1, 3, 2, 4)
    kd = (k * jnp.exp(gcum[:, :, -1:] - gcum)[..., None]).transpose(0, 1, 3, 2, 4)
    last = jnp.exp(gcum[:, :, -1])

    def step(state, inp):
        u_i, w_i, qg_i, kd_i, attn_i, last_i = inp
        v_new = u_i - jnp.einsum('bhlk,bhkv->bhlv', w_i, state)
        o = jnp.einsum('bhlk,bhkv->bhlv', qg_i, state) + jnp.einsum('bhls,bhsv->bhlv', attn_i, v_new)
        state = state * last_i[..., None, None] + jnp.einsum('bhlk,bhlv->bhkv', kd_i, v_new)
        return state, o

    init = jnp.zeros((bsz, h, dk, dv), q.dtype)
    xs = tuple(jnp.moveaxis(z, 1, 0) for z in (u_c, w_c, qg, kd, attn, last))
    _, o = lax.scan(step, init, xs)
    return o.transpose(1, 0, 3, 2, 4).reshape(bsz, t, h, dv)


def _gdn_branch(qkv, a_raw, b_raw, z, conv_w, dt_bias, a_log, norm_g):
    bsz, t, _ = qkv.shape
    qkv = jax.nn.silu(_causal_dwconv(qkv, conv_w)).astype(jnp.float32)
    q, k, v = jnp.split(qkv, 3, axis=-1)
    heads = lambda u: u.reshape(bsz, t, GDN_HEADS, GDN_HEAD_DIM)
    q = _l2norm(heads(q)) * GDN_HEAD_DIM ** -0.5
    k = _l2norm(heads(k))
    v = heads(v)
    beta = jax.nn.sigmoid(b_raw.astype(jnp.float32))
    g = -jnp.exp(a_log.astype(jnp.float32)) * jax.nn.softplus(a_raw.astype(jnp.float32) + dt_bias.astype(jnp.float32))
    o = _gated_delta_chunked(_front_pad(q), _front_pad(k), _front_pad(v), _front_pad(beta), _front_pad(g))[:, META_PAD:]
    o = _rms_norm(o, norm_g).reshape(bsz, t, GDN_WIDTH) * jax.nn.silu(z.astype(jnp.float32))
    return o.astype(z.dtype)


def _inv_softplus(y):
    return y + jnp.log(-jnp.expm1(-y))


def _log_uniform(key, shape, lo, hi):
    return jnp.exp(jax.random.uniform(key, shape, jnp.float32, math.log(lo), math.log(hi)))


def _fwd_setup_inputs(seed: int = 0) -> dict:
    key = jax.random.key(seed)
    ks = jax.random.split(key, 32)

    def nrm(i, shape, scale):
        return scale * jax.random.normal(ks[i], shape, jnp.float32)

    n_idx = jnp.arange(S5_STATE, dtype=jnp.float32)
    s5_shape = (DEPTH, S5_GROUPS, S5_STATE)
    return {
        'x': nrm(0, (BATCH, SEQ, D_MODEL), 1.0),
        'meta': nrm(1, (N_META, D_MODEL), 1.0),
        'ln_in_g': 1.0 + nrm(2, (D_MODEL,), 0.02),
        'ln_in_b': nrm(3, (D_MODEL,), 0.02),
        'w_in': nrm(4, (DEPTH, D_MODEL, IN_WIDTH), D_MODEL ** -0.5),
        's5_a_re': -0.5 + nrm(5, s5_shape, 0.01),
        's5_a_im': math.pi * n_idx + nrm(6, s5_shape, 0.01),
        's5_log_step': jax.random.uniform(ks[7], (DEPTH, S5_GROUPS), jnp.float32, math.log(S5_STEP_MIN), math.log(S5_STEP_MAX)),
        's5_b_re': nrm(8, (DEPTH, S5_GROUPS, S5_STATE, S5_GROUP), S5_GROUP ** -0.5),
        's5_b_im': nrm(9, (DEPTH, S5_GROUPS, S5_STATE, S5_GROUP), S5_GROUP ** -0.5),
        's5_c_re': nrm(10, (DEPTH, S5_GROUPS, S5_GROUP, S5_STATE), S5_STATE ** -0.5),
        's5_c_im': nrm(11, (DEPTH, S5_GROUPS, S5_GROUP, S5_STATE), S5_STATE ** -0.5),
        's5_d': nrm(12, (DEPTH, S5_WIDTH), 1.0),
        's5_w_glu': nrm(13, (DEPTH, S5_WIDTH, S5_WIDTH), S5_WIDTH ** -0.5),
        's5_b_glu': nrm(14, (DEPTH, S5_WIDTH), 0.02),
        'ssd_conv_w': nrm(15, (DEPTH, CONV_K, SSD_CONV_WIDTH), CONV_K ** -0.5),
        'ssd_conv_b': nrm(16, (DEPTH, SSD_CONV_WIDTH), 0.02),
        'ssd_dt_bias': _inv_softplus(_log_uniform(ks[17], (DEPTH, SSD_HEADS), 1e-3, 1e-1)),
        'ssd_a_log': jnp.log(jax.random.uniform(ks[18], (DEPTH, SSD_HEADS), jnp.float32, 1.0, 16.0)),
        'ssd_d': 1.0 + nrm(19, (DEPTH, SSD_HEADS), 0.1),
        'ssd_norm_g': 1.0 + nrm(20, (DEPTH, SSD_WIDTH), 0.02),
        'gdn_conv_w': nrm(21, (DEPTH, CONV_K, 3 * GDN_WIDTH), CONV_K ** -0.5),
        'gdn_dt_bias': _inv_softplus(_log_uniform(ks[22], (DEPTH, GDN_HEADS), 1e-3, 1e-1)),
        'gdn_a_log': jnp.log(jax.random.uniform(ks[23], (DEPTH, GDN_HEADS), jnp.float32, 1.0, 16.0)),
        'gdn_norm_g': 1.0 + nrm(24, (DEPTH, GDN_HEAD_DIM), 0.02),
        'w_branch': nrm(25, (DEPTH, N_BRANCH, BRANCH_WIDTH, D_MODEL), BRANCH_WIDTH ** -0.5 * BETA),
        'b_gate': nrm(26, (DEPTH, N_BRANCH, D_MODEL), 0.02),
        'w_out': nrm(27, (DEPTH, D_MODEL, D_MODEL), D_MODEL ** -0.5 * BETA),
        'ln_g': 1.0 + nrm(28, (DEPTH, D_MODEL), 0.02),
        'ln_b': nrm(29, (DEPTH, D_MODEL), 0.02),
    }


def _fwd_reference(x, meta, ln_in_g, ln_in_b, w_in, s5_a_re, s5_a_im, s5_log_step, s5_b_re, s5_b_im,
              s5_c_re, s5_c_im, s5_d, s5_w_glu, s5_b_glu, ssd_conv_w, ssd_conv_b, ssd_dt_bias,
              ssd_a_log, ssd_d, ssd_norm_g, gdn_conv_w, gdn_dt_bias, gdn_a_log, gdn_norm_g,
              w_branch, b_gate, w_out, ln_g, ln_b):
    bsz = x.shape[0]
    h = jnp.concatenate([jnp.broadcast_to(meta[None].astype(x.dtype), (bsz, N_META, D_MODEL)), x], axis=1)
    h = _layer_norm(h, ln_in_g, ln_in_b)
    t = h.shape[1]
    points = _split_points()
    for layer in range(DEPTH):
        proj = h @ w_in[layer]
        (s5_u, s5_z, ssd_xbc, ssd_dt, ssd_z, gdn_qkv, gdn_a, gdn_b, gdn_z,
         gate_logits) = jnp.split(proj, points, axis=-1)
        y_a = _s5_branch(s5_u, s5_z, s5_a_re[layer], s5_a_im[layer], s5_log_step[layer], s5_b_re[layer],
                         s5_b_im[layer], s5_c_re[layer], s5_c_im[layer], s5_d[layer], s5_w_glu[layer],
                         s5_b_glu[layer])
        y_b = _ssd_branch(ssd_xbc, ssd_dt, ssd_z, ssd_conv_w[layer], ssd_conv_b[layer], ssd_dt_bias[layer],
                          ssd_a_log[layer], ssd_d[layer], ssd_norm_g[layer])
        y_c = _gdn_branch(gdn_qkv, gdn_a, gdn_b, gdn_z, gdn_conv_w[layer], gdn_dt_bias[layer],
                          gdn_a_log[layer], gdn_norm_g[layer])
        branches = jnp.stack([y_a, y_b, y_c], axis=2)
        outs = jnp.einsum('btnw,nwd->btnd', branches, w_branch[layer])
        gates = jax.nn.sigmoid(gate_logits.reshape(bsz, t, N_BRANCH, D_MODEL) + b_gate[layer])
        merged = jnp.sum(gates * outs, axis=2)
        h = _layer_norm(ALPHA * h + merged @ w_out[layer], ln_g[layer], ln_b[layer])
    return h[:, N_META:]


import jax as _jax
import jax.numpy as _jnp

TWIN_FORMAT = 'train_step'
FWD_PARAMS = ['x', 'meta', 'ln_in_g', 'ln_in_b', 'w_in', 's5_a_re', 's5_a_im', 's5_log_step', 's5_b_re', 's5_b_im', 's5_c_re', 's5_c_im', 's5_d', 's5_w_glu', 's5_b_glu', 'ssd_conv_w', 'ssd_conv_b', 'ssd_dt_bias', 'ssd_a_log', 'ssd_d', 'ssd_norm_g', 'gdn_conv_w', 'gdn_dt_bias', 'gdn_a_log', 'gdn_norm_g', 'w_branch', 'b_gate', 'w_out', 'ln_g', 'ln_b']
TWIN_WEIGHTS = ['meta', 'ln_in_g', 'ln_in_b', 'w_in', 's5_a_re', 's5_a_im', 's5_log_step', 's5_b_re', 's5_b_im', 's5_c_re', 's5_c_im', 's5_d', 's5_w_glu', 's5_b_glu', 'ssd_conv_w', 'ssd_conv_b', 'ssd_dt_bias', 'ssd_a_log', 'ssd_d', 'ssd_norm_g', 'gdn_conv_w', 'gdn_dt_bias', 'gdn_a_log', 'gdn_norm_g', 'w_branch', 'b_gate', 'w_out', 'ln_g', 'ln_b']
TWIN_DIFF_INPUT = 'x'
TWIN_INPUTS = ['x', 'meta', 'ln_in_g', 'ln_in_b', 'w_in', 's5_a_re', 's5_a_im', 's5_log_step', 's5_b_re', 's5_b_im', 's5_c_re', 's5_c_im', 's5_d', 's5_w_glu', 's5_b_glu', 'ssd_conv_w', 'ssd_conv_b', 'ssd_dt_bias', 'ssd_a_log', 'ssd_d', 'ssd_norm_g', 'gdn_conv_w', 'gdn_dt_bias', 'gdn_a_log', 'gdn_norm_g', 'w_branch', 'b_gate', 'w_out', 'ln_g', 'ln_b', 'loss_target', 'm_meta', 'm_ln_in_g', 'm_ln_in_b', 'm_w_in', 'm_s5_a_re', 'm_s5_a_im', 'm_s5_log_step', 'm_s5_b_re', 'm_s5_b_im', 'm_s5_c_re', 'm_s5_c_im', 'm_s5_d', 'm_s5_w_glu', 'm_s5_b_glu', 'm_ssd_conv_w', 'm_ssd_conv_b', 'm_ssd_dt_bias', 'm_ssd_a_log', 'm_ssd_d', 'm_ssd_norm_g', 'm_gdn_conv_w', 'm_gdn_dt_bias', 'm_gdn_a_log', 'm_gdn_norm_g', 'm_w_branch', 'm_b_gate', 'm_w_out', 'm_ln_g', 'm_ln_b', 'v_meta', 'v_ln_in_g', 'v_ln_in_b', 'v_w_in', 'v_s5_a_re', 'v_s5_a_im', 'v_s5_log_step', 'v_s5_b_re', 'v_s5_b_im', 'v_s5_c_re', 'v_s5_c_im', 'v_s5_d', 'v_s5_w_glu', 'v_s5_b_glu', 'v_ssd_conv_w', 'v_ssd_conv_b', 'v_ssd_dt_bias', 'v_ssd_a_log', 'v_ssd_d', 'v_ssd_norm_g', 'v_gdn_conv_w', 'v_gdn_dt_bias', 'v_gdn_a_log', 'v_gdn_norm_g', 'v_w_branch', 'v_b_gate', 'v_w_out', 'v_ln_g', 'v_ln_b']
TWIN_OUTPUTS = ['loss', 'grad_x', 'grad_meta', 'grad_ln_in_g', 'grad_ln_in_b', 'grad_w_in', 'grad_s5_a_re', 'grad_s5_a_im', 'grad_s5_log_step', 'grad_s5_b_re', 'grad_s5_b_im', 'grad_s5_c_re', 'grad_s5_c_im', 'grad_s5_d', 'grad_s5_w_glu', 'grad_s5_b_glu', 'grad_ssd_conv_w', 'grad_ssd_conv_b', 'grad_ssd_dt_bias', 'grad_ssd_a_log', 'grad_ssd_d', 'grad_ssd_norm_g', 'grad_gdn_conv_w', 'grad_gdn_dt_bias', 'grad_gdn_a_log', 'grad_gdn_norm_g', 'grad_w_branch', 'grad_b_gate', 'grad_w_out', 'grad_ln_g', 'grad_ln_b', 'delta_meta', 'delta_ln_in_g', 'delta_ln_in_b', 'delta_w_in', 'delta_s5_a_re', 'delta_s5_a_im', 'delta_s5_log_step', 'delta_s5_b_re', 'delta_s5_b_im', 'delta_s5_c_re', 'delta_s5_c_im', 'delta_s5_d', 'delta_s5_w_glu', 'delta_s5_b_glu', 'delta_ssd_conv_w', 'delta_ssd_conv_b', 'delta_ssd_dt_bias', 'delta_ssd_a_log', 'delta_ssd_d', 'delta_ssd_norm_g', 'delta_gdn_conv_w', 'delta_gdn_dt_bias', 'delta_gdn_a_log', 'delta_gdn_norm_g', 'delta_w_branch', 'delta_b_gate', 'delta_w_out', 'delta_ln_g', 'delta_ln_b', 'new_m_meta', 'new_m_ln_in_g', 'new_m_ln_in_b', 'new_m_w_in', 'new_m_s5_a_re', 'new_m_s5_a_im', 'new_m_s5_log_step', 'new_m_s5_b_re', 'new_m_s5_b_im', 'new_m_s5_c_re', 'new_m_s5_c_im', 'new_m_s5_d', 'new_m_s5_w_glu', 'new_m_s5_b_glu', 'new_m_ssd_conv_w', 'new_m_ssd_conv_b', 'new_m_ssd_dt_bias', 'new_m_ssd_a_log', 'new_m_ssd_d', 'new_m_ssd_norm_g', 'new_m_gdn_conv_w', 'new_m_gdn_dt_bias', 'new_m_gdn_a_log', 'new_m_gdn_norm_g', 'new_m_w_branch', 'new_m_b_gate', 'new_m_w_out', 'new_m_ln_g', 'new_m_ln_b', 'new_v_meta', 'new_v_ln_in_g', 'new_v_ln_in_b', 'new_v_w_in', 'new_v_s5_a_re', 'new_v_s5_a_im', 'new_v_s5_log_step', 'new_v_s5_b_re', 'new_v_s5_b_im', 'new_v_s5_c_re', 'new_v_s5_c_im', 'new_v_s5_d', 'new_v_s5_w_glu', 'new_v_s5_b_glu', 'new_v_ssd_conv_w', 'new_v_ssd_conv_b', 'new_v_ssd_dt_bias', 'new_v_ssd_a_log', 'new_v_ssd_d', 'new_v_ssd_norm_g', 'new_v_gdn_conv_w', 'new_v_gdn_dt_bias', 'new_v_gdn_a_log', 'new_v_gdn_norm_g', 'new_v_w_branch', 'new_v_b_gate', 'new_v_w_out', 'new_v_ln_g', 'new_v_ln_b']
TWIN_LEAF_KINDS = {'loss': 'loss', 'grad_x': 'grad_x', 'grad_meta': 'grad_w', 'grad_ln_in_g': 'grad_w', 'grad_ln_in_b': 'grad_w', 'grad_w_in': 'grad_w', 'grad_s5_a_re': 'grad_w', 'grad_s5_a_im': 'grad_w', 'grad_s5_log_step': 'grad_w', 'grad_s5_b_re': 'grad_w', 'grad_s5_b_im': 'grad_w', 'grad_s5_c_re': 'grad_w', 'grad_s5_c_im': 'grad_w', 'grad_s5_d': 'grad_w', 'grad_s5_w_glu': 'grad_w', 'grad_s5_b_glu': 'grad_w', 'grad_ssd_conv_w': 'grad_w', 'grad_ssd_conv_b': 'grad_w', 'grad_ssd_dt_bias': 'grad_w', 'grad_ssd_a_log': 'grad_w', 'grad_ssd_d': 'grad_w', 'grad_ssd_norm_g': 'grad_w', 'grad_gdn_conv_w': 'grad_w', 'grad_gdn_dt_bias': 'grad_w', 'grad_gdn_a_log': 'grad_w', 'grad_gdn_norm_g': 'grad_w', 'grad_w_branch': 'grad_w', 'grad_b_gate': 'grad_w', 'grad_w_out': 'grad_w', 'grad_ln_g': 'grad_w', 'grad_ln_b': 'grad_w', 'delta_meta': 'delta_w', 'delta_ln_in_g': 'delta_w', 'delta_ln_in_b': 'delta_w', 'delta_w_in': 'delta_w', 'delta_s5_a_re': 'delta_w', 'delta_s5_a_im': 'delta_w', 'delta_s5_log_step': 'delta_w', 'delta_s5_b_re': 'delta_w', 'delta_s5_b_im': 'delta_w', 'delta_s5_c_re': 'delta_w', 'delta_s5_c_im': 'delta_w', 'delta_s5_d': 'delta_w', 'delta_s5_w_glu': 'delta_w', 'delta_s5_b_glu': 'delta_w', 'delta_ssd_conv_w': 'delta_w', 'delta_ssd_conv_b': 'delta_w', 'delta_ssd_dt_bias': 'delta_w', 'delta_ssd_a_log': 'delta_w', 'delta_ssd_d': 'delta_w', 'delta_ssd_norm_g': 'delta_w', 'delta_gdn_conv_w': 'delta_w', 'delta_gdn_dt_bias': 'delta_w', 'delta_gdn_a_log': 'delta_w', 'delta_gdn_norm_g': 'delta_w', 'delta_w_branch': 'delta_w', 'delta_b_gate': 'delta_w', 'delta_w_out': 'delta_w', 'delta_ln_g': 'delta_w', 'delta_ln_b': 'delta_w', 'new_m_meta': 'new_m', 'new_m_ln_in_g': 'new_m', 'new_m_ln_in_b': 'new_m', 'new_m_w_in': 'new_m', 'new_m_s5_a_re': 'new_m', 'new_m_s5_a_im': 'new_m', 'new_m_s5_log_step': 'new_m', 'new_m_s5_b_re': 'new_m', 'new_m_s5_b_im': 'new_m', 'new_m_s5_c_re': 'new_m', 'new_m_s5_c_im': 'new_m', 'new_m_s5_d': 'new_m', 'new_m_s5_w_glu': 'new_m', 'new_m_s5_b_glu': 'new_m', 'new_m_ssd_conv_w': 'new_m', 'new_m_ssd_conv_b': 'new_m', 'new_m_ssd_dt_bias': 'new_m', 'new_m_ssd_a_log': 'new_m', 'new_m_ssd_d': 'new_m', 'new_m_ssd_norm_g': 'new_m', 'new_m_gdn_conv_w': 'new_m', 'new_m_gdn_dt_bias': 'new_m', 'new_m_gdn_a_log': 'new_m', 'new_m_gdn_norm_g': 'new_m', 'new_m_w_branch': 'new_m', 'new_m_b_gate': 'new_m', 'new_m_w_out': 'new_m', 'new_m_ln_g': 'new_m', 'new_m_ln_b': 'new_m', 'new_v_meta': 'new_v', 'new_v_ln_in_g': 'new_v', 'new_v_ln_in_b': 'new_v', 'new_v_w_in': 'new_v', 'new_v_s5_a_re': 'new_v', 'new_v_s5_a_im': 'new_v', 'new_v_s5_log_step': 'new_v', 'new_v_s5_b_re': 'new_v', 'new_v_s5_b_im': 'new_v', 'new_v_s5_c_re': 'new_v', 'new_v_s5_c_im': 'new_v', 'new_v_s5_d': 'new_v', 'new_v_s5_w_glu': 'new_v', 'new_v_s5_b_glu': 'new_v', 'new_v_ssd_conv_w': 'new_v', 'new_v_ssd_conv_b': 'new_v', 'new_v_ssd_dt_bias': 'new_v', 'new_v_ssd_a_log': 'new_v', 'new_v_ssd_d': 'new_v', 'new_v_ssd_norm_g': 'new_v', 'new_v_gdn_conv_w': 'new_v', 'new_v_gdn_dt_bias': 'new_v', 'new_v_gdn_a_log': 'new_v', 'new_v_gdn_norm_g': 'new_v', 'new_v_w_branch': 'new_v', 'new_v_b_gate': 'new_v', 'new_v_w_out': 'new_v', 'new_v_ln_g': 'new_v', 'new_v_ln_b': 'new_v'}


def _forward(args):
    return _fwd_reference(*[args[k] for k in FWD_PARAMS])


def _output_shape():
    def fwd():
        inp = _fwd_setup_inputs(0)
        return _fwd_reference(*[inp[k] for k in FWD_PARAMS])
    out = _jax.eval_shape(fwd)
    return out.shape, out.dtype

N_MICROBATCH = 1
ADAM_LR = 0.001
ADAM_B1 = 0.9
ADAM_B2 = 0.999
ADAM_EPS = 1e-08
ADAM_WD = 0.01
ADAM_STEP = 10
PER_EXAMPLE_BATCH_AXIS = {'x': 0, 'loss_target': 0}
SHARED_INPUTS = []
_WEIGHT_DTYPES = {'meta': _jnp.float32, 'ln_in_g': _jnp.float32, 'ln_in_b': _jnp.float32, 'w_in': _jnp.float32, 's5_a_re': _jnp.float32, 's5_a_im': _jnp.float32, 's5_log_step': _jnp.float32, 's5_b_re': _jnp.float32, 's5_b_im': _jnp.float32, 's5_c_re': _jnp.float32, 's5_c_im': _jnp.float32, 's5_d': _jnp.float32, 's5_w_glu': _jnp.float32, 's5_b_glu': _jnp.float32, 'ssd_conv_w': _jnp.float32, 'ssd_conv_b': _jnp.float32, 'ssd_dt_bias': _jnp.float32, 'ssd_a_log': _jnp.float32, 'ssd_d': _jnp.float32, 'ssd_norm_g': _jnp.float32, 'gdn_conv_w': _jnp.float32, 'gdn_dt_bias': _jnp.float32, 'gdn_a_log': _jnp.float32, 'gdn_norm_g': _jnp.float32, 'w_branch': _jnp.float32, 'b_gate': _jnp.float32, 'w_out': _jnp.float32, 'ln_g': _jnp.float32, 'ln_b': _jnp.float32}
MOMENT_SCALE = {'meta': 7.974020e-04, 'ln_in_g': 1.174710e+00, 'ln_in_b': 5.489573e-01, 'w_in': 6.035069e-03, 's5_a_re': 2.414810e-04, 's5_a_im': 2.392079e-04, 's5_log_step': 2.157019e-01, 's5_b_re': 1.123023e-04, 's5_b_im': 1.130162e-04, 's5_c_re': 2.276753e-04, 's5_c_im': 2.267376e-04, 's5_d': 2.707476e-03, 's5_w_glu': 7.121913e-04, 's5_b_glu': 1.093929e-03, 'ssd_conv_w': 9.883020e-03, 'ssd_conv_b': 1.524051e-02, 'ssd_dt_bias': 2.549763e-02, 'ssd_a_log': 3.886536e-02, 'ssd_d': 5.033075e-02, 'ssd_norm_g': 1.283561e-02, 'gdn_conv_w': 4.973199e-03, 'gdn_dt_bias': 2.353670e-02, 'gdn_a_log': 2.368378e-02, 'gdn_norm_g': 1.736907e-02, 'w_branch': 1.740055e-02, 'b_gate': 2.942379e-03, 'w_out': 3.011528e-02, 'ln_g': 1.607827e+01, 'ln_b': 9.083174e-01}


def _to_microbatches(a, axis):
    t = _jnp.moveaxis(a, axis, 0)
    t = t.reshape((N_MICROBATCH, t.shape[0] // N_MICROBATCH) + t.shape[1:])
    return _jnp.moveaxis(t, 1, axis + 1)


def setup_inputs(seed: int = 0) -> dict:
    inp = _fwd_setup_inputs(seed)
    key = _jax.random.fold_in(_jax.random.key(seed), 7919)
    shape, _ = _output_shape()
    out = dict(inp)
    out["loss_target"] = _jax.random.normal(_jax.random.fold_in(key, 0), shape, _jnp.float32)
    for i, name in enumerate(TWIN_WEIGHTS):
        w = inp[name].astype(_jnp.float32)
        if MOMENT_SCALE is None:
            s = _jnp.sqrt(_jnp.mean(_jnp.square(w)) + 1e-30)
        else:
            s = MOMENT_SCALE[name]
        km, kv = _jax.random.split(_jax.random.fold_in(key, i + 1))
        out[name] = w
        out["m_" + name] = s * _jax.random.normal(km, w.shape, _jnp.float32)
        out["v_" + name] = (s * s) * _jax.random.uniform(kv, w.shape, _jnp.float32, 0.5, 1.5)
    if N_MICROBATCH > 1:
        for name, axis in PER_EXAMPLE_BATCH_AXIS.items():
            out[name] = _to_microbatches(out[name], axis)
    return {'x': out['x'], 'meta': out['meta'], 'ln_in_g': out['ln_in_g'], 'ln_in_b': out['ln_in_b'], 'w_in': out['w_in'], 's5_a_re': out['s5_a_re'], 's5_a_im': out['s5_a_im'], 's5_log_step': out['s5_log_step'], 's5_b_re': out['s5_b_re'], 's5_b_im': out['s5_b_im'], 's5_c_re': out['s5_c_re'], 's5_c_im': out['s5_c_im'], 's5_d': out['s5_d'], 's5_w_glu': out['s5_w_glu'], 's5_b_glu': out['s5_b_glu'], 'ssd_conv_w': out['ssd_conv_w'], 'ssd_conv_b': out['ssd_conv_b'], 'ssd_dt_bias': out['ssd_dt_bias'], 'ssd_a_log': out['ssd_a_log'], 'ssd_d': out['ssd_d'], 'ssd_norm_g': out['ssd_norm_g'], 'gdn_conv_w': out['gdn_conv_w'], 'gdn_dt_bias': out['gdn_dt_bias'], 'gdn_a_log': out['gdn_a_log'], 'gdn_norm_g': out['gdn_norm_g'], 'w_branch': out['w_branch'], 'b_gate': out['b_gate'], 'w_out': out['w_out'], 'ln_g': out['ln_g'], 'ln_b': out['ln_b'], 'loss_target': out['loss_target'], 'm_meta': out['m_meta'], 'm_ln_in_g': out['m_ln_in_g'], 'm_ln_in_b': out['m_ln_in_b'], 'm_w_in': out['m_w_in'], 'm_s5_a_re': out['m_s5_a_re'], 'm_s5_a_im': out['m_s5_a_im'], 'm_s5_log_step': out['m_s5_log_step'], 'm_s5_b_re': out['m_s5_b_re'], 'm_s5_b_im': out['m_s5_b_im'], 'm_s5_c_re': out['m_s5_c_re'], 'm_s5_c_im': out['m_s5_c_im'], 'm_s5_d': out['m_s5_d'], 'm_s5_w_glu': out['m_s5_w_glu'], 'm_s5_b_glu': out['m_s5_b_glu'], 'm_ssd_conv_w': out['m_ssd_conv_w'], 'm_ssd_conv_b': out['m_ssd_conv_b'], 'm_ssd_dt_bias': out['m_ssd_dt_bias'], 'm_ssd_a_log': out['m_ssd_a_log'], 'm_ssd_d': out['m_ssd_d'], 'm_ssd_norm_g': out['m_ssd_norm_g'], 'm_gdn_conv_w': out['m_gdn_conv_w'], 'm_gdn_dt_bias': out['m_gdn_dt_bias'], 'm_gdn_a_log': out['m_gdn_a_log'], 'm_gdn_norm_g': out['m_gdn_norm_g'], 'm_w_branch': out['m_w_branch'], 'm_b_gate': out['m_b_gate'], 'm_w_out': out['m_w_out'], 'm_ln_g': out['m_ln_g'], 'm_ln_b': out['m_ln_b'], 'v_meta': out['v_meta'], 'v_ln_in_g': out['v_ln_in_g'], 'v_ln_in_b': out['v_ln_in_b'], 'v_w_in': out['v_w_in'], 'v_s5_a_re': out['v_s5_a_re'], 'v_s5_a_im': out['v_s5_a_im'], 'v_s5_log_step': out['v_s5_log_step'], 'v_s5_b_re': out['v_s5_b_re'], 'v_s5_b_im': out['v_s5_b_im'], 'v_s5_c_re': out['v_s5_c_re'], 'v_s5_c_im': out['v_s5_c_im'], 'v_s5_d': out['v_s5_d'], 'v_s5_w_glu': out['v_s5_w_glu'], 'v_s5_b_glu': out['v_s5_b_glu'], 'v_ssd_conv_w': out['v_ssd_conv_w'], 'v_ssd_conv_b': out['v_ssd_conv_b'], 'v_ssd_dt_bias': out['v_ssd_dt_bias'], 'v_ssd_a_log': out['v_ssd_a_log'], 'v_ssd_d': out['v_ssd_d'], 'v_ssd_norm_g': out['v_ssd_norm_g'], 'v_gdn_conv_w': out['v_gdn_conv_w'], 'v_gdn_dt_bias': out['v_gdn_dt_bias'], 'v_gdn_a_log': out['v_gdn_a_log'], 'v_gdn_norm_g': out['v_gdn_norm_g'], 'v_w_branch': out['v_w_branch'], 'v_b_gate': out['v_b_gate'], 'v_w_out': out['v_w_out'], 'v_ln_g': out['v_ln_g'], 'v_ln_b': out['v_ln_b']}


def _loss(weights, diff, rest, loss_target):
    with _jax.named_scope("forward"):
        args = {**rest, TWIN_DIFF_INPUT: diff, **{k: w.astype(_WEIGHT_DTYPES[k]) for k, w in weights.items()}}
        y = _forward(args)
    with _jax.named_scope("loss_head"):
        err = _jnp.square(y.astype(_jnp.float32) - loss_target)
        return 0.5 * _jnp.sum(_jnp.mean(err, axis=-1)) if err.ndim else 0.5 * err


def _adamw(w, g, m, v):
    m = ADAM_B1 * m + (1.0 - ADAM_B1) * g
    v = ADAM_B2 * v + (1.0 - ADAM_B2) * _jnp.square(g)
    m_hat = m / (1.0 - ADAM_B1 ** ADAM_STEP)
    v_hat = v / (1.0 - ADAM_B2 ** ADAM_STEP)
    delta = -ADAM_LR * (m_hat / (_jnp.sqrt(v_hat) + ADAM_EPS) + ADAM_WD * w)
    return delta, m, v


def reference(x, meta, ln_in_g, ln_in_b, w_in, s5_a_re, s5_a_im, s5_log_step, s5_b_re, s5_b_im, s5_c_re, s5_c_im, s5_d, s5_w_glu, s5_b_glu, ssd_conv_w, ssd_conv_b, ssd_dt_bias, ssd_a_log, ssd_d, ssd_norm_g, gdn_conv_w, gdn_dt_bias, gdn_a_log, gdn_norm_g, w_branch, b_gate, w_out, ln_g, ln_b, loss_target, m_meta, m_ln_in_g, m_ln_in_b, m_w_in, m_s5_a_re, m_s5_a_im, m_s5_log_step, m_s5_b_re, m_s5_b_im, m_s5_c_re, m_s5_c_im, m_s5_d, m_s5_w_glu, m_s5_b_glu, m_ssd_conv_w, m_ssd_conv_b, m_ssd_dt_bias, m_ssd_a_log, m_ssd_d, m_ssd_norm_g, m_gdn_conv_w, m_gdn_dt_bias, m_gdn_a_log, m_gdn_norm_g, m_w_branch, m_b_gate, m_w_out, m_ln_g, m_ln_b, v_meta, v_ln_in_g, v_ln_in_b, v_w_in, v_s5_a_re, v_s5_a_im, v_s5_log_step, v_s5_b_re, v_s5_b_im, v_s5_c_re, v_s5_c_im, v_s5_d, v_s5_w_glu, v_s5_b_glu, v_ssd_conv_w, v_ssd_conv_b, v_ssd_dt_bias, v_ssd_a_log, v_ssd_d, v_ssd_norm_g, v_gdn_conv_w, v_gdn_dt_bias, v_gdn_a_log, v_gdn_norm_g, v_w_branch, v_b_gate, v_w_out, v_ln_g, v_ln_b):
    given = dict(x=x, meta=meta, ln_in_g=ln_in_g, ln_in_b=ln_in_b, w_in=w_in, s5_a_re=s5_a_re, s5_a_im=s5_a_im, s5_log_step=s5_log_step, s5_b_re=s5_b_re, s5_b_im=s5_b_im, s5_c_re=s5_c_re, s5_c_im=s5_c_im, s5_d=s5_d, s5_w_glu=s5_w_glu, s5_b_glu=s5_b_glu, ssd_conv_w=ssd_conv_w, ssd_conv_b=ssd_conv_b, ssd_dt_bias=ssd_dt_bias, ssd_a_log=ssd_a_log, ssd_d=ssd_d, ssd_norm_g=ssd_norm_g, gdn_conv_w=gdn_conv_w, gdn_dt_bias=gdn_dt_bias, gdn_a_log=gdn_a_log, gdn_norm_g=gdn_norm_g, w_branch=w_branch, b_gate=b_gate, w_out=w_out, ln_g=ln_g, ln_b=ln_b, loss_target=loss_target, m_meta=m_meta, m_ln_in_g=m_ln_in_g, m_ln_in_b=m_ln_in_b, m_w_in=m_w_in, m_s5_a_re=m_s5_a_re, m_s5_a_im=m_s5_a_im, m_s5_log_step=m_s5_log_step, m_s5_b_re=m_s5_b_re, m_s5_b_im=m_s5_b_im, m_s5_c_re=m_s5_c_re, m_s5_c_im=m_s5_c_im, m_s5_d=m_s5_d, m_s5_w_glu=m_s5_w_glu, m_s5_b_glu=m_s5_b_glu, m_ssd_conv_w=m_ssd_conv_w, m_ssd_conv_b=m_ssd_conv_b, m_ssd_dt_bias=m_ssd_dt_bias, m_ssd_a_log=m_ssd_a_log, m_ssd_d=m_ssd_d, m_ssd_norm_g=m_ssd_norm_g, m_gdn_conv_w=m_gdn_conv_w, m_gdn_dt_bias=m_gdn_dt_bias, m_gdn_a_log=m_gdn_a_log, m_gdn_norm_g=m_gdn_norm_g, m_w_branch=m_w_branch, m_b_gate=m_b_gate, m_w_out=m_w_out, m_ln_g=m_ln_g, m_ln_b=m_ln_b, v_meta=v_meta, v_ln_in_g=v_ln_in_g, v_ln_in_b=v_ln_in_b, v_w_in=v_w_in, v_s5_a_re=v_s5_a_re, v_s5_a_im=v_s5_a_im, v_s5_log_step=v_s5_log_step, v_s5_b_re=v_s5_b_re, v_s5_b_im=v_s5_b_im, v_s5_c_re=v_s5_c_re, v_s5_c_im=v_s5_c_im, v_s5_d=v_s5_d, v_s5_w_glu=v_s5_w_glu, v_s5_b_glu=v_s5_b_glu, v_ssd_conv_w=v_ssd_conv_w, v_ssd_conv_b=v_ssd_conv_b, v_ssd_dt_bias=v_ssd_dt_bias, v_ssd_a_log=v_ssd_a_log, v_ssd_d=v_ssd_d, v_ssd_norm_g=v_ssd_norm_g, v_gdn_conv_w=v_gdn_conv_w, v_gdn_dt_bias=v_gdn_dt_bias, v_gdn_a_log=v_gdn_a_log, v_gdn_norm_g=v_gdn_norm_g, v_w_branch=v_w_branch, v_b_gate=v_b_gate, v_w_out=v_w_out, v_ln_g=v_ln_g, v_ln_b=v_ln_b)
    weights = {n: given[n] for n in TWIN_WEIGHTS}
    shared = {n: given[n] for n in SHARED_INPUTS}
    per_example = {n: given[n] for n in ['x']}
    grad_fn = _jax.value_and_grad(_loss, argnums=(0, 1))

    def one_microbatch(ex, loss_target):
        ex = dict(ex)
        diff = ex.pop(TWIN_DIFF_INPUT)
        return grad_fn(weights, diff, {**shared, **ex}, loss_target)

    if N_MICROBATCH == 1:
        loss, (grad_w, grad_x) = one_microbatch(per_example, given["loss_target"])
    else:
        def body(carry, xs):
            loss_sum, grad_sum = carry
            l_k, (gw_k, gx_k) = one_microbatch(xs[0], xs[1])
            with _jax.named_scope("update"):
                return (loss_sum + l_k, _jax.tree.map(_jnp.add, grad_sum, gw_k)), gx_k

        init = (_jnp.zeros((), _jnp.float32), _jax.tree.map(_jnp.zeros_like, weights))
        (loss, grad_w), grad_x = _jax.lax.scan(body, init, (per_example, given["loss_target"]))
    with _jax.named_scope("update"):
        delta_w, new_m, new_v = {}, {}, {}
        for n in TWIN_WEIGHTS:
            delta_w[n], new_m[n], new_v[n] = _adamw(weights[n], grad_w[n], given["m_" + n], given["v_" + n])
    return (loss, grad_x, *[grad_w[n] for n in TWIN_WEIGHTS], *[delta_w[n] for n in TWIN_WEIGHTS],
            *[new_m[n] for n in TWIN_WEIGHTS], *[new_v[n] for n in TWIN_WEIGHTS])
```

```python
import functools
import math

import jax
import jax.numpy as jnp
from jax import lax
from jax.experimental import pallas as pl
from jax.experimental.pallas import tpu as pltpu

F32 = jnp.float32
BF16 = jnp.bfloat16

N_DEV = 8
LANES = 128
SUBLANES = 8
VMEM_LIMIT = 56 * 1024 * 1024
FLAT_ROWS = 1024

CHUNK = 64
CONV_K = 4
S5_GROUP = 16
S5_STATE = 64
S5_BLOCK_GROUPS = 8
SSD_HEAD = 64
SSD_HEADS = 12
SSD_GROUPS = 2
SSD_STATE = 128
GDN_HEAD = 128
GDN_HEADS = 6
BRANCH = 768
LN_EPS = 1e-5

ADAM_LR = 0.001
ADAM_B1 = 0.9
ADAM_B2 = 0.999
ADAM_EPS = 1e-08
ADAM_WD = 0.01
ADAM_STEP = 10

WEIGHTS = ['meta', 'ln_in_g', 'ln_in_b', 'w_in', 's5_a_re', 's5_a_im', 's5_log_step', 's5_b_re', 's5_b_im',
           's5_c_re', 's5_c_im', 's5_d', 's5_w_glu', 's5_b_glu', 'ssd_conv_w', 'ssd_conv_b', 'ssd_dt_bias',
           'ssd_a_log', 'ssd_d', 'ssd_norm_g', 'gdn_conv_w', 'gdn_dt_bias', 'gdn_a_log', 'gdn_norm_g',
           'w_branch', 'b_gate', 'w_out', 'ln_g', 'ln_b']
SHARD_DIM = {'meta': 1, 'w_in': 2, 's5_w_glu': 1, 'ssd_conv_w': 2, 'gdn_conv_w': 2, 'w_branch': 3, 'b_gate': 2,
             'w_out': 1}


def _params(sem):
    return pltpu.CompilerParams(dimension_semantics=sem, vmem_limit_bytes=VMEM_LIMIT)


def _row_tile(m, cap):
    best = None
    for t in range(SUBLANES, min(m, cap) + 1, SUBLANES):
        if m % t == 0:
            best = t
    return best if best is not None else m


def _col_tile(n):
    for t in (512, 384, 256, 128):
        if n % t == 0:
            return t
    return n


def _any_tile(m, cap):
    best = 1
    for t in range(1, min(m, cap) + 1):
        if m % t == 0:
            best = t
    return best


def _mm_fwd(a, b, name):
    m, _ = a.shape
    g, k, n = b.shape
    tm, tn = _row_tile(m, 832), _col_tile(n)
    nj = n // tn

    def body(a_ref, b_ref, o_ref):
        o_ref[...] = jnp.dot(a_ref[...].astype(BF16), b_ref[0].astype(BF16), preferred_element_type=F32)

    return pl.pallas_call(
        body, name=name, grid=(g, m // tm, nj),
        in_specs=[pl.BlockSpec((tm, k), lambda gi, i, j: (i, gi)),
                  pl.BlockSpec((1, k, tn), lambda gi, i, j: (gi, 0, j))],
        out_specs=pl.BlockSpec((tm, tn), lambda gi, i, j: (i, gi * nj + j)),
        out_shape=jax.ShapeDtypeStruct((m, g * n), F32),
        compiler_params=_params(("arbitrary", "arbitrary", "arbitrary")),
    )(a, b)


def _mm_da(ct, b, name):
    m, _ = ct.shape
    g, k, n = b.shape
    tm, tk = _row_tile(m, 416), _col_tile(k)
    nk = k // tk

    def body(c_ref, b_ref, o_ref):
        o_ref[...] = lax.dot_general(c_ref[...].astype(BF16), b_ref[0].astype(BF16), (((1,), (1,)), ((), ())),
                                     preferred_element_type=F32)

    return pl.pallas_call(
        body, name=name, grid=(g, m // tm, nk),
        in_specs=[pl.BlockSpec((tm, n), lambda gi, i, j: (i, gi)),
                  pl.BlockSpec((1, tk, n), lambda gi, i, j: (gi, j, 0))],
        out_specs=pl.BlockSpec((tm, tk), lambda gi, i, j: (i, gi * nk + j)),
        out_shape=jax.ShapeDtypeStruct((m, g * k), F32),
        compiler_params=_params(("arbitrary", "arbitrary", "arbitrary")),
    )(ct, b)


def _mm_db(a, ct, g, k, n, name):
    m = a.shape[0]
    tm, tk, tn = _row_tile(m, 832), _col_tile(k), _col_tile(n)
    nk, nn = k // tk, n // tn

    def body(a_ref, c_ref, o_ref):
        @pl.when(pl.program_id(3) == 0)
        def _():
            o_ref[...] = jnp.zeros_like(o_ref)

        o_ref[0] += lax.dot_general(a_ref[...].astype(BF16), c_ref[...].astype(BF16), (((0,), (0,)), ((), ())),
                                    preferred_element_type=F32)

    return pl.pallas_call(
        body, name=name, grid=(g, nk, nn, m // tm),
        in_specs=[pl.BlockSpec((tm, tk), lambda gi, i, j, r: (r, gi * nk + i)),
                  pl.BlockSpec((tm, tn), lambda gi, i, j, r: (r, gi * nn + j))],
        out_specs=pl.BlockSpec((1, tk, tn), lambda gi, i, j, r: (gi, i, j)),
        out_shape=jax.ShapeDtypeStruct((g, k, n), F32),
        compiler_params=_params(("arbitrary", "arbitrary", "arbitrary", "arbitrary")),
    )(a, ct)


def _make_gmm(name):
    @jax.custom_vjp
    def gmm(a, b):
        return _mm_fwd(a, b, name + "_fwd")

    def fwd(a, b):
        return _mm_fwd(a, b, name + "_fwd"), (a, b)

    def bwd(res, ct):
        a, b = res
        g, k, n = b.shape
        return _mm_da(ct, b, name + "_da"), _mm_db(a, ct, g, k, n, name + "_db")

    gmm.defvjp(fwd, bwd)
    return gmm


def _mm(name, a, w):
    return _make_gmm(name)(a, w[None])


def _make_rowwise(name, fn, n_row, n_par, out_widths, tm_cap, use_ridx=False):
    n_out = len(out_widths)

    def bind(tm):
        if not use_ridx:
            return fn
        ridx = pl.program_id(0) * tm + lax.broadcasted_iota(jnp.int32, (tm, 1), 0)
        return functools.partial(fn, ridx)

    def specs(args, tm):
        rows = [pl.BlockSpec((tm, a.shape[1]), lambda i: (i, 0)) for a in args[:n_row]]
        pars = [pl.BlockSpec(a.shape, lambda i: (0, 0)) for a in args[n_row:]]
        return rows, pars

    def fwd_call(*args):
        t = args[0].shape[0]
        tm = _row_tile(t, tm_cap)
        rows, pars = specs(args, tm)

        def body(*refs):
            vals = [r[...] for r in refs[:n_row + n_par]]
            res = bind(tm)(*vals)
            for o_ref, r in zip(refs[n_row + n_par:], res):
                o_ref[...] = r

        outs = pl.pallas_call(
            body, name=name + "_fwd", grid=(t // tm,), in_specs=rows + pars,
            out_specs=[pl.BlockSpec((tm, w), lambda i: (i, 0)) for w in out_widths],
            out_shape=[jax.ShapeDtypeStruct((t, w), F32) for w in out_widths],
            compiler_params=_params(("arbitrary",)),
        )(*args)
        return tuple(outs)

    def bwd_call(args, cts):
        t = args[0].shape[0]
        tm = _row_tile(t, tm_cap)
        rows, pars = specs(args, tm)
        ct_specs = [pl.BlockSpec((tm, w), lambda i: (i, 0)) for w in out_widths]
        n_in = n_row + n_par

        def body(*refs):
            vals = [r[...] for r in refs[:n_in]]
            ct_vals = tuple(r[...] for r in refs[n_in:n_in + n_out])
            d_refs = refs[n_in + n_out:]
            f = bind(tm)
            _, vjp = jax.vjp(lambda *a: tuple(f(*a)), *vals)
            grads = vjp(ct_vals)
            for i in range(n_row):
                d_refs[i][...] = grads[i]
            if n_par:
                @pl.when(pl.program_id(0) == 0)
                def _():
                    for j in range(n_par):
                        d_refs[n_row + j][...] = jnp.zeros_like(d_refs[n_row + j])

                for j in range(n_par):
                    d_refs[n_row + j][...] += grads[n_row + j]

        outs = pl.pallas_call(
            body, name=name + "_bwd", grid=(t // tm,), in_specs=rows + pars + ct_specs,
            out_specs=rows + pars,
            out_shape=[jax.ShapeDtypeStruct(a.shape, F32) for a in args],
            compiler_params=_params(("arbitrary",)),
        )(*args, *cts)
        return tuple(outs)

    @jax.custom_vjp
    def op(*args):
        return fwd_call(*args)

    def fwd(*args):
        return fwd_call(*args), args

    def bwd(args, cts):
        return bwd_call(args, cts)

    op.defvjp(fwd, bwd)
    return op


def _make_chunk_scan(name, fn, state_shape, n_seq, n_par, out_widths):
    n_out = len(out_widths)
    zeros_idx = (0,) * len(state_shape)

    def fwd_call(*args):
        t = args[0].shape[0]
        nc = t // CHUNK
        seq_specs = [pl.BlockSpec((CHUNK, a.shape[1]), lambda c: (c, 0)) for a in args[:n_seq]]
        par_specs = [pl.BlockSpec(a.shape, lambda c: (0, 0)) for a in args[n_seq:]]

        def body(*refs):
            ins = refs[:n_seq + n_par]
            out_refs = refs[n_seq + n_par:n_seq + n_par + n_out]
            states_ref = refs[n_seq + n_par + n_out]
            st = refs[-1]

            @pl.when(pl.program_id(0) == 0)
            def _():
                st[...] = jnp.zeros_like(st)

            s0 = st[...]
            states_ref[0] = s0
            res = fn(s0, *[r[...] for r in ins])
            st[...] = res[0]
            for o_ref, r in zip(out_refs, res[1:]):
                o_ref[...] = r

        outs = pl.pallas_call(
            body, name=name + "_fwd", grid=(nc,), in_specs=seq_specs + par_specs,
            out_specs=[pl.BlockSpec((CHUNK, w), lambda c: (c, 0)) for w in out_widths]
            + [pl.BlockSpec((1,) + state_shape, lambda c: (c,) + zeros_idx)],
            out_shape=[jax.ShapeDtypeStruct((t, w), F32) for w in out_widths]
            + [jax.ShapeDtypeStruct((nc,) + state_shape, F32)],
            scratch_shapes=[pltpu.VMEM(state_shape, F32)],
            compiler_params=_params(("arbitrary",)),
        )(*args)
        return tuple(outs[:n_out]), outs[n_out]

    def bwd_call(args, states, cts):
        t = args[0].shape[0]
        nc = t // CHUNK
        rev = lambda c: (nc - 1 - c, 0)
        seq_specs = [pl.BlockSpec((CHUNK, a.shape[1]), rev) for a in args[:n_seq]]
        par_specs = [pl.BlockSpec(a.shape, lambda c: (0, 0)) for a in args[n_seq:]]
        ct_specs = [pl.BlockSpec((CHUNK, w), rev) for w in out_widths]
        st_spec = pl.BlockSpec((1,) + state_shape, lambda c: (nc - 1 - c,) + zeros_idx)
        n_in = n_seq + n_par

        def body(*refs):
            vals = [r[...] for r in refs[:n_in]]
            s0 = refs[n_in][0]
            ct_vals = tuple(r[...] for r in refs[n_in + 1:n_in + 1 + n_out])
            d_refs = refs[n_in + 1 + n_out:-1]
            dst = refs[-1]

            @pl.when(pl.program_id(0) == 0)
            def _():
                dst[...] = jnp.zeros_like(dst)
                for j in range(n_par):
                    d_refs[n_seq + j][...] = jnp.zeros_like(d_refs[n_seq + j])

            _, vjp = jax.vjp(lambda *a: tuple(fn(*a)), s0, *vals)
            grads = vjp((dst[...],) + ct_vals)
            dst[...] = grads[0]
            for i in range(n_seq):
                d_refs[i][...] = grads[1 + i]
            for j in range(n_par):
                d_refs[n_seq + j][...] += grads[1 + n_seq + j]

        outs = pl.pallas_call(
            body, name=name + "_bwd", grid=(nc,), in_specs=seq_specs + par_specs + [st_spec] + ct_specs,
            out_specs=seq_specs + par_specs,
            out_shape=[jax.ShapeDtypeStruct(a.shape, F32) for a in args],
            scratch_shapes=[pltpu.VMEM(state_shape, F32)],
            compiler_params=_params(("arbitrary",)),
        )(*args, states, *cts)
        return tuple(outs)

    @jax.custom_vjp
    def op(*args):
        return fwd_call(*args)[0]

    def fwd(*args):
        outs, states = fwd_call(*args)
        return outs, (args, states)

    def bwd(res, cts):
        args, states = res
        return bwd_call(args, states, cts)

    op.defvjp(fwd, bwd)
    return op


def _s5_scan_fwd(bre, bim, are, aim, name):
    t, r, _ = bre.shape
    tb = _any_tile(t, 208)
    blk = pl.BlockSpec((tb, r, LANES), lambda i: (i, 0, 0))
    par = pl.BlockSpec((r, LANES), lambda i: (0, 0))

    def body(bre_ref, bim_ref, are_ref, aim_ref, sre_ref, sim_ref, st):
        @pl.when(pl.program_id(0) == 0)
        def _():
            st[...] = jnp.zeros_like(st)

        ar, ai = are_ref[...], aim_ref[...]

        def step(k, carry):
            sr, si = carry
            nr = ar * sr - ai * si + bre_ref[k]
            ni = ar * si + ai * sr + bim_ref[k]
            sre_ref[k] = nr
            sim_ref[k] = ni
            return nr, ni

        sr, si = lax.fori_loop(0, tb, step, (st[0], st[1]), unroll=4)
        st[0] = sr
        st[1] = si

    return pl.pallas_call(
        body, name=name, grid=(t // tb,), in_specs=[blk, blk, par, par], out_specs=[blk, blk],
        out_shape=[jax.ShapeDtypeStruct(bre.shape, F32)] * 2,
        scratch_shapes=[pltpu.VMEM((2, r, LANES), F32)],
        compiler_params=_params(("arbitrary",)),
    )(bre, bim, are, aim)


def _s5_scan_bwd(dsr, dsi, sre, sim, are, aim, name):
    t, r, _ = sre.shape
    tb = _any_tile(t, 208)
    nb = t // tb
    blk = pl.BlockSpec((tb, r, LANES), lambda i: (nb - 1 - i, 0, 0))
    par = pl.BlockSpec((r, LANES), lambda i: (0, 0))

    def body(dsr_ref, dsi_ref, sre_ref, sim_ref, are_ref, aim_ref, gre_ref, gim_ref, dar_ref, dai_ref, st):
        @pl.when(pl.program_id(0) == 0)
        def _():
            st[...] = jnp.zeros_like(st)
            dar_ref[...] = jnp.zeros_like(dar_ref)
            dai_ref[...] = jnp.zeros_like(dai_ref)

        ar, ai = are_ref[...], aim_ref[...]

        def step(k, carry):
            gr, gi, dar, dai = carry
            q = tb - 1 - k
            s_r, s_i = sre_ref[q], sim_ref[q]
            dar = dar + gr * s_r + gi * s_i
            dai = dai + gi * s_r - gr * s_i
            ngr = dsr_ref[q] + ar * gr + ai * gi
            ngi = dsi_ref[q] + ar * gi - ai * gr
            gre_ref[q] = ngr
            gim_ref[q] = ngi
            return ngr, ngi, dar, dai

        gr, gi, dar, dai = lax.fori_loop(0, tb, step, (st[0], st[1], dar_ref[...], dai_ref[...]), unroll=4)
        st[0] = gr
        st[1] = gi
        dar_ref[...] = dar
        dai_ref[...] = dai

    return pl.pallas_call(
        body, name=name, grid=(nb,), in_specs=[blk, blk, blk, blk, par, par], out_specs=[blk, blk, par, par],
        out_shape=[jax.ShapeDtypeStruct(sre.shape, F32)] * 2 + [jax.ShapeDtypeStruct(are.shape, F32)] * 2,
        scratch_shapes=[pltpu.VMEM((2, r, LANES), F32)],
        compiler_params=_params(("arbitrary",)),
    )(dsr, dsi, sre, sim, are, aim)


def _make_s5_scan(name):
    @jax.custom_vjp
    def scan(bre, bim, are, aim):
        return tuple(_s5_scan_fwd(bre, bim, are, aim, name + "_fwd"))

    def fwd(bre, bim, are, aim):
        sre, sim = _s5_scan_fwd(bre, bim, are, aim, name + "_fwd")
        return (sre, sim), (sre, sim, are, aim)

    def bwd(res, cts):
        sre, sim, are, aim = res
        return tuple(_s5_scan_bwd(cts[0], cts[1], sre, sim, are, aim, name + "_bwd"))

    scan.defvjp(fwd, bwd)
    return scan


def _dot(a, b):
    return jnp.dot(a.astype(BF16), b.astype(BF16), preferred_element_type=F32)


def _dot_nt(a, b):
    return lax.dot_general(a.astype(BF16), b.astype(BF16), (((1,), (1,)), ((), ())), preferred_element_type=F32)


def _dot_tn(a, b):
    return lax.dot_general(a.astype(BF16), b.astype(BF16), (((0,), (0,)), ((), ())), preferred_element_type=F32)


def _dot_f32(a, b):
    return jnp.dot(a, b, precision=lax.Precision.HIGHEST, preferred_element_type=F32)


def _dot_tn_f32(a, b):
    return lax.dot_general(a, b, (((0,), (0,)), ((), ())), precision=lax.Precision.HIGHEST,
                           preferred_element_type=F32)


def _iota(shape, dim):
    return lax.broadcasted_iota(jnp.int32, shape, dim)


def _tri(strict=False):
    r, c = _iota((CHUNK, CHUNK), 0), _iota((CHUNK, CHUNK), 1)
    return (r > c) if strict else (r >= c)


def _silu(x):
    return x * jax.nn.sigmoid(x)


def _layer_norm(z, g, b):
    mu = jnp.mean(z, axis=-1, keepdims=True)
    var = jnp.mean(jnp.square(z - mu), axis=-1, keepdims=True)
    return (z - mu) * lax.rsqrt(var + LN_EPS) * g + b


def _ssd_chunk(state, xbc, sm, a_c, d_exp):
    width = SSD_HEADS * SSD_HEAD
    x = xbc[:, :width]
    lane = _iota((CHUNK, LANES), 1)
    dtc = jnp.where(lane < SSD_HEADS, sm, 0.0)
    low = _tri().astype(F32)
    eye = (_iota((CHUNK, CHUNK), 0) == _iota((CHUNK, CHUNK), 1)).astype(F32)
    head_col, head_row = _iota((LANES, width), 1), _iota((LANES, width), 0) * SSD_HEAD
    expand = ((head_col >= head_row) & (head_col < head_row + SSD_HEAD)).astype(F32)
    acum_c = _dot_f32(low, dtc * a_c)
    acum_ct = _dot_tn_f32(acum_c, eye)
    dt_exp = _dot_f32(dtc, expand)
    acum = _dot_f32(acum_c, expand)
    xd = x * dt_exp
    last = acum[CHUNK - 1:CHUNK, :]
    to_end = jnp.exp(last - acum)
    eac = jnp.exp(acum)
    causal = _tri()
    first_half = _iota((CHUNK, LANES), 1) < SSD_HEAD
    top_rows = _iota((LANES, LANES), 0) < SSD_HEAD
    ys, new_states = [], []
    for p in range(SSD_HEADS // 2):
        grp = (2 * p) // (SSD_HEADS // SSD_GROUPS)
        bg = xbc[:, width + grp * SSD_STATE: width + (grp + 1) * SSD_STATE]
        cg = xbc[:, width + (SSD_GROUPS + grp) * SSD_STATE: width + (SSD_GROUPS + grp + 1) * SSD_STATE]
        cols = slice(p * LANES, (p + 1) * LANES)
        scores = _dot_nt(cg, bg)
        xd_p = xd[:, cols]
        y = jnp.zeros((CHUNK, LANES), F32)
        for hh in range(2):
            h = 2 * p + hh
            seg = acum_c[:, h:h + 1] - acum_ct[h:h + 1, :]
            dec = jnp.where(causal, jnp.exp(jnp.minimum(seg, 0.0)), 0.0)
            xm = jnp.where(first_half if hh == 0 else jnp.logical_not(first_half), xd_p, 0.0)
            y = y + _dot(scores * dec, xm)
        s_prev = state[p * LANES:(p + 1) * LANES, :]
        y = y + _dot_nt(cg, s_prev) * eac[:, cols]
        y = y + x[:, cols] * d_exp[:, cols]
        cd = jnp.where(top_rows, jnp.exp(acum_c[CHUNK - 1:CHUNK, 2 * p:2 * p + 1]),
                       jnp.exp(acum_c[CHUNK - 1:CHUNK, 2 * p + 1:2 * p + 2]))
        new_states.append(s_prev * cd + _dot_tn(xd_p * to_end[:, cols], bg))
        ys.append(y)
    return jnp.concatenate(new_states, axis=0), jnp.concatenate(ys, axis=1)


def _gdn_chunk(state, qkv, sm):
    width = GDN_HEADS * GDN_HEAD
    g0, b0 = SSD_HEADS, SSD_HEADS + GDN_HEADS
    lane = _iota((CHUNK, LANES), 1)
    gc = jnp.where((lane >= g0) & (lane < b0), sm, 0.0)
    low = _tri().astype(F32)
    eye = (_iota((CHUNK, CHUNK), 0) == _iota((CHUNK, CHUNK), 1)).astype(F32)
    gcum = _dot_f32(low, gc)
    gcum_t = _dot_tn_f32(gcum, eye)
    causal, strict = _tri(), _tri(strict=True)
    outs, new_states = [], []
    for h in range(GDN_HEADS):
        q = qkv[:, h * GDN_HEAD:(h + 1) * GDN_HEAD]
        k = qkv[:, width + h * GDN_HEAD: width + (h + 1) * GDN_HEAD]
        v = qkv[:, 2 * width + h * GDN_HEAD: 2 * width + (h + 1) * GDN_HEAD]
        beta = sm[:, b0 + h:b0 + h + 1]
        gcol = gcum[:, g0 + h:g0 + h + 1]
        grow = gcum_t[g0 + h:g0 + h + 1, :]
        glast = gcum[CHUNK - 1:CHUNK, g0 + h:g0 + h + 1]
        gamma = jnp.where(causal, jnp.exp(jnp.minimum(gcol - grow, 0.0)), 0.0)
        a_mat = jnp.where(strict, _dot_nt(k, k) * gamma * beta, 0.0)
        egc = jnp.exp(gcol)
        sol = jnp.concatenate([v * beta, k * (beta * egc)], axis=1)
        nmat = -a_mat
        for i in range(6):
            sol = sol + _dot(nmat, sol)
            if i < 5:
                nmat = _dot(nmat, nmat)
        u, w = sol[:, :GDN_HEAD], sol[:, GDN_HEAD:]
        attn = _dot_nt(q, k) * gamma
        s_prev = state[h * GDN_HEAD:(h + 1) * GDN_HEAD, :]
        v_new = u - _dot(w, s_prev)
        outs.append(_dot(q * egc, s_prev) + _dot(attn, v_new))
        new_states.append(s_prev * jnp.exp(glast) + _dot_tn(k * jnp.exp(glast - gcol), v_new))
    return jnp.concatenate(new_states, axis=0), jnp.concatenate(outs, axis=1)


def _row_fns(d_model, pad_rows, loss_rows, alpha):
    def keep(ridx, v):
        return jnp.where(ridx >= pad_rows, v, 0.0)

    def ln_in(ridx, h, g, b):
        return (keep(ridx, _layer_norm(h, g, b)),)

    def ln_layer(ridx, h, o, g, b):
        return (keep(ridx, _layer_norm(alpha * h + o, g, b)),)

    def s5_act(y_re, y_im, u, d):
        return (jax.nn.gelu(y_re + y_im + d * u),)

    def s5_glu(v0, lin, z, bias):
        return (v0 * jax.nn.sigmoid(lin + bias) * _silu(z),)

    def small_act(ridx, raw, bias, scale):
        lane = _iota(raw.shape, 1)
        sp = jax.nn.softplus(raw + bias)
        g0, b0 = SSD_HEADS, SSD_HEADS + GDN_HEADS
        out = jnp.where(lane < g0, sp, jnp.where(lane < b0, scale * sp,
                                                 jnp.where(lane < b0 + GDN_HEADS, jax.nn.sigmoid(raw), 0.0)))
        return (keep(ridx, out),)

    def conv(xs, w):
        acc = xs[0] * w[0:1, :]
        for j in range(1, CONV_K):
            acc = acc + xs[j] * w[j:j + 1, :]
        return acc

    def ssd_conv(ridx, x0, x1, x2, x3, w, b):
        return (keep(ridx, _silu(conv((x0, x1, x2, x3), w) + b)),)

    def ssd_norm(y, z, g):
        v = y * _silu(z)
        return (v * lax.rsqrt(jnp.mean(v * v, axis=-1, keepdims=True) + LN_EPS) * g,)

    def gdn_conv(x0, x1, x2, x3, w):
        a = _silu(conv((x0, x1, x2, x3), w))
        width = GDN_HEADS * GDN_HEAD
        parts = []
        for h in range(2 * GDN_HEADS):
            z = a[:, h * GDN_HEAD:(h + 1) * GDN_HEAD]
            z = z * lax.rsqrt(jnp.sum(z * z, axis=-1, keepdims=True) + 1e-6)
            parts.append(z * GDN_HEAD ** -0.5 if h < GDN_HEADS else z)
        parts.append(a[:, 2 * width:])
        return (jnp.concatenate(parts, axis=1),)

    def gdn_norm(o, z, g):
        parts = []
        for h in range(GDN_HEADS):
            cols = slice(h * GDN_HEAD, (h + 1) * GDN_HEAD)
            oh = o[:, cols]
            oh = oh * lax.rsqrt(jnp.mean(oh * oh, axis=-1, keepdims=True) + LN_EPS) * g
            parts.append(oh * _silu(z[:, cols]))
        return (jnp.concatenate(parts, axis=1),)

    def merge(oa, ob, oc, gate, bias):
        acc = None
        for k, o in enumerate((oa, ob, oc)):
            cols = slice(k * d_model, (k + 1) * d_model)
            term = jax.nn.sigmoid(gate[:, cols] + bias[:, cols]) * o
            acc = term if acc is None else acc + term
        return (acc,)

    def loss_rows_fn(ridx, h, tgt):
        row = 0.5 * jnp.mean(jnp.square(h - tgt), axis=-1, keepdims=True)
        row = jnp.where(ridx >= loss_rows, row, 0.0)
        lane = _iota((h.shape[0], LANES), 1)
        return (jnp.where(lane == 0, row, 0.0),)

    return dict(ln_in=ln_in, ln_layer=ln_layer, s5_act=s5_act, s5_glu=s5_glu, small_act=small_act,
                ssd_conv=ssd_conv, ssd_norm=ssd_norm, gdn_conv=gdn_conv, gdn_norm=gdn_norm, merge=merge,
                loss=loss_rows_fn)


def _exchange(src, gather, name):
    rows = src.shape[-2]
    flips = [(fx, fy, fc) for fx in (0, 1) for fy in (0, 1) for fc in (0, 1)][1:]

    def body(src_ref, out_ref, send_sems, recv_sems, local_sem):
        x, y, c = lax.axis_index("x"), lax.axis_index("y"), lax.axis_index("c")
        me = 4 * x + 2 * y + c

        def pick(j):
            return src_ref if gather else src_ref.at[j]

        mine = pltpu.make_async_copy(pick(me), out_ref.at[me], local_sem)
        mine.start()
        copies = []
        for k, (fx, fy, fc) in enumerate(flips):
            px = 1 - x if fx else x
            py = 1 - y if fy else y
            pc = 1 - c if fc else c
            peer = 4 * px + 2 * py + pc
            cp = pltpu.make_async_remote_copy(
                src_ref=pick(peer), dst_ref=out_ref.at[me], send_sem=send_sems.at[k], recv_sem=recv_sems.at[k],
                device_id=(px, py, pc), device_id_type=pl.DeviceIdType.MESH)
            cp.start()
            copies.append(cp)
        for cp in copies:
            cp.wait()
        mine.wait()

    return pl.pallas_call(
        body, name=name,
        in_specs=[pl.BlockSpec(memory_space=pl.ANY)], out_specs=pl.BlockSpec(memory_space=pl.ANY),
        out_shape=jax.ShapeDtypeStruct((N_DEV, rows, LANES), F32),
        scratch_shapes=[pltpu.SemaphoreType.DMA((N_DEV - 1,)), pltpu.SemaphoreType.DMA((N_DEV - 1,)),
                        pltpu.SemaphoreType.DMA],
    )(src)


def _adamw_flat(parts, w, m, v):
    rows = w.shape[0]
    tr = FLAT_ROWS
    bc1 = 1.0 - ADAM_B1 ** ADAM_STEP
    bc2 = 1.0 - ADAM_B2 ** ADAM_STEP

    def body(p_ref, w_ref, m_ref, v_ref, g_ref, d_ref, nm_ref, nv_ref):
        g = p_ref[0]
        for k in range(1, N_DEV):
            g = g + p_ref[k]
        nm = ADAM_B1 * m_ref[...] + (1.0 - ADAM_B1) * g
        nv = ADAM_B2 * v_ref[...] + (1.0 - ADAM_B2) * jnp.square(g)
        m_hat = nm / bc1
        v_hat = nv / bc2
        g_ref[...] = g
        d_ref[...] = -ADAM_LR * (m_hat / (jnp.sqrt(v_hat) + ADAM_EPS) + ADAM_WD * w_ref[...])
        nm_ref[...] = nm
        nv_ref[...] = nv

    blk = pl.BlockSpec((tr, LANES), lambda i: (i, 0))
    return pl.pallas_call(
        body, name="adamw", grid=(rows // tr,),
        in_specs=[pl.BlockSpec((N_DEV, tr, LANES), lambda i: (0, i, 0)), blk, blk, blk],
        out_specs=[blk] * 4, out_shape=[jax.ShapeDtypeStruct((rows, LANES), F32)] * 4,
        compiler_params=_params(("arbitrary",)),
    )(parts, w, m, v)


def _pad_flat(vec, rows):
    return jnp.pad(vec, (0, rows * LANES - vec.shape[0])).reshape(rows, LANES)


def _rows_for(n):
    return -(-n // (FLAT_ROWS * LANES)) * FLAT_ROWS


def _split_shards(full, dim):
    shp = full.shape
    parts = full.reshape(shp[:dim] + (N_DEV, shp[dim] // N_DEV) + shp[dim + 1:])
    return jnp.moveaxis(parts, dim, 0).reshape(N_DEV, -1)


def _join_shards(rows, local_shape, dim):
    parts = jnp.moveaxis(rows.reshape((N_DEV,) + tuple(local_shape)), 0, dim)
    shp = tuple(local_shape)
    return parts.reshape(shp[:dim] + (N_DEV * shp[dim],) + shp[dim + 1:])


def _s5_tables(a_re, a_im, log_step, b_re, b_im, c_re, c_im):
    lam_re = jnp.minimum(a_re, -1e-4)
    lam_im = a_im
    step = jnp.exp(log_step)[:, None]
    mag = jnp.exp(lam_re * step)
    abar_re, abar_im = mag * jnp.cos(lam_im * step), mag * jnp.sin(lam_im * step)
    den = lam_re * lam_re + lam_im * lam_im
    nr, ni = abar_re - 1.0, abar_im
    coef_re = (nr * lam_re + ni * lam_im) / den
    coef_im = (ni * lam_re - nr * lam_im) / den
    bbar_re = coef_re[..., None] * b_re - coef_im[..., None] * b_im
    bbar_im = coef_re[..., None] * b_im + coef_im[..., None] * b_re
    groups = a_re.shape[0]
    nblk = groups // S5_BLOCK_GROUPS
    eye = jnp.eye(S5_BLOCK_GROUPS, dtype=F32)

    def in_blocks(bb):
        t = jnp.swapaxes(bb, 1, 2).reshape(nblk, S5_BLOCK_GROUPS, S5_GROUP, S5_STATE)
        blk = jnp.einsum('ab,jacp->jacbp', eye, t)
        return blk.reshape(nblk, S5_BLOCK_GROUPS * S5_GROUP, S5_BLOCK_GROUPS * S5_STATE)

    def out_blocks(cc):
        t = jnp.swapaxes(cc, 1, 2).reshape(nblk, S5_BLOCK_GROUPS, S5_STATE, S5_GROUP)
        blk = jnp.einsum('ab,japc->japbc', eye, t)
        return blk.reshape(nblk, S5_BLOCK_GROUPS * S5_STATE, S5_BLOCK_GROUPS * S5_GROUP)

    rows = groups * S5_STATE // LANES
    return dict(b_re=in_blocks(bbar_re), b_im=in_blocks(bbar_im), c_re=out_blocks(c_re), c_im=out_blocks(-c_im),
                a_re=abar_re.reshape(rows, LANES), a_im=abar_im.reshape(rows, LANES))


def _shifted(x):
    t = x.shape[0]
    return [jnp.pad(x, ((CONV_K - 1 - j, 0), (0, 0)))[:t] for j in range(CONV_K)]


def _local_loss(w, x, target):
    n_meta, d_model = w['meta'].shape
    depth = w['w_in'].shape[0]
    seq = x.shape[0]
    pad_rows = CHUNK - n_meta
    first = pad_rows + n_meta
    t_all = first + seq
    alpha = (2 * depth) ** 0.25
    fns = _row_fns(d_model, pad_rows, first, alpha)
    row = lambda nm, key, n_row, n_par, widths, cap, ridx=False: _make_rowwise(nm, fns[key], n_row, n_par, widths, cap, ridx)

    h = jnp.concatenate([jnp.zeros((pad_rows, d_model), F32), w['meta'], x], axis=0)
    (h,) = row("ln_in", 'ln_in', 1, 2, [d_model], 416, True)(h, w['ln_in_g'][None], w['ln_in_b'][None])

    widths = [BRANCH, BRANCH, SSD_HEADS * SSD_HEAD + 2 * SSD_GROUPS * SSD_STATE, SSD_HEADS, BRANCH,
              3 * BRANCH, GDN_HEADS, GDN_HEADS, BRANCH, 3 * d_model]
    offs = [0]
    for wd in widths:
        offs.append(offs[-1] + wd)
    n_small = SSD_HEADS + 2 * GDN_HEADS

    for l in range(depth):
        w_in = w['w_in'][l]
        piece = lambda i: w_in[:, offs[i]:offs[i + 1]]
        w_small = jnp.pad(jnp.concatenate([piece(3), piece(6), piece(7)], axis=1), ((0, 0), (0, LANES - n_small)))
        s5_u = _mm(f"in_s5u", h, piece(0))
        s5_z = _mm(f"in_s5z", h, piece(1))
        ssd_xbc = _mm(f"in_ssdx", h, piece(2))
        ssd_z = _mm(f"in_ssdz", h, piece(4))
        gdn_qkv = _mm(f"in_gdnq", h, piece(5))
        gdn_z = _mm(f"in_gdnz", h, piece(8))
        gate = _mm(f"in_gate", h, piece(9))
        small = _mm(f"in_small", h, w_small)

        zeros_tail = jnp.zeros((LANES - n_small,), F32)
        bias = jnp.concatenate([w['ssd_dt_bias'][l], w['gdn_dt_bias'][l], jnp.zeros((GDN_HEADS,), F32), zeros_tail])[None]
        scale = jnp.concatenate([jnp.ones((SSD_HEADS,), F32), -jnp.exp(w['gdn_a_log'][l]),
                                 jnp.zeros((GDN_HEADS,), F32), zeros_tail])[None]
        (sm,) = row("small_act", 'small_act', 1, 2, [LANES], 832, True)(small, bias, scale)

        tb = _s5_tables(w['s5_a_re'][l], w['s5_a_im'][l], w['s5_log_step'][l], w['s5_b_re'][l], w['s5_b_im'][l],
                        w['s5_c_re'][l], w['s5_c_im'][l])
        srows = tb['a_re'].shape[0]
        bu_re = _make_gmm("s5_bre")(s5_u, tb['b_re']).reshape(t_all, srows, LANES)
        bu_im = _make_gmm("s5_bim")(s5_u, tb['b_im']).reshape(t_all, srows, LANES)
        s_re, s_im = _make_s5_scan("s5_scan")(bu_re, bu_im, tb['a_re'], tb['a_im'])
        y_re = _make_gmm("s5_cre")(s_re.reshape(t_all, srows * LANES), tb['c_re'])
        y_im = _make_gmm("s5_cim")(s_im.reshape(t_all, srows * LANES), tb['c_im'])
        (v0,) = row("s5_act", 's5_act', 3, 1, [BRANCH], 416)(y_re, y_im, s5_u, w['s5_d'][l][None])
        lin = _mm("s5_glu_mm", v0, w['s5_w_glu'][l])
        (y_a,) = row("s5_glu", 's5_glu', 3, 1, [BRANCH], 416)(v0, lin, s5_z, w['s5_b_glu'][l][None])

        (xbc,) = row("ssd_conv", 'ssd_conv', 4, 2, [ssd_xbc.shape[1]], 208, True)(
            *_shifted(ssd_xbc), w['ssd_conv_w'][l], w['ssd_conv_b'][l][None])
        a_c = jnp.pad(-jnp.exp(w['ssd_a_log'][l]), (0, LANES - SSD_HEADS))[None]
        d_exp = jnp.repeat(w['ssd_d'][l], SSD_HEAD)[None]
        (y_ssd,) = _make_chunk_scan("ssd_scan", _ssd_chunk, (SSD_HEADS // 2 * LANES, SSD_STATE), 2, 2, [BRANCH])(
            xbc, sm, a_c, d_exp)
        (y_b,) = row("ssd_norm", 'ssd_norm', 2, 1, [BRANCH], 416)(y_ssd, ssd_z, w['ssd_norm_g'][l][None])

        (qkv,) = row("gdn_conv", 'gdn_conv', 4, 1, [3 * BRANCH], 104)(*_shifted(gdn_qkv), w['gdn_conv_w'][l])
        (o_gdn,) = _make_chunk_scan("gdn_scan", _gdn_chunk, (GDN_HEADS * GDN_HEAD, GDN_HEAD), 2, 0, [BRANCH])(qkv, sm)
        (y_c,) = row("gdn_norm", 'gdn_norm', 2, 1, [BRANCH], 416)(o_gdn, gdn_z, w['gdn_norm_g'][l][None])

        outs = [_mm(f"branch{k}", yk, w['w_branch'][l, k]) for k, yk in enumerate((y_a, y_b, y_c))]
        (merged,) = row("merge", 'merge', 4, 1, [d_model], 208)(*outs, gate, w['b_gate'][l].reshape(1, 3 * d_model))
        o = _mm("w_out", merged, w['w_out'][l])
        (h,) = row("ln_layer", 'ln_layer', 2, 2, [d_model], 416, True)(h, o, w['ln_g'][l][None], w['ln_b'][l][None])

    tgt = jnp.concatenate([jnp.zeros((first, d_model), F32), target], axis=0)
    (rows_loss,) = row("loss", 'loss', 2, 0, [LANES], 416, True)(h, tgt)
    return jnp.sum(rows_loss)


def _step(x, target, w_loc, m_loc, v_loc):
    sharded = [n for n in WEIGHTS if n in SHARD_DIM]
    repl = [n for n in WEIGHTS if n not in SHARD_DIM]
    n_sh = sum(int(w_loc[n].size) for n in sharded)
    n_rp = sum(int(w_loc[n].size) for n in repl) + 1
    rows_sh, rows_rp = _rows_for(n_sh), _rows_for(n_rp)

    sh_flat = _pad_flat(jnp.concatenate([w_loc[n].reshape(-1) for n in sharded]), rows_sh)
    gathered = _exchange(sh_flat, True, "gather_weights").reshape(N_DEV, -1)
    w_full, off = {}, 0
    for n in sharded:
        sz = int(w_loc[n].size)
        w_full[n] = _join_shards(gathered[:, off:off + sz], w_loc[n].shape, SHARD_DIM[n])
        off += sz
    for n in repl:
        w_full[n] = w_loc[n]

    loss, (g_w, g_x) = jax.value_and_grad(_local_loss, argnums=(0, 1))(w_full, x[0], target[0])

    sh_rows = jnp.concatenate([_split_shards(g_w[n], SHARD_DIM[n]) for n in sharded], axis=1)
    sh_rows = jnp.pad(sh_rows, ((0, 0), (0, rows_sh * LANES - n_sh)))
    rp_vec = jnp.concatenate([g_w[n].reshape(-1) for n in repl] + [loss.reshape(1)])
    rp_vec = jnp.pad(rp_vec, (0, rows_rp * LANES - n_rp))
    send = jnp.concatenate([sh_rows, jnp.broadcast_to(rp_vec[None], (N_DEV, rows_rp * LANES))], axis=1)
    rows = rows_sh + rows_rp
    parts = _exchange(send.reshape(N_DEV, rows, LANES), False, "exchange_grads")

    def flat(src, extra):
        vec = jnp.concatenate([src[n].reshape(-1) for n in sharded])
        vec = jnp.pad(vec, (0, rows_sh * LANES - n_sh))
        rep = jnp.concatenate([src[n].reshape(-1) for n in repl] + [extra])
        rep = jnp.pad(rep, (0, rows_rp * LANES - n_rp))
        return jnp.concatenate([vec, rep]).reshape(rows, LANES)

    one = jnp.ones((1,), F32)
    g_f, d_f, m_f, v_f = _adamw_flat(parts, flat(w_loc, one), flat(m_loc, one), flat(v_loc, one))

    def unflat(buf):
        vec = buf.reshape(-1)
        out, o = {}, 0
        for n in sharded:
            sz = int(w_loc[n].size)
            out[n] = vec[o:o + sz].reshape(w_loc[n].shape)
            o += sz
        o = rows_sh * LANES
        for n in repl:
            sz = int(w_loc[n].size)
            out[n] = vec[o:o + sz].reshape(w_loc[n].shape)
            o += sz
        return out, vec[o]

    grads, loss_total = unflat(g_f)
    deltas, _ = unflat(d_f)
    new_m, _ = unflat(m_f)
    new_v, _ = unflat(v_f)
    return (loss_total, g_x[None], *[grads[n] for n in WEIGHTS], *[deltas[n] for n in WEIGHTS],
            *[new_m[n] for n in WEIGHTS], *[new_v[n] for n in WEIGHTS])


def kernel(x, meta, ln_in_g, ln_in_b, w_in, s5_a_re, s5_a_im, s5_log_step, s5_b_re, s5_b_im, s5_c_re, s5_c_im, s5_d, s5_w_glu, s5_b_glu, ssd_conv_w, ssd_conv_b, ssd_dt_bias, ssd_a_log, ssd_d, ssd_norm_g, gdn_conv_w, gdn_dt_bias, gdn_a_log, gdn_norm_g, w_branch, b_gate, w_out, ln_g, ln_b, loss_target, m_meta, m_ln_in_g, m_ln_in_b, m_w_in, m_s5_a_re, m_s5_a_im, m_s5_log_step, m_s5_b_re, m_s5_b_im, m_s5_c_re, m_s5_c_im, m_s5_d, m_s5_w_glu, m_s5_b_glu, m_ssd_conv_w, m_ssd_conv_b, m_ssd_dt_bias, m_ssd_a_log, m_ssd_d, m_ssd_norm_g, m_gdn_conv_w, m_gdn_dt_bias, m_gdn_a_log, m_gdn_norm_g, m_w_branch, m_b_gate, m_w_out, m_ln_g, m_ln_b, v_meta, v_ln_in_g, v_ln_in_b, v_w_in, v_s5_a_re, v_s5_a_im, v_s5_log_step, v_s5_b_re, v_s5_b_im, v_s5_c_re, v_s5_c_im, v_s5_d, v_s5_w_glu, v_s5_b_glu, v_ssd_conv_w, v_ssd_conv_b, v_ssd_dt_bias, v_ssd_a_log, v_ssd_d, v_ssd_norm_g, v_gdn_conv_w, v_gdn_dt_bias, v_gdn_a_log, v_gdn_norm_g, v_w_branch, v_b_gate, v_w_out, v_ln_g, v_ln_b):
    w_loc = dict(zip(WEIGHTS, (meta, ln_in_g, ln_in_b, w_in, s5_a_re, s5_a_im, s5_log_step, s5_b_re, s5_b_im, s5_c_re, s5_c_im, s5_d, s5_w_glu, s5_b_glu, ssd_conv_w, ssd_conv_b, ssd_dt_bias, ssd_a_log, ssd_d, ssd_norm_g, gdn_conv_w, gdn_dt_bias, gdn_a_log, gdn_norm_g, w_branch, b_gate, w_out, ln_g, ln_b)))
    m_loc = dict(zip(WEIGHTS, (m_meta, m_ln_in_g, m_ln_in_b, m_w_in, m_s5_a_re, m_s5_a_im, m_s5_log_step, m_s5_b_re, m_s5_b_im, m_s5_c_re, m_s5_c_im, m_s5_d, m_s5_w_glu, m_s5_b_glu, m_ssd_conv_w, m_ssd_conv_b, m_ssd_dt_bias, m_ssd_a_log, m_ssd_d, m_ssd_norm_g, m_gdn_conv_w, m_gdn_dt_bias, m_gdn_a_log, m_gdn_norm_g, m_w_branch, m_b_gate, m_w_out, m_ln_g, m_ln_b)))
    v_loc = dict(zip(WEIGHTS, (v_meta, v_ln_in_g, v_ln_in_b, v_w_in, v_s5_a_re, v_s5_a_im, v_s5_log_step, v_s5_b_re, v_s5_b_im, v_s5_c_re, v_s5_c_im, v_s5_d, v_s5_w_glu, v_s5_b_glu, v_ssd_conv_w, v_ssd_conv_b, v_ssd_dt_bias, v_ssd_a_log, v_ssd_d, v_ssd_norm_g, v_gdn_conv_w, v_gdn_dt_bias, v_gdn_a_log, v_gdn_norm_g, v_w_branch, v_b_gate, v_w_out, v_ln_g, v_ln_b)))
    return _step(x, loss_target, w_loc, m_loc, v_loc)
```

```python
import functools
import math

import jax
import jax.numpy as jnp
from jax import lax
from jax.experimental import pallas as pl
from jax.experimental.pallas import tpu as pltpu

F32 = jnp.float32
BF16 = jnp.bfloat16

N_DEV = 8
LANES = 128
SUBLANES = 8
VMEM_LIMIT = 56 * 1024 * 1024
FLAT_ROWS = 1024

CHUNK = 64
CONV_K = 4
S5_GROUP = 16
S5_STATE = 64
S5_BLOCK_GROUPS = 8
SSD_HEAD = 64
SSD_HEADS = 12
SSD_GROUPS = 2
SSD_STATE = 128
GDN_HEAD = 128
GDN_HEADS = 6
BRANCH = 768
LN_EPS = 1e-5

ADAM_LR = 0.001
ADAM_B1 = 0.9
ADAM_B2 = 0.999
ADAM_EPS = 1e-08
ADAM_WD = 0.01
ADAM_STEP = 10

WEIGHTS = ['meta', 'ln_in_g', 'ln_in_b', 'w_in', 's5_a_re', 's5_a_im', 's5_log_step', 's5_b_re', 's5_b_im',
           's5_c_re', 's5_c_im', 's5_d', 's5_w_glu', 's5_b_glu', 'ssd_conv_w', 'ssd_conv_b', 'ssd_dt_bias',
           'ssd_a_log', 'ssd_d', 'ssd_norm_g', 'gdn_conv_w', 'gdn_dt_bias', 'gdn_a_log', 'gdn_norm_g',
           'w_branch', 'b_gate', 'w_out', 'ln_g', 'ln_b']
SHARD_DIM = {'meta': 1, 'w_in': 2, 's5_w_glu': 1, 'ssd_conv_w': 2, 'gdn_conv_w': 2, 'w_branch': 3, 'b_gate': 2,
             'w_out': 1}


def _params(sem):
    return pltpu.CompilerParams(dimension_semantics=sem, vmem_limit_bytes=VMEM_LIMIT)


def _row_tile(m, cap):
    best = None
    for t in range(SUBLANES, min(m, cap) + 1, SUBLANES):
        if m % t == 0:
            best = t
    return best if best is not None else m


def _col_tile(n):
    for t in (512, 384, 256, 128):
        if n % t == 0:
            return t
    return n


def _any_tile(m, cap):
    best = 1
    for t in range(1, min(m, cap) + 1):
        if m % t == 0:
            best = t
    return best


def _mm_fwd(a, b, name):
    m, _ = a.shape
    g, k, n = b.shape
    tm, tn = _row_tile(m, 832), _col_tile(n)
    nj = n // tn

    def body(a_ref, b_ref, o_ref):
        o_ref[...] = jnp.dot(a_ref[...].astype(BF16), b_ref[0].astype(BF16), preferred_element_type=F32)

    return pl.pallas_call(
        body, name=name, grid=(g, m // tm, nj),
        in_specs=[pl.BlockSpec((tm, k), lambda gi, i, j: (i, gi)),
                  pl.BlockSpec((1, k, tn), lambda gi, i, j: (gi, 0, j))],
        out_specs=pl.BlockSpec((tm, tn), lambda gi, i, j: (i, gi * nj + j)),
        out_shape=jax.ShapeDtypeStruct((m, g * n), F32),
        compiler_params=_params(("arbitrary", "arbitrary", "arbitrary")),
    )(a, b)


def _mm_da(ct, b, name):
    m, _ = ct.shape
    g, k, n = b.shape
    tm, tk = _row_tile(m, 416), _col_tile(k)
    nk = k // tk

    def body(c_ref, b_ref, o_ref):
        o_ref[...] = lax.dot_general(c_ref[...].astype(BF16), b_ref[0].astype(BF16), (((1,), (1,)), ((), ())),
                                     preferred_element_type=F32)

    return pl.pallas_call(
        body, name=name, grid=(g, m // tm, nk),
        in_specs=[pl.BlockSpec((tm, n), lambda gi, i, j: (i, gi)),
                  pl.BlockSpec((1, tk, n), lambda gi, i, j: (gi, j, 0))],
        out_specs=pl.BlockSpec((tm, tk), lambda gi, i, j: (i, gi * nk + j)),
        out_shape=jax.ShapeDtypeStruct((m, g * k), F32),
        compiler_params=_params(("arbitrary", "arbitrary", "arbitrary")),
    )(ct, b)


def _mm_db(a, ct, g, k, n, name):
    m = a.shape[0]
    tm, tk, tn = _row_tile(m, 832), _col_tile(k), _col_tile(n)
    nk, nn = k // tk, n // tn

    def body(a_ref, c_ref, o_ref):
        @pl.when(pl.program_id(3) == 0)
        def _():
            o_ref[...] = jnp.zeros_like(o_ref)

        o_ref[0] += lax.dot_general(a_ref[...].astype(BF16), c_ref[...].astype(BF16), (((0,), (0,)), ((), ())),
                                    preferred_element_type=F32)

    return pl.pallas_call(
        body, name=name, grid=(g, nk, nn, m // tm),
        in_specs=[pl.BlockSpec((tm, tk), lambda gi, i, j, r: (r, gi * nk + i)),
                  pl.BlockSpec((tm, tn), lambda gi, i, j, r: (r, gi * nn + j))],
        out_specs=pl.BlockSpec((1, tk, tn), lambda gi, i, j, r: (gi, i, j)),
        out_shape=jax.ShapeDtypeStruct((g, k, n), F32),
        compiler_params=_params(("arbitrary", "arbitrary", "arbitrary", "arbitrary")),
    )(a, ct)


def _make_gmm(name):
    @jax.custom_vjp
    def gmm(a, b):
        return _mm_fwd(a, b, name + "_fwd")

    def fwd(a, b):
        return _mm_fwd(a, b, name + "_fwd"), (a, b)

    def bwd(res, ct):
        a, b = res
        g, k, n = b.shape
        return _mm_da(ct, b, name + "_da"), _mm_db(a, ct, g, k, n, name + "_db")

    gmm.defvjp(fwd, bwd)
    return gmm


def _mm(name, a, w, carrier):
    @jax.custom_vjp
    def mm(a, w3, carrier3):
        return _mm_fwd(a, w3, name + "_fwd")

    def fwd(a, w3, carrier3):
        return _mm_fwd(a, w3, name + "_fwd"), (a, w3)

    def bwd(res, ct):
        a, w3 = res
        g, k, n = w3.shape
        return _mm_da(ct, w3, name + "_da"), jnp.zeros_like(w3), _mm_db(a, ct, g, k, n, name + "_db")

    mm.defvjp(fwd, bwd)
    return mm(a, w[None], carrier[None])


def _make_rowwise(name, fn, n_row, n_par, out_widths, tm_cap, use_ridx=False):
    n_out = len(out_widths)

    def bind(tm):
        if not use_ridx:
            return fn
        ridx = pl.program_id(0) * tm + lax.broadcasted_iota(jnp.int32, (tm, 1), 0)
        return functools.partial(fn, ridx)

    def specs(args, tm):
        rows = [pl.BlockSpec((tm, a.shape[1]), lambda i: (i, 0)) for a in args[:n_row]]
        pars = [pl.BlockSpec(a.shape, lambda i: (0, 0)) for a in args[n_row:]]
        return rows, pars

    def fwd_call(*args):
        t = args[0].shape[0]
        tm = _row_tile(t, tm_cap)
        rows, pars = specs(args, tm)

        def body(*refs):
            vals = [r[...] for r in refs[:n_row + n_par]]
            res = bind(tm)(*vals)
            for o_ref, r in zip(refs[n_row + n_par:], res):
                o_ref[...] = r

        outs = pl.pallas_call(
            body, name=name + "_fwd", grid=(t // tm,), in_specs=rows + pars,
            out_specs=[pl.BlockSpec((tm, w), lambda i: (i, 0)) for w in out_widths],
            out_shape=[jax.ShapeDtypeStruct((t, w), F32) for w in out_widths],
            compiler_params=_params(("arbitrary",)),
        )(*args)
        return tuple(outs)

    def bwd_call(args, cts):
        t = args[0].shape[0]
        tm = _row_tile(t, tm_cap)
        rows, pars = specs(args, tm)
        ct_specs = [pl.BlockSpec((tm, w), lambda i: (i, 0)) for w in out_widths]
        n_in = n_row + n_par

        def body(*refs):
            vals = [r[...] for r in refs[:n_in]]
            ct_vals = tuple(r[...] for r in refs[n_in:n_in + n_out])
            d_refs = refs[n_in + n_out:]
            f = bind(tm)
            _, vjp = jax.vjp(lambda *a: tuple(f(*a)), *vals)
            grads = vjp(ct_vals)
            for i in range(n_row):
                d_refs[i][...] = grads[i]
            if n_par:
                @pl.when(pl.program_id(0) == 0)
                def _():
                    for j in range(n_par):
                        d_refs[n_row + j][...] = jnp.zeros_like(d_refs[n_row + j])

                for j in range(n_par):
                    d_refs[n_row + j][...] += grads[n_row + j]

        outs = pl.pallas_call(
            body, name=name + "_bwd", grid=(t // tm,), in_specs=rows + pars + ct_specs,
            out_specs=rows + pars,
            out_shape=[jax.ShapeDtypeStruct(a.shape, F32) for a in args],
            compiler_params=_params(("arbitrary",)),
        )(*args, *cts)
        return tuple(outs)

    @jax.custom_vjp
    def op(*args):
        return fwd_call(*args)

    def fwd(*args):
        return fwd_call(*args), args

    def bwd(args, cts):
        return bwd_call(args, cts)

    op.defvjp(fwd, bwd)
    return op


def _make_chunk_scan(name, fn, state_shape, n_seq, n_par, out_widths):
    n_out = len(out_widths)
    zeros_idx = (0,) * len(state_shape)

    def fwd_call(*args):
        t = args[0].shape[0]
        nc = t // CHUNK
        seq_specs = [pl.BlockSpec((CHUNK, a.shape[1]), lambda c: (c, 0)) for a in args[:n_seq]]
        par_specs = [pl.BlockSpec(a.shape, lambda c: (0, 0)) for a in args[n_seq:]]

        def body(*refs):
            ins = refs[:n_seq + n_par]
            out_refs = refs[n_seq + n_par:n_seq + n_par + n_out]
            states_ref = refs[n_seq + n_par + n_out]
            st = refs[-1]

            @pl.when(pl.program_id(0) == 0)
            def _():
                st[...] = jnp.zeros_like(st)

            s0 = st[...]
            states_ref[0] = s0
            res = fn(s0, *[r[...] for r in ins])
            st[...] = res[0]
            for o_ref, r in zip(out_refs, res[1:]):
                o_ref[...] = r

        outs = pl.pallas_call(
            body, name=name + "_fwd", grid=(nc,), in_specs=seq_specs + par_specs,
            out_specs=[pl.BlockSpec((CHUNK, w), lambda c: (c, 0)) for w in out_widths]
            + [pl.BlockSpec((1,) + state_shape, lambda c: (c,) + zeros_idx)],
            out_shape=[jax.ShapeDtypeStruct((t, w), F32) for w in out_widths]
            + [jax.ShapeDtypeStruct((nc,) + state_shape, F32)],
            scratch_shapes=[pltpu.VMEM(state_shape, F32)],
            compiler_params=_params(("arbitrary",)),
        )(*args)
        return tuple(outs[:n_out]), outs[n_out]

    def bwd_call(args, states, cts):
        t = args[0].shape[0]
        nc = t // CHUNK
        rev = lambda c: (nc - 1 - c, 0)
        seq_specs = [pl.BlockSpec((CHUNK, a.shape[1]), rev) for a in args[:n_seq]]
        par_specs = [pl.BlockSpec(a.shape, lambda c: (0, 0)) for a in args[n_seq:]]
        ct_specs = [pl.BlockSpec((CHUNK, w), rev) for w in out_widths]
        st_spec = pl.BlockSpec((1,) + state_shape, lambda c: (nc - 1 - c,) + zeros_idx)
        n_in = n_seq + n_par

        def body(*refs):
            vals = [r[...] for r in refs[:n_in]]
            s0 = refs[n_in][0]
            ct_vals = tuple(r[...] for r in refs[n_in + 1:n_in + 1 + n_out])
            d_refs = refs[n_in + 1 + n_out:-1]
            dst = refs[-1]

            @pl.when(pl.program_id(0) == 0)
            def _():
                dst[...] = jnp.zeros_like(dst)
                for j in range(n_par):
                    d_refs[n_seq + j][...] = jnp.zeros_like(d_refs[n_seq + j])

            _, vjp = jax.vjp(lambda *a: tuple(fn(*a)), s0, *vals)
            grads = vjp((dst[...],) + ct_vals)
            dst[...] = grads[0]
            for i in range(n_seq):
                d_refs[i][...] = grads[1 + i]
            for j in range(n_par):
                d_refs[n_seq + j][...] += grads[1 + n_seq + j]

        outs = pl.pallas_call(
            body, name=name + "_bwd", grid=(nc,), in_specs=seq_specs + par_specs + [st_spec] + ct_specs,
            out_specs=seq_specs + par_specs,
            out_shape=[jax.ShapeDtypeStruct(a.shape, F32) for a in args],
            scratch_shapes=[pltpu.VMEM(state_shape, F32)],
            compiler_params=_params(("arbitrary",)),
        )(*args, states, *cts)
        return tuple(outs)

    @jax.custom_vjp
    def op(*args):
        return fwd_call(*args)[0]

    def fwd(*args):
        outs, states = fwd_call(*args)
        return outs, (args, states)

    def bwd(res, cts):
        args, states = res
        return bwd_call(args, states, cts)

    op.defvjp(fwd, bwd)
    return op


def _s5_scan_fwd(bre, bim, are, aim, name):
    t, r, _ = bre.shape
    tb = _any_tile(t, 208)
    blk = pl.BlockSpec((tb, r, LANES), lambda i: (i, 0, 0))
    par = pl.BlockSpec((r, LANES), lambda i: (0, 0))

    def body(bre_ref, bim_ref, are_ref, aim_ref, sre_ref, sim_ref, st):
        @pl.when(pl.program_id(0) == 0)
        def _():
            st[...] = jnp.zeros_like(st)

        ar, ai = are_ref[...], aim_ref[...]

        def step(k, carry):
            sr, si = carry
            nr = ar * sr - ai * si + bre_ref[k]
            ni = ar * si + ai * sr + bim_ref[k]
            sre_ref[k] = nr
            sim_ref[k] = ni
            return nr, ni

        sr, si = lax.fori_loop(0, tb, step, (st[0], st[1]), unroll=4)
        st[0] = sr
        st[1] = si

    return pl.pallas_call(
        body, name=name, grid=(t // tb,), in_specs=[blk, blk, par, par], out_specs=[blk, blk],
        out_shape=[jax.ShapeDtypeStruct(bre.shape, F32)] * 2,
        scratch_shapes=[pltpu.VMEM((2, r, LANES), F32)],
        compiler_params=_params(("arbitrary",)),
    )(bre, bim, are, aim)


def _s5_scan_bwd(dsr, dsi, sre, sim, are, aim, name):
    t, r, _ = sre.shape
    tb = _any_tile(t, 208)
    nb = t // tb
    blk = pl.BlockSpec((tb, r, LANES), lambda i: (nb - 1 - i, 0, 0))
    par = pl.BlockSpec((r, LANES), lambda i: (0, 0))

    def body(dsr_ref, dsi_ref, sre_ref, sim_ref, are_ref, aim_ref, gre_ref, gim_ref, dar_ref, dai_ref, st):
        @pl.when(pl.program_id(0) == 0)
        def _():
            st[...] = jnp.zeros_like(st)
            dar_ref[...] = jnp.zeros_like(dar_ref)
            dai_ref[...] = jnp.zeros_like(dai_ref)

        ar, ai = are_ref[...], aim_ref[...]

        def step(k, carry):
            gr, gi, dar, dai = carry
            q = tb - 1 - k
            s_r, s_i = sre_ref[q], sim_ref[q]
            dar = dar + gr * s_r + gi * s_i
            dai = dai + gi * s_r - gr * s_i
            ngr = dsr_ref[q] + ar * gr + ai * gi
            ngi = dsi_ref[q] + ar * gi - ai * gr
            gre_ref[q] = ngr
            gim_ref[q] = ngi
            return ngr, ngi, dar, dai

        gr, gi, dar, dai = lax.fori_loop(0, tb, step, (st[0], st[1], dar_ref[...], dai_ref[...]), unroll=4)
        st[0] = gr
        st[1] = gi
        dar_ref[...] = dar
        dai_ref[...] = dai

    return pl.pallas_call(
        body, name=name, grid=(nb,), in_specs=[blk, blk, blk, blk, par, par], out_specs=[blk, blk, par, par],
        out_shape=[jax.ShapeDtypeStruct(sre.shape, F32)] * 2 + [jax.ShapeDtypeStruct(are.shape, F32)] * 2,
        scratch_shapes=[pltpu.VMEM((2, r, LANES), F32)],
        compiler_params=_params(("arbitrary",)),
    )(dsr, dsi, sre, sim, are, aim)


def _make_s5_scan(name):
    @jax.custom_vjp
    def scan(bre, bim, are, aim):
        return tuple(_s5_scan_fwd(bre, bim, are, aim, name + "_fwd"))

    def fwd(bre, bim, are, aim):
        sre, sim = _s5_scan_fwd(bre, bim, are, aim, name + "_fwd")
        return (sre, sim), (sre, sim, are, aim)

    def bwd(res, cts):
        sre, sim, are, aim = res
        return tuple(_s5_scan_bwd(cts[0], cts[1], sre, sim, are, aim, name + "_bwd"))

    scan.defvjp(fwd, bwd)
    return scan


def _dot(a, b):
    return jnp.dot(a.astype(BF16), b.astype(BF16), preferred_element_type=F32)


def _dot_nt(a, b):
    return lax.dot_general(a.astype(BF16), b.astype(BF16), (((1,), (1,)), ((), ())), preferred_element_type=F32)


def _dot_tn(a, b):
    return lax.dot_general(a.astype(BF16), b.astype(BF16), (((0,), (0,)), ((), ())), preferred_element_type=F32)


def _dot_f32(a, b):
    return jnp.dot(a, b, precision=lax.Precision.HIGHEST, preferred_element_type=F32)


def _dot_tn_f32(a, b):
    return lax.dot_general(a, b, (((0,), (0,)), ((), ())), precision=lax.Precision.HIGHEST,
                           preferred_element_type=F32)


def _iota(shape, dim):
    return lax.broadcasted_iota(jnp.int32, shape, dim)


def _tri(strict=False):
    r, c = _iota((CHUNK, CHUNK), 0), _iota((CHUNK, CHUNK), 1)
    return (r > c) if strict else (r >= c)


def _silu(x):
    return x * jax.nn.sigmoid(x)


def _layer_norm(z, g, b):
    mu = jnp.mean(z, axis=-1, keepdims=True)
    var = jnp.mean(jnp.square(z - mu), axis=-1, keepdims=True)
    return (z - mu) * lax.rsqrt(var + LN_EPS) * g + b


def _ssd_chunk(state, xbc, sm, a_c, d_exp):
    width = SSD_HEADS * SSD_HEAD
    x = xbc[:, :width]
    lane = _iota((CHUNK, LANES), 1)
    dtc = jnp.where(lane < SSD_HEADS, sm, 0.0)
    low = _tri().astype(F32)
    eye = (_iota((CHUNK, CHUNK), 0) == _iota((CHUNK, CHUNK), 1)).astype(F32)
    head_col, head_row = _iota((LANES, width), 1), _iota((LANES, width), 0) * SSD_HEAD
    expand = ((head_col >= head_row) & (head_col < head_row + SSD_HEAD)).astype(F32)
    acum_c = _dot_f32(low, dtc * a_c)
    acum_ct = _dot_tn_f32(acum_c, eye)
    dt_exp = _dot_f32(dtc, expand)
    acum = _dot_f32(acum_c, expand)
    xd = x * dt_exp
    last = acum[CHUNK - 1:CHUNK, :]
    to_end = jnp.exp(last - acum)
    eac = jnp.exp(acum)
    causal = _tri()
    first_half = _iota((CHUNK, LANES), 1) < SSD_HEAD
    top_rows = _iota((LANES, LANES), 0) < SSD_HEAD
    ys, new_states = [], []
    for p in range(SSD_HEADS // 2):
        grp = (2 * p) // (SSD_HEADS // SSD_GROUPS)
        bg = xbc[:, width + grp * SSD_STATE: width + (grp + 1) * SSD_STATE]
        cg = xbc[:, width + (SSD_GROUPS + grp) * SSD_STATE: width + (SSD_GROUPS + grp + 1) * SSD_STATE]
        cols = slice(p * LANES, (p + 1) * LANES)
        scores = _dot_nt(cg, bg)
        xd_p = xd[:, cols]
        y = jnp.zeros((CHUNK, LANES), F32)
        for hh in range(2):
            h = 2 * p + hh
            seg = acum_c[:, h:h + 1] - acum_ct[h:h + 1, :]
            dec = jnp.where(causal, jnp.exp(jnp.minimum(seg, 0.0)), 0.0)
            xm = jnp.where(first_half if hh == 0 else jnp.logical_not(first_half), xd_p, 0.0)
            y = y + _dot(scores * dec, xm)
        s_prev = state[p * LANES:(p + 1) * LANES, :]
        y = y + _dot_nt(cg, s_prev) * eac[:, cols]
        y = y + x[:, cols] * d_exp[:, cols]
        cd = jnp.where(top_rows, jnp.exp(acum_c[CHUNK - 1:CHUNK, 2 * p:2 * p + 1]),
                       jnp.exp(acum_c[CHUNK - 1:CHUNK, 2 * p + 1:2 * p + 2]))
        new_states.append(s_prev * cd + _dot_tn(xd_p * to_end[:, cols], bg))
        ys.append(y)
    return jnp.concatenate(new_states, axis=0), jnp.concatenate(ys, axis=1)


def _gdn_chunk(state, qkv, sm):
    width = GDN_HEADS * GDN_HEAD
    g0, b0 = SSD_HEADS, SSD_HEADS + GDN_HEADS
    lane = _iota((CHUNK, LANES), 1)
    gc = jnp.where((lane >= g0) & (lane < b0), sm, 0.0)
    low = _tri().astype(F32)
    eye = (_iota((CHUNK, CHUNK), 0) == _iota((CHUNK, CHUNK), 1)).astype(F32)
    gcum = _dot_f32(low, gc)
    gcum_t = _dot_tn_f32(gcum, eye)
    causal, strict = _tri(), _tri(strict=True)
    outs, new_states = [], []
    for h in range(GDN_HEADS):
        q = qkv[:, h * GDN_HEAD:(h + 1) * GDN_HEAD]
        k = qkv[:, width + h * GDN_HEAD: width + (h + 1) * GDN_HEAD]
        v = qkv[:, 2 * width + h * GDN_HEAD: 2 * width + (h + 1) * GDN_HEAD]
        beta = sm[:, b0 + h:b0 + h + 1]
        gcol = gcum[:, g0 + h:g0 + h + 1]
        grow = gcum_t[g0 + h:g0 + h + 1, :]
        glast = gcum[CHUNK - 1:CHUNK, g0 + h:g0 + h + 1]
        gamma = jnp.where(causal, jnp.exp(jnp.minimum(gcol - grow, 0.0)), 0.0)
        a_mat = jnp.where(strict, _dot_nt(k, k) * gamma * beta, 0.0)
        egc = jnp.exp(gcol)
        sol = jnp.concatenate([v * beta, k * (beta * egc)], axis=1)
        nmat = -a_mat
        for i in range(6):
            sol = sol + _dot(nmat, sol)
            if i < 5:
                nmat = _dot(nmat, nmat)
        u, w = sol[:, :GDN_HEAD], sol[:, GDN_HEAD:]
        attn = _dot_nt(q, k) * gamma
        s_prev = state[h * GDN_HEAD:(h + 1) * GDN_HEAD, :]
        v_new = u - _dot(w, s_prev)
        outs.append(_dot(q * egc, s_prev) + _dot(attn, v_new))
        new_states.append(s_prev * jnp.exp(glast) + _dot_tn(k * jnp.exp(glast - gcol), v_new))
    return jnp.concatenate(new_states, axis=0), jnp.concatenate(outs, axis=1)


def _row_fns(d_model, pad_rows, loss_rows, alpha):
    def keep(ridx, v):
        return jnp.where(ridx >= pad_rows, v, 0.0)

    def ln_in(ridx, h, g, b):
        return (keep(ridx, _layer_norm(h, g, b)),)

    def ln_layer(ridx, h, o, g, b):
        return (keep(ridx, _layer_norm(alpha * h + o, g, b)),)

    def s5_act(y_re, y_im, u, d):
        return (jax.nn.gelu(y_re + y_im + d * u),)

    def s5_glu(v0, lin, z, bias):
        return (v0 * jax.nn.sigmoid(lin + bias) * _silu(z),)

    def small_act(ridx, raw, bias, scale):
        lane = _iota(raw.shape, 1)
        sp = jax.nn.softplus(raw + bias)
        g0, b0 = SSD_HEADS, SSD_HEADS + GDN_HEADS
        out = jnp.where(lane < g0, sp, jnp.where(lane < b0, scale * sp,
                                                 jnp.where(lane < b0 + GDN_HEADS, jax.nn.sigmoid(raw), 0.0)))
        return (keep(ridx, out),)

    def conv(xs, w):
        acc = xs[0] * w[0:1, :]
        for j in range(1, CONV_K):
            acc = acc + xs[j] * w[j:j + 1, :]
        return acc

    def ssd_conv(ridx, x0, x1, x2, x3, w, b):
        return (keep(ridx, _silu(conv((x0, x1, x2, x3), w) + b)),)

    def ssd_norm(y, z, g):
        v = y * _silu(z)
        return (v * lax.rsqrt(jnp.mean(v * v, axis=-1, keepdims=True) + LN_EPS) * g,)

    def gdn_conv(x0, x1, x2, x3, w):
        a = _silu(conv((x0, x1, x2, x3), w))
        width = GDN_HEADS * GDN_HEAD
        parts = []
        for h in range(2 * GDN_HEADS):
            z = a[:, h * GDN_HEAD:(h + 1) * GDN_HEAD]
            z = z * lax.rsqrt(jnp.sum(z * z, axis=-1, keepdims=True) + 1e-6)
            parts.append(z * GDN_HEAD ** -0.5 if h < GDN_HEADS else z)
        parts.append(a[:, 2 * width:])
        return (jnp.concatenate(parts, axis=1),)

    def gdn_norm(o, z, g):
        parts = []
        for h in range(GDN_HEADS):
            cols = slice(h * GDN_HEAD, (h + 1) * GDN_HEAD)
            oh = o[:, cols]
            oh = oh * lax.rsqrt(jnp.mean(oh * oh, axis=-1, keepdims=True) + LN_EPS) * g
            parts.append(oh * _silu(z[:, cols]))
        return (jnp.concatenate(parts, axis=1),)

    def merge(oa, ob, oc, gate, bias):
        acc = None
        for k, o in enumerate((oa, ob, oc)):
            cols = slice(k * d_model, (k + 1) * d_model)
            term = jax.nn.sigmoid(gate[:, cols] + bias[:, cols]) * o
            acc = term if acc is None else acc + term
        return (acc,)

    def loss_rows_fn(ridx, h, tgt):
        row = 0.5 * jnp.mean(jnp.square(h - tgt), axis=-1, keepdims=True)
        row = jnp.where(ridx >= loss_rows, row, 0.0)
        lane = _iota((h.shape[0], LANES), 1)
        return (jnp.where(lane == 0, row, 0.0),)

    return dict(ln_in=ln_in, ln_layer=ln_layer, s5_act=s5_act, s5_glu=s5_glu, small_act=small_act,
                ssd_conv=ssd_conv, ssd_norm=ssd_norm, gdn_conv=gdn_conv, gdn_norm=gdn_norm, merge=merge,
                loss=loss_rows_fn)


def _exchange(srcs, gather, name):
    n_arr = len(srcs)
    n_peer = N_DEV - 1
    flips = [(fx, fy, fc) for fx in (0, 1) for fy in (0, 1) for fc in (0, 1)][1:]
    blocks = [s.shape if gather else s.shape[1:] for s in srcs]

    def body(*refs):
        src_refs, out_refs = refs[:n_arr], refs[n_arr:2 * n_arr]
        send_sems, recv_sems, local_sems = refs[2 * n_arr:]
        x, y, c = lax.axis_index("x"), lax.axis_index("y"), lax.axis_index("c")
        me = 4 * x + 2 * y + c

        def pick(a, j):
            return src_refs[a] if gather else src_refs[a].at[j]

        own = [pltpu.make_async_copy(pick(a, me), out_refs[a].at[me], local_sems.at[a]) for a in range(n_arr)]
        for cp in own:
            cp.start()
        copies = []
        for k, (fx, fy, fc) in enumerate(flips):
            px = 1 - x if fx else x
            py = 1 - y if fy else y
            pc = 1 - c if fc else c
            peer = 4 * px + 2 * py + pc
            for a in range(n_arr):
                cp = pltpu.make_async_remote_copy(
                    src_ref=pick(a, peer), dst_ref=out_refs[a].at[me],
                    send_sem=send_sems.at[a * n_peer + k], recv_sem=recv_sems.at[a * n_peer + k],
                    device_id=(px, py, pc), device_id_type=pl.DeviceIdType.MESH)
                cp.start()
                copies.append(cp)
        for cp in copies:
            cp.wait()
        for cp in own:
            cp.wait()

    return pl.pallas_call(
        body, name=name,
        in_specs=[pl.BlockSpec(memory_space=pl.ANY)] * n_arr, out_specs=[pl.BlockSpec(memory_space=pl.ANY)] * n_arr,
        out_shape=[jax.ShapeDtypeStruct((N_DEV,) + tuple(b), s.dtype) for b, s in zip(blocks, srcs)],
        scratch_shapes=[pltpu.SemaphoreType.DMA((n_arr * n_peer,)), pltpu.SemaphoreType.DMA((n_arr * n_peer,)),
                        pltpu.SemaphoreType.DMA((n_arr,))],
    )(*srcs)


def _adamw_body(p_ref, w_ref, m_ref, v_ref, g_ref, d_ref, nm_ref, nv_ref):
    bc1 = 1.0 - ADAM_B1 ** ADAM_STEP
    bc2 = 1.0 - ADAM_B2 ** ADAM_STEP
    g = p_ref[0]
    for k in range(1, N_DEV):
        g = g + p_ref[k]
    nm = ADAM_B1 * m_ref[...] + (1.0 - ADAM_B1) * g
    nv = ADAM_B2 * v_ref[...] + (1.0 - ADAM_B2) * jnp.square(g)
    m_hat = nm / bc1
    v_hat = nv / bc2
    g_ref[...] = g
    d_ref[...] = -ADAM_LR * (m_hat / (jnp.sqrt(v_hat) + ADAM_EPS) + ADAM_WD * w_ref[...])
    nm_ref[...] = nm
    nv_ref[...] = nv


def _adamw_flat(parts, w, m, v):
    rows = w.shape[0]
    tr = FLAT_ROWS
    blk = pl.BlockSpec((tr, LANES), lambda i: (i, 0))
    return pl.pallas_call(
        functools.partial(_adamw_body), name="adamw_flat", grid=(rows // tr,),
        in_specs=[pl.BlockSpec((N_DEV, tr, LANES), lambda i: (0, i, 0)), blk, blk, blk],
        out_specs=[blk] * 4, out_shape=[jax.ShapeDtypeStruct((rows, LANES), F32)] * 4,
        compiler_params=_params(("arbitrary",)),
    )(parts, w, m, v)


def _adamw_stacked(parts, w, m, v):
    depth, d, n = w.shape
    tr = _row_tile(d, 128)
    blk = pl.BlockSpec((1, tr, n), lambda l, i: (l, i, 0))
    return pl.pallas_call(
        functools.partial(_adamw_body), name="adamw_w_in", grid=(depth, d // tr),
        in_specs=[pl.BlockSpec((N_DEV, 1, tr, n), lambda l, i: (0, l, i, 0)), blk, blk, blk],
        out_specs=[blk] * 4, out_shape=[jax.ShapeDtypeStruct(w.shape, F32)] * 4,
        compiler_params=_params(("arbitrary", "arbitrary")),
    )(parts, w, m, v)


def _pad_flat(vec, rows):
    return jnp.pad(vec, (0, rows * LANES - vec.shape[0])).reshape(rows, LANES)


def _rows_for(n):
    return -(-n // (FLAT_ROWS * LANES)) * FLAT_ROWS


def _split_shards(full, dim):
    shp = full.shape
    parts = full.reshape(shp[:dim] + (N_DEV, shp[dim] // N_DEV) + shp[dim + 1:])
    return jnp.moveaxis(parts, dim, 0).reshape(N_DEV, -1)


def _join_shards(rows, local_shape, dim):
    parts = jnp.moveaxis(rows.reshape((N_DEV,) + tuple(local_shape)), 0, dim)
    shp = tuple(local_shape)
    return parts.reshape(shp[:dim] + (N_DEV * shp[dim],) + shp[dim + 1:])


def _s5_tables(a_re, a_im, log_step, b_re, b_im, c_re, c_im):
    lam_re = jnp.minimum(a_re, -1e-4)
    lam_im = a_im
    step = jnp.exp(log_step)[:, None]
    mag = jnp.exp(lam_re * step)
    abar_re, abar_im = mag * jnp.cos(lam_im * step), mag * jnp.sin(lam_im * step)
    den = lam_re * lam_re + lam_im * lam_im
    nr, ni = abar_re - 1.0, abar_im
    coef_re = (nr * lam_re + ni * lam_im) / den
    coef_im = (ni * lam_re - nr * lam_im) / den
    bbar_re = coef_re[..., None] * b_re - coef_im[..., None] * b_im
    bbar_im = coef_re[..., None] * b_im + coef_im[..., None] * b_re
    groups = a_re.shape[0]
    nblk = groups // S5_BLOCK_GROUPS
    eye = jnp.eye(S5_BLOCK_GROUPS, dtype=F32)

    def in_blocks(bb):
        t = jnp.swapaxes(bb, 1, 2).reshape(nblk, S5_BLOCK_GROUPS, S5_GROUP, S5_STATE)
        blk = jnp.einsum('ab,jacp->jacbp', eye, t)
        return blk.reshape(nblk, S5_BLOCK_GROUPS * S5_GROUP, S5_BLOCK_GROUPS * S5_STATE)

    def out_blocks(cc):
        t = jnp.swapaxes(cc, 1, 2).reshape(nblk, S5_BLOCK_GROUPS, S5_STATE, S5_GROUP)
        blk = jnp.einsum('ab,japc->japbc', eye, t)
        return blk.reshape(nblk, S5_BLOCK_GROUPS * S5_STATE, S5_BLOCK_GROUPS * S5_GROUP)

    rows = groups * S5_STATE // LANES
    return dict(b_re=in_blocks(bbar_re), b_im=in_blocks(bbar_im), c_re=out_blocks(c_re), c_im=out_blocks(-c_im),
                a_re=abar_re.reshape(rows, LANES), a_im=abar_im.reshape(rows, LANES))


def _shifted(x):
    t = x.shape[0]
    return [jnp.pad(x, ((CONV_K - 1 - j, 0), (0, 0)))[:t] for j in range(CONV_K)]


def _in_widths(d_model):
    return [BRANCH, BRANCH, SSD_HEADS * SSD_HEAD + 2 * SSD_GROUPS * SSD_STATE, SSD_HEADS, BRANCH,
            3 * BRANCH, GDN_HEADS, GDN_HEADS, BRANCH, 3 * d_model]


def _local_loss(w, mats, x, target):
    n_meta, d_model = w['meta'].shape
    depth = len(mats['in'])
    seq = x.shape[0]
    pad_rows = CHUNK - n_meta
    first = pad_rows + n_meta
    t_all = first + seq
    alpha = (2 * depth) ** 0.25
    fns = _row_fns(d_model, pad_rows, first, alpha)
    row = lambda nm, key, n_row, n_par, widths, cap, ridx=False: _make_rowwise(nm, fns[key], n_row, n_par, widths, cap, ridx)

    h = jnp.concatenate([jnp.zeros((pad_rows, d_model), F32), w['meta'], x], axis=0)
    (h,) = row("ln_in", 'ln_in', 1, 2, [d_model], 416, True)(h, w['ln_in_g'][None], w['ln_in_b'][None])

    n_small = SSD_HEADS + 2 * GDN_HEADS

    def small_cols(ps):
        return jnp.pad(jnp.concatenate([ps[3], ps[6], ps[7]], axis=1), ((0, 0), (0, LANES - n_small)))

    for l in range(depth):
        pw, pc = mats['in'][l], w['c_in'][l]
        s5_u = _mm("in_s5u", h, pw[0], pc[0])
        s5_z = _mm("in_s5z", h, pw[1], pc[1])
        ssd_xbc = _mm("in_ssdx", h, pw[2], pc[2])
        ssd_z = _mm("in_ssdz", h, pw[4], pc[4])
        gdn_qkv = _mm("in_gdnq", h, pw[5], pc[5])
        gdn_z = _mm("in_gdnz", h, pw[8], pc[8])
        gate = _mm("in_gate", h, pw[9], pc[9])
        small = _mm("in_small", h, small_cols(pw), small_cols(pc))

        zeros_tail = jnp.zeros((LANES - n_small,), F32)
        bias = jnp.concatenate([w['ssd_dt_bias'][l], w['gdn_dt_bias'][l], jnp.zeros((GDN_HEADS,), F32), zeros_tail])[None]
        scale = jnp.concatenate([jnp.ones((SSD_HEADS,), F32), -jnp.exp(w['gdn_a_log'][l]),
                                 jnp.zeros((GDN_HEADS,), F32), zeros_tail])[None]
        (sm,) = row("small_act", 'small_act', 1, 2, [LANES], 832, True)(small, bias, scale)

        tb = _s5_tables(w['s5_a_re'][l], w['s5_a_im'][l], w['s5_log_step'][l], w['s5_b_re'][l], w['s5_b_im'][l],
                        w['s5_c_re'][l], w['s5_c_im'][l])
        srows = tb['a_re'].shape[0]
        bu_re = _make_gmm("s5_bre")(s5_u, tb['b_re']).reshape(t_all, srows, LANES)
        bu_im = _make_gmm("s5_bim")(s5_u, tb['b_im']).reshape(t_all, srows, LANES)
        s_re, s_im = _make_s5_scan("s5_scan")(bu_re, bu_im, tb['a_re'], tb['a_im'])
        y_re = _make_gmm("s5_cre")(s_re.reshape(t_all, srows * LANES), tb['c_re'])
        y_im = _make_gmm("s5_cim")(s_im.reshape(t_all, srows * LANES), tb['c_im'])
        (v0,) = row("s5_act", 's5_act', 3, 1, [BRANCH], 416)(y_re, y_im, s5_u, w['s5_d'][l][None])
        lin = _mm("s5_glu_mm", v0, mats['glu'][l], w['c_glu'][l])
        (y_a,) = row("s5_glu", 's5_glu', 3, 1, [BRANCH], 416)(v0, lin, s5_z, w['s5_b_glu'][l][None])

        (xbc,) = row("ssd_conv", 'ssd_conv', 4, 2, [ssd_xbc.shape[1]], 208, True)(
            *_shifted(ssd_xbc), w['ssd_conv_w'][l], w['ssd_conv_b'][l][None])
        a_c = jnp.pad(-jnp.exp(w['ssd_a_log'][l]), (0, LANES - SSD_HEADS))[None]
        d_exp = jnp.repeat(w['ssd_d'][l], SSD_HEAD)[None]
        (y_ssd,) = _make_chunk_scan("ssd_scan", _ssd_chunk, (SSD_HEADS // 2 * LANES, SSD_STATE), 2, 2, [BRANCH])(
            xbc, sm, a_c, d_exp)
        (y_b,) = row("ssd_norm", 'ssd_norm', 2, 1, [BRANCH], 416)(y_ssd, ssd_z, w['ssd_norm_g'][l][None])

        (qkv,) = row("gdn_conv", 'gdn_conv', 4, 1, [3 * BRANCH], 104)(*_shifted(gdn_qkv), w['gdn_conv_w'][l])
        (o_gdn,) = _make_chunk_scan("gdn_scan", _gdn_chunk, (GDN_HEADS * GDN_HEAD, GDN_HEAD), 2, 0, [BRANCH])(qkv, sm)
        (y_c,) = row("gdn_norm", 'gdn_norm', 2, 1, [BRANCH], 416)(o_gdn, gdn_z, w['gdn_norm_g'][l][None])

        outs = [_mm(f"branch{k}", yk, mats['branch'][l, k], w['c_branch'][l, k])
                for k, yk in enumerate((y_a, y_b, y_c))]
        (merged,) = row("merge", 'merge', 4, 1, [d_model], 208)(*outs, gate, w['b_gate'][l].reshape(1, 3 * d_model))
        o = _mm("w_out", merged, mats['out'][l], w['c_out'][l])
        (h,) = row("ln_layer", 'ln_layer', 2, 2, [d_model], 416, True)(h, o, w['ln_g'][l][None], w['ln_b'][l][None])

    tgt = jnp.concatenate([jnp.zeros((first, d_model), F32), target], axis=0)
    (rows_loss,) = row("loss", 'loss', 2, 0, [LANES], 416, True)(h, tgt)
    return jnp.sum(rows_loss)


def _in_overlaps(d_model, n_loc):
    offs = [0]
    for wd in _in_widths(d_model):
        offs.append(offs[-1] + wd)
    out = []
    for i in range(len(offs) - 1):
        c0, c1 = offs[i], offs[i + 1]
        segs = []
        for k in range(N_DEV):
            g0, g1 = max(c0, k * n_loc), min(c1, (k + 1) * n_loc)
            if g0 < g1:
                segs.append((k, g0 - k * n_loc, g1 - k * n_loc, g0 - c0))
        out.append(segs)
    return out


MAT_WEIGHTS = ['s5_w_glu', 'w_branch', 'w_out']


def _step(x, target, w_loc, m_loc, v_loc):
    rest = [n for n in WEIGHTS if n in SHARD_DIM and n != 'w_in']
    small = [n for n in rest if n not in MAT_WEIGHTS]
    repl = [n for n in WEIGHTS if n not in SHARD_DIM]
    size = lambda names: sum(int(w_loc[n].size) for n in names)
    rows_mat, rows_small = _rows_for(size(MAT_WEIGHTS)), _rows_for(size(small))
    n_sh, n_rp = size(rest), size(repl) + 1
    rows_sh, rows_rp = _rows_for(n_sh), _rows_for(n_rp)
    depth, d_model, n_loc = w_loc['w_in'].shape
    overlaps = _in_overlaps(d_model, n_loc)

    mat_flat = _pad_flat(jnp.concatenate([w_loc[n].reshape(-1) for n in MAT_WEIGHTS]), rows_mat).astype(BF16)
    small_flat = _pad_flat(jnp.concatenate([w_loc[n].reshape(-1) for n in small]), rows_small)
    g_in, g_mat, g_small = _exchange([w_loc['w_in'].astype(BF16), mat_flat, small_flat], True, "gather_weights")
    full = {}
    for names, buf in ((MAT_WEIGHTS, g_mat.reshape(N_DEV, -1)), (small, g_small.reshape(N_DEV, -1))):
        off = 0
        for n in names:
            sz = int(w_loc[n].size)
            full[n] = _join_shards(buf[:, off:off + sz], w_loc[n].shape, SHARD_DIM[n])
            off += sz
    mats = dict(glu=full['s5_w_glu'], branch=full['w_branch'], out=full['w_out'], **{
        'in': [[jnp.concatenate([g_in[k, l, :, lo:hi] for k, lo, hi, _ in segs], axis=1) for segs in overlaps]
               for l in range(depth)]})
    w_diff = {n: w_loc[n] for n in repl}
    w_diff.update({n: full[n] for n in small})
    w_diff['c_in'] = [[jnp.zeros((d_model, wd), F32) for wd in _in_widths(d_model)] for _ in range(depth)]
    w_diff['c_glu'] = jnp.zeros(full['s5_w_glu'].shape, F32)
    w_diff['c_branch'] = jnp.zeros(full['w_branch'].shape, F32)
    w_diff['c_out'] = jnp.zeros(full['w_out'].shape, F32)

    loss, (g_w, g_x) = jax.value_and_grad(_local_loss, argnums=(0, 2))(w_diff, mats, x[0], target[0])
    g_w['s5_w_glu'], g_w['w_branch'], g_w['w_out'] = g_w['c_glu'], g_w['c_branch'], g_w['c_out']

    send_in = jnp.stack([jnp.stack([
        jnp.concatenate([g_w['c_in'][l][i][:, plo:plo + hi - lo]
                         for i, segs in enumerate(overlaps) for (kk, lo, hi, plo) in segs if kk == k], axis=1)
        for l in range(depth)]) for k in range(N_DEV)])
    sh_rows = jnp.concatenate([_split_shards(g_w[n], SHARD_DIM[n]) for n in rest], axis=1)
    sh_rows = jnp.pad(sh_rows, ((0, 0), (0, rows_sh * LANES - n_sh)))
    rp_vec = jnp.concatenate([g_w[n].reshape(-1) for n in repl] + [loss.reshape(1)])
    rp_vec = jnp.pad(rp_vec, (0, rows_rp * LANES - n_rp))
    send = jnp.concatenate([sh_rows, jnp.broadcast_to(rp_vec[None], (N_DEV, rows_rp * LANES))], axis=1)
    rows = rows_sh + rows_rp
    parts_in, parts = _exchange([send_in, send.reshape(N_DEV, rows, LANES)], False, "exchange_grads")

    def flat(src, extra):
        vec = jnp.concatenate([src[n].reshape(-1) for n in rest])
        vec = jnp.pad(vec, (0, rows_sh * LANES - n_sh))
        rep = jnp.concatenate([src[n].reshape(-1) for n in repl] + [extra])
        rep = jnp.pad(rep, (0, rows_rp * LANES - n_rp))
        return jnp.concatenate([vec, rep]).reshape(rows, LANES)

    one = jnp.ones((1,), F32)
    flat_outs = _adamw_flat(parts, flat(w_loc, one), flat(m_loc, one), flat(v_loc, one))
    in_outs = _adamw_stacked(parts_in, w_loc['w_in'], m_loc['w_in'], v_loc['w_in'])

    def unflat(buf, big):
        vec = buf.reshape(-1)
        out, o = {'w_in': big}, 0
        for n in rest:
            sz = int(w_loc[n].size)
            out[n] = vec[o:o + sz].reshape(w_loc[n].shape)
            o += sz
        o = rows_sh * LANES
        for n in repl:
            sz = int(w_loc[n].size)
            out[n] = vec[o:o + sz].reshape(w_loc[n].shape)
            o += sz
        return out, vec[o]

    (grads, loss_total), (deltas, _), (new_m, _), (new_v, _) = [unflat(f, b) for f, b in zip(flat_outs, in_outs)]
    return (loss_total, g_x[None], *[grads[n] for n in WEIGHTS], *[deltas[n] for n in WEIGHTS],
            *[new_m[n] for n in WEIGHTS], *[new_v[n] for n in WEIGHTS])


def kernel(x, meta, ln_in_g, ln_in_b, w_in, s5_a_re, s5_a_im, s5_log_step, s5_b_re, s5_b_im, s5_c_re, s5_c_im, s5_d, s5_w_glu, s5_b_glu, ssd_conv_w, ssd_conv_b, ssd_dt_bias, ssd_a_log, ssd_d, ssd_norm_g, gdn_conv_w, gdn_dt_bias, gdn_a_log, gdn_norm_g, w_branch, b_gate, w_out, ln_g, ln_b, loss_target, m_meta, m_ln_in_g, m_ln_in_b, m_w_in, m_s5_a_re, m_s5_a_im, m_s5_log_step, m_s5_b_re, m_s5_b_im, m_s5_c_re, m_s5_c_im, m_s5_d, m_s5_w_glu, m_s5_b_glu, m_ssd_conv_w, m_ssd_conv_b, m_ssd_dt_bias, m_ssd_a_log, m_ssd_d, m_ssd_norm_g, m_gdn_conv_w, m_gdn_dt_bias, m_gdn_a_log, m_gdn_norm_g, m_w_branch, m_b_gate, m_w_out, m_ln_g, m_ln_b, v_meta, v_ln_in_g, v_ln_in_b, v_w_in, v_s5_a_re, v_s5_a_im, v_s5_log_step, v_s5_b_re, v_s5_b_im, v_s5_c_re, v_s5_c_im, v_s5_d, v_s5_w_glu, v_s5_b_glu, v_ssd_conv_w, v_ssd_conv_b, v_ssd_dt_bias, v_ssd_a_log, v_ssd_d, v_ssd_norm_g, v_gdn_conv_w, v_gdn_dt_bias, v_gdn_a_log, v_gdn_norm_g, v_w_branch, v_b_gate, v_w_out, v_ln_g, v_ln_b):
    w_loc = dict(zip(WEIGHTS, (meta, ln_in_g, ln_in_b, w_in, s5_a_re, s5_a_im, s5_log_step, s5_b_re, s5_b_im, s5_c_re, s5_c_im, s5_d, s5_w_glu, s5_b_glu, ssd_conv_w, ssd_conv_b, ssd_dt_bias, ssd_a_log, ssd_d, ssd_norm_g, gdn_conv_w, gdn_dt_bias, gdn_a_log, gdn_norm_g, w_branch, b_gate, w_out, ln_g, ln_b)))
    m_loc = dict(zip(WEIGHTS, (m_meta, m_ln_in_g, m_ln_in_b, m_w_in, m_s5_a_re, m_s5_a_im, m_s5_log_step, m_s5_b_re, m_s5_b_im, m_s5_c_re, m_s5_c_im, m_s5_d, m_s5_w_glu, m_s5_b_glu, m_ssd_conv_w, m_ssd_conv_b, m_ssd_dt_bias, m_ssd_a_log, m_ssd_d, m_ssd_norm_g, m_gdn_conv_w, m_gdn_dt_bias, m_gdn_a_log, m_gdn_norm_g, m_w_branch, m_b_gate, m_w_out, m_ln_g, m_ln_b)))
    v_loc = dict(zip(WEIGHTS, (v_meta, v_ln_in_g, v_ln_in_b, v_w_in, v_s5_a_re, v_s5_a_im, v_s5_log_step, v_s5_b_re, v_s5_b_im, v_s5_c_re, v_s5_c_im, v_s5_d, v_s5_w_glu, v_s5_b_glu, v_ssd_conv_w, v_ssd_conv_b, v_ssd_dt_bias, v_ssd_a_log, v_ssd_d, v_ssd_norm_g, v_gdn_conv_w, v_gdn_dt_bias, v_gdn_a_log, v_gdn_norm_g, v_w_branch, v_b_gate, v_w_out, v_ln_g, v_ln_b)))
    return _step(x, loss_target, w_loc, m_loc, v_loc)
```

```python
import functools
import math

import jax
import jax.numpy as jnp
from jax import lax
from jax.experimental import pallas as pl
from jax.experimental.pallas import tpu as pltpu

F32 = jnp.float32
BF16 = jnp.bfloat16

N_DEV = 8
LANES = 128
SUBLANES = 8
VMEM_LIMIT = 56 * 1024 * 1024
FLAT_ROWS = 1024

CHUNK = 64
CONV_K = 4
S5_GROUP = 16
S5_STATE = 64
S5_BLOCK_GROUPS = 8
SSD_HEAD = 64
SSD_HEADS = 12
SSD_GROUPS = 2
SSD_STATE = 128
GDN_HEAD = 128
GDN_HEADS = 6
BRANCH = 768
LN_EPS = 1e-5

ADAM_LR = 0.001
ADAM_B1 = 0.9
ADAM_B2 = 0.999
ADAM_EPS = 1e-08
ADAM_WD = 0.01
ADAM_STEP = 10

WEIGHTS = ['meta', 'ln_in_g', 'ln_in_b', 'w_in', 's5_a_re', 's5_a_im', 's5_log_step', 's5_b_re', 's5_b_im',
           's5_c_re', 's5_c_im', 's5_d', 's5_w_glu', 's5_b_glu', 'ssd_conv_w', 'ssd_conv_b', 'ssd_dt_bias',
           'ssd_a_log', 'ssd_d', 'ssd_norm_g', 'gdn_conv_w', 'gdn_dt_bias', 'gdn_a_log', 'gdn_norm_g',
           'w_branch', 'b_gate', 'w_out', 'ln_g', 'ln_b']
SHARD_DIM = {'meta': 1, 'w_in': 2, 's5_w_glu': 1, 'ssd_conv_w': 2, 'gdn_conv_w': 2, 'w_branch': 3, 'b_gate': 2,
             'w_out': 1}


def _params(sem):
    return pltpu.CompilerParams(dimension_semantics=sem, vmem_limit_bytes=VMEM_LIMIT)


def _row_tile(m, cap):
    best = None
    for t in range(SUBLANES, min(m, cap) + 1, SUBLANES):
        if m % t == 0:
            best = t
    return best if best is not None else m


def _col_tile(n):
    for t in (512, 384, 256, 128):
        if n % t == 0:
            return t
    return n


def _any_tile(m, cap):
    best = 1
    for t in range(1, min(m, cap) + 1):
        if m % t == 0:
            best = t
    return best


def _mm_fwd(a, b, name):
    m, _ = a.shape
    g, k, n = b.shape
    tm, tn = _row_tile(m, 832), _col_tile(n)
    nj = n // tn

    def body(a_ref, b_ref, o_ref):
        o_ref[...] = jnp.dot(a_ref[...].astype(BF16), b_ref[0].astype(BF16), preferred_element_type=F32)

    return pl.pallas_call(
        body, name=name, grid=(g, m // tm, nj),
        in_specs=[pl.BlockSpec((tm, k), lambda gi, i, j: (i, gi)),
                  pl.BlockSpec((1, k, tn), lambda gi, i, j: (gi, 0, j))],
        out_specs=pl.BlockSpec((tm, tn), lambda gi, i, j: (i, gi * nj + j)),
        out_shape=jax.ShapeDtypeStruct((m, g * n), F32),
        compiler_params=_params(("arbitrary", "arbitrary", "arbitrary")),
    )(a, b)


def _mm_da(ct, b, name):
    m, _ = ct.shape
    g, k, n = b.shape
    tm, tk = _row_tile(m, 416), _col_tile(k)
    nk = k // tk

    def body(c_ref, b_ref, o_ref):
        o_ref[...] = lax.dot_general(c_ref[...].astype(BF16), b_ref[0].astype(BF16), (((1,), (1,)), ((), ())),
                                     preferred_element_type=F32)

    return pl.pallas_call(
        body, name=name, grid=(g, m // tm, nk),
        in_specs=[pl.BlockSpec((tm, n), lambda gi, i, j: (i, gi)),
                  pl.BlockSpec((1, tk, n), lambda gi, i, j: (gi, j, 0))],
        out_specs=pl.BlockSpec((tm, tk), lambda gi, i, j: (i, gi * nk + j)),
        out_shape=jax.ShapeDtypeStruct((m, g * k), F32),
        compiler_params=_params(("arbitrary", "arbitrary", "arbitrary")),
    )(ct, b)


def _mm_db(a, ct, g, k, n, name):
    m = a.shape[0]
    tm, tk, tn = _row_tile(m, 832), _col_tile(k), _col_tile(n)
    nk, nn = k // tk, n // tn

    def body(a_ref, c_ref, o_ref):
        @pl.when(pl.program_id(3) == 0)
        def _():
            o_ref[...] = jnp.zeros_like(o_ref)

        o_ref[0] += lax.dot_general(a_ref[...].astype(BF16), c_ref[...].astype(BF16), (((0,), (0,)), ((), ())),
                                    preferred_element_type=F32)

    return pl.pallas_call(
        body, name=name, grid=(g, nk, nn, m // tm),
        in_specs=[pl.BlockSpec((tm, tk), lambda gi, i, j, r: (r, gi * nk + i)),
                  pl.BlockSpec((tm, tn), lambda gi, i, j, r: (r, gi * nn + j))],
        out_specs=pl.BlockSpec((1, tk, tn), lambda gi, i, j, r: (gi, i, j)),
        out_shape=jax.ShapeDtypeStruct((g, k, n), F32),
        compiler_params=_params(("arbitrary", "arbitrary", "arbitrary", "arbitrary")),
    )(a, ct)


def _make_gmm(name):
    @jax.custom_vjp
    def gmm(a, b):
        return _mm_fwd(a, b, name + "_fwd")

    def fwd(a, b):
        return _mm_fwd(a, b, name + "_fwd"), (a, b)

    def bwd(res, ct):
        a, b = res
        g, k, n = b.shape
        return _mm_da(ct, b, name + "_da"), _mm_db(a, ct, g, k, n, name + "_db")

    gmm.defvjp(fwd, bwd)
    return gmm


def _mm(name, a, w, carrier):
    @jax.custom_vjp
    def mm(a, w3, carrier3):
        return _mm_fwd(a, w3, name + "_fwd")

    def fwd(a, w3, carrier3):
        return _mm_fwd(a, w3, name + "_fwd"), (a, w3)

    def bwd(res, ct):
        a, w3 = res
        g, k, n = w3.shape
        return _mm_da(ct, w3, name + "_da"), jnp.zeros_like(w3), _mm_db(a, ct, g, k, n, name + "_db")

    mm.defvjp(fwd, bwd)
    return mm(a, w[None], carrier[None])


def _make_rowwise(name, fn, n_row, n_par, out_widths, tm_cap, use_ridx=False):
    n_out = len(out_widths)

    def bind(tm):
        if not use_ridx:
            return fn
        ridx = pl.program_id(0) * tm + lax.broadcasted_iota(jnp.int32, (tm, 1), 0)
        return functools.partial(fn, ridx)

    def specs(args, tm):
        rows = [pl.BlockSpec((tm, a.shape[1]), lambda i: (i, 0)) for a in args[:n_row]]
        pars = [pl.BlockSpec(a.shape, lambda i: (0, 0)) for a in args[n_row:]]
        return rows, pars

    def fwd_call(*args):
        t = args[0].shape[0]
        tm = _row_tile(t, tm_cap)
        rows, pars = specs(args, tm)

        def body(*refs):
            vals = [r[...] for r in refs[:n_row + n_par]]
            res = bind(tm)(*vals)
            for o_ref, r in zip(refs[n_row + n_par:], res):
                o_ref[...] = r

        outs = pl.pallas_call(
            body, name=name + "_fwd", grid=(t // tm,), in_specs=rows + pars,
            out_specs=[pl.BlockSpec((tm, w), lambda i: (i, 0)) for w in out_widths],
            out_shape=[jax.ShapeDtypeStruct((t, w), F32) for w in out_widths],
            compiler_params=_params(("arbitrary",)),
        )(*args)
        return tuple(outs)

    def bwd_call(args, cts):
        t = args[0].shape[0]
        tm = _row_tile(t, tm_cap)
        rows, pars = specs(args, tm)
        ct_specs = [pl.BlockSpec((tm, w), lambda i: (i, 0)) for w in out_widths]
        n_in = n_row + n_par

        def body(*refs):
            vals = [r[...] for r in refs[:n_in]]
            ct_vals = tuple(r[...] for r in refs[n_in:n_in + n_out])
            d_refs = refs[n_in + n_out:]
            f = bind(tm)
            _, vjp = jax.vjp(lambda *a: tuple(f(*a)), *vals)
            grads = vjp(ct_vals)
            for i in range(n_row):
                d_refs[i][...] = grads[i]
            if n_par:
                @pl.when(pl.program_id(0) == 0)
                def _():
                    for j in range(n_par):
                        d_refs[n_row + j][...] = jnp.zeros_like(d_refs[n_row + j])

                for j in range(n_par):
                    d_refs[n_row + j][...] += grads[n_row + j]

        outs = pl.pallas_call(
            body, name=name + "_bwd", grid=(t // tm,), in_specs=rows + pars + ct_specs,
            out_specs=rows + pars,
            out_shape=[jax.ShapeDtypeStruct(a.shape, F32) for a in args],
            compiler_params=_params(("arbitrary",)),
        )(*args, *cts)
        return tuple(outs)

    @jax.custom_vjp
    def op(*args):
        return fwd_call(*args)

    def fwd(*args):
        return fwd_call(*args), args

    def bwd(args, cts):
        return bwd_call(args, cts)

    op.defvjp(fwd, bwd)
    return op


def _make_chunk_scan(name, fn, state_shape, n_seq, n_par, out_widths):
    n_out = len(out_widths)
    zeros_idx = (0,) * len(state_shape)

    def fwd_call(*args):
        t = args[0].shape[0]
        nc = t // CHUNK
        seq_specs = [pl.BlockSpec((CHUNK, a.shape[1]), lambda c: (c, 0)) for a in args[:n_seq]]
        par_specs = [pl.BlockSpec(a.shape, lambda c: (0, 0)) for a in args[n_seq:]]

        def body(*refs):
            ins = refs[:n_seq + n_par]
            out_refs = refs[n_seq + n_par:n_seq + n_par + n_out]
            states_ref = refs[n_seq + n_par + n_out]
            st = refs[-1]

            @pl.when(pl.program_id(0) == 0)
            def _():
                st[...] = jnp.zeros_like(st)

            s0 = st[...]
            states_ref[0] = s0
            res = fn(s0, *[r[...] for r in ins])
            st[...] = res[0]
            for o_ref, r in zip(out_refs, res[1:]):
                o_ref[...] = r

        outs = pl.pallas_call(
            body, name=name + "_fwd", grid=(nc,), in_specs=seq_specs + par_specs,
            out_specs=[pl.BlockSpec((CHUNK, w), lambda c: (c, 0)) for w in out_widths]
            + [pl.BlockSpec((1,) + state_shape, lambda c: (c,) + zeros_idx)],
            out_shape=[jax.ShapeDtypeStruct((t, w), F32) for w in out_widths]
            + [jax.ShapeDtypeStruct((nc,) + state_shape, F32)],
            scratch_shapes=[pltpu.VMEM(state_shape, F32)],
            compiler_params=_params(("arbitrary",)),
        )(*args)
        return tuple(outs[:n_out]), outs[n_out]

    def bwd_call(args, states, cts):
        t = args[0].shape[0]
        nc = t // CHUNK
        rev = lambda c: (nc - 1 - c, 0)
        seq_specs = [pl.BlockSpec((CHUNK, a.shape[1]), rev) for a in args[:n_seq]]
        par_specs = [pl.BlockSpec(a.shape, lambda c: (0, 0)) for a in args[n_seq:]]
        ct_specs = [pl.BlockSpec((CHUNK, w), rev) for w in out_widths]
        st_spec = pl.BlockSpec((1,) + state_shape, lambda c: (nc - 1 - c,) + zeros_idx)
        n_in = n_seq + n_par

        def body(*refs):
            vals = [r[...] for r in refs[:n_in]]
            s0 = refs[n_in][0]
            ct_vals = tuple(r[...] for r in refs[n_in + 1:n_in + 1 + n_out])
            d_refs = refs[n_in + 1 + n_out:-1]
            dst = refs[-1]

            @pl.when(pl.program_id(0) == 0)
            def _():
                dst[...] = jnp.zeros_like(dst)
                for j in range(n_par):
                    d_refs[n_seq + j][...] = jnp.zeros_like(d_refs[n_seq + j])

            _, vjp = jax.vjp(lambda *a: tuple(fn(*a)), s0, *vals)
            grads = vjp((dst[...],) + ct_vals)
            dst[...] = grads[0]
            for i in range(n_seq):
                d_refs[i][...] = grads[1 + i]
            for j in range(n_par):
                d_refs[n_seq + j][...] += grads[1 + n_seq + j]

        outs = pl.pallas_call(
            body, name=name + "_bwd", grid=(nc,), in_specs=seq_specs + par_specs + [st_spec] + ct_specs,
            out_specs=seq_specs + par_specs,
            out_shape=[jax.ShapeDtypeStruct(a.shape, F32) for a in args],
            scratch_shapes=[pltpu.VMEM(state_shape, F32)],
            compiler_params=_params(("arbitrary",)),
        )(*args, states, *cts)
        return tuple(outs)

    @jax.custom_vjp
    def op(*args):
        return fwd_call(*args)[0]

    def fwd(*args):
        outs, states = fwd_call(*args)
        return outs, (args, states)

    def bwd(res, cts):
        args, states = res
        return bwd_call(args, states, cts)

    op.defvjp(fwd, bwd)
    return op


def _s5_scan_fwd(bre, bim, are, aim, name):
    t, r, _ = bre.shape
    tb = _any_tile(t, 208)
    blk = pl.BlockSpec((tb, r, LANES), lambda i: (i, 0, 0))
    par = pl.BlockSpec((r, LANES), lambda i: (0, 0))

    def body(bre_ref, bim_ref, are_ref, aim_ref, sre_ref, sim_ref, st):
        @pl.when(pl.program_id(0) == 0)
        def _():
            st[...] = jnp.zeros_like(st)

        ar, ai = are_ref[...], aim_ref[...]

        def step(k, carry):
            sr, si = carry
            nr = ar * sr - ai * si + bre_ref[k]
            ni = ar * si + ai * sr + bim_ref[k]
            sre_ref[k] = nr
            sim_ref[k] = ni
            return nr, ni

        sr, si = lax.fori_loop(0, tb, step, (st[0], st[1]), unroll=4)
        st[0] = sr
        st[1] = si

    return pl.pallas_call(
        body, name=name, grid=(t // tb,), in_specs=[blk, blk, par, par], out_specs=[blk, blk],
        out_shape=[jax.ShapeDtypeStruct(bre.shape, F32)] * 2,
        scratch_shapes=[pltpu.VMEM((2, r, LANES), F32)],
        compiler_params=_params(("arbitrary",)),
    )(bre, bim, are, aim)


def _s5_scan_bwd(dsr, dsi, sre, sim, are, aim, name):
    t, r, _ = sre.shape
    tb = _any_tile(t, 208)
    nb = t // tb
    blk = pl.BlockSpec((tb, r, LANES), lambda i: (nb - 1 - i, 0, 0))
    par = pl.BlockSpec((r, LANES), lambda i: (0, 0))

    def body(dsr_ref, dsi_ref, sre_ref, sim_ref, are_ref, aim_ref, gre_ref, gim_ref, dar_ref, dai_ref, st):
        @pl.when(pl.program_id(0) == 0)
        def _():
            st[...] = jnp.zeros_like(st)
            dar_ref[...] = jnp.zeros_like(dar_ref)
            dai_ref[...] = jnp.zeros_like(dai_ref)

        ar, ai = are_ref[...], aim_ref[...]

        def step(k, carry):
            gr, gi, dar, dai = carry
            q = tb - 1 - k
            s_r, s_i = sre_ref[q], sim_ref[q]
            dar = dar + gr * s_r + gi * s_i
            dai = dai + gi * s_r - gr * s_i
            ngr = dsr_ref[q] + ar * gr + ai * gi
            ngi = dsi_ref[q] + ar * gi - ai * gr
            gre_ref[q] = ngr
            gim_ref[q] = ngi
            return ngr, ngi, dar, dai

        gr, gi, dar, dai = lax.fori_loop(0, tb, step, (st[0], st[1], dar_ref[...], dai_ref[...]), unroll=4)
        st[0] = gr
        st[1] = gi
        dar_ref[...] = dar
        dai_ref[...] = dai

    return pl.pallas_call(
        body, name=name, grid=(nb,), in_specs=[blk, blk, blk, blk, par, par], out_specs=[blk, blk, par, par],
        out_shape=[jax.ShapeDtypeStruct(sre.shape, F32)] * 2 + [jax.ShapeDtypeStruct(are.shape, F32)] * 2,
        scratch_shapes=[pltpu.VMEM((2, r, LANES), F32)],
        compiler_params=_params(("arbitrary",)),
    )(dsr, dsi, sre, sim, are, aim)


def _make_s5_scan(name):
    @jax.custom_vjp
    def scan(bre, bim, are, aim):
        return tuple(_s5_scan_fwd(bre, bim, are, aim, name + "_fwd"))

    def fwd(bre, bim, are, aim):
        sre, sim = _s5_scan_fwd(bre, bim, are, aim, name + "_fwd")
        return (sre, sim), (sre, sim, are, aim)

    def bwd(res, cts):
        sre, sim, are, aim = res
        return tuple(_s5_scan_bwd(cts[0], cts[1], sre, sim, are, aim, name + "_bwd"))

    scan.defvjp(fwd, bwd)
    return scan


def _dot(a, b):
    return jnp.dot(a.astype(BF16), b.astype(BF16), preferred_element_type=F32)


def _dot_nt(a, b):
    return lax.dot_general(a.astype(BF16), b.astype(BF16), (((1,), (1,)), ((), ())), preferred_element_type=F32)


def _dot_tn(a, b):
    return lax.dot_general(a.astype(BF16), b.astype(BF16), (((0,), (0,)), ((), ())), preferred_element_type=F32)


def _dot_f32(a, b):
    return jnp.dot(a, b, precision=lax.Precision.HIGHEST, preferred_element_type=F32)


def _dot_tn_f32(a, b):
    return lax.dot_general(a, b, (((0,), (0,)), ((), ())), precision=lax.Precision.HIGHEST,
                           preferred_element_type=F32)


def _iota(shape, dim):
    return lax.broadcasted_iota(jnp.int32, shape, dim)


def _tri(strict=False):
    r, c = _iota((CHUNK, CHUNK), 0), _iota((CHUNK, CHUNK), 1)
    return (r > c) if strict else (r >= c)


def _silu(x):
    return x * jax.nn.sigmoid(x)


def _layer_norm(z, g, b):
    mu = jnp.mean(z, axis=-1, keepdims=True)
    var = jnp.mean(jnp.square(z - mu), axis=-1, keepdims=True)
    return (z - mu) * lax.rsqrt(var + LN_EPS) * g + b


def _ssd_chunk(state, xbc, sm, a_c, d_exp):
    width = SSD_HEADS * SSD_HEAD
    x = xbc[:, :width]
    lane = _iota((CHUNK, LANES), 1)
    dtc = jnp.where(lane < SSD_HEADS, sm, 0.0)
    low = _tri().astype(F32)
    eye = (_iota((CHUNK, CHUNK), 0) == _iota((CHUNK, CHUNK), 1)).astype(F32)
    head_col, head_row = _iota((LANES, width), 1), _iota((LANES, width), 0) * SSD_HEAD
    expand = ((head_col >= head_row) & (head_col < head_row + SSD_HEAD)).astype(F32)
    acum_c = _dot_f32(low, dtc * a_c)
    acum_ct = _dot_tn_f32(acum_c, eye)
    dt_exp = _dot_f32(dtc, expand)
    acum = _dot_f32(acum_c, expand)
    xd = x * dt_exp
    last = acum[CHUNK - 1:CHUNK, :]
    to_end = jnp.exp(last - acum)
    eac = jnp.exp(acum)
    causal = _tri()
    first_half = _iota((CHUNK, LANES), 1) < SSD_HEAD
    top_rows = _iota((LANES, LANES), 0) < SSD_HEAD
    pairs = range(SSD_HEADS // 2)
    grp = [(2 * p) // (SSD_HEADS // SSD_GROUPS) for p in pairs]
    cols = [slice(p * LANES, (p + 1) * LANES) for p in pairs]
    bg = [xbc[:, width + g * SSD_STATE: width + (g + 1) * SSD_STATE] for g in range(SSD_GROUPS)]
    cg = [xbc[:, width + (SSD_GROUPS + g) * SSD_STATE: width + (SSD_GROUPS + g + 1) * SSD_STATE]
          for g in range(SSD_GROUPS)]
    scores = [_dot_nt(cg[g], bg[g]) for g in range(SSD_GROUPS)]
    dec = [jnp.where(causal, jnp.exp(jnp.minimum(acum_c[:, h:h + 1] - acum_ct[h:h + 1, :], 0.0)), 0.0)
           for h in range(SSD_HEADS)]
    y_lo = [_dot(scores[grp[p]] * dec[2 * p], jnp.where(first_half, xd[:, cols[p]], 0.0)) for p in pairs]
    y_hi = [_dot(scores[grp[p]] * dec[2 * p + 1], jnp.where(first_half, 0.0, xd[:, cols[p]])) for p in pairs]
    s_prev = [state[p * LANES:(p + 1) * LANES, :] for p in pairs]
    y_off = [_dot_nt(cg[grp[p]], s_prev[p]) for p in pairs]
    s_add = [_dot_tn(xd[:, cols[p]] * to_end[:, cols[p]], bg[grp[p]]) for p in pairs]
    ys = [y_lo[p] + y_hi[p] + y_off[p] * eac[:, cols[p]] + x[:, cols[p]] * d_exp[:, cols[p]] for p in pairs]
    cd = [jnp.where(top_rows, jnp.exp(acum_c[CHUNK - 1:CHUNK, 2 * p:2 * p + 1]),
                    jnp.exp(acum_c[CHUNK - 1:CHUNK, 2 * p + 1:2 * p + 2])) for p in pairs]
    new_states = [s_prev[p] * cd[p] + s_add[p] for p in pairs]
    return jnp.concatenate(new_states, axis=0), jnp.concatenate(ys, axis=1)


def _gdn_chunk(state, qkv, sm):
    width = GDN_HEADS * GDN_HEAD
    g0, b0 = SSD_HEADS, SSD_HEADS + GDN_HEADS
    lane = _iota((CHUNK, LANES), 1)
    gc = jnp.where((lane >= g0) & (lane < b0), sm, 0.0)
    low = _tri().astype(F32)
    eye = (_iota((CHUNK, CHUNK), 0) == _iota((CHUNK, CHUNK), 1)).astype(F32)
    gcum = _dot_f32(low, gc)
    gcum_t = _dot_tn_f32(gcum, eye)
    causal, strict = _tri(), _tri(strict=True)
    heads = range(GDN_HEADS)
    q = [qkv[:, h * GDN_HEAD:(h + 1) * GDN_HEAD] for h in heads]
    k = [qkv[:, width + h * GDN_HEAD: width + (h + 1) * GDN_HEAD] for h in heads]
    v = [qkv[:, 2 * width + h * GDN_HEAD: 2 * width + (h + 1) * GDN_HEAD] for h in heads]
    beta = [sm[:, b0 + h:b0 + h + 1] for h in heads]
    gcol = [gcum[:, g0 + h:g0 + h + 1] for h in heads]
    glast = [gcum[CHUNK - 1:CHUNK, g0 + h:g0 + h + 1] for h in heads]
    gamma = [jnp.where(causal, jnp.exp(jnp.minimum(gcol[h] - gcum_t[g0 + h:g0 + h + 1, :], 0.0)), 0.0) for h in heads]
    kk = [_dot_nt(k[h], k[h]) for h in heads]
    qk = [_dot_nt(q[h], k[h]) for h in heads]
    egc = [jnp.exp(gcol[h]) for h in heads]
    nmat = [-jnp.where(strict, kk[h] * gamma[h] * beta[h], 0.0) for h in heads]
    sol = [jnp.concatenate([v[h] * beta[h], k[h] * (beta[h] * egc[h])], axis=1) for h in heads]
    for i in range(6):
        upd = [_dot(nmat[h], sol[h]) for h in heads]
        if i < 5:
            nmat = [_dot(nmat[h], nmat[h]) for h in heads]
        sol = [sol[h] + upd[h] for h in heads]
    s_prev = [state[h * GDN_HEAD:(h + 1) * GDN_HEAD, :] for h in heads]
    w_s = [_dot(sol[h][:, GDN_HEAD:], s_prev[h]) for h in heads]
    q_s = [_dot(q[h] * egc[h], s_prev[h]) for h in heads]
    v_new = [sol[h][:, :GDN_HEAD] - w_s[h] for h in heads]
    a_v = [_dot(qk[h] * gamma[h], v_new[h]) for h in heads]
    k_v = [_dot_tn(k[h] * jnp.exp(glast[h] - gcol[h]), v_new[h]) for h in heads]
    outs = [q_s[h] + a_v[h] for h in heads]
    new_states = [s_prev[h] * jnp.exp(glast[h]) + k_v[h] for h in heads]
    return jnp.concatenate(new_states, axis=0), jnp.concatenate(outs, axis=1)


def _row_fns(d_model, pad_rows, loss_rows, alpha):
    def keep(ridx, v):
        return jnp.where(ridx >= pad_rows, v, 0.0)

    def ln_in(ridx, h, g, b):
        return (keep(ridx, _layer_norm(h, g, b)),)

    def ln_layer(ridx, h, o, g, b):
        return (keep(ridx, _layer_norm(alpha * h + o, g, b)),)

    def s5_act(y_re, y_im, u, d):
        return (jax.nn.gelu(y_re + y_im + d * u),)

    def s5_glu(v0, lin, z, bias):
        return (v0 * jax.nn.sigmoid(lin + bias) * _silu(z),)

    def small_act(ridx, raw, bias, scale):
        lane = _iota(raw.shape, 1)
        sp = jax.nn.softplus(raw + bias)
        g0, b0 = SSD_HEADS, SSD_HEADS + GDN_HEADS
        out = jnp.where(lane < g0, sp, jnp.where(lane < b0, scale * sp,
                                                 jnp.where(lane < b0 + GDN_HEADS, jax.nn.sigmoid(raw), 0.0)))
        return (keep(ridx, out),)

    def conv(xs, w):
        acc = xs[0] * w[0:1, :]
        for j in range(1, CONV_K):
            acc = acc + xs[j] * w[j:j + 1, :]
        return acc

    def ssd_conv(ridx, x0, x1, x2, x3, w, b):
        return (keep(ridx, _silu(conv((x0, x1, x2, x3), w) + b)),)

    def ssd_norm(y, z, g):
        v = y * _silu(z)
        return (v * lax.rsqrt(jnp.mean(v * v, axis=-1, keepdims=True) + LN_EPS) * g,)

    def gdn_conv(x0, x1, x2, x3, w):
        a = _silu(conv((x0, x1, x2, x3), w))
        width = GDN_HEADS * GDN_HEAD
        parts = []
        for h in range(2 * GDN_HEADS):
            z = a[:, h * GDN_HEAD:(h + 1) * GDN_HEAD]
            z = z * lax.rsqrt(jnp.sum(z * z, axis=-1, keepdims=True) + 1e-6)
            parts.append(z * GDN_HEAD ** -0.5 if h < GDN_HEADS else z)
        parts.append(a[:, 2 * width:])
        return (jnp.concatenate(parts, axis=1),)

    def gdn_norm(o, z, g):
        parts = []
        for h in range(GDN_HEADS):
            cols = slice(h * GDN_HEAD, (h + 1) * GDN_HEAD)
            oh = o[:, cols]
            oh = oh * lax.rsqrt(jnp.mean(oh * oh, axis=-1, keepdims=True) + LN_EPS) * g
            parts.append(oh * _silu(z[:, cols]))
        return (jnp.concatenate(parts, axis=1),)

    def merge(oa, ob, oc, gate, bias):
        acc = None
        for k, o in enumerate((oa, ob, oc)):
            cols = slice(k * d_model, (k + 1) * d_model)
            term = jax.nn.sigmoid(gate[:, cols] + bias[:, cols]) * o
            acc = term if acc is None else acc + term
        return (acc,)

    def loss_rows_fn(ridx, h, tgt):
        row = 0.5 * jnp.mean(jnp.square(h - tgt), axis=-1, keepdims=True)
        row = jnp.where(ridx >= loss_rows, row, 0.0)
        lane = _iota((h.shape[0], LANES), 1)
        return (jnp.where(lane == 0, row, 0.0),)

    return dict(ln_in=ln_in, ln_layer=ln_layer, s5_act=s5_act, s5_glu=s5_glu, small_act=small_act,
                ssd_conv=ssd_conv, ssd_norm=ssd_norm, gdn_conv=gdn_conv, gdn_norm=gdn_norm, merge=merge,
                loss=loss_rows_fn)


def _exchange(srcs, gather, name):
    n_arr = len(srcs)
    n_peer = N_DEV - 1
    flips = [(fx, fy, fc) for fx in (0, 1) for fy in (0, 1) for fc in (0, 1)][1:]
    blocks = [s.shape if gather else s.shape[1:] for s in srcs]

    def body(*refs):
        src_refs, out_refs = refs[:n_arr], refs[n_arr:2 * n_arr]
        send_sems, recv_sems, local_sems = refs[2 * n_arr:]
        x, y, c = lax.axis_index("x"), lax.axis_index("y"), lax.axis_index("c")
        me = 4 * x + 2 * y + c

        def pick(a, j):
            return src_refs[a] if gather else src_refs[a].at[j]

        own = [pltpu.make_async_copy(pick(a, me), out_refs[a].at[me], local_sems.at[a]) for a in range(n_arr)]
        for cp in own:
            cp.start()
        copies = []
        for k, (fx, fy, fc) in enumerate(flips):
            px = 1 - x if fx else x
            py = 1 - y if fy else y
            pc = 1 - c if fc else c
            peer = 4 * px + 2 * py + pc
            for a in range(n_arr):
                cp = pltpu.make_async_remote_copy(
                    src_ref=pick(a, peer), dst_ref=out_refs[a].at[me],
                    send_sem=send_sems.at[a * n_peer + k], recv_sem=recv_sems.at[a * n_peer + k],
                    device_id=(px, py, pc), device_id_type=pl.DeviceIdType.MESH)
                cp.start()
                copies.append(cp)
        for cp in copies:
            cp.wait()
        for cp in own:
            cp.wait()

    return pl.pallas_call(
        body, name=name,
        in_specs=[pl.BlockSpec(memory_space=pl.ANY)] * n_arr, out_specs=[pl.BlockSpec(memory_space=pl.ANY)] * n_arr,
        out_shape=[jax.ShapeDtypeStruct((N_DEV,) + tuple(b), s.dtype) for b, s in zip(blocks, srcs)],
        scratch_shapes=[pltpu.SemaphoreType.DMA((n_arr * n_peer,)), pltpu.SemaphoreType.DMA((n_arr * n_peer,)),
                        pltpu.SemaphoreType.DMA((n_arr,))],
    )(*srcs)


def _adamw_body(p_ref, w_ref, m_ref, v_ref, g_ref, d_ref, nm_ref, nv_ref):
    bc1 = 1.0 - ADAM_B1 ** ADAM_STEP
    bc2 = 1.0 - ADAM_B2 ** ADAM_STEP
    g = p_ref[0].astype(F32)
    for k in range(1, N_DEV):
        g = g + p_ref[k].astype(F32)
    nm = ADAM_B1 * m_ref[...] + (1.0 - ADAM_B1) * g
    nv = ADAM_B2 * v_ref[...] + (1.0 - ADAM_B2) * jnp.square(g)
    m_hat = nm / bc1
    v_hat = nv / bc2
    g_ref[...] = g
    d_ref[...] = -ADAM_LR * (m_hat / (jnp.sqrt(v_hat) + ADAM_EPS) + ADAM_WD * w_ref[...])
    nm_ref[...] = nm
    nv_ref[...] = nv


def _adamw_flat(parts, w, m, v, name):
    rows = w.shape[0]
    tr = FLAT_ROWS
    blk = pl.BlockSpec((tr, LANES), lambda i: (i, 0))
    return pl.pallas_call(
        functools.partial(_adamw_body), name=name, grid=(rows // tr,),
        in_specs=[pl.BlockSpec((N_DEV, tr, LANES), lambda i: (0, i, 0)), blk, blk, blk],
        out_specs=[blk] * 4, out_shape=[jax.ShapeDtypeStruct((rows, LANES), F32)] * 4,
        compiler_params=_params(("arbitrary",)),
    )(parts, w, m, v)


def _adamw_stacked(parts, w, m, v):
    depth, d, n = w.shape
    tr = _row_tile(d, 128)
    blk = pl.BlockSpec((1, tr, n), lambda l, i: (l, i, 0))
    return pl.pallas_call(
        functools.partial(_adamw_body), name="adamw_w_in", grid=(depth, d // tr),
        in_specs=[pl.BlockSpec((N_DEV, 1, tr, n), lambda l, i: (0, l, i, 0)), blk, blk, blk],
        out_specs=[blk] * 4, out_shape=[jax.ShapeDtypeStruct(w.shape, F32)] * 4,
        compiler_params=_params(("arbitrary", "arbitrary")),
    )(parts, w, m, v)


def _pad_flat(vec, rows):
    return jnp.pad(vec, (0, rows * LANES - vec.shape[0])).reshape(rows, LANES)


def _rows_for(n):
    return -(-n // (FLAT_ROWS * LANES)) * FLAT_ROWS


def _split_shards(full, dim):
    shp = full.shape
    parts = full.reshape(shp[:dim] + (N_DEV, shp[dim] // N_DEV) + shp[dim + 1:])
    return jnp.moveaxis(parts, dim, 0).reshape(N_DEV, -1)


def _join_shards(rows, local_shape, dim):
    parts = jnp.moveaxis(rows.reshape((N_DEV,) + tuple(local_shape)), 0, dim)
    shp = tuple(local_shape)
    return parts.reshape(shp[:dim] + (N_DEV * shp[dim],) + shp[dim + 1:])


def _s5_tables(a_re, a_im, log_step, b_re, b_im, c_re, c_im):
    lam_re = jnp.minimum(a_re, -1e-4)
    lam_im = a_im
    step = jnp.exp(log_step)[:, None]
    mag = jnp.exp(lam_re * step)
    abar_re, abar_im = mag * jnp.cos(lam_im * step), mag * jnp.sin(lam_im * step)
    den = lam_re * lam_re + lam_im * lam_im
    nr, ni = abar_re - 1.0, abar_im
    coef_re = (nr * lam_re + ni * lam_im) / den
    coef_im = (ni * lam_re - nr * lam_im) / den
    bbar_re = coef_re[..., None] * b_re - coef_im[..., None] * b_im
    bbar_im = coef_re[..., None] * b_im + coef_im[..., None] * b_re
    groups = a_re.shape[0]
    nblk = groups // S5_BLOCK_GROUPS
    eye = jnp.eye(S5_BLOCK_GROUPS, dtype=F32)

    def in_blocks(bb):
        t = jnp.swapaxes(bb, 1, 2).reshape(nblk, S5_BLOCK_GROUPS, S5_GROUP, S5_STATE)
        blk = jnp.einsum('ab,jacp->jacbp', eye, t)
        return blk.reshape(nblk, S5_BLOCK_GROUPS * S5_GROUP, S5_BLOCK_GROUPS * S5_STATE)

    def out_blocks(cc):
        t = jnp.swapaxes(cc, 1, 2).reshape(nblk, S5_BLOCK_GROUPS, S5_STATE, S5_GROUP)
        blk = jnp.einsum('ab,japc->japbc', eye, t)
        return blk.reshape(nblk, S5_BLOCK_GROUPS * S5_STATE, S5_BLOCK_GROUPS * S5_GROUP)

    rows = groups * S5_STATE // LANES
    return dict(b_re=in_blocks(bbar_re), b_im=in_blocks(bbar_im), c_re=out_blocks(c_re), c_im=out_blocks(-c_im),
                a_re=abar_re.reshape(rows, LANES), a_im=abar_im.reshape(rows, LANES))


def _shifted(x):
    t = x.shape[0]
    return [jnp.pad(x, ((CONV_K - 1 - j, 0), (0, 0)))[:t] for j in range(CONV_K)]


def _in_widths(d_model):
    return [BRANCH, BRANCH, SSD_HEADS * SSD_HEAD + 2 * SSD_GROUPS * SSD_STATE, SSD_HEADS, BRANCH,
            3 * BRANCH, GDN_HEADS, GDN_HEADS, BRANCH, 3 * d_model]


def _local_loss(w, mats, x, target):
    n_meta, d_model = w['meta'].shape
    depth = len(mats['in'])
    seq = x.shape[0]
    pad_rows = CHUNK - n_meta
    first = pad_rows + n_meta
    t_all = first + seq
    alpha = (2 * depth) ** 0.25
    fns = _row_fns(d_model, pad_rows, first, alpha)
    row = lambda nm, key, n_row, n_par, widths, cap, ridx=False: _make_rowwise(nm, fns[key], n_row, n_par, widths, cap, ridx)

    h = jnp.concatenate([jnp.zeros((pad_rows, d_model), F32), w['meta'], x], axis=0)
    (h,) = row("ln_in", 'ln_in', 1, 2, [d_model], 416, True)(h, w['ln_in_g'][None], w['ln_in_b'][None])

    n_small = SSD_HEADS + 2 * GDN_HEADS

    def small_cols(ps):
        return jnp.pad(jnp.concatenate([ps[3], ps[6], ps[7]], axis=1), ((0, 0), (0, LANES - n_small)))

    for l in range(depth):
        pw, pc = mats['in'][l], w['c_in'][l]
        s5_u = _mm("in_s5u", h, pw[0], pc[0])
        s5_z = _mm("in_s5z", h, pw[1], pc[1])
        ssd_xbc = _mm("in_ssdx", h, pw[2], pc[2])
        ssd_z = _mm("in_ssdz", h, pw[4], pc[4])
        gdn_qkv = _mm("in_gdnq", h, pw[5], pc[5])
        gdn_z = _mm("in_gdnz", h, pw[8], pc[8])
        gate = _mm("in_gate", h, pw[9], pc[9])
        small = _mm("in_small", h, small_cols(pw), small_cols(pc))

        zeros_tail = jnp.zeros((LANES - n_small,), F32)
        bias = jnp.concatenate([w['ssd_dt_bias'][l], w['gdn_dt_bias'][l], jnp.zeros((GDN_HEADS,), F32), zeros_tail])[None]
        scale = jnp.concatenate([jnp.ones((SSD_HEADS,), F32), -jnp.exp(w['gdn_a_log'][l]),
                                 jnp.zeros((GDN_HEADS,), F32), zeros_tail])[None]
        (sm,) = row("small_act", 'small_act', 1, 2, [LANES], 832, True)(small, bias, scale)

        tb = _s5_tables(w['s5_a_re'][l], w['s5_a_im'][l], w['s5_log_step'][l], w['s5_b_re'][l], w['s5_b_im'][l],
                        w['s5_c_re'][l], w['s5_c_im'][l])
        srows = tb['a_re'].shape[0]
        bu_re = _make_gmm("s5_bre")(s5_u, tb['b_re']).reshape(t_all, srows, LANES)
        bu_im = _make_gmm("s5_bim")(s5_u, tb['b_im']).reshape(t_all, srows, LANES)
        s_re, s_im = _make_s5_scan("s5_scan")(bu_re, bu_im, tb['a_re'], tb['a_im'])
        y_re = _make_gmm("s5_cre")(s_re.reshape(t_all, srows * LANES), tb['c_re'])
        y_im = _make_gmm("s5_cim")(s_im.reshape(t_all, srows * LANES), tb['c_im'])
        (v0,) = row("s5_act", 's5_act', 3, 1, [BRANCH], 416)(y_re, y_im, s5_u, w['s5_d'][l][None])
        lin = _mm("s5_glu_mm", v0, mats['glu'][l], w['c_glu'][l])
        (y_a,) = row("s5_glu", 's5_glu', 3, 1, [BRANCH], 416)(v0, lin, s5_z, w['s5_b_glu'][l][None])

        (xbc,) = row("ssd_conv", 'ssd_conv', 4, 2, [ssd_xbc.shape[1]], 208, True)(
            *_shifted(ssd_xbc), w['ssd_conv_w'][l], w['ssd_conv_b'][l][None])
        a_c = jnp.pad(-jnp.exp(w['ssd_a_log'][l]), (0, LANES - SSD_HEADS))[None]
        d_exp = jnp.repeat(w['ssd_d'][l], SSD_HEAD)[None]
        (y_ssd,) = _make_chunk_scan("ssd_scan", _ssd_chunk, (SSD_HEADS // 2 * LANES, SSD_STATE), 2, 2, [BRANCH])(
            xbc, sm, a_c, d_exp)
        (y_b,) = row("ssd_norm", 'ssd_norm', 2, 1, [BRANCH], 416)(y_ssd, ssd_z, w['ssd_norm_g'][l][None])

        (qkv,) = row("gdn_conv", 'gdn_conv', 4, 1, [3 * BRANCH], 104)(*_shifted(gdn_qkv), w['gdn_conv_w'][l])
        (o_gdn,) = _make_chunk_scan("gdn_scan", _gdn_chunk, (GDN_HEADS * GDN_HEAD, GDN_HEAD), 2, 0, [BRANCH])(qkv, sm)
        (y_c,) = row("gdn_norm", 'gdn_norm', 2, 1, [BRANCH], 416)(o_gdn, gdn_z, w['gdn_norm_g'][l][None])

        outs = [_mm(f"branch{k}", yk, mats['branch'][l, k], w['c_branch'][l, k])
                for k, yk in enumerate((y_a, y_b, y_c))]
        (merged,) = row("merge", 'merge', 4, 1, [d_model], 208)(*outs, gate, w['b_gate'][l].reshape(1, 3 * d_model))
        o = _mm("w_out", merged, mats['out'][l], w['c_out'][l])
        (h,) = row("ln_layer", 'ln_layer', 2, 2, [d_model], 416, True)(h, o, w['ln_g'][l][None], w['ln_b'][l][None])

    tgt = jnp.concatenate([jnp.zeros((first, d_model), F32), target], axis=0)
    (rows_loss,) = row("loss", 'loss', 2, 0, [LANES], 416, True)(h, tgt)
    return jnp.sum(rows_loss)


def _in_overlaps(d_model, n_loc):
    offs = [0]
    for wd in _in_widths(d_model):
        offs.append(offs[-1] + wd)
    out = []
    for i in range(len(offs) - 1):
        c0, c1 = offs[i], offs[i + 1]
        segs = []
        for k in range(N_DEV):
            g0, g1 = max(c0, k * n_loc), min(c1, (k + 1) * n_loc)
            if g0 < g1:
                segs.append((k, g0 - k * n_loc, g1 - k * n_loc, g0 - c0))
        out.append(segs)
    return out


MAT_WEIGHTS = ['s5_w_glu', 'w_branch', 'w_out']


def _step(x, target, w_loc, m_loc, v_loc):
    rest = [n for n in WEIGHTS if n in SHARD_DIM and n != 'w_in']
    small = [n for n in rest if n not in MAT_WEIGHTS]
    repl = [n for n in WEIGHTS if n not in SHARD_DIM]
    size = lambda names: sum(int(w_loc[n].size) for n in names)
    rows_mat, rows_small = _rows_for(size(MAT_WEIGHTS)), _rows_for(size(small))
    n_rp = size(repl) + 1
    rows_rp = _rows_for(n_rp)
    depth, d_model, n_loc = w_loc['w_in'].shape
    overlaps = _in_overlaps(d_model, n_loc)

    mat_flat = _pad_flat(jnp.concatenate([w_loc[n].reshape(-1) for n in MAT_WEIGHTS]), rows_mat).astype(BF16)
    small_flat = _pad_flat(jnp.concatenate([w_loc[n].reshape(-1) for n in small]), rows_small)
    g_in, g_mat, g_small = _exchange([w_loc['w_in'].astype(BF16), mat_flat, small_flat], True, "gather_weights")
    full = {}
    for names, buf in ((MAT_WEIGHTS, g_mat.reshape(N_DEV, -1)), (small, g_small.reshape(N_DEV, -1))):
        off = 0
        for n in names:
            sz = int(w_loc[n].size)
            full[n] = _join_shards(buf[:, off:off + sz], w_loc[n].shape, SHARD_DIM[n])
            off += sz
    mats = dict(glu=full['s5_w_glu'], branch=full['w_branch'], out=full['w_out'], **{
        'in': [[jnp.concatenate([g_in[k, l, :, lo:hi] for k, lo, hi, _ in segs], axis=1) for segs in overlaps]
               for l in range(depth)]})
    w_diff = {n: w_loc[n] for n in repl}
    w_diff.update({n: full[n] for n in small})
    w_diff['c_in'] = [[jnp.zeros((d_model, wd), F32) for wd in _in_widths(d_model)] for _ in range(depth)]
    w_diff['c_glu'] = jnp.zeros(full['s5_w_glu'].shape, F32)
    w_diff['c_branch'] = jnp.zeros(full['w_branch'].shape, F32)
    w_diff['c_out'] = jnp.zeros(full['w_out'].shape, F32)

    loss, (g_w, g_x) = jax.value_and_grad(_local_loss, argnums=(0, 2))(w_diff, mats, x[0], target[0])
    g_w['s5_w_glu'], g_w['w_branch'], g_w['w_out'] = g_w['c_glu'], g_w['c_branch'], g_w['c_out']

    def shard_rows(names, rows_n):
        buf = jnp.concatenate([_split_shards(g_w[n], SHARD_DIM[n]) for n in names], axis=1)
        return jnp.pad(buf, ((0, 0), (0, rows_n * LANES - buf.shape[1])))

    send_in = jnp.stack([jnp.stack([
        jnp.concatenate([g_w['c_in'][l][i][:, plo:plo + hi - lo]
                         for i, segs in enumerate(overlaps) for (kk, lo, hi, plo) in segs if kk == k], axis=1)
        for l in range(depth)]) for k in range(N_DEV)]).astype(BF16)
    send_mat = shard_rows(MAT_WEIGHTS, rows_mat).astype(BF16).reshape(N_DEV, rows_mat, LANES)
    rp_vec = jnp.concatenate([g_w[n].reshape(-1) for n in repl] + [loss.reshape(1)])
    rp_vec = jnp.pad(rp_vec, (0, rows_rp * LANES - n_rp))
    rows_f32 = rows_small + rows_rp
    send_f32 = jnp.concatenate([shard_rows(small, rows_small),
                                jnp.broadcast_to(rp_vec[None], (N_DEV, rows_rp * LANES))], axis=1)
    parts_in, parts_mat, parts_f32 = _exchange([send_in, send_mat, send_f32.reshape(N_DEV, rows_f32, LANES)], False,
                                               "exchange_grads")

    def flat_mat(src):
        return _pad_flat(jnp.concatenate([src[n].reshape(-1) for n in MAT_WEIGHTS]), rows_mat)

    def flat_f32(src):
        sm_vec = jnp.concatenate([src[n].reshape(-1) for n in small])
        sm_vec = jnp.pad(sm_vec, (0, rows_small * LANES - sm_vec.shape[0]))
        rep = jnp.concatenate([src[n].reshape(-1) for n in repl] + [jnp.ones((1,), F32)])
        rep = jnp.pad(rep, (0, rows_rp * LANES - n_rp))
        return jnp.concatenate([sm_vec, rep]).reshape(rows_f32, LANES)

    in_outs = _adamw_stacked(parts_in, w_loc['w_in'], m_loc['w_in'], v_loc['w_in'])
    mat_outs = _adamw_flat(parts_mat, flat_mat(w_loc), flat_mat(m_loc), flat_mat(v_loc), "adamw_mat")
    f32_outs = _adamw_flat(parts_f32, flat_f32(w_loc), flat_f32(m_loc), flat_f32(v_loc), "adamw_rest")

    def unflat(big, mat_buf, f32_buf):
        out = {'w_in': big}
        for names, vec, o in ((MAT_WEIGHTS, mat_buf.reshape(-1), 0), (small, f32_buf.reshape(-1), 0),
                              (repl, f32_buf.reshape(-1), rows_small * LANES)):
            for n in names:
                sz = int(w_loc[n].size)
                out[n] = vec[o:o + sz].reshape(w_loc[n].shape)
                o += sz
        return out, f32_buf.reshape(-1)[rows_small * LANES + n_rp - 1]

    (grads, loss_total), (deltas, _), (new_m, _), (new_v, _) = [
        unflat(b, a, f) for b, a, f in zip(in_outs, mat_outs, f32_outs)]
    return (loss_total, g_x[None], *[grads[n] for n in WEIGHTS], *[deltas[n] for n in WEIGHTS],
            *[new_m[n] for n in WEIGHTS], *[new_v[n] for n in WEIGHTS])


def kernel(x, meta, ln_in_g, ln_in_b, w_in, s5_a_re, s5_a_im, s5_log_step, s5_b_re, s5_b_im, s5_c_re, s5_c_im, s5_d, s5_w_glu, s5_b_glu, ssd_conv_w, ssd_conv_b, ssd_dt_bias, ssd_a_log, ssd_d, ssd_norm_g, gdn_conv_w, gdn_dt_bias, gdn_a_log, gdn_norm_g, w_branch, b_gate, w_out, ln_g, ln_b, loss_target, m_meta, m_ln_in_g, m_ln_in_b, m_w_in, m_s5_a_re, m_s5_a_im, m_s5_log_step, m_s5_b_re, m_s5_b_im, m_s5_c_re, m_s5_c_im, m_s5_d, m_s5_w_glu, m_s5_b_glu, m_ssd_conv_w, m_ssd_conv_b, m_ssd_dt_bias, m_ssd_a_log, m_ssd_d, m_ssd_norm_g, m_gdn_conv_w, m_gdn_dt_bias, m_gdn_a_log, m_gdn_norm_g, m_w_branch, m_b_gate, m_w_out, m_ln_g, m_ln_b, v_meta, v_ln_in_g, v_ln_in_b, v_w_in, v_s5_a_re, v_s5_a_im, v_s5_log_step, v_s5_b_re, v_s5_b_im, v_s5_c_re, v_s5_c_im, v_s5_d, v_s5_w_glu, v_s5_b_glu, v_ssd_conv_w, v_ssd_conv_b, v_ssd_dt_bias, v_ssd_a_log, v_ssd_d, v_ssd_norm_g, v_gdn_conv_w, v_gdn_dt_bias, v_gdn_a_log, v_gdn_norm_g, v_w_branch, v_b_gate, v_w_out, v_ln_g, v_ln_b):
    w_loc = dict(zip(WEIGHTS, (meta, ln_in_g, ln_in_b, w_in, s5_a_re, s5_a_im, s5_log_step, s5_b_re, s5_b_im, s5_c_re, s5_c_im, s5_d, s5_w_glu, s5_b_glu, ssd_conv_w, ssd_conv_b, ssd_dt_bias, ssd_a_log, ssd_d, ssd_norm_g, gdn_conv_w, gdn_dt_bias, gdn_a_log, gdn_norm_g, w_branch, b_gate, w_out, ln_g, ln_b)))
    m_loc = dict(zip(WEIGHTS, (m_meta, m_ln_in_g, m_ln_in_b, m_w_in, m_s5_a_re, m_s5_a_im, m_s5_log_step, m_s5_b_re, m_s5_b_im, m_s5_c_re, m_s5_c_im, m_s5_d, m_s5_w_glu, m_s5_b_glu, m_ssd_conv_w, m_ssd_conv_b, m_ssd_dt_bias, m_ssd_a_log, m_ssd_d, m_ssd_norm_g, m_gdn_conv_w, m_gdn_dt_bias, m_gdn_a_log, m_gdn_norm_g, m_w_branch, m_b_gate, m_w_out, m_ln_g, m_ln_b)))
    v_loc = dict(zip(WEIGHTS, (v_meta, v_ln_in_g, v_ln_in_b, v_w_in, v_s5_a_re, v_s5_a_im, v_s5_log_step, v_s5_b_re, v_s5_b_im, v_s5_c_re, v_s5_c_im, v_s5_d, v_s5_w_glu, v_s5_b_glu, v_ssd_conv_w, v_ssd_conv_b, v_ssd_dt_bias, v_ssd_a_log, v_ssd_d, v_ssd_norm_g, v_gdn_conv_w, v_gdn_dt_bias, v_gdn_a_log, v_gdn_norm_g, v_w_branch, v_b_gate, v_w_out, v_ln_g, v_ln_b)))
    return _step(x, loss_target, w_loc, m_loc, v_loc)
```

```python
import functools
import math

import jax
import jax.numpy as jnp
from jax import lax
from jax.experimental import pallas as pl
from jax.experimental.pallas import tpu as pltpu

F32 = jnp.float32
BF16 = jnp.bfloat16

N_DEV = 8
LANES = 128
SUBLANES = 8
VMEM_LIMIT = 56 * 1024 * 1024
FLAT_ROWS = 1024

CHUNK = 64
CONV_K = 4
S5_GROUP = 16
S5_STATE = 64
S5_BLOCK_GROUPS = 8
SSD_HEAD = 64
SSD_HEADS = 12
SSD_GROUPS = 2
SSD_STATE = 128
GDN_HEAD = 128
GDN_HEADS = 6
BRANCH = 768
LN_EPS = 1e-5

ADAM_LR = 0.001
ADAM_B1 = 0.9
ADAM_B2 = 0.999
ADAM_EPS = 1e-08
ADAM_WD = 0.01
ADAM_STEP = 10

WEIGHTS = ['meta', 'ln_in_g', 'ln_in_b', 'w_in', 's5_a_re', 's5_a_im', 's5_log_step', 's5_b_re', 's5_b_im',
           's5_c_re', 's5_c_im', 's5_d', 's5_w_glu', 's5_b_glu', 'ssd_conv_w', 'ssd_conv_b', 'ssd_dt_bias',
           'ssd_a_log', 'ssd_d', 'ssd_norm_g', 'gdn_conv_w', 'gdn_dt_bias', 'gdn_a_log', 'gdn_norm_g',
           'w_branch', 'b_gate', 'w_out', 'ln_g', 'ln_b']
SHARD_DIM = {'meta': 1, 'w_in': 2, 's5_w_glu': 1, 'ssd_conv_w': 2, 'gdn_conv_w': 2, 'w_branch': 3, 'b_gate': 2,
             'w_out': 1}


def _params(sem):
    return pltpu.CompilerParams(dimension_semantics=sem, vmem_limit_bytes=VMEM_LIMIT)


def _row_tile(m, cap):
    best = None
    for t in range(SUBLANES, min(m, cap) + 1, SUBLANES):
        if m % t == 0:
            best = t
    return best if best is not None else m


def _col_tile(n, cap):
    best = None
    for t in range(LANES, min(n, cap) + 1, LANES):
        if n % t == 0:
            best = t
    return best if best is not None else n


def _any_tile(m, cap):
    best = 1
    for t in range(1, min(m, cap) + 1):
        if m % t == 0:
            best = t
    return best


def _mm_fwd(a, b, name):
    m, _ = a.shape
    g, k, n = b.shape
    tm, tn = _row_tile(m, 832), _col_tile(n, 1024)
    nj = n // tn

    def body(a_ref, b_ref, o_ref):
        o_ref[...] = jnp.dot(a_ref[...].astype(BF16), b_ref[0].astype(BF16), preferred_element_type=F32)

    return pl.pallas_call(
        body, name=name, grid=(g, m // tm, nj),
        in_specs=[pl.BlockSpec((tm, k), lambda gi, i, j: (i, gi)),
                  pl.BlockSpec((1, k, tn), lambda gi, i, j: (gi, 0, j))],
        out_specs=pl.BlockSpec((tm, tn), lambda gi, i, j: (i, gi * nj + j)),
        out_shape=jax.ShapeDtypeStruct((m, g * n), F32),
        compiler_params=_params(("arbitrary", "arbitrary", "arbitrary")),
    )(a, b)


def _mm_da(ct, b, name):
    m, _ = ct.shape
    g, k, n = b.shape
    tm, tk = _row_tile(m, 832 if n <= 1536 else 416), _col_tile(k, 1024)
    nk = k // tk

    def body(c_ref, b_ref, o_ref):
        o_ref[...] = lax.dot_general(c_ref[...].astype(BF16), b_ref[0].astype(BF16), (((1,), (1,)), ((), ())),
                                     preferred_element_type=F32)

    return pl.pallas_call(
        body, name=name, grid=(g, m // tm, nk),
        in_specs=[pl.BlockSpec((tm, n), lambda gi, i, j: (i, gi)),
                  pl.BlockSpec((1, tk, n), lambda gi, i, j: (gi, j, 0))],
        out_specs=pl.BlockSpec((tm, tk), lambda gi, i, j: (i, gi * nk + j)),
        out_shape=jax.ShapeDtypeStruct((m, g * k), F32),
        compiler_params=_params(("arbitrary", "arbitrary", "arbitrary")),
    )(ct, b)


def _mm_db(a, ct, g, k, n, name):
    m = a.shape[0]
    tm, tk, tn = _row_tile(m, 832), _col_tile(k, 1024), _col_tile(n, 1280)
    nk, nn = k // tk, n // tn

    def body(a_ref, c_ref, o_ref):
        @pl.when(pl.program_id(3) == 0)
        def _():
            o_ref[...] = jnp.zeros_like(o_ref)

        o_ref[0] += lax.dot_general(a_ref[...].astype(BF16), c_ref[...].astype(BF16), (((0,), (0,)), ((), ())),
                                    preferred_element_type=F32)

    return pl.pallas_call(
        body, name=name, grid=(g, nk, nn, m // tm),
        in_specs=[pl.BlockSpec((tm, tk), lambda gi, i, j, r: (r, gi * nk + i)),
                  pl.BlockSpec((tm, tn), lambda gi, i, j, r: (r, gi * nn + j))],
        out_specs=pl.BlockSpec((1, tk, tn), lambda gi, i, j, r: (gi, i, j)),
        out_shape=jax.ShapeDtypeStruct((g, k, n), F32),
        compiler_params=_params(("arbitrary", "arbitrary", "arbitrary", "arbitrary")),
    )(a, ct)


def _make_gmm(name):
    @jax.custom_vjp
    def gmm(a, b):
        return _mm_fwd(a, b, name + "_fwd")

    def fwd(a, b):
        return _mm_fwd(a, b, name + "_fwd"), (a, b)

    def bwd(res, ct):
        a, b = res
        g, k, n = b.shape
        return _mm_da(ct, b, name + "_da"), _mm_db(a, ct, g, k, n, name + "_db")

    gmm.defvjp(fwd, bwd)
    return gmm


def _mm(name, a, w, carrier):
    @jax.custom_vjp
    def mm(a, w3, carrier3):
        return _mm_fwd(a, w3, name + "_fwd")

    def fwd(a, w3, carrier3):
        return _mm_fwd(a, w3, name + "_fwd"), (a, w3)

    def bwd(res, ct):
        a, w3 = res
        g, k, n = w3.shape
        return _mm_da(ct, w3, name + "_da"), jnp.zeros_like(w3), _mm_db(a, ct, g, k, n, name + "_db")

    mm.defvjp(fwd, bwd)
    return mm(a, w[None], carrier[None])


def _make_rowwise(name, fn, n_row, n_par, out_widths, tm_cap, use_ridx=False):
    n_out = len(out_widths)

    def bind(tm):
        if not use_ridx:
            return fn
        ridx = pl.program_id(0) * tm + lax.broadcasted_iota(jnp.int32, (tm, 1), 0)
        return functools.partial(fn, ridx)

    def specs(args, tm):
        rows = [pl.BlockSpec((tm, a.shape[1]), lambda i: (i, 0)) for a in args[:n_row]]
        pars = [pl.BlockSpec(a.shape, lambda i: (0, 0)) for a in args[n_row:]]
        return rows, pars

    def fwd_call(*args):
        t = args[0].shape[0]
        tm = _row_tile(t, tm_cap)
        rows, pars = specs(args, tm)

        def body(*refs):
            vals = [r[...] for r in refs[:n_row + n_par]]
            res = bind(tm)(*vals)
            for o_ref, r in zip(refs[n_row + n_par:], res):
                o_ref[...] = r

        outs = pl.pallas_call(
            body, name=name + "_fwd", grid=(t // tm,), in_specs=rows + pars,
            out_specs=[pl.BlockSpec((tm, w), lambda i: (i, 0)) for w in out_widths],
            out_shape=[jax.ShapeDtypeStruct((t, w), F32) for w in out_widths],
            compiler_params=_params(("arbitrary",)),
        )(*args)
        return tuple(outs)

    def bwd_call(args, cts):
        t = args[0].shape[0]
        tm = _row_tile(t, tm_cap)
        rows, pars = specs(args, tm)
        ct_specs = [pl.BlockSpec((tm, w), lambda i: (i, 0)) for w in out_widths]
        n_in = n_row + n_par

        def body(*refs):
            vals = [r[...] for r in refs[:n_in]]
            ct_vals = tuple(r[...] for r in refs[n_in:n_in + n_out])
            d_refs = refs[n_in + n_out:]
            f = bind(tm)
            _, vjp = jax.vjp(lambda *a: tuple(f(*a)), *vals)
            grads = vjp(ct_vals)
            for i in range(n_row):
                d_refs[i][...] = grads[i]
            if n_par:
                @pl.when(pl.program_id(0) == 0)
                def _():
                    for j in range(n_par):
                        d_refs[n_row + j][...] = jnp.zeros_like(d_refs[n_row + j])

                for j in range(n_par):
                    d_refs[n_row + j][...] += grads[n_row + j]

        outs = pl.pallas_call(
            body, name=name + "_bwd", grid=(t // tm,), in_specs=rows + pars + ct_specs,
            out_specs=rows + pars,
            out_shape=[jax.ShapeDtypeStruct(a.shape, F32) for a in args],
            compiler_params=_params(("arbitrary",)),
        )(*args, *cts)
        return tuple(outs)

    @jax.custom_vjp
    def op(*args):
        return fwd_call(*args)

    def fwd(*args):
        return fwd_call(*args), args

    def bwd(args, cts):
        return bwd_call(args, cts)

    op.defvjp(fwd, bwd)
    return op


HALO = SUBLANES


def _make_conv_rowwise(name, fn, n_par, out_width, tm_cap, use_ridx=False):
    def bind(ridx):
        return functools.partial(fn, ridx) if use_ridx else fn

    def stage(x_ref, halo_ref, xs, first):
        xs[0:HALO, :] = jnp.where(first, 0.0, halo_ref[...])
        xs[HALO:, :] = x_ref[...]

    def taps(xs, tm):
        return [xs[pl.ds(HALO - (CONV_K - 1) + j, tm), :] for j in range(CONV_K)]

    def fwd_call(x, *pars):
        t, wd = x.shape
        tm = _row_tile(t, tm_cap)
        per = tm // HALO

        def body(*refs):
            x_ref, halo_ref = refs[:2]
            par_refs, o_ref, xs = refs[2:2 + n_par], refs[2 + n_par], refs[-1]
            i = pl.program_id(0)
            stage(x_ref, halo_ref, xs, i == 0)
            ridx = i * tm + lax.broadcasted_iota(jnp.int32, (tm, 1), 0)
            (o_ref[...],) = bind(ridx)(*taps(xs, tm), *[r[...] for r in par_refs])

        return pl.pallas_call(
            body, name=name + "_fwd", grid=(t // tm,),
            in_specs=[pl.BlockSpec((tm, wd), lambda i: (i, 0)),
                      pl.BlockSpec((HALO, wd), lambda i: (jnp.maximum(i * per - 1, 0), 0))]
            + [pl.BlockSpec(p.shape, lambda i: (0, 0)) for p in pars],
            out_specs=pl.BlockSpec((tm, out_width), lambda i: (i, 0)),
            out_shape=jax.ShapeDtypeStruct((t, out_width), F32),
            scratch_shapes=[pltpu.VMEM((tm + HALO, wd), F32)],
            compiler_params=_params(("arbitrary",)),
        )(x, x, *pars)

    def bwd_call(x, pars, ct):
        t, wd = x.shape
        tm = _row_tile(t, tm_cap)
        per = tm // HALO
        nb = t // tm

        def body(*refs):
            x_ref, halo_ref = refs[:2]
            par_refs, ct_ref = refs[2:2 + n_par], refs[2 + n_par]
            dx_ref, dpar_refs = refs[3 + n_par], refs[4 + n_par:4 + 2 * n_par]
            xs, ds, carry = refs[-3:]
            step = pl.program_id(0)
            blk = nb - 1 - step

            @pl.when(step == 0)
            def _():
                carry[...] = jnp.zeros_like(carry)
                for r in dpar_refs:
                    r[...] = jnp.zeros_like(r)

            stage(x_ref, halo_ref, xs, blk == 0)
            ridx = blk * tm + lax.broadcasted_iota(jnp.int32, (tm, 1), 0)
            f = bind(ridx)
            _, vjp = jax.vjp(lambda *a: tuple(f(*a)), *taps(xs, tm), *[r[...] for r in par_refs])
            grads = vjp((ct_ref[...],))
            ds[...] = jnp.zeros_like(ds)
            for j in range(CONV_K):
                ds[pl.ds(HALO - (CONV_K - 1) + j, tm), :] += grads[j]
            ds[pl.ds(tm, HALO), :] += carry[...]
            dx_ref[...] = ds[HALO:, :]
            carry[...] = ds[0:HALO, :]
            for r, g in zip(dpar_refs, grads[CONV_K:]):
                r[...] += g

        rev = lambda i: (nb - 1 - i, 0)
        outs = pl.pallas_call(
            body, name=name + "_bwd", grid=(nb,),
            in_specs=[pl.BlockSpec((tm, wd), rev),
                      pl.BlockSpec((HALO, wd), lambda i: (jnp.maximum((nb - 1 - i) * per - 1, 0), 0))]
            + [pl.BlockSpec(p.shape, lambda i: (0, 0)) for p in pars]
            + [pl.BlockSpec((tm, out_width), rev)],
            out_specs=[pl.BlockSpec((tm, wd), rev)] + [pl.BlockSpec(p.shape, lambda i: (0, 0)) for p in pars],
            out_shape=[jax.ShapeDtypeStruct(x.shape, F32)] + [jax.ShapeDtypeStruct(p.shape, F32) for p in pars],
            scratch_shapes=[pltpu.VMEM((tm + HALO, wd), F32), pltpu.VMEM((tm + HALO, wd), F32),
                            pltpu.VMEM((HALO, wd), F32)],
            compiler_params=_params(("arbitrary",)),
        )(x, x, *pars, ct)
        return tuple(outs)

    @jax.custom_vjp
    def op(x, *pars):
        return fwd_call(x, *pars)

    def fwd(x, *pars):
        return fwd_call(x, *pars), (x, pars)

    def bwd(res, ct):
        x, pars = res
        return bwd_call(x, pars, ct)

    op.defvjp(fwd, bwd)
    return op


def _make_chunk_scan(name, fn, state_shape, n_seq, n_par, out_widths):
    n_out = len(out_widths)
    zeros_idx = (0,) * len(state_shape)

    def fwd_call(*args):
        t = args[0].shape[0]
        nc = t // CHUNK
        seq_specs = [pl.BlockSpec((CHUNK, a.shape[1]), lambda c: (c, 0)) for a in args[:n_seq]]
        par_specs = [pl.BlockSpec(a.shape, lambda c: (0, 0)) for a in args[n_seq:]]

        def body(*refs):
            ins = refs[:n_seq + n_par]
            out_refs = refs[n_seq + n_par:n_seq + n_par + n_out]
            states_ref = refs[n_seq + n_par + n_out]
            st = refs[-1]

            @pl.when(pl.program_id(0) == 0)
            def _():
                st[...] = jnp.zeros_like(st)

            s0 = st[...]
            states_ref[0] = s0
            res = fn(s0, *[r[...] for r in ins])
            st[...] = res[0]
            for o_ref, r in zip(out_refs, res[1:]):
                o_ref[...] = r

        outs = pl.pallas_call(
            body, name=name + "_fwd", grid=(nc,), in_specs=seq_specs + par_specs,
            out_specs=[pl.BlockSpec((CHUNK, w), lambda c: (c, 0)) for w in out_widths]
            + [pl.BlockSpec((1,) + state_shape, lambda c: (c,) + zeros_idx)],
            out_shape=[jax.ShapeDtypeStruct((t, w), F32) for w in out_widths]
            + [jax.ShapeDtypeStruct((nc,) + state_shape, F32)],
            scratch_shapes=[pltpu.VMEM(state_shape, F32)],
            compiler_params=_params(("arbitrary",)),
        )(*args)
        return tuple(outs[:n_out]), outs[n_out]

    def bwd_call(args, states, cts):
        t = args[0].shape[0]
        nc = t // CHUNK
        rev = lambda c: (nc - 1 - c, 0)
        seq_specs = [pl.BlockSpec((CHUNK, a.shape[1]), rev) for a in args[:n_seq]]
        par_specs = [pl.BlockSpec(a.shape, lambda c: (0, 0)) for a in args[n_seq:]]
        ct_specs = [pl.BlockSpec((CHUNK, w), rev) for w in out_widths]
        st_spec = pl.BlockSpec((1,) + state_shape, lambda c: (nc - 1 - c,) + zeros_idx)
        n_in = n_seq + n_par

        def body(*refs):
            vals = [r[...] for r in refs[:n_in]]
            s0 = refs[n_in][0]
            ct_vals = tuple(r[...] for r in refs[n_in + 1:n_in + 1 + n_out])
            d_refs = refs[n_in + 1 + n_out:-1]
            dst = refs[-1]

            @pl.when(pl.program_id(0) == 0)
            def _():
                dst[...] = jnp.zeros_like(dst)
                for j in range(n_par):
                    d_refs[n_seq + j][...] = jnp.zeros_like(d_refs[n_seq + j])

            _, vjp = jax.vjp(lambda *a: tuple(fn(*a)), s0, *vals)
            grads = vjp((dst[...],) + ct_vals)
            dst[...] = grads[0]
            for i in range(n_seq):
                d_refs[i][...] = grads[1 + i]
            for j in range(n_par):
                d_refs[n_seq + j][...] += grads[1 + n_seq + j]

        outs = pl.pallas_call(
            body, name=name + "_bwd", grid=(nc,), in_specs=seq_specs + par_specs + [st_spec] + ct_specs,
            out_specs=seq_specs + par_specs,
            out_shape=[jax.ShapeDtypeStruct(a.shape, F32) for a in args],
            scratch_shapes=[pltpu.VMEM(state_shape, F32)],
            compiler_params=_params(("arbitrary",)),
        )(*args, states, *cts)
        return tuple(outs)

    @jax.custom_vjp
    def op(*args):
        return fwd_call(*args)[0]

    def fwd(*args):
        outs, states = fwd_call(*args)
        return outs, (args, states)

    def bwd(res, cts):
        args, states = res
        return bwd_call(args, states, cts)

    op.defvjp(fwd, bwd)
    return op


def _s5_scan_fwd(bre, bim, are, aim, name):
    t, r, _ = bre.shape
    tb = _any_tile(t, 208)
    blk = pl.BlockSpec((tb, r, LANES), lambda i: (i, 0, 0))
    par = pl.BlockSpec((r, LANES), lambda i: (0, 0))

    def body(bre_ref, bim_ref, are_ref, aim_ref, sre_ref, sim_ref, st):
        @pl.when(pl.program_id(0) == 0)
        def _():
            st[...] = jnp.zeros_like(st)

        ar, ai = are_ref[...], aim_ref[...]

        def step(k, carry):
            sr, si = carry
            nr = ar * sr - ai * si + bre_ref[k]
            ni = ar * si + ai * sr + bim_ref[k]
            sre_ref[k] = nr
            sim_ref[k] = ni
            return nr, ni

        sr, si = lax.fori_loop(0, tb, step, (st[0], st[1]), unroll=4)
        st[0] = sr
        st[1] = si

    return pl.pallas_call(
        body, name=name, grid=(t // tb,), in_specs=[blk, blk, par, par], out_specs=[blk, blk],
        out_shape=[jax.ShapeDtypeStruct(bre.shape, F32)] * 2,
        scratch_shapes=[pltpu.VMEM((2, r, LANES), F32)],
        compiler_params=_params(("arbitrary",)),
    )(bre, bim, are, aim)


def _s5_scan_bwd(dsr, dsi, sre, sim, are, aim, name):
    t, r, _ = sre.shape
    tb = _any_tile(t, 208)
    nb = t // tb
    blk = pl.BlockSpec((tb, r, LANES), lambda i: (nb - 1 - i, 0, 0))
    par = pl.BlockSpec((r, LANES), lambda i: (0, 0))

    def body(dsr_ref, dsi_ref, sre_ref, sim_ref, are_ref, aim_ref, gre_ref, gim_ref, dar_ref, dai_ref, st):
        @pl.when(pl.program_id(0) == 0)
        def _():
            st[...] = jnp.zeros_like(st)
            dar_ref[...] = jnp.zeros_like(dar_ref)
            dai_ref[...] = jnp.zeros_like(dai_ref)

        ar, ai = are_ref[...], aim_ref[...]

        def step(k, carry):
            gr, gi, dar, dai = carry
            q = tb - 1 - k
            s_r, s_i = sre_ref[q], sim_ref[q]
            dar = dar + gr * s_r + gi * s_i
            dai = dai + gi * s_r - gr * s_i
            ngr = dsr_ref[q] + ar * gr + ai * gi
            ngi = dsi_ref[q] + ar * gi - ai * gr
            gre_ref[q] = ngr
            gim_ref[q] = ngi
            return ngr, ngi, dar, dai

        gr, gi, dar, dai = lax.fori_loop(0, tb, step, (st[0], st[1], dar_ref[...], dai_ref[...]), unroll=4)
        st[0] = gr
        st[1] = gi
        dar_ref[...] = dar
        dai_ref[...] = dai

    return pl.pallas_call(
        body, name=name, grid=(nb,), in_specs=[blk, blk, blk, blk, par, par], out_specs=[blk, blk, par, par],
        out_shape=[jax.ShapeDtypeStruct(sre.shape, F32)] * 2 + [jax.ShapeDtypeStruct(are.shape, F32)] * 2,
        scratch_shapes=[pltpu.VMEM((2, r, LANES), F32)],
        compiler_params=_params(("arbitrary",)),
    )(dsr, dsi, sre, sim, are, aim)


def _make_s5_scan(name):
    @jax.custom_vjp
    def scan(bre, bim, are, aim):
        return tuple(_s5_scan_fwd(bre, bim, are, aim, name + "_fwd"))

    def fwd(bre, bim, are, aim):
        sre, sim = _s5_scan_fwd(bre, bim, are, aim, name + "_fwd")
        return (sre, sim), (sre, sim, are, aim)

    def bwd(res, cts):
        sre, sim, are, aim = res
        return tuple(_s5_scan_bwd(cts[0], cts[1], sre, sim, are, aim, name + "_bwd"))

    scan.defvjp(fwd, bwd)
    return scan


def _dot(a, b):
    return jnp.dot(a.astype(BF16), b.astype(BF16), preferred_element_type=F32)


def _dot_nt(a, b):
    return lax.dot_general(a.astype(BF16), b.astype(BF16), (((1,), (1,)), ((), ())), preferred_element_type=F32)


def _dot_tn(a, b):
    return lax.dot_general(a.astype(BF16), b.astype(BF16), (((0,), (0,)), ((), ())), preferred_element_type=F32)


def _dot_f32(a, b):
    return jnp.dot(a, b, precision=lax.Precision.HIGHEST, preferred_element_type=F32)


def _dot_tn_f32(a, b):
    return lax.dot_general(a, b, (((0,), (0,)), ((), ())), precision=lax.Precision.HIGHEST,
                           preferred_element_type=F32)


def _iota(shape, dim):
    return lax.broadcasted_iota(jnp.int32, shape, dim)


def _tri(strict=False):
    r, c = _iota((CHUNK, CHUNK), 0), _iota((CHUNK, CHUNK), 1)
    return (r > c) if strict else (r >= c)


def _silu(x):
    return x * jax.nn.sigmoid(x)


def _layer_norm(z, g, b):
    mu = jnp.mean(z, axis=-1, keepdims=True)
    var = jnp.mean(jnp.square(z - mu), axis=-1, keepdims=True)
    return (z - mu) * lax.rsqrt(var + LN_EPS) * g + b


def _ssd_chunk(state, xbc, sm, a_c, d_exp):
    width = SSD_HEADS * SSD_HEAD
    x = xbc[:, :width]
    lane = _iota((CHUNK, LANES), 1)
    dtc = jnp.where(lane < SSD_HEADS, sm, 0.0)
    low = _tri().astype(F32)
    eye = (_iota((CHUNK, CHUNK), 0) == _iota((CHUNK, CHUNK), 1)).astype(F32)
    head_col, head_row = _iota((LANES, width), 1), _iota((LANES, width), 0) * SSD_HEAD
    expand = ((head_col >= head_row) & (head_col < head_row + SSD_HEAD)).astype(F32)
    acum_c = _dot_f32(low, dtc * a_c)
    acum_ct = _dot_tn_f32(acum_c, eye)
    dt_exp = _dot_f32(dtc, expand)
    acum = _dot_f32(acum_c, expand)
    xd = x * dt_exp
    last = acum[CHUNK - 1:CHUNK, :]
    to_end = jnp.exp(last - acum)
    eac = jnp.exp(acum)
    causal = _tri()
    first_half = _iota((CHUNK, LANES), 1) < SSD_HEAD
    top_rows = _iota((LANES, LANES), 0) < SSD_HEAD
    pairs = range(SSD_HEADS // 2)
    grp = [(2 * p) // (SSD_HEADS // SSD_GROUPS) for p in pairs]
    cols = [slice(p * LANES, (p + 1) * LANES) for p in pairs]
    bg = [xbc[:, width + g * SSD_STATE: width + (g + 1) * SSD_STATE] for g in range(SSD_GROUPS)]
    cg = [xbc[:, width + (SSD_GROUPS + g) * SSD_STATE: width + (SSD_GROUPS + g + 1) * SSD_STATE]
          for g in range(SSD_GROUPS)]
    scores = [_dot_nt(cg[g], bg[g]) for g in range(SSD_GROUPS)]
    dec = [jnp.where(causal, jnp.exp(jnp.minimum(acum_c[:, h:h + 1] - acum_ct[h:h + 1, :], 0.0)), 0.0)
           for h in range(SSD_HEADS)]
    y_lo = [_dot(scores[grp[p]] * dec[2 * p], jnp.where(first_half, xd[:, cols[p]], 0.0)) for p in pairs]
    y_hi = [_dot(scores[grp[p]] * dec[2 * p + 1], jnp.where(first_half, 0.0, xd[:, cols[p]])) for p in pairs]
    s_prev = [state[p * LANES:(p + 1) * LANES, :] for p in pairs]
    y_off = [_dot_nt(cg[grp[p]], s_prev[p]) for p in pairs]
    s_add = [_dot_tn(xd[:, cols[p]] * to_end[:, cols[p]], bg[grp[p]]) for p in pairs]
    ys = [y_lo[p] + y_hi[p] + y_off[p] * eac[:, cols[p]] + x[:, cols[p]] * d_exp[:, cols[p]] for p in pairs]
    cd = [jnp.where(top_rows, jnp.exp(acum_c[CHUNK - 1:CHUNK, 2 * p:2 * p + 1]),
                    jnp.exp(acum_c[CHUNK - 1:CHUNK, 2 * p + 1:2 * p + 2])) for p in pairs]
    new_states = [s_prev[p] * cd[p] + s_add[p] for p in pairs]
    return jnp.concatenate(new_states, axis=0), jnp.concatenate(ys, axis=1)


def _gdn_chunk(state, qkv, sm):
    width = GDN_HEADS * GDN_HEAD
    g0, b0 = SSD_HEADS, SSD_HEADS + GDN_HEADS
    lane = _iota((CHUNK, LANES), 1)
    gc = jnp.where((lane >= g0) & (lane < b0), sm, 0.0)
    low = _tri().astype(F32)
    eye = (_iota((CHUNK, CHUNK), 0) == _iota((CHUNK, CHUNK), 1)).astype(F32)
    gcum = _dot_f32(low, gc)
    gcum_t = _dot_tn_f32(gcum, eye)
    causal, strict = _tri(), _tri(strict=True)
    heads = range(GDN_HEADS)
    q = [qkv[:, h * GDN_HEAD:(h + 1) * GDN_HEAD] for h in heads]
    k = [qkv[:, width + h * GDN_HEAD: width + (h + 1) * GDN_HEAD] for h in heads]
    v = [qkv[:, 2 * width + h * GDN_HEAD: 2 * width + (h + 1) * GDN_HEAD] for h in heads]
    beta = [sm[:, b0 + h:b0 + h + 1] for h in heads]
    gcol = [gcum[:, g0 + h:g0 + h + 1] for h in heads]
    glast = [gcum[CHUNK - 1:CHUNK, g0 + h:g0 + h + 1] for h in heads]
    gamma = [jnp.where(causal, jnp.exp(jnp.minimum(gcol[h] - gcum_t[g0 + h:g0 + h + 1, :], 0.0)), 0.0) for h in heads]
    kk = [_dot_nt(k[h], k[h]) for h in heads]
    qk = [_dot_nt(q[h], k[h]) for h in heads]
    egc = [jnp.exp(gcol[h]) for h in heads]
    nmat = [-jnp.where(strict, kk[h] * gamma[h] * beta[h], 0.0) for h in heads]
    sol = [jnp.concatenate([v[h] * beta[h], k[h] * (beta[h] * egc[h])], axis=1) for h in heads]
    for i in range(6):
        upd = [_dot(nmat[h], sol[h]) for h in heads]
        if i < 5:
            nmat = [_dot(nmat[h], nmat[h]) for h in heads]
        sol = [sol[h] + upd[h] for h in heads]
    s_prev = [state[h * GDN_HEAD:(h + 1) * GDN_HEAD, :] for h in heads]
    w_s = [_dot(sol[h][:, GDN_HEAD:], s_prev[h]) for h in heads]
    q_s = [_dot(q[h] * egc[h], s_prev[h]) for h in heads]
    v_new = [sol[h][:, :GDN_HEAD] - w_s[h] for h in heads]
    a_v = [_dot(qk[h] * gamma[h], v_new[h]) for h in heads]
    k_v = [_dot_tn(k[h] * jnp.exp(glast[h] - gcol[h]), v_new[h]) for h in heads]
    outs = [q_s[h] + a_v[h] for h in heads]
    new_states = [s_prev[h] * jnp.exp(glast[h]) + k_v[h] for h in heads]
    return jnp.concatenate(new_states, axis=0), jnp.concatenate(outs, axis=1)


def _row_fns(d_model, pad_rows, loss_rows, alpha):
    def keep(ridx, v):
        return jnp.where(ridx >= pad_rows, v, 0.0)

    def ln_in(ridx, h, g, b):
        return (keep(ridx, _layer_norm(h, g, b)),)

    def ln_layer(ridx, h, o, g, b):
        return (keep(ridx, _layer_norm(alpha * h + o, g, b)),)

    def s5_act(y_re, y_im, u, d):
        return (jax.nn.gelu(y_re + y_im + d * u),)

    def s5_glu(v0, lin, z, bias):
        return (v0 * jax.nn.sigmoid(lin + bias) * _silu(z),)

    def small_act(ridx, raw, bias, scale):
        lane = _iota(raw.shape, 1)
        sp = jax.nn.softplus(raw + bias)
        g0, b0 = SSD_HEADS, SSD_HEADS + GDN_HEADS
        out = jnp.where(lane < g0, sp, jnp.where(lane < b0, scale * sp,
                                                 jnp.where(lane < b0 + GDN_HEADS, jax.nn.sigmoid(raw), 0.0)))
        return (keep(ridx, out),)

    def conv(xs, w):
        acc = xs[0] * w[0:1, :]
        for j in range(1, CONV_K):
            acc = acc + xs[j] * w[j:j + 1, :]
        return acc

    def ssd_conv(ridx, x0, x1, x2, x3, w, b):
        return (keep(ridx, _silu(conv((x0, x1, x2, x3), w) + b)),)

    def ssd_norm(y, z, g):
        v = y * _silu(z)
        return (v * lax.rsqrt(jnp.mean(v * v, axis=-1, keepdims=True) + LN_EPS) * g,)

    def gdn_conv(x0, x1, x2, x3, w):
        a = _silu(conv((x0, x1, x2, x3), w))
        width = GDN_HEADS * GDN_HEAD
        parts = []
        for h in range(2 * GDN_HEADS):
            z = a[:, h * GDN_HEAD:(h + 1) * GDN_HEAD]
            z = z * lax.rsqrt(jnp.sum(z * z, axis=-1, keepdims=True) + 1e-6)
            parts.append(z * GDN_HEAD ** -0.5 if h < GDN_HEADS else z)
        parts.append(a[:, 2 * width:])
        return (jnp.concatenate(parts, axis=1),)

    def gdn_norm(o, z, g):
        parts = []
        for h in range(GDN_HEADS):
            cols = slice(h * GDN_HEAD, (h + 1) * GDN_HEAD)
            oh = o[:, cols]
            oh = oh * lax.rsqrt(jnp.mean(oh * oh, axis=-1, keepdims=True) + LN_EPS) * g
            parts.append(oh * _silu(z[:, cols]))
        return (jnp.concatenate(parts, axis=1),)

    def merge(oa, ob, oc, gate, bias):
        acc = None
        for k, o in enumerate((oa, ob, oc)):
            cols = slice(k * d_model, (k + 1) * d_model)
            term = jax.nn.sigmoid(gate[:, cols] + bias[:, cols]) * o
            acc = term if acc is None else acc + term
        return (acc,)

    def loss_rows_fn(ridx, h, tgt):
        row = 0.5 * jnp.mean(jnp.square(h - tgt), axis=-1, keepdims=True)
        row = jnp.where(ridx >= loss_rows, row, 0.0)
        lane = _iota((h.shape[0], LANES), 1)
        return (jnp.where(lane == 0, row, 0.0),)

    return dict(ln_in=ln_in, ln_layer=ln_layer, s5_act=s5_act, s5_glu=s5_glu, small_act=small_act,
                ssd_conv=ssd_conv, ssd_norm=ssd_norm, gdn_conv=gdn_conv, gdn_norm=gdn_norm, merge=merge,
                loss=loss_rows_fn)


def _exchange(srcs, name):
    n_arr = len(srcs)
    n_peer = N_DEV - 1
    flips = [(fx, fy, fc) for fx in (0, 1) for fy in (0, 1) for fc in (0, 1)][1:]
    blocks = [s.shape[1:] for s in srcs]

    def body(*refs):
        src_refs, out_refs = refs[:n_arr], refs[n_arr:2 * n_arr]
        send_sems, recv_sems, local_sems = refs[2 * n_arr:]
        x, y, c = lax.axis_index("x"), lax.axis_index("y"), lax.axis_index("c")
        me = 4 * x + 2 * y + c

        def pick(a, j):
            return src_refs[a].at[j]

        own = [pltpu.make_async_copy(pick(a, me), out_refs[a].at[me], local_sems.at[a]) for a in range(n_arr)]
        for cp in own:
            cp.start()
        copies = []
        for k, (fx, fy, fc) in enumerate(flips):
            px = 1 - x if fx else x
            py = 1 - y if fy else y
            pc = 1 - c if fc else c
            peer = 4 * px + 2 * py + pc
            for a in range(n_arr):
                cp = pltpu.make_async_remote_copy(
                    src_ref=pick(a, peer), dst_ref=out_refs[a].at[me],
                    send_sem=send_sems.at[a * n_peer + k], recv_sem=recv_sems.at[a * n_peer + k],
                    device_id=(px, py, pc), device_id_type=pl.DeviceIdType.MESH)
                cp.start()
                copies.append(cp)
        for cp in copies:
            cp.wait()
        for cp in own:
            cp.wait()

    return pl.pallas_call(
        body, name=name,
        in_specs=[pl.BlockSpec(memory_space=pl.ANY)] * n_arr, out_specs=[pl.BlockSpec(memory_space=pl.ANY)] * n_arr,
        out_shape=[jax.ShapeDtypeStruct((N_DEV,) + tuple(b), s.dtype) for b, s in zip(blocks, srcs)],
        scratch_shapes=[pltpu.SemaphoreType.DMA((n_arr * n_peer,)), pltpu.SemaphoreType.DMA((n_arr * n_peer,)),
                        pltpu.SemaphoreType.DMA((n_arr,))],
    )(*srcs)


def _gather(srcs, name):
    n_arr = len(srcs)
    n_sem = N_DEV - 1

    def body(*refs):
        src_refs, out_refs = refs[:n_arr], refs[n_arr:2 * n_arr]
        send_sems, recv_sems, local_sems = refs[2 * n_arr:]
        x, y, c = lax.axis_index("x"), lax.axis_index("y"), lax.axis_index("c")
        me, sibling = (x, y, c), (x, y, 1 - c)
        chips = [(1 - x, y), (x, 1 - y), (1 - x, 1 - y)]

        def slot(a, dev):
            return out_refs[a].at[4 * dev[0] + 2 * dev[1] + dev[2]]

        def copy(a, k, block, to, own=False):
            return pltpu.make_async_remote_copy(
                src_ref=src_refs[a] if own else slot(a, block), dst_ref=slot(a, block),
                send_sem=send_sems.at[a * n_sem + k], recv_sem=recv_sems.at[a * n_sem + k],
                device_id=to, device_id_type=pl.DeviceIdType.MESH)

        arrays = range(n_arr)
        mine = [pltpu.make_async_copy(src_refs[a], slot(a, me), local_sems.at[a]) for a in arrays]
        for cp in mine:
            cp.start()
        first = [copy(a, 0, me, sibling, own=True) for a in arrays]
        first += [copy(a, 1 + j, me, (*chip, c), own=True) for j, chip in enumerate(chips) for a in arrays]
        for cp in first:
            cp.start()
        passed = []
        for j, chip in enumerate(chips):
            for a in arrays:
                copy(a, 1 + j, (*chip, c), me).wait_recv()
                fwd = copy(a, 4 + j, (*chip, c), sibling)
                fwd.start()
                passed.append(fwd)
        for a in arrays:
            copy(a, 0, sibling, me).wait_recv()
        for j, chip in enumerate(chips):
            for a in arrays:
                copy(a, 4 + j, (*chip, 1 - c), me).wait_recv()
        for cp in first + passed:
            cp.wait_send()
        for cp in mine:
            cp.wait()

    return pl.pallas_call(
        body, name=name,
        in_specs=[pl.BlockSpec(memory_space=pl.ANY)] * n_arr, out_specs=[pl.BlockSpec(memory_space=pl.ANY)] * n_arr,
        out_shape=[jax.ShapeDtypeStruct((N_DEV,) + tuple(s.shape), s.dtype) for s in srcs],
        scratch_shapes=[pltpu.SemaphoreType.DMA((n_arr * n_sem,)), pltpu.SemaphoreType.DMA((n_arr * n_sem,)),
                        pltpu.SemaphoreType.DMA((n_arr,))],
    )(*srcs)


def _adamw_body(p_ref, w_ref, m_ref, v_ref, g_ref, d_ref, nm_ref, nv_ref):
    bc1 = 1.0 - ADAM_B1 ** ADAM_STEP
    bc2 = 1.0 - ADAM_B2 ** ADAM_STEP
    g = p_ref[0].astype(F32)
    for k in range(1, N_DEV):
        g = g + p_ref[k].astype(F32)
    nm = ADAM_B1 * m_ref[...] + (1.0 - ADAM_B1) * g
    nv = ADAM_B2 * v_ref[...] + (1.0 - ADAM_B2) * jnp.square(g)
    m_hat = nm / bc1
    v_hat = nv / bc2
    g_ref[...] = g
    d_ref[...] = -ADAM_LR * (m_hat / (jnp.sqrt(v_hat) + ADAM_EPS) + ADAM_WD * w_ref[...])
    nm_ref[...] = nm
    nv_ref[...] = nv


def _adamw_flat(parts, w, m, v, name):
    rows = w.shape[0]
    tr = FLAT_ROWS
    blk = pl.BlockSpec((tr, LANES), lambda i: (i, 0))
    return pl.pallas_call(
        functools.partial(_adamw_body), name=name, grid=(rows // tr,),
        in_specs=[pl.BlockSpec((N_DEV, tr, LANES), lambda i: (0, i, 0)), blk, blk, blk],
        out_specs=[blk] * 4, out_shape=[jax.ShapeDtypeStruct((rows, LANES), F32)] * 4,
        compiler_params=_params(("arbitrary",)),
    )(parts, w, m, v)


def _adamw_stacked(parts, w, m, v):
    depth, d, n = w.shape
    tr = _row_tile(d, 128)
    blk = pl.BlockSpec((1, tr, n), lambda l, i: (l, i, 0))
    return pl.pallas_call(
        functools.partial(_adamw_body), name="adamw_w_in", grid=(depth, d // tr),
        in_specs=[pl.BlockSpec((N_DEV, 1, tr, n), lambda l, i: (0, l, i, 0)), blk, blk, blk],
        out_specs=[blk] * 4, out_shape=[jax.ShapeDtypeStruct(w.shape, F32)] * 4,
        compiler_params=_params(("arbitrary", "arbitrary")),
    )(parts, w, m, v)


def _pad_flat(vec, rows):
    return jnp.pad(vec, (0, rows * LANES - vec.shape[0])).reshape(rows, LANES)


def _rows_for(n):
    return -(-n // (FLAT_ROWS * LANES)) * FLAT_ROWS


def _split_shards(full, dim):
    shp = full.shape
    parts = full.reshape(shp[:dim] + (N_DEV, shp[dim] // N_DEV) + shp[dim + 1:])
    return jnp.moveaxis(parts, dim, 0).reshape(N_DEV, -1)


def _join_shards(rows, local_shape, dim):
    parts = jnp.moveaxis(rows.reshape((N_DEV,) + tuple(local_shape)), 0, dim)
    shp = tuple(local_shape)
    return parts.reshape(shp[:dim] + (N_DEV * shp[dim],) + shp[dim + 1:])


def _s5_tables(a_re, a_im, log_step, b_re, b_im, c_re, c_im):
    lam_re = jnp.minimum(a_re, -1e-4)
    lam_im = a_im
    step = jnp.exp(log_step)[:, None]
    mag = jnp.exp(lam_re * step)
    abar_re, abar_im = mag * jnp.cos(lam_im * step), mag * jnp.sin(lam_im * step)
    den = lam_re * lam_re + lam_im * lam_im
    nr, ni = abar_re - 1.0, abar_im
    coef_re = (nr * lam_re + ni * lam_im) / den
    coef_im = (ni * lam_re - nr * lam_im) / den
    bbar_re = coef_re[..., None] * b_re - coef_im[..., None] * b_im
    bbar_im = coef_re[..., None] * b_im + coef_im[..., None] * b_re
    groups = a_re.shape[0]
    nblk = groups // S5_BLOCK_GROUPS
    eye = jnp.eye(S5_BLOCK_GROUPS, dtype=F32)

    def in_blocks(bb):
        t = jnp.swapaxes(bb, 1, 2).reshape(nblk, S5_BLOCK_GROUPS, S5_GROUP, S5_STATE)
        blk = jnp.einsum('ab,jacp->jacbp', eye, t)
        return blk.reshape(nblk, S5_BLOCK_GROUPS * S5_GROUP, S5_BLOCK_GROUPS * S5_STATE)

    def out_blocks(cc):
        t = jnp.swapaxes(cc, 1, 2).reshape(nblk, S5_BLOCK_GROUPS, S5_STATE, S5_GROUP)
        blk = jnp.einsum('ab,japc->japbc', eye, t)
        return blk.reshape(nblk, S5_BLOCK_GROUPS * S5_STATE, S5_BLOCK_GROUPS * S5_GROUP)

    rows = groups * S5_STATE // LANES
    return dict(b_re=in_blocks(bbar_re), b_im=in_blocks(bbar_im), c_re=out_blocks(c_re), c_im=out_blocks(-c_im),
                a_re=abar_re.reshape(rows, LANES), a_im=abar_im.reshape(rows, LANES))


def _in_widths(d_model):
    return [BRANCH, BRANCH, SSD_HEADS * SSD_HEAD + 2 * SSD_GROUPS * SSD_STATE, SSD_HEADS, BRANCH,
            3 * BRANCH, GDN_HEADS, GDN_HEADS, BRANCH, 3 * d_model]


def _local_loss(w, mats, x, target):
    n_meta, d_model = w['meta'].shape
    depth = len(mats['in'])
    seq = x.shape[0]
    pad_rows = CHUNK - n_meta
    first = pad_rows + n_meta
    t_all = first + seq
    alpha = (2 * depth) ** 0.25
    fns = _row_fns(d_model, pad_rows, first, alpha)
    row = lambda nm, key, n_row, n_par, widths, cap, ridx=False: _make_rowwise(nm, fns[key], n_row, n_par, widths, cap, ridx)

    h = jnp.concatenate([jnp.zeros((pad_rows, d_model), F32), w['meta'], x], axis=0)
    (h,) = row("ln_in", 'ln_in', 1, 2, [d_model], 416, True)(h, w['ln_in_g'][None], w['ln_in_b'][None])

    n_small = SSD_HEADS + 2 * GDN_HEADS

    def small_cols(ps):
        return jnp.pad(jnp.concatenate([ps[3], ps[6], ps[7]], axis=1), ((0, 0), (0, LANES - n_small)))

    for l in range(depth):
        pw, pc = mats['in'][l], w['c_in'][l]
        s5_u = _mm("in_s5u", h, pw[0], pc[0])
        s5_z = _mm("in_s5z", h, pw[1], pc[1])
        ssd_xbc = _mm("in_ssdx", h, pw[2], pc[2])
        ssd_z = _mm("in_ssdz", h, pw[4], pc[4])
        gdn_qkv = _mm("in_gdnq", h, pw[5], pc[5])
        gdn_z = _mm("in_gdnz", h, pw[8], pc[8])
        gate = _mm("in_gate", h, pw[9], pc[9])
        small = _mm("in_small", h, small_cols(pw), small_cols(pc))

        zeros_tail = jnp.zeros((LANES - n_small,), F32)
        bias = jnp.concatenate([w['ssd_dt_bias'][l], w['gdn_dt_bias'][l], jnp.zeros((GDN_HEADS,), F32), zeros_tail])[None]
        scale = jnp.concatenate([jnp.ones((SSD_HEADS,), F32), -jnp.exp(w['gdn_a_log'][l]),
                                 jnp.zeros((GDN_HEADS,), F32), zeros_tail])[None]
        (sm,) = row("small_act", 'small_act', 1, 2, [LANES], 832, True)(small, bias, scale)

        tb = _s5_tables(w['s5_a_re'][l], w['s5_a_im'][l], w['s5_log_step'][l], w['s5_b_re'][l], w['s5_b_im'][l],
                        w['s5_c_re'][l], w['s5_c_im'][l])
        srows = tb['a_re'].shape[0]
        bu_re = _make_gmm("s5_bre")(s5_u, tb['b_re']).reshape(t_all, srows, LANES)
        bu_im = _make_gmm("s5_bim")(s5_u, tb['b_im']).reshape(t_all, srows, LANES)
        s_re, s_im = _make_s5_scan("s5_scan")(bu_re, bu_im, tb['a_re'], tb['a_im'])
        y_re = _make_gmm("s5_cre")(s_re.reshape(t_all, srows * LANES), tb['c_re'])
        y_im = _make_gmm("s5_cim")(s_im.reshape(t_all, srows * LANES), tb['c_im'])
        (v0,) = row("s5_act", 's5_act', 3, 1, [BRANCH], 416)(y_re, y_im, s5_u, w['s5_d'][l][None])
        lin = _mm("s5_glu_mm", v0, mats['glu'][l], w['c_glu'][l])
        (y_a,) = row("s5_glu", 's5_glu', 3, 1, [BRANCH], 416)(v0, lin, s5_z, w['s5_b_glu'][l][None])

        xbc = _make_conv_rowwise("ssd_conv", fns['ssd_conv'], 2, ssd_xbc.shape[1], 416, True)(
            ssd_xbc, w['ssd_conv_w'][l], w['ssd_conv_b'][l][None])
        a_c = jnp.pad(-jnp.exp(w['ssd_a_log'][l]), (0, LANES - SSD_HEADS))[None]
        d_exp = jnp.repeat(w['ssd_d'][l], SSD_HEAD)[None]
        (y_ssd,) = _make_chunk_scan("ssd_scan", _ssd_chunk, (SSD_HEADS // 2 * LANES, SSD_STATE), 2, 2, [BRANCH])(
            xbc, sm, a_c, d_exp)
        (y_b,) = row("ssd_norm", 'ssd_norm', 2, 1, [BRANCH], 416)(y_ssd, ssd_z, w['ssd_norm_g'][l][None])

        qkv = _make_conv_rowwise("gdn_conv", fns['gdn_conv'], 1, 3 * BRANCH, 208)(gdn_qkv, w['gdn_conv_w'][l])
        (o_gdn,) = _make_chunk_scan("gdn_scan", _gdn_chunk, (GDN_HEADS * GDN_HEAD, GDN_HEAD), 2, 0, [BRANCH])(qkv, sm)
        (y_c,) = row("gdn_norm", 'gdn_norm', 2, 1, [BRANCH], 416)(o_gdn, gdn_z, w['gdn_norm_g'][l][None])

        outs = [_mm(f"branch{k}", yk, mats['branch'][l, k], w['c_branch'][l, k])
                for k, yk in enumerate((y_a, y_b, y_c))]
        (merged,) = row("merge", 'merge', 4, 1, [d_model], 208)(*outs, gate, w['b_gate'][l].reshape(1, 3 * d_model))
        o = _mm("w_out", merged, mats['out'][l], w['c_out'][l])
        (h,) = row("ln_layer", 'ln_layer', 2, 2, [d_model], 416, True)(h, o, w['ln_g'][l][None], w['ln_b'][l][None])

    tgt = jnp.concatenate([jnp.zeros((first, d_model), F32), target], axis=0)
    (rows_loss,) = row("loss", 'loss', 2, 0, [LANES], 416, True)(h, tgt)
    return jnp.sum(rows_loss)


def _in_overlaps(d_model, n_loc):
    offs = [0]
    for wd in _in_widths(d_model):
        offs.append(offs[-1] + wd)
    out = []
    for i in range(len(offs) - 1):
        c0, c1 = offs[i], offs[i + 1]
        segs = []
        for k in range(N_DEV):
            g0, g1 = max(c0, k * n_loc), min(c1, (k + 1) * n_loc)
            if g0 < g1:
                segs.append((k, g0 - k * n_loc, g1 - k * n_loc, g0 - c0))
        out.append(segs)
    return out


MAT_WEIGHTS = ['s5_w_glu', 'w_branch', 'w_out']


def _step(x, target, w_loc, m_loc, v_loc):
    rest = [n for n in WEIGHTS if n in SHARD_DIM and n != 'w_in']
    small = [n for n in rest if n not in MAT_WEIGHTS]
    repl = [n for n in WEIGHTS if n not in SHARD_DIM]
    size = lambda names: sum(int(w_loc[n].size) for n in names)
    rows_mat, rows_small = _rows_for(size(MAT_WEIGHTS)), _rows_for(size(small))
    n_rp = size(repl) + 1
    rows_rp = _rows_for(n_rp)
    depth, d_model, n_loc = w_loc['w_in'].shape
    overlaps = _in_overlaps(d_model, n_loc)

    mat_flat = _pad_flat(jnp.concatenate([w_loc[n].reshape(-1) for n in MAT_WEIGHTS]), rows_mat).astype(BF16)
    small_flat = _pad_flat(jnp.concatenate([w_loc[n].reshape(-1) for n in small]), rows_small)
    g_in, g_mat, g_small = _gather([w_loc['w_in'].astype(BF16), mat_flat, small_flat], "gather_weights")
    full = {}
    for names, buf in ((MAT_WEIGHTS, g_mat.reshape(N_DEV, -1)), (small, g_small.reshape(N_DEV, -1))):
        off = 0
        for n in names:
            sz = int(w_loc[n].size)
            full[n] = _join_shards(buf[:, off:off + sz], w_loc[n].shape, SHARD_DIM[n])
            off += sz
    mats = dict(glu=full['s5_w_glu'], branch=full['w_branch'], out=full['w_out'], **{
        'in': [[jnp.concatenate([g_in[k, l, :, lo:hi] for k, lo, hi, _ in segs], axis=1) for segs in overlaps]
               for l in range(depth)]})
    w_diff = {n: w_loc[n] for n in repl}
    w_diff.update({n: full[n] for n in small})
    w_diff['c_in'] = [[jnp.zeros((d_model, wd), F32) for wd in _in_widths(d_model)] for _ in range(depth)]
    w_diff['c_glu'] = jnp.zeros(full['s5_w_glu'].shape, F32)
    w_diff['c_branch'] = jnp.zeros(full['w_branch'].shape, F32)
    w_diff['c_out'] = jnp.zeros(full['w_out'].shape, F32)

    loss, (g_w, g_x) = jax.value_and_grad(_local_loss, argnums=(0, 2))(w_diff, mats, x[0], target[0])
    g_w['s5_w_glu'], g_w['w_branch'], g_w['w_out'] = g_w['c_glu'], g_w['c_branch'], g_w['c_out']

    def shard_rows(names, rows_n):
        buf = jnp.concatenate([_split_shards(g_w[n], SHARD_DIM[n]) for n in names], axis=1)
        return jnp.pad(buf, ((0, 0), (0, rows_n * LANES - buf.shape[1])))

    send_in = jnp.stack([jnp.stack([
        jnp.concatenate([g_w['c_in'][l][i][:, plo:plo + hi - lo]
                         for i, segs in enumerate(overlaps) for (kk, lo, hi, plo) in segs if kk == k], axis=1)
        for l in range(depth)]) for k in range(N_DEV)]).astype(BF16)
    send_mat = shard_rows(MAT_WEIGHTS, rows_mat).astype(BF16).reshape(N_DEV, rows_mat, LANES)
    rp_vec = jnp.concatenate([g_w[n].reshape(-1) for n in repl] + [loss.reshape(1)])
    rp_vec = jnp.pad(rp_vec, (0, rows_rp * LANES - n_rp))
    rows_f32 = rows_small + rows_rp
    send_f32 = jnp.concatenate([shard_rows(small, rows_small),
                                jnp.broadcast_to(rp_vec[None], (N_DEV, rows_rp * LANES))], axis=1)
    parts_in, parts_mat, parts_f32 = _exchange([send_in, send_mat, send_f32.reshape(N_DEV, rows_f32, LANES)],
                                               "exchange_grads")

    def flat_mat(src):
        return _pad_flat(jnp.concatenate([src[n].reshape(-1) for n in MAT_WEIGHTS]), rows_mat)

    def flat_f32(src):
        sm_vec = jnp.concatenate([src[n].reshape(-1) for n in small])
        sm_vec = jnp.pad(sm_vec, (0, rows_small * LANES - sm_vec.shape[0]))
        rep = jnp.concatenate([src[n].reshape(-1) for n in repl] + [jnp.ones((1,), F32)])
        rep = jnp.pad(rep, (0, rows_rp * LANES - n_rp))
        return jnp.concatenate([sm_vec, rep]).reshape(rows_f32, LANES)

    in_outs = _adamw_stacked(parts_in, w_loc['w_in'], m_loc['w_in'], v_loc['w_in'])
    mat_outs = _adamw_flat(parts_mat, flat_mat(w_loc), flat_mat(m_loc), flat_mat(v_loc), "adamw_mat")
    f32_outs = _adamw_flat(parts_f32, flat_f32(w_loc), flat_f32(m_loc), flat_f32(v_loc), "adamw_rest")

    def unflat(big, mat_buf, f32_buf):
        out = {'w_in': big}
        for names, vec, o in ((MAT_WEIGHTS, mat_buf.reshape(-1), 0), (small, f32_buf.reshape(-1), 0),
                              (repl, f32_buf.reshape(-1), rows_small * LANES)):
            for n in names:
                sz = int(w_loc[n].size)
                out[n] = vec[o:o + sz].reshape(w_loc[n].shape)
                o += sz
        return out, f32_buf.reshape(-1)[rows_small * LANES + n_rp - 1]

    (grads, loss_total), (deltas, _), (new_m, _), (new_v, _) = [
        unflat(b, a, f) for b, a, f in zip(in_outs, mat_outs, f32_outs)]
    return (loss_total, g_x[None], *[grads[n] for n in WEIGHTS], *[deltas[n] for n in WEIGHTS],
            *[new_m[n] for n in WEIGHTS], *[new_v[n] for n in WEIGHTS])


def kernel(x, meta, ln_in_g, ln_in_b, w_in, s5_a_re, s5_a_im, s5_log_step, s5_b_re, s5_b_im, s5_c_re, s5_c_im, s5_d, s5_w_glu, s5_b_glu, ssd_conv_w, ssd_conv_b, ssd_dt_bias, ssd_a_log, ssd_d, ssd_norm_g, gdn_conv_w, gdn_dt_bias, gdn_a_log, gdn_norm_g, w_branch, b_gate, w_out, ln_g, ln_b, loss_target, m_meta, m_ln_in_g, m_ln_in_b, m_w_in, m_s5_a_re, m_s5_a_im, m_s5_log_step, m_s5_b_re, m_s5_b_im, m_s5_c_re, m_s5_c_im, m_s5_d, m_s5_w_glu, m_s5_b_glu, m_ssd_conv_w, m_ssd_conv_b, m_ssd_dt_bias, m_ssd_a_log, m_ssd_d, m_ssd_norm_g, m_gdn_conv_w, m_gdn_dt_bias, m_gdn_a_log, m_gdn_norm_g, m_w_branch, m_b_gate, m_w_out, m_ln_g, m_ln_b, v_meta, v_ln_in_g, v_ln_in_b, v_w_in, v_s5_a_re, v_s5_a_im, v_s5_log_step, v_s5_b_re, v_s5_b_im, v_s5_c_re, v_s5_c_im, v_s5_d, v_s5_w_glu, v_s5_b_glu, v_ssd_conv_w, v_ssd_conv_b, v_ssd_dt_bias, v_ssd_a_log, v_ssd_d, v_ssd_norm_g, v_gdn_conv_w, v_gdn_dt_bias, v_gdn_a_log, v_gdn_norm_g, v_w_branch, v_b_gate, v_w_out, v_ln_g, v_ln_b):
    w_loc = dict(zip(WEIGHTS, (meta, ln_in_g, ln_in_b, w_in, s5_a_re, s5_a_im, s5_log_step, s5_b_re, s5_b_im, s5_c_re, s5_c_im, s5_d, s5_w_glu, s5_b_glu, ssd_conv_w, ssd_conv_b, ssd_dt_bias, ssd_a_log, ssd_d, ssd_norm_g, gdn_conv_w, gdn_dt_bias, gdn_a_log, gdn_norm_g, w_branch, b_gate, w_out, ln_g, ln_b)))
    m_loc = dict(zip(WEIGHTS, (m_meta, m_ln_in_g, m_ln_in_b, m_w_in, m_s5_a_re, m_s5_a_im, m_s5_log_step, m_s5_b_re, m_s5_b_im, m_s5_c_re, m_s5_c_im, m_s5_d, m_s5_w_glu, m_s5_b_glu, m_ssd_conv_w, m_ssd_conv_b, m_ssd_dt_bias, m_ssd_a_log, m_ssd_d, m_ssd_norm_g, m_gdn_conv_w, m_gdn_dt_bias, m_gdn_a_log, m_gdn_norm_g, m_w_branch, m_b_gate, m_w_out, m_ln_g, m_ln_b)))
    v_loc = dict(zip(WEIGHTS, (v_meta, v_ln_in_g, v_ln_in_b, v_w_in, v_s5_a_re, v_s5_a_im, v_s5_log_step, v_s5_b_re, v_s5_b_im, v_s5_c_re, v_s5_c_im, v_s5_d, v_s5_w_glu, v_s5_b_glu, v_ssd_conv_w, v_ssd_conv_b, v_ssd_dt_bias, v_ssd_a_log, v_ssd_d, v_ssd_norm_g, v_gdn_conv_w, v_gdn_dt_bias, v_gdn_a_log, v_gdn_norm_g, v_w_branch, v_b_gate, v_w_out, v_ln_g, v_ln_b)))
    return _step(x, loss_target, w_loc, m_loc, v_loc)
```

```python
import functools
import math

import jax
import jax.numpy as jnp
from jax import lax
from jax.experimental import pallas as pl
from jax.experimental.pallas import tpu as pltpu

F32 = jnp.float32
BF16 = jnp.bfloat16

N_DEV = 8
LANES = 128
SUBLANES = 8
VMEM_LIMIT = 56 * 1024 * 1024
FLAT_ROWS = 1024

CHUNK = 64
CONV_K = 4
S5_GROUP = 16
S5_STATE = 64
S5_BLOCK_GROUPS = 8
SSD_HEAD = 64
SSD_HEADS = 12
SSD_GROUPS = 2
SSD_STATE = 128
GDN_HEAD = 128
GDN_HEADS = 6
BRANCH = 768
LN_EPS = 1e-5

ADAM_LR = 0.001
ADAM_B1 = 0.9
ADAM_B2 = 0.999
ADAM_EPS = 1e-08
ADAM_WD = 0.01
ADAM_STEP = 10

WEIGHTS = ['meta', 'ln_in_g', 'ln_in_b', 'w_in', 's5_a_re', 's5_a_im', 's5_log_step', 's5_b_re', 's5_b_im',
           's5_c_re', 's5_c_im', 's5_d', 's5_w_glu', 's5_b_glu', 'ssd_conv_w', 'ssd_conv_b', 'ssd_dt_bias',
           'ssd_a_log', 'ssd_d', 'ssd_norm_g', 'gdn_conv_w', 'gdn_dt_bias', 'gdn_a_log', 'gdn_norm_g',
           'w_branch', 'b_gate', 'w_out', 'ln_g', 'ln_b']
SHARD_DIM = {'meta': 1, 'w_in': 2, 's5_w_glu': 1, 'ssd_conv_w': 2, 'gdn_conv_w': 2, 'w_branch': 3, 'b_gate': 2,
             'w_out': 1}


def _params(sem):
    return pltpu.CompilerParams(dimension_semantics=sem, vmem_limit_bytes=VMEM_LIMIT)


def _row_tile(m, cap):
    best = None
    for t in range(SUBLANES, min(m, cap) + 1, SUBLANES):
        if m % t == 0:
            best = t
    return best if best is not None else m


def _col_tile(n, cap):
    best = None
    for t in range(LANES, min(n, cap) + 1, LANES):
        if n % t == 0:
            best = t
    return best if best is not None else n


def _any_tile(m, cap):
    best = 1
    for t in range(1, min(m, cap) + 1):
        if m % t == 0:
            best = t
    return best


def _mm_fwd(a, b, name):
    m, _ = a.shape
    g, k, n = b.shape
    tm, tn = _row_tile(m, 832), _col_tile(n, 1024)
    nj = n // tn

    def body(a_ref, b_ref, o_ref):
        o_ref[...] = jnp.dot(a_ref[...].astype(BF16), b_ref[0].astype(BF16), preferred_element_type=F32)

    return pl.pallas_call(
        body, name=name, grid=(g, m // tm, nj),
        in_specs=[pl.BlockSpec((tm, k), lambda gi, i, j: (i, gi)),
                  pl.BlockSpec((1, k, tn), lambda gi, i, j: (gi, 0, j))],
        out_specs=pl.BlockSpec((tm, tn), lambda gi, i, j: (i, gi * nj + j)),
        out_shape=jax.ShapeDtypeStruct((m, g * n), F32),
        compiler_params=_params(("arbitrary", "arbitrary", "arbitrary")),
    )(a, b)


def _mm_da(ct, b, name):
    m, _ = ct.shape
    g, k, n = b.shape
    tm, tk = _row_tile(m, 832 if n <= 1536 else 416), _col_tile(k, 1024)
    nk = k // tk

    def body(c_ref, b_ref, o_ref):
        o_ref[...] = lax.dot_general(c_ref[...].astype(BF16), b_ref[0].astype(BF16), (((1,), (1,)), ((), ())),
                                     preferred_element_type=F32)

    return pl.pallas_call(
        body, name=name, grid=(g, m // tm, nk),
        in_specs=[pl.BlockSpec((tm, n), lambda gi, i, j: (i, gi)),
                  pl.BlockSpec((1, tk, n), lambda gi, i, j: (gi, j, 0))],
        out_specs=pl.BlockSpec((tm, tk), lambda gi, i, j: (i, gi * nk + j)),
        out_shape=jax.ShapeDtypeStruct((m, g * k), F32),
        compiler_params=_params(("arbitrary", "arbitrary", "arbitrary")),
    )(ct, b)


def _mm_db(a, ct, g, k, n, name):
    m = a.shape[0]
    tm, tk, tn = _row_tile(m, 832), _col_tile(k, 1024), _col_tile(n, 1280)
    nk, nn = k // tk, n // tn

    def body(a_ref, c_ref, o_ref):
        @pl.when(pl.program_id(3) == 0)
        def _():
            o_ref[...] = jnp.zeros_like(o_ref)

        o_ref[0] += lax.dot_general(a_ref[...].astype(BF16), c_ref[...].astype(BF16), (((0,), (0,)), ((), ())),
                                    preferred_element_type=F32)

    return pl.pallas_call(
        body, name=name, grid=(g, nk, nn, m // tm),
        in_specs=[pl.BlockSpec((tm, tk), lambda gi, i, j, r: (r, gi * nk + i)),
                  pl.BlockSpec((tm, tn), lambda gi, i, j, r: (r, gi * nn + j))],
        out_specs=pl.BlockSpec((1, tk, tn), lambda gi, i, j, r: (gi, i, j)),
        out_shape=jax.ShapeDtypeStruct((g, k, n), F32),
        compiler_params=_params(("arbitrary", "arbitrary", "arbitrary", "arbitrary")),
    )(a, ct)


def _make_gmm(name):
    @jax.custom_vjp
    def gmm(a, b):
        return _mm_fwd(a, b, name + "_fwd")

    def fwd(a, b):
        return _mm_fwd(a, b, name + "_fwd"), (a, b)

    def bwd(res, ct):
        a, b = res
        g, k, n = b.shape
        return _mm_da(ct, b, name + "_da"), _mm_db(a, ct, g, k, n, name + "_db")

    gmm.defvjp(fwd, bwd)
    return gmm


def _mm(name, a, w, carrier):
    @jax.custom_vjp
    def mm(a, w3, carrier3):
        return _mm_fwd(a, w3, name + "_fwd")

    def fwd(a, w3, carrier3):
        return _mm_fwd(a, w3, name + "_fwd"), (a, w3)

    def bwd(res, ct):
        a, w3 = res
        g, k, n = w3.shape
        return _mm_da(ct, w3, name + "_da"), jnp.zeros_like(w3), _mm_db(a, ct, g, k, n, name + "_db")

    mm.defvjp(fwd, bwd)
    return mm(a, w[None], carrier[None])


@jax.custom_vjp
def _wdot(x, w, carrier):
    return _dot(x, w)


def _wdot_fwd(x, w, carrier):
    return _dot(x, w), (x, w)


def _wdot_bwd(res, ct):
    x, w = res
    return _dot_nt(ct, w), jnp.zeros_like(w), _dot_tn(x, ct)


_wdot.defvjp(_wdot_fwd, _wdot_bwd)


def _make_rowwise(name, fn, n_row, n_par, out_widths, tm_cap, use_ridx=False, n_wt=0):
    n_out = len(out_widths)
    n_in = n_row + n_par + n_wt

    def bind(tm):
        if not use_ridx:
            return fn
        ridx = pl.program_id(0) * tm + lax.broadcasted_iota(jnp.int32, (tm, 1), 0)
        return functools.partial(fn, ridx)

    def specs(args, tm):
        rows = [pl.BlockSpec((tm, a.shape[1]), lambda i: (i, 0)) for a in args[:n_row]]
        pars = [pl.BlockSpec(a.shape, lambda i: (0, 0)) for a in args[n_row:n_in]]
        return rows, pars

    def fwd_call(*args):
        t = args[0].shape[0]
        tm = _row_tile(t, tm_cap)
        rows, pars = specs(args, tm)

        def body(*refs):
            vals = [r[...] for r in refs[:n_row + n_par]]
            mats = [functools.partial(lambda x, w: _dot(x, w), w=r[...]) for r in refs[n_row + n_par:n_in]]
            res = bind(tm)(*vals, *mats)
            for o_ref, r in zip(refs[n_in:], res):
                o_ref[...] = r

        outs = pl.pallas_call(
            body, name=name + "_fwd", grid=(t // tm,), in_specs=rows + pars,
            out_specs=[pl.BlockSpec((tm, w), lambda i: (i, 0)) for w in out_widths],
            out_shape=[jax.ShapeDtypeStruct((t, w), F32) for w in out_widths],
            compiler_params=_params(("arbitrary",)),
        )(*args)
        return tuple(outs)

    def bwd_call(args, cts):
        t = args[0].shape[0]
        tm = _row_tile(t, tm_cap)
        rows, pars = specs(args, tm)
        ct_specs = [pl.BlockSpec((tm, w), lambda i: (i, 0)) for w in out_widths]
        n_diff = n_row + n_par

        def body(*refs):
            vals = [r[...] for r in refs[:n_diff]]
            wts = [r[...] for r in refs[n_diff:n_in]]
            ct_vals = tuple(r[...] for r in refs[n_in:n_in + n_out])
            d_refs = refs[n_in + n_out:]
            f = bind(tm)

            def g(*a):
                mats = [functools.partial(lambda x, w, c: _wdot(x, w, c), w=w, c=c)
                        for w, c in zip(wts, a[n_diff:])]
                return tuple(f(*a[:n_diff], *mats))

            _, vjp = jax.vjp(g, *vals, *[jnp.zeros(w.shape, F32) for w in wts])
            grads = vjp(ct_vals)
            for i in range(n_row):
                d_refs[i][...] = grads[i]
            if n_par + n_wt:
                @pl.when(pl.program_id(0) == 0)
                def _():
                    for r in d_refs[n_row:]:
                        r[...] = jnp.zeros_like(r)

                for r, gr in zip(d_refs[n_row:], grads[n_row:]):
                    r[...] += gr

        outs = pl.pallas_call(
            body, name=name + "_bwd", grid=(t // tm,), in_specs=rows + pars + ct_specs,
            out_specs=rows + pars,
            out_shape=[jax.ShapeDtypeStruct(a.shape, F32) for a in args],
            compiler_params=_params(("arbitrary",)),
        )(*args, *cts)
        return tuple(outs)

    @jax.custom_vjp
    def op(*args):
        return fwd_call(*args[:n_in])

    def fwd(*args):
        return fwd_call(*args[:n_in]), args[:n_in]

    def bwd(args, cts):
        grads = bwd_call(args, cts)
        return grads[:n_row + n_par] + tuple(jnp.zeros_like(a) for a in args[n_row + n_par:]) + grads[n_row + n_par:]

    op.defvjp(fwd, bwd)
    return op


HALO = SUBLANES


def _make_conv_rowwise(name, fn, n_par, out_width, tm_cap, use_ridx=False):
    def bind(ridx):
        return functools.partial(fn, ridx) if use_ridx else fn

    def stage(x_ref, halo_ref, xs, first):
        xs[0:HALO, :] = jnp.where(first, 0.0, halo_ref[...])
        xs[HALO:, :] = x_ref[...]

    def taps(xs, tm):
        return [xs[pl.ds(HALO - (CONV_K - 1) + j, tm), :] for j in range(CONV_K)]

    def fwd_call(x, *pars):
        t, wd = x.shape
        tm = _row_tile(t, tm_cap)
        per = tm // HALO

        def body(*refs):
            x_ref, halo_ref = refs[:2]
            par_refs, o_ref, xs = refs[2:2 + n_par], refs[2 + n_par], refs[-1]
            i = pl.program_id(0)
            stage(x_ref, halo_ref, xs, i == 0)
            ridx = i * tm + lax.broadcasted_iota(jnp.int32, (tm, 1), 0)
            (o_ref[...],) = bind(ridx)(*taps(xs, tm), *[r[...] for r in par_refs])

        return pl.pallas_call(
            body, name=name + "_fwd", grid=(t // tm,),
            in_specs=[pl.BlockSpec((tm, wd), lambda i: (i, 0)),
                      pl.BlockSpec((HALO, wd), lambda i: (jnp.maximum(i * per - 1, 0), 0))]
            + [pl.BlockSpec(p.shape, lambda i: (0, 0)) for p in pars],
            out_specs=pl.BlockSpec((tm, out_width), lambda i: (i, 0)),
            out_shape=jax.ShapeDtypeStruct((t, out_width), F32),
            scratch_shapes=[pltpu.VMEM((tm + HALO, wd), F32)],
            compiler_params=_params(("arbitrary",)),
        )(x, x, *pars)

    def bwd_call(x, pars, ct):
        t, wd = x.shape
        tm = _row_tile(t, tm_cap)
        per = tm // HALO
        nb = t // tm

        def body(*refs):
            x_ref, halo_ref = refs[:2]
            par_refs, ct_ref = refs[2:2 + n_par], refs[2 + n_par]
            dx_ref, dpar_refs = refs[3 + n_par], refs[4 + n_par:4 + 2 * n_par]
            xs, ds, carry = refs[-3:]
            step = pl.program_id(0)
            blk = nb - 1 - step

            @pl.when(step == 0)
            def _():
                carry[...] = jnp.zeros_like(carry)
                for r in dpar_refs:
                    r[...] = jnp.zeros_like(r)

            stage(x_ref, halo_ref, xs, blk == 0)
            ridx = blk * tm + lax.broadcasted_iota(jnp.int32, (tm, 1), 0)
            f = bind(ridx)
            _, vjp = jax.vjp(lambda *a: tuple(f(*a)), *taps(xs, tm), *[r[...] for r in par_refs])
            grads = vjp((ct_ref[...],))
            ds[...] = jnp.zeros_like(ds)
            for j in range(CONV_K):
                ds[pl.ds(HALO - (CONV_K - 1) + j, tm), :] += grads[j]
            ds[pl.ds(tm, HALO), :] += carry[...]
            dx_ref[...] = ds[HALO:, :]
            carry[...] = ds[0:HALO, :]
            for r, g in zip(dpar_refs, grads[CONV_K:]):
                r[...] += g

        rev = lambda i: (nb - 1 - i, 0)
        outs = pl.pallas_call(
            body, name=name + "_bwd", grid=(nb,),
            in_specs=[pl.BlockSpec((tm, wd), rev),
                      pl.BlockSpec((HALO, wd), lambda i: (jnp.maximum((nb - 1 - i) * per - 1, 0), 0))]
            + [pl.BlockSpec(p.shape, lambda i: (0, 0)) for p in pars]
            + [pl.BlockSpec((tm, out_width), rev)],
            out_specs=[pl.BlockSpec((tm, wd), rev)] + [pl.BlockSpec(p.shape, lambda i: (0, 0)) for p in pars],
            out_shape=[jax.ShapeDtypeStruct(x.shape, F32)] + [jax.ShapeDtypeStruct(p.shape, F32) for p in pars],
            scratch_shapes=[pltpu.VMEM((tm + HALO, wd), F32), pltpu.VMEM((tm + HALO, wd), F32),
                            pltpu.VMEM((HALO, wd), F32)],
            compiler_params=_params(("arbitrary",)),
        )(x, x, *pars, ct)
        return tuple(outs)

    @jax.custom_vjp
    def op(x, *pars):
        return fwd_call(x, *pars)

    def fwd(x, *pars):
        return fwd_call(x, *pars), (x, pars)

    def bwd(res, ct):
        x, pars = res
        return bwd_call(x, pars, ct)

    op.defvjp(fwd, bwd)
    return op


def _make_chunk_scan(name, fn, state_shape, n_seq, n_par, out_widths):
    n_out = len(out_widths)
    zeros_idx = (0,) * len(state_shape)

    def fwd_call(*args):
        t = args[0].shape[0]
        nc = t // CHUNK
        seq_specs = [pl.BlockSpec((CHUNK, a.shape[1]), lambda c: (c, 0)) for a in args[:n_seq]]
        par_specs = [pl.BlockSpec(a.shape, lambda c: (0, 0)) for a in args[n_seq:]]

        def body(*refs):
            ins = refs[:n_seq + n_par]
            out_refs = refs[n_seq + n_par:n_seq + n_par + n_out]
            states_ref = refs[n_seq + n_par + n_out]
            st = refs[-1]

            @pl.when(pl.program_id(0) == 0)
            def _():
                st[...] = jnp.zeros_like(st)

            s0 = st[...]
            states_ref[0] = s0
            res = fn(s0, *[r[...] for r in ins])
            st[...] = res[0]
            for o_ref, r in zip(out_refs, res[1:]):
                o_ref[...] = r

        outs = pl.pallas_call(
            body, name=name + "_fwd", grid=(nc,), in_specs=seq_specs + par_specs,
            out_specs=[pl.BlockSpec((CHUNK, w), lambda c: (c, 0)) for w in out_widths]
            + [pl.BlockSpec((1,) + state_shape, lambda c: (c,) + zeros_idx)],
            out_shape=[jax.ShapeDtypeStruct((t, w), F32) for w in out_widths]
            + [jax.ShapeDtypeStruct((nc,) + state_shape, F32)],
            scratch_shapes=[pltpu.VMEM(state_shape, F32)],
            compiler_params=_params(("arbitrary",)),
        )(*args)
        return tuple(outs[:n_out]), outs[n_out]

    def bwd_call(args, states, cts):
        t = args[0].shape[0]
        nc = t // CHUNK
        rev = lambda c: (nc - 1 - c, 0)
        seq_specs = [pl.BlockSpec((CHUNK, a.shape[1]), rev) for a in args[:n_seq]]
        par_specs = [pl.BlockSpec(a.shape, lambda c: (0, 0)) for a in args[n_seq:]]
        ct_specs = [pl.BlockSpec((CHUNK, w), rev) for w in out_widths]
        st_spec = pl.BlockSpec((1,) + state_shape, lambda c: (nc - 1 - c,) + zeros_idx)
        n_in = n_seq + n_par

        def body(*refs):
            vals = [r[...] for r in refs[:n_in]]
            s0 = refs[n_in][0]
            ct_vals = tuple(r[...] for r in refs[n_in + 1:n_in + 1 + n_out])
            d_refs = refs[n_in + 1 + n_out:-1]
            dst = refs[-1]

            @pl.when(pl.program_id(0) == 0)
            def _():
                dst[...] = jnp.zeros_like(dst)
                for j in range(n_par):
                    d_refs[n_seq + j][...] = jnp.zeros_like(d_refs[n_seq + j])

            _, vjp = jax.vjp(lambda *a: tuple(fn(*a)), s0, *vals)
            grads = vjp((dst[...],) + ct_vals)
            dst[...] = grads[0]
            for i in range(n_seq):
                d_refs[i][...] = grads[1 + i]
            for j in range(n_par):
                d_refs[n_seq + j][...] += grads[1 + n_seq + j]

        outs = pl.pallas_call(
            body, name=name + "_bwd", grid=(nc,), in_specs=seq_specs + par_specs + [st_spec] + ct_specs,
            out_specs=seq_specs + par_specs,
            out_shape=[jax.ShapeDtypeStruct(a.shape, F32) for a in args],
            scratch_shapes=[pltpu.VMEM(state_shape, F32)],
            compiler_params=_params(("arbitrary",)),
        )(*args, states, *cts)
        return tuple(outs)

    @jax.custom_vjp
    def op(*args):
        return fwd_call(*args)[0]

    def fwd(*args):
        outs, states = fwd_call(*args)
        return outs, (args, states)

    def bwd(res, cts):
        args, states = res
        return bwd_call(args, states, cts)

    op.defvjp(fwd, bwd)
    return op


def _s5_scan_fwd(bre, bim, are, aim, name):
    t, r, _ = bre.shape
    tb = _any_tile(t, 208)
    blk = pl.BlockSpec((tb, r, LANES), lambda i: (i, 0, 0))
    par = pl.BlockSpec((r, LANES), lambda i: (0, 0))

    def body(bre_ref, bim_ref, are_ref, aim_ref, sre_ref, sim_ref, st):
        @pl.when(pl.program_id(0) == 0)
        def _():
            st[...] = jnp.zeros_like(st)

        ar, ai = are_ref[...], aim_ref[...]

        def step(k, carry):
            sr, si = carry
            nr = ar * sr - ai * si + bre_ref[k]
            ni = ar * si + ai * sr + bim_ref[k]
            sre_ref[k] = nr
            sim_ref[k] = ni
            return nr, ni

        sr, si = lax.fori_loop(0, tb, step, (st[0], st[1]), unroll=4)
        st[0] = sr
        st[1] = si

    return pl.pallas_call(
        body, name=name, grid=(t // tb,), in_specs=[blk, blk, par, par], out_specs=[blk, blk],
        out_shape=[jax.ShapeDtypeStruct(bre.shape, F32)] * 2,
        scratch_shapes=[pltpu.VMEM((2, r, LANES), F32)],
        compiler_params=_params(("arbitrary",)),
    )(bre, bim, are, aim)


def _s5_scan_bwd(dsr, dsi, sre, sim, are, aim, name):
    t, r, _ = sre.shape
    tb = _any_tile(t, 208)
    nb = t // tb
    blk = pl.BlockSpec((tb, r, LANES), lambda i: (nb - 1 - i, 0, 0))
    par = pl.BlockSpec((r, LANES), lambda i: (0, 0))

    def body(dsr_ref, dsi_ref, sre_ref, sim_ref, are_ref, aim_ref, gre_ref, gim_ref, dar_ref, dai_ref, st):
        @pl.when(pl.program_id(0) == 0)
        def _():
            st[...] = jnp.zeros_like(st)
            dar_ref[...] = jnp.zeros_like(dar_ref)
            dai_ref[...] = jnp.zeros_like(dai_ref)

        ar, ai = are_ref[...], aim_ref[...]

        def step(k, carry):
            gr, gi, dar, dai = carry
            q = tb - 1 - k
            s_r, s_i = sre_ref[q], sim_ref[q]
            dar = dar + gr * s_r + gi * s_i
            dai = dai + gi * s_r - gr * s_i
            ngr = dsr_ref[q] + ar * gr + ai * gi
            ngi = dsi_ref[q] + ar * gi - ai * gr
            gre_ref[q] = ngr
            gim_ref[q] = ngi
            return ngr, ngi, dar, dai

        gr, gi, dar, dai = lax.fori_loop(0, tb, step, (st[0], st[1], dar_ref[...], dai_ref[...]), unroll=4)
        st[0] = gr
        st[1] = gi
        dar_ref[...] = dar
        dai_ref[...] = dai

    return pl.pallas_call(
        body, name=name, grid=(nb,), in_specs=[blk, blk, blk, blk, par, par], out_specs=[blk, blk, par, par],
        out_shape=[jax.ShapeDtypeStruct(sre.shape, F32)] * 2 + [jax.ShapeDtypeStruct(are.shape, F32)] * 2,
        scratch_shapes=[pltpu.VMEM((2, r, LANES), F32)],
        compiler_params=_params(("arbitrary",)),
    )(dsr, dsi, sre, sim, are, aim)


def _make_s5_scan(name):
    @jax.custom_vjp
    def scan(bre, bim, are, aim):
        return tuple(_s5_scan_fwd(bre, bim, are, aim, name + "_fwd"))

    def fwd(bre, bim, are, aim):
        sre, sim = _s5_scan_fwd(bre, bim, are, aim, name + "_fwd")
        return (sre, sim), (sre, sim, are, aim)

    def bwd(res, cts):
        sre, sim, are, aim = res
        return tuple(_s5_scan_bwd(cts[0], cts[1], sre, sim, are, aim, name + "_bwd"))

    scan.defvjp(fwd, bwd)
    return scan


def _dot(a, b):
    return jnp.dot(a.astype(BF16), b.astype(BF16), preferred_element_type=F32)


def _dot_nt(a, b):
    return lax.dot_general(a.astype(BF16), b.astype(BF16), (((1,), (1,)), ((), ())), preferred_element_type=F32)


def _dot_tn(a, b):
    return lax.dot_general(a.astype(BF16), b.astype(BF16), (((0,), (0,)), ((), ())), preferred_element_type=F32)


def _dot_f32(a, b):
    return jnp.dot(a, b, precision=lax.Precision.HIGHEST, preferred_element_type=F32)


def _dot_tn_f32(a, b):
    return lax.dot_general(a, b, (((0,), (0,)), ((), ())), precision=lax.Precision.HIGHEST,
                           preferred_element_type=F32)


def _iota(shape, dim):
    return lax.broadcasted_iota(jnp.int32, shape, dim)


def _tri(strict=False):
    r, c = _iota((CHUNK, CHUNK), 0), _iota((CHUNK, CHUNK), 1)
    return (r > c) if strict else (r >= c)


def _silu(x):
    return x * jax.nn.sigmoid(x)


def _layer_norm(z, g, b):
    mu = jnp.mean(z, axis=-1, keepdims=True)
    var = jnp.mean(jnp.square(z - mu), axis=-1, keepdims=True)
    return (z - mu) * lax.rsqrt(var + LN_EPS) * g + b


def _ssd_chunk(state, xbc, sm, a_c, d_exp):
    width = SSD_HEADS * SSD_HEAD
    x = xbc[:, :width]
    lane = _iota((CHUNK, LANES), 1)
    dtc = jnp.where(lane < SSD_HEADS, sm, 0.0)
    low = _tri().astype(F32)
    eye = (_iota((CHUNK, CHUNK), 0) == _iota((CHUNK, CHUNK), 1)).astype(F32)
    head_col, head_row = _iota((LANES, width), 1), _iota((LANES, width), 0) * SSD_HEAD
    expand = ((head_col >= head_row) & (head_col < head_row + SSD_HEAD)).astype(F32)
    acum_c = _dot_f32(low, dtc * a_c)
    acum_ct = _dot_tn_f32(acum_c, eye)
    dt_exp = _dot_f32(dtc, expand)
    acum = _dot_f32(acum_c, expand)
    xd = x * dt_exp
    last = acum[CHUNK - 1:CHUNK, :]
    to_end = jnp.exp(last - acum)
    eac = jnp.exp(acum)
    causal = _tri()
    first_half = _iota((CHUNK, LANES), 1) < SSD_HEAD
    top_rows = _iota((LANES, LANES), 0) < SSD_HEAD
    pairs = range(SSD_HEADS // 2)
    grp = [(2 * p) // (SSD_HEADS // SSD_GROUPS) for p in pairs]
    cols = [slice(p * LANES, (p + 1) * LANES) for p in pairs]
    bg = [xbc[:, width + g * SSD_STATE: width + (g + 1) * SSD_STATE] for g in range(SSD_GROUPS)]
    cg = [xbc[:, width + (SSD_GROUPS + g) * SSD_STATE: width + (SSD_GROUPS + g + 1) * SSD_STATE]
          for g in range(SSD_GROUPS)]
    scores = [_dot_nt(cg[g], bg[g]) for g in range(SSD_GROUPS)]
    dec = [jnp.where(causal, jnp.exp(jnp.minimum(acum_c[:, h:h + 1] - acum_ct[h:h + 1, :], 0.0)), 0.0)
           for h in range(SSD_HEADS)]
    y_lo = [_dot(scores[grp[p]] * dec[2 * p], jnp.where(first_half, xd[:, cols[p]], 0.0)) for p in pairs]
    y_hi = [_dot(scores[grp[p]] * dec[2 * p + 1], jnp.where(first_half, 0.0, xd[:, cols[p]])) for p in pairs]
    s_prev = [state[p * LANES:(p + 1) * LANES, :] for p in pairs]
    y_off = [_dot_nt(cg[grp[p]], s_prev[p]) for p in pairs]
    s_add = [_dot_tn(xd[:, cols[p]] * to_end[:, cols[p]], bg[grp[p]]) for p in pairs]
    ys = [y_lo[p] + y_hi[p] + y_off[p] * eac[:, cols[p]] + x[:, cols[p]] * d_exp[:, cols[p]] for p in pairs]
    cd = [jnp.where(top_rows, jnp.exp(acum_c[CHUNK - 1:CHUNK, 2 * p:2 * p + 1]),
                    jnp.exp(acum_c[CHUNK - 1:CHUNK, 2 * p + 1:2 * p + 2])) for p in pairs]
    new_states = [s_prev[p] * cd[p] + s_add[p] for p in pairs]
    return jnp.concatenate(new_states, axis=0), jnp.concatenate(ys, axis=1)


def _gdn_chunk(state, qkv, sm):
    width = GDN_HEADS * GDN_HEAD
    g0, b0 = SSD_HEADS, SSD_HEADS + GDN_HEADS
    lane = _iota((CHUNK, LANES), 1)
    gc = jnp.where((lane >= g0) & (lane < b0), sm, 0.0)
    low = _tri().astype(F32)
    eye = (_iota((CHUNK, CHUNK), 0) == _iota((CHUNK, CHUNK), 1)).astype(F32)
    gcum = _dot_f32(low, gc)
    gcum_t = _dot_tn_f32(gcum, eye)
    causal, strict = _tri(), _tri(strict=True)
    heads = range(GDN_HEADS)
    q = [qkv[:, h * GDN_HEAD:(h + 1) * GDN_HEAD] for h in heads]
    k = [qkv[:, width + h * GDN_HEAD: width + (h + 1) * GDN_HEAD] for h in heads]
    v = [qkv[:, 2 * width + h * GDN_HEAD: 2 * width + (h + 1) * GDN_HEAD] for h in heads]
    beta = [sm[:, b0 + h:b0 + h + 1] for h in heads]
    gcol = [gcum[:, g0 + h:g0 + h + 1] for h in heads]
    glast = [gcum[CHUNK - 1:CHUNK, g0 + h:g0 + h + 1] for h in heads]
    gamma = [jnp.where(causal, jnp.exp(jnp.minimum(gcol[h] - gcum_t[g0 + h:g0 + h + 1, :], 0.0)), 0.0) for h in heads]
    kk = [_dot_nt(k[h], k[h]) for h in heads]
    qk = [_dot_nt(q[h], k[h]) for h in heads]
    egc = [jnp.exp(gcol[h]) for h in heads]
    nmat = [-jnp.where(strict, kk[h] * gamma[h] * beta[h], 0.0) for h in heads]
    sol = [jnp.concatenate([v[h] * beta[h], k[h] * (beta[h] * egc[h])], axis=1) for h in heads]
    for i in range(6):
        upd = [_dot(nmat[h], sol[h]) for h in heads]
        if i < 5:
            nmat = [_dot(nmat[h], nmat[h]) for h in heads]
        sol = [sol[h] + upd[h] for h in heads]
    s_prev = [state[h * GDN_HEAD:(h + 1) * GDN_HEAD, :] for h in heads]
    w_s = [_dot(sol[h][:, GDN_HEAD:], s_prev[h]) for h in heads]
    q_s = [_dot(q[h] * egc[h], s_prev[h]) for h in heads]
    v_new = [sol[h][:, :GDN_HEAD] - w_s[h] for h in heads]
    a_v = [_dot(qk[h] * gamma[h], v_new[h]) for h in heads]
    k_v = [_dot_tn(k[h] * jnp.exp(glast[h] - gcol[h]), v_new[h]) for h in heads]
    outs = [q_s[h] + a_v[h] for h in heads]
    new_states = [s_prev[h] * jnp.exp(glast[h]) + k_v[h] for h in heads]
    return jnp.concatenate(new_states, axis=0), jnp.concatenate(outs, axis=1)


def _row_fns(d_model, pad_rows, loss_rows, alpha):
    def keep(ridx, v):
        return jnp.where(ridx >= pad_rows, v, 0.0)

    def ln_in(ridx, h, g, b):
        return (keep(ridx, _layer_norm(h, g, b)),)

    def s5_tail(y_re, y_im, u, z, d, bias, glu, branch):
        v0 = jax.nn.gelu(y_re + y_im + d * u)
        return (branch(v0 * jax.nn.sigmoid(glu(v0) + bias) * _silu(z)),)

    def small_act(ridx, raw, bias, scale):
        lane = _iota(raw.shape, 1)
        sp = jax.nn.softplus(raw + bias)
        g0, b0 = SSD_HEADS, SSD_HEADS + GDN_HEADS
        out = jnp.where(lane < g0, sp, jnp.where(lane < b0, scale * sp,
                                                 jnp.where(lane < b0 + GDN_HEADS, jax.nn.sigmoid(raw), 0.0)))
        return (keep(ridx, out),)

    def conv(xs, w):
        acc = xs[0] * w[0:1, :]
        for j in range(1, CONV_K):
            acc = acc + xs[j] * w[j:j + 1, :]
        return acc

    def ssd_conv(ridx, x0, x1, x2, x3, w, b):
        return (keep(ridx, _silu(conv((x0, x1, x2, x3), w) + b)),)

    def ssd_tail(y, z, g, branch):
        v = y * _silu(z)
        return (branch(v * lax.rsqrt(jnp.mean(v * v, axis=-1, keepdims=True) + LN_EPS) * g),)

    def gdn_conv(x0, x1, x2, x3, w):
        a = _silu(conv((x0, x1, x2, x3), w))
        width = GDN_HEADS * GDN_HEAD
        parts = []
        for h in range(2 * GDN_HEADS):
            z = a[:, h * GDN_HEAD:(h + 1) * GDN_HEAD]
            z = z * lax.rsqrt(jnp.sum(z * z, axis=-1, keepdims=True) + 1e-6)
            parts.append(z * GDN_HEAD ** -0.5 if h < GDN_HEADS else z)
        parts.append(a[:, 2 * width:])
        return (jnp.concatenate(parts, axis=1),)

    def gdn_tail(o, z, g, branch):
        parts = []
        for h in range(GDN_HEADS):
            cols = slice(h * GDN_HEAD, (h + 1) * GDN_HEAD)
            oh = o[:, cols]
            oh = oh * lax.rsqrt(jnp.mean(oh * oh, axis=-1, keepdims=True) + LN_EPS) * g
            parts.append(oh * _silu(z[:, cols]))
        return (branch(jnp.concatenate(parts, axis=1)),)

    def merge_out(ridx, oa, ob, oc, gate, h, bias, g, b, w_out):
        acc = None
        for k, o in enumerate((oa, ob, oc)):
            cols = slice(k * d_model, (k + 1) * d_model)
            term = jax.nn.sigmoid(gate[:, cols] + bias[:, cols]) * o
            acc = term if acc is None else acc + term
        return (keep(ridx, _layer_norm(alpha * h + w_out(acc), g, b)),)

    def loss_rows_fn(ridx, h, tgt):
        row = 0.5 * jnp.mean(jnp.square(h - tgt), axis=-1, keepdims=True)
        row = jnp.where(ridx >= loss_rows, row, 0.0)
        lane = _iota((h.shape[0], LANES), 1)
        return (jnp.where(lane == 0, row, 0.0),)

    return dict(ln_in=ln_in, s5_tail=s5_tail, small_act=small_act, ssd_conv=ssd_conv, ssd_tail=ssd_tail,
                gdn_conv=gdn_conv, gdn_tail=gdn_tail, merge_out=merge_out, loss=loss_rows_fn)


def _exchange(srcs, name):
    n_arr = len(srcs)
    n_peer = N_DEV - 1
    flips = [(fx, fy, fc) for fx in (0, 1) for fy in (0, 1) for fc in (0, 1)][1:]
    blocks = [s.shape[1:] for s in srcs]

    def body(*refs):
        src_refs, out_refs = refs[:n_arr], refs[n_arr:2 * n_arr]
        send_sems, recv_sems, local_sems = refs[2 * n_arr:]
        x, y, c = lax.axis_index("x"), lax.axis_index("y"), lax.axis_index("c")
        me = 4 * x + 2 * y + c

        def pick(a, j):
            return src_refs[a].at[j]

        own = [pltpu.make_async_copy(pick(a, me), out_refs[a].at[me], local_sems.at[a]) for a in range(n_arr)]
        for cp in own:
            cp.start()
        copies = []
        for k, (fx, fy, fc) in enumerate(flips):
            px = 1 - x if fx else x
            py = 1 - y if fy else y
            pc = 1 - c if fc else c
            peer = 4 * px + 2 * py + pc
            for a in range(n_arr):
                cp = pltpu.make_async_remote_copy(
                    src_ref=pick(a, peer), dst_ref=out_refs[a].at[me],
                    send_sem=send_sems.at[a * n_peer + k], recv_sem=recv_sems.at[a * n_peer + k],
                    device_id=(px, py, pc), device_id_type=pl.DeviceIdType.MESH)
                cp.start()
                copies.append(cp)
        for cp in copies:
            cp.wait()
        for cp in own:
            cp.wait()

    return pl.pallas_call(
        body, name=name,
        in_specs=[pl.BlockSpec(memory_space=pl.ANY)] * n_arr, out_specs=[pl.BlockSpec(memory_space=pl.ANY)] * n_arr,
        out_shape=[jax.ShapeDtypeStruct((N_DEV,) + tuple(b), s.dtype) for b, s in zip(blocks, srcs)],
        scratch_shapes=[pltpu.SemaphoreType.DMA((n_arr * n_peer,)), pltpu.SemaphoreType.DMA((n_arr * n_peer,)),
                        pltpu.SemaphoreType.DMA((n_arr,))],
    )(*srcs)


def _gather(srcs, name):
    n_arr = len(srcs)
    n_sem = N_DEV - 1

    def body(*refs):
        src_refs, out_refs = refs[:n_arr], refs[n_arr:2 * n_arr]
        send_sems, recv_sems, local_sems = refs[2 * n_arr:]
        x, y, c = lax.axis_index("x"), lax.axis_index("y"), lax.axis_index("c")
        me, sibling = (x, y, c), (x, y, 1 - c)
        chips = [(1 - x, y), (x, 1 - y), (1 - x, 1 - y)]

        def slot(a, dev):
            return out_refs[a].at[4 * dev[0] + 2 * dev[1] + dev[2]]

        def copy(a, k, block, to, own=False):
            return pltpu.make_async_remote_copy(
                src_ref=src_refs[a] if own else slot(a, block), dst_ref=slot(a, block),
                send_sem=send_sems.at[a * n_sem + k], recv_sem=recv_sems.at[a * n_sem + k],
                device_id=to, device_id_type=pl.DeviceIdType.MESH)

        arrays = range(n_arr)
        mine = [pltpu.make_async_copy(src_refs[a], slot(a, me), local_sems.at[a]) for a in arrays]
        for cp in mine:
            cp.start()
        first = [copy(a, 0, me, sibling, own=True) for a in arrays]
        first += [copy(a, 1 + j, me, (*chip, c), own=True) for j, chip in enumerate(chips) for a in arrays]
        for cp in first:
            cp.start()
        passed = []
        for j, chip in enumerate(chips):
            for a in arrays:
                copy(a, 1 + j, (*chip, c), me).wait_recv()
                fwd = copy(a, 4 + j, (*chip, c), sibling)
                fwd.start()
                passed.append(fwd)
        for a in arrays:
            copy(a, 0, sibling, me).wait_recv()
        for j, chip in enumerate(chips):
            for a in arrays:
                copy(a, 4 + j, (*chip, 1 - c), me).wait_recv()
        for cp in first + passed:
            cp.wait_send()
        for cp in mine:
            cp.wait()

    return pl.pallas_call(
        body, name=name,
        in_specs=[pl.BlockSpec(memory_space=pl.ANY)] * n_arr, out_specs=[pl.BlockSpec(memory_space=pl.ANY)] * n_arr,
        out_shape=[jax.ShapeDtypeStruct((N_DEV,) + tuple(s.shape), s.dtype) for s in srcs],
        scratch_shapes=[pltpu.SemaphoreType.DMA((n_arr * n_sem,)), pltpu.SemaphoreType.DMA((n_arr * n_sem,)),
                        pltpu.SemaphoreType.DMA((n_arr,))],
    )(*srcs)


def _adamw_body(p_ref, w_ref, m_ref, v_ref, g_ref, d_ref, nm_ref, nv_ref):
    bc1 = 1.0 - ADAM_B1 ** ADAM_STEP
    bc2 = 1.0 - ADAM_B2 ** ADAM_STEP
    g = p_ref[0].astype(F32)
    for k in range(1, N_DEV):
        g = g + p_ref[k].astype(F32)
    nm = ADAM_B1 * m_ref[...] + (1.0 - ADAM_B1) * g
    nv = ADAM_B2 * v_ref[...] + (1.0 - ADAM_B2) * jnp.square(g)
    m_hat = nm / bc1
    v_hat = nv / bc2
    g_ref[...] = g
    d_ref[...] = -ADAM_LR * (m_hat / (jnp.sqrt(v_hat) + ADAM_EPS) + ADAM_WD * w_ref[...])
    nm_ref[...] = nm
    nv_ref[...] = nv


def _adamw_flat(parts, w, m, v, name):
    rows = w.shape[0]
    tr = FLAT_ROWS
    blk = pl.BlockSpec((tr, LANES), lambda i: (i, 0))
    return pl.pallas_call(
        functools.partial(_adamw_body), name=name, grid=(rows // tr,),
        in_specs=[pl.BlockSpec((N_DEV, tr, LANES), lambda i: (0, i, 0)), blk, blk, blk],
        out_specs=[blk] * 4, out_shape=[jax.ShapeDtypeStruct((rows, LANES), F32)] * 4,
        compiler_params=_params(("arbitrary",)),
    )(parts, w, m, v)


def _adamw_stacked(parts, w, m, v):
    depth, d, n = w.shape
    tr = _row_tile(d, 128)
    blk = pl.BlockSpec((1, tr, n), lambda l, i: (l, i, 0))
    return pl.pallas_call(
        functools.partial(_adamw_body), name="adamw_w_in", grid=(depth, d // tr),
        in_specs=[pl.BlockSpec((N_DEV, 1, tr, n), lambda l, i: (0, l, i, 0)), blk, blk, blk],
        out_specs=[blk] * 4, out_shape=[jax.ShapeDtypeStruct(w.shape, F32)] * 4,
        compiler_params=_params(("arbitrary", "arbitrary")),
    )(parts, w, m, v)


def _pad_flat(vec, rows):
    return jnp.pad(vec, (0, rows * LANES - vec.shape[0])).reshape(rows, LANES)


def _rows_for(n):
    return -(-n // (FLAT_ROWS * LANES)) * FLAT_ROWS


def _split_shards(full, dim):
    shp = full.shape
    parts = full.reshape(shp[:dim] + (N_DEV, shp[dim] // N_DEV) + shp[dim + 1:])
    return jnp.moveaxis(parts, dim, 0).reshape(N_DEV, -1)


def _join_shards(rows, local_shape, dim):
    parts = jnp.moveaxis(rows.reshape((N_DEV,) + tuple(local_shape)), 0, dim)
    shp = tuple(local_shape)
    return parts.reshape(shp[:dim] + (N_DEV * shp[dim],) + shp[dim + 1:])


def _s5_tables(a_re, a_im, log_step, b_re, b_im, c_re, c_im):
    lam_re = jnp.minimum(a_re, -1e-4)
    lam_im = a_im
    step = jnp.exp(log_step)[:, None]
    mag = jnp.exp(lam_re * step)
    abar_re, abar_im = mag * jnp.cos(lam_im * step), mag * jnp.sin(lam_im * step)
    den = lam_re * lam_re + lam_im * lam_im
    nr, ni = abar_re - 1.0, abar_im
    coef_re = (nr * lam_re + ni * lam_im) / den
    coef_im = (ni * lam_re - nr * lam_im) / den
    bbar_re = coef_re[..., None] * b_re - coef_im[..., None] * b_im
    bbar_im = coef_re[..., None] * b_im + coef_im[..., None] * b_re
    groups = a_re.shape[0]
    nblk = groups // S5_BLOCK_GROUPS
    eye = jnp.eye(S5_BLOCK_GROUPS, dtype=F32)

    def in_blocks(bb):
        t = jnp.swapaxes(bb, 1, 2).reshape(nblk, S5_BLOCK_GROUPS, S5_GROUP, S5_STATE)
        blk = jnp.einsum('ab,jacp->jacbp', eye, t)
        return blk.reshape(nblk, S5_BLOCK_GROUPS * S5_GROUP, S5_BLOCK_GROUPS * S5_STATE)

    def out_blocks(cc):
        t = jnp.swapaxes(cc, 1, 2).reshape(nblk, S5_BLOCK_GROUPS, S5_STATE, S5_GROUP)
        blk = jnp.einsum('ab,japc->japbc', eye, t)
        return blk.reshape(nblk, S5_BLOCK_GROUPS * S5_STATE, S5_BLOCK_GROUPS * S5_GROUP)

    rows = groups * S5_STATE // LANES
    return dict(b_re=in_blocks(bbar_re), b_im=in_blocks(bbar_im), c_re=out_blocks(c_re), c_im=out_blocks(-c_im),
                a_re=abar_re.reshape(rows, LANES), a_im=abar_im.reshape(rows, LANES))


def _in_widths(d_model):
    return [BRANCH, BRANCH, SSD_HEADS * SSD_HEAD + 2 * SSD_GROUPS * SSD_STATE, SSD_HEADS, BRANCH,
            3 * BRANCH, GDN_HEADS, GDN_HEADS, BRANCH, 3 * d_model]


def _local_loss(w, mats, x, target):
    n_meta, d_model = w['meta'].shape
    depth = len(mats['in'])
    seq = x.shape[0]
    pad_rows = CHUNK - n_meta
    first = pad_rows + n_meta
    t_all = first + seq
    alpha = (2 * depth) ** 0.25
    fns = _row_fns(d_model, pad_rows, first, alpha)
    row = lambda nm, key, n_row, n_par, widths, cap, ridx=False: _make_rowwise(nm, fns[key], n_row, n_par, widths, cap, ridx)

    h = jnp.concatenate([jnp.zeros((pad_rows, d_model), F32), w['meta'], x], axis=0)
    (h,) = row("ln_in", 'ln_in', 1, 2, [d_model], 416, True)(h, w['ln_in_g'][None], w['ln_in_b'][None])

    n_small = SSD_HEADS + 2 * GDN_HEADS

    def small_cols(ps):
        return jnp.pad(jnp.concatenate([ps[3], ps[6], ps[7]], axis=1), ((0, 0), (0, LANES - n_small)))

    for l in range(depth):
        pw, pc = mats['in'][l], w['c_in'][l]
        s5_u = _mm("in_s5u", h, pw[0], pc[0])
        s5_z = _mm("in_s5z", h, pw[1], pc[1])
        ssd_xbc = _mm("in_ssdx", h, pw[2], pc[2])
        ssd_z = _mm("in_ssdz", h, pw[4], pc[4])
        gdn_qkv = _mm("in_gdnq", h, pw[5], pc[5])
        gdn_z = _mm("in_gdnz", h, pw[8], pc[8])
        gate = _mm("in_gate", h, pw[9], pc[9])
        small = _mm("in_small", h, small_cols(pw), small_cols(pc))

        zeros_tail = jnp.zeros((LANES - n_small,), F32)
        bias = jnp.concatenate([w['ssd_dt_bias'][l], w['gdn_dt_bias'][l], jnp.zeros((GDN_HEADS,), F32), zeros_tail])[None]
        scale = jnp.concatenate([jnp.ones((SSD_HEADS,), F32), -jnp.exp(w['gdn_a_log'][l]),
                                 jnp.zeros((GDN_HEADS,), F32), zeros_tail])[None]
        (sm,) = row("small_act", 'small_act', 1, 2, [LANES], 832, True)(small, bias, scale)

        tb = _s5_tables(w['s5_a_re'][l], w['s5_a_im'][l], w['s5_log_step'][l], w['s5_b_re'][l], w['s5_b_im'][l],
                        w['s5_c_re'][l], w['s5_c_im'][l])
        srows = tb['a_re'].shape[0]
        bu_re = _make_gmm("s5_bre")(s5_u, tb['b_re']).reshape(t_all, srows, LANES)
        bu_im = _make_gmm("s5_bim")(s5_u, tb['b_im']).reshape(t_all, srows, LANES)
        s_re, s_im = _make_s5_scan("s5_scan")(bu_re, bu_im, tb['a_re'], tb['a_im'])
        y_re = _make_gmm("s5_cre")(s_re.reshape(t_all, srows * LANES), tb['c_re'])
        y_im = _make_gmm("s5_cim")(s_im.reshape(t_all, srows * LANES), tb['c_im'])
        (out_a,) = _make_rowwise("s5_tail", fns['s5_tail'], 4, 2, [d_model], 208, n_wt=2)(
            y_re, y_im, s5_u, s5_z, w['s5_d'][l][None], w['s5_b_glu'][l][None],
            mats['glu'][l], mats['branch'][l, 0], w['c_glu'][l], w['c_branch'][l, 0])

        xbc = _make_conv_rowwise("ssd_conv", fns['ssd_conv'], 2, ssd_xbc.shape[1], 416, True)(
            ssd_xbc, w['ssd_conv_w'][l], w['ssd_conv_b'][l][None])
        a_c = jnp.pad(-jnp.exp(w['ssd_a_log'][l]), (0, LANES - SSD_HEADS))[None]
        d_exp = jnp.repeat(w['ssd_d'][l], SSD_HEAD)[None]
        (y_ssd,) = _make_chunk_scan("ssd_scan", _ssd_chunk, (SSD_HEADS // 2 * LANES, SSD_STATE), 2, 2, [BRANCH])(
            xbc, sm, a_c, d_exp)
        (out_b,) = _make_rowwise("ssd_tail", fns['ssd_tail'], 2, 1, [d_model], 416, n_wt=1)(
            y_ssd, ssd_z, w['ssd_norm_g'][l][None], mats['branch'][l, 1], w['c_branch'][l, 1])

        qkv = _make_conv_rowwise("gdn_conv", fns['gdn_conv'], 1, 3 * BRANCH, 208)(gdn_qkv, w['gdn_conv_w'][l])
        (o_gdn,) = _make_chunk_scan("gdn_scan", _gdn_chunk, (GDN_HEADS * GDN_HEAD, GDN_HEAD), 2, 0, [BRANCH])(qkv, sm)
        (out_c,) = _make_rowwise("gdn_tail", fns['gdn_tail'], 2, 1, [d_model], 416, n_wt=1)(
            o_gdn, gdn_z, w['gdn_norm_g'][l][None], mats['branch'][l, 2], w['c_branch'][l, 2])

        (h,) = _make_rowwise("merge_out", fns['merge_out'], 5, 3, [d_model], 208, True, n_wt=1)(
            out_a, out_b, out_c, gate, h, w['b_gate'][l].reshape(1, 3 * d_model), w['ln_g'][l][None],
            w['ln_b'][l][None], mats['out'][l], w['c_out'][l])

    tgt = jnp.concatenate([jnp.zeros((first, d_model), F32), target], axis=0)
    (rows_loss,) = row("loss", 'loss', 2, 0, [LANES], 416, True)(h, tgt)
    return jnp.sum(rows_loss)


def _in_overlaps(d_model, n_loc):
    offs = [0]
    for wd in _in_widths(d_model):
        offs.append(offs[-1] + wd)
    out = []
    for i in range(len(offs) - 1):
        c0, c1 = offs[i], offs[i + 1]
        segs = []
        for k in range(N_DEV):
            g0, g1 = max(c0, k * n_loc), min(c1, (k + 1) * n_loc)
            if g0 < g1:
                segs.append((k, g0 - k * n_loc, g1 - k * n_loc, g0 - c0))
        out.append(segs)
    return out


MAT_WEIGHTS = ['s5_w_glu', 'w_branch', 'w_out']


def _step(x, target, w_loc, m_loc, v_loc):
    rest = [n for n in WEIGHTS if n in SHARD_DIM and n != 'w_in']
    small = [n for n in rest if n not in MAT_WEIGHTS]
    repl = [n for n in WEIGHTS if n not in SHARD_DIM]
    size = lambda names: sum(int(w_loc[n].size) for n in names)
    rows_mat, rows_small = _rows_for(size(MAT_WEIGHTS)), _rows_for(size(small))
    n_rp = size(repl) + 1
    rows_rp = _rows_for(n_rp)
    depth, d_model, n_loc = w_loc['w_in'].shape
    overlaps = _in_overlaps(d_model, n_loc)

    mat_flat = _pad_flat(jnp.concatenate([w_loc[n].reshape(-1) for n in MAT_WEIGHTS]), rows_mat).astype(BF16)
    small_flat = _pad_flat(jnp.concatenate([w_loc[n].reshape(-1) for n in small]), rows_small)
    g_in, g_mat, g_small = _gather([w_loc['w_in'].astype(BF16), mat_flat, small_flat], "gather_weights")
    full = {}
    for names, buf in ((MAT_WEIGHTS, g_mat.reshape(N_DEV, -1)), (small, g_small.reshape(N_DEV, -1))):
        off = 0
        for n in names:
            sz = int(w_loc[n].size)
            full[n] = _join_shards(buf[:, off:off + sz], w_loc[n].shape, SHARD_DIM[n])
            off += sz
    mats = dict(glu=full['s5_w_glu'], branch=full['w_branch'], out=full['w_out'], **{
        'in': [[jnp.concatenate([g_in[k, l, :, lo:hi] for k, lo, hi, _ in segs], axis=1) for segs in overlaps]
               for l in range(depth)]})
    w_diff = {n: w_loc[n] for n in repl}
    w_diff.update({n: full[n] for n in small})
    w_diff['c_in'] = [[jnp.zeros((d_model, wd), F32) for wd in _in_widths(d_model)] for _ in range(depth)]
    w_diff['c_glu'] = jnp.zeros(full['s5_w_glu'].shape, F32)
    w_diff['c_branch'] = jnp.zeros(full['w_branch'].shape, F32)
    w_diff['c_out'] = jnp.zeros(full['w_out'].shape, F32)

    loss, (g_w, g_x) = jax.value_and_grad(_local_loss, argnums=(0, 2))(w_diff, mats, x[0], target[0])
    g_w['s5_w_glu'], g_w['w_branch'], g_w['w_out'] = g_w['c_glu'], g_w['c_branch'], g_w['c_out']

    def shard_rows(names, rows_n):
        buf = jnp.concatenate([_split_shards(g_w[n], SHARD_DIM[n]) for n in names], axis=1)
        return jnp.pad(buf, ((0, 0), (0, rows_n * LANES - buf.shape[1])))

    send_in = jnp.stack([jnp.stack([
        jnp.concatenate([g_w['c_in'][l][i][:, plo:plo + hi - lo]
                         for i, segs in enumerate(overlaps) for (kk, lo, hi, plo) in segs if kk == k], axis=1)
        for l in range(depth)]) for k in range(N_DEV)]).astype(BF16)
    send_mat = shard_rows(MAT_WEIGHTS, rows_mat).astype(BF16).reshape(N_DEV, rows_mat, LANES)
    rp_vec = jnp.concatenate([g_w[n].reshape(-1) for n in repl] + [loss.reshape(1)])
    rp_vec = jnp.pad(rp_vec, (0, rows_rp * LANES - n_rp))
    rows_f32 = rows_small + rows_rp
    send_f32 = jnp.concatenate([shard_rows(small, rows_small),
                                jnp.broadcast_to(rp_vec[None], (N_DEV, rows_rp * LANES))], axis=1)
    parts_in, parts_mat, parts_f32 = _exchange([send_in, send_mat, send_f32.reshape(N_DEV, rows_f32, LANES)],
                                               "exchange_grads")

    def flat_mat(src):
        return _pad_flat(jnp.concatenate([src[n].reshape(-1) for n in MAT_WEIGHTS]), rows_mat)

    def flat_f32(src):
        sm_vec = jnp.concatenate([src[n].reshape(-1) for n in small])
        sm_vec = jnp.pad(sm_vec, (0, rows_small * LANES - sm_vec.shape[0]))
        rep = jnp.concatenate([src[n].reshape(-1) for n in repl] + [jnp.ones((1,), F32)])
        rep = jnp.pad(rep, (0, rows_rp * LANES - n_rp))
        return jnp.concatenate([sm_vec, rep]).reshape(rows_f32, LANES)

    in_outs = _adamw_stacked(parts_in, w_loc['w_in'], m_loc['w_in'], v_loc['w_in'])
    mat_outs = _adamw_flat(parts_mat, flat_mat(w_loc), flat_mat(m_loc), flat_mat(v_loc), "adamw_mat")
    f32_outs = _adamw_flat(parts_f32, flat_f32(w_loc), flat_f32(m_loc), flat_f32(v_loc), "adamw_rest")

    def unflat(big, mat_buf, f32_buf):
        out = {'w_in': big}
        for names, vec, o in ((MAT_WEIGHTS, mat_buf.reshape(-1), 0), (small, f32_buf.reshape(-1), 0),
                              (repl, f32_buf.reshape(-1), rows_small * LANES)):
            for n in names:
                sz = int(w_loc[n].size)
                out[n] = vec[o:o + sz].reshape(w_loc[n].shape)
                o += sz
        return out, f32_buf.reshape(-1)[rows_small * LANES + n_rp - 1]

    (grads, loss_total), (deltas, _), (new_m, _), (new_v, _) = [
        unflat(b, a, f) for b, a, f in zip(in_outs, mat_outs, f32_outs)]
    return (loss_total, g_x[None], *[grads[n] for n in WEIGHTS], *[deltas[n] for n in WEIGHTS],
            *[new_m[n] for n in WEIGHTS], *[new_v[n] for n in WEIGHTS])


def kernel(x, meta, ln_in_g, ln_in_b, w_in, s5_a_re, s5_a_im, s5_log_step, s5_b_re, s5_b_im, s5_c_re, s5_c_im, s5_d, s5_w_glu, s5_b_glu, ssd_conv_w, ssd_conv_b, ssd_dt_bias, ssd_a_log, ssd_d, ssd_norm_g, gdn_conv_w, gdn_dt_bias, gdn_a_log, gdn_norm_g, w_branch, b_gate, w_out, ln_g, ln_b, loss_target, m_meta, m_ln_in_g, m_ln_in_b, m_w_in, m_s5_a_re, m_s5_a_im, m_s5_log_step, m_s5_b_re, m_s5_b_im, m_s5_c_re, m_s5_c_im, m_s5_d, m_s5_w_glu, m_s5_b_glu, m_ssd_conv_w, m_ssd_conv_b, m_ssd_dt_bias, m_ssd_a_log, m_ssd_d, m_ssd_norm_g, m_gdn_conv_w, m_gdn_dt_bias, m_gdn_a_log, m_gdn_norm_g, m_w_branch, m_b_gate, m_w_out, m_ln_g, m_ln_b, v_meta, v_ln_in_g, v_ln_in_b, v_w_in, v_s5_a_re, v_s5_a_im, v_s5_log_step, v_s5_b_re, v_s5_b_im, v_s5_c_re, v_s5_c_im, v_s5_d, v_s5_w_glu, v_s5_b_glu, v_ssd_conv_w, v_ssd_conv_b, v_ssd_dt_bias, v_ssd_a_log, v_ssd_d, v_ssd_norm_g, v_gdn_conv_w, v_gdn_dt_bias, v_gdn_a_log, v_gdn_norm_g, v_w_branch, v_b_gate, v_w_out, v_ln_g, v_ln_b):
    w_loc = dict(zip(WEIGHTS, (meta, ln_in_g, ln_in_b, w_in, s5_a_re, s5_a_im, s5_log_step, s5_b_re, s5_b_im, s5_c_re, s5_c_im, s5_d, s5_w_glu, s5_b_glu, ssd_conv_w, ssd_conv_b, ssd_dt_bias, ssd_a_log, ssd_d, ssd_norm_g, gdn_conv_w, gdn_dt_bias, gdn_a_log, gdn_norm_g, w_branch, b_gate, w_out, ln_g, ln_b)))
    m_loc = dict(zip(WEIGHTS, (m_meta, m_ln_in_g, m_ln_in_b, m_w_in, m_s5_a_re, m_s5_a_im, m_s5_log_step, m_s5_b_re, m_s5_b_im, m_s5_c_re, m_s5_c_im, m_s5_d, m_s5_w_glu, m_s5_b_glu, m_ssd_conv_w, m_ssd_conv_b, m_ssd_dt_bias, m_ssd_a_log, m_ssd_d, m_ssd_norm_g, m_gdn_conv_w, m_gdn_dt_bias, m_gdn_a_log, m_gdn_norm_g, m_w_branch, m_b_gate, m_w_out, m_ln_g, m_ln_b)))
    v_loc = dict(zip(WEIGHTS, (v_meta, v_ln_in_g, v_ln_in_b, v_w_in, v_s5_a_re, v_s5_a_im, v_s5_log_step, v_s5_b_re, v_s5_b_im, v_s5_c_re, v_s5_c_im, v_s5_d, v_s5_w_glu, v_s5_b_glu, v_ssd_conv_w, v_ssd_conv_b, v_ssd_dt_bias, v_ssd_a_log, v_ssd_d, v_ssd_norm_g, v_gdn_conv_w, v_gdn_dt_bias, v_gdn_a_log, v_gdn_norm_g, v_w_branch, v_b_gate, v_w_out, v_ln_g, v_ln_b)))
    return _step(x, loss_target, w_loc, m_loc, v_loc)
```

```python
import functools
import math

import jax
import jax.numpy as jnp
from jax import lax
from jax.experimental import pallas as pl
from jax.experimental.pallas import tpu as pltpu

F32 = jnp.float32
BF16 = jnp.bfloat16

N_DEV = 8
LANES = 128
SUBLANES = 8
VMEM_LIMIT = 56 * 1024 * 1024
FLAT_ROWS = 1024

CHUNK = 64
CONV_K = 4
S5_GROUP = 16
S5_STATE = 64
S5_BLOCK_GROUPS = 8
SSD_HEAD = 64
SSD_HEADS = 12
SSD_GROUPS = 2
SSD_STATE = 128
GDN_HEAD = 128
GDN_HEADS = 6
BRANCH = 768
LN_EPS = 1e-5

ADAM_LR = 0.001
ADAM_B1 = 0.9
ADAM_B2 = 0.999
ADAM_EPS = 1e-08
ADAM_WD = 0.01
ADAM_STEP = 10

WEIGHTS = ['meta', 'ln_in_g', 'ln_in_b', 'w_in', 's5_a_re', 's5_a_im', 's5_log_step', 's5_b_re', 's5_b_im',
           's5_c_re', 's5_c_im', 's5_d', 's5_w_glu', 's5_b_glu', 'ssd_conv_w', 'ssd_conv_b', 'ssd_dt_bias',
           'ssd_a_log', 'ssd_d', 'ssd_norm_g', 'gdn_conv_w', 'gdn_dt_bias', 'gdn_a_log', 'gdn_norm_g',
           'w_branch', 'b_gate', 'w_out', 'ln_g', 'ln_b']
SHARD_DIM = {'meta': 1, 'w_in': 2, 's5_w_glu': 1, 'ssd_conv_w': 2, 'gdn_conv_w': 2, 'w_branch': 3, 'b_gate': 2,
             'w_out': 1}


def _params(sem):
    return pltpu.CompilerParams(dimension_semantics=sem, vmem_limit_bytes=VMEM_LIMIT)


def _row_tile(m, cap):
    best = None
    for t in range(SUBLANES, min(m, cap) + 1, SUBLANES):
        if m % t == 0:
            best = t
    return best if best is not None else m


def _col_tile(n, cap):
    best = None
    for t in range(LANES, min(n, cap) + 1, LANES):
        if n % t == 0:
            best = t
    return best if best is not None else n


def _any_tile(m, cap):
    best = 1
    for t in range(1, min(m, cap) + 1):
        if m % t == 0:
            best = t
    return best


def _mm_fwd(a, b, name):
    m, _ = a.shape
    g, k, n = b.shape
    tm, tn = _row_tile(m, 832), _col_tile(n, 1024)
    nj = n // tn

    def body(a_ref, b_ref, o_ref):
        o_ref[...] = jnp.dot(a_ref[...].astype(BF16), b_ref[0].astype(BF16), preferred_element_type=F32)

    return pl.pallas_call(
        body, name=name, grid=(g, m // tm, nj),
        in_specs=[pl.BlockSpec((tm, k), lambda gi, i, j: (i, gi)),
                  pl.BlockSpec((1, k, tn), lambda gi, i, j: (gi, 0, j))],
        out_specs=pl.BlockSpec((tm, tn), lambda gi, i, j: (i, gi * nj + j)),
        out_shape=jax.ShapeDtypeStruct((m, g * n), F32),
        compiler_params=_params(("arbitrary", "arbitrary", "arbitrary")),
    )(a, b)


def _mm_da(ct, b, name):
    m, _ = ct.shape
    g, k, n = b.shape
    tm, tk = _row_tile(m, 832 if n <= 1536 else 416), _col_tile(k, 1024)
    nk = k // tk

    def body(c_ref, b_ref, o_ref):
        o_ref[...] = lax.dot_general(c_ref[...].astype(BF16), b_ref[0].astype(BF16), (((1,), (1,)), ((), ())),
                                     preferred_element_type=F32)

    return pl.pallas_call(
        body, name=name, grid=(g, m // tm, nk),
        in_specs=[pl.BlockSpec((tm, n), lambda gi, i, j: (i, gi)),
                  pl.BlockSpec((1, tk, n), lambda gi, i, j: (gi, j, 0))],
        out_specs=pl.BlockSpec((tm, tk), lambda gi, i, j: (i, gi * nk + j)),
        out_shape=jax.ShapeDtypeStruct((m, g * k), F32),
        compiler_params=_params(("arbitrary", "arbitrary", "arbitrary")),
    )(ct, b)


def _mm_db(a, ct, g, k, n, name):
    m = a.shape[0]
    tm, tk, tn = _row_tile(m, 832), _col_tile(k, 1024), _col_tile(n, 1280)
    nk, nn = k // tk, n // tn

    def body(a_ref, c_ref, o_ref):
        @pl.when(pl.program_id(3) == 0)
        def _():
            o_ref[...] = jnp.zeros_like(o_ref)

        o_ref[0] += lax.dot_general(a_ref[...].astype(BF16), c_ref[...].astype(BF16), (((0,), (0,)), ((), ())),
                                    preferred_element_type=F32)

    return pl.pallas_call(
        body, name=name, grid=(g, nk, nn, m // tm),
        in_specs=[pl.BlockSpec((tm, tk), lambda gi, i, j, r: (r, gi * nk + i)),
                  pl.BlockSpec((tm, tn), lambda gi, i, j, r: (r, gi * nn + j))],
        out_specs=pl.BlockSpec((1, tk, tn), lambda gi, i, j, r: (gi, i, j)),
        out_shape=jax.ShapeDtypeStruct((g, k, n), F32),
        compiler_params=_params(("arbitrary", "arbitrary", "arbitrary", "arbitrary")),
    )(a, ct)


def _make_gmm(name):
    @jax.custom_vjp
    def gmm(a, b):
        return _mm_fwd(a, b, name + "_fwd")

    def fwd(a, b):
        return _mm_fwd(a, b, name + "_fwd"), (a, b)

    def bwd(res, ct):
        a, b = res
        g, k, n = b.shape
        return _mm_da(ct, b, name + "_da"), _mm_db(a, ct, g, k, n, name + "_db")

    gmm.defvjp(fwd, bwd)
    return gmm


def _mm(name, a, w, carrier):
    @jax.custom_vjp
    def mm(a, w3, carrier3):
        return _mm_fwd(a, w3, name + "_fwd")

    def fwd(a, w3, carrier3):
        return _mm_fwd(a, w3, name + "_fwd"), (a, w3)

    def bwd(res, ct):
        a, w3 = res
        g, k, n = w3.shape
        return _mm_da(ct, w3, name + "_da"), jnp.zeros_like(w3), _mm_db(a, ct, g, k, n, name + "_db")

    mm.defvjp(fwd, bwd)
    return mm(a, w[None], carrier[None])


@jax.custom_vjp
def _wdot(x, w, carrier):
    return _dot(x, w)


def _wdot_fwd(x, w, carrier):
    return _dot(x, w), (x, w)


def _wdot_bwd(res, ct):
    x, w = res
    return _dot_nt(ct, w), jnp.zeros_like(w), _dot_tn(x, ct)


_wdot.defvjp(_wdot_fwd, _wdot_bwd)


def _make_rowwise(name, fn, n_row, n_par, out_widths, tm_cap, use_ridx=False, n_wt=0):
    n_out = len(out_widths)
    n_in = n_row + n_par + n_wt

    def bind(tm):
        if not use_ridx:
            return fn
        ridx = pl.program_id(0) * tm + lax.broadcasted_iota(jnp.int32, (tm, 1), 0)
        return functools.partial(fn, ridx)

    def specs(args, tm):
        rows = [pl.BlockSpec((tm, a.shape[1]), lambda i: (i, 0)) for a in args[:n_row]]
        pars = [pl.BlockSpec(a.shape, lambda i: (0, 0)) for a in args[n_row:n_in]]
        return rows, pars

    def fwd_call(*args):
        t = args[0].shape[0]
        tm = _row_tile(t, tm_cap)
        rows, pars = specs(args, tm)

        def body(*refs):
            vals = [r[...] for r in refs[:n_row + n_par]]
            mats = [functools.partial(lambda x, w: _dot(x, w), w=r[...]) for r in refs[n_row + n_par:n_in]]
            res = bind(tm)(*vals, *mats)
            for o_ref, r in zip(refs[n_in:], res):
                o_ref[...] = r

        outs = pl.pallas_call(
            body, name=name + "_fwd", grid=(t // tm,), in_specs=rows + pars,
            out_specs=[pl.BlockSpec((tm, w), lambda i: (i, 0)) for w in out_widths],
            out_shape=[jax.ShapeDtypeStruct((t, w), F32) for w in out_widths],
            compiler_params=_params(("arbitrary",)),
        )(*args)
        return tuple(outs)

    def bwd_call(args, cts):
        t = args[0].shape[0]
        tm = _row_tile(t, tm_cap)
        rows, pars = specs(args, tm)
        ct_specs = [pl.BlockSpec((tm, w), lambda i: (i, 0)) for w in out_widths]
        n_diff = n_row + n_par

        def body(*refs):
            vals = [r[...] for r in refs[:n_diff]]
            wts = [r[...] for r in refs[n_diff:n_in]]
            ct_vals = tuple(r[...] for r in refs[n_in:n_in + n_out])
            d_refs = refs[n_in + n_out:]
            f = bind(tm)

            def g(*a):
                mats = [functools.partial(lambda x, w, c: _wdot(x, w, c), w=w, c=c)
                        for w, c in zip(wts, a[n_diff:])]
                return tuple(f(*a[:n_diff], *mats))

            _, vjp = jax.vjp(g, *vals, *[jnp.zeros(w.shape, F32) for w in wts])
            grads = vjp(ct_vals)
            for i in range(n_row):
                d_refs[i][...] = grads[i]
            if n_par + n_wt:
                @pl.when(pl.program_id(0) == 0)
                def _():
                    for r in d_refs[n_row:]:
                        r[...] = jnp.zeros_like(r)

                for r, gr in zip(d_refs[n_row:], grads[n_row:]):
                    r[...] += gr

        outs = pl.pallas_call(
            body, name=name + "_bwd", grid=(t // tm,), in_specs=rows + pars + ct_specs,
            out_specs=rows + pars,
            out_shape=[jax.ShapeDtypeStruct(a.shape, F32) for a in args],
            compiler_params=_params(("arbitrary",)),
        )(*args, *cts)
        return tuple(outs)

    @jax.custom_vjp
    def op(*args):
        return fwd_call(*args[:n_in])

    def fwd(*args):
        return fwd_call(*args[:n_in]), args[:n_in]

    def bwd(args, cts):
        grads = bwd_call(args, cts)
        return grads[:n_row + n_par] + tuple(jnp.zeros_like(a) for a in args[n_row + n_par:]) + grads[n_row + n_par:]

    op.defvjp(fwd, bwd)
    return op


HALO = SUBLANES


def _make_conv_rowwise(name, fn, n_par, out_width, tm_cap, use_ridx=False):
    def bind(ridx):
        return functools.partial(fn, ridx) if use_ridx else fn

    def stage(x_ref, halo_ref, xs, first):
        xs[0:HALO, :] = jnp.where(first, 0.0, halo_ref[...])
        xs[HALO:, :] = x_ref[...]

    def taps(xs, tm):
        return [xs[pl.ds(HALO - (CONV_K - 1) + j, tm), :] for j in range(CONV_K)]

    def fwd_call(x, *pars):
        t, wd = x.shape
        tm = _row_tile(t, tm_cap)
        per = tm // HALO

        def body(*refs):
            x_ref, halo_ref = refs[:2]
            par_refs, o_ref, xs = refs[2:2 + n_par], refs[2 + n_par], refs[-1]
            i = pl.program_id(0)
            stage(x_ref, halo_ref, xs, i == 0)
            ridx = i * tm + lax.broadcasted_iota(jnp.int32, (tm, 1), 0)
            (o_ref[...],) = bind(ridx)(*taps(xs, tm), *[r[...] for r in par_refs])

        return pl.pallas_call(
            body, name=name + "_fwd", grid=(t // tm,),
            in_specs=[pl.BlockSpec((tm, wd), lambda i: (i, 0)),
                      pl.BlockSpec((HALO, wd), lambda i: (jnp.maximum(i * per - 1, 0), 0))]
            + [pl.BlockSpec(p.shape, lambda i: (0, 0)) for p in pars],
            out_specs=pl.BlockSpec((tm, out_width), lambda i: (i, 0)),
            out_shape=jax.ShapeDtypeStruct((t, out_width), F32),
            scratch_shapes=[pltpu.VMEM((tm + HALO, wd), F32)],
            compiler_params=_params(("arbitrary",)),
        )(x, x, *pars)

    def bwd_call(x, pars, ct):
        t, wd = x.shape
        tm = _row_tile(t, tm_cap)
        per = tm // HALO
        nb = t // tm

        def body(*refs):
            x_ref, halo_ref = refs[:2]
            par_refs, ct_ref = refs[2:2 + n_par], refs[2 + n_par]
            dx_ref, dpar_refs = refs[3 + n_par], refs[4 + n_par:4 + 2 * n_par]
            xs, ds, carry = refs[-3:]
            step = pl.program_id(0)
            blk = nb - 1 - step

            @pl.when(step == 0)
            def _():
                carry[...] = jnp.zeros_like(carry)
                for r in dpar_refs:
                    r[...] = jnp.zeros_like(r)

            stage(x_ref, halo_ref, xs, blk == 0)
            ridx = blk * tm + lax.broadcasted_iota(jnp.int32, (tm, 1), 0)
            f = bind(ridx)
            _, vjp = jax.vjp(lambda *a: tuple(f(*a)), *taps(xs, tm), *[r[...] for r in par_refs])
            grads = vjp((ct_ref[...],))
            ds[...] = jnp.zeros_like(ds)
            for j in range(CONV_K):
                ds[pl.ds(HALO - (CONV_K - 1) + j, tm), :] += grads[j]
            ds[pl.ds(tm, HALO), :] += carry[...]
            dx_ref[...] = ds[HALO:, :]
            carry[...] = ds[0:HALO, :]
            for r, g in zip(dpar_refs, grads[CONV_K:]):
                r[...] += g

        rev = lambda i: (nb - 1 - i, 0)
        outs = pl.pallas_call(
            body, name=name + "_bwd", grid=(nb,),
            in_specs=[pl.BlockSpec((tm, wd), rev),
                      pl.BlockSpec((HALO, wd), lambda i: (jnp.maximum((nb - 1 - i) * per - 1, 0), 0))]
            + [pl.BlockSpec(p.shape, lambda i: (0, 0)) for p in pars]
            + [pl.BlockSpec((tm, out_width), rev)],
            out_specs=[pl.BlockSpec((tm, wd), rev)] + [pl.BlockSpec(p.shape, lambda i: (0, 0)) for p in pars],
            out_shape=[jax.ShapeDtypeStruct(x.shape, F32)] + [jax.ShapeDtypeStruct(p.shape, F32) for p in pars],
            scratch_shapes=[pltpu.VMEM((tm + HALO, wd), F32), pltpu.VMEM((tm + HALO, wd), F32),
                            pltpu.VMEM((HALO, wd), F32)],
            compiler_params=_params(("arbitrary",)),
        )(x, x, *pars, ct)
        return tuple(outs)

    @jax.custom_vjp
    def op(x, *pars):
        return fwd_call(x, *pars)

    def fwd(x, *pars):
        return fwd_call(x, *pars), (x, pars)

    def bwd(res, ct):
        x, pars = res
        return bwd_call(x, pars, ct)

    op.defvjp(fwd, bwd)
    return op


def _make_chunk_scan(name, fn, state_shape, n_seq, n_par, out_widths):
    n_out = len(out_widths)
    zeros_idx = (0,) * len(state_shape)

    def fwd_call(*args):
        t = args[0].shape[0]
        nc = t // CHUNK
        seq_specs = [pl.BlockSpec((CHUNK, a.shape[1]), lambda c: (c, 0)) for a in args[:n_seq]]
        par_specs = [pl.BlockSpec(a.shape, lambda c: (0, 0)) for a in args[n_seq:]]

        def body(*refs):
            ins = refs[:n_seq + n_par]
            out_refs = refs[n_seq + n_par:n_seq + n_par + n_out]
            states_ref = refs[n_seq + n_par + n_out]
            st = refs[-1]

            @pl.when(pl.program_id(0) == 0)
            def _():
                st[...] = jnp.zeros_like(st)

            s0 = st[...]
            states_ref[0] = s0
            res = fn(s0, *[r[...] for r in ins])
            st[...] = res[0]
            for o_ref, r in zip(out_refs, res[1:]):
                o_ref[...] = r

        outs = pl.pallas_call(
            body, name=name + "_fwd", grid=(nc,), in_specs=seq_specs + par_specs,
            out_specs=[pl.BlockSpec((CHUNK, w), lambda c: (c, 0)) for w in out_widths]
            + [pl.BlockSpec((1,) + state_shape, lambda c: (c,) + zeros_idx)],
            out_shape=[jax.ShapeDtypeStruct((t, w), F32) for w in out_widths]
            + [jax.ShapeDtypeStruct((nc,) + state_shape, F32)],
            scratch_shapes=[pltpu.VMEM(state_shape, F32)],
            compiler_params=_params(("arbitrary",)),
        )(*args)
        return tuple(outs[:n_out]), outs[n_out]

    def bwd_call(args, states, cts):
        t = args[0].shape[0]
        nc = t // CHUNK
        rev = lambda c: (nc - 1 - c, 0)
        seq_specs = [pl.BlockSpec((CHUNK, a.shape[1]), rev) for a in args[:n_seq]]
        par_specs = [pl.BlockSpec(a.shape, lambda c: (0, 0)) for a in args[n_seq:]]
        ct_specs = [pl.BlockSpec((CHUNK, w), rev) for w in out_widths]
        st_spec = pl.BlockSpec((1,) + state_shape, lambda c: (nc - 1 - c,) + zeros_idx)
        n_in = n_seq + n_par

        def body(*refs):
            vals = [r[...] for r in refs[:n_in]]
            s0 = refs[n_in][0]
            ct_vals = tuple(r[...] for r in refs[n_in + 1:n_in + 1 + n_out])
            d_refs = refs[n_in + 1 + n_out:-1]
            dst = refs[-1]

            @pl.when(pl.program_id(0) == 0)
            def _():
                dst[...] = jnp.zeros_like(dst)
                for j in range(n_par):
                    d_refs[n_seq + j][...] = jnp.zeros_like(d_refs[n_seq + j])

            _, vjp = jax.vjp(lambda *a: tuple(fn(*a)), s0, *vals)
            grads = vjp((dst[...],) + ct_vals)
            dst[...] = grads[0]
            for i in range(n_seq):
                d_refs[i][...] = grads[1 + i]
            for j in range(n_par):
                d_refs[n_seq + j][...] += grads[1 + n_seq + j]

        outs = pl.pallas_call(
            body, name=name + "_bwd", grid=(nc,), in_specs=seq_specs + par_specs + [st_spec] + ct_specs,
            out_specs=seq_specs + par_specs,
            out_shape=[jax.ShapeDtypeStruct(a.shape, F32) for a in args],
            scratch_shapes=[pltpu.VMEM(state_shape, F32)],
            compiler_params=_params(("arbitrary",)),
        )(*args, states, *cts)
        return tuple(outs)

    @jax.custom_vjp
    def op(*args):
        return fwd_call(*args)[0]

    def fwd(*args):
        outs, states = fwd_call(*args)
        return outs, (args, states)

    def bwd(res, cts):
        args, states = res
        return bwd_call(args, states, cts)

    op.defvjp(fwd, bwd)
    return op


def _s5_scan_fwd(bre, bim, are, aim, name):
    t, r, _ = bre.shape
    tb = _any_tile(t, 208)
    blk = pl.BlockSpec((tb, r, LANES), lambda i: (i, 0, 0))
    par = pl.BlockSpec((r, LANES), lambda i: (0, 0))

    def body(bre_ref, bim_ref, are_ref, aim_ref, sre_ref, sim_ref, st):
        @pl.when(pl.program_id(0) == 0)
        def _():
            st[...] = jnp.zeros_like(st)

        ar, ai = are_ref[...], aim_ref[...]

        def step(k, carry):
            sr, si = carry
            nr = ar * sr - ai * si + bre_ref[k]
            ni = ar * si + ai * sr + bim_ref[k]
            sre_ref[k] = nr
            sim_ref[k] = ni
            return nr, ni

        sr, si = lax.fori_loop(0, tb, step, (st[0], st[1]), unroll=4)
        st[0] = sr
        st[1] = si

    return pl.pallas_call(
        body, name=name, grid=(t // tb,), in_specs=[blk, blk, par, par], out_specs=[blk, blk],
        out_shape=[jax.ShapeDtypeStruct(bre.shape, F32)] * 2,
        scratch_shapes=[pltpu.VMEM((2, r, LANES), F32)],
        compiler_params=_params(("arbitrary",)),
    )(bre, bim, are, aim)


def _s5_scan_bwd(dsr, dsi, sre, sim, are, aim, name):
    t, r, _ = sre.shape
    tb = _any_tile(t, 208)
    nb = t // tb
    blk = pl.BlockSpec((tb, r, LANES), lambda i: (nb - 1 - i, 0, 0))
    par = pl.BlockSpec((r, LANES), lambda i: (0, 0))

    def body(dsr_ref, dsi_ref, sre_ref, sim_ref, are_ref, aim_ref, gre_ref, gim_ref, dar_ref, dai_ref, st):
        @pl.when(pl.program_id(0) == 0)
        def _():
            st[...] = jnp.zeros_like(st)
            dar_ref[...] = jnp.zeros_like(dar_ref)
            dai_ref[...] = jnp.zeros_like(dai_ref)

        ar, ai = are_ref[...], aim_ref[...]

        def step(k, carry):
            gr, gi, dar, dai = carry
            q = tb - 1 - k
            s_r, s_i = sre_ref[q], sim_ref[q]
            dar = dar + gr * s_r + gi * s_i
            dai = dai + gi * s_r - gr * s_i
            ngr = dsr_ref[q] + ar * gr + ai * gi
            ngi = dsi_ref[q] + ar * gi - ai * gr
            gre_ref[q] = ngr
            gim_ref[q] = ngi
            return ngr, ngi, dar, dai

        gr, gi, dar, dai = lax.fori_loop(0, tb, step, (st[0], st[1], dar_ref[...], dai_ref[...]), unroll=4)
        st[0] = gr
        st[1] = gi
        dar_ref[...] = dar
        dai_ref[...] = dai

    return pl.pallas_call(
        body, name=name, grid=(nb,), in_specs=[blk, blk, blk, blk, par, par], out_specs=[blk, blk, par, par],
        out_shape=[jax.ShapeDtypeStruct(sre.shape, F32)] * 2 + [jax.ShapeDtypeStruct(are.shape, F32)] * 2,
        scratch_shapes=[pltpu.VMEM((2, r, LANES), F32)],
        compiler_params=_params(("arbitrary",)),
    )(dsr, dsi, sre, sim, are, aim)


def _make_s5_scan(name):
    @jax.custom_vjp
    def scan(bre, bim, are, aim):
        return tuple(_s5_scan_fwd(bre, bim, are, aim, name + "_fwd"))

    def fwd(bre, bim, are, aim):
        sre, sim = _s5_scan_fwd(bre, bim, are, aim, name + "_fwd")
        return (sre, sim), (sre, sim, are, aim)

    def bwd(res, cts):
        sre, sim, are, aim = res
        return tuple(_s5_scan_bwd(cts[0], cts[1], sre, sim, are, aim, name + "_bwd"))

    scan.defvjp(fwd, bwd)
    return scan


def _dot(a, b):
    return jnp.dot(a.astype(BF16), b.astype(BF16), preferred_element_type=F32)


def _dot_nt(a, b):
    return lax.dot_general(a.astype(BF16), b.astype(BF16), (((1,), (1,)), ((), ())), preferred_element_type=F32)


def _dot_tn(a, b):
    return lax.dot_general(a.astype(BF16), b.astype(BF16), (((0,), (0,)), ((), ())), preferred_element_type=F32)


def _dot_f32(a, b):
    return jnp.dot(a, b, precision=lax.Precision.HIGHEST, preferred_element_type=F32)


def _dot_tn_f32(a, b):
    return lax.dot_general(a, b, (((0,), (0,)), ((), ())), precision=lax.Precision.HIGHEST,
                           preferred_element_type=F32)


def _iota(shape, dim):
    return lax.broadcasted_iota(jnp.int32, shape, dim)


def _tri(strict=False):
    r, c = _iota((CHUNK, CHUNK), 0), _iota((CHUNK, CHUNK), 1)
    return (r > c) if strict else (r >= c)


def _silu(x):
    return x * jax.nn.sigmoid(x)


def _layer_norm(z, g, b):
    mu = jnp.mean(z, axis=-1, keepdims=True)
    var = jnp.mean(jnp.square(z - mu), axis=-1, keepdims=True)
    return (z - mu) * lax.rsqrt(var + LN_EPS) * g + b


def _ssd_chunk(state, xbc, sm, a_c, d_exp):
    width = SSD_HEADS * SSD_HEAD
    x = xbc[:, :width]
    lane = _iota((CHUNK, LANES), 1)
    dtc = jnp.where(lane < SSD_HEADS, sm, 0.0)
    low = _tri().astype(F32)
    eye = (_iota((CHUNK, CHUNK), 0) == _iota((CHUNK, CHUNK), 1)).astype(F32)
    head_col, head_row = _iota((LANES, width), 1), _iota((LANES, width), 0) * SSD_HEAD
    expand = ((head_col >= head_row) & (head_col < head_row + SSD_HEAD)).astype(F32)
    acum_c = _dot_f32(low, dtc * a_c)
    acum_ct = _dot_tn_f32(acum_c, eye)
    dt_exp = _dot_f32(dtc, expand)
    acum = _dot_f32(acum_c, expand)
    xd = x * dt_exp
    last = acum[CHUNK - 1:CHUNK, :]
    to_end = jnp.exp(last - acum)
    eac = jnp.exp(acum)
    causal = _tri()
    first_half = _iota((CHUNK, LANES), 1) < SSD_HEAD
    top_rows = _iota((LANES, LANES), 0) < SSD_HEAD
    pairs = range(SSD_HEADS // 2)
    grp = [(2 * p) // (SSD_HEADS // SSD_GROUPS) for p in pairs]
    cols = [slice(p * LANES, (p + 1) * LANES) for p in pairs]
    bg = [xbc[:, width + g * SSD_STATE: width + (g + 1) * SSD_STATE] for g in range(SSD_GROUPS)]
    cg = [xbc[:, width + (SSD_GROUPS + g) * SSD_STATE: width + (SSD_GROUPS + g + 1) * SSD_STATE]
          for g in range(SSD_GROUPS)]
    scores = [_dot_nt(cg[g], bg[g]) for g in range(SSD_GROUPS)]
    dec = [jnp.where(causal, jnp.exp(jnp.minimum(acum_c[:, h:h + 1] - acum_ct[h:h + 1, :], 0.0)), 0.0)
           for h in range(SSD_HEADS)]
    y_lo = [_dot(scores[grp[p]] * dec[2 * p], jnp.where(first_half, xd[:, cols[p]], 0.0)) for p in pairs]
    y_hi = [_dot(scores[grp[p]] * dec[2 * p + 1], jnp.where(first_half, 0.0, xd[:, cols[p]])) for p in pairs]
    s_prev = [state[p * LANES:(p + 1) * LANES, :] for p in pairs]
    y_off = [_dot_nt(cg[grp[p]], s_prev[p]) for p in pairs]
    s_add = [_dot_tn(xd[:, cols[p]] * to_end[:, cols[p]], bg[grp[p]]) for p in pairs]
    ys = [y_lo[p] + y_hi[p] + y_off[p] * eac[:, cols[p]] + x[:, cols[p]] * d_exp[:, cols[p]] for p in pairs]
    cd = [jnp.where(top_rows, jnp.exp(acum_c[CHUNK - 1:CHUNK, 2 * p:2 * p + 1]),
                    jnp.exp(acum_c[CHUNK - 1:CHUNK, 2 * p + 1:2 * p + 2])) for p in pairs]
    new_states = [s_prev[p] * cd[p] + s_add[p] for p in pairs]
    return jnp.concatenate(new_states, axis=0), jnp.concatenate(ys, axis=1)


def _gdn_chunk(state, qkv, sm):
    width = GDN_HEADS * GDN_HEAD
    g0, b0 = SSD_HEADS, SSD_HEADS + GDN_HEADS
    lane = _iota((CHUNK, LANES), 1)
    gc = jnp.where((lane >= g0) & (lane < b0), sm, 0.0)
    low = _tri().astype(F32)
    eye = (_iota((CHUNK, CHUNK), 0) == _iota((CHUNK, CHUNK), 1)).astype(F32)
    gcum = _dot_f32(low, gc)
    gcum_t = _dot_tn_f32(gcum, eye)
    causal, strict = _tri(), _tri(strict=True)
    heads = range(GDN_HEADS)
    q = [qkv[:, h * GDN_HEAD:(h + 1) * GDN_HEAD] for h in heads]
    k = [qkv[:, width + h * GDN_HEAD: width + (h + 1) * GDN_HEAD] for h in heads]
    v = [qkv[:, 2 * width + h * GDN_HEAD: 2 * width + (h + 1) * GDN_HEAD] for h in heads]
    beta = [sm[:, b0 + h:b0 + h + 1] for h in heads]
    gcol = [gcum[:, g0 + h:g0 + h + 1] for h in heads]
    glast = [gcum[CHUNK - 1:CHUNK, g0 + h:g0 + h + 1] for h in heads]
    gamma = [jnp.where(causal, jnp.exp(jnp.minimum(gcol[h] - gcum_t[g0 + h:g0 + h + 1, :], 0.0)), 0.0) for h in heads]
    kk = [_dot_nt(k[h], k[h]) for h in heads]
    qk = [_dot_nt(q[h], k[h]) for h in heads]
    egc = [jnp.exp(gcol[h]) for h in heads]
    nmat = [-jnp.where(strict, kk[h] * gamma[h] * beta[h], 0.0) for h in heads]
    sol = [jnp.concatenate([v[h] * beta[h], k[h] * (beta[h] * egc[h])], axis=1) for h in heads]
    for i in range(6):
        upd = [_dot(nmat[h], sol[h]) for h in heads]
        if i < 5:
            nmat = [_dot(nmat[h], nmat[h]) for h in heads]
        sol = [sol[h] + upd[h] for h in heads]
    s_prev = [state[h * GDN_HEAD:(h + 1) * GDN_HEAD, :] for h in heads]
    w_s = [_dot(sol[h][:, GDN_HEAD:], s_prev[h]) for h in heads]
    q_s = [_dot(q[h] * egc[h], s_prev[h]) for h in heads]
    v_new = [sol[h][:, :GDN_HEAD] - w_s[h] for h in heads]
    a_v = [_dot(qk[h] * gamma[h], v_new[h]) for h in heads]
    k_v = [_dot_tn(k[h] * jnp.exp(glast[h] - gcol[h]), v_new[h]) for h in heads]
    outs = [q_s[h] + a_v[h] for h in heads]
    new_states = [s_prev[h] * jnp.exp(glast[h]) + k_v[h] for h in heads]
    return jnp.concatenate(new_states, axis=0), jnp.concatenate(outs, axis=1)


def _row_fns(d_model, pad_rows, loss_rows, alpha):
    def keep(ridx, v):
        return jnp.where(ridx >= pad_rows, v, 0.0)

    def ln_in(ridx, h, g, b):
        return (keep(ridx, _layer_norm(h, g, b)),)

    def s5_tail(y_re, y_im, u, z, d, bias, glu, branch):
        v0 = jax.nn.gelu(y_re + y_im + d * u)
        return (branch(v0 * jax.nn.sigmoid(glu(v0) + bias) * _silu(z)),)

    def small_act(ridx, raw, bias, scale):
        lane = _iota(raw.shape, 1)
        sp = jax.nn.softplus(raw + bias)
        g0, b0 = SSD_HEADS, SSD_HEADS + GDN_HEADS
        out = jnp.where(lane < g0, sp, jnp.where(lane < b0, scale * sp,
                                                 jnp.where(lane < b0 + GDN_HEADS, jax.nn.sigmoid(raw), 0.0)))
        return (keep(ridx, out),)

    def conv(xs, w):
        acc = xs[0] * w[0:1, :]
        for j in range(1, CONV_K):
            acc = acc + xs[j] * w[j:j + 1, :]
        return acc

    def ssd_conv(ridx, x0, x1, x2, x3, w, b):
        return (keep(ridx, _silu(conv((x0, x1, x2, x3), w) + b)),)

    def ssd_tail(y, z, g, branch):
        v = y * _silu(z)
        return (branch(v * lax.rsqrt(jnp.mean(v * v, axis=-1, keepdims=True) + LN_EPS) * g),)

    def gdn_conv(x0, x1, x2, x3, w):
        a = _silu(conv((x0, x1, x2, x3), w))
        width = GDN_HEADS * GDN_HEAD
        parts = []
        for h in range(2 * GDN_HEADS):
            z = a[:, h * GDN_HEAD:(h + 1) * GDN_HEAD]
            z = z * lax.rsqrt(jnp.sum(z * z, axis=-1, keepdims=True) + 1e-6)
            parts.append(z * GDN_HEAD ** -0.5 if h < GDN_HEADS else z)
        parts.append(a[:, 2 * width:])
        return (jnp.concatenate(parts, axis=1),)

    def gdn_tail(o, z, g, branch):
        parts = []
        for h in range(GDN_HEADS):
            cols = slice(h * GDN_HEAD, (h + 1) * GDN_HEAD)
            oh = o[:, cols]
            oh = oh * lax.rsqrt(jnp.mean(oh * oh, axis=-1, keepdims=True) + LN_EPS) * g
            parts.append(oh * _silu(z[:, cols]))
        return (branch(jnp.concatenate(parts, axis=1)),)

    def merge_out(ridx, oa, ob, oc, gate, h, bias, g, b, w_out):
        acc = None
        for k, o in enumerate((oa, ob, oc)):
            cols = slice(k * d_model, (k + 1) * d_model)
            term = jax.nn.sigmoid(gate[:, cols] + bias[:, cols]) * o
            acc = term if acc is None else acc + term
        return (keep(ridx, _layer_norm(alpha * h + w_out(acc), g, b)),)

    def loss_rows_fn(ridx, h, tgt):
        row = 0.5 * jnp.mean(jnp.square(h - tgt), axis=-1, keepdims=True)
        row = jnp.where(ridx >= loss_rows, row, 0.0)
        lane = _iota((h.shape[0], LANES), 1)
        return (jnp.where(lane == 0, row, 0.0),)

    return dict(ln_in=ln_in, s5_tail=s5_tail, small_act=small_act, ssd_conv=ssd_conv, ssd_tail=ssd_tail,
                gdn_conv=gdn_conv, gdn_tail=gdn_tail, merge_out=merge_out, loss=loss_rows_fn)


def _exchange(srcs, name):
    n_arr = len(srcs)
    n_peer = N_DEV - 1
    flips = [(fx, fy, fc) for fx in (0, 1) for fy in (0, 1) for fc in (0, 1)][1:]
    blocks = [s.shape[1:] for s in srcs]

    def body(*refs):
        src_refs, out_refs = refs[:n_arr], refs[n_arr:2 * n_arr]
        send_sems, recv_sems, local_sems = refs[2 * n_arr:]
        x, y, c = lax.axis_index("x"), lax.axis_index("y"), lax.axis_index("c")
        me = 4 * x + 2 * y + c

        def pick(a, j):
            return src_refs[a].at[j]

        own = [pltpu.make_async_copy(pick(a, me), out_refs[a].at[me], local_sems.at[a]) for a in range(n_arr)]
        for cp in own:
            cp.start()
        copies = []
        for k, (fx, fy, fc) in enumerate(flips):
            px = 1 - x if fx else x
            py = 1 - y if fy else y
            pc = 1 - c if fc else c
            peer = 4 * px + 2 * py + pc
            for a in range(n_arr):
                cp = pltpu.make_async_remote_copy(
                    src_ref=pick(a, peer), dst_ref=out_refs[a].at[me],
                    send_sem=send_sems.at[a * n_peer + k], recv_sem=recv_sems.at[a * n_peer + k],
                    device_id=(px, py, pc), device_id_type=pl.DeviceIdType.MESH)
                cp.start()
                copies.append(cp)
        for cp in copies:
            cp.wait()
        for cp in own:
            cp.wait()

    return pl.pallas_call(
        body, name=name,
        in_specs=[pl.BlockSpec(memory_space=pl.ANY)] * n_arr, out_specs=[pl.BlockSpec(memory_space=pl.ANY)] * n_arr,
        out_shape=[jax.ShapeDtypeStruct((N_DEV,) + tuple(b), s.dtype) for b, s in zip(blocks, srcs)],
        scratch_shapes=[pltpu.SemaphoreType.DMA((n_arr * n_peer,)), pltpu.SemaphoreType.DMA((n_arr * n_peer,)),
                        pltpu.SemaphoreType.DMA((n_arr,))],
    )(*srcs)


def _pair_exchange(srcs, name):
    n_arr = len(srcs)
    n_chip = N_DEV // 2

    def body(*refs):
        src_refs = refs[:n_arr]
        kept_refs, recv_refs = refs[n_arr:2 * n_arr], refs[2 * n_arr:3 * n_arr]
        send_sems, recv_sems, local_sems = refs[3 * n_arr:]
        x, y, c = lax.axis_index("x"), lax.axis_index("y"), lax.axis_index("c")
        copies = []
        for a in range(n_arr):
            for q in range(n_chip):
                own = pltpu.make_async_copy(src_refs[a].at[q, c], kept_refs[a].at[q], local_sems.at[a * n_chip + q])
                cp = pltpu.make_async_remote_copy(
                    src_ref=src_refs[a].at[q, 1 - c], dst_ref=recv_refs[a].at[q],
                    send_sem=send_sems.at[a * n_chip + q], recv_sem=recv_sems.at[a * n_chip + q],
                    device_id=(x, y, 1 - c), device_id_type=pl.DeviceIdType.MESH)
                own.start()
                cp.start()
                copies += [cp, own]
        for cp in copies:
            cp.wait()

    outs = pl.pallas_call(
        body, name=name,
        in_specs=[pl.BlockSpec(memory_space=pl.ANY)] * n_arr,
        out_specs=[pl.BlockSpec(memory_space=pl.ANY)] * (2 * n_arr),
        out_shape=[jax.ShapeDtypeStruct((n_chip,) + tuple(s.shape[2:]), s.dtype) for s in srcs] * 2,
        scratch_shapes=[pltpu.SemaphoreType.DMA((n_arr * n_chip,)), pltpu.SemaphoreType.DMA((n_arr * n_chip,)),
                        pltpu.SemaphoreType.DMA((n_arr * n_chip,))],
    )(*srcs)
    return list(zip(outs[:n_arr], outs[n_arr:]))


def _pair_sum(kept, recv, name):
    n_chip, rows, cols = kept.shape
    tr = _row_tile(rows, 1024 if cols <= LANES else 512)
    blk = pl.BlockSpec((1, tr, cols), lambda q, i: (q, i, 0))

    def body(a_ref, b_ref, o_ref):
        o_ref[...] = (a_ref[...].astype(F32) + b_ref[...].astype(F32)).astype(o_ref.dtype)

    return pl.pallas_call(
        body, name=name, grid=(n_chip, rows // tr), in_specs=[blk, blk], out_specs=blk,
        out_shape=jax.ShapeDtypeStruct(kept.shape, kept.dtype),
        compiler_params=_params(("arbitrary", "arbitrary")),
    )(kept, recv)


def _chip_exchange(chip_srcs, direct_srcs, name):
    n_chip_arr, n_dir = len(chip_srcs), len(direct_srcs)
    n_arr = n_chip_arr + n_dir
    chip_flips = [(1, 0), (0, 1), (1, 1)]
    dev_flips = [(fx, fy, fc) for fx in (0, 1) for fy in (0, 1) for fc in (0, 1)][1:]
    base = n_chip_arr * len(chip_flips)

    def body(*refs):
        src_refs, out_refs = refs[:n_arr], refs[n_arr:2 * n_arr]
        send_sems, recv_sems, local_sems = refs[2 * n_arr:]
        x, y, c = lax.axis_index("x"), lax.axis_index("y"), lax.axis_index("c")
        my_chip, me = 2 * x + y, 4 * x + 2 * y + c
        copies = []
        for a in range(n_chip_arr):
            own = pltpu.make_async_copy(src_refs[a].at[my_chip], out_refs[a].at[my_chip], local_sems.at[a])
            own.start()
            copies.append(own)
            for k, (fx, fy) in enumerate(chip_flips):
                px = 1 - x if fx else x
                py = 1 - y if fy else y
                cp = pltpu.make_async_remote_copy(
                    src_ref=src_refs[a].at[2 * px + py], dst_ref=out_refs[a].at[my_chip],
                    send_sem=send_sems.at[a * 3 + k], recv_sem=recv_sems.at[a * 3 + k],
                    device_id=(px, py, c), device_id_type=pl.DeviceIdType.MESH)
                cp.start()
                copies.append(cp)
        for b in range(n_dir):
            a = n_chip_arr + b
            own = pltpu.make_async_copy(src_refs[a].at[me], out_refs[a].at[me], local_sems.at[a])
            own.start()
            copies.append(own)
            for k, (fx, fy, fc) in enumerate(dev_flips):
                px = 1 - x if fx else x
                py = 1 - y if fy else y
                pc = 1 - c if fc else c
                cp = pltpu.make_async_remote_copy(
                    src_ref=src_refs[a].at[4 * px + 2 * py + pc], dst_ref=out_refs[a].at[me],
                    send_sem=send_sems.at[base + b * 7 + k], recv_sem=recv_sems.at[base + b * 7 + k],
                    device_id=(px, py, pc), device_id_type=pl.DeviceIdType.MESH)
                cp.start()
                copies.append(cp)
        for cp in copies:
            cp.wait()

    n_sem = base + n_dir * len(dev_flips)
    srcs = list(chip_srcs) + list(direct_srcs)
    return pl.pallas_call(
        body, name=name,
        in_specs=[pl.BlockSpec(memory_space=pl.ANY)] * n_arr, out_specs=[pl.BlockSpec(memory_space=pl.ANY)] * n_arr,
        out_shape=[jax.ShapeDtypeStruct(s.shape, s.dtype) for s in srcs],
        scratch_shapes=[pltpu.SemaphoreType.DMA((n_sem,)), pltpu.SemaphoreType.DMA((n_sem,)),
                        pltpu.SemaphoreType.DMA((n_arr,))],
    )(*srcs)


def _gather(srcs, name):
    n_arr = len(srcs)
    n_sem = N_DEV - 1

    def body(*refs):
        src_refs, out_refs = refs[:n_arr], refs[n_arr:2 * n_arr]
        send_sems, recv_sems, local_sems = refs[2 * n_arr:]
        x, y, c = lax.axis_index("x"), lax.axis_index("y"), lax.axis_index("c")
        me, sibling = (x, y, c), (x, y, 1 - c)
        chips = [(1 - x, y), (x, 1 - y), (1 - x, 1 - y)]

        def slot(a, dev):
            return out_refs[a].at[4 * dev[0] + 2 * dev[1] + dev[2]]

        def copy(a, k, block, to, own=False):
            return pltpu.make_async_remote_copy(
                src_ref=src_refs[a] if own else slot(a, block), dst_ref=slot(a, block),
                send_sem=send_sems.at[a * n_sem + k], recv_sem=recv_sems.at[a * n_sem + k],
                device_id=to, device_id_type=pl.DeviceIdType.MESH)

        arrays = range(n_arr)
        mine = [pltpu.make_async_copy(src_refs[a], slot(a, me), local_sems.at[a]) for a in arrays]
        for cp in mine:
            cp.start()
        first = [copy(a, 0, me, sibling, own=True) for a in arrays]
        first += [copy(a, 1 + j, me, (*chip, c), own=True) for j, chip in enumerate(chips) for a in arrays]
        for cp in first:
            cp.start()
        passed = []
        for j, chip in enumerate(chips):
            for a in arrays:
                copy(a, 1 + j, (*chip, c), me).wait_recv()
                fwd = copy(a, 4 + j, (*chip, c), sibling)
                fwd.start()
                passed.append(fwd)
        for a in arrays:
            copy(a, 0, sibling, me).wait_recv()
        for j, chip in enumerate(chips):
            for a in arrays:
                copy(a, 4 + j, (*chip, 1 - c), me).wait_recv()
        for cp in first + passed:
            cp.wait_send()
        for cp in mine:
            cp.wait()

    return pl.pallas_call(
        body, name=name,
        in_specs=[pl.BlockSpec(memory_space=pl.ANY)] * n_arr, out_specs=[pl.BlockSpec(memory_space=pl.ANY)] * n_arr,
        out_shape=[jax.ShapeDtypeStruct((N_DEV,) + tuple(s.shape), s.dtype) for s in srcs],
        scratch_shapes=[pltpu.SemaphoreType.DMA((n_arr * n_sem,)), pltpu.SemaphoreType.DMA((n_arr * n_sem,)),
                        pltpu.SemaphoreType.DMA((n_arr,))],
    )(*srcs)


def _adamw_body(p_ref, w_ref, m_ref, v_ref, g_ref, d_ref, nm_ref, nv_ref):
    bc1 = 1.0 - ADAM_B1 ** ADAM_STEP
    bc2 = 1.0 - ADAM_B2 ** ADAM_STEP
    g = p_ref[0].astype(F32)
    for k in range(1, p_ref.shape[0]):
        g = g + p_ref[k].astype(F32)
    nm = ADAM_B1 * m_ref[...] + (1.0 - ADAM_B1) * g
    nv = ADAM_B2 * v_ref[...] + (1.0 - ADAM_B2) * jnp.square(g)
    m_hat = nm / bc1
    v_hat = nv / bc2
    g_ref[...] = g
    d_ref[...] = -ADAM_LR * (m_hat / (jnp.sqrt(v_hat) + ADAM_EPS) + ADAM_WD * w_ref[...])
    nm_ref[...] = nm
    nv_ref[...] = nv


def _adamw_flat(parts, w, m, v, name):
    rows = w.shape[0]
    tr = FLAT_ROWS
    blk = pl.BlockSpec((tr, LANES), lambda i: (i, 0))
    return pl.pallas_call(
        functools.partial(_adamw_body), name=name, grid=(rows // tr,),
        in_specs=[pl.BlockSpec((parts.shape[0], tr, LANES), lambda i: (0, i, 0)), blk, blk, blk],
        out_specs=[blk] * 4, out_shape=[jax.ShapeDtypeStruct((rows, LANES), F32)] * 4,
        compiler_params=_params(("arbitrary",)),
    )(parts, w, m, v)


def _adamw_stacked(parts, w, m, v):
    depth, d, n = w.shape
    tr = _row_tile(d, 128)
    blk = pl.BlockSpec((1, tr, n), lambda l, i: (l, i, 0))
    return pl.pallas_call(
        functools.partial(_adamw_body), name="adamw_w_in", grid=(depth, d // tr),
        in_specs=[pl.BlockSpec((parts.shape[0], 1, tr, n), lambda l, i: (0, l, i, 0)), blk, blk, blk],
        out_specs=[blk] * 4, out_shape=[jax.ShapeDtypeStruct(w.shape, F32)] * 4,
        compiler_params=_params(("arbitrary", "arbitrary")),
    )(parts, w, m, v)


def _pad_flat(vec, rows):
    return jnp.pad(vec, (0, rows * LANES - vec.shape[0])).reshape(rows, LANES)


def _rows_for(n):
    return -(-n // (FLAT_ROWS * LANES)) * FLAT_ROWS


def _split_shards(full, dim):
    shp = full.shape
    parts = full.reshape(shp[:dim] + (N_DEV, shp[dim] // N_DEV) + shp[dim + 1:])
    return jnp.moveaxis(parts, dim, 0).reshape(N_DEV, -1)


def _join_shards(rows, local_shape, dim):
    parts = jnp.moveaxis(rows.reshape((N_DEV,) + tuple(local_shape)), 0, dim)
    shp = tuple(local_shape)
    return parts.reshape(shp[:dim] + (N_DEV * shp[dim],) + shp[dim + 1:])


def _s5_tables(a_re, a_im, log_step, b_re, b_im, c_re, c_im):
    lam_re = jnp.minimum(a_re, -1e-4)
    lam_im = a_im
    step = jnp.exp(log_step)[:, None]
    mag = jnp.exp(lam_re * step)
    abar_re, abar_im = mag * jnp.cos(lam_im * step), mag * jnp.sin(lam_im * step)
    den = lam_re * lam_re + lam_im * lam_im
    nr, ni = abar_re - 1.0, abar_im
    coef_re = (nr * lam_re + ni * lam_im) / den
    coef_im = (ni * lam_re - nr * lam_im) / den
    bbar_re = coef_re[..., None] * b_re - coef_im[..., None] * b_im
    bbar_im = coef_re[..., None] * b_im + coef_im[..., None] * b_re
    groups = a_re.shape[0]
    nblk = groups // S5_BLOCK_GROUPS
    eye = jnp.eye(S5_BLOCK_GROUPS, dtype=F32)

    def in_blocks(bb):
        t = jnp.swapaxes(bb, 1, 2).reshape(nblk, S5_BLOCK_GROUPS, S5_GROUP, S5_STATE)
        blk = jnp.einsum('ab,jacp->jacbp', eye, t)
        return blk.reshape(nblk, S5_BLOCK_GROUPS * S5_GROUP, S5_BLOCK_GROUPS * S5_STATE)

    def out_blocks(cc):
        t = jnp.swapaxes(cc, 1, 2).reshape(nblk, S5_BLOCK_GROUPS, S5_STATE, S5_GROUP)
        blk = jnp.einsum('ab,japc->japbc', eye, t)
        return blk.reshape(nblk, S5_BLOCK_GROUPS * S5_STATE, S5_BLOCK_GROUPS * S5_GROUP)

    rows = groups * S5_STATE // LANES
    return dict(b_re=in_blocks(bbar_re), b_im=in_blocks(bbar_im), c_re=out_blocks(c_re), c_im=out_blocks(-c_im),
                a_re=abar_re.reshape(rows, LANES), a_im=abar_im.reshape(rows, LANES))


def _in_widths(d_model):
    return [BRANCH, BRANCH, SSD_HEADS * SSD_HEAD + 2 * SSD_GROUPS * SSD_STATE, SSD_HEADS, BRANCH,
            3 * BRANCH, GDN_HEADS, GDN_HEADS, BRANCH, 3 * d_model]


def _local_loss(w, mats, x, target):
    n_meta, d_model = w['meta'].shape
    depth = len(mats['in'])
    seq = x.shape[0]
    pad_rows = CHUNK - n_meta
    first = pad_rows + n_meta
    t_all = first + seq
    alpha = (2 * depth) ** 0.25
    fns = _row_fns(d_model, pad_rows, first, alpha)
    row = lambda nm, key, n_row, n_par, widths, cap, ridx=False: _make_rowwise(nm, fns[key], n_row, n_par, widths, cap, ridx)

    h = jnp.concatenate([jnp.zeros((pad_rows, d_model), F32), w['meta'], x], axis=0)
    (h,) = row("ln_in", 'ln_in', 1, 2, [d_model], 416, True)(h, w['ln_in_g'][None], w['ln_in_b'][None])

    n_small = SSD_HEADS + 2 * GDN_HEADS

    def small_cols(ps):
        return jnp.pad(jnp.concatenate([ps[3], ps[6], ps[7]], axis=1), ((0, 0), (0, LANES - n_small)))

    for l in range(depth):
        pw, pc = mats['in'][l], w['c_in'][l]
        s5_u = _mm("in_s5u", h, pw[0], pc[0])
        s5_z = _mm("in_s5z", h, pw[1], pc[1])
        ssd_xbc = _mm("in_ssdx", h, pw[2], pc[2])
        ssd_z = _mm("in_ssdz", h, pw[4], pc[4])
        gdn_qkv = _mm("in_gdnq", h, pw[5], pc[5])
        gdn_z = _mm("in_gdnz", h, pw[8], pc[8])
        gate = _mm("in_gate", h, pw[9], pc[9])
        small = _mm("in_small", h, small_cols(pw), small_cols(pc))

        zeros_tail = jnp.zeros((LANES - n_small,), F32)
        bias = jnp.concatenate([w['ssd_dt_bias'][l], w['gdn_dt_bias'][l], jnp.zeros((GDN_HEADS,), F32), zeros_tail])[None]
        scale = jnp.concatenate([jnp.ones((SSD_HEADS,), F32), -jnp.exp(w['gdn_a_log'][l]),
                                 jnp.zeros((GDN_HEADS,), F32), zeros_tail])[None]
        (sm,) = row("small_act", 'small_act', 1, 2, [LANES], 832, True)(small, bias, scale)

        tb = _s5_tables(w['s5_a_re'][l], w['s5_a_im'][l], w['s5_log_step'][l], w['s5_b_re'][l], w['s5_b_im'][l],
                        w['s5_c_re'][l], w['s5_c_im'][l])
        srows = tb['a_re'].shape[0]
        bu_re = _make_gmm("s5_bre")(s5_u, tb['b_re']).reshape(t_all, srows, LANES)
        bu_im = _make_gmm("s5_bim")(s5_u, tb['b_im']).reshape(t_all, srows, LANES)
        s_re, s_im = _make_s5_scan("s5_scan")(bu_re, bu_im, tb['a_re'], tb['a_im'])
        y_re = _make_gmm("s5_cre")(s_re.reshape(t_all, srows * LANES), tb['c_re'])
        y_im = _make_gmm("s5_cim")(s_im.reshape(t_all, srows * LANES), tb['c_im'])
        (out_a,) = _make_rowwise("s5_tail", fns['s5_tail'], 4, 2, [d_model], 208, n_wt=2)(
            y_re, y_im, s5_u, s5_z, w['s5_d'][l][None], w['s5_b_glu'][l][None],
            mats['glu'][l], mats['branch'][l, 0], w['c_glu'][l], w['c_branch'][l, 0])

        xbc = _make_conv_rowwise("ssd_conv", fns['ssd_conv'], 2, ssd_xbc.shape[1], 416, True)(
            ssd_xbc, w['ssd_conv_w'][l], w['ssd_conv_b'][l][None])
        a_c = jnp.pad(-jnp.exp(w['ssd_a_log'][l]), (0, LANES - SSD_HEADS))[None]
        d_exp = jnp.repeat(w['ssd_d'][l], SSD_HEAD)[None]
        (y_ssd,) = _make_chunk_scan("ssd_scan", _ssd_chunk, (SSD_HEADS // 2 * LANES, SSD_STATE), 2, 2, [BRANCH])(
            xbc, sm, a_c, d_exp)
        (out_b,) = _make_rowwise("ssd_tail", fns['ssd_tail'], 2, 1, [d_model], 416, n_wt=1)(
            y_ssd, ssd_z, w['ssd_norm_g'][l][None], mats['branch'][l, 1], w['c_branch'][l, 1])

        qkv = _make_conv_rowwise("gdn_conv", fns['gdn_conv'], 1, 3 * BRANCH, 208)(gdn_qkv, w['gdn_conv_w'][l])
        (o_gdn,) = _make_chunk_scan("gdn_scan", _gdn_chunk, (GDN_HEADS * GDN_HEAD, GDN_HEAD), 2, 0, [BRANCH])(qkv, sm)
        (out_c,) = _make_rowwise("gdn_tail", fns['gdn_tail'], 2, 1, [d_model], 416, n_wt=1)(
            o_gdn, gdn_z, w['gdn_norm_g'][l][None], mats['branch'][l, 2], w['c_branch'][l, 2])

        (h,) = _make_rowwise("merge_out", fns['merge_out'], 5, 3, [d_model], 208, True, n_wt=1)(
            out_a, out_b, out_c, gate, h, w['b_gate'][l].reshape(1, 3 * d_model), w['ln_g'][l][None],
            w['ln_b'][l][None], mats['out'][l], w['c_out'][l])

    tgt = jnp.concatenate([jnp.zeros((first, d_model), F32), target], axis=0)
    (rows_loss,) = row("loss", 'loss', 2, 0, [LANES], 416, True)(h, tgt)
    return jnp.sum(rows_loss)


def _in_overlaps(d_model, n_loc):
    offs = [0]
    for wd in _in_widths(d_model):
        offs.append(offs[-1] + wd)
    out = []
    for i in range(len(offs) - 1):
        c0, c1 = offs[i], offs[i + 1]
        segs = []
        for k in range(N_DEV):
            g0, g1 = max(c0, k * n_loc), min(c1, (k + 1) * n_loc)
            if g0 < g1:
                segs.append((k, g0 - k * n_loc, g1 - k * n_loc, g0 - c0))
        out.append(segs)
    return out


MAT_WEIGHTS = ['s5_w_glu', 'w_branch', 'w_out']


def _step(x, target, w_loc, m_loc, v_loc):
    rest = [n for n in WEIGHTS if n in SHARD_DIM and n != 'w_in']
    small = [n for n in rest if n not in MAT_WEIGHTS]
    repl = [n for n in WEIGHTS if n not in SHARD_DIM]
    size = lambda names: sum(int(w_loc[n].size) for n in names)
    rows_mat, rows_small = _rows_for(size(MAT_WEIGHTS)), _rows_for(size(small))
    n_rp = size(repl) + 1
    rows_rp = _rows_for(n_rp)
    depth, d_model, n_loc = w_loc['w_in'].shape
    overlaps = _in_overlaps(d_model, n_loc)

    mat_flat = _pad_flat(jnp.concatenate([w_loc[n].reshape(-1) for n in MAT_WEIGHTS]), rows_mat).astype(BF16)
    small_flat = _pad_flat(jnp.concatenate([w_loc[n].reshape(-1) for n in small]), rows_small)
    g_in, g_mat, g_small = _gather([w_loc['w_in'].astype(BF16), mat_flat, small_flat], "gather_weights")
    full = {}
    for names, buf in ((MAT_WEIGHTS, g_mat.reshape(N_DEV, -1)), (small, g_small.reshape(N_DEV, -1))):
        off = 0
        for n in names:
            sz = int(w_loc[n].size)
            full[n] = _join_shards(buf[:, off:off + sz], w_loc[n].shape, SHARD_DIM[n])
            off += sz
    mats = dict(glu=full['s5_w_glu'], branch=full['w_branch'], out=full['w_out'], **{
        'in': [[jnp.concatenate([g_in[k, l, :, lo:hi] for k, lo, hi, _ in segs], axis=1) for segs in overlaps]
               for l in range(depth)]})
    w_diff = {n: w_loc[n] for n in repl}
    w_diff.update({n: full[n] for n in small})
    w_diff['c_in'] = [[jnp.zeros((d_model, wd), F32) for wd in _in_widths(d_model)] for _ in range(depth)]
    w_diff['c_glu'] = jnp.zeros(full['s5_w_glu'].shape, F32)
    w_diff['c_branch'] = jnp.zeros(full['w_branch'].shape, F32)
    w_diff['c_out'] = jnp.zeros(full['w_out'].shape, F32)

    loss, (g_w, g_x) = jax.value_and_grad(_local_loss, argnums=(0, 2))(w_diff, mats, x[0], target[0])
    g_w['s5_w_glu'], g_w['w_branch'], g_w['w_out'] = g_w['c_glu'], g_w['c_branch'], g_w['c_out']

    def shard_rows(names, rows_n):
        buf = jnp.concatenate([_split_shards(g_w[n], SHARD_DIM[n]) for n in names], axis=1)
        return jnp.pad(buf, ((0, 0), (0, rows_n * LANES - buf.shape[1])))

    send_in = jnp.stack([jnp.stack([
        jnp.concatenate([g_w['c_in'][l][i][:, plo:plo + hi - lo]
                         for i, segs in enumerate(overlaps) for (kk, lo, hi, plo) in segs if kk == k], axis=1)
        for l in range(depth)]) for k in range(N_DEV)]).astype(BF16)
    send_mat = shard_rows(MAT_WEIGHTS, rows_mat).astype(BF16).reshape(N_DEV, rows_mat, LANES)
    rp_vec = jnp.concatenate([g_w[n].reshape(-1) for n in repl] + [loss.reshape(1)])
    rp_vec = jnp.pad(rp_vec, (0, rows_rp * LANES - n_rp))
    rows_f32 = rows_small + rows_rp
    send_f32 = jnp.concatenate([shard_rows(small, rows_small),
                                jnp.broadcast_to(rp_vec[None], (N_DEV, rows_rp * LANES))], axis=1)
    n_chip = N_DEV // 2
    (kept_in, recv_in), (kept_mat, recv_mat) = _pair_exchange(
        [send_in.reshape((n_chip, 2) + send_in.shape[1:]), send_mat.reshape(n_chip, 2, rows_mat, LANES)],
        "pair_exchange")
    sum_in = _pair_sum(kept_in.reshape(n_chip, depth * d_model, n_loc), recv_in.reshape(n_chip, depth * d_model, n_loc),
                       "pair_sum_w_in").reshape(n_chip, depth, d_model, n_loc)
    sum_mat = _pair_sum(kept_mat, recv_mat, "pair_sum_mat")
    parts_in, parts_mat, parts_f32 = _chip_exchange(
        [sum_in, sum_mat], [send_f32.reshape(N_DEV, rows_f32, LANES)], "exchange_grads")

    def flat_mat(src):
        return _pad_flat(jnp.concatenate([src[n].reshape(-1) for n in MAT_WEIGHTS]), rows_mat)

    def flat_f32(src):
        sm_vec = jnp.concatenate([src[n].reshape(-1) for n in small])
        sm_vec = jnp.pad(sm_vec, (0, rows_small * LANES - sm_vec.shape[0]))
        rep = jnp.concatenate([src[n].reshape(-1) for n in repl] + [jnp.ones((1,), F32)])
        rep = jnp.pad(rep, (0, rows_rp * LANES - n_rp))
        return jnp.concatenate([sm_vec, rep]).reshape(rows_f32, LANES)

    in_outs = _adamw_stacked(parts_in, w_loc['w_in'], m_loc['w_in'], v_loc['w_in'])
    mat_outs = _adamw_flat(parts_mat, flat_mat(w_loc), flat_mat(m_loc), flat_mat(v_loc), "adamw_mat")
    f32_outs = _adamw_flat(parts_f32, flat_f32(w_loc), flat_f32(m_loc), flat_f32(v_loc), "adamw_rest")

    def unflat(big, mat_buf, f32_buf):
        out = {'w_in': big}
        for names, vec, o in ((MAT_WEIGHTS, mat_buf.reshape(-1), 0), (small, f32_buf.reshape(-1), 0),
                              (repl, f32_buf.reshape(-1), rows_small * LANES)):
            for n in names:
                sz = int(w_loc[n].size)
                out[n] = vec[o:o + sz].reshape(w_loc[n].shape)
                o += sz
        return out, f32_buf.reshape(-1)[rows_small * LANES + n_rp - 1]

    (grads, loss_total), (deltas, _), (new_m, _), (new_v, _) = [
        unflat(b, a, f) for b, a, f in zip(in_outs, mat_outs, f32_outs)]
    return (loss_total, g_x[None], *[grads[n] for n in WEIGHTS], *[deltas[n] for n in WEIGHTS],
            *[new_m[n] for n in WEIGHTS], *[new_v[n] for n in WEIGHTS])


def kernel(x, meta, ln_in_g, ln_in_b, w_in, s5_a_re, s5_a_im, s5_log_step, s5_b_re, s5_b_im, s5_c_re, s5_c_im, s5_d, s5_w_glu, s5_b_glu, ssd_conv_w, ssd_conv_b, ssd_dt_bias, ssd_a_log, ssd_d, ssd_norm_g, gdn_conv_w, gdn_dt_bias, gdn_a_log, gdn_norm_g, w_branch, b_gate, w_out, ln_g, ln_b, loss_target, m_meta, m_ln_in_g, m_ln_in_b, m_w_in, m_s5_a_re, m_s5_a_im, m_s5_log_step, m_s5_b_re, m_s5_b_im, m_s5_c_re, m_s5_c_im, m_s5_d, m_s5_w_glu, m_s5_b_glu, m_ssd_conv_w, m_ssd_conv_b, m_ssd_dt_bias, m_ssd_a_log, m_ssd_d, m_ssd_norm_g, m_gdn_conv_w, m_gdn_dt_bias, m_gdn_a_log, m_gdn_norm_g, m_w_branch, m_b_gate, m_w_out, m_ln_g, m_ln_b, v_meta, v_ln_in_g, v_ln_in_b, v_w_in, v_s5_a_re, v_s5_a_im, v_s5_log_step, v_s5_b_re, v_s5_b_im, v_s5_c_re, v_s5_c_im, v_s5_d, v_s5_w_glu, v_s5_b_glu, v_ssd_conv_w, v_ssd_conv_b, v_ssd_dt_bias, v_ssd_a_log, v_ssd_d, v_ssd_norm_g, v_gdn_conv_w, v_gdn_dt_bias, v_gdn_a_log, v_gdn_norm_g, v_w_branch, v_b_gate, v_w_out, v_ln_g, v_ln_b):
    w_loc = dict(zip(WEIGHTS, (meta, ln_in_g, ln_in_b, w_in, s5_a_re, s5_a_im, s5_log_step, s5_b_re, s5_b_im, s5_c_re, s5_c_im, s5_d, s5_w_glu, s5_b_glu, ssd_conv_w, ssd_conv_b, ssd_dt_bias, ssd_a_log, ssd_d, ssd_norm_g, gdn_conv_w, gdn_dt_bias, gdn_a_log, gdn_norm_g, w_branch, b_gate, w_out, ln_g, ln_b)))
    m_loc = dict(zip(WEIGHTS, (m_meta, m_ln_in_g, m_ln_in_b, m_w_in, m_s5_a_re, m_s5_a_im, m_s5_log_step, m_s5_b_re, m_s5_b_im, m_s5_c_re, m_s5_c_im, m_s5_d, m_s5_w_glu, m_s5_b_glu, m_ssd_conv_w, m_ssd_conv_b, m_ssd_dt_bias, m_ssd_a_log, m_ssd_d, m_ssd_norm_g, m_gdn_conv_w, m_gdn_dt_bias, m_gdn_a_log, m_gdn_norm_g, m_w_branch, m_b_gate, m_w_out, m_ln_g, m_ln_b)))
    v_loc = dict(zip(WEIGHTS, (v_meta, v_ln_in_g, v_ln_in_b, v_w_in, v_s5_a_re, v_s5_a_im, v_s5_log_step, v_s5_b_re, v_s5_b_im, v_s5_c_re, v_s5_c_im, v_s5_d, v_s5_w_glu, v_s5_b_glu, v_ssd_conv_w, v_ssd_conv_b, v_ssd_dt_bias, v_ssd_a_log, v_ssd_d, v_ssd_norm_g, v_gdn_conv_w, v_gdn_dt_bias, v_gdn_a_log, v_gdn_norm_g, v_w_branch, v_b_gate, v_w_out, v_ln_g, v_ln_b)))
    return _step(x, loss_target, w_loc, m_loc, v_loc)
```

```python
import functools
import math

import jax
import jax.numpy as jnp
from jax import lax
from jax.experimental import pallas as pl
from jax.experimental.pallas import tpu as pltpu

F32 = jnp.float32
BF16 = jnp.bfloat16

N_DEV = 8
LANES = 128
SUBLANES = 8
VMEM_LIMIT = 56 * 1024 * 1024
FLAT_ROWS = 1024

CHUNK = 64
CONV_K = 4
S5_GROUP = 16
S5_STATE = 64
S5_BLOCK_GROUPS = 8
SSD_HEAD = 64
SSD_HEADS = 12
SSD_GROUPS = 2
SSD_STATE = 128
GDN_HEAD = 128
GDN_HEADS = 6
BRANCH = 768
LN_EPS = 1e-5

ADAM_LR = 0.001
ADAM_B1 = 0.9
ADAM_B2 = 0.999
ADAM_EPS = 1e-08
ADAM_WD = 0.01
ADAM_STEP = 10

WEIGHTS = ['meta', 'ln_in_g', 'ln_in_b', 'w_in', 's5_a_re', 's5_a_im', 's5_log_step', 's5_b_re', 's5_b_im',
           's5_c_re', 's5_c_im', 's5_d', 's5_w_glu', 's5_b_glu', 'ssd_conv_w', 'ssd_conv_b', 'ssd_dt_bias',
           'ssd_a_log', 'ssd_d', 'ssd_norm_g', 'gdn_conv_w', 'gdn_dt_bias', 'gdn_a_log', 'gdn_norm_g',
           'w_branch', 'b_gate', 'w_out', 'ln_g', 'ln_b']
SHARD_DIM = {'meta': 1, 'w_in': 2, 's5_w_glu': 1, 'ssd_conv_w': 2, 'gdn_conv_w': 2, 'w_branch': 3, 'b_gate': 2,
             'w_out': 1}


def _params(sem):
    return pltpu.CompilerParams(dimension_semantics=sem, vmem_limit_bytes=VMEM_LIMIT)


def _row_tile(m, cap):
    best = None
    for t in range(SUBLANES, min(m, cap) + 1, SUBLANES):
        if m % t == 0:
            best = t
    return best if best is not None else m


def _col_tile(n, cap):
    best = None
    for t in range(LANES, min(n, cap) + 1, LANES):
        if n % t == 0:
            best = t
    return best if best is not None else n


def _any_tile(m, cap):
    best = 1
    for t in range(1, min(m, cap) + 1):
        if m % t == 0:
            best = t
    return best


def _mm_fwd(a, b, name):
    m, _ = a.shape
    g, k, n = b.shape
    tm, tn = _row_tile(m, 832), _col_tile(n, 1024)
    nj = n // tn

    def body(a_ref, b_ref, o_ref):
        o_ref[...] = jnp.dot(a_ref[...].astype(BF16), b_ref[0].astype(BF16), preferred_element_type=F32)

    return pl.pallas_call(
        body, name=name, grid=(g, m // tm, nj),
        in_specs=[pl.BlockSpec((tm, k), lambda gi, i, j: (i, gi)),
                  pl.BlockSpec((1, k, tn), lambda gi, i, j: (gi, 0, j))],
        out_specs=pl.BlockSpec((tm, tn), lambda gi, i, j: (i, gi * nj + j)),
        out_shape=jax.ShapeDtypeStruct((m, g * n), F32),
        compiler_params=_params(("arbitrary", "arbitrary", "arbitrary")),
    )(a, b)


def _mm_da(ct, b, name):
    m, _ = ct.shape
    g, k, n = b.shape
    tm, tk = _row_tile(m, 832 if n <= 1536 else 416), _col_tile(k, 1024)
    nk = k // tk

    def body(c_ref, b_ref, o_ref):
        o_ref[...] = lax.dot_general(c_ref[...].astype(BF16), b_ref[0].astype(BF16), (((1,), (1,)), ((), ())),
                                     preferred_element_type=F32)

    return pl.pallas_call(
        body, name=name, grid=(g, m // tm, nk),
        in_specs=[pl.BlockSpec((tm, n), lambda gi, i, j: (i, gi)),
                  pl.BlockSpec((1, tk, n), lambda gi, i, j: (gi, j, 0))],
        out_specs=pl.BlockSpec((tm, tk), lambda gi, i, j: (i, gi * nk + j)),
        out_shape=jax.ShapeDtypeStruct((m, g * k), F32),
        compiler_params=_params(("arbitrary", "arbitrary", "arbitrary")),
    )(ct, b)


def _mm_db(a, ct, g, k, n, name):
    m = a.shape[0]
    tm, tk, tn = _row_tile(m, 832), _col_tile(k, 1024), _col_tile(n, 1280)
    nk, nn = k // tk, n // tn

    def body(a_ref, c_ref, o_ref):
        @pl.when(pl.program_id(3) == 0)
        def _():
            o_ref[...] = jnp.zeros_like(o_ref)

        o_ref[0] += lax.dot_general(a_ref[...].astype(BF16), c_ref[...].astype(BF16), (((0,), (0,)), ((), ())),
                                    preferred_element_type=F32)

    return pl.pallas_call(
        body, name=name, grid=(g, nk, nn, m // tm),
        in_specs=[pl.BlockSpec((tm, tk), lambda gi, i, j, r: (r, gi * nk + i)),
                  pl.BlockSpec((tm, tn), lambda gi, i, j, r: (r, gi * nn + j))],
        out_specs=pl.BlockSpec((1, tk, tn), lambda gi, i, j, r: (gi, i, j)),
        out_shape=jax.ShapeDtypeStruct((g, k, n), F32),
        compiler_params=_params(("arbitrary", "arbitrary", "arbitrary", "arbitrary")),
    )(a, ct)


def _make_gmm(name):
    @jax.custom_vjp
    def gmm(a, b):
        return _mm_fwd(a, b, name + "_fwd")

    def fwd(a, b):
        return _mm_fwd(a, b, name + "_fwd"), (a, b)

    def bwd(res, ct):
        a, b = res
        g, k, n = b.shape
        return _mm_da(ct, b, name + "_da"), _mm_db(a, ct, g, k, n, name + "_db")

    gmm.defvjp(fwd, bwd)
    return gmm


def _mm(name, a, w, carrier):
    @jax.custom_vjp
    def mm(a, w3, carrier3):
        return _mm_fwd(a, w3, name + "_fwd")

    def fwd(a, w3, carrier3):
        return _mm_fwd(a, w3, name + "_fwd"), (a, w3)

    def bwd(res, ct):
        a, w3 = res
        g, k, n = w3.shape
        return _mm_da(ct, w3, name + "_da"), jnp.zeros_like(w3), _mm_db(a, ct, g, k, n, name + "_db")

    mm.defvjp(fwd, bwd)
    return mm(a, w[None], carrier[None])


@jax.custom_vjp
def _wdot(x, w, carrier):
    return _dot(x, w)


def _wdot_fwd(x, w, carrier):
    return _dot(x, w), (x, w)


def _wdot_bwd(res, ct):
    x, w = res
    return _dot_nt(ct, w), jnp.zeros_like(w), _dot_tn(x, ct)


_wdot.defvjp(_wdot_fwd, _wdot_bwd)


def _make_rowwise(name, fn, n_row, n_par, out_widths, tm_cap, use_ridx=False, n_wt=0):
    n_out = len(out_widths)
    n_in = n_row + n_par + n_wt

    def bind(tm):
        if not use_ridx:
            return fn
        ridx = pl.program_id(0) * tm + lax.broadcasted_iota(jnp.int32, (tm, 1), 0)
        return functools.partial(fn, ridx)

    def specs(args, tm):
        rows = [pl.BlockSpec((tm, a.shape[1]), lambda i: (i, 0)) for a in args[:n_row]]
        pars = [pl.BlockSpec(a.shape, lambda i: (0, 0)) for a in args[n_row:n_in]]
        return rows, pars

    def fwd_call(*args):
        t = args[0].shape[0]
        tm = _row_tile(t, tm_cap)
        rows, pars = specs(args, tm)

        def body(*refs):
            vals = [r[...] for r in refs[:n_row + n_par]]
            mats = [functools.partial(lambda x, w: _dot(x, w), w=r[...]) for r in refs[n_row + n_par:n_in]]
            res = bind(tm)(*vals, *mats)
            for o_ref, r in zip(refs[n_in:], res):
                o_ref[...] = r

        outs = pl.pallas_call(
            body, name=name + "_fwd", grid=(t // tm,), in_specs=rows + pars,
            out_specs=[pl.BlockSpec((tm, w), lambda i: (i, 0)) for w in out_widths],
            out_shape=[jax.ShapeDtypeStruct((t, w), F32) for w in out_widths],
            compiler_params=_params(("arbitrary",)),
        )(*args)
        return tuple(outs)

    def bwd_call(args, cts):
        t = args[0].shape[0]
        tm = _row_tile(t, tm_cap)
        rows, pars = specs(args, tm)
        ct_specs = [pl.BlockSpec((tm, w), lambda i: (i, 0)) for w in out_widths]
        n_diff = n_row + n_par

        def body(*refs):
            vals = [r[...] for r in refs[:n_diff]]
            wts = [r[...] for r in refs[n_diff:n_in]]
            ct_vals = tuple(r[...] for r in refs[n_in:n_in + n_out])
            d_refs = refs[n_in + n_out:]
            f = bind(tm)

            def g(*a):
                mats = [functools.partial(lambda x, w, c: _wdot(x, w, c), w=w, c=c)
                        for w, c in zip(wts, a[n_diff:])]
                return tuple(f(*a[:n_diff], *mats))

            _, vjp = jax.vjp(g, *vals, *[jnp.zeros(w.shape, F32) for w in wts])
            grads = vjp(ct_vals)
            for i in range(n_row):
                d_refs[i][...] = grads[i]
            if n_par + n_wt:
                @pl.when(pl.program_id(0) == 0)
                def _():
                    for r in d_refs[n_row:]:
                        r[...] = jnp.zeros_like(r)

                for r, gr in zip(d_refs[n_row:], grads[n_row:]):
                    r[...] += gr

        outs = pl.pallas_call(
            body, name=name + "_bwd", grid=(t // tm,), in_specs=rows + pars + ct_specs,
            out_specs=rows + pars,
            out_shape=[jax.ShapeDtypeStruct(a.shape, F32) for a in args],
            compiler_params=_params(("arbitrary",)),
        )(*args, *cts)
        return tuple(outs)

    @jax.custom_vjp
    def op(*args):
        return fwd_call(*args[:n_in])

    def fwd(*args):
        return fwd_call(*args[:n_in]), args[:n_in]

    def bwd(args, cts):
        grads = bwd_call(args, cts)
        return grads[:n_row + n_par] + tuple(jnp.zeros_like(a) for a in args[n_row + n_par:]) + grads[n_row + n_par:]

    op.defvjp(fwd, bwd)
    return op


HALO = SUBLANES


def _make_conv_rowwise(name, fn, n_par, out_width, tm_cap, use_ridx=False):
    def bind(ridx):
        return functools.partial(fn, ridx) if use_ridx else fn

    def stage(x_ref, halo_ref, xs, first):
        xs[0:HALO, :] = jnp.where(first, 0.0, halo_ref[...])
        xs[HALO:, :] = x_ref[...]

    def taps(xs, tm):
        return [xs[pl.ds(HALO - (CONV_K - 1) + j, tm), :] for j in range(CONV_K)]

    def fwd_call(x, *pars):
        t, wd = x.shape
        tm = _row_tile(t, tm_cap)
        per = tm // HALO

        def body(*refs):
            x_ref, halo_ref = refs[:2]
            par_refs, o_ref, xs = refs[2:2 + n_par], refs[2 + n_par], refs[-1]
            i = pl.program_id(0)
            stage(x_ref, halo_ref, xs, i == 0)
            ridx = i * tm + lax.broadcasted_iota(jnp.int32, (tm, 1), 0)
            (o_ref[...],) = bind(ridx)(*taps(xs, tm), *[r[...] for r in par_refs])

        return pl.pallas_call(
            body, name=name + "_fwd", grid=(t // tm,),
            in_specs=[pl.BlockSpec((tm, wd), lambda i: (i, 0)),
                      pl.BlockSpec((HALO, wd), lambda i: (jnp.maximum(i * per - 1, 0), 0))]
            + [pl.BlockSpec(p.shape, lambda i: (0, 0)) for p in pars],
            out_specs=pl.BlockSpec((tm, out_width), lambda i: (i, 0)),
            out_shape=jax.ShapeDtypeStruct((t, out_width), F32),
            scratch_shapes=[pltpu.VMEM((tm + HALO, wd), F32)],
            compiler_params=_params(("arbitrary",)),
        )(x, x, *pars)

    def bwd_call(x, pars, ct):
        t, wd = x.shape
        tm = _row_tile(t, tm_cap)
        per = tm // HALO
        nb = t // tm

        def body(*refs):
            x_ref, halo_ref = refs[:2]
            par_refs, ct_ref = refs[2:2 + n_par], refs[2 + n_par]
            dx_ref, dpar_refs = refs[3 + n_par], refs[4 + n_par:4 + 2 * n_par]
            xs, ds, carry = refs[-3:]
            step = pl.program_id(0)
            blk = nb - 1 - step

            @pl.when(step == 0)
            def _():
                carry[...] = jnp.zeros_like(carry)
                for r in dpar_refs:
                    r[...] = jnp.zeros_like(r)

            stage(x_ref, halo_ref, xs, blk == 0)
            ridx = blk * tm + lax.broadcasted_iota(jnp.int32, (tm, 1), 0)
            f = bind(ridx)
            _, vjp = jax.vjp(lambda *a: tuple(f(*a)), *taps(xs, tm), *[r[...] for r in par_refs])
            grads = vjp((ct_ref[...],))
            ds[...] = jnp.zeros_like(ds)
            for j in range(CONV_K):
                ds[pl.ds(HALO - (CONV_K - 1) + j, tm), :] += grads[j]
            ds[pl.ds(tm, HALO), :] += carry[...]
            dx_ref[...] = ds[HALO:, :]
            carry[...] = ds[0:HALO, :]
            for r, g in zip(dpar_refs, grads[CONV_K:]):
                r[...] += g

        rev = lambda i: (nb - 1 - i, 0)
        outs = pl.pallas_call(
            body, name=name + "_bwd", grid=(nb,),
            in_specs=[pl.BlockSpec((tm, wd), rev),
                      pl.BlockSpec((HALO, wd), lambda i: (jnp.maximum((nb - 1 - i) * per - 1, 0), 0))]
            + [pl.BlockSpec(p.shape, lambda i: (0, 0)) for p in pars]
            + [pl.BlockSpec((tm, out_width), rev)],
            out_specs=[pl.BlockSpec((tm, wd), rev)] + [pl.BlockSpec(p.shape, lambda i: (0, 0)) for p in pars],
            out_shape=[jax.ShapeDtypeStruct(x.shape, F32)] + [jax.ShapeDtypeStruct(p.shape, F32) for p in pars],
            scratch_shapes=[pltpu.VMEM((tm + HALO, wd), F32), pltpu.VMEM((tm + HALO, wd), F32),
                            pltpu.VMEM((HALO, wd), F32)],
            compiler_params=_params(("arbitrary",)),
        )(x, x, *pars, ct)
        return tuple(outs)

    @jax.custom_vjp
    def op(x, *pars):
        return fwd_call(x, *pars)

    def fwd(x, *pars):
        return fwd_call(x, *pars), (x, pars)

    def bwd(res, ct):
        x, pars = res
        return bwd_call(x, pars, ct)

    op.defvjp(fwd, bwd)
    return op


def _make_chunk_scan(name, fn, state_shape, n_seq, n_par, out_widths):
    n_out = len(out_widths)
    zeros_idx = (0,) * len(state_shape)

    def fwd_call(*args):
        t = args[0].shape[0]
        nc = t // CHUNK
        seq_specs = [pl.BlockSpec((CHUNK, a.shape[1]), lambda c: (c, 0)) for a in args[:n_seq]]
        par_specs = [pl.BlockSpec(a.shape, lambda c: (0, 0)) for a in args[n_seq:]]

        def body(*refs):
            ins = refs[:n_seq + n_par]
            out_refs = refs[n_seq + n_par:n_seq + n_par + n_out]
            states_ref = refs[n_seq + n_par + n_out]
            st = refs[-1]

            @pl.when(pl.program_id(0) == 0)
            def _():
                st[...] = jnp.zeros_like(st)

            s0 = st[...]
            states_ref[0] = s0
            res = fn(s0, *[r[...] for r in ins])
            st[...] = res[0]
            for o_ref, r in zip(out_refs, res[1:]):
                o_ref[...] = r

        outs = pl.pallas_call(
            body, name=name + "_fwd", grid=(nc,), in_specs=seq_specs + par_specs,
            out_specs=[pl.BlockSpec((CHUNK, w), lambda c: (c, 0)) for w in out_widths]
            + [pl.BlockSpec((1,) + state_shape, lambda c: (c,) + zeros_idx)],
            out_shape=[jax.ShapeDtypeStruct((t, w), F32) for w in out_widths]
            + [jax.ShapeDtypeStruct((nc,) + state_shape, F32)],
            scratch_shapes=[pltpu.VMEM(state_shape, F32)],
            compiler_params=_params(("arbitrary",)),
        )(*args)
        return tuple(outs[:n_out]), outs[n_out]

    def bwd_call(args, states, cts):
        t = args[0].shape[0]
        nc = t // CHUNK
        rev = lambda c: (nc - 1 - c, 0)
        seq_specs = [pl.BlockSpec((CHUNK, a.shape[1]), rev) for a in args[:n_seq]]
        par_specs = [pl.BlockSpec(a.shape, lambda c: (0, 0)) for a in args[n_seq:]]
        ct_specs = [pl.BlockSpec((CHUNK, w), rev) for w in out_widths]
        st_spec = pl.BlockSpec((1,) + state_shape, lambda c: (nc - 1 - c,) + zeros_idx)
        n_in = n_seq + n_par

        def body(*refs):
            vals = [r[...] for r in refs[:n_in]]
            s0 = refs[n_in][0]
            ct_vals = tuple(r[...] for r in refs[n_in + 1:n_in + 1 + n_out])
            d_refs = refs[n_in + 1 + n_out:-1]
            dst = refs[-1]

            @pl.when(pl.program_id(0) == 0)
            def _():
                dst[...] = jnp.zeros_like(dst)
                for j in range(n_par):
                    d_refs[n_seq + j][...] = jnp.zeros_like(d_refs[n_seq + j])

            _, vjp = jax.vjp(lambda *a: tuple(fn(*a)), s0, *vals)
            grads = vjp((dst[...],) + ct_vals)
            dst[...] = grads[0]
            for i in range(n_seq):
                d_refs[i][...] = grads[1 + i]
            for j in range(n_par):
                d_refs[n_seq + j][...] += grads[1 + n_seq + j]

        outs = pl.pallas_call(
            body, name=name + "_bwd", grid=(nc,), in_specs=seq_specs + par_specs + [st_spec] + ct_specs,
            out_specs=seq_specs + par_specs,
            out_shape=[jax.ShapeDtypeStruct(a.shape, F32) for a in args],
            scratch_shapes=[pltpu.VMEM(state_shape, F32)],
            compiler_params=_params(("arbitrary",)),
        )(*args, states, *cts)
        return tuple(outs)

    @jax.custom_vjp
    def op(*args):
        return fwd_call(*args)[0]

    def fwd(*args):
        outs, states = fwd_call(*args)
        return outs, (args, states)

    def bwd(res, cts):
        args, states = res
        return bwd_call(args, states, cts)

    op.defvjp(fwd, bwd)
    return op


def _s5_scan_fwd(bre, bim, are, aim, name):
    t, r, _ = bre.shape
    tb = _any_tile(t, 208)
    blk = pl.BlockSpec((tb, r, LANES), lambda i: (i, 0, 0))
    par = pl.BlockSpec((r, LANES), lambda i: (0, 0))

    def body(bre_ref, bim_ref, are_ref, aim_ref, sre_ref, sim_ref, st):
        @pl.when(pl.program_id(0) == 0)
        def _():
            st[...] = jnp.zeros_like(st)

        ar, ai = are_ref[...], aim_ref[...]

        def step(k, carry):
            sr, si = carry
            nr = ar * sr - ai * si + bre_ref[k]
            ni = ar * si + ai * sr + bim_ref[k]
            sre_ref[k] = nr
            sim_ref[k] = ni
            return nr, ni

        sr, si = lax.fori_loop(0, tb, step, (st[0], st[1]), unroll=4)
        st[0] = sr
        st[1] = si

    return pl.pallas_call(
        body, name=name, grid=(t // tb,), in_specs=[blk, blk, par, par], out_specs=[blk, blk],
        out_shape=[jax.ShapeDtypeStruct(bre.shape, F32)] * 2,
        scratch_shapes=[pltpu.VMEM((2, r, LANES), F32)],
        compiler_params=_params(("arbitrary",)),
    )(bre, bim, are, aim)


def _s5_scan_bwd(dsr, dsi, sre, sim, are, aim, name):
    t, r, _ = sre.shape
    tb = _any_tile(t, 208)
    nb = t // tb
    blk = pl.BlockSpec((tb, r, LANES), lambda i: (nb - 1 - i, 0, 0))
    par = pl.BlockSpec((r, LANES), lambda i: (0, 0))

    def body(dsr_ref, dsi_ref, sre_ref, sim_ref, are_ref, aim_ref, gre_ref, gim_ref, dar_ref, dai_ref, st):
        @pl.when(pl.program_id(0) == 0)
        def _():
            st[...] = jnp.zeros_like(st)
            dar_ref[...] = jnp.zeros_like(dar_ref)
            dai_ref[...] = jnp.zeros_like(dai_ref)

        ar, ai = are_ref[...], aim_ref[...]

        def step(k, carry):
            gr, gi, dar, dai = carry
            q = tb - 1 - k
            s_r, s_i = sre_ref[q], sim_ref[q]
            dar = dar + gr * s_r + gi * s_i
            dai = dai + gi * s_r - gr * s_i
            ngr = dsr_ref[q] + ar * gr + ai * gi
            ngi = dsi_ref[q] + ar * gi - ai * gr
            gre_ref[q] = ngr
            gim_ref[q] = ngi
            return ngr, ngi, dar, dai

        gr, gi, dar, dai = lax.fori_loop(0, tb, step, (st[0], st[1], dar_ref[...], dai_ref[...]), unroll=4)
        st[0] = gr
        st[1] = gi
        dar_ref[...] = dar
        dai_ref[...] = dai

    return pl.pallas_call(
        body, name=name, grid=(nb,), in_specs=[blk, blk, blk, blk, par, par], out_specs=[blk, blk, par, par],
        out_shape=[jax.ShapeDtypeStruct(sre.shape, F32)] * 2 + [jax.ShapeDtypeStruct(are.shape, F32)] * 2,
        scratch_shapes=[pltpu.VMEM((2, r, LANES), F32)],
        compiler_params=_params(("arbitrary",)),
    )(dsr, dsi, sre, sim, are, aim)


def _make_s5_scan(name):
    @jax.custom_vjp
    def scan(bre, bim, are, aim):
        return tuple(_s5_scan_fwd(bre, bim, are, aim, name + "_fwd"))

    def fwd(bre, bim, are, aim):
        sre, sim = _s5_scan_fwd(bre, bim, are, aim, name + "_fwd")
        return (sre, sim), (sre, sim, are, aim)

    def bwd(res, cts):
        sre, sim, are, aim = res
        return tuple(_s5_scan_bwd(cts[0], cts[1], sre, sim, are, aim, name + "_bwd"))

    scan.defvjp(fwd, bwd)
    return scan


def _dot(a, b):
    return jnp.dot(a.astype(BF16), b.astype(BF16), preferred_element_type=F32)


def _dot_nt(a, b):
    return lax.dot_general(a.astype(BF16), b.astype(BF16), (((1,), (1,)), ((), ())), preferred_element_type=F32)


def _dot_tn(a, b):
    return lax.dot_general(a.astype(BF16), b.astype(BF16), (((0,), (0,)), ((), ())), preferred_element_type=F32)


def _dot_f32(a, b):
    return jnp.dot(a, b, precision=lax.Precision.HIGHEST, preferred_element_type=F32)


def _dot_tn_f32(a, b):
    return lax.dot_general(a, b, (((0,), (0,)), ((), ())), precision=lax.Precision.HIGHEST,
                           preferred_element_type=F32)


def _iota(shape, dim):
    return lax.broadcasted_iota(jnp.int32, shape, dim)


def _tri(strict=False):
    r, c = _iota((CHUNK, CHUNK), 0), _iota((CHUNK, CHUNK), 1)
    return (r > c) if strict else (r >= c)


def _silu(x):
    return x * jax.nn.sigmoid(x)


def _layer_norm(z, g, b):
    mu = jnp.mean(z, axis=-1, keepdims=True)
    var = jnp.mean(jnp.square(z - mu), axis=-1, keepdims=True)
    return (z - mu) * lax.rsqrt(var + LN_EPS) * g + b


def _ssd_chunk(state, xbc, sm, a_c, d_exp):
    width = SSD_HEADS * SSD_HEAD
    x = xbc[:, :width]
    lane = _iota((CHUNK, LANES), 1)
    dtc = jnp.where(lane < SSD_HEADS, sm, 0.0)
    low = _tri().astype(F32)
    eye = (_iota((CHUNK, CHUNK), 0) == _iota((CHUNK, CHUNK), 1)).astype(F32)
    head_col, head_row = _iota((LANES, width), 1), _iota((LANES, width), 0) * SSD_HEAD
    expand = ((head_col >= head_row) & (head_col < head_row + SSD_HEAD)).astype(F32)
    acum_c = _dot_f32(low, dtc * a_c)
    acum_ct = _dot_tn_f32(acum_c, eye)
    dt_exp = _dot_f32(dtc, expand)
    acum = _dot_f32(acum_c, expand)
    xd = x * dt_exp
    last = acum[CHUNK - 1:CHUNK, :]
    to_end = jnp.exp(last - acum)
    eac = jnp.exp(acum)
    causal = _tri()
    first_half = _iota((CHUNK, LANES), 1) < SSD_HEAD
    top_rows = _iota((LANES, LANES), 0) < SSD_HEAD
    pairs = range(SSD_HEADS // 2)
    grp = [(2 * p) // (SSD_HEADS // SSD_GROUPS) for p in pairs]
    cols = [slice(p * LANES, (p + 1) * LANES) for p in pairs]
    bg = [xbc[:, width + g * SSD_STATE: width + (g + 1) * SSD_STATE] for g in range(SSD_GROUPS)]
    cg = [xbc[:, width + (SSD_GROUPS + g) * SSD_STATE: width + (SSD_GROUPS + g + 1) * SSD_STATE]
          for g in range(SSD_GROUPS)]
    scores = [_dot_nt(cg[g], bg[g]) for g in range(SSD_GROUPS)]
    dec = [jnp.where(causal, jnp.exp(jnp.minimum(acum_c[:, h:h + 1] - acum_ct[h:h + 1, :], 0.0)), 0.0)
           for h in range(SSD_HEADS)]
    y_lo = [_dot(scores[grp[p]] * dec[2 * p], jnp.where(first_half, xd[:, cols[p]], 0.0)) for p in pairs]
    y_hi = [_dot(scores[grp[p]] * dec[2 * p + 1], jnp.where(first_half, 0.0, xd[:, cols[p]])) for p in pairs]
    s_prev = [state[p * LANES:(p + 1) * LANES, :] for p in pairs]
    y_off = [_dot_nt(cg[grp[p]], s_prev[p]) for p in pairs]
    s_add = [_dot_tn(xd[:, cols[p]] * to_end[:, cols[p]], bg[grp[p]]) for p in pairs]
    ys = [y_lo[p] + y_hi[p] + y_off[p] * eac[:, cols[p]] + x[:, cols[p]] * d_exp[:, cols[p]] for p in pairs]
    cd = [jnp.where(top_rows, jnp.exp(acum_c[CHUNK - 1:CHUNK, 2 * p:2 * p + 1]),
                    jnp.exp(acum_c[CHUNK - 1:CHUNK, 2 * p + 1:2 * p + 2])) for p in pairs]
    new_states = [s_prev[p] * cd[p] + s_add[p] for p in pairs]
    return jnp.concatenate(new_states, axis=0), jnp.concatenate(ys, axis=1)


def _gdn_chunk(state, qkv, sm):
    width = GDN_HEADS * GDN_HEAD
    g0, b0 = SSD_HEADS, SSD_HEADS + GDN_HEADS
    lane = _iota((CHUNK, LANES), 1)
    gc = jnp.where((lane >= g0) & (lane < b0), sm, 0.0)
    low = _tri().astype(F32)
    eye = (_iota((CHUNK, CHUNK), 0) == _iota((CHUNK, CHUNK), 1)).astype(F32)
    gcum = _dot_f32(low, gc)
    gcum_t = _dot_tn_f32(gcum, eye)
    causal, strict = _tri(), _tri(strict=True)
    heads = range(GDN_HEADS)
    q = [qkv[:, h * GDN_HEAD:(h + 1) * GDN_HEAD] for h in heads]
    k = [qkv[:, width + h * GDN_HEAD: width + (h + 1) * GDN_HEAD] for h in heads]
    v = [qkv[:, 2 * width + h * GDN_HEAD: 2 * width + (h + 1) * GDN_HEAD] for h in heads]
    beta = [sm[:, b0 + h:b0 + h + 1] for h in heads]
    gcol = [gcum[:, g0 + h:g0 + h + 1] for h in heads]
    glast = [gcum[CHUNK - 1:CHUNK, g0 + h:g0 + h + 1] for h in heads]
    gamma = [jnp.where(causal, jnp.exp(jnp.minimum(gcol[h] - gcum_t[g0 + h:g0 + h + 1, :], 0.0)), 0.0) for h in heads]
    kk = [_dot_nt(k[h], k[h]) for h in heads]
    qk = [_dot_nt(q[h], k[h]) for h in heads]
    egc = [jnp.exp(gcol[h]) for h in heads]
    nmat = [-jnp.where(strict, kk[h] * gamma[h] * beta[h], 0.0) for h in heads]
    sol = [jnp.concatenate([v[h] * beta[h], k[h] * (beta[h] * egc[h])], axis=1) for h in heads]
    for i in range(6):
        upd = [_dot(nmat[h], sol[h]) for h in heads]
        if i < 5:
            nmat = [_dot(nmat[h], nmat[h]) for h in heads]
        sol = [sol[h] + upd[h] for h in heads]
    s_prev = [state[h * GDN_HEAD:(h + 1) * GDN_HEAD, :] for h in heads]
    w_s = [_dot(sol[h][:, GDN_HEAD:], s_prev[h]) for h in heads]
    q_s = [_dot(q[h] * egc[h], s_prev[h]) for h in heads]
    v_new = [sol[h][:, :GDN_HEAD] - w_s[h] for h in heads]
    a_v = [_dot(qk[h] * gamma[h], v_new[h]) for h in heads]
    k_v = [_dot_tn(k[h] * jnp.exp(glast[h] - gcol[h]), v_new[h]) for h in heads]
    outs = [q_s[h] + a_v[h] for h in heads]
    new_states = [s_prev[h] * jnp.exp(glast[h]) + k_v[h] for h in heads]
    return jnp.concatenate(new_states, axis=0), jnp.concatenate(outs, axis=1)


def _row_fns(d_model, pad_rows, loss_rows, alpha):
    def keep(ridx, v):
        return jnp.where(ridx >= pad_rows, v, 0.0)

    def ln_in(ridx, h, g, b):
        return (keep(ridx, _layer_norm(h, g, b)),)

    def s5_tail(y_re, y_im, u, z, d, bias, glu, branch):
        v0 = jax.nn.gelu(y_re + y_im + d * u)
        return (branch(v0 * jax.nn.sigmoid(glu(v0) + bias) * _silu(z)),)

    def small_act(ridx, raw, bias, scale):
        lane = _iota(raw.shape, 1)
        sp = jax.nn.softplus(raw + bias)
        g0, b0 = SSD_HEADS, SSD_HEADS + GDN_HEADS
        out = jnp.where(lane < g0, sp, jnp.where(lane < b0, scale * sp,
                                                 jnp.where(lane < b0 + GDN_HEADS, jax.nn.sigmoid(raw), 0.0)))
        return (keep(ridx, out),)

    def conv(xs, w):
        acc = xs[0] * w[0:1, :]
        for j in range(1, CONV_K):
            acc = acc + xs[j] * w[j:j + 1, :]
        return acc

    def ssd_conv(ridx, x0, x1, x2, x3, w, b):
        return (keep(ridx, _silu(conv((x0, x1, x2, x3), w) + b)),)

    def ssd_tail(y, z, g, branch):
        v = y * _silu(z)
        return (branch(v * lax.rsqrt(jnp.mean(v * v, axis=-1, keepdims=True) + LN_EPS) * g),)

    def gdn_conv(x0, x1, x2, x3, w):
        a = _silu(conv((x0, x1, x2, x3), w))
        width = GDN_HEADS * GDN_HEAD
        parts = []
        for h in range(2 * GDN_HEADS):
            z = a[:, h * GDN_HEAD:(h + 1) * GDN_HEAD]
            z = z * lax.rsqrt(jnp.sum(z * z, axis=-1, keepdims=True) + 1e-6)
            parts.append(z * GDN_HEAD ** -0.5 if h < GDN_HEADS else z)
        parts.append(a[:, 2 * width:])
        return (jnp.concatenate(parts, axis=1),)

    def gdn_tail(o, z, g, branch):
        parts = []
        for h in range(GDN_HEADS):
            cols = slice(h * GDN_HEAD, (h + 1) * GDN_HEAD)
            oh = o[:, cols]
            oh = oh * lax.rsqrt(jnp.mean(oh * oh, axis=-1, keepdims=True) + LN_EPS) * g
            parts.append(oh * _silu(z[:, cols]))
        return (branch(jnp.concatenate(parts, axis=1)),)

    def merge_out(ridx, oa, ob, oc, gate, h, bias, g, b, w_out):
        acc = None
        for k, o in enumerate((oa, ob, oc)):
            cols = slice(k * d_model, (k + 1) * d_model)
            term = jax.nn.sigmoid(gate[:, cols] + bias[:, cols]) * o
            acc = term if acc is None else acc + term
        return (keep(ridx, _layer_norm(alpha * h + w_out(acc), g, b)),)

    def loss_rows_fn(ridx, h, tgt):
        row = 0.5 * jnp.mean(jnp.square(h - tgt), axis=-1, keepdims=True)
        row = jnp.where(ridx >= loss_rows, row, 0.0)
        lane = _iota((h.shape[0], LANES), 1)
        return (jnp.where(lane == 0, row, 0.0),)

    return dict(ln_in=ln_in, s5_tail=s5_tail, small_act=small_act, ssd_conv=ssd_conv, ssd_tail=ssd_tail,
                gdn_conv=gdn_conv, gdn_tail=gdn_tail, merge_out=merge_out, loss=loss_rows_fn)


def _pair_exchange(srcs, name):
    n_arr = len(srcs)
    n_chip = N_DEV // 2

    def body(*refs):
        src_refs, recv_refs = refs[:n_arr], refs[n_arr:2 * n_arr]
        send_sems, recv_sems = refs[2 * n_arr:]
        x, y, c = lax.axis_index("x"), lax.axis_index("y"), lax.axis_index("c")
        copies = []
        for a in range(n_arr):
            for q in range(n_chip):
                cp = pltpu.make_async_remote_copy(
                    src_ref=src_refs[a].at[q, 1 - c], dst_ref=recv_refs[a].at[q],
                    send_sem=send_sems.at[a * n_chip + q], recv_sem=recv_sems.at[a * n_chip + q],
                    device_id=(x, y, 1 - c), device_id_type=pl.DeviceIdType.MESH)
                cp.start()
                copies.append(cp)
        for cp in copies:
            cp.wait()

    return pl.pallas_call(
        body, name=name,
        in_specs=[pl.BlockSpec(memory_space=pl.ANY)] * n_arr, out_specs=[pl.BlockSpec(memory_space=pl.ANY)] * n_arr,
        out_shape=[jax.ShapeDtypeStruct((n_chip,) + tuple(s.shape[2:]), s.dtype) for s in srcs],
        scratch_shapes=[pltpu.SemaphoreType.DMA((n_arr * n_chip,)), pltpu.SemaphoreType.DMA((n_arr * n_chip,))],
    )(*srcs)


def _pair_sum(src, recv, core, name):
    n_chip, rows, cols = recv.shape
    tr = _row_tile(rows, 1024 if cols <= LANES else 512)

    def body(core_ref, a_ref, b_ref, o_ref):
        o_ref[...] = (a_ref[0].astype(F32) + b_ref[...].astype(F32)).astype(o_ref.dtype)

    blk = pl.BlockSpec((1, tr, cols), lambda q, i, core_ref: (q, i, 0))
    return pl.pallas_call(
        body, name=name,
        grid_spec=pltpu.PrefetchScalarGridSpec(
            num_scalar_prefetch=1, grid=(n_chip, rows // tr),
            in_specs=[pl.BlockSpec((1, 1, tr, cols), lambda q, i, core_ref: (q, core_ref[0], i, 0)), blk],
            out_specs=blk),
        out_shape=jax.ShapeDtypeStruct(recv.shape, recv.dtype),
        compiler_params=_params(("arbitrary", "arbitrary")),
    )(core, src, recv)


def _chip_exchange(chip_srcs, direct_srcs, name):
    n_chip_arr, n_dir = len(chip_srcs), len(direct_srcs)
    n_arr = n_chip_arr + n_dir
    chip_flips = [(1, 0), (0, 1), (1, 1)]
    dev_flips = [(fx, fy, fc) for fx in (0, 1) for fy in (0, 1) for fc in (0, 1)][1:]
    base = n_chip_arr * len(chip_flips)

    def body(*refs):
        src_refs, out_refs = refs[:n_arr], refs[n_arr:2 * n_arr]
        send_sems, recv_sems = refs[2 * n_arr:]
        x, y, c = lax.axis_index("x"), lax.axis_index("y"), lax.axis_index("c")
        my_chip, me = 2 * x + y, 4 * x + 2 * y + c
        copies = []
        for a in range(n_chip_arr):
            for k, (fx, fy) in enumerate(chip_flips):
                px = 1 - x if fx else x
                py = 1 - y if fy else y
                cp = pltpu.make_async_remote_copy(
                    src_ref=src_refs[a].at[2 * px + py], dst_ref=out_refs[a].at[my_chip],
                    send_sem=send_sems.at[a * 3 + k], recv_sem=recv_sems.at[a * 3 + k],
                    device_id=(px, py, c), device_id_type=pl.DeviceIdType.MESH)
                cp.start()
                copies.append(cp)
        for b in range(n_dir):
            a = n_chip_arr + b
            for k, (fx, fy, fc) in enumerate(dev_flips):
                px = 1 - x if fx else x
                py = 1 - y if fy else y
                pc = 1 - c if fc else c
                cp = pltpu.make_async_remote_copy(
                    src_ref=src_refs[a].at[4 * px + 2 * py + pc], dst_ref=out_refs[a].at[me],
                    send_sem=send_sems.at[base + b * 7 + k], recv_sem=recv_sems.at[base + b * 7 + k],
                    device_id=(px, py, pc), device_id_type=pl.DeviceIdType.MESH)
                cp.start()
                copies.append(cp)
        for cp in copies:
            cp.wait()

    n_sem = base + n_dir * len(dev_flips)
    srcs = list(chip_srcs) + list(direct_srcs)
    return pl.pallas_call(
        body, name=name,
        in_specs=[pl.BlockSpec(memory_space=pl.ANY)] * n_arr, out_specs=[pl.BlockSpec(memory_space=pl.ANY)] * n_arr,
        out_shape=[jax.ShapeDtypeStruct(s.shape, s.dtype) for s in srcs],
        scratch_shapes=[pltpu.SemaphoreType.DMA((n_sem,)), pltpu.SemaphoreType.DMA((n_sem,))],
    )(*srcs)


def _gather(srcs, name):
    n_arr = len(srcs)
    n_sem = N_DEV - 1

    def body(*refs):
        src_refs, out_refs = refs[:n_arr], refs[n_arr:2 * n_arr]
        send_sems, recv_sems = refs[2 * n_arr:]
        x, y, c = lax.axis_index("x"), lax.axis_index("y"), lax.axis_index("c")
        me, sibling = (x, y, c), (x, y, 1 - c)
        chips = [(1 - x, y), (x, 1 - y), (1 - x, 1 - y)]

        def slot(a, dev):
            return out_refs[a].at[4 * dev[0] + 2 * dev[1] + dev[2]]

        def copy(a, k, block, to, own=False):
            return pltpu.make_async_remote_copy(
                src_ref=src_refs[a] if own else slot(a, block), dst_ref=slot(a, block),
                send_sem=send_sems.at[a * n_sem + k], recv_sem=recv_sems.at[a * n_sem + k],
                device_id=to, device_id_type=pl.DeviceIdType.MESH)

        arrays = range(n_arr)
        first = [copy(a, 0, me, sibling, own=True) for a in arrays]
        first += [copy(a, 1 + j, me, (*chip, c), own=True) for j, chip in enumerate(chips) for a in arrays]
        for cp in first:
            cp.start()
        passed = []
        for j, chip in enumerate(chips):
            for a in arrays:
                copy(a, 1 + j, (*chip, c), me).wait_recv()
                fwd = copy(a, 4 + j, (*chip, c), sibling)
                fwd.start()
                passed.append(fwd)
        for a in arrays:
            copy(a, 0, sibling, me).wait_recv()
        for j, chip in enumerate(chips):
            for a in arrays:
                copy(a, 4 + j, (*chip, 1 - c), me).wait_recv()
        for cp in first + passed:
            cp.wait_send()

    return pl.pallas_call(
        body, name=name,
        in_specs=[pl.BlockSpec(memory_space=pl.ANY)] * n_arr, out_specs=[pl.BlockSpec(memory_space=pl.ANY)] * n_arr,
        out_shape=[jax.ShapeDtypeStruct((N_DEV,) + tuple(s.shape), s.dtype) for s in srcs],
        scratch_shapes=[pltpu.SemaphoreType.DMA((n_arr * n_sem,)), pltpu.SemaphoreType.DMA((n_arr * n_sem,))],
    )(*srcs)


def _adamw_body(p_ref, w_ref, m_ref, v_ref, g_ref, d_ref, nm_ref, nv_ref):
    bc1 = 1.0 - ADAM_B1 ** ADAM_STEP
    bc2 = 1.0 - ADAM_B2 ** ADAM_STEP
    g = p_ref[0].astype(F32)
    for k in range(1, p_ref.shape[0]):
        g = g + p_ref[k].astype(F32)
    nm = ADAM_B1 * m_ref[...] + (1.0 - ADAM_B1) * g
    nv = ADAM_B2 * v_ref[...] + (1.0 - ADAM_B2) * jnp.square(g)
    m_hat = nm / bc1
    v_hat = nv / bc2
    g_ref[...] = g
    d_ref[...] = -ADAM_LR * (m_hat / (jnp.sqrt(v_hat) + ADAM_EPS) + ADAM_WD * w_ref[...])
    nm_ref[...] = nm
    nv_ref[...] = nv


def _adamw_flat(parts, w, m, v, name):
    rows = w.shape[0]
    tr = FLAT_ROWS
    blk = pl.BlockSpec((tr, LANES), lambda i: (i, 0))
    return pl.pallas_call(
        functools.partial(_adamw_body), name=name, grid=(rows // tr,),
        in_specs=[pl.BlockSpec((parts.shape[0], tr, LANES), lambda i: (0, i, 0)), blk, blk, blk],
        out_specs=[blk] * 4, out_shape=[jax.ShapeDtypeStruct((rows, LANES), F32)] * 4,
        compiler_params=_params(("arbitrary",)),
    )(parts, w, m, v)


def _adamw_stacked(parts, w, m, v):
    depth, d, n = w.shape
    tr = _row_tile(d, 128)
    blk = pl.BlockSpec((1, tr, n), lambda l, i: (l, i, 0))
    return pl.pallas_call(
        functools.partial(_adamw_body), name="adamw_w_in", grid=(depth, d // tr),
        in_specs=[pl.BlockSpec((parts.shape[0], 1, tr, n), lambda l, i: (0, l, i, 0)), blk, blk, blk],
        out_specs=[blk] * 4, out_shape=[jax.ShapeDtypeStruct(w.shape, F32)] * 4,
        compiler_params=_params(("arbitrary", "arbitrary")),
    )(parts, w, m, v)


def _pad_flat(vec, rows):
    return jnp.pad(vec, (0, rows * LANES - vec.shape[0])).reshape(rows, LANES)


def _rows_for(n):
    return -(-n // (FLAT_ROWS * LANES)) * FLAT_ROWS


def _split_shards(full, dim):
    shp = full.shape
    parts = full.reshape(shp[:dim] + (N_DEV, shp[dim] // N_DEV) + shp[dim + 1:])
    return jnp.moveaxis(parts, dim, 0).reshape(N_DEV, -1)


def _join_shards(rows, local_shape, dim):
    parts = jnp.moveaxis(rows.reshape((N_DEV,) + tuple(local_shape)), 0, dim)
    shp = tuple(local_shape)
    return parts.reshape(shp[:dim] + (N_DEV * shp[dim],) + shp[dim + 1:])


def _s5_tables(a_re, a_im, log_step, b_re, b_im, c_re, c_im):
    lam_re = jnp.minimum(a_re, -1e-4)
    lam_im = a_im
    step = jnp.exp(log_step)[:, None]
    mag = jnp.exp(lam_re * step)
    abar_re, abar_im = mag * jnp.cos(lam_im * step), mag * jnp.sin(lam_im * step)
    den = lam_re * lam_re + lam_im * lam_im
    nr, ni = abar_re - 1.0, abar_im
    coef_re = (nr * lam_re + ni * lam_im) / den
    coef_im = (ni * lam_re - nr * lam_im) / den
    bbar_re = coef_re[..., None] * b_re - coef_im[..., None] * b_im
    bbar_im = coef_re[..., None] * b_im + coef_im[..., None] * b_re
    groups = a_re.shape[0]
    nblk = groups // S5_BLOCK_GROUPS
    eye = jnp.eye(S5_BLOCK_GROUPS, dtype=F32)

    def in_blocks(bb):
        t = jnp.swapaxes(bb, 1, 2).reshape(nblk, S5_BLOCK_GROUPS, S5_GROUP, S5_STATE)
        blk = jnp.einsum('ab,jacp->jacbp', eye, t)
        return blk.reshape(nblk, S5_BLOCK_GROUPS * S5_GROUP, S5_BLOCK_GROUPS * S5_STATE)

    def out_blocks(cc):
        t = jnp.swapaxes(cc, 1, 2).reshape(nblk, S5_BLOCK_GROUPS, S5_STATE, S5_GROUP)
        blk = jnp.einsum('ab,japc->japbc', eye, t)
        return blk.reshape(nblk, S5_BLOCK_GROUPS * S5_STATE, S5_BLOCK_GROUPS * S5_GROUP)

    rows = groups * S5_STATE // LANES
    return dict(b_re=in_blocks(bbar_re), b_im=in_blocks(bbar_im), c_re=out_blocks(c_re), c_im=out_blocks(-c_im),
                a_re=abar_re.reshape(rows, LANES), a_im=abar_im.reshape(rows, LANES))


def _in_widths(d_model):
    return [BRANCH, BRANCH, SSD_HEADS * SSD_HEAD + 2 * SSD_GROUPS * SSD_STATE, SSD_HEADS, BRANCH,
            3 * BRANCH, GDN_HEADS, GDN_HEADS, BRANCH, 3 * d_model]


def _local_loss(w, mats, x, target):
    n_meta, d_model = w['meta'].shape
    depth = len(mats['in'])
    seq = x.shape[0]
    pad_rows = CHUNK - n_meta
    first = pad_rows + n_meta
    t_all = first + seq
    alpha = (2 * depth) ** 0.25
    fns = _row_fns(d_model, pad_rows, first, alpha)
    row = lambda nm, key, n_row, n_par, widths, cap, ridx=False: _make_rowwise(nm, fns[key], n_row, n_par, widths, cap, ridx)

    h = jnp.concatenate([jnp.zeros((pad_rows, d_model), F32), w['meta'], x], axis=0)
    (h,) = row("ln_in", 'ln_in', 1, 2, [d_model], 416, True)(h, w['ln_in_g'][None], w['ln_in_b'][None])

    n_small = SSD_HEADS + 2 * GDN_HEADS

    def small_cols(ps):
        return jnp.pad(jnp.concatenate([ps[3], ps[6], ps[7]], axis=1), ((0, 0), (0, LANES - n_small)))

    for l in range(depth):
        pw, pc = mats['in'][l], w['c_in'][l]
        s5_u = _mm("in_s5u", h, pw[0], pc[0])
        s5_z = _mm("in_s5z", h, pw[1], pc[1])
        ssd_xbc = _mm("in_ssdx", h, pw[2], pc[2])
        ssd_z = _mm("in_ssdz", h, pw[4], pc[4])
        gdn_qkv = _mm("in_gdnq", h, pw[5], pc[5])
        gdn_z = _mm("in_gdnz", h, pw[8], pc[8])
        gate = _mm("in_gate", h, pw[9], pc[9])
        small = _mm("in_small", h, small_cols(pw), small_cols(pc))

        zeros_tail = jnp.zeros((LANES - n_small,), F32)
        bias = jnp.concatenate([w['ssd_dt_bias'][l], w['gdn_dt_bias'][l], jnp.zeros((GDN_HEADS,), F32), zeros_tail])[None]
        scale = jnp.concatenate([jnp.ones((SSD_HEADS,), F32), -jnp.exp(w['gdn_a_log'][l]),
                                 jnp.zeros((GDN_HEADS,), F32), zeros_tail])[None]
        (sm,) = row("small_act", 'small_act', 1, 2, [LANES], 832, True)(small, bias, scale)

        tb = _s5_tables(w['s5_a_re'][l], w['s5_a_im'][l], w['s5_log_step'][l], w['s5_b_re'][l], w['s5_b_im'][l],
                        w['s5_c_re'][l], w['s5_c_im'][l])
        srows = tb['a_re'].shape[0]
        bu_re = _make_gmm("s5_bre")(s5_u, tb['b_re']).reshape(t_all, srows, LANES)
        bu_im = _make_gmm("s5_bim")(s5_u, tb['b_im']).reshape(t_all, srows, LANES)
        s_re, s_im = _make_s5_scan("s5_scan")(bu_re, bu_im, tb['a_re'], tb['a_im'])
        y_re = _make_gmm("s5_cre")(s_re.reshape(t_all, srows * LANES), tb['c_re'])
        y_im = _make_gmm("s5_cim")(s_im.reshape(t_all, srows * LANES), tb['c_im'])
        (out_a,) = _make_rowwise("s5_tail", fns['s5_tail'], 4, 2, [d_model], 208, n_wt=2)(
            y_re, y_im, s5_u, s5_z, w['s5_d'][l][None], w['s5_b_glu'][l][None],
            mats['glu'][l], mats['branch'][l, 0], w['c_glu'][l], w['c_branch'][l, 0])

        xbc = _make_conv_rowwise("ssd_conv", fns['ssd_conv'], 2, ssd_xbc.shape[1], 416, True)(
            ssd_xbc, w['ssd_conv_w'][l], w['ssd_conv_b'][l][None])
        a_c = jnp.pad(-jnp.exp(w['ssd_a_log'][l]), (0, LANES - SSD_HEADS))[None]
        d_exp = jnp.repeat(w['ssd_d'][l], SSD_HEAD)[None]
        (y_ssd,) = _make_chunk_scan("ssd_scan", _ssd_chunk, (SSD_HEADS // 2 * LANES, SSD_STATE), 2, 2, [BRANCH])(
            xbc, sm, a_c, d_exp)
        (out_b,) = _make_rowwise("ssd_tail", fns['ssd_tail'], 2, 1, [d_model], 416, n_wt=1)(
            y_ssd, ssd_z, w['ssd_norm_g'][l][None], mats['branch'][l, 1], w['c_branch'][l, 1])

        qkv = _make_conv_rowwise("gdn_conv", fns['gdn_conv'], 1, 3 * BRANCH, 208)(gdn_qkv, w['gdn_conv_w'][l])
        (o_gdn,) = _make_chunk_scan("gdn_scan", _gdn_chunk, (GDN_HEADS * GDN_HEAD, GDN_HEAD), 2, 0, [BRANCH])(qkv, sm)
        (out_c,) = _make_rowwise("gdn_tail", fns['gdn_tail'], 2, 1, [d_model], 416, n_wt=1)(
            o_gdn, gdn_z, w['gdn_norm_g'][l][None], mats['branch'][l, 2], w['c_branch'][l, 2])

        (h,) = _make_rowwise("merge_out", fns['merge_out'], 5, 3, [d_model], 208, True, n_wt=1)(
            out_a, out_b, out_c, gate, h, w['b_gate'][l].reshape(1, 3 * d_model), w['ln_g'][l][None],
            w['ln_b'][l][None], mats['out'][l], w['c_out'][l])

    tgt = jnp.concatenate([jnp.zeros((first, d_model), F32), target], axis=0)
    (rows_loss,) = row("loss", 'loss', 2, 0, [LANES], 416, True)(h, tgt)
    return jnp.sum(rows_loss)


def _in_overlaps(d_model, n_loc):
    offs = [0]
    for wd in _in_widths(d_model):
        offs.append(offs[-1] + wd)
    out = []
    for i in range(len(offs) - 1):
        c0, c1 = offs[i], offs[i + 1]
        segs = []
        for k in range(N_DEV):
            g0, g1 = max(c0, k * n_loc), min(c1, (k + 1) * n_loc)
            if g0 < g1:
                segs.append((k, g0 - k * n_loc, g1 - k * n_loc, g0 - c0))
        out.append(segs)
    return out


MAT_WEIGHTS = ['s5_w_glu', 'w_branch', 'w_out']


def _step(x, target, w_loc, m_loc, v_loc):
    rest = [n for n in WEIGHTS if n in SHARD_DIM and n != 'w_in']
    small = [n for n in rest if n not in MAT_WEIGHTS]
    repl = [n for n in WEIGHTS if n not in SHARD_DIM]
    size = lambda names: sum(int(w_loc[n].size) for n in names)
    rows_mat, rows_small = _rows_for(size(MAT_WEIGHTS)), _rows_for(size(small))
    n_rp = size(repl) + 1
    rows_rp = _rows_for(n_rp)
    depth, d_model, n_loc = w_loc['w_in'].shape
    overlaps = _in_overlaps(d_model, n_loc)

    mat_flat = _pad_flat(jnp.concatenate([w_loc[n].reshape(-1) for n in MAT_WEIGHTS]), rows_mat).astype(BF16)
    small_flat = _pad_flat(jnp.concatenate([w_loc[n].reshape(-1) for n in small]), rows_small)
    me = 4 * lax.axis_index("x") + 2 * lax.axis_index("y") + lax.axis_index("c")
    my_chip = 2 * lax.axis_index("x") + lax.axis_index("y")
    own_blocks = [w_loc['w_in'].astype(BF16), mat_flat, small_flat]
    g_in, g_mat, g_small = [lax.dynamic_update_index_in_dim(got, blk, me, 0)
                            for got, blk in zip(_gather(own_blocks, "gather_weights"), own_blocks)]
    full = {}
    for names, buf in ((MAT_WEIGHTS, g_mat.reshape(N_DEV, -1)), (small, g_small.reshape(N_DEV, -1))):
        off = 0
        for n in names:
            sz = int(w_loc[n].size)
            full[n] = _join_shards(buf[:, off:off + sz], w_loc[n].shape, SHARD_DIM[n])
            off += sz
    mats = dict(glu=full['s5_w_glu'], branch=full['w_branch'], out=full['w_out'], **{
        'in': [[jnp.concatenate([g_in[k, l, :, lo:hi] for k, lo, hi, _ in segs], axis=1) for segs in overlaps]
               for l in range(depth)]})
    w_diff = {n: w_loc[n] for n in repl}
    w_diff.update({n: full[n] for n in small})
    w_diff['c_in'] = [[jnp.zeros((d_model, wd), F32) for wd in _in_widths(d_model)] for _ in range(depth)]
    w_diff['c_glu'] = jnp.zeros(full['s5_w_glu'].shape, F32)
    w_diff['c_branch'] = jnp.zeros(full['w_branch'].shape, F32)
    w_diff['c_out'] = jnp.zeros(full['w_out'].shape, F32)

    loss, (g_w, g_x) = jax.value_and_grad(_local_loss, argnums=(0, 2))(w_diff, mats, x[0], target[0])
    g_w['s5_w_glu'], g_w['w_branch'], g_w['w_out'] = g_w['c_glu'], g_w['c_branch'], g_w['c_out']

    def shard_rows(names, rows_n):
        buf = jnp.concatenate([_split_shards(g_w[n], SHARD_DIM[n]) for n in names], axis=1)
        return jnp.pad(buf, ((0, 0), (0, rows_n * LANES - buf.shape[1])))

    send_in = jnp.stack([jnp.stack([
        jnp.concatenate([g_w['c_in'][l][i][:, plo:plo + hi - lo]
                         for i, segs in enumerate(overlaps) for (kk, lo, hi, plo) in segs if kk == k], axis=1)
        for l in range(depth)]) for k in range(N_DEV)]).astype(BF16)
    send_mat = shard_rows(MAT_WEIGHTS, rows_mat).astype(BF16).reshape(N_DEV, rows_mat, LANES)
    rp_vec = jnp.concatenate([g_w[n].reshape(-1) for n in repl] + [loss.reshape(1)])
    rp_vec = jnp.pad(rp_vec, (0, rows_rp * LANES - n_rp))
    rows_f32 = rows_small + rows_rp
    send_f32 = jnp.concatenate([shard_rows(small, rows_small),
                                jnp.broadcast_to(rp_vec[None], (N_DEV, rows_rp * LANES))], axis=1)
    n_chip = N_DEV // 2
    core = lax.axis_index("c").astype(jnp.int32).reshape(1)
    send_in = send_in.reshape(n_chip, 2, depth * d_model, n_loc)
    send_mat = send_mat.reshape(n_chip, 2, rows_mat, LANES)
    send_f32 = send_f32.reshape(N_DEV, rows_f32, LANES)
    recv_in, recv_mat = _pair_exchange([send_in, send_mat], "pair_exchange")
    sum_in = _pair_sum(send_in, recv_in, core, "pair_sum_w_in")
    sum_mat = _pair_sum(send_mat, recv_mat, core, "pair_sum_mat")
    parts_in, parts_mat, parts_f32 = [
        lax.dynamic_update_index_in_dim(got, lax.dynamic_index_in_dim(src, idx, 0, keepdims=False), idx, 0)
        for got, src, idx in zip(_chip_exchange([sum_in, sum_mat], [send_f32], "exchange_grads"),
                                 (sum_in, sum_mat, send_f32), (my_chip, my_chip, me))]
    parts_in = parts_in.reshape(n_chip, depth, d_model, n_loc)

    def flat_mat(src):
        return _pad_flat(jnp.concatenate([src[n].reshape(-1) for n in MAT_WEIGHTS]), rows_mat)

    def flat_f32(src):
        sm_vec = jnp.concatenate([src[n].reshape(-1) for n in small])
        sm_vec = jnp.pad(sm_vec, (0, rows_small * LANES - sm_vec.shape[0]))
        rep = jnp.concatenate([src[n].reshape(-1) for n in repl] + [jnp.ones((1,), F32)])
        rep = jnp.pad(rep, (0, rows_rp * LANES - n_rp))
        return jnp.concatenate([sm_vec, rep]).reshape(rows_f32, LANES)

    in_outs = _adamw_stacked(parts_in, w_loc['w_in'], m_loc['w_in'], v_loc['w_in'])
    mat_outs = _adamw_flat(parts_mat, flat_mat(w_loc), flat_mat(m_loc), flat_mat(v_loc), "adamw_mat")
    f32_outs = _adamw_flat(parts_f32, flat_f32(w_loc), flat_f32(m_loc), flat_f32(v_loc), "adamw_rest")

    def unflat(big, mat_buf, f32_buf):
        out = {'w_in': big}
        for names, vec, o in ((MAT_WEIGHTS, mat_buf.reshape(-1), 0), (small, f32_buf.reshape(-1), 0),
                              (repl, f32_buf.reshape(-1), rows_small * LANES)):
            for n in names:
                sz = int(w_loc[n].size)
                out[n] = vec[o:o + sz].reshape(w_loc[n].shape)
                o += sz
        return out, f32_buf.reshape(-1)[rows_small * LANES + n_rp - 1]

    (grads, loss_total), (deltas, _), (new_m, _), (new_v, _) = [
        unflat(b, a, f) for b, a, f in zip(in_outs, mat_outs, f32_outs)]
    return (loss_total, g_x[None], *[grads[n] for n in WEIGHTS], *[deltas[n] for n in WEIGHTS],
            *[new_m[n] for n in WEIGHTS], *[new_v[n] for n in WEIGHTS])


def kernel(x, meta, ln_in_g, ln_in_b, w_in, s5_a_re, s5_a_im, s5_log_step, s5_b_re, s5_b_im, s5_c_re, s5_c_im, s5_d, s5_w_glu, s5_b_glu, ssd_conv_w, ssd_conv_b, ssd_dt_bias, ssd_a_log, ssd_d, ssd_norm_g, gdn_conv_w, gdn_dt_bias, gdn_a_log, gdn_norm_g, w_branch, b_gate, w_out, ln_g, ln_b, loss_target, m_meta, m_ln_in_g, m_ln_in_b, m_w_in, m_s5_a_re, m_s5_a_im, m_s5_log_step, m_s5_b_re, m_s5_b_im, m_s5_c_re, m_s5_c_im, m_s5_d, m_s5_w_glu, m_s5_b_glu, m_ssd_conv_w, m_ssd_conv_b, m_ssd_dt_bias, m_ssd_a_log, m_ssd_d, m_ssd_norm_g, m_gdn_conv_w, m_gdn_dt_bias, m_gdn_a_log, m_gdn_norm_g, m_w_branch, m_b_gate, m_w_out, m_ln_g, m_ln_b, v_meta, v_ln_in_g, v_ln_in_b, v_w_in, v_s5_a_re, v_s5_a_im, v_s5_log_step, v_s5_b_re, v_s5_b_im, v_s5_c_re, v_s5_c_im, v_s5_d, v_s5_w_glu, v_s5_b_glu, v_ssd_conv_w, v_ssd_conv_b, v_ssd_dt_bias, v_ssd_a_log, v_ssd_d, v_ssd_norm_g, v_gdn_conv_w, v_gdn_dt_bias, v_gdn_a_log, v_gdn_norm_g, v_w_branch, v_b_gate, v_w_out, v_ln_g, v_ln_b):
    w_loc = dict(zip(WEIGHTS, (meta, ln_in_g, ln_in_b, w_in, s5_a_re, s5_a_im, s5_log_step, s5_b_re, s5_b_im, s5_c_re, s5_c_im, s5_d, s5_w_glu, s5_b_glu, ssd_conv_w, ssd_conv_b, ssd_dt_bias, ssd_a_log, ssd_d, ssd_norm_g, gdn_conv_w, gdn_dt_bias, gdn_a_log, gdn_norm_g, w_branch, b_gate, w_out, ln_g, ln_b)))
    m_loc = dict(zip(WEIGHTS, (m_meta, m_ln_in_g, m_ln_in_b, m_w_in, m_s5_a_re, m_s5_a_im, m_s5_log_step, m_s5_b_re, m_s5_b_im, m_s5_c_re, m_s5_c_im, m_s5_d, m_s5_w_glu, m_s5_b_glu, m_ssd_conv_w, m_ssd_conv_b, m_ssd_dt_bias, m_ssd_a_log, m_ssd_d, m_ssd_norm_g, m_gdn_conv_w, m_gdn_dt_bias, m_gdn_a_log, m_gdn_norm_g, m_w_branch, m_b_gate, m_w_out, m_ln_g, m_ln_b)))
    v_loc = dict(zip(WEIGHTS, (v_meta, v_ln_in_g, v_ln_in_b, v_w_in, v_s5_a_re, v_s5_a_im, v_s5_log_step, v_s5_b_re, v_s5_b_im, v_s5_c_re, v_s5_c_im, v_s5_d, v_s5_w_glu, v_s5_b_glu, v_ssd_conv_w, v_ssd_conv_b, v_ssd_dt_bias, v_ssd_a_log, v_ssd_d, v_ssd_norm_g, v_gdn_conv_w, v_gdn_dt_bias, v_gdn_a_log, v_gdn_norm_g, v_w_branch, v_b_gate, v_w_out, v_ln_g, v_ln_b)))
    return _step(x, loss_target, w_loc, m_loc, v_loc)
```

```python
import functools
import math

import jax
import jax.numpy as jnp
from jax import lax
from jax.experimental import pallas as pl
from jax.experimental.pallas import tpu as pltpu

F32 = jnp.float32
BF16 = jnp.bfloat16

N_DEV = 8
LANES = 128
SUBLANES = 8
VMEM_LIMIT = 56 * 1024 * 1024
FLAT_ROWS = 1024

CHUNK = 64
CONV_K = 4
S5_GROUP = 16
S5_STATE = 64
S5_BLOCK_GROUPS = 8
SSD_HEAD = 64
SSD_HEADS = 12
SSD_GROUPS = 2
SSD_STATE = 128
GDN_HEAD = 128
GDN_HEADS = 6
BRANCH = 768
LN_EPS = 1e-5

ADAM_LR = 0.001
ADAM_B1 = 0.9
ADAM_B2 = 0.999
ADAM_EPS = 1e-08
ADAM_WD = 0.01
ADAM_STEP = 10

WEIGHTS = ['meta', 'ln_in_g', 'ln_in_b', 'w_in', 's5_a_re', 's5_a_im', 's5_log_step', 's5_b_re', 's5_b_im',
           's5_c_re', 's5_c_im', 's5_d', 's5_w_glu', 's5_b_glu', 'ssd_conv_w', 'ssd_conv_b', 'ssd_dt_bias',
           'ssd_a_log', 'ssd_d', 'ssd_norm_g', 'gdn_conv_w', 'gdn_dt_bias', 'gdn_a_log', 'gdn_norm_g',
           'w_branch', 'b_gate', 'w_out', 'ln_g', 'ln_b']
SHARD_DIM = {'meta': 1, 'w_in': 2, 's5_w_glu': 1, 'ssd_conv_w': 2, 'gdn_conv_w': 2, 'w_branch': 3, 'b_gate': 2,
             'w_out': 1}


def _params(sem):
    return pltpu.CompilerParams(dimension_semantics=sem, vmem_limit_bytes=VMEM_LIMIT)


def _row_tile(m, cap):
    best = None
    for t in range(SUBLANES, min(m, cap) + 1, SUBLANES):
        if m % t == 0:
            best = t
    return best if best is not None else m


def _col_tile(n, cap):
    best = None
    for t in range(LANES, min(n, cap) + 1, LANES):
        if n % t == 0:
            best = t
    return best if best is not None else n


def _any_tile(m, cap):
    best = 1
    for t in range(1, min(m, cap) + 1):
        if m % t == 0:
            best = t
    return best


def _mm_fwd(a, b, name):
    m, _ = a.shape
    g, k, n = b.shape
    tm, tn = _row_tile(m, 832), _col_tile(n, 1024)
    nj = n // tn

    def body(a_ref, b_ref, o_ref):
        o_ref[...] = jnp.dot(a_ref[...].astype(BF16), b_ref[0].astype(BF16), preferred_element_type=F32)

    return pl.pallas_call(
        body, name=name, grid=(g, m // tm, nj),
        in_specs=[pl.BlockSpec((tm, k), lambda gi, i, j: (i, gi)),
                  pl.BlockSpec((1, k, tn), lambda gi, i, j: (gi, 0, j))],
        out_specs=pl.BlockSpec((tm, tn), lambda gi, i, j: (i, gi * nj + j)),
        out_shape=jax.ShapeDtypeStruct((m, g * n), F32),
        compiler_params=_params(("arbitrary", "arbitrary", "arbitrary")),
    )(a, b)


def _mm_da(ct, b, name):
    m, _ = ct.shape
    g, k, n = b.shape
    tm, tk = _row_tile(m, 832 if n <= 1536 else 416), _col_tile(k, 1024)
    nk = k // tk

    def body(c_ref, b_ref, o_ref):
        o_ref[...] = lax.dot_general(c_ref[...].astype(BF16), b_ref[0].astype(BF16), (((1,), (1,)), ((), ())),
                                     preferred_element_type=F32)

    return pl.pallas_call(
        body, name=name, grid=(g, m // tm, nk),
        in_specs=[pl.BlockSpec((tm, n), lambda gi, i, j: (i, gi)),
                  pl.BlockSpec((1, tk, n), lambda gi, i, j: (gi, j, 0))],
        out_specs=pl.BlockSpec((tm, tk), lambda gi, i, j: (i, gi * nk + j)),
        out_shape=jax.ShapeDtypeStruct((m, g * k), F32),
        compiler_params=_params(("arbitrary", "arbitrary", "arbitrary")),
    )(ct, b)


def _mm_db(a, ct, g, k, n, name):
    m = a.shape[0]
    tm, tk, tn = _row_tile(m, 832), _col_tile(k, 1024), _col_tile(n, 1280)
    nk, nn = k // tk, n // tn

    def body(a_ref, c_ref, o_ref):
        @pl.when(pl.program_id(3) == 0)
        def _():
            o_ref[...] = jnp.zeros_like(o_ref)

        o_ref[0] += lax.dot_general(a_ref[...].astype(BF16), c_ref[...].astype(BF16), (((0,), (0,)), ((), ())),
                                    preferred_element_type=F32)

    return pl.pallas_call(
        body, name=name, grid=(g, nk, nn, m // tm),
        in_specs=[pl.BlockSpec((tm, tk), lambda gi, i, j, r: (r, gi * nk + i)),
                  pl.BlockSpec((tm, tn), lambda gi, i, j, r: (r, gi * nn + j))],
        out_specs=pl.BlockSpec((1, tk, tn), lambda gi, i, j, r: (gi, i, j)),
        out_shape=jax.ShapeDtypeStruct((g, k, n), F32),
        compiler_params=_params(("arbitrary", "arbitrary", "arbitrary", "arbitrary")),
    )(a, ct)


def _make_gmm(name):
    @jax.custom_vjp
    def gmm(a, b):
        return _mm_fwd(a, b, name + "_fwd")

    def fwd(a, b):
        return _mm_fwd(a, b, name + "_fwd"), (a, b)

    def bwd(res, ct):
        a, b = res
        g, k, n = b.shape
        return _mm_da(ct, b, name + "_da"), _mm_db(a, ct, g, k, n, name + "_db")

    gmm.defvjp(fwd, bwd)
    return gmm


def _mm(name, a, w, carrier):
    @jax.custom_vjp
    def mm(a, w3, carrier3):
        return _mm_fwd(a, w3, name + "_fwd")

    def fwd(a, w3, carrier3):
        return _mm_fwd(a, w3, name + "_fwd"), (a, w3)

    def bwd(res, ct):
        a, w3 = res
        g, k, n = w3.shape
        return _mm_da(ct, w3, name + "_da"), jnp.zeros_like(w3), _mm_db(a, ct, g, k, n, name + "_db")

    mm.defvjp(fwd, bwd)
    return mm(a, w[None], carrier[None])


@jax.custom_vjp
def _wdot(x, w, carrier):
    return _dot(x, w)


def _wdot_fwd(x, w, carrier):
    return _dot(x, w), (x, w)


def _wdot_bwd(res, ct):
    x, w = res
    return _dot_nt(ct, w), jnp.zeros_like(w), _dot_tn(x, ct)


_wdot.defvjp(_wdot_fwd, _wdot_bwd)


def _make_rowwise(name, fn, n_row, n_par, out_widths, tm_cap, use_ridx=False, n_wt=0):
    n_out = len(out_widths)
    n_in = n_row + n_par + n_wt

    def bind(tm):
        if not use_ridx:
            return fn
        ridx = pl.program_id(0) * tm + lax.broadcasted_iota(jnp.int32, (tm, 1), 0)
        return functools.partial(fn, ridx)

    def specs(args, tm):
        rows = [pl.BlockSpec((tm, a.shape[1]), lambda i: (i, 0)) for a in args[:n_row]]
        pars = [pl.BlockSpec(a.shape, lambda i: (0, 0)) for a in args[n_row:n_in]]
        return rows, pars

    def fwd_call(*args):
        t = args[0].shape[0]
        tm = _row_tile(t, tm_cap)
        rows, pars = specs(args, tm)

        def body(*refs):
            vals = [r[...] for r in refs[:n_row + n_par]]
            mats = [functools.partial(lambda x, w: _dot(x, w), w=r[...]) for r in refs[n_row + n_par:n_in]]
            res = bind(tm)(*vals, *mats)
            for o_ref, r in zip(refs[n_in:], res):
                o_ref[...] = r

        outs = pl.pallas_call(
            body, name=name + "_fwd", grid=(t // tm,), in_specs=rows + pars,
            out_specs=[pl.BlockSpec((tm, w), lambda i: (i, 0)) for w in out_widths],
            out_shape=[jax.ShapeDtypeStruct((t, w), F32) for w in out_widths],
            compiler_params=_params(("arbitrary",)),
        )(*args)
        return tuple(outs)

    def bwd_call(args, cts):
        t = args[0].shape[0]
        tm = _row_tile(t, tm_cap)
        rows, pars = specs(args, tm)
        ct_specs = [pl.BlockSpec((tm, w), lambda i: (i, 0)) for w in out_widths]
        n_diff = n_row + n_par

        def body(*refs):
            vals = [r[...] for r in refs[:n_diff]]
            wts = [r[...] for r in refs[n_diff:n_in]]
            ct_vals = tuple(r[...] for r in refs[n_in:n_in + n_out])
            d_refs = refs[n_in + n_out:]
            f = bind(tm)

            def g(*a):
                mats = [functools.partial(lambda x, w, c: _wdot(x, w, c), w=w, c=c)
                        for w, c in zip(wts, a[n_diff:])]
                return tuple(f(*a[:n_diff], *mats))

            _, vjp = jax.vjp(g, *vals, *[jnp.zeros(w.shape, F32) for w in wts])
            grads = vjp(ct_vals)
            for i in range(n_row):
                d_refs[i][...] = grads[i]
            if n_par + n_wt:
                @pl.when(pl.program_id(0) == 0)
                def _():
                    for r in d_refs[n_row:]:
                        r[...] = jnp.zeros_like(r)

                for r, gr in zip(d_refs[n_row:], grads[n_row:]):
                    r[...] += gr

        outs = pl.pallas_call(
            body, name=name + "_bwd", grid=(t // tm,), in_specs=rows + pars + ct_specs,
            out_specs=rows + pars,
            out_shape=[jax.ShapeDtypeStruct(a.shape, F32) for a in args],
            compiler_params=_params(("arbitrary",)),
        )(*args, *cts)
        return tuple(outs)

    @jax.custom_vjp
    def op(*args):
        return fwd_call(*args[:n_in])

    def fwd(*args):
        return fwd_call(*args[:n_in]), args[:n_in]

    def bwd(args, cts):
        grads = bwd_call(args, cts)
        return grads[:n_row + n_par] + tuple(jnp.zeros_like(a) for a in args[n_row + n_par:]) + grads[n_row + n_par:]

    op.defvjp(fwd, bwd)
    return op


HALO = SUBLANES


def _make_conv_rowwise(name, fn, n_par, out_width, tm_cap, use_ridx=False):
    def bind(ridx):
        return functools.partial(fn, ridx) if use_ridx else fn

    def stage(x_ref, halo_ref, xs, first):
        xs[0:HALO, :] = jnp.where(first, 0.0, halo_ref[...])
        xs[HALO:, :] = x_ref[...]

    def taps(xs, tm):
        return [xs[pl.ds(HALO - (CONV_K - 1) + j, tm), :] for j in range(CONV_K)]

    def fwd_call(x, *pars):
        t, wd = x.shape
        tm = _row_tile(t, tm_cap)
        per = tm // HALO

        def body(*refs):
            x_ref, halo_ref = refs[:2]
            par_refs, o_ref, xs = refs[2:2 + n_par], refs[2 + n_par], refs[-1]
            i = pl.program_id(0)
            stage(x_ref, halo_ref, xs, i == 0)
            ridx = i * tm + lax.broadcasted_iota(jnp.int32, (tm, 1), 0)
            (o_ref[...],) = bind(ridx)(*taps(xs, tm), *[r[...] for r in par_refs])

        return pl.pallas_call(
            body, name=name + "_fwd", grid=(t // tm,),
            in_specs=[pl.BlockSpec((tm, wd), lambda i: (i, 0)),
                      pl.BlockSpec((HALO, wd), lambda i: (jnp.maximum(i * per - 1, 0), 0))]
            + [pl.BlockSpec(p.shape, lambda i: (0, 0)) for p in pars],
            out_specs=pl.BlockSpec((tm, out_width), lambda i: (i, 0)),
            out_shape=jax.ShapeDtypeStruct((t, out_width), F32),
            scratch_shapes=[pltpu.VMEM((tm + HALO, wd), F32)],
            compiler_params=_params(("arbitrary",)),
        )(x, x, *pars)

    def bwd_call(x, pars, ct):
        t, wd = x.shape
        tm = _row_tile(t, tm_cap)
        per = tm // HALO
        nb = t // tm

        def body(*refs):
            x_ref, halo_ref = refs[:2]
            par_refs, ct_ref = refs[2:2 + n_par], refs[2 + n_par]
            dx_ref, dpar_refs = refs[3 + n_par], refs[4 + n_par:4 + 2 * n_par]
            xs, ds, carry = refs[-3:]
            step = pl.program_id(0)
            blk = nb - 1 - step

            @pl.when(step == 0)
            def _():
                carry[...] = jnp.zeros_like(carry)
                for r in dpar_refs:
                    r[...] = jnp.zeros_like(r)

            stage(x_ref, halo_ref, xs, blk == 0)
            ridx = blk * tm + lax.broadcasted_iota(jnp.int32, (tm, 1), 0)
            f = bind(ridx)
            _, vjp = jax.vjp(lambda *a: tuple(f(*a)), *taps(xs, tm), *[r[...] for r in par_refs])
            grads = vjp((ct_ref[...],))
            ds[...] = jnp.zeros_like(ds)
            for j in range(CONV_K):
                ds[pl.ds(HALO - (CONV_K - 1) + j, tm), :] += grads[j]
            ds[pl.ds(tm, HALO), :] += carry[...]
            dx_ref[...] = ds[HALO:, :]
            carry[...] = ds[0:HALO, :]
            for r, g in zip(dpar_refs, grads[CONV_K:]):
                r[...] += g

        rev = lambda i: (nb - 1 - i, 0)
        outs = pl.pallas_call(
            body, name=name + "_bwd", grid=(nb,),
            in_specs=[pl.BlockSpec((tm, wd), rev),
                      pl.BlockSpec((HALO, wd), lambda i: (jnp.maximum((nb - 1 - i) * per - 1, 0), 0))]
            + [pl.BlockSpec(p.shape, lambda i: (0, 0)) for p in pars]
            + [pl.BlockSpec((tm, out_width), rev)],
            out_specs=[pl.BlockSpec((tm, wd), rev)] + [pl.BlockSpec(p.shape, lambda i: (0, 0)) for p in pars],
            out_shape=[jax.ShapeDtypeStruct(x.shape, F32)] + [jax.ShapeDtypeStruct(p.shape, F32) for p in pars],
            scratch_shapes=[pltpu.VMEM((tm + HALO, wd), F32), pltpu.VMEM((tm + HALO, wd), F32),
                            pltpu.VMEM((HALO, wd), F32)],
            compiler_params=_params(("arbitrary",)),
        )(x, x, *pars, ct)
        return tuple(outs)

    @jax.custom_vjp
    def op(x, *pars):
        return fwd_call(x, *pars)

    def fwd(x, *pars):
        return fwd_call(x, *pars), (x, pars)

    def bwd(res, ct):
        x, pars = res
        return bwd_call(x, pars, ct)

    op.defvjp(fwd, bwd)
    return op


def _make_chunk_scan(name, fn, state_shape, n_seq, n_par, out_widths):
    n_out = len(out_widths)
    zeros_idx = (0,) * len(state_shape)

    def fwd_call(*args):
        t = args[0].shape[0]
        nc = t // CHUNK
        seq_specs = [pl.BlockSpec((CHUNK, a.shape[1]), lambda c: (c, 0)) for a in args[:n_seq]]
        par_specs = [pl.BlockSpec(a.shape, lambda c: (0, 0)) for a in args[n_seq:]]

        def body(*refs):
            ins = refs[:n_seq + n_par]
            out_refs = refs[n_seq + n_par:n_seq + n_par + n_out]
            states_ref = refs[n_seq + n_par + n_out]
            st = refs[-1]

            @pl.when(pl.program_id(0) == 0)
            def _():
                st[...] = jnp.zeros_like(st)

            s0 = st[...]
            states_ref[0] = s0
            res = fn(s0, *[r[...] for r in ins])
            st[...] = res[0]
            for o_ref, r in zip(out_refs, res[1:]):
                o_ref[...] = r

        outs = pl.pallas_call(
            body, name=name + "_fwd", grid=(nc,), in_specs=seq_specs + par_specs,
            out_specs=[pl.BlockSpec((CHUNK, w), lambda c: (c, 0)) for w in out_widths]
            + [pl.BlockSpec((1,) + state_shape, lambda c: (c,) + zeros_idx)],
            out_shape=[jax.ShapeDtypeStruct((t, w), F32) for w in out_widths]
            + [jax.ShapeDtypeStruct((nc,) + state_shape, F32)],
            scratch_shapes=[pltpu.VMEM(state_shape, F32)],
            compiler_params=_params(("arbitrary",)),
        )(*args)
        return tuple(outs[:n_out]), outs[n_out]

    def bwd_call(args, states, cts):
        t = args[0].shape[0]
        nc = t // CHUNK
        rev = lambda c: (nc - 1 - c, 0)
        seq_specs = [pl.BlockSpec((CHUNK, a.shape[1]), rev) for a in args[:n_seq]]
        par_specs = [pl.BlockSpec(a.shape, lambda c: (0, 0)) for a in args[n_seq:]]
        ct_specs = [pl.BlockSpec((CHUNK, w), rev) for w in out_widths]
        st_spec = pl.BlockSpec((1,) + state_shape, lambda c: (nc - 1 - c,) + zeros_idx)
        n_in = n_seq + n_par

        def body(*refs):
            vals = [r[...] for r in refs[:n_in]]
            s0 = refs[n_in][0]
            ct_vals = tuple(r[...] for r in refs[n_in + 1:n_in + 1 + n_out])
            d_refs = refs[n_in + 1 + n_out:-1]
            dst = refs[-1]

            @pl.when(pl.program_id(0) == 0)
            def _():
                dst[...] = jnp.zeros_like(dst)
                for j in range(n_par):
                    d_refs[n_seq + j][...] = jnp.zeros_like(d_refs[n_seq + j])

            _, vjp = jax.vjp(lambda *a: tuple(fn(*a)), s0, *vals)
            grads = vjp((dst[...],) + ct_vals)
            dst[...] = grads[0]
            for i in range(n_seq):
                d_refs[i][...] = grads[1 + i]
            for j in range(n_par):
                d_refs[n_seq + j][...] += grads[1 + n_seq + j]

        outs = pl.pallas_call(
            body, name=name + "_bwd", grid=(nc,), in_specs=seq_specs + par_specs + [st_spec] + ct_specs,
            out_specs=seq_specs + par_specs,
            out_shape=[jax.ShapeDtypeStruct(a.shape, F32) for a in args],
            scratch_shapes=[pltpu.VMEM(state_shape, F32)],
            compiler_params=_params(("arbitrary",)),
        )(*args, states, *cts)
        return tuple(outs)

    @jax.custom_vjp
    def op(*args):
        return fwd_call(*args)[0]

    def fwd(*args):
        outs, states = fwd_call(*args)
        return outs, (args, states)

    def bwd(res, cts):
        args, states = res
        return bwd_call(args, states, cts)

    op.defvjp(fwd, bwd)
    return op


def _s5_scan_fwd(bre, bim, are, aim, name):
    t, r, _ = bre.shape
    tb = _any_tile(t, 208)
    blk = pl.BlockSpec((tb, r, LANES), lambda i: (i, 0, 0))
    par = pl.BlockSpec((r, LANES), lambda i: (0, 0))

    def body(bre_ref, bim_ref, are_ref, aim_ref, sre_ref, sim_ref, st):
        @pl.when(pl.program_id(0) == 0)
        def _():
            st[...] = jnp.zeros_like(st)

        ar, ai = are_ref[...], aim_ref[...]

        def step(k, carry):
            sr, si = carry
            nr = ar * sr - ai * si + bre_ref[k]
            ni = ar * si + ai * sr + bim_ref[k]
            sre_ref[k] = nr
            sim_ref[k] = ni
            return nr, ni

        sr, si = lax.fori_loop(0, tb, step, (st[0], st[1]), unroll=4)
        st[0] = sr
        st[1] = si

    return pl.pallas_call(
        body, name=name, grid=(t // tb,), in_specs=[blk, blk, par, par], out_specs=[blk, blk],
        out_shape=[jax.ShapeDtypeStruct(bre.shape, F32)] * 2,
        scratch_shapes=[pltpu.VMEM((2, r, LANES), F32)],
        compiler_params=_params(("arbitrary",)),
    )(bre, bim, are, aim)


def _s5_scan_bwd(dsr, dsi, sre, sim, are, aim, name):
    t, r, _ = sre.shape
    tb = _any_tile(t, 208)
    nb = t // tb
    blk = pl.BlockSpec((tb, r, LANES), lambda i: (nb - 1 - i, 0, 0))
    par = pl.BlockSpec((r, LANES), lambda i: (0, 0))

    def body(dsr_ref, dsi_ref, sre_ref, sim_ref, are_ref, aim_ref, gre_ref, gim_ref, dar_ref, dai_ref, st):
        @pl.when(pl.program_id(0) == 0)
        def _():
            st[...] = jnp.zeros_like(st)
            dar_ref[...] = jnp.zeros_like(dar_ref)
            dai_ref[...] = jnp.zeros_like(dai_ref)

        ar, ai = are_ref[...], aim_ref[...]

        def step(k, carry):
            gr, gi, dar, dai = carry
            q = tb - 1 - k
            s_r, s_i = sre_ref[q], sim_ref[q]
            dar = dar + gr * s_r + gi * s_i
            dai = dai + gi * s_r - gr * s_i
            ngr = dsr_ref[q] + ar * gr + ai * gi
            ngi = dsi_ref[q] + ar * gi - ai * gr
            gre_ref[q] = ngr
            gim_ref[q] = ngi
            return ngr, ngi, dar, dai

        gr, gi, dar, dai = lax.fori_loop(0, tb, step, (st[0], st[1], dar_ref[...], dai_ref[...]), unroll=4)
        st[0] = gr
        st[1] = gi
        dar_ref[...] = dar
        dai_ref[...] = dai

    return pl.pallas_call(
        body, name=name, grid=(nb,), in_specs=[blk, blk, blk, blk, par, par], out_specs=[blk, blk, par, par],
        out_shape=[jax.ShapeDtypeStruct(sre.shape, F32)] * 2 + [jax.ShapeDtypeStruct(are.shape, F32)] * 2,
        scratch_shapes=[pltpu.VMEM((2, r, LANES), F32)],
        compiler_params=_params(("arbitrary",)),
    )(dsr, dsi, sre, sim, are, aim)


def _make_s5_scan(name):
    @jax.custom_vjp
    def scan(bre, bim, are, aim):
        return tuple(_s5_scan_fwd(bre, bim, are, aim, name + "_fwd"))

    def fwd(bre, bim, are, aim):
        sre, sim = _s5_scan_fwd(bre, bim, are, aim, name + "_fwd")
        return (sre, sim), (sre, sim, are, aim)

    def bwd(res, cts):
        sre, sim, are, aim = res
        return tuple(_s5_scan_bwd(cts[0], cts[1], sre, sim, are, aim, name + "_bwd"))

    scan.defvjp(fwd, bwd)
    return scan


def _dot(a, b):
    return jnp.dot(a.astype(BF16), b.astype(BF16), preferred_element_type=F32)


def _dot_nt(a, b):
    return lax.dot_general(a.astype(BF16), b.astype(BF16), (((1,), (1,)), ((), ())), preferred_element_type=F32)


def _dot_tn(a, b):
    return lax.dot_general(a.astype(BF16), b.astype(BF16), (((0,), (0,)), ((), ())), preferred_element_type=F32)


def _dot_f32(a, b):
    return jnp.dot(a, b, precision=lax.Precision.HIGHEST, preferred_element_type=F32)


def _dot_tn_f32(a, b):
    return lax.dot_general(a, b, (((0,), (0,)), ((), ())), precision=lax.Precision.HIGHEST,
                           preferred_element_type=F32)


def _iota(shape, dim):
    return lax.broadcasted_iota(jnp.int32, shape, dim)


def _tri(strict=False):
    r, c = _iota((CHUNK, CHUNK), 0), _iota((CHUNK, CHUNK), 1)
    return (r > c) if strict else (r >= c)


def _silu(x):
    return x * jax.nn.sigmoid(x)


def _layer_norm(z, g, b):
    mu = jnp.mean(z, axis=-1, keepdims=True)
    var = jnp.mean(jnp.square(z - mu), axis=-1, keepdims=True)
    return (z - mu) * lax.rsqrt(var + LN_EPS) * g + b


def _ssd_chunk(state, xbc, sm, a_c, d_exp):
    width = SSD_HEADS * SSD_HEAD
    x = xbc[:, :width]
    lane = _iota((CHUNK, LANES), 1)
    dtc = jnp.where(lane < SSD_HEADS, sm, 0.0)
    low = _tri().astype(F32)
    eye = (_iota((CHUNK, CHUNK), 0) == _iota((CHUNK, CHUNK), 1)).astype(F32)
    head_col, head_row = _iota((LANES, width), 1), _iota((LANES, width), 0) * SSD_HEAD
    expand = ((head_col >= head_row) & (head_col < head_row + SSD_HEAD)).astype(F32)
    acum_c = _dot_f32(low, dtc * a_c)
    acum_ct = _dot_tn_f32(acum_c, eye)
    dt_exp = _dot_f32(dtc, expand)
    acum = _dot_f32(acum_c, expand)
    xd = x * dt_exp
    last = acum[CHUNK - 1:CHUNK, :]
    to_end = jnp.exp(last - acum)
    eac = jnp.exp(acum)
    causal = _tri()
    first_half = _iota((CHUNK, LANES), 1) < SSD_HEAD
    top_rows = _iota((LANES, LANES), 0) < SSD_HEAD
    pairs = range(SSD_HEADS // 2)
    grp = [(2 * p) // (SSD_HEADS // SSD_GROUPS) for p in pairs]
    cols = [slice(p * LANES, (p + 1) * LANES) for p in pairs]
    bg = [xbc[:, width + g * SSD_STATE: width + (g + 1) * SSD_STATE] for g in range(SSD_GROUPS)]
    cg = [xbc[:, width + (SSD_GROUPS + g) * SSD_STATE: width + (SSD_GROUPS + g + 1) * SSD_STATE]
          for g in range(SSD_GROUPS)]
    scores = [_dot_nt(cg[g], bg[g]) for g in range(SSD_GROUPS)]
    dec = [jnp.where(causal, jnp.exp(jnp.minimum(acum_c[:, h:h + 1] - acum_ct[h:h + 1, :], 0.0)), 0.0)
           for h in range(SSD_HEADS)]
    y_lo = [_dot(scores[grp[p]] * dec[2 * p], jnp.where(first_half, xd[:, cols[p]], 0.0)) for p in pairs]
    y_hi = [_dot(scores[grp[p]] * dec[2 * p + 1], jnp.where(first_half, 0.0, xd[:, cols[p]])) for p in pairs]
    s_prev = [state[p * LANES:(p + 1) * LANES, :] for p in pairs]
    y_off = [_dot_nt(cg[grp[p]], s_prev[p]) for p in pairs]
    s_add = [_dot_tn(xd[:, cols[p]] * to_end[:, cols[p]], bg[grp[p]]) for p in pairs]
    ys = [y_lo[p] + y_hi[p] + y_off[p] * eac[:, cols[p]] + x[:, cols[p]] * d_exp[:, cols[p]] for p in pairs]
    cd = [jnp.where(top_rows, jnp.exp(acum_c[CHUNK - 1:CHUNK, 2 * p:2 * p + 1]),
                    jnp.exp(acum_c[CHUNK - 1:CHUNK, 2 * p + 1:2 * p + 2])) for p in pairs]
    new_states = [s_prev[p] * cd[p] + s_add[p] for p in pairs]
    return jnp.concatenate(new_states, axis=0), jnp.concatenate(ys, axis=1)


def _gdn_chunk(state, qkv, sm):
    width = GDN_HEADS * GDN_HEAD
    g0, b0 = SSD_HEADS, SSD_HEADS + GDN_HEADS
    lane = _iota((CHUNK, LANES), 1)
    gc = jnp.where((lane >= g0) & (lane < b0), sm, 0.0)
    low = _tri().astype(F32)
    eye = (_iota((CHUNK, CHUNK), 0) == _iota((CHUNK, CHUNK), 1)).astype(F32)
    gcum = _dot_f32(low, gc)
    gcum_t = _dot_tn_f32(gcum, eye)
    causal, strict = _tri(), _tri(strict=True)
    heads = range(GDN_HEADS)
    q = [qkv[:, h * GDN_HEAD:(h + 1) * GDN_HEAD] for h in heads]
    k = [qkv[:, width + h * GDN_HEAD: width + (h + 1) * GDN_HEAD] for h in heads]
    v = [qkv[:, 2 * width + h * GDN_HEAD: 2 * width + (h + 1) * GDN_HEAD] for h in heads]
    beta = [sm[:, b0 + h:b0 + h + 1] for h in heads]
    gcol = [gcum[:, g0 + h:g0 + h + 1] for h in heads]
    glast = [gcum[CHUNK - 1:CHUNK, g0 + h:g0 + h + 1] for h in heads]
    gamma = [jnp.where(causal, jnp.exp(jnp.minimum(gcol[h] - gcum_t[g0 + h:g0 + h + 1, :], 0.0)), 0.0) for h in heads]
    kk = [_dot_nt(k[h], k[h]) for h in heads]
    qk = [_dot_nt(q[h], k[h]) for h in heads]
    egc = [jnp.exp(gcol[h]) for h in heads]
    nmat = [-jnp.where(strict, kk[h] * gamma[h] * beta[h], 0.0) for h in heads]
    sol = [jnp.concatenate([v[h] * beta[h], k[h] * (beta[h] * egc[h])], axis=1) for h in heads]
    for i in range(6):
        upd = [_dot(nmat[h], sol[h]) for h in heads]
        if i < 5:
            nmat = [_dot(nmat[h], nmat[h]) for h in heads]
        sol = [sol[h] + upd[h] for h in heads]
    s_prev = [state[h * GDN_HEAD:(h + 1) * GDN_HEAD, :] for h in heads]
    w_s = [_dot(sol[h][:, GDN_HEAD:], s_prev[h]) for h in heads]
    q_s = [_dot(q[h] * egc[h], s_prev[h]) for h in heads]
    v_new = [sol[h][:, :GDN_HEAD] - w_s[h] for h in heads]
    a_v = [_dot(qk[h] * gamma[h], v_new[h]) for h in heads]
    k_v = [_dot_tn(k[h] * jnp.exp(glast[h] - gcol[h]), v_new[h]) for h in heads]
    outs = [q_s[h] + a_v[h] for h in heads]
    new_states = [s_prev[h] * jnp.exp(glast[h]) + k_v[h] for h in heads]
    return jnp.concatenate(new_states, axis=0), jnp.concatenate(outs, axis=1)


def _row_fns(d_model, pad_rows, loss_rows, alpha):
    def keep(ridx, v):
        return jnp.where(ridx >= pad_rows, v, 0.0)

    def ln_in(ridx, h, g, b):
        return (keep(ridx, _layer_norm(h, g, b)),)

    def s5_tail(y_re, y_im, u, z, d, bias, glu, branch):
        v0 = jax.nn.gelu(y_re + y_im + d * u)
        return (branch(v0 * jax.nn.sigmoid(glu(v0) + bias) * _silu(z)),)

    def small_act(ridx, raw, bias, scale):
        lane = _iota(raw.shape, 1)
        sp = jax.nn.softplus(raw + bias)
        g0, b0 = SSD_HEADS, SSD_HEADS + GDN_HEADS
        out = jnp.where(lane < g0, sp, jnp.where(lane < b0, scale * sp,
                                                 jnp.where(lane < b0 + GDN_HEADS, jax.nn.sigmoid(raw), 0.0)))
        return (keep(ridx, out),)

    def conv(xs, w):
        acc = xs[0] * w[0:1, :]
        for j in range(1, CONV_K):
            acc = acc + xs[j] * w[j:j + 1, :]
        return acc

    def ssd_conv(ridx, x0, x1, x2, x3, w, b):
        return (keep(ridx, _silu(conv((x0, x1, x2, x3), w) + b)),)

    def ssd_tail(y, z, g, branch):
        v = y * _silu(z)
        return (branch(v * lax.rsqrt(jnp.mean(v * v, axis=-1, keepdims=True) + LN_EPS) * g),)

    def gdn_conv(x0, x1, x2, x3, w):
        a = _silu(conv((x0, x1, x2, x3), w))
        width = GDN_HEADS * GDN_HEAD
        parts = []
        for h in range(2 * GDN_HEADS):
            z = a[:, h * GDN_HEAD:(h + 1) * GDN_HEAD]
            z = z * lax.rsqrt(jnp.sum(z * z, axis=-1, keepdims=True) + 1e-6)
            parts.append(z * GDN_HEAD ** -0.5 if h < GDN_HEADS else z)
        parts.append(a[:, 2 * width:])
        return (jnp.concatenate(parts, axis=1),)

    def gdn_tail(o, z, g, branch):
        parts = []
        for h in range(GDN_HEADS):
            cols = slice(h * GDN_HEAD, (h + 1) * GDN_HEAD)
            oh = o[:, cols]
            oh = oh * lax.rsqrt(jnp.mean(oh * oh, axis=-1, keepdims=True) + LN_EPS) * g
            parts.append(oh * _silu(z[:, cols]))
        return (branch(jnp.concatenate(parts, axis=1)),)

    def merge_out(ridx, oa, ob, oc, gate, h, bias, g, b, w_out):
        acc = None
        for k, o in enumerate((oa, ob, oc)):
            cols = slice(k * d_model, (k + 1) * d_model)
            term = jax.nn.sigmoid(gate[:, cols] + bias[:, cols]) * o
            acc = term if acc is None else acc + term
        return (keep(ridx, _layer_norm(alpha * h + w_out(acc), g, b)),)

    def loss_rows_fn(ridx, h, tgt):
        row = 0.5 * jnp.mean(jnp.square(h - tgt), axis=-1, keepdims=True)
        row = jnp.where(ridx >= loss_rows, row, 0.0)
        lane = _iota((h.shape[0], LANES), 1)
        return (jnp.where(lane == 0, row, 0.0),)

    return dict(ln_in=ln_in, s5_tail=s5_tail, small_act=small_act, ssd_conv=ssd_conv, ssd_tail=ssd_tail,
                gdn_conv=gdn_conv, gdn_tail=gdn_tail, merge_out=merge_out, loss=loss_rows_fn)


def _pair_exchange(srcs, whole_srcs, name):
    n_arr, n_whole = len(srcs), len(whole_srcs)
    n_chip = N_DEV // 2
    base = n_arr * n_chip

    def body(*refs):
        src_refs, recv_refs = refs[:n_arr + n_whole], refs[n_arr + n_whole:2 * (n_arr + n_whole)]
        send_sems, recv_sems = refs[2 * (n_arr + n_whole):]
        x, y, c = lax.axis_index("x"), lax.axis_index("y"), lax.axis_index("c")

        def to_sibling(src, dst, k):
            return pltpu.make_async_remote_copy(src_ref=src, dst_ref=dst, send_sem=send_sems.at[k],
                                                recv_sem=recv_sems.at[k], device_id=(x, y, 1 - c),
                                                device_id_type=pl.DeviceIdType.MESH)

        copies = [to_sibling(src_refs[a].at[q, 1 - c], recv_refs[a].at[q], a * n_chip + q)
                  for a in range(n_arr) for q in range(n_chip)]
        copies += [to_sibling(src_refs[n_arr + b], recv_refs[n_arr + b], base + b) for b in range(n_whole)]
        for cp in copies:
            cp.start()
        for cp in copies:
            cp.wait()

    n_sem = base + n_whole
    return pl.pallas_call(
        body, name=name,
        in_specs=[pl.BlockSpec(memory_space=pl.ANY)] * (n_arr + n_whole),
        out_specs=[pl.BlockSpec(memory_space=pl.ANY)] * (n_arr + n_whole),
        out_shape=[jax.ShapeDtypeStruct((n_chip,) + tuple(s.shape[2:]), s.dtype) for s in srcs]
        + [jax.ShapeDtypeStruct(s.shape, s.dtype) for s in whole_srcs],
        scratch_shapes=[pltpu.SemaphoreType.DMA((n_sem,)), pltpu.SemaphoreType.DMA((n_sem,))],
    )(*srcs, *whole_srcs)


def _pair_sum(src, recv, core, name):
    n_chip, rows, cols = recv.shape
    tr = _row_tile(rows, 1024 if cols <= LANES else 512)

    def body(core_ref, a_ref, b_ref, o_ref):
        o_ref[...] = (a_ref[0].astype(F32) + b_ref[...].astype(F32)).astype(o_ref.dtype)

    blk = pl.BlockSpec((1, tr, cols), lambda q, i, core_ref: (q, i, 0))
    return pl.pallas_call(
        body, name=name,
        grid_spec=pltpu.PrefetchScalarGridSpec(
            num_scalar_prefetch=1, grid=(n_chip, rows // tr),
            in_specs=[pl.BlockSpec((1, 1, tr, cols), lambda q, i, core_ref: (q, core_ref[0], i, 0)), blk],
            out_specs=blk),
        out_shape=jax.ShapeDtypeStruct(recv.shape, recv.dtype),
        compiler_params=_params(("arbitrary", "arbitrary")),
    )(core, src, recv)


def _add2(a, b, name):
    rows = a.shape[0]
    tr = _row_tile(rows, FLAT_ROWS)
    blk = pl.BlockSpec((tr, LANES), lambda i: (i, 0))

    def body(a_ref, b_ref, o_ref):
        o_ref[...] = a_ref[...] + b_ref[...]

    return pl.pallas_call(body, name=name, grid=(rows // tr,), in_specs=[blk, blk], out_specs=blk,
                          out_shape=jax.ShapeDtypeStruct(a.shape, F32), compiler_params=_params(("arbitrary",)))(a, b)


def _add_parts(parts, name):
    n, rows, _ = parts.shape
    tr = _row_tile(rows, FLAT_ROWS)

    def body(p_ref, o_ref):
        acc = p_ref[0]
        for k in range(1, n):
            acc = acc + p_ref[k]
        o_ref[...] = acc

    return pl.pallas_call(body, name=name, grid=(rows // tr,),
                          in_specs=[pl.BlockSpec((n, tr, LANES), lambda i: (0, i, 0))],
                          out_specs=pl.BlockSpec((tr, LANES), lambda i: (i, 0)),
                          out_shape=jax.ShapeDtypeStruct((rows, LANES), F32),
                          compiler_params=_params(("arbitrary",)))(parts)


def _chip_exchange(chip_srcs, whole_srcs, name):
    n_chip_arr, n_whole = len(chip_srcs), len(whole_srcs)
    n_arr = n_chip_arr + n_whole
    n_chip = N_DEV // 2
    chip_flips = [(1, 0), (0, 1), (1, 1)]

    def body(*refs):
        src_refs, out_refs = refs[:n_arr], refs[n_arr:2 * n_arr]
        send_sems, recv_sems = refs[2 * n_arr:]
        x, y, c = lax.axis_index("x"), lax.axis_index("y"), lax.axis_index("c")
        my_chip = 2 * x + y
        copies = []
        for a in range(n_arr):
            for k, (fx, fy) in enumerate(chip_flips):
                px = 1 - x if fx else x
                py = 1 - y if fy else y
                src = src_refs[a].at[2 * px + py] if a < n_chip_arr else src_refs[a]
                copies.append(pltpu.make_async_remote_copy(
                    src_ref=src, dst_ref=out_refs[a].at[my_chip],
                    send_sem=send_sems.at[a * 3 + k], recv_sem=recv_sems.at[a * 3 + k],
                    device_id=(px, py, c), device_id_type=pl.DeviceIdType.MESH))
        for cp in copies:
            cp.start()
        for cp in copies:
            cp.wait()

    n_sem = n_arr * len(chip_flips)
    return pl.pallas_call(
        body, name=name,
        in_specs=[pl.BlockSpec(memory_space=pl.ANY)] * n_arr, out_specs=[pl.BlockSpec(memory_space=pl.ANY)] * n_arr,
        out_shape=[jax.ShapeDtypeStruct(s.shape, s.dtype) for s in chip_srcs]
        + [jax.ShapeDtypeStruct((n_chip,) + tuple(s.shape), s.dtype) for s in whole_srcs],
        scratch_shapes=[pltpu.SemaphoreType.DMA((n_sem,)), pltpu.SemaphoreType.DMA((n_sem,))],
    )(*chip_srcs, *whole_srcs)


def _gather(srcs, name):
    n_arr = len(srcs)
    n_sem = N_DEV - 1

    def body(*refs):
        src_refs, out_refs = refs[:n_arr], refs[n_arr:2 * n_arr]
        send_sems, recv_sems = refs[2 * n_arr:]
        x, y, c = lax.axis_index("x"), lax.axis_index("y"), lax.axis_index("c")
        me, sibling = (x, y, c), (x, y, 1 - c)
        chips = [(1 - x, y), (x, 1 - y), (1 - x, 1 - y)]

        def slot(a, dev):
            return out_refs[a].at[4 * dev[0] + 2 * dev[1] + dev[2]]

        def copy(a, k, block, to, own=False):
            return pltpu.make_async_remote_copy(
                src_ref=src_refs[a] if own else slot(a, block), dst_ref=slot(a, block),
                send_sem=send_sems.at[a * n_sem + k], recv_sem=recv_sems.at[a * n_sem + k],
                device_id=to, device_id_type=pl.DeviceIdType.MESH)

        arrays = range(n_arr)
        first = [copy(a, 0, me, sibling, own=True) for a in arrays]
        first += [copy(a, 1 + j, me, (*chip, c), own=True) for j, chip in enumerate(chips) for a in arrays]
        for cp in first:
            cp.start()
        passed = []
        for j, chip in enumerate(chips):
            for a in arrays:
                copy(a, 1 + j, (*chip, c), me).wait_recv()
                fwd = copy(a, 4 + j, (*chip, c), sibling)
                fwd.start()
                passed.append(fwd)
        for a in arrays:
            copy(a, 0, sibling, me).wait_recv()
        for j, chip in enumerate(chips):
            for a in arrays:
                copy(a, 4 + j, (*chip, 1 - c), me).wait_recv()
        for cp in first + passed:
            cp.wait_send()

    return pl.pallas_call(
        body, name=name,
        in_specs=[pl.BlockSpec(memory_space=pl.ANY)] * n_arr, out_specs=[pl.BlockSpec(memory_space=pl.ANY)] * n_arr,
        out_shape=[jax.ShapeDtypeStruct((N_DEV,) + tuple(s.shape), s.dtype) for s in srcs],
        scratch_shapes=[pltpu.SemaphoreType.DMA((n_arr * n_sem,)), pltpu.SemaphoreType.DMA((n_arr * n_sem,))],
    )(*srcs)


def _adamw_body(p_ref, w_ref, m_ref, v_ref, g_ref, d_ref, nm_ref, nv_ref):
    bc1 = 1.0 - ADAM_B1 ** ADAM_STEP
    bc2 = 1.0 - ADAM_B2 ** ADAM_STEP
    g = p_ref[0].astype(F32)
    for k in range(1, p_ref.shape[0]):
        g = g + p_ref[k].astype(F32)
    nm = ADAM_B1 * m_ref[...] + (1.0 - ADAM_B1) * g
    nv = ADAM_B2 * v_ref[...] + (1.0 - ADAM_B2) * jnp.square(g)
    m_hat = nm / bc1
    v_hat = nv / bc2
    g_ref[...] = g
    d_ref[...] = -ADAM_LR * (m_hat / (jnp.sqrt(v_hat) + ADAM_EPS) + ADAM_WD * w_ref[...])
    nm_ref[...] = nm
    nv_ref[...] = nv


def _adamw_rows(parts, w, m, v, name):
    rows, cols = w.shape
    tr = _row_tile(rows, max(128, (1 << 18) // cols))
    blk = pl.BlockSpec((tr, cols), lambda i: (i, 0))
    return pl.pallas_call(
        functools.partial(_adamw_body), name=name, grid=(rows // tr,),
        in_specs=[pl.BlockSpec((parts.shape[0], tr, cols), lambda i: (0, i, 0)), blk, blk, blk],
        out_specs=[blk] * 4, out_shape=[jax.ShapeDtypeStruct(w.shape, F32)] * 4,
        compiler_params=_params(("arbitrary",)),
    )(parts, w, m, v)


def _pad_flat(vec, rows):
    return jnp.pad(vec, (0, rows * LANES - vec.shape[0])).reshape(rows, LANES)


def _rows_for(n):
    return -(-n // (FLAT_ROWS * LANES)) * FLAT_ROWS


def _join_shards(rows, local_shape, dim):
    parts = jnp.moveaxis(rows.reshape((N_DEV,) + tuple(local_shape)), 0, dim)
    shp = tuple(local_shape)
    return parts.reshape(shp[:dim] + (N_DEV * shp[dim],) + shp[dim + 1:])


def _s5_tables(a_re, a_im, log_step, b_re, b_im, c_re, c_im):
    lam_re = jnp.minimum(a_re, -1e-4)
    lam_im = a_im
    step = jnp.exp(log_step)[:, None]
    mag = jnp.exp(lam_re * step)
    abar_re, abar_im = mag * jnp.cos(lam_im * step), mag * jnp.sin(lam_im * step)
    den = lam_re * lam_re + lam_im * lam_im
    nr, ni = abar_re - 1.0, abar_im
    coef_re = (nr * lam_re + ni * lam_im) / den
    coef_im = (ni * lam_re - nr * lam_im) / den
    bbar_re = coef_re[..., None] * b_re - coef_im[..., None] * b_im
    bbar_im = coef_re[..., None] * b_im + coef_im[..., None] * b_re
    groups = a_re.shape[0]
    nblk = groups // S5_BLOCK_GROUPS
    eye = jnp.eye(S5_BLOCK_GROUPS, dtype=F32)

    def in_blocks(bb):
        t = jnp.swapaxes(bb, 1, 2).reshape(nblk, S5_BLOCK_GROUPS, S5_GROUP, S5_STATE)
        blk = jnp.einsum('ab,jacp->jacbp', eye, t)
        return blk.reshape(nblk, S5_BLOCK_GROUPS * S5_GROUP, S5_BLOCK_GROUPS * S5_STATE)

    def out_blocks(cc):
        t = jnp.swapaxes(cc, 1, 2).reshape(nblk, S5_BLOCK_GROUPS, S5_STATE, S5_GROUP)
        blk = jnp.einsum('ab,japc->japbc', eye, t)
        return blk.reshape(nblk, S5_BLOCK_GROUPS * S5_STATE, S5_BLOCK_GROUPS * S5_GROUP)

    rows = groups * S5_STATE // LANES
    return dict(b_re=in_blocks(bbar_re), b_im=in_blocks(bbar_im), c_re=out_blocks(c_re), c_im=out_blocks(-c_im),
                a_re=abar_re.reshape(rows, LANES), a_im=abar_im.reshape(rows, LANES))


def _in_widths(d_model):
    return [BRANCH, BRANCH, SSD_HEADS * SSD_HEAD + 2 * SSD_GROUPS * SSD_STATE, SSD_HEADS, BRANCH,
            3 * BRANCH, GDN_HEADS, GDN_HEADS, BRANCH, 3 * d_model]


def _local_loss(w, mats, x, target):
    n_meta, d_model = w['meta'].shape
    depth = len(mats['in'])
    seq = x.shape[0]
    pad_rows = CHUNK - n_meta
    first = pad_rows + n_meta
    t_all = first + seq
    alpha = (2 * depth) ** 0.25
    fns = _row_fns(d_model, pad_rows, first, alpha)
    row = lambda nm, key, n_row, n_par, widths, cap, ridx=False: _make_rowwise(nm, fns[key], n_row, n_par, widths, cap, ridx)

    h = jnp.concatenate([jnp.zeros((pad_rows, d_model), F32), w['meta'], x], axis=0)
    (h,) = row("ln_in", 'ln_in', 1, 2, [d_model], 416, True)(h, w['ln_in_g'][None], w['ln_in_b'][None])

    n_small = SSD_HEADS + 2 * GDN_HEADS

    def small_cols(ps):
        return jnp.pad(jnp.concatenate([ps[3], ps[6], ps[7]], axis=1), ((0, 0), (0, LANES - n_small)))

    for l in range(depth):
        pw, pc = mats['in'][l], w['c_in'][l]
        s5_u = _mm("in_s5u", h, pw[0], pc[0])
        s5_z = _mm("in_s5z", h, pw[1], pc[1])
        ssd_xbc = _mm("in_ssdx", h, pw[2], pc[2])
        ssd_z = _mm("in_ssdz", h, pw[4], pc[4])
        gdn_qkv = _mm("in_gdnq", h, pw[5], pc[5])
        gdn_z = _mm("in_gdnz", h, pw[8], pc[8])
        gate = _mm("in_gate", h, pw[9], pc[9])
        small = _mm("in_small", h, small_cols(pw), small_cols(pc))

        zeros_tail = jnp.zeros((LANES - n_small,), F32)
        bias = jnp.concatenate([w['ssd_dt_bias'][l], w['gdn_dt_bias'][l], jnp.zeros((GDN_HEADS,), F32), zeros_tail])[None]
        scale = jnp.concatenate([jnp.ones((SSD_HEADS,), F32), -jnp.exp(w['gdn_a_log'][l]),
                                 jnp.zeros((GDN_HEADS,), F32), zeros_tail])[None]
        (sm,) = row("small_act", 'small_act', 1, 2, [LANES], 832, True)(small, bias, scale)

        tb = _s5_tables(w['s5_a_re'][l], w['s5_a_im'][l], w['s5_log_step'][l], w['s5_b_re'][l], w['s5_b_im'][l],
                        w['s5_c_re'][l], w['s5_c_im'][l])
        srows = tb['a_re'].shape[0]
        bu_re = _make_gmm("s5_bre")(s5_u, tb['b_re']).reshape(t_all, srows, LANES)
        bu_im = _make_gmm("s5_bim")(s5_u, tb['b_im']).reshape(t_all, srows, LANES)
        s_re, s_im = _make_s5_scan("s5_scan")(bu_re, bu_im, tb['a_re'], tb['a_im'])
        y_re = _make_gmm("s5_cre")(s_re.reshape(t_all, srows * LANES), tb['c_re'])
        y_im = _make_gmm("s5_cim")(s_im.reshape(t_all, srows * LANES), tb['c_im'])
        (out_a,) = _make_rowwise("s5_tail", fns['s5_tail'], 4, 2, [d_model], 208, n_wt=2)(
            y_re, y_im, s5_u, s5_z, w['s5_d'][l][None], w['s5_b_glu'][l][None],
            mats['glu'][l], mats['branch'][l, 0], w['c_glu'][l], w['c_branch'][l, 0])

        xbc = _make_conv_rowwise("ssd_conv", fns['ssd_conv'], 2, ssd_xbc.shape[1], 416, True)(
            ssd_xbc, w['ssd_conv_w'][l], w['ssd_conv_b'][l][None])
        a_c = jnp.pad(-jnp.exp(w['ssd_a_log'][l]), (0, LANES - SSD_HEADS))[None]
        d_exp = jnp.repeat(w['ssd_d'][l], SSD_HEAD)[None]
        (y_ssd,) = _make_chunk_scan("ssd_scan", _ssd_chunk, (SSD_HEADS // 2 * LANES, SSD_STATE), 2, 2, [BRANCH])(
            xbc, sm, a_c, d_exp)
        (out_b,) = _make_rowwise("ssd_tail", fns['ssd_tail'], 2, 1, [d_model], 416, n_wt=1)(
            y_ssd, ssd_z, w['ssd_norm_g'][l][None], mats['branch'][l, 1], w['c_branch'][l, 1])

        qkv = _make_conv_rowwise("gdn_conv", fns['gdn_conv'], 1, 3 * BRANCH, 208)(gdn_qkv, w['gdn_conv_w'][l])
        (o_gdn,) = _make_chunk_scan("gdn_scan", _gdn_chunk, (GDN_HEADS * GDN_HEAD, GDN_HEAD), 2, 0, [BRANCH])(qkv, sm)
        (out_c,) = _make_rowwise("gdn_tail", fns['gdn_tail'], 2, 1, [d_model], 416, n_wt=1)(
            o_gdn, gdn_z, w['gdn_norm_g'][l][None], mats['branch'][l, 2], w['c_branch'][l, 2])

        (h,) = _make_rowwise("merge_out", fns['merge_out'], 5, 3, [d_model], 208, True, n_wt=1)(
            out_a, out_b, out_c, gate, h, w['b_gate'][l].reshape(1, 3 * d_model), w['ln_g'][l][None],
            w['ln_b'][l][None], mats['out'][l], w['c_out'][l])

    tgt = jnp.concatenate([jnp.zeros((first, d_model), F32), target], axis=0)
    (rows_loss,) = row("loss", 'loss', 2, 0, [LANES], 416, True)(h, tgt)
    return jnp.sum(rows_loss)


def _in_overlaps(d_model, n_loc):
    offs = [0]
    for wd in _in_widths(d_model):
        offs.append(offs[-1] + wd)
    out = []
    for i in range(len(offs) - 1):
        c0, c1 = offs[i], offs[i + 1]
        segs = []
        for k in range(N_DEV):
            g0, g1 = max(c0, k * n_loc), min(c1, (k + 1) * n_loc)
            if g0 < g1:
                segs.append((k, g0 - k * n_loc, g1 - k * n_loc, g0 - c0))
        out.append(segs)
    return out


STACKED = ['w_in', 's5_w_glu', 'w_branch', 'w_out']


def _shard_blocks(full, dim):
    shp = full.shape
    parts = full.reshape(shp[:dim] + (N_DEV, shp[dim] // N_DEV) + shp[dim + 1:])
    return jnp.moveaxis(parts, dim, 0)


def _join_blocks(blocks, dim):
    shp = blocks.shape[1:]
    return jnp.moveaxis(blocks, 0, dim).reshape(shp[:dim] + (N_DEV * shp[dim],) + shp[dim + 1:])


def _step(x, target, w_loc, m_loc, v_loc):
    small = [n for n in WEIGHTS if n in SHARD_DIM and n not in STACKED]
    repl = [n for n in WEIGHTS if n not in SHARD_DIM]
    size = lambda names: sum(int(w_loc[n].size) for n in names)
    depth, d_model, n_loc = w_loc['w_in'].shape
    overlaps = _in_overlaps(d_model, n_loc)
    n_chip = N_DEV // 2
    me = 4 * lax.axis_index("x") + 2 * lax.axis_index("y") + lax.axis_index("c")
    my_chip = 2 * lax.axis_index("x") + lax.axis_index("y")
    core = lax.axis_index("c").astype(jnp.int32).reshape(1)
    put = lambda buf, blk, idx: lax.dynamic_update_index_in_dim(buf, blk, idx, 0)
    take = lambda buf, idx: lax.dynamic_index_in_dim(buf, idx, 0, keepdims=False)

    rows_small = _rows_for(size(small))
    small_flat = _pad_flat(jnp.concatenate([w_loc[n].reshape(-1) for n in small]), rows_small)
    own_blocks = [w_loc[n].astype(BF16) for n in STACKED] + [small_flat]
    gathered = [put(got, blk, me) for got, blk in zip(_gather(own_blocks, "gather_weights"), own_blocks)]
    g_in = gathered[0]
    full = {n: _join_blocks(g, SHARD_DIM[n]) for n, g in zip(STACKED[1:], gathered[1:])}
    buf, off = gathered[-1].reshape(N_DEV, -1), 0
    for n in small:
        sz = int(w_loc[n].size)
        full[n] = _join_shards(buf[:, off:off + sz], w_loc[n].shape, SHARD_DIM[n])
        off += sz
    mats = dict(glu=full['s5_w_glu'], branch=full['w_branch'], out=full['w_out'], **{
        'in': [[jnp.concatenate([g_in[k, l, :, lo:hi] for k, lo, hi, _ in segs], axis=1) for segs in overlaps]
               for l in range(depth)]})
    w_diff = {n: w_loc[n] for n in repl}
    w_diff.update({n: full[n] for n in small})
    w_diff['c_in'] = [[jnp.zeros((d_model, wd), F32) for wd in _in_widths(d_model)] for _ in range(depth)]
    w_diff['c_glu'] = jnp.zeros(full['s5_w_glu'].shape, F32)
    w_diff['c_branch'] = jnp.zeros(full['w_branch'].shape, F32)
    w_diff['c_out'] = jnp.zeros(full['w_out'].shape, F32)

    loss, (g_w, g_x) = jax.value_and_grad(_local_loss, argnums=(0, 2))(w_diff, mats, x[0], target[0])
    g_w['s5_w_glu'], g_w['w_branch'], g_w['w_out'] = g_w['c_glu'], g_w['c_branch'], g_w['c_out']

    blocks = {'w_in': jnp.stack([jnp.stack([
        jnp.concatenate([g_w['c_in'][l][i][:, plo:plo + hi - lo]
                         for i, segs in enumerate(overlaps) for (kk, lo, hi, plo) in segs if kk == k], axis=1)
        for l in range(depth)]) for k in range(N_DEV)])}
    for n in STACKED[1:]:
        blocks[n] = _shard_blocks(g_w[n], SHARD_DIM[n])
    cols = {n: w_loc[n].shape[-1] for n in STACKED}
    send = [blocks[n].astype(BF16).reshape(n_chip, 2, -1, cols[n]) for n in STACKED]
    red_names = small + repl
    n_red = sum(int(g_w[n].size) for n in red_names) + 1
    red = _pad_flat(jnp.concatenate([g_w[n].reshape(-1) for n in red_names] + [loss.reshape(1)]), _rows_for(n_red))
    *recv, red_sib = _pair_exchange(send, [red], "pair_exchange")
    sums = [_pair_sum(s, r, core, "pair_sum_" + n) for s, r, n in zip(send, recv, STACKED)]
    red_chip = _add2(red, red_sib, "pair_sum_rest")
    *parts, red_parts = _chip_exchange(sums, [red_chip], "exchange_grads")
    parts = [put(p, take(s, my_chip), my_chip) for p, s in zip(parts, sums)]
    red_total = _add_parts(put(red_parts, red_chip, my_chip), "reduce_rest").reshape(-1)

    outs = {}
    for n, p in zip(STACKED, parts):
        as_rows = lambda a: a.reshape(-1, cols[n])
        res = _adamw_rows(p, as_rows(w_loc[n]), as_rows(m_loc[n]), as_rows(v_loc[n]), "adamw_" + n)
        outs[n] = [r.reshape(w_loc[n].shape) for r in res]
    g_red, off = {}, 0
    for n in red_names:
        sz = int(g_w[n].size)
        g_red[n] = red_total[off:off + sz].reshape(g_w[n].shape)
        off += sz
    loss_total = red_total[off]
    for n in small:
        g_red[n] = take(_shard_blocks(g_red[n], SHARD_DIM[n]), me)
    rows_rest = _rows_for(size(red_names))
    flat = lambda src: _pad_flat(jnp.concatenate([src[n].reshape(-1) for n in red_names]), rows_rest)
    res = _adamw_rows(flat(g_red)[None], flat(w_loc), flat(m_loc), flat(v_loc), "adamw_rest")
    off = 0
    for n in red_names:
        sz = int(w_loc[n].size)
        outs[n] = [r.reshape(-1)[off:off + sz].reshape(w_loc[n].shape) for r in res]
        off += sz
    return (loss_total, g_x[None], *[outs[n][k] for k in range(4) for n in WEIGHTS])


def kernel(x, meta, ln_in_g, ln_in_b, w_in, s5_a_re, s5_a_im, s5_log_step, s5_b_re, s5_b_im, s5_c_re, s5_c_im, s5_d, s5_w_glu, s5_b_glu, ssd_conv_w, ssd_conv_b, ssd_dt_bias, ssd_a_log, ssd_d, ssd_norm_g, gdn_conv_w, gdn_dt_bias, gdn_a_log, gdn_norm_g, w_branch, b_gate, w_out, ln_g, ln_b, loss_target, m_meta, m_ln_in_g, m_ln_in_b, m_w_in, m_s5_a_re, m_s5_a_im, m_s5_log_step, m_s5_b_re, m_s5_b_im, m_s5_c_re, m_s5_c_im, m_s5_d, m_s5_w_glu, m_s5_b_glu, m_ssd_conv_w, m_ssd_conv_b, m_ssd_dt_bias, m_ssd_a_log, m_ssd_d, m_ssd_norm_g, m_gdn_conv_w, m_gdn_dt_bias, m_gdn_a_log, m_gdn_norm_g, m_w_branch, m_b_gate, m_w_out, m_ln_g, m_ln_b, v_meta, v_ln_in_g, v_ln_in_b, v_w_in, v_s5_a_re, v_s5_a_im, v_s5_log_step, v_s5_b_re, v_s5_b_im, v_s5_c_re, v_s5_c_im, v_s5_d, v_s5_w_glu, v_s5_b_glu, v_ssd_conv_w, v_ssd_conv_b, v_ssd_dt_bias, v_ssd_a_log, v_ssd_d, v_ssd_norm_g, v_gdn_conv_w, v_gdn_dt_bias, v_gdn_a_log, v_gdn_norm_g, v_w_branch, v_b_gate, v_w_out, v_ln_g, v_ln_b):
    w_loc = dict(zip(WEIGHTS, (meta, ln_in_g, ln_in_b, w_in, s5_a_re, s5_a_im, s5_log_step, s5_b_re, s5_b_im, s5_c_re, s5_c_im, s5_d, s5_w_glu, s5_b_glu, ssd_conv_w, ssd_conv_b, ssd_dt_bias, ssd_a_log, ssd_d, ssd_norm_g, gdn_conv_w, gdn_dt_bias, gdn_a_log, gdn_norm_g, w_branch, b_gate, w_out, ln_g, ln_b)))
    m_loc = dict(zip(WEIGHTS, (m_meta, m_ln_in_g, m_ln_in_b, m_w_in, m_s5_a_re, m_s5_a_im, m_s5_log_step, m_s5_b_re, m_s5_b_im, m_s5_c_re, m_s5_c_im, m_s5_d, m_s5_w_glu, m_s5_b_glu, m_ssd_conv_w, m_ssd_conv_b, m_ssd_dt_bias, m_ssd_a_log, m_ssd_d, m_ssd_norm_g, m_gdn_conv_w, m_gdn_dt_bias, m_gdn_a_log, m_gdn_norm_g, m_w_branch, m_b_gate, m_w_out, m_ln_g, m_ln_b)))
    v_loc = dict(zip(WEIGHTS, (v_meta, v_ln_in_g, v_ln_in_b, v_w_in, v_s5_a_re, v_s5_a_im, v_s5_log_step, v_s5_b_re, v_s5_b_im, v_s5_c_re, v_s5_c_im, v_s5_d, v_s5_w_glu, v_s5_b_glu, v_ssd_conv_w, v_ssd_conv_b, v_ssd_dt_bias, v_ssd_a_log, v_ssd_d, v_ssd_norm_g, v_gdn_conv_w, v_gdn_dt_bias, v_gdn_a_log, v_gdn_norm_g, v_w_branch, v_b_gate, v_w_out, v_ln_g, v_ln_b)))
    return _step(x, loss_target, w_loc, m_loc, v_loc)
```

```python
import functools
import math

import jax
import jax.numpy as jnp
from jax import lax
from jax.experimental import pallas as pl
from jax.experimental.pallas import tpu as pltpu

F32 = jnp.float32
BF16 = jnp.bfloat16

N_DEV = 8
LANES = 128
SUBLANES = 8
VMEM_LIMIT = 56 * 1024 * 1024
FLAT_ROWS = 1024

CHUNK = 64
CONV_K = 4
S5_GROUP = 16
S5_STATE = 64
S5_BLOCK_GROUPS = 8
SSD_HEAD = 64
SSD_HEADS = 12
SSD_GROUPS = 2
SSD_STATE = 128
GDN_HEAD = 128
GDN_HEADS = 6
BRANCH = 768
LN_EPS = 1e-5

ADAM_LR = 0.001
ADAM_B1 = 0.9
ADAM_B2 = 0.999
ADAM_EPS = 1e-08
ADAM_WD = 0.01
ADAM_STEP = 10

WEIGHTS = ['meta', 'ln_in_g', 'ln_in_b', 'w_in', 's5_a_re', 's5_a_im', 's5_log_step', 's5_b_re', 's5_b_im',
           's5_c_re', 's5_c_im', 's5_d', 's5_w_glu', 's5_b_glu', 'ssd_conv_w', 'ssd_conv_b', 'ssd_dt_bias',
           'ssd_a_log', 'ssd_d', 'ssd_norm_g', 'gdn_conv_w', 'gdn_dt_bias', 'gdn_a_log', 'gdn_norm_g',
           'w_branch', 'b_gate', 'w_out', 'ln_g', 'ln_b']
SHARD_DIM = {'meta': 1, 'w_in': 2, 's5_w_glu': 1, 'ssd_conv_w': 2, 'gdn_conv_w': 2, 'w_branch': 3, 'b_gate': 2,
             'w_out': 1}


def _params(sem):
    return pltpu.CompilerParams(dimension_semantics=sem, vmem_limit_bytes=VMEM_LIMIT)


def _row_tile(m, cap):
    best = None
    for t in range(SUBLANES, min(m, cap) + 1, SUBLANES):
        if m % t == 0:
            best = t
    return best if best is not None else m


def _col_tile(n, cap):
    best = None
    for t in range(LANES, min(n, cap) + 1, LANES):
        if n % t == 0:
            best = t
    return best if best is not None else n


def _any_tile(m, cap):
    best = 1
    for t in range(1, min(m, cap) + 1):
        if m % t == 0:
            best = t
    return best


def _mm_fwd(a, b, name):
    m, _ = a.shape
    g, k, n = b.shape
    tm, tn = _row_tile(m, 832), _col_tile(n, 1024)
    nj = n // tn

    def body(a_ref, b_ref, o_ref):
        o_ref[...] = jnp.dot(a_ref[...].astype(BF16), b_ref[0].astype(BF16), preferred_element_type=F32)

    return pl.pallas_call(
        body, name=name, grid=(g, m // tm, nj),
        in_specs=[pl.BlockSpec((tm, k), lambda gi, i, j: (i, gi)),
                  pl.BlockSpec((1, k, tn), lambda gi, i, j: (gi, 0, j))],
        out_specs=pl.BlockSpec((tm, tn), lambda gi, i, j: (i, gi * nj + j)),
        out_shape=jax.ShapeDtypeStruct((m, g * n), F32),
        compiler_params=_params(("arbitrary", "arbitrary", "arbitrary")),
    )(a, b)


def _mm_da(ct, b, name):
    m, _ = ct.shape
    g, k, n = b.shape
    tm, tk = _row_tile(m, 832 if n <= 1536 else 416), _col_tile(k, 1024)
    nk = k // tk

    def body(c_ref, b_ref, o_ref):
        o_ref[...] = lax.dot_general(c_ref[...].astype(BF16), b_ref[0].astype(BF16), (((1,), (1,)), ((), ())),
                                     preferred_element_type=F32)

    return pl.pallas_call(
        body, name=name, grid=(g, m // tm, nk),
        in_specs=[pl.BlockSpec((tm, n), lambda gi, i, j: (i, gi)),
                  pl.BlockSpec((1, tk, n), lambda gi, i, j: (gi, j, 0))],
        out_specs=pl.BlockSpec((tm, tk), lambda gi, i, j: (i, gi * nk + j)),
        out_shape=jax.ShapeDtypeStruct((m, g * k), F32),
        compiler_params=_params(("arbitrary", "arbitrary", "arbitrary")),
    )(ct, b)


def _mm_db(a, ct, g, k, n, name):
    m = a.shape[0]
    tm, tk, tn = _row_tile(m, 832), _col_tile(k, 1024), _col_tile(n, 1280)
    nk, nn = k // tk, n // tn

    def body(a_ref, c_ref, o_ref):
        @pl.when(pl.program_id(3) == 0)
        def _():
            o_ref[...] = jnp.zeros_like(o_ref)

        o_ref[0] += lax.dot_general(a_ref[...].astype(BF16), c_ref[...].astype(BF16), (((0,), (0,)), ((), ())),
                                    preferred_element_type=F32)

    return pl.pallas_call(
        body, name=name, grid=(g, nk, nn, m // tm),
        in_specs=[pl.BlockSpec((tm, tk), lambda gi, i, j, r: (r, gi * nk + i)),
                  pl.BlockSpec((tm, tn), lambda gi, i, j, r: (r, gi * nn + j))],
        out_specs=pl.BlockSpec((1, tk, tn), lambda gi, i, j, r: (gi, i, j)),
        out_shape=jax.ShapeDtypeStruct((g, k, n), F32),
        compiler_params=_params(("arbitrary", "arbitrary", "arbitrary", "arbitrary")),
    )(a, ct)


def _make_gmm(name):
    @jax.custom_vjp
    def gmm(a, b):
        return _mm_fwd(a, b, name + "_fwd")

    def fwd(a, b):
        return _mm_fwd(a, b, name + "_fwd"), (a, b)

    def bwd(res, ct):
        a, b = res
        g, k, n = b.shape
        return _mm_da(ct, b, name + "_da"), _mm_db(a, ct, g, k, n, name + "_db")

    gmm.defvjp(fwd, bwd)
    return gmm


def _mm(name, a, a16, w, carrier):
    @jax.custom_vjp
    def mm(a, a16, w3, carrier3):
        return _mm_fwd(a16, w3, name + "_fwd")

    def fwd(a, a16, w3, carrier3):
        return _mm_fwd(a16, w3, name + "_fwd"), (a16, w3)

    def bwd(res, ct):
        a16, w3 = res
        g, k, n = w3.shape
        return (_mm_da(ct, w3, name + "_da"), jnp.zeros_like(a16), jnp.zeros_like(w3),
                _mm_db(a16, ct, g, k, n, name + "_db"))

    mm.defvjp(fwd, bwd)
    return mm(a, a16, w[None], carrier[None])


@jax.custom_vjp
def _wdot(x, w, carrier):
    return _dot(x, w)


def _wdot_fwd(x, w, carrier):
    return _dot(x, w), (x, w)


def _wdot_bwd(res, ct):
    x, w = res
    return _dot_nt(ct, w), jnp.zeros_like(w), _dot_tn(x, ct)


_wdot.defvjp(_wdot_fwd, _wdot_bwd)


def _make_rowwise(name, fn, n_row, n_par, out_widths, tm_cap, use_ridx=False, n_wt=0, out_dtypes=None):
    n_out = len(out_widths)
    out_dtypes = out_dtypes or [F32] * n_out
    n_in = n_row + n_par + n_wt

    def bind(tm):
        if not use_ridx:
            return fn
        ridx = pl.program_id(0) * tm + lax.broadcasted_iota(jnp.int32, (tm, 1), 0)
        return functools.partial(fn, ridx)

    def specs(args, tm):
        rows = [pl.BlockSpec((tm, a.shape[1]), lambda i: (i, 0)) for a in args[:n_row]]
        pars = [pl.BlockSpec(a.shape, lambda i: (0, 0)) for a in args[n_row:n_in]]
        return rows, pars

    def fwd_call(*args):
        t = args[0].shape[0]
        tm = _row_tile(t, tm_cap)
        rows, pars = specs(args, tm)

        def body(*refs):
            vals = [r[...] for r in refs[:n_row + n_par]]
            mats = [functools.partial(lambda x, w: _dot(x, w), w=r[...]) for r in refs[n_row + n_par:n_in]]
            res = bind(tm)(*vals, *mats)
            for o_ref, r in zip(refs[n_in:], res):
                o_ref[...] = r

        outs = pl.pallas_call(
            body, name=name + "_fwd", grid=(t // tm,), in_specs=rows + pars,
            out_specs=[pl.BlockSpec((tm, w), lambda i: (i, 0)) for w in out_widths],
            out_shape=[jax.ShapeDtypeStruct((t, w), dt) for w, dt in zip(out_widths, out_dtypes)],
            compiler_params=_params(("arbitrary",)),
        )(*args)
        return tuple(outs)

    def bwd_call(args, cts):
        t = args[0].shape[0]
        tm = _row_tile(t, tm_cap)
        rows, pars = specs(args, tm)
        ct_specs = [pl.BlockSpec((tm, w), lambda i: (i, 0)) for w in out_widths]
        n_diff = n_row + n_par

        def body(*refs):
            vals = [r[...] for r in refs[:n_diff]]
            wts = [r[...] for r in refs[n_diff:n_in]]
            ct_vals = tuple(r[...] for r in refs[n_in:n_in + n_out])
            d_refs = refs[n_in + n_out:]
            f = bind(tm)

            def g(*a):
                mats = [functools.partial(lambda x, w, c: _wdot(x, w, c), w=w, c=c)
                        for w, c in zip(wts, a[n_diff:])]
                return tuple(f(*a[:n_diff], *mats))

            _, vjp = jax.vjp(g, *vals, *[jnp.zeros(w.shape, F32) for w in wts])
            grads = vjp(ct_vals)
            for i in range(n_row):
                d_refs[i][...] = grads[i]
            if n_par + n_wt:
                @pl.when(pl.program_id(0) == 0)
                def _():
                    for r in d_refs[n_row:]:
                        r[...] = jnp.zeros_like(r)

                for r, gr in zip(d_refs[n_row:], grads[n_row:]):
                    r[...] += gr

        outs = pl.pallas_call(
            body, name=name + "_bwd", grid=(t // tm,), in_specs=rows + pars + ct_specs,
            out_specs=rows + pars,
            out_shape=[jax.ShapeDtypeStruct(a.shape, F32) for a in args],
            compiler_params=_params(("arbitrary",)),
        )(*args, *cts)
        return tuple(outs)

    @jax.custom_vjp
    def op(*args):
        return fwd_call(*args[:n_in])

    def fwd(*args):
        return fwd_call(*args[:n_in]), args[:n_in]

    def bwd(args, cts):
        grads = bwd_call(args, cts)
        return grads[:n_row + n_par] + tuple(jnp.zeros_like(a) for a in args[n_row + n_par:]) + grads[n_row + n_par:]

    op.defvjp(fwd, bwd)
    return op


HALO = SUBLANES


def _make_conv_rowwise(name, fn, n_par, out_width, tm_cap, use_ridx=False):
    def bind(ridx):
        return functools.partial(fn, ridx) if use_ridx else fn

    def stage(x_ref, halo_ref, xs, first):
        xs[0:HALO, :] = jnp.where(first, 0.0, halo_ref[...])
        xs[HALO:, :] = x_ref[...]

    def taps(xs, tm):
        return [xs[pl.ds(HALO - (CONV_K - 1) + j, tm), :] for j in range(CONV_K)]

    def fwd_call(x, *pars):
        t, wd = x.shape
        tm = _row_tile(t, tm_cap)
        per = tm // HALO

        def body(*refs):
            x_ref, halo_ref = refs[:2]
            par_refs, o_ref, xs = refs[2:2 + n_par], refs[2 + n_par], refs[-1]
            i = pl.program_id(0)
            stage(x_ref, halo_ref, xs, i == 0)
            ridx = i * tm + lax.broadcasted_iota(jnp.int32, (tm, 1), 0)
            (o_ref[...],) = bind(ridx)(*taps(xs, tm), *[r[...] for r in par_refs])

        return pl.pallas_call(
            body, name=name + "_fwd", grid=(t // tm,),
            in_specs=[pl.BlockSpec((tm, wd), lambda i: (i, 0)),
                      pl.BlockSpec((HALO, wd), lambda i: (jnp.maximum(i * per - 1, 0), 0))]
            + [pl.BlockSpec(p.shape, lambda i: (0, 0)) for p in pars],
            out_specs=pl.BlockSpec((tm, out_width), lambda i: (i, 0)),
            out_shape=jax.ShapeDtypeStruct((t, out_width), F32),
            scratch_shapes=[pltpu.VMEM((tm + HALO, wd), F32)],
            compiler_params=_params(("arbitrary",)),
        )(x, x, *pars)

    def bwd_call(x, pars, ct):
        t, wd = x.shape
        tm = _row_tile(t, tm_cap)
        per = tm // HALO
        nb = t // tm

        def body(*refs):
            x_ref, halo_ref = refs[:2]
            par_refs, ct_ref = refs[2:2 + n_par], refs[2 + n_par]
            dx_ref, dpar_refs = refs[3 + n_par], refs[4 + n_par:4 + 2 * n_par]
            xs, ds, carry = refs[-3:]
            step = pl.program_id(0)
            blk = nb - 1 - step

            @pl.when(step == 0)
            def _():
                carry[...] = jnp.zeros_like(carry)
                for r in dpar_refs:
                    r[...] = jnp.zeros_like(r)

            stage(x_ref, halo_ref, xs, blk == 0)
            ridx = blk * tm + lax.broadcasted_iota(jnp.int32, (tm, 1), 0)
            f = bind(ridx)
            _, vjp = jax.vjp(lambda *a: tuple(f(*a)), *taps(xs, tm), *[r[...] for r in par_refs])
            grads = vjp((ct_ref[...],))
            ds[...] = jnp.zeros_like(ds)
            for j in range(CONV_K):
                ds[pl.ds(HALO - (CONV_K - 1) + j, tm), :] += grads[j]
            ds[pl.ds(tm, HALO), :] += carry[...]
            dx_ref[...] = ds[HALO:, :]
            carry[...] = ds[0:HALO, :]
            for r, g in zip(dpar_refs, grads[CONV_K:]):
                r[...] += g

        rev = lambda i: (nb - 1 - i, 0)
        outs = pl.pallas_call(
            body, name=name + "_bwd", grid=(nb,),
            in_specs=[pl.BlockSpec((tm, wd), rev),
                      pl.BlockSpec((HALO, wd), lambda i: (jnp.maximum((nb - 1 - i) * per - 1, 0), 0))]
            + [pl.BlockSpec(p.shape, lambda i: (0, 0)) for p in pars]
            + [pl.BlockSpec((tm, out_width), rev)],
            out_specs=[pl.BlockSpec((tm, wd), rev)] + [pl.BlockSpec(p.shape, lambda i: (0, 0)) for p in pars],
            out_shape=[jax.ShapeDtypeStruct(x.shape, F32)] + [jax.ShapeDtypeStruct(p.shape, F32) for p in pars],
            scratch_shapes=[pltpu.VMEM((tm + HALO, wd), F32), pltpu.VMEM((tm + HALO, wd), F32),
                            pltpu.VMEM((HALO, wd), F32)],
            compiler_params=_params(("arbitrary",)),
        )(x, x, *pars, ct)
        return tuple(outs)

    @jax.custom_vjp
    def op(x, *pars):
        return fwd_call(x, *pars)

    def fwd(x, *pars):
        return fwd_call(x, *pars), (x, pars)

    def bwd(res, ct):
        x, pars = res
        return bwd_call(x, pars, ct)

    op.defvjp(fwd, bwd)
    return op


def _make_chunk_scan(name, fn, state_shape, n_seq, n_par, out_widths):
    n_out = len(out_widths)
    zeros_idx = (0,) * len(state_shape)

    def fwd_call(*args):
        t = args[0].shape[0]
        nc = t // CHUNK
        seq_specs = [pl.BlockSpec((CHUNK, a.shape[1]), lambda c: (c, 0)) for a in args[:n_seq]]
        par_specs = [pl.BlockSpec(a.shape, lambda c: (0, 0)) for a in args[n_seq:]]

        def body(*refs):
            ins = refs[:n_seq + n_par]
            out_refs = refs[n_seq + n_par:n_seq + n_par + n_out]
            states_ref = refs[n_seq + n_par + n_out]
            st = refs[-1]

            @pl.when(pl.program_id(0) == 0)
            def _():
                st[...] = jnp.zeros_like(st)

            s0 = st[...]
            states_ref[0] = s0
            res = fn(s0, *[r[...] for r in ins])
            st[...] = res[0]
            for o_ref, r in zip(out_refs, res[1:]):
                o_ref[...] = r

        outs = pl.pallas_call(
            body, name=name + "_fwd", grid=(nc,), in_specs=seq_specs + par_specs,
            out_specs=[pl.BlockSpec((CHUNK, w), lambda c: (c, 0)) for w in out_widths]
            + [pl.BlockSpec((1,) + state_shape, lambda c: (c,) + zeros_idx)],
            out_shape=[jax.ShapeDtypeStruct((t, w), F32) for w in out_widths]
            + [jax.ShapeDtypeStruct((nc,) + state_shape, F32)],
            scratch_shapes=[pltpu.VMEM(state_shape, F32)],
            compiler_params=_params(("arbitrary",)),
        )(*args)
        return tuple(outs[:n_out]), outs[n_out]

    def bwd_call(args, states, cts):
        t = args[0].shape[0]
        nc = t // CHUNK
        rev = lambda c: (nc - 1 - c, 0)
        seq_specs = [pl.BlockSpec((CHUNK, a.shape[1]), rev) for a in args[:n_seq]]
        par_specs = [pl.BlockSpec(a.shape, lambda c: (0, 0)) for a in args[n_seq:]]
        ct_specs = [pl.BlockSpec((CHUNK, w), rev) for w in out_widths]
        st_spec = pl.BlockSpec((1,) + state_shape, lambda c: (nc - 1 - c,) + zeros_idx)
        n_in = n_seq + n_par

        def body(*refs):
            vals = [r[...] for r in refs[:n_in]]
            s0 = refs[n_in][0]
            ct_vals = tuple(r[...] for r in refs[n_in + 1:n_in + 1 + n_out])
            d_refs = refs[n_in + 1 + n_out:-1]
            dst = refs[-1]

            @pl.when(pl.program_id(0) == 0)
            def _():
                dst[...] = jnp.zeros_like(dst)
                for j in range(n_par):
                    d_refs[n_seq + j][...] = jnp.zeros_like(d_refs[n_seq + j])

            _, vjp = jax.vjp(lambda *a: tuple(fn(*a)), s0, *vals)
            grads = vjp((dst[...],) + ct_vals)
            dst[...] = grads[0]
            for i in range(n_seq):
                d_refs[i][...] = grads[1 + i]
            for j in range(n_par):
                d_refs[n_seq + j][...] += grads[1 + n_seq + j]

        outs = pl.pallas_call(
            body, name=name + "_bwd", grid=(nc,), in_specs=seq_specs + par_specs + [st_spec] + ct_specs,
            out_specs=seq_specs + par_specs,
            out_shape=[jax.ShapeDtypeStruct(a.shape, F32) for a in args],
            scratch_shapes=[pltpu.VMEM(state_shape, F32)],
            compiler_params=_params(("arbitrary",)),
        )(*args, states, *cts)
        return tuple(outs)

    @jax.custom_vjp
    def op(*args):
        return fwd_call(*args)[0]

    def fwd(*args):
        outs, states = fwd_call(*args)
        return outs, (args, states)

    def bwd(res, cts):
        args, states = res
        return bwd_call(args, states, cts)

    op.defvjp(fwd, bwd)
    return op


def _s5_scan_fwd(bre, bim, are, aim, name):
    t, r, _ = bre.shape
    tb = _any_tile(t, 208)
    blk = pl.BlockSpec((tb, r, LANES), lambda i: (i, 0, 0))
    par = pl.BlockSpec((r, LANES), lambda i: (0, 0))

    def body(bre_ref, bim_ref, are_ref, aim_ref, sre_ref, sim_ref, st):
        @pl.when(pl.program_id(0) == 0)
        def _():
            st[...] = jnp.zeros_like(st)

        ar, ai = are_ref[...], aim_ref[...]

        def step(k, carry):
            sr, si = carry
            nr = ar * sr - ai * si + bre_ref[k]
            ni = ar * si + ai * sr + bim_ref[k]
            sre_ref[k] = nr
            sim_ref[k] = ni
            return nr, ni

        sr, si = lax.fori_loop(0, tb, step, (st[0], st[1]), unroll=4)
        st[0] = sr
        st[1] = si

    return pl.pallas_call(
        body, name=name, grid=(t // tb,), in_specs=[blk, blk, par, par], out_specs=[blk, blk],
        out_shape=[jax.ShapeDtypeStruct(bre.shape, F32)] * 2,
        scratch_shapes=[pltpu.VMEM((2, r, LANES), F32)],
        compiler_params=_params(("arbitrary",)),
    )(bre, bim, are, aim)


def _s5_scan_bwd(dsr, dsi, sre, sim, are, aim, name):
    t, r, _ = sre.shape
    tb = _any_tile(t, 208)
    nb = t // tb
    blk = pl.BlockSpec((tb, r, LANES), lambda i: (nb - 1 - i, 0, 0))
    par = pl.BlockSpec((r, LANES), lambda i: (0, 0))

    def body(dsr_ref, dsi_ref, sre_ref, sim_ref, are_ref, aim_ref, gre_ref, gim_ref, dar_ref, dai_ref, st):
        @pl.when(pl.program_id(0) == 0)
        def _():
            st[...] = jnp.zeros_like(st)
            dar_ref[...] = jnp.zeros_like(dar_ref)
            dai_ref[...] = jnp.zeros_like(dai_ref)

        ar, ai = are_ref[...], aim_ref[...]

        def step(k, carry):
            gr, gi, dar, dai = carry
            q = tb - 1 - k
            s_r, s_i = sre_ref[q], sim_ref[q]
            dar = dar + gr * s_r + gi * s_i
            dai = dai + gi * s_r - gr * s_i
            ngr = dsr_ref[q] + ar * gr + ai * gi
            ngi = dsi_ref[q] + ar * gi - ai * gr
            gre_ref[q] = ngr
            gim_ref[q] = ngi
            return ngr, ngi, dar, dai

        gr, gi, dar, dai = lax.fori_loop(0, tb, step, (st[0], st[1], dar_ref[...], dai_ref[...]), unroll=4)
        st[0] = gr
        st[1] = gi
        dar_ref[...] = dar
        dai_ref[...] = dai

    return pl.pallas_call(
        body, name=name, grid=(nb,), in_specs=[blk, blk, blk, blk, par, par], out_specs=[blk, blk, par, par],
        out_shape=[jax.ShapeDtypeStruct(sre.shape, F32)] * 2 + [jax.ShapeDtypeStruct(are.shape, F32)] * 2,
        scratch_shapes=[pltpu.VMEM((2, r, LANES), F32)],
        compiler_params=_params(("arbitrary",)),
    )(dsr, dsi, sre, sim, are, aim)


def _make_s5_scan(name):
    @jax.custom_vjp
    def scan(bre, bim, are, aim):
        return tuple(_s5_scan_fwd(bre, bim, are, aim, name + "_fwd"))

    def fwd(bre, bim, are, aim):
        sre, sim = _s5_scan_fwd(bre, bim, are, aim, name + "_fwd")
        return (sre, sim), (sre, sim, are, aim)

    def bwd(res, cts):
        sre, sim, are, aim = res
        return tuple(_s5_scan_bwd(cts[0], cts[1], sre, sim, are, aim, name + "_bwd"))

    scan.defvjp(fwd, bwd)
    return scan


def _dot(a, b):
    return jnp.dot(a.astype(BF16), b.astype(BF16), preferred_element_type=F32)


def _dot_nt(a, b):
    return lax.dot_general(a.astype(BF16), b.astype(BF16), (((1,), (1,)), ((), ())), preferred_element_type=F32)


def _dot_tn(a, b):
    return lax.dot_general(a.astype(BF16), b.astype(BF16), (((0,), (0,)), ((), ())), preferred_element_type=F32)


def _split3(x):
    x1 = x.astype(BF16)
    rest = x - x1.astype(F32)
    x2 = rest.astype(BF16)
    return x1, x2, (rest - x2.astype(F32)).astype(BF16)


def _sel_dot(dims, a, b, a_is_sel):
    sel = (a if a_is_sel else b).astype(BF16)
    acc = None
    for piece in _split3(b if a_is_sel else a):
        pair = (sel, piece) if a_is_sel else (piece, sel)
        term = lax.dot_general(*pair, (dims, ((), ())), preferred_element_type=F32)
        acc = term if acc is None else acc + term
    return acc


@jax.custom_vjp
def _running_sum(low, y):
    return _sel_dot(((1,), (0,)), low, y, True)


_running_sum.defvjp(lambda low, y: (_running_sum(low, y), low),
                    lambda low, ct: (jnp.zeros_like(low), _sel_dot(((0,), (0,)), low, ct, True)))


@jax.custom_vjp
def _spread(x, sel):
    return _sel_dot(((1,), (0,)), x, sel, False)


_spread.defvjp(lambda x, sel: (_spread(x, sel), sel),
               lambda sel, ct: (_sel_dot(((1,), (1,)), ct, sel, False), jnp.zeros_like(sel)))


@jax.custom_vjp
def _transposed(x, eye):
    return _sel_dot(((0,), (0,)), x, eye, False)


_transposed.defvjp(lambda x, eye: (_transposed(x, eye), eye),
                   lambda eye, ct: (_sel_dot(((1,), (1,)), eye, ct, True), jnp.zeros_like(eye)))


def _iota(shape, dim):
    return lax.broadcasted_iota(jnp.int32, shape, dim)


def _tri(strict=False):
    r, c = _iota((CHUNK, CHUNK), 0), _iota((CHUNK, CHUNK), 1)
    return (r > c) if strict else (r >= c)


def _silu(x):
    return x * jax.nn.sigmoid(x)


def _layer_norm(z, g, b):
    mu = jnp.mean(z, axis=-1, keepdims=True)
    var = jnp.mean(jnp.square(z - mu), axis=-1, keepdims=True)
    return (z - mu) * lax.rsqrt(var + LN_EPS) * g + b


def _ssd_chunk(state, xbc, sm, a_c, d_exp):
    width = SSD_HEADS * SSD_HEAD
    x = xbc[:, :width]
    lane = _iota((CHUNK, LANES), 1)
    dtc = jnp.where(lane < SSD_HEADS, sm, 0.0)
    low = _tri().astype(F32)
    eye = (_iota((CHUNK, CHUNK), 0) == _iota((CHUNK, CHUNK), 1)).astype(F32)
    head_col, head_row = _iota((LANES, width), 1), _iota((LANES, width), 0) * SSD_HEAD
    expand = ((head_col >= head_row) & (head_col < head_row + SSD_HEAD)).astype(F32)
    acum_c = _running_sum(low, dtc * a_c)
    acum_ct = _transposed(acum_c, eye)
    dt_exp = _spread(dtc, expand)
    acum = _spread(acum_c, expand)
    xd = x * dt_exp
    last = acum[CHUNK - 1:CHUNK, :]
    to_end = jnp.exp(last - acum)
    eac = jnp.exp(acum)
    causal = _tri()
    first_half = _iota((CHUNK, LANES), 1) < SSD_HEAD
    top_rows = _iota((LANES, LANES), 0) < SSD_HEAD
    pairs = range(SSD_HEADS // 2)
    grp = [(2 * p) // (SSD_HEADS // SSD_GROUPS) for p in pairs]
    cols = [slice(p * LANES, (p + 1) * LANES) for p in pairs]
    bg = [xbc[:, width + g * SSD_STATE: width + (g + 1) * SSD_STATE] for g in range(SSD_GROUPS)]
    cg = [xbc[:, width + (SSD_GROUPS + g) * SSD_STATE: width + (SSD_GROUPS + g + 1) * SSD_STATE]
          for g in range(SSD_GROUPS)]
    scores = [_dot_nt(cg[g], bg[g]) for g in range(SSD_GROUPS)]
    dec = [jnp.where(causal, jnp.exp(jnp.minimum(acum_c[:, h:h + 1] - acum_ct[h:h + 1, :], 0.0)), 0.0)
           for h in range(SSD_HEADS)]
    y_lo = [_dot(scores[grp[p]] * dec[2 * p], jnp.where(first_half, xd[:, cols[p]], 0.0)) for p in pairs]
    y_hi = [_dot(scores[grp[p]] * dec[2 * p + 1], jnp.where(first_half, 0.0, xd[:, cols[p]])) for p in pairs]
    s_prev = [state[p * LANES:(p + 1) * LANES, :] for p in pairs]
    y_off = [_dot_nt(cg[grp[p]], s_prev[p]) for p in pairs]
    s_add = [_dot_tn(xd[:, cols[p]] * to_end[:, cols[p]], bg[grp[p]]) for p in pairs]
    ys = [y_lo[p] + y_hi[p] + y_off[p] * eac[:, cols[p]] + x[:, cols[p]] * d_exp[:, cols[p]] for p in pairs]
    cd = [jnp.where(top_rows, jnp.exp(acum_c[CHUNK - 1:CHUNK, 2 * p:2 * p + 1]),
                    jnp.exp(acum_c[CHUNK - 1:CHUNK, 2 * p + 1:2 * p + 2])) for p in pairs]
    new_states = [s_prev[p] * cd[p] + s_add[p] for p in pairs]
    return jnp.concatenate(new_states, axis=0), jnp.concatenate(ys, axis=1)


def _neumann(a_mats, rhs, transposed):
    dot = _dot_tn if transposed else _dot
    nmats, sols = [-a for a in a_mats], list(rhs)
    for i in range(6):
        upd = [dot(n, s) for n, s in zip(nmats, sols)]
        if i < 5:
            nmats = [_dot(n, n) for n in nmats]
        sols = [s + u for s, u in zip(sols, upd)]
    return tuple(sols)


@jax.custom_vjp
def _solve_unit_lower(a_mats, rhs):
    return _neumann(a_mats, rhs, False)


def _solve_unit_lower_fwd(a_mats, rhs):
    sols = _neumann(a_mats, rhs, False)
    return sols, (a_mats, sols)


def _solve_unit_lower_bwd(res, d_sols):
    a_mats, sols = res
    d_rhs = _neumann(a_mats, d_sols, True)
    return tuple(-_dot_nt(dr, x) for dr, x in zip(d_rhs, sols)), d_rhs


_solve_unit_lower.defvjp(_solve_unit_lower_fwd, _solve_unit_lower_bwd)


def _gdn_chunk(state, qkv, sm):
    width = GDN_HEADS * GDN_HEAD
    g0, b0 = SSD_HEADS, SSD_HEADS + GDN_HEADS
    lane = _iota((CHUNK, LANES), 1)
    gc = jnp.where((lane >= g0) & (lane < b0), sm, 0.0)
    low = _tri().astype(F32)
    eye = (_iota((CHUNK, CHUNK), 0) == _iota((CHUNK, CHUNK), 1)).astype(F32)
    gcum = _running_sum(low, gc)
    gcum_t = _transposed(gcum, eye)
    causal, strict = _tri(), _tri(strict=True)
    heads = range(GDN_HEADS)
    q = [qkv[:, h * GDN_HEAD:(h + 1) * GDN_HEAD] for h in heads]
    k = [qkv[:, width + h * GDN_HEAD: width + (h + 1) * GDN_HEAD] for h in heads]
    v = [qkv[:, 2 * width + h * GDN_HEAD: 2 * width + (h + 1) * GDN_HEAD] for h in heads]
    beta = [sm[:, b0 + h:b0 + h + 1] for h in heads]
    gcol = [gcum[:, g0 + h:g0 + h + 1] for h in heads]
    glast = [gcum[CHUNK - 1:CHUNK, g0 + h:g0 + h + 1] for h in heads]
    gamma = [jnp.where(causal, jnp.exp(jnp.minimum(gcol[h] - gcum_t[g0 + h:g0 + h + 1, :], 0.0)), 0.0) for h in heads]
    kk = [_dot_nt(k[h], k[h]) for h in heads]
    qk = [_dot_nt(q[h], k[h]) for h in heads]
    egc = [jnp.exp(gcol[h]) for h in heads]
    a_mat = tuple(jnp.where(strict, kk[h] * gamma[h] * beta[h], 0.0) for h in heads)
    sol = _solve_unit_lower(a_mat, tuple(jnp.concatenate([v[h] * beta[h], k[h] * (beta[h] * egc[h])], axis=1)
                                         for h in heads))
    s_prev = [state[h * GDN_HEAD:(h + 1) * GDN_HEAD, :] for h in heads]
    w_s = [_dot(sol[h][:, GDN_HEAD:], s_prev[h]) for h in heads]
    q_s = [_dot(q[h] * egc[h], s_prev[h]) for h in heads]
    v_new = [sol[h][:, :GDN_HEAD] - w_s[h] for h in heads]
    a_v = [_dot(qk[h] * gamma[h], v_new[h]) for h in heads]
    k_v = [_dot_tn(k[h] * jnp.exp(glast[h] - gcol[h]), v_new[h]) for h in heads]
    outs = [q_s[h] + a_v[h] for h in heads]
    new_states = [s_prev[h] * jnp.exp(glast[h]) + k_v[h] for h in heads]
    return jnp.concatenate(new_states, axis=0), jnp.concatenate(outs, axis=1)


def _row_fns(d_model, pad_rows, loss_rows, alpha):
    def keep(ridx, v):
        return jnp.where(ridx >= pad_rows, v, 0.0)

    def both(h):
        return h, h.astype(BF16)

    def ln_in(ridx, h, g, b):
        return both(keep(ridx, _layer_norm(h, g, b)))

    def s5_tail(y_re, y_im, u, z, d, bias, glu, branch):
        v0 = jax.nn.gelu(y_re + y_im + d * u)
        return (branch(v0 * jax.nn.sigmoid(glu(v0) + bias) * _silu(z)),)

    def small_act(ridx, raw, bias, scale):
        lane = _iota(raw.shape, 1)
        sp = jax.nn.softplus(raw + bias)
        g0, b0 = SSD_HEADS, SSD_HEADS + GDN_HEADS
        out = jnp.where(lane < g0, sp, jnp.where(lane < b0, scale * sp,
                                                 jnp.where(lane < b0 + GDN_HEADS, jax.nn.sigmoid(raw), 0.0)))
        return (keep(ridx, out),)

    def conv(xs, w):
        acc = xs[0] * w[0:1, :]
        for j in range(1, CONV_K):
            acc = acc + xs[j] * w[j:j + 1, :]
        return acc

    def ssd_conv(ridx, x0, x1, x2, x3, w, b):
        return (keep(ridx, _silu(conv((x0, x1, x2, x3), w) + b)),)

    def ssd_tail(y, z, g, branch):
        v = y * _silu(z)
        return (branch(v * lax.rsqrt(jnp.mean(v * v, axis=-1, keepdims=True) + LN_EPS) * g),)

    def gdn_conv(x0, x1, x2, x3, w):
        a = _silu(conv((x0, x1, x2, x3), w))
        width = GDN_HEADS * GDN_HEAD
        parts = []
        for h in range(2 * GDN_HEADS):
            z = a[:, h * GDN_HEAD:(h + 1) * GDN_HEAD]
            z = z * lax.rsqrt(jnp.sum(z * z, axis=-1, keepdims=True) + 1e-6)
            parts.append(z * GDN_HEAD ** -0.5 if h < GDN_HEADS else z)
        parts.append(a[:, 2 * width:])
        return (jnp.concatenate(parts, axis=1),)

    def gdn_tail(o, z, g, branch):
        parts = []
        for h in range(GDN_HEADS):
            cols = slice(h * GDN_HEAD, (h + 1) * GDN_HEAD)
            oh = o[:, cols]
            oh = oh * lax.rsqrt(jnp.mean(oh * oh, axis=-1, keepdims=True) + LN_EPS) * g
            parts.append(oh * _silu(z[:, cols]))
        return (branch(jnp.concatenate(parts, axis=1)),)

    def merge_out(ridx, oa, ob, oc, gate, h, bias, g, b, w_out):
        acc = None
        for k, o in enumerate((oa, ob, oc)):
            cols = slice(k * d_model, (k + 1) * d_model)
            term = jax.nn.sigmoid(gate[:, cols] + bias[:, cols]) * o
            acc = term if acc is None else acc + term
        return both(keep(ridx, _layer_norm(alpha * h + w_out(acc), g, b)))

    def loss_rows_fn(ridx, h, tgt):
        row = 0.5 * jnp.mean(jnp.square(h - tgt), axis=-1, keepdims=True)
        row = jnp.where(ridx >= loss_rows, row, 0.0)
        lane = _iota((h.shape[0], LANES), 1)
        return (jnp.where(lane == 0, row, 0.0),)

    return dict(ln_in=ln_in, s5_tail=s5_tail, small_act=small_act, ssd_conv=ssd_conv, ssd_tail=ssd_tail,
                gdn_conv=gdn_conv, gdn_tail=gdn_tail, merge_out=merge_out, loss=loss_rows_fn)


def _pair_exchange(srcs, whole_srcs, name):
    n_arr, n_whole = len(srcs), len(whole_srcs)
    n_chip = N_DEV // 2
    base = n_arr * n_chip

    def body(*refs):
        src_refs, recv_refs = refs[:n_arr + n_whole], refs[n_arr + n_whole:2 * (n_arr + n_whole)]
        send_sems, recv_sems = refs[2 * (n_arr + n_whole):]
        x, y, c = lax.axis_index("x"), lax.axis_index("y"), lax.axis_index("c")

        def to_sibling(src, dst, k):
            return pltpu.make_async_remote_copy(src_ref=src, dst_ref=dst, send_sem=send_sems.at[k],
                                                recv_sem=recv_sems.at[k], device_id=(x, y, 1 - c),
                                                device_id_type=pl.DeviceIdType.MESH)

        copies = [to_sibling(src_refs[a].at[q, 1 - c], recv_refs[a].at[q], a * n_chip + q)
                  for a in range(n_arr) for q in range(n_chip)]
        copies += [to_sibling(src_refs[n_arr + b], recv_refs[n_arr + b], base + b) for b in range(n_whole)]
        for cp in copies:
            cp.start()
        for cp in copies:
            cp.wait()

    n_sem = base + n_whole
    return pl.pallas_call(
        body, name=name,
        in_specs=[pl.BlockSpec(memory_space=pl.ANY)] * (n_arr + n_whole),
        out_specs=[pl.BlockSpec(memory_space=pl.ANY)] * (n_arr + n_whole),
        out_shape=[jax.ShapeDtypeStruct((n_chip,) + tuple(s.shape[2:]), s.dtype) for s in srcs]
        + [jax.ShapeDtypeStruct(s.shape, s.dtype) for s in whole_srcs],
        scratch_shapes=[pltpu.SemaphoreType.DMA((n_sem,)), pltpu.SemaphoreType.DMA((n_sem,))],
    )(*srcs, *whole_srcs)


def _pair_sum(src, recv, core, name):
    n_chip, rows, cols = recv.shape
    tr = _row_tile(rows, 1024 if cols <= LANES else 512)

    def body(core_ref, a_ref, b_ref, o_ref):
        o_ref[...] = (a_ref[0].astype(F32) + b_ref[...].astype(F32)).astype(o_ref.dtype)

    blk = pl.BlockSpec((1, tr, cols), lambda q, i, core_ref: (q, i, 0))
    return pl.pallas_call(
        body, name=name,
        grid_spec=pltpu.PrefetchScalarGridSpec(
            num_scalar_prefetch=1, grid=(n_chip, rows // tr),
            in_specs=[pl.BlockSpec((1, 1, tr, cols), lambda q, i, core_ref: (q, core_ref[0], i, 0)), blk],
            out_specs=blk),
        out_shape=jax.ShapeDtypeStruct(recv.shape, recv.dtype),
        compiler_params=_params(("arbitrary", "arbitrary")),
    )(core, src, recv)


def _add2(a, b, name):
    rows = a.shape[0]
    tr = _row_tile(rows, FLAT_ROWS)
    blk = pl.BlockSpec((tr, LANES), lambda i: (i, 0))

    def body(a_ref, b_ref, o_ref):
        o_ref[...] = a_ref[...] + b_ref[...]

    return pl.pallas_call(body, name=name, grid=(rows // tr,), in_specs=[blk, blk], out_specs=blk,
                          out_shape=jax.ShapeDtypeStruct(a.shape, F32), compiler_params=_params(("arbitrary",)))(a, b)


def _add_parts(parts, name):
    n, rows, _ = parts.shape
    tr = _row_tile(rows, FLAT_ROWS)

    def body(p_ref, o_ref):
        acc = p_ref[0]
        for k in range(1, n):
            acc = acc + p_ref[k]
        o_ref[...] = acc

    return pl.pallas_call(body, name=name, grid=(rows // tr,),
                          in_specs=[pl.BlockSpec((n, tr, LANES), lambda i: (0, i, 0))],
                          out_specs=pl.BlockSpec((tr, LANES), lambda i: (i, 0)),
                          out_shape=jax.ShapeDtypeStruct((rows, LANES), F32),
                          compiler_params=_params(("arbitrary",)))(parts)


def _chip_exchange(chip_srcs, whole_srcs, name):
    n_chip_arr, n_whole = len(chip_srcs), len(whole_srcs)
    n_arr = n_chip_arr + n_whole
    n_chip = N_DEV // 2
    chip_flips = [(1, 0), (0, 1), (1, 1)]

    def body(*refs):
        src_refs, out_refs = refs[:n_arr], refs[n_arr:2 * n_arr]
        send_sems, recv_sems = refs[2 * n_arr:]
        x, y, c = lax.axis_index("x"), lax.axis_index("y"), lax.axis_index("c")
        my_chip = 2 * x + y
        copies = []
        for a in range(n_arr):
            for k, (fx, fy) in enumerate(chip_flips):
                px = 1 - x if fx else x
                py = 1 - y if fy else y
                src = src_refs[a].at[2 * px + py] if a < n_chip_arr else src_refs[a]
                copies.append(pltpu.make_async_remote_copy(
                    src_ref=src, dst_ref=out_refs[a].at[my_chip],
                    send_sem=send_sems.at[a * 3 + k], recv_sem=recv_sems.at[a * 3 + k],
                    device_id=(px, py, c), device_id_type=pl.DeviceIdType.MESH))
        for cp in copies:
            cp.start()
        for cp in copies:
            cp.wait()

    n_sem = n_arr * len(chip_flips)
    return pl.pallas_call(
        body, name=name,
        in_specs=[pl.BlockSpec(memory_space=pl.ANY)] * n_arr, out_specs=[pl.BlockSpec(memory_space=pl.ANY)] * n_arr,
        out_shape=[jax.ShapeDtypeStruct(s.shape, s.dtype) for s in chip_srcs]
        + [jax.ShapeDtypeStruct((n_chip,) + tuple(s.shape), s.dtype) for s in whole_srcs],
        scratch_shapes=[pltpu.SemaphoreType.DMA((n_sem,)), pltpu.SemaphoreType.DMA((n_sem,))],
    )(*chip_srcs, *whole_srcs)


def _gather(srcs, name):
    n_arr = len(srcs)
    n_sem = N_DEV - 1

    def body(*refs):
        src_refs, out_refs = refs[:n_arr], refs[n_arr:2 * n_arr]
        send_sems, recv_sems = refs[2 * n_arr:]
        x, y, c = lax.axis_index("x"), lax.axis_index("y"), lax.axis_index("c")
        me, sibling = (x, y, c), (x, y, 1 - c)
        chips = [(1 - x, y), (x, 1 - y), (1 - x, 1 - y)]

        def slot(a, dev):
            return out_refs[a].at[4 * dev[0] + 2 * dev[1] + dev[2]]

        def copy(a, k, block, to, own=False):
            return pltpu.make_async_remote_copy(
                src_ref=src_refs[a] if own else slot(a, block), dst_ref=slot(a, block),
                send_sem=send_sems.at[a * n_sem + k], recv_sem=recv_sems.at[a * n_sem + k],
                device_id=to, device_id_type=pl.DeviceIdType.MESH)

        arrays = range(n_arr)
        first = [copy(a, 0, me, sibling, own=True) for a in arrays]
        first += [copy(a, 1 + j, me, (*chip, c), own=True) for j, chip in enumerate(chips) for a in arrays]
        for cp in first:
            cp.start()
        passed = []
        for j, chip in enumerate(chips):
            for a in arrays:
                copy(a, 1 + j, (*chip, c), me).wait_recv()
                fwd = copy(a, 4 + j, (*chip, c), sibling)
                fwd.start()
                passed.append(fwd)
        for a in arrays:
            copy(a, 0, sibling, me).wait_recv()
        for j, chip in enumerate(chips):
            for a in arrays:
                copy(a, 4 + j, (*chip, 1 - c), me).wait_recv()
        for cp in first + passed:
            cp.wait_send()

    return pl.pallas_call(
        body, name=name,
        in_specs=[pl.BlockSpec(memory_space=pl.ANY)] * n_arr, out_specs=[pl.BlockSpec(memory_space=pl.ANY)] * n_arr,
        out_shape=[jax.ShapeDtypeStruct((N_DEV,) + tuple(s.shape), s.dtype) for s in srcs],
        scratch_shapes=[pltpu.SemaphoreType.DMA((n_arr * n_sem,)), pltpu.SemaphoreType.DMA((n_arr * n_sem,))],
    )(*srcs)


def _adamw_body(p_ref, w_ref, m_ref, v_ref, g_ref, d_ref, nm_ref, nv_ref):
    bc1 = 1.0 - ADAM_B1 ** ADAM_STEP
    bc2 = 1.0 - ADAM_B2 ** ADAM_STEP
    g = p_ref[0].astype(F32)
    for k in range(1, p_ref.shape[0]):
        g = g + p_ref[k].astype(F32)
    nm = ADAM_B1 * m_ref[...] + (1.0 - ADAM_B1) * g
    nv = ADAM_B2 * v_ref[...] + (1.0 - ADAM_B2) * jnp.square(g)
    m_hat = nm / bc1
    v_hat = nv / bc2
    g_ref[...] = g
    d_ref[...] = -ADAM_LR * (m_hat / (jnp.sqrt(v_hat) + ADAM_EPS) + ADAM_WD * w_ref[...])
    nm_ref[...] = nm
    nv_ref[...] = nv


def _adamw_rows(parts, w, m, v, name):
    rows, cols = w.shape
    tr = _row_tile(rows, max(128, (1 << 18) // cols))
    blk = pl.BlockSpec((tr, cols), lambda i: (i, 0))
    return pl.pallas_call(
        functools.partial(_adamw_body), name=name, grid=(rows // tr,),
        in_specs=[pl.BlockSpec((parts.shape[0], tr, cols), lambda i: (0, i, 0)), blk, blk, blk],
        out_specs=[blk] * 4, out_shape=[jax.ShapeDtypeStruct(w.shape, F32)] * 4,
        compiler_params=_params(("arbitrary",)),
    )(parts, w, m, v)


def _pad_flat(vec, rows):
    return jnp.pad(vec, (0, rows * LANES - vec.shape[0])).reshape(rows, LANES)


def _rows_for(n):
    return -(-n // (FLAT_ROWS * LANES)) * FLAT_ROWS


def _join_shards(rows, local_shape, dim):
    parts = jnp.moveaxis(rows.reshape((N_DEV,) + tuple(local_shape)), 0, dim)
    shp = tuple(local_shape)
    return parts.reshape(shp[:dim] + (N_DEV * shp[dim],) + shp[dim + 1:])


def _s5_tables(a_re, a_im, log_step, b_re, b_im, c_re, c_im):
    lam_re = jnp.minimum(a_re, -1e-4)
    lam_im = a_im
    step = jnp.exp(log_step)[:, None]
    mag = jnp.exp(lam_re * step)
    abar_re, abar_im = mag * jnp.cos(lam_im * step), mag * jnp.sin(lam_im * step)
    den = lam_re * lam_re + lam_im * lam_im
    nr, ni = abar_re - 1.0, abar_im
    coef_re = (nr * lam_re + ni * lam_im) / den
    coef_im = (ni * lam_re - nr * lam_im) / den
    bbar_re = coef_re[..., None] * b_re - coef_im[..., None] * b_im
    bbar_im = coef_re[..., None] * b_im + coef_im[..., None] * b_re
    groups = a_re.shape[0]
    nblk = groups // S5_BLOCK_GROUPS
    eye = jnp.eye(S5_BLOCK_GROUPS, dtype=F32)

    def in_blocks(bb):
        t = jnp.swapaxes(bb, 1, 2).reshape(nblk, S5_BLOCK_GROUPS, S5_GROUP, S5_STATE)
        blk = jnp.einsum('ab,jacp->jacbp', eye, t)
        return blk.reshape(nblk, S5_BLOCK_GROUPS * S5_GROUP, S5_BLOCK_GROUPS * S5_STATE)

    def out_blocks(cc):
        t = jnp.swapaxes(cc, 1, 2).reshape(nblk, S5_BLOCK_GROUPS, S5_STATE, S5_GROUP)
        blk = jnp.einsum('ab,japc->japbc', eye, t)
        return blk.reshape(nblk, S5_BLOCK_GROUPS * S5_STATE, S5_BLOCK_GROUPS * S5_GROUP)

    rows = groups * S5_STATE // LANES
    return dict(b_re=in_blocks(bbar_re), b_im=in_blocks(bbar_im), c_re=out_blocks(c_re), c_im=out_blocks(-c_im),
                a_re=abar_re.reshape(rows, LANES), a_im=abar_im.reshape(rows, LANES))


def _in_widths(d_model):
    return [BRANCH, BRANCH, SSD_HEADS * SSD_HEAD + 2 * SSD_GROUPS * SSD_STATE, SSD_HEADS, BRANCH,
            3 * BRANCH, GDN_HEADS, GDN_HEADS, BRANCH, 3 * d_model]


def _local_loss(w, mats, x, target):
    n_meta, d_model = w['meta'].shape
    depth = len(mats['in'])
    seq = x.shape[0]
    pad_rows = CHUNK - n_meta
    first = pad_rows + n_meta
    t_all = first + seq
    alpha = (2 * depth) ** 0.25
    fns = _row_fns(d_model, pad_rows, first, alpha)
    row = lambda nm, key, n_row, n_par, widths, cap, ridx=False: _make_rowwise(nm, fns[key], n_row, n_par, widths, cap, ridx)

    h = jnp.concatenate([jnp.zeros((pad_rows, d_model), F32), w['meta'], x], axis=0)
    h, h16 = _make_rowwise("ln_in", fns['ln_in'], 1, 2, [d_model, d_model], 416, True, out_dtypes=[F32, BF16])(
        h, w['ln_in_g'][None], w['ln_in_b'][None])

    n_small = SSD_HEADS + 2 * GDN_HEADS

    def small_cols(ps):
        return jnp.pad(jnp.concatenate([ps[3], ps[6], ps[7]], axis=1), ((0, 0), (0, LANES - n_small)))

    for l in range(depth):
        pw, pc = mats['in'][l], w['c_in'][l]
        s5_u = _mm("in_s5u", h, h16, pw[0], pc[0])
        s5_z = _mm("in_s5z", h, h16, pw[1], pc[1])
        ssd_xbc = _mm("in_ssdx", h, h16, pw[2], pc[2])
        ssd_z = _mm("in_ssdz", h, h16, pw[4], pc[4])
        gdn_qkv = _mm("in_gdnq", h, h16, pw[5], pc[5])
        gdn_z = _mm("in_gdnz", h, h16, pw[8], pc[8])
        gate = _mm("in_gate", h, h16, pw[9], pc[9])
        small = _mm("in_small", h, h16, small_cols(pw), small_cols(pc))

        zeros_tail = jnp.zeros((LANES - n_small,), F32)
        bias = jnp.concatenate([w['ssd_dt_bias'][l], w['gdn_dt_bias'][l], jnp.zeros((GDN_HEADS,), F32), zeros_tail])[None]
        scale = jnp.concatenate([jnp.ones((SSD_HEADS,), F32), -jnp.exp(w['gdn_a_log'][l]),
                                 jnp.zeros((GDN_HEADS,), F32), zeros_tail])[None]
        (sm,) = row("small_act", 'small_act', 1, 2, [LANES], 832, True)(small, bias, scale)

        tb = _s5_tables(w['s5_a_re'][l], w['s5_a_im'][l], w['s5_log_step'][l], w['s5_b_re'][l], w['s5_b_im'][l],
                        w['s5_c_re'][l], w['s5_c_im'][l])
        srows = tb['a_re'].shape[0]
        bu_re = _make_gmm("s5_bre")(s5_u, tb['b_re']).reshape(t_all, srows, LANES)
        bu_im = _make_gmm("s5_bim")(s5_u, tb['b_im']).reshape(t_all, srows, LANES)
        s_re, s_im = _make_s5_scan("s5_scan")(bu_re, bu_im, tb['a_re'], tb['a_im'])
        y_re = _make_gmm("s5_cre")(s_re.reshape(t_all, srows * LANES), tb['c_re'])
        y_im = _make_gmm("s5_cim")(s_im.reshape(t_all, srows * LANES), tb['c_im'])
        (out_a,) = _make_rowwise("s5_tail", fns['s5_tail'], 4, 2, [d_model], 208, n_wt=2)(
            y_re, y_im, s5_u, s5_z, w['s5_d'][l][None], w['s5_b_glu'][l][None],
            mats['glu'][l], mats['branch'][l, 0], w['c_glu'][l], w['c_branch'][l, 0])

        xbc = _make_conv_rowwise("ssd_conv", fns['ssd_conv'], 2, ssd_xbc.shape[1], 416, True)(
            ssd_xbc, w['ssd_conv_w'][l], w['ssd_conv_b'][l][None])
        a_c = jnp.pad(-jnp.exp(w['ssd_a_log'][l]), (0, LANES - SSD_HEADS))[None]
        d_exp = jnp.repeat(w['ssd_d'][l], SSD_HEAD)[None]
        (y_ssd,) = _make_chunk_scan("ssd_scan", _ssd_chunk, (SSD_HEADS // 2 * LANES, SSD_STATE), 2, 2, [BRANCH])(
            xbc, sm, a_c, d_exp)
        (out_b,) = _make_rowwise("ssd_tail", fns['ssd_tail'], 2, 1, [d_model], 416, n_wt=1)(
            y_ssd, ssd_z, w['ssd_norm_g'][l][None], mats['branch'][l, 1], w['c_branch'][l, 1])

        qkv = _make_conv_rowwise("gdn_conv", fns['gdn_conv'], 1, 3 * BRANCH, 208)(gdn_qkv, w['gdn_conv_w'][l])
        (o_gdn,) = _make_chunk_scan("gdn_scan", _gdn_chunk, (GDN_HEADS * GDN_HEAD, GDN_HEAD), 2, 0, [BRANCH])(qkv, sm)
        (out_c,) = _make_rowwise("gdn_tail", fns['gdn_tail'], 2, 1, [d_model], 416, n_wt=1)(
            o_gdn, gdn_z, w['gdn_norm_g'][l][None], mats['branch'][l, 2], w['c_branch'][l, 2])

        h, h16 = _make_rowwise("merge_out", fns['merge_out'], 5, 3, [d_model, d_model], 208, True, n_wt=1,
                               out_dtypes=[F32, BF16])(
            out_a, out_b, out_c, gate, h, w['b_gate'][l].reshape(1, 3 * d_model), w['ln_g'][l][None],
            w['ln_b'][l][None], mats['out'][l], w['c_out'][l])

    tgt = jnp.concatenate([jnp.zeros((first, d_model), F32), target], axis=0)
    (rows_loss,) = row("loss", 'loss', 2, 0, [LANES], 416, True)(h, tgt)
    return jnp.sum(rows_loss)


def _in_overlaps(d_model, n_loc):
    offs = [0]
    for wd in _in_widths(d_model):
        offs.append(offs[-1] + wd)
    out = []
    for i in range(len(offs) - 1):
        c0, c1 = offs[i], offs[i + 1]
        segs = []
        for k in range(N_DEV):
            g0, g1 = max(c0, k * n_loc), min(c1, (k + 1) * n_loc)
            if g0 < g1:
                segs.append((k, g0 - k * n_loc, g1 - k * n_loc, g0 - c0))
        out.append(segs)
    return out


STACKED = ['w_in', 's5_w_glu', 'w_branch', 'w_out']


def _shard_blocks(full, dim):
    shp = full.shape
    parts = full.reshape(shp[:dim] + (N_DEV, shp[dim] // N_DEV) + shp[dim + 1:])
    return jnp.moveaxis(parts, dim, 0)


def _join_blocks(blocks, dim):
    shp = blocks.shape[1:]
    return jnp.moveaxis(blocks, 0, dim).reshape(shp[:dim] + (N_DEV * shp[dim],) + shp[dim + 1:])


def _step(x, target, w_loc, m_loc, v_loc):
    small = [n for n in WEIGHTS if n in SHARD_DIM and n not in STACKED]
    repl = [n for n in WEIGHTS if n not in SHARD_DIM]
    size = lambda names: sum(int(w_loc[n].size) for n in names)
    depth, d_model, n_loc = w_loc['w_in'].shape
    overlaps = _in_overlaps(d_model, n_loc)
    n_chip = N_DEV // 2
    me = 4 * lax.axis_index("x") + 2 * lax.axis_index("y") + lax.axis_index("c")
    my_chip = 2 * lax.axis_index("x") + lax.axis_index("y")
    core = lax.axis_index("c").astype(jnp.int32).reshape(1)
    put = lambda buf, blk, idx: lax.dynamic_update_index_in_dim(buf, blk, idx, 0)
    take = lambda buf, idx: lax.dynamic_index_in_dim(buf, idx, 0, keepdims=False)

    rows_small = _rows_for(size(small))
    small_flat = _pad_flat(jnp.concatenate([w_loc[n].reshape(-1) for n in small]), rows_small)
    own_blocks = [w_loc[n].astype(BF16) for n in STACKED] + [small_flat]
    gathered = [put(got, blk, me) for got, blk in zip(_gather(own_blocks, "gather_weights"), own_blocks)]
    g_in = gathered[0]
    full = {n: _join_blocks(g, SHARD_DIM[n]) for n, g in zip(STACKED[1:], gathered[1:])}
    buf, off = gathered[-1].reshape(N_DEV, -1), 0
    for n in small:
        sz = int(w_loc[n].size)
        full[n] = _join_shards(buf[:, off:off + sz], w_loc[n].shape, SHARD_DIM[n])
        off += sz
    mats = dict(glu=full['s5_w_glu'], branch=full['w_branch'], out=full['w_out'], **{
        'in': [[jnp.concatenate([g_in[k, l, :, lo:hi] for k, lo, hi, _ in segs], axis=1) for segs in overlaps]
               for l in range(depth)]})
    w_diff = {n: w_loc[n] for n in repl}
    w_diff.update({n: full[n] for n in small})
    w_diff['c_in'] = [[jnp.zeros((d_model, wd), F32) for wd in _in_widths(d_model)] for _ in range(depth)]
    w_diff['c_glu'] = jnp.zeros(full['s5_w_glu'].shape, F32)
    w_diff['c_branch'] = jnp.zeros(full['w_branch'].shape, F32)
    w_diff['c_out'] = jnp.zeros(full['w_out'].shape, F32)

    loss, (g_w, g_x) = jax.value_and_grad(_local_loss, argnums=(0, 2))(w_diff, mats, x[0], target[0])
    g_w['s5_w_glu'], g_w['w_branch'], g_w['w_out'] = g_w['c_glu'], g_w['c_branch'], g_w['c_out']

    blocks = {'w_in': jnp.stack([jnp.stack([
        jnp.concatenate([g_w['c_in'][l][i][:, plo:plo + hi - lo]
                         for i, segs in enumerate(overlaps) for (kk, lo, hi, plo) in segs if kk == k], axis=1)
        for l in range(depth)]) for k in range(N_DEV)])}
    for n in STACKED[1:]:
        blocks[n] = _shard_blocks(g_w[n], SHARD_DIM[n])
    cols = {n: w_loc[n].shape[-1] for n in STACKED}
    send = [blocks[n].astype(BF16).reshape(n_chip, 2, -1, cols[n]) for n in STACKED]
    red_names = small + repl
    n_red = sum(int(g_w[n].size) for n in red_names) + 1
    red = _pad_flat(jnp.concatenate([g_w[n].reshape(-1) for n in red_names] + [loss.reshape(1)]), _rows_for(n_red))
    *recv, red_sib = _pair_exchange(send, [red], "pair_exchange")
    sums = [_pair_sum(s, r, core, "pair_sum_" + n) for s, r, n in zip(send, recv, STACKED)]
    red_chip = _add2(red, red_sib, "pair_sum_rest")
    *parts, red_parts = _chip_exchange(sums, [red_chip], "exchange_grads")
    parts = [put(p, take(s, my_chip), my_chip) for p, s in zip(parts, sums)]
    red_total = _add_parts(put(red_parts, red_chip, my_chip), "reduce_rest").reshape(-1)

    outs = {}
    for n, p in zip(STACKED, parts):
        as_rows = lambda a: a.reshape(-1, cols[n])
        res = _adamw_rows(p, as_rows(w_loc[n]), as_rows(m_loc[n]), as_rows(v_loc[n]), "adamw_" + n)
        outs[n] = [r.reshape(w_loc[n].shape) for r in res]
    g_red, off = {}, 0
    for n in red_names:
        sz = int(g_w[n].size)
        g_red[n] = red_total[off:off + sz].reshape(g_w[n].shape)
        off += sz
    loss_total = red_total[off]
    for n in small:
        g_red[n] = take(_shard_blocks(g_red[n], SHARD_DIM[n]), me)
    rows_rest = _rows_for(size(red_names))
    flat = lambda src: _pad_flat(jnp.concatenate([src[n].reshape(-1) for n in red_names]), rows_rest)
    res = _adamw_rows(flat(g_red)[None], flat(w_loc), flat(m_loc), flat(v_loc), "adamw_rest")
    off = 0
    for n in red_names:
        sz = int(w_loc[n].size)
        outs[n] = [r.reshape(-1)[off:off + sz].reshape(w_loc[n].shape) for r in res]
        off += sz
    return (loss_total, g_x[None], *[outs[n][k] for k in range(4) for n in WEIGHTS])


def kernel(x, meta, ln_in_g, ln_in_b, w_in, s5_a_re, s5_a_im, s5_log_step, s5_b_re, s5_b_im, s5_c_re, s5_c_im, s5_d, s5_w_glu, s5_b_glu, ssd_conv_w, ssd_conv_b, ssd_dt_bias, ssd_a_log, ssd_d, ssd_norm_g, gdn_conv_w, gdn_dt_bias, gdn_a_log, gdn_norm_g, w_branch, b_gate, w_out, ln_g, ln_b, loss_target, m_meta, m_ln_in_g, m_ln_in_b, m_w_in, m_s5_a_re, m_s5_a_im, m_s5_log_step, m_s5_b_re, m_s5_b_im, m_s5_c_re, m_s5_c_im, m_s5_d, m_s5_w_glu, m_s5_b_glu, m_ssd_conv_w, m_ssd_conv_b, m_ssd_dt_bias, m_ssd_a_log, m_ssd_d, m_ssd_norm_g, m_gdn_conv_w, m_gdn_dt_bias, m_gdn_a_log, m_gdn_norm_g, m_w_branch, m_b_gate, m_w_out, m_ln_g, m_ln_b, v_meta, v_ln_in_g, v_ln_in_b, v_w_in, v_s5_a_re, v_s5_a_im, v_s5_log_step, v_s5_b_re, v_s5_b_im, v_s5_c_re, v_s5_c_im, v_s5_d, v_s5_w_glu, v_s5_b_glu, v_ssd_conv_w, v_ssd_conv_b, v_ssd_dt_bias, v_ssd_a_log, v_ssd_d, v_ssd_norm_g, v_gdn_conv_w, v_gdn_dt_bias, v_gdn_a_log, v_gdn_norm_g, v_w_branch, v_b_gate, v_w_out, v_ln_g, v_ln_b):
    w_loc = dict(zip(WEIGHTS, (meta, ln_in_g, ln_in_b, w_in, s5_a_re, s5_a_im, s5_log_step, s5_b_re, s5_b_im, s5_c_re, s5_c_im, s5_d, s5_w_glu, s5_b_glu, ssd_conv_w, ssd_conv_b, ssd_dt_bias, ssd_a_log, ssd_d, ssd_norm_g, gdn_conv_w, gdn_dt_bias, gdn_a_log, gdn_norm_g, w_branch, b_gate, w_out, ln_g, ln_b)))
    m_loc = dict(zip(WEIGHTS, (m_meta, m_ln_in_g, m_ln_in_b, m_w_in, m_s5_a_re, m_s5_a_im, m_s5_log_step, m_s5_b_re, m_s5_b_im, m_s5_c_re, m_s5_c_im, m_s5_d, m_s5_w_glu, m_s5_b_glu, m_ssd_conv_w, m_ssd_conv_b, m_ssd_dt_bias, m_ssd_a_log, m_ssd_d, m_ssd_norm_g, m_gdn_conv_w, m_gdn_dt_bias, m_gdn_a_log, m_gdn_norm_g, m_w_branch, m_b_gate, m_w_out, m_ln_g, m_ln_b)))
    v_loc = dict(zip(WEIGHTS, (v_meta, v_ln_in_g, v_ln_in_b, v_w_in, v_s5_a_re, v_s5_a_im, v_s5_log_step, v_s5_b_re, v_s5_b_im, v_s5_c_re, v_s5_c_im, v_s5_d, v_s5_w_glu, v_s5_b_glu, v_ssd_conv_w, v_ssd_conv_b, v_ssd_dt_bias, v_ssd_a_log, v_ssd_d, v_ssd_norm_g, v_gdn_conv_w, v_gdn_dt_bias, v_gdn_a_log, v_gdn_norm_g, v_w_branch, v_b_gate, v_w_out, v_ln_g, v_ln_b)))
    return _step(x, loss_target, w_loc, m_loc, v_loc)
```

```python
import functools
import math

import jax
import jax.numpy as jnp
from jax import lax
from jax.experimental import pallas as pl
from jax.experimental.pallas import tpu as pltpu

F32 = jnp.float32
BF16 = jnp.bfloat16

N_DEV = 8
LANES = 128
SUBLANES = 8
VMEM_LIMIT = 56 * 1024 * 1024
FLAT_ROWS = 1024

CHUNK = 64
CONV_K = 4
S5_GROUP = 16
S5_STATE = 64
S5_BLOCK_GROUPS = 8
SSD_HEAD = 64
SSD_HEADS = 12
SSD_GROUPS = 2
SSD_STATE = 128
GDN_HEAD = 128
GDN_HEADS = 6
BRANCH = 768
LN_EPS = 1e-5

ADAM_LR = 0.001
ADAM_B1 = 0.9
ADAM_B2 = 0.999
ADAM_EPS = 1e-08
ADAM_WD = 0.01
ADAM_STEP = 10

WEIGHTS = ['meta', 'ln_in_g', 'ln_in_b', 'w_in', 's5_a_re', 's5_a_im', 's5_log_step', 's5_b_re', 's5_b_im',
           's5_c_re', 's5_c_im', 's5_d', 's5_w_glu', 's5_b_glu', 'ssd_conv_w', 'ssd_conv_b', 'ssd_dt_bias',
           'ssd_a_log', 'ssd_d', 'ssd_norm_g', 'gdn_conv_w', 'gdn_dt_bias', 'gdn_a_log', 'gdn_norm_g',
           'w_branch', 'b_gate', 'w_out', 'ln_g', 'ln_b']
SHARD_DIM = {'meta': 1, 'w_in': 2, 's5_w_glu': 1, 'ssd_conv_w': 2, 'gdn_conv_w': 2, 'w_branch': 3, 'b_gate': 2,
             'w_out': 1}


def _params(sem):
    return pltpu.CompilerParams(dimension_semantics=sem, vmem_limit_bytes=VMEM_LIMIT)


def _row_tile(m, cap):
    best = None
    for t in range(SUBLANES, min(m, cap) + 1, SUBLANES):
        if m % t == 0:
            best = t
    return best if best is not None else m


def _col_tile(n, cap):
    best = None
    for t in range(LANES, min(n, cap) + 1, LANES):
        if n % t == 0:
            best = t
    return best if best is not None else n


def _any_tile(m, cap):
    best = 1
    for t in range(1, min(m, cap) + 1):
        if m % t == 0:
            best = t
    return best


def _mm_fwd(a, b, name):
    m, _ = a.shape
    g, k, n = b.shape
    tm, tn = _row_tile(m, 832), _col_tile(n, 1024)
    nj = n // tn

    def body(a_ref, b_ref, o_ref):
        o_ref[...] = jnp.dot(a_ref[...].astype(BF16), b_ref[0].astype(BF16), preferred_element_type=F32)

    return pl.pallas_call(
        body, name=name, grid=(g, m // tm, nj),
        in_specs=[pl.BlockSpec((tm, k), lambda gi, i, j: (i, gi)),
                  pl.BlockSpec((1, k, tn), lambda gi, i, j: (gi, 0, j))],
        out_specs=pl.BlockSpec((tm, tn), lambda gi, i, j: (i, gi * nj + j)),
        out_shape=jax.ShapeDtypeStruct((m, g * n), F32),
        compiler_params=_params(("arbitrary", "arbitrary", "arbitrary")),
    )(a, b)


def _mm_da(ct, b, name):
    m, _ = ct.shape
    g, k, n = b.shape
    tm, tk = _row_tile(m, 832 if n <= 1536 else 416), _col_tile(k, 1024)
    nk = k // tk

    def body(c_ref, b_ref, o_ref):
        o_ref[...] = lax.dot_general(c_ref[...].astype(BF16), b_ref[0].astype(BF16), (((1,), (1,)), ((), ())),
                                     preferred_element_type=F32)

    return pl.pallas_call(
        body, name=name, grid=(g, m // tm, nk),
        in_specs=[pl.BlockSpec((tm, n), lambda gi, i, j: (i, gi)),
                  pl.BlockSpec((1, tk, n), lambda gi, i, j: (gi, j, 0))],
        out_specs=pl.BlockSpec((tm, tk), lambda gi, i, j: (i, gi * nk + j)),
        out_shape=jax.ShapeDtypeStruct((m, g * k), F32),
        compiler_params=_params(("arbitrary", "arbitrary", "arbitrary")),
    )(ct, b)


def _mm_db(a, ct, g, k, n, name):
    m = a.shape[0]
    tm, tk, tn = _row_tile(m, 832), _col_tile(k, 1024), _col_tile(n, 1280)
    nk, nn = k // tk, n // tn

    def body(a_ref, c_ref, o_ref):
        @pl.when(pl.program_id(3) == 0)
        def _():
            o_ref[...] = jnp.zeros_like(o_ref)

        o_ref[0] += lax.dot_general(a_ref[...].astype(BF16), c_ref[...].astype(BF16), (((0,), (0,)), ((), ())),
                                    preferred_element_type=F32)

    return pl.pallas_call(
        body, name=name, grid=(g, nk, nn, m // tm),
        in_specs=[pl.BlockSpec((tm, tk), lambda gi, i, j, r: (r, gi * nk + i)),
                  pl.BlockSpec((tm, tn), lambda gi, i, j, r: (r, gi * nn + j))],
        out_specs=pl.BlockSpec((1, tk, tn), lambda gi, i, j, r: (gi, i, j)),
        out_shape=jax.ShapeDtypeStruct((g, k, n), F32),
        compiler_params=_params(("arbitrary", "arbitrary", "arbitrary", "arbitrary")),
    )(a, ct)


def _make_gmm(name):
    @jax.custom_vjp
    def gmm(a, b):
        return _mm_fwd(a, b, name + "_fwd")

    def fwd(a, b):
        return _mm_fwd(a, b, name + "_fwd"), (a, b)

    def bwd(res, ct):
        a, b = res
        g, k, n = b.shape
        return _mm_da(ct, b, name + "_da"), _mm_db(a, ct, g, k, n, name + "_db")

    gmm.defvjp(fwd, bwd)
    return gmm


def _mm_da_sum(cts, ws, name):
    m, k = cts[0].shape[0], ws[0].shape[0]
    tm = _row_tile(m, 208)
    n_p = len(cts)

    def body(*refs):
        acc = None
        for c_ref, w_ref in zip(refs[:n_p], refs[n_p:2 * n_p]):
            term = lax.dot_general(c_ref[...].astype(BF16), w_ref[...].astype(BF16), (((1,), (1,)), ((), ())),
                                   preferred_element_type=F32)
            acc = term if acc is None else acc + term
        refs[2 * n_p][...] = acc

    return pl.pallas_call(
        body, name=name, grid=(m // tm,),
        in_specs=[pl.BlockSpec((tm, c.shape[1]), lambda i: (i, 0)) for c in cts]
        + [pl.BlockSpec(w.shape, lambda i: (0, 0), pipeline_mode=pl.Buffered(1)) for w in ws],
        out_specs=pl.BlockSpec((tm, k), lambda i: (i, 0)),
        out_shape=jax.ShapeDtypeStruct((m, k), F32),
        compiler_params=_params(("arbitrary",)),
    )(*cts, *ws)


def _in_proj(names, a, a16, ws, carriers):
    def products(a16, ws):
        return tuple(_mm_fwd(a16, w[None], nm + "_fwd") for nm, w in zip(names, ws))

    @jax.custom_vjp
    def proj(a, a16, ws, carriers):
        return products(a16, ws)

    def fwd(a, a16, ws, carriers):
        return products(a16, ws), (a16, ws)

    def bwd(res, cts):
        a16, ws = res
        d_ws = tuple(_mm_db(a16, ct, 1, w.shape[0], w.shape[1], nm + "_db")[0] for nm, w, ct in zip(names, ws, cts))
        return (_mm_da_sum(list(cts), list(ws), "in_da"), jnp.zeros_like(a16),
                tuple(jnp.zeros_like(w) for w in ws), d_ws)

    proj.defvjp(fwd, bwd)
    return proj(a, a16, tuple(ws), tuple(carriers))


@jax.custom_vjp
def _wdot(x, w, carrier):
    return _dot(x, w)


def _wdot_fwd(x, w, carrier):
    return _dot(x, w), (x, w)


def _wdot_bwd(res, ct):
    x, w = res
    return _dot_nt(ct, w), jnp.zeros_like(w), _dot_tn(x, ct)


_wdot.defvjp(_wdot_fwd, _wdot_bwd)


def _make_rowwise(name, fn, n_row, n_par, out_widths, tm_cap, use_ridx=False, n_wt=0, out_dtypes=None):
    n_out = len(out_widths)
    out_dtypes = out_dtypes or [F32] * n_out
    n_in = n_row + n_par + n_wt

    def bind(tm):
        if not use_ridx:
            return fn
        ridx = pl.program_id(0) * tm + lax.broadcasted_iota(jnp.int32, (tm, 1), 0)
        return functools.partial(fn, ridx)

    def specs(args, tm):
        rows = [pl.BlockSpec((tm, a.shape[1]), lambda i: (i, 0)) for a in args[:n_row]]
        pars = [pl.BlockSpec(a.shape, lambda i: (0, 0)) for a in args[n_row:n_in]]
        return rows, pars

    def fwd_call(*args):
        t = args[0].shape[0]
        tm = _row_tile(t, tm_cap)
        rows, pars = specs(args, tm)

        def body(*refs):
            vals = [r[...] for r in refs[:n_row + n_par]]
            mats = [functools.partial(lambda x, w: _dot(x, w), w=r[...]) for r in refs[n_row + n_par:n_in]]
            res = bind(tm)(*vals, *mats)
            for o_ref, r in zip(refs[n_in:], res):
                o_ref[...] = r

        outs = pl.pallas_call(
            body, name=name + "_fwd", grid=(t // tm,), in_specs=rows + pars,
            out_specs=[pl.BlockSpec((tm, w), lambda i: (i, 0)) for w in out_widths],
            out_shape=[jax.ShapeDtypeStruct((t, w), dt) for w, dt in zip(out_widths, out_dtypes)],
            compiler_params=_params(("arbitrary",)),
        )(*args)
        return tuple(outs)

    def bwd_call(args, cts):
        t = args[0].shape[0]
        tm = _row_tile(t, tm_cap)
        rows, pars = specs(args, tm)
        ct_specs = [pl.BlockSpec((tm, w), lambda i: (i, 0)) for w in out_widths]
        n_diff = n_row + n_par

        def body(*refs):
            vals = [r[...] for r in refs[:n_diff]]
            wts = [r[...] for r in refs[n_diff:n_in]]
            ct_vals = tuple(r[...] for r in refs[n_in:n_in + n_out])
            d_refs = refs[n_in + n_out:]
            f = bind(tm)

            def g(*a):
                mats = [functools.partial(lambda x, w, c: _wdot(x, w, c), w=w, c=c)
                        for w, c in zip(wts, a[n_diff:])]
                return tuple(f(*a[:n_diff], *mats))

            _, vjp = jax.vjp(g, *vals, *[jnp.zeros(w.shape, F32) for w in wts])
            grads = vjp(ct_vals)
            for i in range(n_row):
                d_refs[i][...] = grads[i]
            if n_par + n_wt:
                @pl.when(pl.program_id(0) == 0)
                def _():
                    for r in d_refs[n_row:]:
                        r[...] = jnp.zeros_like(r)

                for r, gr in zip(d_refs[n_row:], grads[n_row:]):
                    r[...] += gr

        outs = pl.pallas_call(
            body, name=name + "_bwd", grid=(t // tm,), in_specs=rows + pars + ct_specs,
            out_specs=rows + pars,
            out_shape=[jax.ShapeDtypeStruct(a.shape, F32) for a in args],
            compiler_params=_params(("arbitrary",)),
        )(*args, *cts)
        return tuple(outs)

    @jax.custom_vjp
    def op(*args):
        return fwd_call(*args[:n_in])

    def fwd(*args):
        return fwd_call(*args[:n_in]), args[:n_in]

    def bwd(args, cts):
        grads = bwd_call(args, cts)
        return grads[:n_row + n_par] + tuple(jnp.zeros_like(a) for a in args[n_row + n_par:]) + grads[n_row + n_par:]

    op.defvjp(fwd, bwd)
    return op


HALO = SUBLANES


def _make_conv_rowwise(name, fn, n_par, out_width, tm_cap, use_ridx=False):
    def bind(ridx):
        return functools.partial(fn, ridx) if use_ridx else fn

    def stage(x_ref, halo_ref, xs, first):
        xs[0:HALO, :] = jnp.where(first, 0.0, halo_ref[...])
        xs[HALO:, :] = x_ref[...]

    def taps(xs, tm):
        return [xs[pl.ds(HALO - (CONV_K - 1) + j, tm), :] for j in range(CONV_K)]

    def fwd_call(x, *pars):
        t, wd = x.shape
        tm = _row_tile(t, tm_cap)
        per = tm // HALO

        def body(*refs):
            x_ref, halo_ref = refs[:2]
            par_refs, o_ref, xs = refs[2:2 + n_par], refs[2 + n_par], refs[-1]
            i = pl.program_id(0)
            stage(x_ref, halo_ref, xs, i == 0)
            ridx = i * tm + lax.broadcasted_iota(jnp.int32, (tm, 1), 0)
            (o_ref[...],) = bind(ridx)(*taps(xs, tm), *[r[...] for r in par_refs])

        return pl.pallas_call(
            body, name=name + "_fwd", grid=(t // tm,),
            in_specs=[pl.BlockSpec((tm, wd), lambda i: (i, 0)),
                      pl.BlockSpec((HALO, wd), lambda i: (jnp.maximum(i * per - 1, 0), 0))]
            + [pl.BlockSpec(p.shape, lambda i: (0, 0)) for p in pars],
            out_specs=pl.BlockSpec((tm, out_width), lambda i: (i, 0)),
            out_shape=jax.ShapeDtypeStruct((t, out_width), F32),
            scratch_shapes=[pltpu.VMEM((tm + HALO, wd), F32)],
            compiler_params=_params(("arbitrary",)),
        )(x, x, *pars)

    def bwd_call(x, pars, ct):
        t, wd = x.shape
        tm = _row_tile(t, tm_cap)
        per = tm // HALO
        nb = t // tm

        def body(*refs):
            x_ref, halo_ref = refs[:2]
            par_refs, ct_ref = refs[2:2 + n_par], refs[2 + n_par]
            dx_ref, dpar_refs = refs[3 + n_par], refs[4 + n_par:4 + 2 * n_par]
            xs, ds, carry = refs[-3:]
            step = pl.program_id(0)
            blk = nb - 1 - step

            @pl.when(step == 0)
            def _():
                carry[...] = jnp.zeros_like(carry)
                for r in dpar_refs:
                    r[...] = jnp.zeros_like(r)

            stage(x_ref, halo_ref, xs, blk == 0)
            ridx = blk * tm + lax.broadcasted_iota(jnp.int32, (tm, 1), 0)
            f = bind(ridx)
            _, vjp = jax.vjp(lambda *a: tuple(f(*a)), *taps(xs, tm), *[r[...] for r in par_refs])
            grads = vjp((ct_ref[...],))
            ds[...] = jnp.zeros_like(ds)
            for j in range(CONV_K):
                ds[pl.ds(HALO - (CONV_K - 1) + j, tm), :] += grads[j]
            ds[pl.ds(tm, HALO), :] += carry[...]
            dx_ref[...] = ds[HALO:, :]
            carry[...] = ds[0:HALO, :]
            for r, g in zip(dpar_refs, grads[CONV_K:]):
                r[...] += g

        rev = lambda i: (nb - 1 - i, 0)
        outs = pl.pallas_call(
            body, name=name + "_bwd", grid=(nb,),
            in_specs=[pl.BlockSpec((tm, wd), rev),
                      pl.BlockSpec((HALO, wd), lambda i: (jnp.maximum((nb - 1 - i) * per - 1, 0), 0))]
            + [pl.BlockSpec(p.shape, lambda i: (0, 0)) for p in pars]
            + [pl.BlockSpec((tm, out_width), rev)],
            out_specs=[pl.BlockSpec((tm, wd), rev)] + [pl.BlockSpec(p.shape, lambda i: (0, 0)) for p in pars],
            out_shape=[jax.ShapeDtypeStruct(x.shape, F32)] + [jax.ShapeDtypeStruct(p.shape, F32) for p in pars],
            scratch_shapes=[pltpu.VMEM((tm + HALO, wd), F32), pltpu.VMEM((tm + HALO, wd), F32),
                            pltpu.VMEM((HALO, wd), F32)],
            compiler_params=_params(("arbitrary",)),
        )(x, x, *pars, ct)
        return tuple(outs)

    @jax.custom_vjp
    def op(x, *pars):
        return fwd_call(x, *pars)

    def fwd(x, *pars):
        return fwd_call(x, *pars), (x, pars)

    def bwd(res, ct):
        x, pars = res
        return bwd_call(x, pars, ct)

    op.defvjp(fwd, bwd)
    return op


def _make_chunk_scan(name, fn, state_shape, n_seq, n_par, out_widths):
    n_out = len(out_widths)
    zeros_idx = (0,) * len(state_shape)

    def fwd_call(*args):
        t = args[0].shape[0]
        nc = t // CHUNK
        seq_specs = [pl.BlockSpec((CHUNK, a.shape[1]), lambda c: (c, 0)) for a in args[:n_seq]]
        par_specs = [pl.BlockSpec(a.shape, lambda c: (0, 0)) for a in args[n_seq:]]

        def body(*refs):
            ins = refs[:n_seq + n_par]
            out_refs = refs[n_seq + n_par:n_seq + n_par + n_out]
            states_ref = refs[n_seq + n_par + n_out]
            st = refs[-1]

            @pl.when(pl.program_id(0) == 0)
            def _():
                st[...] = jnp.zeros_like(st)

            s0 = st[...]
            states_ref[0] = s0
            res = fn(s0, *[r[...] for r in ins])
            st[...] = res[0]
            for o_ref, r in zip(out_refs, res[1:]):
                o_ref[...] = r

        outs = pl.pallas_call(
            body, name=name + "_fwd", grid=(nc,), in_specs=seq_specs + par_specs,
            out_specs=[pl.BlockSpec((CHUNK, w), lambda c: (c, 0)) for w in out_widths]
            + [pl.BlockSpec((1,) + state_shape, lambda c: (c,) + zeros_idx)],
            out_shape=[jax.ShapeDtypeStruct((t, w), F32) for w in out_widths]
            + [jax.ShapeDtypeStruct((nc,) + state_shape, F32)],
            scratch_shapes=[pltpu.VMEM(state_shape, F32)],
            compiler_params=_params(("arbitrary",)),
        )(*args)
        return tuple(outs[:n_out]), outs[n_out]

    def bwd_call(args, states, cts):
        t = args[0].shape[0]
        nc = t // CHUNK
        rev = lambda c: (nc - 1 - c, 0)
        seq_specs = [pl.BlockSpec((CHUNK, a.shape[1]), rev) for a in args[:n_seq]]
        par_specs = [pl.BlockSpec(a.shape, lambda c: (0, 0)) for a in args[n_seq:]]
        ct_specs = [pl.BlockSpec((CHUNK, w), rev) for w in out_widths]
        st_spec = pl.BlockSpec((1,) + state_shape, lambda c: (nc - 1 - c,) + zeros_idx)
        n_in = n_seq + n_par

        def body(*refs):
            vals = [r[...] for r in refs[:n_in]]
            s0 = refs[n_in][0]
            ct_vals = tuple(r[...] for r in refs[n_in + 1:n_in + 1 + n_out])
            d_refs = refs[n_in + 1 + n_out:-1]
            dst = refs[-1]

            @pl.when(pl.program_id(0) == 0)
            def _():
                dst[...] = jnp.zeros_like(dst)
                for j in range(n_par):
                    d_refs[n_seq + j][...] = jnp.zeros_like(d_refs[n_seq + j])

            _, vjp = jax.vjp(lambda *a: tuple(fn(*a)), s0, *vals)
            grads = vjp((dst[...],) + ct_vals)
            dst[...] = grads[0]
            for i in range(n_seq):
                d_refs[i][...] = grads[1 + i]
            for j in range(n_par):
                d_refs[n_seq + j][...] += grads[1 + n_seq + j]

        outs = pl.pallas_call(
            body, name=name + "_bwd", grid=(nc,), in_specs=seq_specs + par_specs + [st_spec] + ct_specs,
            out_specs=seq_specs + par_specs,
            out_shape=[jax.ShapeDtypeStruct(a.shape, F32) for a in args],
            scratch_shapes=[pltpu.VMEM(state_shape, F32)],
            compiler_params=_params(("arbitrary",)),
        )(*args, states, *cts)
        return tuple(outs)

    @jax.custom_vjp
    def op(*args):
        return fwd_call(*args)[0]

    def fwd(*args):
        outs, states = fwd_call(*args)
        return outs, (args, states)

    def bwd(res, cts):
        args, states = res
        return bwd_call(args, states, cts)

    op.defvjp(fwd, bwd)
    return op


def _s5_scan_fwd(bre, bim, are, aim, name):
    t, r, _ = bre.shape
    tb = _any_tile(t, 208)
    blk = pl.BlockSpec((tb, r, LANES), lambda i: (i, 0, 0))
    par = pl.BlockSpec((r, LANES), lambda i: (0, 0))

    def body(bre_ref, bim_ref, are_ref, aim_ref, sre_ref, sim_ref, st):
        @pl.when(pl.program_id(0) == 0)
        def _():
            st[...] = jnp.zeros_like(st)

        ar, ai = are_ref[...], aim_ref[...]

        def step(k, carry):
            sr, si = carry
            nr = ar * sr - ai * si + bre_ref[k]
            ni = ar * si + ai * sr + bim_ref[k]
            sre_ref[k] = nr
            sim_ref[k] = ni
            return nr, ni

        sr, si = lax.fori_loop(0, tb, step, (st[0], st[1]), unroll=4)
        st[0] = sr
        st[1] = si

    return pl.pallas_call(
        body, name=name, grid=(t // tb,), in_specs=[blk, blk, par, par], out_specs=[blk, blk],
        out_shape=[jax.ShapeDtypeStruct(bre.shape, F32)] * 2,
        scratch_shapes=[pltpu.VMEM((2, r, LANES), F32)],
        compiler_params=_params(("arbitrary",)),
    )(bre, bim, are, aim)


def _s5_scan_bwd(dsr, dsi, sre, sim, are, aim, name):
    t, r, _ = sre.shape
    tb = _any_tile(t, 208)
    nb = t // tb
    blk = pl.BlockSpec((tb, r, LANES), lambda i: (nb - 1 - i, 0, 0))
    par = pl.BlockSpec((r, LANES), lambda i: (0, 0))

    def body(dsr_ref, dsi_ref, sre_ref, sim_ref, are_ref, aim_ref, gre_ref, gim_ref, dar_ref, dai_ref, st):
        @pl.when(pl.program_id(0) == 0)
        def _():
            st[...] = jnp.zeros_like(st)
            dar_ref[...] = jnp.zeros_like(dar_ref)
            dai_ref[...] = jnp.zeros_like(dai_ref)

        ar, ai = are_ref[...], aim_ref[...]

        def step(k, carry):
            gr, gi, dar, dai = carry
            q = tb - 1 - k
            s_r, s_i = sre_ref[q], sim_ref[q]
            dar = dar + gr * s_r + gi * s_i
            dai = dai + gi * s_r - gr * s_i
            ngr = dsr_ref[q] + ar * gr + ai * gi
            ngi = dsi_ref[q] + ar * gi - ai * gr
            gre_ref[q] = ngr
            gim_ref[q] = ngi
            return ngr, ngi, dar, dai

        gr, gi, dar, dai = lax.fori_loop(0, tb, step, (st[0], st[1], dar_ref[...], dai_ref[...]), unroll=4)
        st[0] = gr
        st[1] = gi
        dar_ref[...] = dar
        dai_ref[...] = dai

    return pl.pallas_call(
        body, name=name, grid=(nb,), in_specs=[blk, blk, blk, blk, par, par], out_specs=[blk, blk, par, par],
        out_shape=[jax.ShapeDtypeStruct(sre.shape, F32)] * 2 + [jax.ShapeDtypeStruct(are.shape, F32)] * 2,
        scratch_shapes=[pltpu.VMEM((2, r, LANES), F32)],
        compiler_params=_params(("arbitrary",)),
    )(dsr, dsi, sre, sim, are, aim)


def _make_s5_scan(name):
    @jax.custom_vjp
    def scan(bre, bim, are, aim):
        return tuple(_s5_scan_fwd(bre, bim, are, aim, name + "_fwd"))

    def fwd(bre, bim, are, aim):
        sre, sim = _s5_scan_fwd(bre, bim, are, aim, name + "_fwd")
        return (sre, sim), (sre, sim, are, aim)

    def bwd(res, cts):
        sre, sim, are, aim = res
        return tuple(_s5_scan_bwd(cts[0], cts[1], sre, sim, are, aim, name + "_bwd"))

    scan.defvjp(fwd, bwd)
    return scan


def _dot(a, b):
    return jnp.dot(a.astype(BF16), b.astype(BF16), preferred_element_type=F32)


def _dot_nt(a, b):
    return lax.dot_general(a.astype(BF16), b.astype(BF16), (((1,), (1,)), ((), ())), preferred_element_type=F32)


def _dot_tn(a, b):
    return lax.dot_general(a.astype(BF16), b.astype(BF16), (((0,), (0,)), ((), ())), preferred_element_type=F32)


def _split3(x):
    x1 = x.astype(BF16)
    rest = x - x1.astype(F32)
    x2 = rest.astype(BF16)
    return x1, x2, (rest - x2.astype(F32)).astype(BF16)


def _sel_dot(dims, a, b, a_is_sel):
    sel = (a if a_is_sel else b).astype(BF16)
    acc = None
    for piece in _split3(b if a_is_sel else a):
        pair = (sel, piece) if a_is_sel else (piece, sel)
        term = lax.dot_general(*pair, (dims, ((), ())), preferred_element_type=F32)
        acc = term if acc is None else acc + term
    return acc


@jax.custom_vjp
def _running_sum(low, y):
    return _sel_dot(((1,), (0,)), low, y, True)


_running_sum.defvjp(lambda low, y: (_running_sum(low, y), low),
                    lambda low, ct: (jnp.zeros_like(low), _sel_dot(((0,), (0,)), low, ct, True)))


@jax.custom_vjp
def _spread(x, sel):
    return _sel_dot(((1,), (0,)), x, sel, False)


_spread.defvjp(lambda x, sel: (_spread(x, sel), sel),
               lambda sel, ct: (_sel_dot(((1,), (1,)), ct, sel, False), jnp.zeros_like(sel)))


@jax.custom_vjp
def _transposed(x, eye):
    return _sel_dot(((0,), (0,)), x, eye, False)


_transposed.defvjp(lambda x, eye: (_transposed(x, eye), eye),
                   lambda eye, ct: (_sel_dot(((1,), (1,)), eye, ct, True), jnp.zeros_like(eye)))


def _iota(shape, dim):
    return lax.broadcasted_iota(jnp.int32, shape, dim)


def _tri(strict=False):
    r, c = _iota((CHUNK, CHUNK), 0), _iota((CHUNK, CHUNK), 1)
    return (r > c) if strict else (r >= c)


def _silu(x):
    return x * jax.nn.sigmoid(x)


def _layer_norm(z, g, b):
    mu = jnp.mean(z, axis=-1, keepdims=True)
    var = jnp.mean(jnp.square(z - mu), axis=-1, keepdims=True)
    return (z - mu) * lax.rsqrt(var + LN_EPS) * g + b


def _ssd_chunk(state, xbc, sm, a_c, d_exp):
    width = SSD_HEADS * SSD_HEAD
    x = xbc[:, :width]
    lane = _iota((CHUNK, LANES), 1)
    dtc = jnp.where(lane < SSD_HEADS, sm, 0.0)
    low = _tri().astype(F32)
    eye = (_iota((CHUNK, CHUNK), 0) == _iota((CHUNK, CHUNK), 1)).astype(F32)
    head_col, head_row = _iota((LANES, width), 1), _iota((LANES, width), 0) * SSD_HEAD
    expand = ((head_col >= head_row) & (head_col < head_row + SSD_HEAD)).astype(F32)
    acum_c = _running_sum(low, dtc * a_c)
    acum_ct = _transposed(acum_c, eye)
    dt_exp = _spread(dtc, expand)
    acum = _spread(acum_c, expand)
    xd = x * dt_exp
    last = acum[CHUNK - 1:CHUNK, :]
    to_end = jnp.exp(last - acum)
    eac = jnp.exp(acum)
    causal = _tri()
    first_half = _iota((CHUNK, LANES), 1) < SSD_HEAD
    top_rows = _iota((LANES, LANES), 0) < SSD_HEAD
    pairs = range(SSD_HEADS // 2)
    grp = [(2 * p) // (SSD_HEADS // SSD_GROUPS) for p in pairs]
    cols = [slice(p * LANES, (p + 1) * LANES) for p in pairs]
    bg = [xbc[:, width + g * SSD_STATE: width + (g + 1) * SSD_STATE] for g in range(SSD_GROUPS)]
    cg = [xbc[:, width + (SSD_GROUPS + g) * SSD_STATE: width + (SSD_GROUPS + g + 1) * SSD_STATE]
          for g in range(SSD_GROUPS)]
    scores = [_dot_nt(cg[g], bg[g]) for g in range(SSD_GROUPS)]
    dec = [jnp.where(causal, jnp.exp(jnp.minimum(acum_c[:, h:h + 1] - acum_ct[h:h + 1, :], 0.0)), 0.0)
           for h in range(SSD_HEADS)]
    y_lo = [_dot(scores[grp[p]] * dec[2 * p], jnp.where(first_half, xd[:, cols[p]], 0.0)) for p in pairs]
    y_hi = [_dot(scores[grp[p]] * dec[2 * p + 1], jnp.where(first_half, 0.0, xd[:, cols[p]])) for p in pairs]
    s_prev = [state[p * LANES:(p + 1) * LANES, :] for p in pairs]
    y_off = [_dot_nt(cg[grp[p]], s_prev[p]) for p in pairs]
    s_add = [_dot_tn(xd[:, cols[p]] * to_end[:, cols[p]], bg[grp[p]]) for p in pairs]
    ys = [y_lo[p] + y_hi[p] + y_off[p] * eac[:, cols[p]] + x[:, cols[p]] * d_exp[:, cols[p]] for p in pairs]
    cd = [jnp.where(top_rows, jnp.exp(acum_c[CHUNK - 1:CHUNK, 2 * p:2 * p + 1]),
                    jnp.exp(acum_c[CHUNK - 1:CHUNK, 2 * p + 1:2 * p + 2])) for p in pairs]
    new_states = [s_prev[p] * cd[p] + s_add[p] for p in pairs]
    return jnp.concatenate(new_states, axis=0), jnp.concatenate(ys, axis=1)


def _neumann(a_mats, rhs, transposed):
    dot = _dot_tn if transposed else _dot
    nmats, sols = [-a for a in a_mats], list(rhs)
    for i in range(6):
        upd = [dot(n, s) for n, s in zip(nmats, sols)]
        if i < 5:
            nmats = [_dot(n, n) for n in nmats]
        sols = [s + u for s, u in zip(sols, upd)]
    return tuple(sols)


@jax.custom_vjp
def _solve_unit_lower(a_mats, rhs):
    return _neumann(a_mats, rhs, False)


def _solve_unit_lower_fwd(a_mats, rhs):
    sols = _neumann(a_mats, rhs, False)
    return sols, (a_mats, sols)


def _solve_unit_lower_bwd(res, d_sols):
    a_mats, sols = res
    d_rhs = _neumann(a_mats, d_sols, True)
    return tuple(-_dot_nt(dr, x) for dr, x in zip(d_rhs, sols)), d_rhs


_solve_unit_lower.defvjp(_solve_unit_lower_fwd, _solve_unit_lower_bwd)


def _gdn_chunk(state, qkv, sm):
    width = GDN_HEADS * GDN_HEAD
    g0, b0 = SSD_HEADS, SSD_HEADS + GDN_HEADS
    lane = _iota((CHUNK, LANES), 1)
    gc = jnp.where((lane >= g0) & (lane < b0), sm, 0.0)
    low = _tri().astype(F32)
    eye = (_iota((CHUNK, CHUNK), 0) == _iota((CHUNK, CHUNK), 1)).astype(F32)
    gcum = _running_sum(low, gc)
    gcum_t = _transposed(gcum, eye)
    causal, strict = _tri(), _tri(strict=True)
    heads = range(GDN_HEADS)
    q = [qkv[:, h * GDN_HEAD:(h + 1) * GDN_HEAD] for h in heads]
    k = [qkv[:, width + h * GDN_HEAD: width + (h + 1) * GDN_HEAD] for h in heads]
    v = [qkv[:, 2 * width + h * GDN_HEAD: 2 * width + (h + 1) * GDN_HEAD] for h in heads]
    beta = [sm[:, b0 + h:b0 + h + 1] for h in heads]
    gcol = [gcum[:, g0 + h:g0 + h + 1] for h in heads]
    glast = [gcum[CHUNK - 1:CHUNK, g0 + h:g0 + h + 1] for h in heads]
    gamma = [jnp.where(causal, jnp.exp(jnp.minimum(gcol[h] - gcum_t[g0 + h:g0 + h + 1, :], 0.0)), 0.0) for h in heads]
    kk = [_dot_nt(k[h], k[h]) for h in heads]
    qk = [_dot_nt(q[h], k[h]) for h in heads]
    egc = [jnp.exp(gcol[h]) for h in heads]
    a_mat = tuple(jnp.where(strict, kk[h] * gamma[h] * beta[h], 0.0) for h in heads)
    sol = _solve_unit_lower(a_mat, tuple(jnp.concatenate([v[h] * beta[h], k[h] * (beta[h] * egc[h])], axis=1)
                                         for h in heads))
    s_prev = [state[h * GDN_HEAD:(h + 1) * GDN_HEAD, :] for h in heads]
    w_s = [_dot(sol[h][:, GDN_HEAD:], s_prev[h]) for h in heads]
    q_s = [_dot(q[h] * egc[h], s_prev[h]) for h in heads]
    v_new = [sol[h][:, :GDN_HEAD] - w_s[h] for h in heads]
    a_v = [_dot(qk[h] * gamma[h], v_new[h]) for h in heads]
    k_v = [_dot_tn(k[h] * jnp.exp(glast[h] - gcol[h]), v_new[h]) for h in heads]
    outs = [q_s[h] + a_v[h] for h in heads]
    new_states = [s_prev[h] * jnp.exp(glast[h]) + k_v[h] for h in heads]
    return jnp.concatenate(new_states, axis=0), jnp.concatenate(outs, axis=1)


def _row_fns(d_model, pad_rows, loss_rows, alpha):
    def keep(ridx, v):
        return jnp.where(ridx >= pad_rows, v, 0.0)

    def both(h):
        return h, h.astype(BF16)

    def ln_in(ridx, h, g, b):
        return both(keep(ridx, _layer_norm(h, g, b)))

    def s5_tail(y_re, y_im, u, z, d, bias, glu, branch):
        v0 = jax.nn.gelu(y_re + y_im + d * u)
        return (branch(v0 * jax.nn.sigmoid(glu(v0) + bias) * _silu(z)),)

    def small_act(ridx, raw, bias, scale):
        lane = _iota(raw.shape, 1)
        sp = jax.nn.softplus(raw + bias)
        g0, b0 = SSD_HEADS, SSD_HEADS + GDN_HEADS
        out = jnp.where(lane < g0, sp, jnp.where(lane < b0, scale * sp,
                                                 jnp.where(lane < b0 + GDN_HEADS, jax.nn.sigmoid(raw), 0.0)))
        return (keep(ridx, out),)

    def conv(xs, w):
        acc = xs[0] * w[0:1, :]
        for j in range(1, CONV_K):
            acc = acc + xs[j] * w[j:j + 1, :]
        return acc

    def ssd_conv(ridx, x0, x1, x2, x3, w, b):
        return (keep(ridx, _silu(conv((x0, x1, x2, x3), w) + b)),)

    def ssd_tail(y, z, g, branch):
        v = y * _silu(z)
        return (branch(v * lax.rsqrt(jnp.mean(v * v, axis=-1, keepdims=True) + LN_EPS) * g),)

    def gdn_conv(x0, x1, x2, x3, w):
        a = _silu(conv((x0, x1, x2, x3), w))
        width = GDN_HEADS * GDN_HEAD
        parts = []
        for h in range(2 * GDN_HEADS):
            z = a[:, h * GDN_HEAD:(h + 1) * GDN_HEAD]
            z = z * lax.rsqrt(jnp.sum(z * z, axis=-1, keepdims=True) + 1e-6)
            parts.append(z * GDN_HEAD ** -0.5 if h < GDN_HEADS else z)
        parts.append(a[:, 2 * width:])
        return (jnp.concatenate(parts, axis=1),)

    def gdn_tail(o, z, g, branch):
        parts = []
        for h in range(GDN_HEADS):
            cols = slice(h * GDN_HEAD, (h + 1) * GDN_HEAD)
            oh = o[:, cols]
            oh = oh * lax.rsqrt(jnp.mean(oh * oh, axis=-1, keepdims=True) + LN_EPS) * g
            parts.append(oh * _silu(z[:, cols]))
        return (branch(jnp.concatenate(parts, axis=1)),)

    def merge_out(ridx, oa, ob, oc, gate, h, bias, g, b, w_out):
        acc = None
        for k, o in enumerate((oa, ob, oc)):
            cols = slice(k * d_model, (k + 1) * d_model)
            term = jax.nn.sigmoid(gate[:, cols] + bias[:, cols]) * o
            acc = term if acc is None else acc + term
        return both(keep(ridx, _layer_norm(alpha * h + w_out(acc), g, b)))

    def loss_rows_fn(ridx, h, tgt):
        row = 0.5 * jnp.mean(jnp.square(h - tgt), axis=-1, keepdims=True)
        row = jnp.where(ridx >= loss_rows, row, 0.0)
        lane = _iota((h.shape[0], LANES), 1)
        return (jnp.where(lane == 0, row, 0.0),)

    return dict(ln_in=ln_in, s5_tail=s5_tail, small_act=small_act, ssd_conv=ssd_conv, ssd_tail=ssd_tail,
                gdn_conv=gdn_conv, gdn_tail=gdn_tail, merge_out=merge_out, loss=loss_rows_fn)


def _pair_exchange(srcs, whole_srcs, name):
    n_arr, n_whole = len(srcs), len(whole_srcs)
    n_chip = N_DEV // 2
    base = n_arr * n_chip

    def body(*refs):
        src_refs, recv_refs = refs[:n_arr + n_whole], refs[n_arr + n_whole:2 * (n_arr + n_whole)]
        send_sems, recv_sems = refs[2 * (n_arr + n_whole):]
        x, y, c = lax.axis_index("x"), lax.axis_index("y"), lax.axis_index("c")

        def to_sibling(src, dst, k):
            return pltpu.make_async_remote_copy(src_ref=src, dst_ref=dst, send_sem=send_sems.at[k],
                                                recv_sem=recv_sems.at[k], device_id=(x, y, 1 - c),
                                                device_id_type=pl.DeviceIdType.MESH)

        copies = [to_sibling(src_refs[a].at[q, 1 - c], recv_refs[a].at[q], a * n_chip + q)
                  for a in range(n_arr) for q in range(n_chip)]
        copies += [to_sibling(src_refs[n_arr + b], recv_refs[n_arr + b], base + b) for b in range(n_whole)]
        for cp in copies:
            cp.start()
        for cp in copies:
            cp.wait()

    n_sem = base + n_whole
    return pl.pallas_call(
        body, name=name,
        in_specs=[pl.BlockSpec(memory_space=pl.ANY)] * (n_arr + n_whole),
        out_specs=[pl.BlockSpec(memory_space=pl.ANY)] * (n_arr + n_whole),
        out_shape=[jax.ShapeDtypeStruct((n_chip,) + tuple(s.shape[2:]), s.dtype) for s in srcs]
        + [jax.ShapeDtypeStruct(s.shape, s.dtype) for s in whole_srcs],
        scratch_shapes=[pltpu.SemaphoreType.DMA((n_sem,)), pltpu.SemaphoreType.DMA((n_sem,))],
    )(*srcs, *whole_srcs)


def _pair_sum(src, recv, core, name):
    n_chip, rows, cols = recv.shape
    tr = _row_tile(rows, 1024 if cols <= LANES else 512)

    def body(core_ref, a_ref, b_ref, o_ref):
        o_ref[...] = (a_ref[0].astype(F32) + b_ref[...].astype(F32)).astype(o_ref.dtype)

    blk = pl.BlockSpec((1, tr, cols), lambda q, i, core_ref: (q, i, 0))
    return pl.pallas_call(
        body, name=name,
        grid_spec=pltpu.PrefetchScalarGridSpec(
            num_scalar_prefetch=1, grid=(n_chip, rows // tr),
            in_specs=[pl.BlockSpec((1, 1, tr, cols), lambda q, i, core_ref: (q, core_ref[0], i, 0)), blk],
            out_specs=blk),
        out_shape=jax.ShapeDtypeStruct(recv.shape, recv.dtype),
        compiler_params=_params(("arbitrary", "arbitrary")),
    )(core, src, recv)


def _add2(a, b, name):
    rows = a.shape[0]
    tr = _row_tile(rows, FLAT_ROWS)
    blk = pl.BlockSpec((tr, LANES), lambda i: (i, 0))

    def body(a_ref, b_ref, o_ref):
        o_ref[...] = a_ref[...] + b_ref[...]

    return pl.pallas_call(body, name=name, grid=(rows // tr,), in_specs=[blk, blk], out_specs=blk,
                          out_shape=jax.ShapeDtypeStruct(a.shape, F32), compiler_params=_params(("arbitrary",)))(a, b)


def _add_parts(parts, name):
    n, rows, _ = parts.shape
    tr = _row_tile(rows, FLAT_ROWS)

    def body(p_ref, o_ref):
        acc = p_ref[0]
        for k in range(1, n):
            acc = acc + p_ref[k]
        o_ref[...] = acc

    return pl.pallas_call(body, name=name, grid=(rows // tr,),
                          in_specs=[pl.BlockSpec((n, tr, LANES), lambda i: (0, i, 0))],
                          out_specs=pl.BlockSpec((tr, LANES), lambda i: (i, 0)),
                          out_shape=jax.ShapeDtypeStruct((rows, LANES), F32),
                          compiler_params=_params(("arbitrary",)))(parts)


def _chip_exchange(chip_srcs, whole_srcs, name):
    n_chip_arr, n_whole = len(chip_srcs), len(whole_srcs)
    n_arr = n_chip_arr + n_whole
    n_chip = N_DEV // 2
    chip_flips = [(1, 0), (0, 1), (1, 1)]

    def body(*refs):
        src_refs, out_refs = refs[:n_arr], refs[n_arr:2 * n_arr]
        send_sems, recv_sems = refs[2 * n_arr:]
        x, y, c = lax.axis_index("x"), lax.axis_index("y"), lax.axis_index("c")
        my_chip = 2 * x + y
        copies = []
        for a in range(n_arr):
            for k, (fx, fy) in enumerate(chip_flips):
                px = 1 - x if fx else x
                py = 1 - y if fy else y
                src = src_refs[a].at[2 * px + py] if a < n_chip_arr else src_refs[a]
                copies.append(pltpu.make_async_remote_copy(
                    src_ref=src, dst_ref=out_refs[a].at[my_chip],
                    send_sem=send_sems.at[a * 3 + k], recv_sem=recv_sems.at[a * 3 + k],
                    device_id=(px, py, c), device_id_type=pl.DeviceIdType.MESH))
        for cp in copies:
            cp.start()
        for cp in copies:
            cp.wait()

    n_sem = n_arr * len(chip_flips)
    return pl.pallas_call(
        body, name=name,
        in_specs=[pl.BlockSpec(memory_space=pl.ANY)] * n_arr, out_specs=[pl.BlockSpec(memory_space=pl.ANY)] * n_arr,
        out_shape=[jax.ShapeDtypeStruct(s.shape, s.dtype) for s in chip_srcs]
        + [jax.ShapeDtypeStruct((n_chip,) + tuple(s.shape), s.dtype) for s in whole_srcs],
        scratch_shapes=[pltpu.SemaphoreType.DMA((n_sem,)), pltpu.SemaphoreType.DMA((n_sem,))],
    )(*chip_srcs, *whole_srcs)


def _gather(srcs, name):
    n_arr = len(srcs)
    n_sem = N_DEV - 1

    def body(*refs):
        src_refs, out_refs = refs[:n_arr], refs[n_arr:2 * n_arr]
        send_sems, recv_sems = refs[2 * n_arr:]
        x, y, c = lax.axis_index("x"), lax.axis_index("y"), lax.axis_index("c")
        me, sibling = (x, y, c), (x, y, 1 - c)
        chips = [(1 - x, y), (x, 1 - y), (1 - x, 1 - y)]

        def slot(a, dev):
            return out_refs[a].at[4 * dev[0] + 2 * dev[1] + dev[2]]

        def copy(a, k, block, to, own=False):
            return pltpu.make_async_remote_copy(
                src_ref=src_refs[a] if own else slot(a, block), dst_ref=slot(a, block),
                send_sem=send_sems.at[a * n_sem + k], recv_sem=recv_sems.at[a * n_sem + k],
                device_id=to, device_id_type=pl.DeviceIdType.MESH)

        arrays = range(n_arr)
        first = [copy(a, 0, me, sibling, own=True) for a in arrays]
        first += [copy(a, 1 + j, me, (*chip, c), own=True) for j, chip in enumerate(chips) for a in arrays]
        for cp in first:
            cp.start()
        passed = []
        for j, chip in enumerate(chips):
            for a in arrays:
                copy(a, 1 + j, (*chip, c), me).wait_recv()
                fwd = copy(a, 4 + j, (*chip, c), sibling)
                fwd.start()
                passed.append(fwd)
        for a in arrays:
            copy(a, 0, sibling, me).wait_recv()
        for j, chip in enumerate(chips):
            for a in arrays:
                copy(a, 4 + j, (*chip, 1 - c), me).wait_recv()
        for cp in first + passed:
            cp.wait_send()

    return pl.pallas_call(
        body, name=name,
        in_specs=[pl.BlockSpec(memory_space=pl.ANY)] * n_arr, out_specs=[pl.BlockSpec(memory_space=pl.ANY)] * n_arr,
        out_shape=[jax.ShapeDtypeStruct((N_DEV,) + tuple(s.shape), s.dtype) for s in srcs],
        scratch_shapes=[pltpu.SemaphoreType.DMA((n_arr * n_sem,)), pltpu.SemaphoreType.DMA((n_arr * n_sem,))],
    )(*srcs)


def _adamw_body(p_ref, w_ref, m_ref, v_ref, g_ref, d_ref, nm_ref, nv_ref):
    bc1 = 1.0 - ADAM_B1 ** ADAM_STEP
    bc2 = 1.0 - ADAM_B2 ** ADAM_STEP
    g = p_ref[0].astype(F32)
    for k in range(1, p_ref.shape[0]):
        g = g + p_ref[k].astype(F32)
    nm = ADAM_B1 * m_ref[...] + (1.0 - ADAM_B1) * g
    nv = ADAM_B2 * v_ref[...] + (1.0 - ADAM_B2) * jnp.square(g)
    m_hat = nm / bc1
    v_hat = nv / bc2
    g_ref[...] = g
    d_ref[...] = -ADAM_LR * (m_hat / (jnp.sqrt(v_hat) + ADAM_EPS) + ADAM_WD * w_ref[...])
    nm_ref[...] = nm
    nv_ref[...] = nv


def _adamw_rows(parts, w, m, v, name):
    rows, cols = w.shape
    tr = _row_tile(rows, max(128, (1 << 18) // cols))
    blk = pl.BlockSpec((tr, cols), lambda i: (i, 0))
    return pl.pallas_call(
        functools.partial(_adamw_body), name=name, grid=(rows // tr,),
        in_specs=[pl.BlockSpec((parts.shape[0], tr, cols), lambda i: (0, i, 0)), blk, blk, blk],
        out_specs=[blk] * 4, out_shape=[jax.ShapeDtypeStruct(w.shape, F32)] * 4,
        compiler_params=_params(("arbitrary",)),
    )(parts, w, m, v)


def _pad_flat(vec, rows):
    return jnp.pad(vec, (0, rows * LANES - vec.shape[0])).reshape(rows, LANES)


def _rows_for(n):
    return -(-n // (FLAT_ROWS * LANES)) * FLAT_ROWS


def _join_shards(rows, local_shape, dim):
    parts = jnp.moveaxis(rows.reshape((N_DEV,) + tuple(local_shape)), 0, dim)
    shp = tuple(local_shape)
    return parts.reshape(shp[:dim] + (N_DEV * shp[dim],) + shp[dim + 1:])


def _s5_tables(a_re, a_im, log_step, b_re, b_im, c_re, c_im):
    lam_re = jnp.minimum(a_re, -1e-4)
    lam_im = a_im
    step = jnp.exp(log_step)[:, None]
    mag = jnp.exp(lam_re * step)
    abar_re, abar_im = mag * jnp.cos(lam_im * step), mag * jnp.sin(lam_im * step)
    den = lam_re * lam_re + lam_im * lam_im
    nr, ni = abar_re - 1.0, abar_im
    coef_re = (nr * lam_re + ni * lam_im) / den
    coef_im = (ni * lam_re - nr * lam_im) / den
    bbar_re = coef_re[..., None] * b_re - coef_im[..., None] * b_im
    bbar_im = coef_re[..., None] * b_im + coef_im[..., None] * b_re
    groups = a_re.shape[0]
    nblk = groups // S5_BLOCK_GROUPS
    eye = jnp.eye(S5_BLOCK_GROUPS, dtype=F32)

    def in_blocks(bb):
        t = jnp.swapaxes(bb, 1, 2).reshape(nblk, S5_BLOCK_GROUPS, S5_GROUP, S5_STATE)
        blk = jnp.einsum('ab,jacp->jacbp', eye, t)
        return blk.reshape(nblk, S5_BLOCK_GROUPS * S5_GROUP, S5_BLOCK_GROUPS * S5_STATE)

    def out_blocks(cc):
        t = jnp.swapaxes(cc, 1, 2).reshape(nblk, S5_BLOCK_GROUPS, S5_STATE, S5_GROUP)
        blk = jnp.einsum('ab,japc->japbc', eye, t)
        return blk.reshape(nblk, S5_BLOCK_GROUPS * S5_STATE, S5_BLOCK_GROUPS * S5_GROUP)

    rows = groups * S5_STATE // LANES
    return dict(b_re=in_blocks(bbar_re), b_im=in_blocks(bbar_im), c_re=out_blocks(c_re), c_im=out_blocks(-c_im),
                a_re=abar_re.reshape(rows, LANES), a_im=abar_im.reshape(rows, LANES))


def _in_widths(d_model):
    return [BRANCH, BRANCH, SSD_HEADS * SSD_HEAD + 2 * SSD_GROUPS * SSD_STATE, SSD_HEADS, BRANCH,
            3 * BRANCH, GDN_HEADS, GDN_HEADS, BRANCH, 3 * d_model]


def _local_loss(w, mats, x, target):
    n_meta, d_model = w['meta'].shape
    depth = len(mats['in'])
    seq = x.shape[0]
    pad_rows = CHUNK - n_meta
    first = pad_rows + n_meta
    t_all = first + seq
    alpha = (2 * depth) ** 0.25
    fns = _row_fns(d_model, pad_rows, first, alpha)
    row = lambda nm, key, n_row, n_par, widths, cap, ridx=False: _make_rowwise(nm, fns[key], n_row, n_par, widths, cap, ridx)

    h = jnp.concatenate([jnp.zeros((pad_rows, d_model), F32), w['meta'], x], axis=0)
    h, h16 = _make_rowwise("ln_in", fns['ln_in'], 1, 2, [d_model, d_model], 416, True, out_dtypes=[F32, BF16])(
        h, w['ln_in_g'][None], w['ln_in_b'][None])

    n_small = SSD_HEADS + 2 * GDN_HEADS

    def small_cols(ps):
        return jnp.pad(jnp.concatenate([ps[3], ps[6], ps[7]], axis=1), ((0, 0), (0, LANES - n_small)))

    for l in range(depth):
        pw, pc = mats['in'][l], w['c_in'][l]
        used = (0, 1, 2, 4, 5, 8, 9)
        s5_u, s5_z, ssd_xbc, ssd_z, gdn_qkv, gdn_z, gate, small = _in_proj(
            ["in_s5u", "in_s5z", "in_ssdx", "in_ssdz", "in_gdnq", "in_gdnz", "in_gate", "in_small"], h, h16,
            [pw[i] for i in used] + [small_cols(pw)], [pc[i] for i in used] + [small_cols(pc)])

        zeros_tail = jnp.zeros((LANES - n_small,), F32)
        bias = jnp.concatenate([w['ssd_dt_bias'][l], w['gdn_dt_bias'][l], jnp.zeros((GDN_HEADS,), F32), zeros_tail])[None]
        scale = jnp.concatenate([jnp.ones((SSD_HEADS,), F32), -jnp.exp(w['gdn_a_log'][l]),
                                 jnp.zeros((GDN_HEADS,), F32), zeros_tail])[None]
        (sm,) = row("small_act", 'small_act', 1, 2, [LANES], 832, True)(small, bias, scale)

        tb = _s5_tables(w['s5_a_re'][l], w['s5_a_im'][l], w['s5_log_step'][l], w['s5_b_re'][l], w['s5_b_im'][l],
                        w['s5_c_re'][l], w['s5_c_im'][l])
        srows = tb['a_re'].shape[0]
        bu_re = _make_gmm("s5_bre")(s5_u, tb['b_re']).reshape(t_all, srows, LANES)
        bu_im = _make_gmm("s5_bim")(s5_u, tb['b_im']).reshape(t_all, srows, LANES)
        s_re, s_im = _make_s5_scan("s5_scan")(bu_re, bu_im, tb['a_re'], tb['a_im'])
        y_re = _make_gmm("s5_cre")(s_re.reshape(t_all, srows * LANES), tb['c_re'])
        y_im = _make_gmm("s5_cim")(s_im.reshape(t_all, srows * LANES), tb['c_im'])
        (out_a,) = _make_rowwise("s5_tail", fns['s5_tail'], 4, 2, [d_model], 208, n_wt=2)(
            y_re, y_im, s5_u, s5_z, w['s5_d'][l][None], w['s5_b_glu'][l][None],
            mats['glu'][l], mats['branch'][l, 0], w['c_glu'][l], w['c_branch'][l, 0])

        xbc = _make_conv_rowwise("ssd_conv", fns['ssd_conv'], 2, ssd_xbc.shape[1], 416, True)(
            ssd_xbc, w['ssd_conv_w'][l], w['ssd_conv_b'][l][None])
        a_c = jnp.pad(-jnp.exp(w['ssd_a_log'][l]), (0, LANES - SSD_HEADS))[None]
        d_exp = jnp.repeat(w['ssd_d'][l], SSD_HEAD)[None]
        (y_ssd,) = _make_chunk_scan("ssd_scan", _ssd_chunk, (SSD_HEADS // 2 * LANES, SSD_STATE), 2, 2, [BRANCH])(
            xbc, sm, a_c, d_exp)
        (out_b,) = _make_rowwise("ssd_tail", fns['ssd_tail'], 2, 1, [d_model], 416, n_wt=1)(
            y_ssd, ssd_z, w['ssd_norm_g'][l][None], mats['branch'][l, 1], w['c_branch'][l, 1])

        qkv = _make_conv_rowwise("gdn_conv", fns['gdn_conv'], 1, 3 * BRANCH, 208)(gdn_qkv, w['gdn_conv_w'][l])
        (o_gdn,) = _make_chunk_scan("gdn_scan", _gdn_chunk, (GDN_HEADS * GDN_HEAD, GDN_HEAD), 2, 0, [BRANCH])(qkv, sm)
        (out_c,) = _make_rowwise("gdn_tail", fns['gdn_tail'], 2, 1, [d_model], 416, n_wt=1)(
            o_gdn, gdn_z, w['gdn_norm_g'][l][None], mats['branch'][l, 2], w['c_branch'][l, 2])

        h, h16 = _make_rowwise("merge_out", fns['merge_out'], 5, 3, [d_model, d_model], 208, True, n_wt=1,
                               out_dtypes=[F32, BF16])(
            out_a, out_b, out_c, gate, h, w['b_gate'][l].reshape(1, 3 * d_model), w['ln_g'][l][None],
            w['ln_b'][l][None], mats['out'][l], w['c_out'][l])

    tgt = jnp.concatenate([jnp.zeros((first, d_model), F32), target], axis=0)
    (rows_loss,) = row("loss", 'loss', 2, 0, [LANES], 416, True)(h, tgt)
    return jnp.sum(rows_loss)


def _in_overlaps(d_model, n_loc):
    offs = [0]
    for wd in _in_widths(d_model):
        offs.append(offs[-1] + wd)
    out = []
    for i in range(len(offs) - 1):
        c0, c1 = offs[i], offs[i + 1]
        segs = []
        for k in range(N_DEV):
            g0, g1 = max(c0, k * n_loc), min(c1, (k + 1) * n_loc)
            if g0 < g1:
                segs.append((k, g0 - k * n_loc, g1 - k * n_loc, g0 - c0))
        out.append(segs)
    return out


STACKED = ['w_in', 's5_w_glu', 'w_branch', 'w_out']


def _shard_blocks(full, dim):
    shp = full.shape
    parts = full.reshape(shp[:dim] + (N_DEV, shp[dim] // N_DEV) + shp[dim + 1:])
    return jnp.moveaxis(parts, dim, 0)


def _join_blocks(blocks, dim):
    shp = blocks.shape[1:]
    return jnp.moveaxis(blocks, 0, dim).reshape(shp[:dim] + (N_DEV * shp[dim],) + shp[dim + 1:])


def _step(x, target, w_loc, m_loc, v_loc):
    small = [n for n in WEIGHTS if n in SHARD_DIM and n not in STACKED]
    repl = [n for n in WEIGHTS if n not in SHARD_DIM]
    size = lambda names: sum(int(w_loc[n].size) for n in names)
    depth, d_model, n_loc = w_loc['w_in'].shape
    overlaps = _in_overlaps(d_model, n_loc)
    n_chip = N_DEV // 2
    me = 4 * lax.axis_index("x") + 2 * lax.axis_index("y") + lax.axis_index("c")
    my_chip = 2 * lax.axis_index("x") + lax.axis_index("y")
    core = lax.axis_index("c").astype(jnp.int32).reshape(1)
    put = lambda buf, blk, idx: lax.dynamic_update_index_in_dim(buf, blk, idx, 0)
    take = lambda buf, idx: lax.dynamic_index_in_dim(buf, idx, 0, keepdims=False)

    rows_small = _rows_for(size(small))
    small_flat = _pad_flat(jnp.concatenate([w_loc[n].reshape(-1) for n in small]), rows_small)
    own_blocks = [w_loc[n].astype(BF16) for n in STACKED] + [small_flat]
    gathered = [put(got, blk, me) for got, blk in zip(_gather(own_blocks, "gather_weights"), own_blocks)]
    g_in = gathered[0]
    full = {n: _join_blocks(g, SHARD_DIM[n]) for n, g in zip(STACKED[1:], gathered[1:])}
    buf, off = gathered[-1].reshape(N_DEV, -1), 0
    for n in small:
        sz = int(w_loc[n].size)
        full[n] = _join_shards(buf[:, off:off + sz], w_loc[n].shape, SHARD_DIM[n])
        off += sz
    mats = dict(glu=full['s5_w_glu'], branch=full['w_branch'], out=full['w_out'], **{
        'in': [[jnp.concatenate([g_in[k, l, :, lo:hi] for k, lo, hi, _ in segs], axis=1) for segs in overlaps]
               for l in range(depth)]})
    w_diff = {n: w_loc[n] for n in repl}
    w_diff.update({n: full[n] for n in small})
    w_diff['c_in'] = [[jnp.zeros((d_model, wd), F32) for wd in _in_widths(d_model)] for _ in range(depth)]
    w_diff['c_glu'] = jnp.zeros(full['s5_w_glu'].shape, F32)
    w_diff['c_branch'] = jnp.zeros(full['w_branch'].shape, F32)
    w_diff['c_out'] = jnp.zeros(full['w_out'].shape, F32)

    loss, (g_w, g_x) = jax.value_and_grad(_local_loss, argnums=(0, 2))(w_diff, mats, x[0], target[0])
    g_w['s5_w_glu'], g_w['w_branch'], g_w['w_out'] = g_w['c_glu'], g_w['c_branch'], g_w['c_out']

    blocks = {'w_in': jnp.stack([jnp.stack([
        jnp.concatenate([g_w['c_in'][l][i][:, plo:plo + hi - lo]
                         for i, segs in enumerate(overlaps) for (kk, lo, hi, plo) in segs if kk == k], axis=1)
        for l in range(depth)]) for k in range(N_DEV)])}
    for n in STACKED[1:]:
        blocks[n] = _shard_blocks(g_w[n], SHARD_DIM[n])
    cols = {n: w_loc[n].shape[-1] for n in STACKED}
    send = [blocks[n].astype(BF16).reshape(n_chip, 2, -1, cols[n]) for n in STACKED]
    red_names = small + repl
    n_red = sum(int(g_w[n].size) for n in red_names) + 1
    red = _pad_flat(jnp.concatenate([g_w[n].reshape(-1) for n in red_names] + [loss.reshape(1)]), _rows_for(n_red))
    *recv, red_sib = _pair_exchange(send, [red], "pair_exchange")
    sums = [_pair_sum(s, r, core, "pair_sum_" + n) for s, r, n in zip(send, recv, STACKED)]
    red_chip = _add2(red, red_sib, "pair_sum_rest")
    *parts, red_parts = _chip_exchange(sums, [red_chip], "exchange_grads")
    parts = [put(p, take(s, my_chip), my_chip) for p, s in zip(parts, sums)]
    red_total = _add_parts(put(red_parts, red_chip, my_chip), "reduce_rest").reshape(-1)

    outs = {}
    for n, p in zip(STACKED, parts):
        as_rows = lambda a: a.reshape(-1, cols[n])
        res = _adamw_rows(p, as_rows(w_loc[n]), as_rows(m_loc[n]), as_rows(v_loc[n]), "adamw_" + n)
        outs[n] = [r.reshape(w_loc[n].shape) for r in res]
    g_red, off = {}, 0
    for n in red_names:
        sz = int(g_w[n].size)
        g_red[n] = red_total[off:off + sz].reshape(g_w[n].shape)
        off += sz
    loss_total = red_total[off]
    for n in small:
        g_red[n] = take(_shard_blocks(g_red[n], SHARD_DIM[n]), me)
    rows_rest = _rows_for(size(red_names))
    flat = lambda src: _pad_flat(jnp.concatenate([src[n].reshape(-1) for n in red_names]), rows_rest)
    res = _adamw_rows(flat(g_red)[None], flat(w_loc), flat(m_loc), flat(v_loc), "adamw_rest")
    off = 0
    for n in red_names:
        sz = int(w_loc[n].size)
        outs[n] = [r.reshape(-1)[off:off + sz].reshape(w_loc[n].shape) for r in res]
        off += sz
    return (loss_total, g_x[None], *[outs[n][k] for k in range(4) for n in WEIGHTS])


def kernel(x, meta, ln_in_g, ln_in_b, w_in, s5_a_re, s5_a_im, s5_log_step, s5_b_re, s5_b_im, s5_c_re, s5_c_im, s5_d, s5_w_glu, s5_b_glu, ssd_conv_w, ssd_conv_b, ssd_dt_bias, ssd_a_log, ssd_d, ssd_norm_g, gdn_conv_w, gdn_dt_bias, gdn_a_log, gdn_norm_g, w_branch, b_gate, w_out, ln_g, ln_b, loss_target, m_meta, m_ln_in_g, m_ln_in_b, m_w_in, m_s5_a_re, m_s5_a_im, m_s5_log_step, m_s5_b_re, m_s5_b_im, m_s5_c_re, m_s5_c_im, m_s5_d, m_s5_w_glu, m_s5_b_glu, m_ssd_conv_w, m_ssd_conv_b, m_ssd_dt_bias, m_ssd_a_log, m_ssd_d, m_ssd_norm_g, m_gdn_conv_w, m_gdn_dt_bias, m_gdn_a_log, m_gdn_norm_g, m_w_branch, m_b_gate, m_w_out, m_ln_g, m_ln_b, v_meta, v_ln_in_g, v_ln_in_b, v_w_in, v_s5_a_re, v_s5_a_im, v_s5_log_step, v_s5_b_re, v_s5_b_im, v_s5_c_re, v_s5_c_im, v_s5_d, v_s5_w_glu, v_s5_b_glu, v_ssd_conv_w, v_ssd_conv_b, v_ssd_dt_bias, v_ssd_a_log, v_ssd_d, v_ssd_norm_g, v_gdn_conv_w, v_gdn_dt_bias, v_gdn_a_log, v_gdn_norm_g, v_w_branch, v_b_gate, v_w_out, v_ln_g, v_ln_b):
    w_loc = dict(zip(WEIGHTS, (meta, ln_in_g, ln_in_b, w_in, s5_a_re, s5_a_im, s5_log_step, s5_b_re, s5_b_im, s5_c_re, s5_c_im, s5_d, s5_w_glu, s5_b_glu, ssd_conv_w, ssd_conv_b, ssd_dt_bias, ssd_a_log, ssd_d, ssd_norm_g, gdn_conv_w, gdn_dt_bias, gdn_a_log, gdn_norm_g, w_branch, b_gate, w_out, ln_g, ln_b)))
    m_loc = dict(zip(WEIGHTS, (m_meta, m_ln_in_g, m_ln_in_b, m_w_in, m_s5_a_re, m_s5_a_im, m_s5_log_step, m_s5_b_re, m_s5_b_im, m_s5_c_re, m_s5_c_im, m_s5_d, m_s5_w_glu, m_s5_b_glu, m_ssd_conv_w, m_ssd_conv_b, m_ssd_dt_bias, m_ssd_a_log, m_ssd_d, m_ssd_norm_g, m_gdn_conv_w, m_gdn_dt_bias, m_gdn_a_log, m_gdn_norm_g, m_w_branch, m_b_gate, m_w_out, m_ln_g, m_ln_b)))
    v_loc = dict(zip(WEIGHTS, (v_meta, v_ln_in_g, v_ln_in_b, v_w_in, v_s5_a_re, v_s5_a_im, v_s5_log_step, v_s5_b_re, v_s5_b_im, v_s5_c_re, v_s5_c_im, v_s5_d, v_s5_w_glu, v_s5_b_glu, v_ssd_conv_w, v_ssd_conv_b, v_ssd_dt_bias, v_ssd_a_log, v_ssd_d, v_ssd_norm_g, v_gdn_conv_w, v_gdn_dt_bias, v_gdn_a_log, v_gdn_norm_g, v_w_branch, v_b_gate, v_w_out, v_ln_g, v_ln_b)))
    return _step(x, loss_target, w_loc, m_loc, v_loc)
```

```python
import functools
import math

import jax
import jax.numpy as jnp
from jax import lax
from jax.experimental import pallas as pl
from jax.experimental.pallas import tpu as pltpu

F32 = jnp.float32
BF16 = jnp.bfloat16

N_DEV = 8
LANES = 128
SUBLANES = 8
VMEM_LIMIT = 56 * 1024 * 1024
FLAT_ROWS = 1024

CHUNK = 64
CONV_K = 4
S5_GROUP = 16
S5_STATE = 64
S5_BLOCK_GROUPS = 8
SSD_HEAD = 64
SSD_HEADS = 12
SSD_GROUPS = 2
SSD_STATE = 128
GDN_HEAD = 128
GDN_HEADS = 6
BRANCH = 768
LN_EPS = 1e-5

ADAM_LR = 0.001
ADAM_B1 = 0.9
ADAM_B2 = 0.999
ADAM_EPS = 1e-08
ADAM_WD = 0.01
ADAM_STEP = 10

WEIGHTS = ['meta', 'ln_in_g', 'ln_in_b', 'w_in', 's5_a_re', 's5_a_im', 's5_log_step', 's5_b_re', 's5_b_im',
           's5_c_re', 's5_c_im', 's5_d', 's5_w_glu', 's5_b_glu', 'ssd_conv_w', 'ssd_conv_b', 'ssd_dt_bias',
           'ssd_a_log', 'ssd_d', 'ssd_norm_g', 'gdn_conv_w', 'gdn_dt_bias', 'gdn_a_log', 'gdn_norm_g',
           'w_branch', 'b_gate', 'w_out', 'ln_g', 'ln_b']
SHARD_DIM = {'meta': 1, 'w_in': 2, 's5_w_glu': 1, 'ssd_conv_w': 2, 'gdn_conv_w': 2, 'w_branch': 3, 'b_gate': 2,
             'w_out': 1}


def _params(sem):
    return pltpu.CompilerParams(dimension_semantics=sem, vmem_limit_bytes=VMEM_LIMIT)


def _row_tile(m, cap):
    best = None
    for t in range(SUBLANES, min(m, cap) + 1, SUBLANES):
        if m % t == 0:
            best = t
    return best if best is not None else m


def _col_tile(n, cap):
    best = None
    for t in range(LANES, min(n, cap) + 1, LANES):
        if n % t == 0:
            best = t
    return best if best is not None else n


def _any_tile(m, cap):
    best = 1
    for t in range(1, min(m, cap) + 1):
        if m % t == 0:
            best = t
    return best


def _mm_fwd(a, b, name):
    m, _ = a.shape
    g, k, n = b.shape
    tm, tn = _row_tile(m, 832), _col_tile(n, 1024)
    nj = n // tn

    def body(a_ref, b_ref, o_ref):
        o_ref[...] = jnp.dot(a_ref[...].astype(BF16), b_ref[0].astype(BF16), preferred_element_type=F32)

    return pl.pallas_call(
        body, name=name, grid=(g, m // tm, nj),
        in_specs=[pl.BlockSpec((tm, k), lambda gi, i, j: (i, gi)),
                  pl.BlockSpec((1, k, tn), lambda gi, i, j: (gi, 0, j))],
        out_specs=pl.BlockSpec((tm, tn), lambda gi, i, j: (i, gi * nj + j)),
        out_shape=jax.ShapeDtypeStruct((m, g * n), F32),
        compiler_params=_params(("arbitrary", "arbitrary", "arbitrary")),
    )(a, b)


def _mm_da(ct, b, name):
    m, _ = ct.shape
    g, k, n = b.shape
    tm, tk = _row_tile(m, 832 if n <= 1536 else 416), _col_tile(k, 1024)
    nk = k // tk

    def body(c_ref, b_ref, o_ref):
        o_ref[...] = lax.dot_general(c_ref[...].astype(BF16), b_ref[0].astype(BF16), (((1,), (1,)), ((), ())),
                                     preferred_element_type=F32)

    return pl.pallas_call(
        body, name=name, grid=(g, m // tm, nk),
        in_specs=[pl.BlockSpec((tm, n), lambda gi, i, j: (i, gi)),
                  pl.BlockSpec((1, tk, n), lambda gi, i, j: (gi, j, 0))],
        out_specs=pl.BlockSpec((tm, tk), lambda gi, i, j: (i, gi * nk + j)),
        out_shape=jax.ShapeDtypeStruct((m, g * k), F32),
        compiler_params=_params(("arbitrary", "arbitrary", "arbitrary")),
    )(ct, b)


def _mm_db(a, ct, g, k, n, name):
    m = a.shape[0]
    tm, tk, tn = _row_tile(m, 832), _col_tile(k, 1024), _col_tile(n, 1280)
    nk, nn = k // tk, n // tn

    def body(a_ref, c_ref, o_ref):
        @pl.when(pl.program_id(3) == 0)
        def _():
            o_ref[...] = jnp.zeros_like(o_ref)

        o_ref[0] += lax.dot_general(a_ref[...].astype(BF16), c_ref[...].astype(BF16), (((0,), (0,)), ((), ())),
                                    preferred_element_type=F32)

    return pl.pallas_call(
        body, name=name, grid=(g, nk, nn, m // tm),
        in_specs=[pl.BlockSpec((tm, tk), lambda gi, i, j, r: (r, gi * nk + i)),
                  pl.BlockSpec((tm, tn), lambda gi, i, j, r: (r, gi * nn + j))],
        out_specs=pl.BlockSpec((1, tk, tn), lambda gi, i, j, r: (gi, i, j)),
        out_shape=jax.ShapeDtypeStruct((g, k, n), F32),
        compiler_params=_params(("arbitrary", "arbitrary", "arbitrary", "arbitrary")),
    )(a, ct)


def _make_gmm(name):
    @jax.custom_vjp
    def gmm(a, b):
        return _mm_fwd(a, b, name + "_fwd")

    def fwd(a, b):
        return _mm_fwd(a, b, name + "_fwd"), (a, b)

    def bwd(res, ct):
        a, b = res
        g, k, n = b.shape
        return _mm_da(ct, b, name + "_da"), _mm_db(a, ct, g, k, n, name + "_db")

    gmm.defvjp(fwd, bwd)
    return gmm


def _mm_da_sum(cts, ws, name):
    m, k = cts[0].shape[0], ws[0].shape[0]
    tm = _row_tile(m, 208)
    n_p = len(cts)

    def body(*refs):
        acc = None
        for c_ref, w_ref in zip(refs[:n_p], refs[n_p:2 * n_p]):
            term = lax.dot_general(c_ref[...].astype(BF16), w_ref[...].astype(BF16), (((1,), (1,)), ((), ())),
                                   preferred_element_type=F32)
            acc = term if acc is None else acc + term
        refs[2 * n_p][...] = acc

    return pl.pallas_call(
        body, name=name, grid=(m // tm,),
        in_specs=[pl.BlockSpec((tm, c.shape[1]), lambda i: (i, 0)) for c in cts]
        + [pl.BlockSpec(w.shape, lambda i: (0, 0), pipeline_mode=pl.Buffered(1)) for w in ws],
        out_specs=pl.BlockSpec((tm, k), lambda i: (i, 0)),
        out_shape=jax.ShapeDtypeStruct((m, k), F32),
        compiler_params=_params(("arbitrary",)),
    )(*cts, *ws)


def _in_proj(names, a, a16, ws, carriers):
    def products(a16, ws):
        return tuple(_mm_fwd(a16, w[None], nm + "_fwd") for nm, w in zip(names, ws))

    @jax.custom_vjp
    def proj(a, a16, ws, carriers):
        return products(a16, ws)

    def fwd(a, a16, ws, carriers):
        return products(a16, ws), (a16, ws)

    def bwd(res, cts):
        a16, ws = res
        d_ws = tuple(_mm_db(a16, ct, 1, w.shape[0], w.shape[1], nm + "_db")[0] for nm, w, ct in zip(names, ws, cts))
        return (_mm_da_sum(list(cts), list(ws), "in_da"), jnp.zeros_like(a16),
                tuple(jnp.zeros_like(w) for w in ws), d_ws)

    proj.defvjp(fwd, bwd)
    return proj(a, a16, tuple(ws), tuple(carriers))


@jax.custom_vjp
def _wdot(x, w, carrier):
    return _dot(x, w)


def _wdot_fwd(x, w, carrier):
    return _dot(x, w), (x, w)


def _wdot_bwd(res, ct):
    x, w = res
    return _dot_nt(ct, w), jnp.zeros_like(w), _dot_tn(x, ct)


_wdot.defvjp(_wdot_fwd, _wdot_bwd)


def _make_rowwise(name, fn, n_row, n_par, out_widths, tm_cap, use_ridx=False, n_wt=0, out_dtypes=None):
    n_out = len(out_widths)
    out_dtypes = out_dtypes or [F32] * n_out
    n_in = n_row + n_par + n_wt

    def bind(tm):
        if not use_ridx:
            return fn
        ridx = pl.program_id(0) * tm + lax.broadcasted_iota(jnp.int32, (tm, 1), 0)
        return functools.partial(fn, ridx)

    def specs(args, tm):
        rows = [pl.BlockSpec((tm, a.shape[1]), lambda i: (i, 0)) for a in args[:n_row]]
        pars = [pl.BlockSpec(a.shape, lambda i: (0, 0)) for a in args[n_row:n_in]]
        return rows, pars

    def fwd_call(*args):
        t = args[0].shape[0]
        tm = _row_tile(t, tm_cap)
        rows, pars = specs(args, tm)

        def body(*refs):
            vals = [r[...] for r in refs[:n_row + n_par]]
            mats = [functools.partial(lambda x, w: _dot(x, w), w=r[...]) for r in refs[n_row + n_par:n_in]]
            res = bind(tm)(*vals, *mats)
            for o_ref, r in zip(refs[n_in:], res):
                o_ref[...] = r

        outs = pl.pallas_call(
            body, name=name + "_fwd", grid=(t // tm,), in_specs=rows + pars,
            out_specs=[pl.BlockSpec((tm, w), lambda i: (i, 0)) for w in out_widths],
            out_shape=[jax.ShapeDtypeStruct((t, w), dt) for w, dt in zip(out_widths, out_dtypes)],
            compiler_params=_params(("arbitrary",)),
        )(*args)
        return tuple(outs)

    def bwd_call(args, cts):
        t = args[0].shape[0]
        tm = _row_tile(t, tm_cap)
        rows, pars = specs(args, tm)
        ct_specs = [pl.BlockSpec((tm, w), lambda i: (i, 0)) for w in out_widths]
        n_diff = n_row + n_par

        def body(*refs):
            vals = [r[...] for r in refs[:n_diff]]
            wts = [r[...] for r in refs[n_diff:n_in]]
            ct_vals = tuple(r[...] for r in refs[n_in:n_in + n_out])
            d_refs = refs[n_in + n_out:]
            f = bind(tm)

            def g(*a):
                mats = [functools.partial(lambda x, w, c: _wdot(x, w, c), w=w, c=c)
                        for w, c in zip(wts, a[n_diff:])]
                return tuple(f(*a[:n_diff], *mats))

            _, vjp = jax.vjp(g, *vals, *[jnp.zeros(w.shape, F32) for w in wts])
            grads = vjp(ct_vals)
            for i in range(n_row):
                d_refs[i][...] = grads[i]
            if n_par + n_wt:
                @pl.when(pl.program_id(0) == 0)
                def _():
                    for r in d_refs[n_row:]:
                        r[...] = jnp.zeros_like(r)

                for r, gr in zip(d_refs[n_row:], grads[n_row:]):
                    r[...] += gr

        outs = pl.pallas_call(
            body, name=name + "_bwd", grid=(t // tm,), in_specs=rows + pars + ct_specs,
            out_specs=rows + pars,
            out_shape=[jax.ShapeDtypeStruct(a.shape, F32) for a in args],
            compiler_params=_params(("arbitrary",)),
        )(*args, *cts)
        return tuple(outs)

    @jax.custom_vjp
    def op(*args):
        return fwd_call(*args[:n_in])

    def fwd(*args):
        return fwd_call(*args[:n_in]), args[:n_in]

    def bwd(args, cts):
        grads = bwd_call(args, cts)
        return grads[:n_row + n_par] + tuple(jnp.zeros_like(a) for a in args[n_row + n_par:]) + grads[n_row + n_par:]

    op.defvjp(fwd, bwd)
    return op


HALO = SUBLANES


def _make_conv_rowwise(name, fn, n_par, out_width, tm_cap, use_ridx=False):
    def bind(ridx):
        return functools.partial(fn, ridx) if use_ridx else fn

    def stage(x_ref, halo_ref, xs, first):
        xs[0:HALO, :] = jnp.where(first, 0.0, halo_ref[...])
        xs[HALO:, :] = x_ref[...]

    def taps(xs, tm):
        return [xs[pl.ds(HALO - (CONV_K - 1) + j, tm), :] for j in range(CONV_K)]

    def fwd_call(x, *pars):
        t, wd = x.shape
        tm = _row_tile(t, tm_cap)
        per = tm // HALO

        def body(*refs):
            x_ref, halo_ref = refs[:2]
            par_refs, o_ref, xs = refs[2:2 + n_par], refs[2 + n_par], refs[-1]
            i = pl.program_id(0)
            stage(x_ref, halo_ref, xs, i == 0)
            ridx = i * tm + lax.broadcasted_iota(jnp.int32, (tm, 1), 0)
            (o_ref[...],) = bind(ridx)(*taps(xs, tm), *[r[...] for r in par_refs])

        return pl.pallas_call(
            body, name=name + "_fwd", grid=(t // tm,),
            in_specs=[pl.BlockSpec((tm, wd), lambda i: (i, 0)),
                      pl.BlockSpec((HALO, wd), lambda i: (jnp.maximum(i * per - 1, 0), 0))]
            + [pl.BlockSpec(p.shape, lambda i: (0, 0)) for p in pars],
            out_specs=pl.BlockSpec((tm, out_width), lambda i: (i, 0)),
            out_shape=jax.ShapeDtypeStruct((t, out_width), F32),
            scratch_shapes=[pltpu.VMEM((tm + HALO, wd), F32)],
            compiler_params=_params(("arbitrary",)),
        )(x, x, *pars)

    def bwd_call(x, pars, ct):
        t, wd = x.shape
        tm = _row_tile(t, tm_cap)
        per = tm // HALO
        nb = t // tm

        def body(*refs):
            x_ref, halo_ref = refs[:2]
            par_refs, ct_ref = refs[2:2 + n_par], refs[2 + n_par]
            dx_ref, dpar_refs = refs[3 + n_par], refs[4 + n_par:4 + 2 * n_par]
            xs, ds, carry = refs[-3:]
            step = pl.program_id(0)
            blk = nb - 1 - step

            @pl.when(step == 0)
            def _():
                carry[...] = jnp.zeros_like(carry)
                for r in dpar_refs:
                    r[...] = jnp.zeros_like(r)

            stage(x_ref, halo_ref, xs, blk == 0)
            ridx = blk * tm + lax.broadcasted_iota(jnp.int32, (tm, 1), 0)
            f = bind(ridx)
            _, vjp = jax.vjp(lambda *a: tuple(f(*a)), *taps(xs, tm), *[r[...] for r in par_refs])
            grads = vjp((ct_ref[...],))
            ds[...] = jnp.zeros_like(ds)
            for j in range(CONV_K):
                ds[pl.ds(HALO - (CONV_K - 1) + j, tm), :] += grads[j]
            ds[pl.ds(tm, HALO), :] += carry[...]
            dx_ref[...] = ds[HALO:, :]
            carry[...] = ds[0:HALO, :]
            for r, g in zip(dpar_refs, grads[CONV_K:]):
                r[...] += g

        rev = lambda i: (nb - 1 - i, 0)
        outs = pl.pallas_call(
            body, name=name + "_bwd", grid=(nb,),
            in_specs=[pl.BlockSpec((tm, wd), rev),
                      pl.BlockSpec((HALO, wd), lambda i: (jnp.maximum((nb - 1 - i) * per - 1, 0), 0))]
            + [pl.BlockSpec(p.shape, lambda i: (0, 0)) for p in pars]
            + [pl.BlockSpec((tm, out_width), rev)],
            out_specs=[pl.BlockSpec((tm, wd), rev)] + [pl.BlockSpec(p.shape, lambda i: (0, 0)) for p in pars],
            out_shape=[jax.ShapeDtypeStruct(x.shape, F32)] + [jax.ShapeDtypeStruct(p.shape, F32) for p in pars],
            scratch_shapes=[pltpu.VMEM((tm + HALO, wd), F32), pltpu.VMEM((tm + HALO, wd), F32),
                            pltpu.VMEM((HALO, wd), F32)],
            compiler_params=_params(("arbitrary",)),
        )(x, x, *pars, ct)
        return tuple(outs)

    @jax.custom_vjp
    def op(x, *pars):
        return fwd_call(x, *pars)

    def fwd(x, *pars):
        return fwd_call(x, *pars), (x, pars)

    def bwd(res, ct):
        x, pars = res
        return bwd_call(x, pars, ct)

    op.defvjp(fwd, bwd)
    return op


STEP_CHUNKS = 5


def _make_chunk_scan(name, chunk_fn, state_shape, n_seq, n_par, out_widths):
    n_out = len(out_widths)
    zeros_idx = (0,) * len(state_shape)

    def plan(t):
        per_step = _any_tile(t // CHUNK, STEP_CHUNKS)
        return per_step, per_step * CHUNK, t // (per_step * CHUNK)

    def fn(state, *args):
        per_step = args[0].shape[0] // CHUNK
        outs = []
        for c in range(per_step):
            rows = slice(c * CHUNK, (c + 1) * CHUNK)
            res = chunk_fn(state, *[a[rows, :] for a in args[:n_seq]], *args[n_seq:])
            state = res[0]
            outs.append(res[1:])
        return (state,) + tuple(jnp.concatenate([o[i] for o in outs], axis=0) for i in range(n_out))

    def fwd_call(*args):
        t = args[0].shape[0]
        _, rows_per_step, nc = plan(t)
        seq_specs = [pl.BlockSpec((rows_per_step, a.shape[1]), lambda c: (c, 0)) for a in args[:n_seq]]
        par_specs = [pl.BlockSpec(a.shape, lambda c: (0, 0)) for a in args[n_seq:]]

        def body(*refs):
            ins = refs[:n_seq + n_par]
            out_refs = refs[n_seq + n_par:n_seq + n_par + n_out]
            states_ref = refs[n_seq + n_par + n_out]
            st = refs[-1]

            @pl.when(pl.program_id(0) == 0)
            def _():
                st[...] = jnp.zeros_like(st)

            s0 = st[...]
            states_ref[0] = s0
            res = fn(s0, *[r[...] for r in ins])
            st[...] = res[0]
            for o_ref, r in zip(out_refs, res[1:]):
                o_ref[...] = r

        outs = pl.pallas_call(
            body, name=name + "_fwd", grid=(nc,), in_specs=seq_specs + par_specs,
            out_specs=[pl.BlockSpec((rows_per_step, w), lambda c: (c, 0)) for w in out_widths]
            + [pl.BlockSpec((1,) + state_shape, lambda c: (c,) + zeros_idx)],
            out_shape=[jax.ShapeDtypeStruct((t, w), F32) for w in out_widths]
            + [jax.ShapeDtypeStruct((nc,) + state_shape, F32)],
            scratch_shapes=[pltpu.VMEM(state_shape, F32)],
            compiler_params=_params(("arbitrary",)),
        )(*args)
        return tuple(outs[:n_out]), outs[n_out]

    def bwd_call(args, states, cts):
        t = args[0].shape[0]
        _, rows_per_step, nc = plan(t)
        rev = lambda c: (nc - 1 - c, 0)
        seq_specs = [pl.BlockSpec((rows_per_step, a.shape[1]), rev) for a in args[:n_seq]]
        par_specs = [pl.BlockSpec(a.shape, lambda c: (0, 0)) for a in args[n_seq:]]
        ct_specs = [pl.BlockSpec((rows_per_step, w), rev) for w in out_widths]
        st_spec = pl.BlockSpec((1,) + state_shape, lambda c: (nc - 1 - c,) + zeros_idx)
        n_in = n_seq + n_par

        def body(*refs):
            vals = [r[...] for r in refs[:n_in]]
            s0 = refs[n_in][0]
            ct_vals = tuple(r[...] for r in refs[n_in + 1:n_in + 1 + n_out])
            d_refs = refs[n_in + 1 + n_out:-1]
            dst = refs[-1]

            @pl.when(pl.program_id(0) == 0)
            def _():
                dst[...] = jnp.zeros_like(dst)
                for j in range(n_par):
                    d_refs[n_seq + j][...] = jnp.zeros_like(d_refs[n_seq + j])

            _, vjp = jax.vjp(lambda *a: tuple(fn(*a)), s0, *vals)
            grads = vjp((dst[...],) + ct_vals)
            dst[...] = grads[0]
            for i in range(n_seq):
                d_refs[i][...] = grads[1 + i]
            for j in range(n_par):
                d_refs[n_seq + j][...] += grads[1 + n_seq + j]

        outs = pl.pallas_call(
            body, name=name + "_bwd", grid=(nc,), in_specs=seq_specs + par_specs + [st_spec] + ct_specs,
            out_specs=seq_specs + par_specs,
            out_shape=[jax.ShapeDtypeStruct(a.shape, F32) for a in args],
            scratch_shapes=[pltpu.VMEM(state_shape, F32)],
            compiler_params=_params(("arbitrary",)),
        )(*args, states, *cts)
        return tuple(outs)

    @jax.custom_vjp
    def op(*args):
        return fwd_call(*args)[0]

    def fwd(*args):
        outs, states = fwd_call(*args)
        return outs, (args, states)

    def bwd(res, cts):
        args, states = res
        return bwd_call(args, states, cts)

    op.defvjp(fwd, bwd)
    return op


def _s5_scan_fwd(bre, bim, are, aim, name):
    t, r, _ = bre.shape
    tb = _any_tile(t, 208)
    blk = pl.BlockSpec((tb, r, LANES), lambda i: (i, 0, 0))
    par = pl.BlockSpec((r, LANES), lambda i: (0, 0))

    def body(bre_ref, bim_ref, are_ref, aim_ref, sre_ref, sim_ref, st):
        @pl.when(pl.program_id(0) == 0)
        def _():
            st[...] = jnp.zeros_like(st)

        ar, ai = are_ref[...], aim_ref[...]

        def step(k, carry):
            sr, si = carry
            nr = ar * sr - ai * si + bre_ref[k]
            ni = ar * si + ai * sr + bim_ref[k]
            sre_ref[k] = nr
            sim_ref[k] = ni
            return nr, ni

        sr, si = lax.fori_loop(0, tb, step, (st[0], st[1]), unroll=4)
        st[0] = sr
        st[1] = si

    return pl.pallas_call(
        body, name=name, grid=(t // tb,), in_specs=[blk, blk, par, par], out_specs=[blk, blk],
        out_shape=[jax.ShapeDtypeStruct(bre.shape, F32)] * 2,
        scratch_shapes=[pltpu.VMEM((2, r, LANES), F32)],
        compiler_params=_params(("arbitrary",)),
    )(bre, bim, are, aim)


def _s5_scan_bwd(dsr, dsi, sre, sim, are, aim, name):
    t, r, _ = sre.shape
    tb = _any_tile(t, 208)
    nb = t // tb
    blk = pl.BlockSpec((tb, r, LANES), lambda i: (nb - 1 - i, 0, 0))
    par = pl.BlockSpec((r, LANES), lambda i: (0, 0))

    def body(dsr_ref, dsi_ref, sre_ref, sim_ref, are_ref, aim_ref, gre_ref, gim_ref, dar_ref, dai_ref, st):
        @pl.when(pl.program_id(0) == 0)
        def _():
            st[...] = jnp.zeros_like(st)
            dar_ref[...] = jnp.zeros_like(dar_ref)
            dai_ref[...] = jnp.zeros_like(dai_ref)

        ar, ai = are_ref[...], aim_ref[...]

        def step(k, carry):
            gr, gi, dar, dai = carry
            q = tb - 1 - k
            s_r, s_i = sre_ref[q], sim_ref[q]
            dar = dar + gr * s_r + gi * s_i
            dai = dai + gi * s_r - gr * s_i
            ngr = dsr_ref[q] + ar * gr + ai * gi
            ngi = dsi_ref[q] + ar * gi - ai * gr
            gre_ref[q] = ngr
            gim_ref[q] = ngi
            return ngr, ngi, dar, dai

        gr, gi, dar, dai = lax.fori_loop(0, tb, step, (st[0], st[1], dar_ref[...], dai_ref[...]), unroll=4)
        st[0] = gr
        st[1] = gi
        dar_ref[...] = dar
        dai_ref[...] = dai

    return pl.pallas_call(
        body, name=name, grid=(nb,), in_specs=[blk, blk, blk, blk, par, par], out_specs=[blk, blk, par, par],
        out_shape=[jax.ShapeDtypeStruct(sre.shape, F32)] * 2 + [jax.ShapeDtypeStruct(are.shape, F32)] * 2,
        scratch_shapes=[pltpu.VMEM((2, r, LANES), F32)],
        compiler_params=_params(("arbitrary",)),
    )(dsr, dsi, sre, sim, are, aim)


def _make_s5_scan(name):
    @jax.custom_vjp
    def scan(bre, bim, are, aim):
        return tuple(_s5_scan_fwd(bre, bim, are, aim, name + "_fwd"))

    def fwd(bre, bim, are, aim):
        sre, sim = _s5_scan_fwd(bre, bim, are, aim, name + "_fwd")
        return (sre, sim), (sre, sim, are, aim)

    def bwd(res, cts):
        sre, sim, are, aim = res
        return tuple(_s5_scan_bwd(cts[0], cts[1], sre, sim, are, aim, name + "_bwd"))

    scan.defvjp(fwd, bwd)
    return scan


def _dot(a, b):
    return jnp.dot(a.astype(BF16), b.astype(BF16), preferred_element_type=F32)


def _dot_nt(a, b):
    return lax.dot_general(a.astype(BF16), b.astype(BF16), (((1,), (1,)), ((), ())), preferred_element_type=F32)


def _dot_tn(a, b):
    return lax.dot_general(a.astype(BF16), b.astype(BF16), (((0,), (0,)), ((), ())), preferred_element_type=F32)


def _split3(x):
    x1 = x.astype(BF16)
    rest = x - x1.astype(F32)
    x2 = rest.astype(BF16)
    return x1, x2, (rest - x2.astype(F32)).astype(BF16)


def _sel_dot(dims, a, b, a_is_sel):
    sel = (a if a_is_sel else b).astype(BF16)
    acc = None
    for piece in _split3(b if a_is_sel else a):
        pair = (sel, piece) if a_is_sel else (piece, sel)
        term = lax.dot_general(*pair, (dims, ((), ())), preferred_element_type=F32)
        acc = term if acc is None else acc + term
    return acc


@jax.custom_vjp
def _running_sum(low, y):
    return _sel_dot(((1,), (0,)), low, y, True)


_running_sum.defvjp(lambda low, y: (_running_sum(low, y), low),
                    lambda low, ct: (jnp.zeros_like(low), _sel_dot(((0,), (0,)), low, ct, True)))


@jax.custom_vjp
def _spread(x, sel):
    return _sel_dot(((1,), (0,)), x, sel, False)


_spread.defvjp(lambda x, sel: (_spread(x, sel), sel),
               lambda sel, ct: (_sel_dot(((1,), (1,)), ct, sel, False), jnp.zeros_like(sel)))


@jax.custom_vjp
def _transposed(x, eye):
    return _sel_dot(((0,), (0,)), x, eye, False)


_transposed.defvjp(lambda x, eye: (_transposed(x, eye), eye),
                   lambda eye, ct: (_sel_dot(((1,), (1,)), eye, ct, True), jnp.zeros_like(eye)))


def _iota(shape, dim):
    return lax.broadcasted_iota(jnp.int32, shape, dim)


def _tri(strict=False):
    r, c = _iota((CHUNK, CHUNK), 0), _iota((CHUNK, CHUNK), 1)
    return (r > c) if strict else (r >= c)


def _silu(x):
    return x * jax.nn.sigmoid(x)


def _layer_norm(z, g, b):
    mu = jnp.mean(z, axis=-1, keepdims=True)
    var = jnp.mean(jnp.square(z - mu), axis=-1, keepdims=True)
    return (z - mu) * lax.rsqrt(var + LN_EPS) * g + b


def _ssd_chunk(state, xbc, sm, a_c, d_exp):
    width = SSD_HEADS * SSD_HEAD
    x = xbc[:, :width]
    lane = _iota((CHUNK, LANES), 1)
    dtc = jnp.where(lane < SSD_HEADS, sm, 0.0)
    low = _tri().astype(F32)
    eye = (_iota((CHUNK, CHUNK), 0) == _iota((CHUNK, CHUNK), 1)).astype(F32)
    head_col, head_row = _iota((LANES, width), 1), _iota((LANES, width), 0) * SSD_HEAD
    expand = ((head_col >= head_row) & (head_col < head_row + SSD_HEAD)).astype(F32)
    acum_c = _running_sum(low, dtc * a_c)
    acum_ct = _transposed(acum_c, eye)
    dt_exp = _spread(dtc, expand)
    acum = _spread(acum_c, expand)
    xd = x * dt_exp
    last = acum[CHUNK - 1:CHUNK, :]
    to_end = jnp.exp(last - acum)
    eac = jnp.exp(acum)
    causal = _tri()
    first_half = _iota((CHUNK, LANES), 1) < SSD_HEAD
    top_rows = _iota((LANES, LANES), 0) < SSD_HEAD
    pairs = range(SSD_HEADS // 2)
    grp = [(2 * p) // (SSD_HEADS // SSD_GROUPS) for p in pairs]
    cols = [slice(p * LANES, (p + 1) * LANES) for p in pairs]
    bg = [xbc[:, width + g * SSD_STATE: width + (g + 1) * SSD_STATE] for g in range(SSD_GROUPS)]
    cg = [xbc[:, width + (SSD_GROUPS + g) * SSD_STATE: width + (SSD_GROUPS + g + 1) * SSD_STATE]
          for g in range(SSD_GROUPS)]
    scores = [_dot_nt(cg[g], bg[g]) for g in range(SSD_GROUPS)]
    dec = [jnp.where(causal, jnp.exp(jnp.minimum(acum_c[:, h:h + 1] - acum_ct[h:h + 1, :], 0.0)), 0.0)
           for h in range(SSD_HEADS)]
    y_lo = [_dot(scores[grp[p]] * dec[2 * p], jnp.where(first_half, xd[:, cols[p]], 0.0)) for p in pairs]
    y_hi = [_dot(scores[grp[p]] * dec[2 * p + 1], jnp.where(first_half, 0.0, xd[:, cols[p]])) for p in pairs]
    s_prev = [state[p * LANES:(p + 1) * LANES, :] for p in pairs]
    y_off = [_dot_nt(cg[grp[p]], s_prev[p]) for p in pairs]
    s_add = [_dot_tn(xd[:, cols[p]] * to_end[:, cols[p]], bg[grp[p]]) for p in pairs]
    ys = [y_lo[p] + y_hi[p] + y_off[p] * eac[:, cols[p]] + x[:, cols[p]] * d_exp[:, cols[p]] for p in pairs]
    cd = [jnp.where(top_rows, jnp.exp(acum_c[CHUNK - 1:CHUNK, 2 * p:2 * p + 1]),
                    jnp.exp(acum_c[CHUNK - 1:CHUNK, 2 * p + 1:2 * p + 2])) for p in pairs]
    new_states = [s_prev[p] * cd[p] + s_add[p] for p in pairs]
    return jnp.concatenate(new_states, axis=0), jnp.concatenate(ys, axis=1)


def _neumann(a_mats, rhs, transposed):
    dot = _dot_tn if transposed else _dot
    nmats, sols = [-a for a in a_mats], list(rhs)
    for i in range(6):
        upd = [dot(n, s) for n, s in zip(nmats, sols)]
        if i < 5:
            nmats = [_dot(n, n) for n in nmats]
        sols = [s + u for s, u in zip(sols, upd)]
    return tuple(sols)


@jax.custom_vjp
def _solve_unit_lower(a_mats, rhs):
    return _neumann(a_mats, rhs, False)


def _solve_unit_lower_fwd(a_mats, rhs):
    sols = _neumann(a_mats, rhs, False)
    return sols, (a_mats, sols)


def _solve_unit_lower_bwd(res, d_sols):
    a_mats, sols = res
    d_rhs = _neumann(a_mats, d_sols, True)
    return tuple(-_dot_nt(dr, x) for dr, x in zip(d_rhs, sols)), d_rhs


_solve_unit_lower.defvjp(_solve_unit_lower_fwd, _solve_unit_lower_bwd)


def _gdn_chunk(state, qkv, sm):
    width = GDN_HEADS * GDN_HEAD
    g0, b0 = SSD_HEADS, SSD_HEADS + GDN_HEADS
    lane = _iota((CHUNK, LANES), 1)
    gc = jnp.where((lane >= g0) & (lane < b0), sm, 0.0)
    low = _tri().astype(F32)
    eye = (_iota((CHUNK, CHUNK), 0) == _iota((CHUNK, CHUNK), 1)).astype(F32)
    gcum = _running_sum(low, gc)
    gcum_t = _transposed(gcum, eye)
    causal, strict = _tri(), _tri(strict=True)
    heads = range(GDN_HEADS)
    q = [qkv[:, h * GDN_HEAD:(h + 1) * GDN_HEAD] for h in heads]
    k = [qkv[:, width + h * GDN_HEAD: width + (h + 1) * GDN_HEAD] for h in heads]
    v = [qkv[:, 2 * width + h * GDN_HEAD: 2 * width + (h + 1) * GDN_HEAD] for h in heads]
    beta = [sm[:, b0 + h:b0 + h + 1] for h in heads]
    gcol = [gcum[:, g0 + h:g0 + h + 1] for h in heads]
    glast = [gcum[CHUNK - 1:CHUNK, g0 + h:g0 + h + 1] for h in heads]
    gamma = [jnp.where(causal, jnp.exp(jnp.minimum(gcol[h] - gcum_t[g0 + h:g0 + h + 1, :], 0.0)), 0.0) for h in heads]
    kk = [_dot_nt(k[h], k[h]) for h in heads]
    qk = [_dot_nt(q[h], k[h]) for h in heads]
    egc = [jnp.exp(gcol[h]) for h in heads]
    a_mat = tuple(jnp.where(strict, kk[h] * gamma[h] * beta[h], 0.0) for h in heads)
    sol = _solve_unit_lower(a_mat, tuple(jnp.concatenate([v[h] * beta[h], k[h] * (beta[h] * egc[h])], axis=1)
                                         for h in heads))
    s_prev = [state[h * GDN_HEAD:(h + 1) * GDN_HEAD, :] for h in heads]
    w_s = [_dot(sol[h][:, GDN_HEAD:], s_prev[h]) for h in heads]
    q_s = [_dot(q[h] * egc[h], s_prev[h]) for h in heads]
    v_new = [sol[h][:, :GDN_HEAD] - w_s[h] for h in heads]
    a_v = [_dot(qk[h] * gamma[h], v_new[h]) for h in heads]
    k_v = [_dot_tn(k[h] * jnp.exp(glast[h] - gcol[h]), v_new[h]) for h in heads]
    outs = [q_s[h] + a_v[h] for h in heads]
    new_states = [s_prev[h] * jnp.exp(glast[h]) + k_v[h] for h in heads]
    return jnp.concatenate(new_states, axis=0), jnp.concatenate(outs, axis=1)


def _row_fns(d_model, pad_rows, loss_rows, alpha):
    def keep(ridx, v):
        return jnp.where(ridx >= pad_rows, v, 0.0)

    def both(h):
        return h, h.astype(BF16)

    def ln_in(ridx, h, g, b):
        return both(keep(ridx, _layer_norm(h, g, b)))

    def s5_tail(y_re, y_im, u, z, d, bias, glu, branch):
        v0 = jax.nn.gelu(y_re + y_im + d * u)
        return (branch(v0 * jax.nn.sigmoid(glu(v0) + bias) * _silu(z)),)

    def small_act(ridx, raw, bias, scale):
        lane = _iota(raw.shape, 1)
        sp = jax.nn.softplus(raw + bias)
        g0, b0 = SSD_HEADS, SSD_HEADS + GDN_HEADS
        out = jnp.where(lane < g0, sp, jnp.where(lane < b0, scale * sp,
                                                 jnp.where(lane < b0 + GDN_HEADS, jax.nn.sigmoid(raw), 0.0)))
        return (keep(ridx, out),)

    def conv(xs, w):
        acc = xs[0] * w[0:1, :]
        for j in range(1, CONV_K):
            acc = acc + xs[j] * w[j:j + 1, :]
        return acc

    def ssd_conv(ridx, x0, x1, x2, x3, w, b):
        return (keep(ridx, _silu(conv((x0, x1, x2, x3), w) + b)),)

    def ssd_tail(y, z, g, branch):
        v = y * _silu(z)
        return (branch(v * lax.rsqrt(jnp.mean(v * v, axis=-1, keepdims=True) + LN_EPS) * g),)

    def gdn_conv(x0, x1, x2, x3, w):
        a = _silu(conv((x0, x1, x2, x3), w))
        width = GDN_HEADS * GDN_HEAD
        parts = []
        for h in range(2 * GDN_HEADS):
            z = a[:, h * GDN_HEAD:(h + 1) * GDN_HEAD]
            z = z * lax.rsqrt(jnp.sum(z * z, axis=-1, keepdims=True) + 1e-6)
            parts.append(z * GDN_HEAD ** -0.5 if h < GDN_HEADS else z)
        parts.append(a[:, 2 * width:])
        return (jnp.concatenate(parts, axis=1),)

    def gdn_tail(o, z, g, branch):
        parts = []
        for h in range(GDN_HEADS):
            cols = slice(h * GDN_HEAD, (h + 1) * GDN_HEAD)
            oh = o[:, cols]
            oh = oh * lax.rsqrt(jnp.mean(oh * oh, axis=-1, keepdims=True) + LN_EPS) * g
            parts.append(oh * _silu(z[:, cols]))
        return (branch(jnp.concatenate(parts, axis=1)),)

    def merge_out(ridx, oa, ob, oc, gate, h, bias, g, b, w_out):
        acc = None
        for k, o in enumerate((oa, ob, oc)):
            cols = slice(k * d_model, (k + 1) * d_model)
            term = jax.nn.sigmoid(gate[:, cols] + bias[:, cols]) * o
            acc = term if acc is None else acc + term
        return both(keep(ridx, _layer_norm(alpha * h + w_out(acc), g, b)))

    def loss_rows_fn(ridx, h, tgt):
        row = 0.5 * jnp.mean(jnp.square(h - tgt), axis=-1, keepdims=True)
        row = jnp.where(ridx >= loss_rows, row, 0.0)
        lane = _iota((h.shape[0], LANES), 1)
        return (jnp.where(lane == 0, row, 0.0),)

    return dict(ln_in=ln_in, s5_tail=s5_tail, small_act=small_act, ssd_conv=ssd_conv, ssd_tail=ssd_tail,
                gdn_conv=gdn_conv, gdn_tail=gdn_tail, merge_out=merge_out, loss=loss_rows_fn)


def _pair_exchange(srcs, whole_srcs, name):
    n_arr, n_whole = len(srcs), len(whole_srcs)
    n_chip = N_DEV // 2
    base = n_arr * n_chip

    def body(*refs):
        src_refs, recv_refs = refs[:n_arr + n_whole], refs[n_arr + n_whole:2 * (n_arr + n_whole)]
        send_sems, recv_sems = refs[2 * (n_arr + n_whole):]
        x, y, c = lax.axis_index("x"), lax.axis_index("y"), lax.axis_index("c")

        def to_sibling(src, dst, k):
            return pltpu.make_async_remote_copy(src_ref=src, dst_ref=dst, send_sem=send_sems.at[k],
                                                recv_sem=recv_sems.at[k], device_id=(x, y, 1 - c),
                                                device_id_type=pl.DeviceIdType.MESH)

        copies = [to_sibling(src_refs[a].at[q, 1 - c], recv_refs[a].at[q], a * n_chip + q)
                  for a in range(n_arr) for q in range(n_chip)]
        copies += [to_sibling(src_refs[n_arr + b], recv_refs[n_arr + b], base + b) for b in range(n_whole)]
        for cp in copies:
            cp.start()
        for cp in copies:
            cp.wait()

    n_sem = base + n_whole
    return pl.pallas_call(
        body, name=name,
        in_specs=[pl.BlockSpec(memory_space=pl.ANY)] * (n_arr + n_whole),
        out_specs=[pl.BlockSpec(memory_space=pl.ANY)] * (n_arr + n_whole),
        out_shape=[jax.ShapeDtypeStruct((n_chip,) + tuple(s.shape[2:]), s.dtype) for s in srcs]
        + [jax.ShapeDtypeStruct(s.shape, s.dtype) for s in whole_srcs],
        scratch_shapes=[pltpu.SemaphoreType.DMA((n_sem,)), pltpu.SemaphoreType.DMA((n_sem,))],
    )(*srcs, *whole_srcs)


def _pair_sum(src, recv, core, name):
    n_chip, rows, cols = recv.shape
    tr = _row_tile(rows, 1024 if cols <= LANES else 512)

    def body(core_ref, a_ref, b_ref, o_ref):
        o_ref[...] = (a_ref[0].astype(F32) + b_ref[...].astype(F32)).astype(o_ref.dtype)

    blk = pl.BlockSpec((1, tr, cols), lambda q, i, core_ref: (q, i, 0))
    return pl.pallas_call(
        body, name=name,
        grid_spec=pltpu.PrefetchScalarGridSpec(
            num_scalar_prefetch=1, grid=(n_chip, rows // tr),
            in_specs=[pl.BlockSpec((1, 1, tr, cols), lambda q, i, core_ref: (q, core_ref[0], i, 0)), blk],
            out_specs=blk),
        out_shape=jax.ShapeDtypeStruct(recv.shape, recv.dtype),
        compiler_params=_params(("arbitrary", "arbitrary")),
    )(core, src, recv)


def _add2(a, b, name):
    rows = a.shape[0]
    tr = _row_tile(rows, FLAT_ROWS)
    blk = pl.BlockSpec((tr, LANES), lambda i: (i, 0))

    def body(a_ref, b_ref, o_ref):
        o_ref[...] = a_ref[...] + b_ref[...]

    return pl.pallas_call(body, name=name, grid=(rows // tr,), in_specs=[blk, blk], out_specs=blk,
                          out_shape=jax.ShapeDtypeStruct(a.shape, F32), compiler_params=_params(("arbitrary",)))(a, b)


def _add_parts(parts, name):
    n, rows, _ = parts.shape
    tr = _row_tile(rows, FLAT_ROWS)

    def body(p_ref, o_ref):
        acc = p_ref[0]
        for k in range(1, n):
            acc = acc + p_ref[k]
        o_ref[...] = acc

    return pl.pallas_call(body, name=name, grid=(rows // tr,),
                          in_specs=[pl.BlockSpec((n, tr, LANES), lambda i: (0, i, 0))],
                          out_specs=pl.BlockSpec((tr, LANES), lambda i: (i, 0)),
                          out_shape=jax.ShapeDtypeStruct((rows, LANES), F32),
                          compiler_params=_params(("arbitrary",)))(parts)


def _chip_exchange(chip_srcs, whole_srcs, name):
    n_chip_arr, n_whole = len(chip_srcs), len(whole_srcs)
    n_arr = n_chip_arr + n_whole
    n_chip = N_DEV // 2
    chip_flips = [(1, 0), (0, 1), (1, 1)]

    def body(*refs):
        src_refs, out_refs = refs[:n_arr], refs[n_arr:2 * n_arr]
        send_sems, recv_sems = refs[2 * n_arr:]
        x, y, c = lax.axis_index("x"), lax.axis_index("y"), lax.axis_index("c")
        my_chip = 2 * x + y
        copies = []
        for a in range(n_arr):
            for k, (fx, fy) in enumerate(chip_flips):
                px = 1 - x if fx else x
                py = 1 - y if fy else y
                src = src_refs[a].at[2 * px + py] if a < n_chip_arr else src_refs[a]
                copies.append(pltpu.make_async_remote_copy(
                    src_ref=src, dst_ref=out_refs[a].at[my_chip],
                    send_sem=send_sems.at[a * 3 + k], recv_sem=recv_sems.at[a * 3 + k],
                    device_id=(px, py, c), device_id_type=pl.DeviceIdType.MESH))
        for cp in copies:
            cp.start()
        for cp in copies:
            cp.wait()

    n_sem = n_arr * len(chip_flips)
    return pl.pallas_call(
        body, name=name,
        in_specs=[pl.BlockSpec(memory_space=pl.ANY)] * n_arr, out_specs=[pl.BlockSpec(memory_space=pl.ANY)] * n_arr,
        out_shape=[jax.ShapeDtypeStruct(s.shape, s.dtype) for s in chip_srcs]
        + [jax.ShapeDtypeStruct((n_chip,) + tuple(s.shape), s.dtype) for s in whole_srcs],
        scratch_shapes=[pltpu.SemaphoreType.DMA((n_sem,)), pltpu.SemaphoreType.DMA((n_sem,))],
    )(*chip_srcs, *whole_srcs)


def _gather(srcs, name):
    n_arr = len(srcs)
    n_sem = N_DEV - 1

    def body(*refs):
        src_refs, out_refs = refs[:n_arr], refs[n_arr:2 * n_arr]
        send_sems, recv_sems = refs[2 * n_arr:]
        x, y, c = lax.axis_index("x"), lax.axis_index("y"), lax.axis_index("c")
        me, sibling = (x, y, c), (x, y, 1 - c)
        chips = [(1 - x, y), (x, 1 - y), (1 - x, 1 - y)]

        def slot(a, dev):
            return out_refs[a].at[4 * dev[0] + 2 * dev[1] + dev[2]]

        def copy(a, k, block, to, own=False):
            return pltpu.make_async_remote_copy(
                src_ref=src_refs[a] if own else slot(a, block), dst_ref=slot(a, block),
                send_sem=send_sems.at[a * n_sem + k], recv_sem=recv_sems.at[a * n_sem + k],
                device_id=to, device_id_type=pl.DeviceIdType.MESH)

        arrays = range(n_arr)
        first = [copy(a, 0, me, sibling, own=True) for a in arrays]
        first += [copy(a, 1 + j, me, (*chip, c), own=True) for j, chip in enumerate(chips) for a in arrays]
        for cp in first:
            cp.start()
        passed = []
        for j, chip in enumerate(chips):
            for a in arrays:
                copy(a, 1 + j, (*chip, c), me).wait_recv()
                fwd = copy(a, 4 + j, (*chip, c), sibling)
                fwd.start()
                passed.append(fwd)
        for a in arrays:
            copy(a, 0, sibling, me).wait_recv()
        for j, chip in enumerate(chips):
            for a in arrays:
                copy(a, 4 + j, (*chip, 1 - c), me).wait_recv()
        for cp in first + passed:
            cp.wait_send()

    return pl.pallas_call(
        body, name=name,
        in_specs=[pl.BlockSpec(memory_space=pl.ANY)] * n_arr, out_specs=[pl.BlockSpec(memory_space=pl.ANY)] * n_arr,
        out_shape=[jax.ShapeDtypeStruct((N_DEV,) + tuple(s.shape), s.dtype) for s in srcs],
        scratch_shapes=[pltpu.SemaphoreType.DMA((n_arr * n_sem,)), pltpu.SemaphoreType.DMA((n_arr * n_sem,))],
    )(*srcs)


def _adamw_body(p_ref, w_ref, m_ref, v_ref, g_ref, d_ref, nm_ref, nv_ref):
    bc1 = 1.0 - ADAM_B1 ** ADAM_STEP
    bc2 = 1.0 - ADAM_B2 ** ADAM_STEP
    g = p_ref[0].astype(F32)
    for k in range(1, p_ref.shape[0]):
        g = g + p_ref[k].astype(F32)
    nm = ADAM_B1 * m_ref[...] + (1.0 - ADAM_B1) * g
    nv = ADAM_B2 * v_ref[...] + (1.0 - ADAM_B2) * jnp.square(g)
    m_hat = nm / bc1
    v_hat = nv / bc2
    g_ref[...] = g
    d_ref[...] = -ADAM_LR * (m_hat / (jnp.sqrt(v_hat) + ADAM_EPS) + ADAM_WD * w_ref[...])
    nm_ref[...] = nm
    nv_ref[...] = nv


def _adamw_rows(parts, w, m, v, name):
    rows, cols = w.shape
    tr = _row_tile(rows, max(128, (1 << 18) // cols))
    blk = pl.BlockSpec((tr, cols), lambda i: (i, 0))
    return pl.pallas_call(
        functools.partial(_adamw_body), name=name, grid=(rows // tr,),
        in_specs=[pl.BlockSpec((parts.shape[0], tr, cols), lambda i: (0, i, 0)), blk, blk, blk],
        out_specs=[blk] * 4, out_shape=[jax.ShapeDtypeStruct(w.shape, F32)] * 4,
        compiler_params=_params(("arbitrary",)),
    )(parts, w, m, v)


def _pad_flat(vec, rows):
    return jnp.pad(vec, (0, rows * LANES - vec.shape[0])).reshape(rows, LANES)


def _rows_for(n):
    return -(-n // (FLAT_ROWS * LANES)) * FLAT_ROWS


def _join_shards(rows, local_shape, dim):
    parts = jnp.moveaxis(rows.reshape((N_DEV,) + tuple(local_shape)), 0, dim)
    shp = tuple(local_shape)
    return parts.reshape(shp[:dim] + (N_DEV * shp[dim],) + shp[dim + 1:])


def _s5_tables(a_re, a_im, log_step, b_re, b_im, c_re, c_im):
    lam_re = jnp.minimum(a_re, -1e-4)
    lam_im = a_im
    step = jnp.exp(log_step)[:, None]
    mag = jnp.exp(lam_re * step)
    abar_re, abar_im = mag * jnp.cos(lam_im * step), mag * jnp.sin(lam_im * step)
    den = lam_re * lam_re + lam_im * lam_im
    nr, ni = abar_re - 1.0, abar_im
    coef_re = (nr * lam_re + ni * lam_im) / den
    coef_im = (ni * lam_re - nr * lam_im) / den
    bbar_re = coef_re[..., None] * b_re - coef_im[..., None] * b_im
    bbar_im = coef_re[..., None] * b_im + coef_im[..., None] * b_re
    groups = a_re.shape[0]
    nblk = groups // S5_BLOCK_GROUPS
    eye = jnp.eye(S5_BLOCK_GROUPS, dtype=F32)

    def in_blocks(bb):
        t = jnp.swapaxes(bb, 1, 2).reshape(nblk, S5_BLOCK_GROUPS, S5_GROUP, S5_STATE)
        blk = jnp.einsum('ab,jacp->jacbp', eye, t)
        return blk.reshape(nblk, S5_BLOCK_GROUPS * S5_GROUP, S5_BLOCK_GROUPS * S5_STATE)

    def out_blocks(cc):
        t = jnp.swapaxes(cc, 1, 2).reshape(nblk, S5_BLOCK_GROUPS, S5_STATE, S5_GROUP)
        blk = jnp.einsum('ab,japc->japbc', eye, t)
        return blk.reshape(nblk, S5_BLOCK_GROUPS * S5_STATE, S5_BLOCK_GROUPS * S5_GROUP)

    rows = groups * S5_STATE // LANES
    return dict(b_re=in_blocks(bbar_re), b_im=in_blocks(bbar_im), c_re=out_blocks(c_re), c_im=out_blocks(-c_im),
                a_re=abar_re.reshape(rows, LANES), a_im=abar_im.reshape(rows, LANES))


def _in_widths(d_model):
    return [BRANCH, BRANCH, SSD_HEADS * SSD_HEAD + 2 * SSD_GROUPS * SSD_STATE, SSD_HEADS, BRANCH,
            3 * BRANCH, GDN_HEADS, GDN_HEADS, BRANCH, 3 * d_model]


def _local_loss(w, mats, x, target):
    n_meta, d_model = w['meta'].shape
    depth = len(mats['in'])
    seq = x.shape[0]
    pad_rows = CHUNK - n_meta
    first = pad_rows + n_meta
    t_all = first + seq
    alpha = (2 * depth) ** 0.25
    fns = _row_fns(d_model, pad_rows, first, alpha)
    row = lambda nm, key, n_row, n_par, widths, cap, ridx=False: _make_rowwise(nm, fns[key], n_row, n_par, widths, cap, ridx)

    h = jnp.concatenate([jnp.zeros((pad_rows, d_model), F32), w['meta'], x], axis=0)
    h, h16 = _make_rowwise("ln_in", fns['ln_in'], 1, 2, [d_model, d_model], 416, True, out_dtypes=[F32, BF16])(
        h, w['ln_in_g'][None], w['ln_in_b'][None])

    n_small = SSD_HEADS + 2 * GDN_HEADS

    def small_cols(ps):
        return jnp.pad(jnp.concatenate([ps[3], ps[6], ps[7]], axis=1), ((0, 0), (0, LANES - n_small)))

    for l in range(depth):
        pw, pc = mats['in'][l], w['c_in'][l]
        used = (0, 1, 2, 4, 5, 8, 9)
        s5_u, s5_z, ssd_xbc, ssd_z, gdn_qkv, gdn_z, gate, small = _in_proj(
            ["in_s5u", "in_s5z", "in_ssdx", "in_ssdz", "in_gdnq", "in_gdnz", "in_gate", "in_small"], h, h16,
            [pw[i] for i in used] + [small_cols(pw)], [pc[i] for i in used] + [small_cols(pc)])

        zeros_tail = jnp.zeros((LANES - n_small,), F32)
        bias = jnp.concatenate([w['ssd_dt_bias'][l], w['gdn_dt_bias'][l], jnp.zeros((GDN_HEADS,), F32), zeros_tail])[None]
        scale = jnp.concatenate([jnp.ones((SSD_HEADS,), F32), -jnp.exp(w['gdn_a_log'][l]),
                                 jnp.zeros((GDN_HEADS,), F32), zeros_tail])[None]
        (sm,) = row("small_act", 'small_act', 1, 2, [LANES], 832, True)(small, bias, scale)

        tb = _s5_tables(w['s5_a_re'][l], w['s5_a_im'][l], w['s5_log_step'][l], w['s5_b_re'][l], w['s5_b_im'][l],
                        w['s5_c_re'][l], w['s5_c_im'][l])
        srows = tb['a_re'].shape[0]
        bu_re = _make_gmm("s5_bre")(s5_u, tb['b_re']).reshape(t_all, srows, LANES)
        bu_im = _make_gmm("s5_bim")(s5_u, tb['b_im']).reshape(t_all, srows, LANES)
        s_re, s_im = _make_s5_scan("s5_scan")(bu_re, bu_im, tb['a_re'], tb['a_im'])
        y_re = _make_gmm("s5_cre")(s_re.reshape(t_all, srows * LANES), tb['c_re'])
        y_im = _make_gmm("s5_cim")(s_im.reshape(t_all, srows * LANES), tb['c_im'])
        (out_a,) = _make_rowwise("s5_tail", fns['s5_tail'], 4, 2, [d_model], 208, n_wt=2)(
            y_re, y_im, s5_u, s5_z, w['s5_d'][l][None], w['s5_b_glu'][l][None],
            mats['glu'][l], mats['branch'][l, 0], w['c_glu'][l], w['c_branch'][l, 0])

        xbc = _make_conv_rowwise("ssd_conv", fns['ssd_conv'], 2, ssd_xbc.shape[1], 416, True)(
            ssd_xbc, w['ssd_conv_w'][l], w['ssd_conv_b'][l][None])
        a_c = jnp.pad(-jnp.exp(w['ssd_a_log'][l]), (0, LANES - SSD_HEADS))[None]
        d_exp = jnp.repeat(w['ssd_d'][l], SSD_HEAD)[None]
        (y_ssd,) = _make_chunk_scan("ssd_scan", _ssd_chunk, (SSD_HEADS // 2 * LANES, SSD_STATE), 2, 2, [BRANCH])(
            xbc, sm, a_c, d_exp)
        (out_b,) = _make_rowwise("ssd_tail", fns['ssd_tail'], 2, 1, [d_model], 416, n_wt=1)(
            y_ssd, ssd_z, w['ssd_norm_g'][l][None], mats['branch'][l, 1], w['c_branch'][l, 1])

        qkv = _make_conv_rowwise("gdn_conv", fns['gdn_conv'], 1, 3 * BRANCH, 208)(gdn_qkv, w['gdn_conv_w'][l])
        (o_gdn,) = _make_chunk_scan("gdn_scan", _gdn_chunk, (GDN_HEADS * GDN_HEAD, GDN_HEAD), 2, 0, [BRANCH])(qkv, sm)
        (out_c,) = _make_rowwise("gdn_tail", fns['gdn_tail'], 2, 1, [d_model], 416, n_wt=1)(
            o_gdn, gdn_z, w['gdn_norm_g'][l][None], mats['branch'][l, 2], w['c_branch'][l, 2])

        h, h16 = _make_rowwise("merge_out", fns['merge_out'], 5, 3, [d_model, d_model], 208, True, n_wt=1,
                               out_dtypes=[F32, BF16])(
            out_a, out_b, out_c, gate, h, w['b_gate'][l].reshape(1, 3 * d_model), w['ln_g'][l][None],
            w['ln_b'][l][None], mats['out'][l], w['c_out'][l])

    tgt = jnp.concatenate([jnp.zeros((first, d_model), F32), target], axis=0)
    (rows_loss,) = row("loss", 'loss', 2, 0, [LANES], 416, True)(h, tgt)
    return jnp.sum(rows_loss)


def _in_overlaps(d_model, n_loc):
    offs = [0]
    for wd in _in_widths(d_model):
        offs.append(offs[-1] + wd)
    out = []
    for i in range(len(offs) - 1):
        c0, c1 = offs[i], offs[i + 1]
        segs = []
        for k in range(N_DEV):
            g0, g1 = max(c0, k * n_loc), min(c1, (k + 1) * n_loc)
            if g0 < g1:
                segs.append((k, g0 - k * n_loc, g1 - k * n_loc, g0 - c0))
        out.append(segs)
    return out


STACKED = ['w_in', 's5_w_glu', 'w_branch', 'w_out']


def _shard_blocks(full, dim):
    shp = full.shape
    parts = full.reshape(shp[:dim] + (N_DEV, shp[dim] // N_DEV) + shp[dim + 1:])
    return jnp.moveaxis(parts, dim, 0)


def _join_blocks(blocks, dim):
    shp = blocks.shape[1:]
    return jnp.moveaxis(blocks, 0, dim).reshape(shp[:dim] + (N_DEV * shp[dim],) + shp[dim + 1:])


def _step(x, target, w_loc, m_loc, v_loc):
    small = [n for n in WEIGHTS if n in SHARD_DIM and n not in STACKED]
    repl = [n for n in WEIGHTS if n not in SHARD_DIM]
    size = lambda names: sum(int(w_loc[n].size) for n in names)
    depth, d_model, n_loc = w_loc['w_in'].shape
    overlaps = _in_overlaps(d_model, n_loc)
    n_chip = N_DEV // 2
    me = 4 * lax.axis_index("x") + 2 * lax.axis_index("y") + lax.axis_index("c")
    my_chip = 2 * lax.axis_index("x") + lax.axis_index("y")
    core = lax.axis_index("c").astype(jnp.int32).reshape(1)
    put = lambda buf, blk, idx: lax.dynamic_update_index_in_dim(buf, blk, idx, 0)
    take = lambda buf, idx: lax.dynamic_index_in_dim(buf, idx, 0, keepdims=False)

    rows_small = _rows_for(size(small))
    small_flat = _pad_flat(jnp.concatenate([w_loc[n].reshape(-1) for n in small]), rows_small)
    own_blocks = [w_loc[n].astype(BF16) for n in STACKED] + [small_flat]
    gathered = [put(got, blk, me) for got, blk in zip(_gather(own_blocks, "gather_weights"), own_blocks)]
    g_in = gathered[0]
    full = {n: _join_blocks(g, SHARD_DIM[n]) for n, g in zip(STACKED[1:], gathered[1:])}
    buf, off = gathered[-1].reshape(N_DEV, -1), 0
    for n in small:
        sz = int(w_loc[n].size)
        full[n] = _join_shards(buf[:, off:off + sz], w_loc[n].shape, SHARD_DIM[n])
        off += sz
    mats = dict(glu=full['s5_w_glu'], branch=full['w_branch'], out=full['w_out'], **{
        'in': [[jnp.concatenate([g_in[k, l, :, lo:hi] for k, lo, hi, _ in segs], axis=1) for segs in overlaps]
               for l in range(depth)]})
    w_diff = {n: w_loc[n] for n in repl}
    w_diff.update({n: full[n] for n in small})
    w_diff['c_in'] = [[jnp.zeros((d_model, wd), F32) for wd in _in_widths(d_model)] for _ in range(depth)]
    w_diff['c_glu'] = jnp.zeros(full['s5_w_glu'].shape, F32)
    w_diff['c_branch'] = jnp.zeros(full['w_branch'].shape, F32)
    w_diff['c_out'] = jnp.zeros(full['w_out'].shape, F32)

    loss, (g_w, g_x) = jax.value_and_grad(_local_loss, argnums=(0, 2))(w_diff, mats, x[0], target[0])
    g_w['s5_w_glu'], g_w['w_branch'], g_w['w_out'] = g_w['c_glu'], g_w['c_branch'], g_w['c_out']

    blocks = {'w_in': jnp.stack([jnp.stack([
        jnp.concatenate([g_w['c_in'][l][i][:, plo:plo + hi - lo]
                         for i, segs in enumerate(overlaps) for (kk, lo, hi, plo) in segs if kk == k], axis=1)
        for l in range(depth)]) for k in range(N_DEV)])}
    for n in STACKED[1:]:
        blocks[n] = _shard_blocks(g_w[n], SHARD_DIM[n])
    cols = {n: w_loc[n].shape[-1] for n in STACKED}
    send = [blocks[n].astype(BF16).reshape(n_chip, 2, -1, cols[n]) for n in STACKED]
    red_names = small + repl
    n_red = sum(int(g_w[n].size) for n in red_names) + 1
    red = _pad_flat(jnp.concatenate([g_w[n].reshape(-1) for n in red_names] + [loss.reshape(1)]), _rows_for(n_red))
    *recv, red_sib = _pair_exchange(send, [red], "pair_exchange")
    sums = [_pair_sum(s, r, core, "pair_sum_" + n) for s, r, n in zip(send, recv, STACKED)]
    red_chip = _add2(red, red_sib, "pair_sum_rest")
    *parts, red_parts = _chip_exchange(sums, [red_chip], "exchange_grads")
    parts = [put(p, take(s, my_chip), my_chip) for p, s in zip(parts, sums)]
    red_total = _add_parts(put(red_parts, red_chip, my_chip), "reduce_rest").reshape(-1)

    outs = {}
    for n, p in zip(STACKED, parts):
        as_rows = lambda a: a.reshape(-1, cols[n])
        res = _adamw_rows(p, as_rows(w_loc[n]), as_rows(m_loc[n]), as_rows(v_loc[n]), "adamw_" + n)
        outs[n] = [r.reshape(w_loc[n].shape) for r in res]
    g_red, off = {}, 0
    for n in red_names:
        sz = int(g_w[n].size)
        g_red[n] = red_total[off:off + sz].reshape(g_w[n].shape)
        off += sz
    loss_total = red_total[off]
    for n in small:
        g_red[n] = take(_shard_blocks(g_red[n], SHARD_DIM[n]), me)
    rows_rest = _rows_for(size(red_names))
    flat = lambda src: _pad_flat(jnp.concatenate([src[n].reshape(-1) for n in red_names]), rows_rest)
    res = _adamw_rows(flat(g_red)[None], flat(w_loc), flat(m_loc), flat(v_loc), "adamw_rest")
    off = 0
    for n in red_names:
        sz = int(w_loc[n].size)
        outs[n] = [r.reshape(-1)[off:off + sz].reshape(w_loc[n].shape) for r in res]
        off += sz
    return (loss_total, g_x[None], *[outs[n][k] for k in range(4) for n in WEIGHTS])


def kernel(x, meta, ln_in_g, ln_in_b, w_in, s5_a_re, s5_a_im, s5_log_step, s5_b_re, s5_b_im, s5_c_re, s5_c_im, s5_d, s5_w_glu, s5_b_glu, ssd_conv_w, ssd_conv_b, ssd_dt_bias, ssd_a_log, ssd_d, ssd_norm_g, gdn_conv_w, gdn_dt_bias, gdn_a_log, gdn_norm_g, w_branch, b_gate, w_out, ln_g, ln_b, loss_target, m_meta, m_ln_in_g, m_ln_in_b, m_w_in, m_s5_a_re, m_s5_a_im, m_s5_log_step, m_s5_b_re, m_s5_b_im, m_s5_c_re, m_s5_c_im, m_s5_d, m_s5_w_glu, m_s5_b_glu, m_ssd_conv_w, m_ssd_conv_b, m_ssd_dt_bias, m_ssd_a_log, m_ssd_d, m_ssd_norm_g, m_gdn_conv_w, m_gdn_dt_bias, m_gdn_a_log, m_gdn_norm_g, m_w_branch, m_b_gate, m_w_out, m_ln_g, m_ln_b, v_meta, v_ln_in_g, v_ln_in_b, v_w_in, v_s5_a_re, v_s5_a_im, v_s5_log_step, v_s5_b_re, v_s5_b_im, v_s5_c_re, v_s5_c_im, v_s5_d, v_s5_w_glu, v_s5_b_glu, v_ssd_conv_w, v_ssd_conv_b, v_ssd_dt_bias, v_ssd_a_log, v_ssd_d, v_ssd_norm_g, v_gdn_conv_w, v_gdn_dt_bias, v_gdn_a_log, v_gdn_norm_g, v_w_branch, v_b_gate, v_w_out, v_ln_g, v_ln_b):
    w_loc = dict(zip(WEIGHTS, (meta, ln_in_g, ln_in_b, w_in, s5_a_re, s5_a_im, s5_log_step, s5_b_re, s5_b_im, s5_c_re, s5_c_im, s5_d, s5_w_glu, s5_b_glu, ssd_conv_w, ssd_conv_b, ssd_dt_bias, ssd_a_log, ssd_d, ssd_norm_g, gdn_conv_w, gdn_dt_bias, gdn_a_log, gdn_norm_g, w_branch, b_gate, w_out, ln_g, ln_b)))
    m_loc = dict(zip(WEIGHTS, (m_meta, m_ln_in_g, m_ln_in_b, m_w_in, m_s5_a_re, m_s5_a_im, m_s5_log_step, m_s5_b_re, m_s5_b_im, m_s5_c_re, m_s5_c_im, m_s5_d, m_s5_w_glu, m_s5_b_glu, m_ssd_conv_w, m_ssd_conv_b, m_ssd_dt_bias, m_ssd_a_log, m_ssd_d, m_ssd_norm_g, m_gdn_conv_w, m_gdn_dt_bias, m_gdn_a_log, m_gdn_norm_g, m_w_branch, m_b_gate, m_w_out, m_ln_g, m_ln_b)))
    v_loc = dict(zip(WEIGHTS, (v_meta, v_ln_in_g, v_ln_in_b, v_w_in, v_s5_a_re, v_s5_a_im, v_s5_log_step, v_s5_b_re, v_s5_b_im, v_s5_c_re, v_s5_c_im, v_s5_d, v_s5_w_glu, v_s5_b_glu, v_ssd_conv_w, v_ssd_conv_b, v_ssd_dt_bias, v_ssd_a_log, v_ssd_d, v_ssd_norm_g, v_gdn_conv_w, v_gdn_dt_bias, v_gdn_a_log, v_gdn_norm_g, v_w_branch, v_b_gate, v_w_out, v_ln_g, v_ln_b)))
    return _step(x, loss_target, w_loc, m_loc, v_loc)
```

```python
import functools
import math

import jax
import jax.numpy as jnp
from jax import lax
from jax.experimental import pallas as pl
from jax.experimental.pallas import tpu as pltpu

F32 = jnp.float32
BF16 = jnp.bfloat16

N_DEV = 8
LANES = 128
SUBLANES = 8
VMEM_LIMIT = 56 * 1024 * 1024
FLAT_ROWS = 1024

CHUNK = 64
CONV_K = 4
S5_GROUP = 16
S5_STATE = 64
S5_BLOCK_GROUPS = 8
SSD_HEAD = 64
SSD_HEADS = 12
SSD_GROUPS = 2
SSD_STATE = 128
GDN_HEAD = 128
GDN_HEADS = 6
BRANCH = 768
LN_EPS = 1e-5

ADAM_LR = 0.001
ADAM_B1 = 0.9
ADAM_B2 = 0.999
ADAM_EPS = 1e-08
ADAM_WD = 0.01
ADAM_STEP = 10

WEIGHTS = ['meta', 'ln_in_g', 'ln_in_b', 'w_in', 's5_a_re', 's5_a_im', 's5_log_step', 's5_b_re', 's5_b_im',
           's5_c_re', 's5_c_im', 's5_d', 's5_w_glu', 's5_b_glu', 'ssd_conv_w', 'ssd_conv_b', 'ssd_dt_bias',
           'ssd_a_log', 'ssd_d', 'ssd_norm_g', 'gdn_conv_w', 'gdn_dt_bias', 'gdn_a_log', 'gdn_norm_g',
           'w_branch', 'b_gate', 'w_out', 'ln_g', 'ln_b']
SHARD_DIM = {'meta': 1, 'w_in': 2, 's5_w_glu': 1, 'ssd_conv_w': 2, 'gdn_conv_w': 2, 'w_branch': 3, 'b_gate': 2,
             'w_out': 1}


def _params(sem):
    return pltpu.CompilerParams(dimension_semantics=sem, vmem_limit_bytes=VMEM_LIMIT)


def _row_tile(m, cap):
    best = None
    for t in range(SUBLANES, min(m, cap) + 1, SUBLANES):
        if m % t == 0:
            best = t
    return best if best is not None else m


def _col_tile(n, cap):
    best = None
    for t in range(LANES, min(n, cap) + 1, LANES):
        if n % t == 0:
            best = t
    return best if best is not None else n


def _any_tile(m, cap):
    best = 1
    for t in range(1, min(m, cap) + 1):
        if m % t == 0:
            best = t
    return best


def _mm_fwd(a, b, name):
    m, _ = a.shape
    g, k, n = b.shape
    tm, tn = _row_tile(m, 2080 if a.dtype == BF16 else 832), _col_tile(n, 1024)
    nj = n // tn

    def body(a_ref, b_ref, o_ref):
        o_ref[...] = jnp.dot(a_ref[...].astype(BF16), b_ref[0].astype(BF16), preferred_element_type=F32)

    return pl.pallas_call(
        body, name=name, grid=(g, m // tm, nj),
        in_specs=[pl.BlockSpec((tm, k), lambda gi, i, j: (i, gi)),
                  pl.BlockSpec((1, k, tn), lambda gi, i, j: (gi, 0, j))],
        out_specs=pl.BlockSpec((tm, tn), lambda gi, i, j: (i, gi * nj + j)),
        out_shape=jax.ShapeDtypeStruct((m, g * n), F32),
        compiler_params=_params(("arbitrary", "arbitrary", "arbitrary")),
    )(a, b)


def _mm_da(ct, b, name):
    m, _ = ct.shape
    g, k, n = b.shape
    tm, tk = _row_tile(m, 832 if n <= 1536 else 416), _col_tile(k, 1024)
    nk = k // tk

    def body(c_ref, b_ref, o_ref):
        o_ref[...] = lax.dot_general(c_ref[...].astype(BF16), b_ref[0].astype(BF16), (((1,), (1,)), ((), ())),
                                     preferred_element_type=F32)

    return pl.pallas_call(
        body, name=name, grid=(g, m // tm, nk),
        in_specs=[pl.BlockSpec((tm, n), lambda gi, i, j: (i, gi)),
                  pl.BlockSpec((1, tk, n), lambda gi, i, j: (gi, j, 0))],
        out_specs=pl.BlockSpec((tm, tk), lambda gi, i, j: (i, gi * nk + j)),
        out_shape=jax.ShapeDtypeStruct((m, g * k), F32),
        compiler_params=_params(("arbitrary", "arbitrary", "arbitrary")),
    )(ct, b)


def _mm_db(a, ct, g, k, n, name):
    m = a.shape[0]
    tm, tk, tn = _row_tile(m, 832), _col_tile(k, 1024), _col_tile(n, 1280)
    nk, nn = k // tk, n // tn

    def body(a_ref, c_ref, o_ref):
        @pl.when(pl.program_id(3) == 0)
        def _():
            o_ref[...] = jnp.zeros_like(o_ref)

        o_ref[0] += lax.dot_general(a_ref[...].astype(BF16), c_ref[...].astype(BF16), (((0,), (0,)), ((), ())),
                                    preferred_element_type=F32)

    return pl.pallas_call(
        body, name=name, grid=(g, nk, nn, m // tm),
        in_specs=[pl.BlockSpec((tm, tk), lambda gi, i, j, r: (r, gi * nk + i)),
                  pl.BlockSpec((tm, tn), lambda gi, i, j, r: (r, gi * nn + j))],
        out_specs=pl.BlockSpec((1, tk, tn), lambda gi, i, j, r: (gi, i, j)),
        out_shape=jax.ShapeDtypeStruct((g, k, n), F32),
        compiler_params=_params(("arbitrary", "arbitrary", "arbitrary", "arbitrary")),
    )(a, ct)


def _make_gmm(name):
    @jax.custom_vjp
    def gmm(a, b):
        return _mm_fwd(a, b, name + "_fwd")

    def fwd(a, b):
        return _mm_fwd(a, b, name + "_fwd"), (a, b)

    def bwd(res, ct):
        a, b = res
        g, k, n = b.shape
        return _mm_da(ct, b, name + "_da"), _mm_db(a, ct, g, k, n, name + "_db")

    gmm.defvjp(fwd, bwd)
    return gmm


def _mm_da_sum(cts, ws, name):
    m, k = cts[0].shape[0], ws[0].shape[0]
    tm = _row_tile(m, 208)
    n_p = len(cts)

    def body(*refs):
        acc = None
        for c_ref, w_ref in zip(refs[:n_p], refs[n_p:2 * n_p]):
            term = lax.dot_general(c_ref[...].astype(BF16), w_ref[...].astype(BF16), (((1,), (1,)), ((), ())),
                                   preferred_element_type=F32)
            acc = term if acc is None else acc + term
        refs[2 * n_p][...] = acc

    return pl.pallas_call(
        body, name=name, grid=(m // tm,),
        in_specs=[pl.BlockSpec((tm, c.shape[1]), lambda i: (i, 0)) for c in cts]
        + [pl.BlockSpec(w.shape, lambda i: (0, 0), pipeline_mode=pl.Buffered(1)) for w in ws],
        out_specs=pl.BlockSpec((tm, k), lambda i: (i, 0)),
        out_shape=jax.ShapeDtypeStruct((m, k), F32),
        compiler_params=_params(("arbitrary",)),
    )(*cts, *ws)


def _in_proj(names, a, a16, ws, carriers):
    def products(a16, ws):
        return tuple(_mm_fwd(a16, w[None], nm + "_fwd") for nm, w in zip(names, ws))

    @jax.custom_vjp
    def proj(a, a16, ws, carriers):
        return products(a16, ws)

    def fwd(a, a16, ws, carriers):
        return products(a16, ws), (a16, ws)

    def bwd(res, cts):
        a16, ws = res
        d_ws = tuple(_mm_db(a16, ct, 1, w.shape[0], w.shape[1], nm + "_db")[0] for nm, w, ct in zip(names, ws, cts))
        return (_mm_da_sum(list(cts), list(ws), "in_da"), jnp.zeros_like(a16),
                tuple(jnp.zeros_like(w) for w in ws), d_ws)

    proj.defvjp(fwd, bwd)
    return proj(a, a16, tuple(ws), tuple(carriers))


@jax.custom_vjp
def _wdot(x, w, carrier):
    return _dot(x, w)


def _wdot_fwd(x, w, carrier):
    return _dot(x, w), (x, w)


def _wdot_bwd(res, ct):
    x, w = res
    return _dot_nt(ct, w), jnp.zeros_like(w), _dot_tn(x, ct)


_wdot.defvjp(_wdot_fwd, _wdot_bwd)


def _make_rowwise(name, fn, n_row, n_par, out_widths, tm_cap, use_ridx=False, n_wt=0, out_dtypes=None):
    n_out = len(out_widths)
    out_dtypes = out_dtypes or [F32] * n_out
    n_in = n_row + n_par + n_wt

    def bind(tm):
        if not use_ridx:
            return fn
        ridx = pl.program_id(0) * tm + lax.broadcasted_iota(jnp.int32, (tm, 1), 0)
        return functools.partial(fn, ridx)

    def specs(args, tm):
        rows = [pl.BlockSpec((tm, a.shape[1]), lambda i: (i, 0)) for a in args[:n_row]]
        pars = [pl.BlockSpec(a.shape, lambda i: (0, 0)) for a in args[n_row:n_in]]
        return rows, pars

    def fwd_call(*args):
        t = args[0].shape[0]
        tm = _row_tile(t, tm_cap)
        rows, pars = specs(args, tm)

        def body(*refs):
            vals = [r[...] for r in refs[:n_row + n_par]]
            mats = [functools.partial(lambda x, w: _dot(x, w), w=r[...]) for r in refs[n_row + n_par:n_in]]
            res = bind(tm)(*vals, *mats)
            for o_ref, r in zip(refs[n_in:], res):
                o_ref[...] = r

        outs = pl.pallas_call(
            body, name=name + "_fwd", grid=(t // tm,), in_specs=rows + pars,
            out_specs=[pl.BlockSpec((tm, w), lambda i: (i, 0)) for w in out_widths],
            out_shape=[jax.ShapeDtypeStruct((t, w), dt) for w, dt in zip(out_widths, out_dtypes)],
            compiler_params=_params(("arbitrary",)),
        )(*args)
        return tuple(outs)

    def bwd_call(args, cts):
        t = args[0].shape[0]
        tm = _row_tile(t, tm_cap)
        rows, pars = specs(args, tm)
        ct_specs = [pl.BlockSpec((tm, w), lambda i: (i, 0)) for w in out_widths]
        n_diff = n_row + n_par

        def body(*refs):
            vals = [r[...] for r in refs[:n_diff]]
            wts = [r[...] for r in refs[n_diff:n_in]]
            ct_vals = tuple(r[...] for r in refs[n_in:n_in + n_out])
            d_refs = refs[n_in + n_out:]
            f = bind(tm)

            def g(*a):
                mats = [functools.partial(lambda x, w, c: _wdot(x, w, c), w=w, c=c)
                        for w, c in zip(wts, a[n_diff:])]
                return tuple(f(*a[:n_diff], *mats))

            _, vjp = jax.vjp(g, *vals, *[jnp.zeros(w.shape, F32) for w in wts])
            grads = vjp(ct_vals)
            for i in range(n_row):
                d_refs[i][...] = grads[i]
            if n_par + n_wt:
                @pl.when(pl.program_id(0) == 0)
                def _():
                    for r in d_refs[n_row:]:
                        r[...] = jnp.zeros_like(r)

                for r, gr in zip(d_refs[n_row:], grads[n_row:]):
                    r[...] += gr

        outs = pl.pallas_call(
            body, name=name + "_bwd", grid=(t // tm,), in_specs=rows + pars + ct_specs,
            out_specs=rows + pars,
            out_shape=[jax.ShapeDtypeStruct(a.shape, F32) for a in args],
            compiler_params=_params(("arbitrary",)),
        )(*args, *cts)
        return tuple(outs)

    @jax.custom_vjp
    def op(*args):
        return fwd_call(*args[:n_in])

    def fwd(*args):
        return fwd_call(*args[:n_in]), args[:n_in]

    def bwd(args, cts):
        grads = bwd_call(args, cts)
        return grads[:n_row + n_par] + tuple(jnp.zeros_like(a) for a in args[n_row + n_par:]) + grads[n_row + n_par:]

    op.defvjp(fwd, bwd)
    return op


HALO = SUBLANES


def _make_conv_rowwise(name, fn, n_par, out_width, tm_cap, use_ridx=False):
    def bind(ridx):
        return functools.partial(fn, ridx) if use_ridx else fn

    def stage(x_ref, halo_ref, xs, first):
        xs[0:HALO, :] = jnp.where(first, 0.0, halo_ref[...])
        xs[HALO:, :] = x_ref[...]

    def taps(xs, tm):
        return [xs[pl.ds(HALO - (CONV_K - 1) + j, tm), :] for j in range(CONV_K)]

    def fwd_call(x, *pars):
        t, wd = x.shape
        tm = _row_tile(t, tm_cap)
        per = tm // HALO

        def body(*refs):
            x_ref, halo_ref = refs[:2]
            par_refs, o_ref, xs = refs[2:2 + n_par], refs[2 + n_par], refs[-1]
            i = pl.program_id(0)
            stage(x_ref, halo_ref, xs, i == 0)
            ridx = i * tm + lax.broadcasted_iota(jnp.int32, (tm, 1), 0)
            (o_ref[...],) = bind(ridx)(*taps(xs, tm), *[r[...] for r in par_refs])

        return pl.pallas_call(
            body, name=name + "_fwd", grid=(t // tm,),
            in_specs=[pl.BlockSpec((tm, wd), lambda i: (i, 0)),
                      pl.BlockSpec((HALO, wd), lambda i: (jnp.maximum(i * per - 1, 0), 0))]
            + [pl.BlockSpec(p.shape, lambda i: (0, 0)) for p in pars],
            out_specs=pl.BlockSpec((tm, out_width), lambda i: (i, 0)),
            out_shape=jax.ShapeDtypeStruct((t, out_width), F32),
            scratch_shapes=[pltpu.VMEM((tm + HALO, wd), F32)],
            compiler_params=_params(("arbitrary",)),
        )(x, x, *pars)

    def bwd_call(x, pars, ct):
        t, wd = x.shape
        tm = _row_tile(t, tm_cap)
        per = tm // HALO
        nb = t // tm

        def body(*refs):
            x_ref, halo_ref = refs[:2]
            par_refs, ct_ref = refs[2:2 + n_par], refs[2 + n_par]
            dx_ref, dpar_refs = refs[3 + n_par], refs[4 + n_par:4 + 2 * n_par]
            xs, gs, carry = refs[-3:]
            step = pl.program_id(0)
            blk = nb - 1 - step

            @pl.when(step == 0)
            def _():
                carry[...] = jnp.zeros_like(carry)
                for r in dpar_refs:
                    r[...] = jnp.zeros_like(r)

            stage(x_ref, halo_ref, xs, blk == 0)
            ridx = blk * tm + lax.broadcasted_iota(jnp.int32, (tm, 1), 0)
            f = bind(ridx)
            _, vjp = jax.vjp(lambda *a: tuple(f(*a)), *taps(xs, tm), *[r[...] for r in par_refs])
            grads = vjp((ct_ref[...],))
            dx = grads[CONV_K - 1]
            for j in range(CONV_K - 1):
                gs[j, 0:tm, :] = grads[j]
                gs[j, tm:, :] = carry[j]
                dx = dx + gs[j, pl.ds(CONV_K - 1 - j, tm), :]
                carry[j] = grads[j][0:HALO, :]
            dx_ref[...] = dx
            for r, g in zip(dpar_refs, grads[CONV_K:]):
                r[...] += g

        rev = lambda i: (nb - 1 - i, 0)
        outs = pl.pallas_call(
            body, name=name + "_bwd", grid=(nb,),
            in_specs=[pl.BlockSpec((tm, wd), rev),
                      pl.BlockSpec((HALO, wd), lambda i: (jnp.maximum((nb - 1 - i) * per - 1, 0), 0))]
            + [pl.BlockSpec(p.shape, lambda i: (0, 0)) for p in pars]
            + [pl.BlockSpec((tm, out_width), rev)],
            out_specs=[pl.BlockSpec((tm, wd), rev)] + [pl.BlockSpec(p.shape, lambda i: (0, 0)) for p in pars],
            out_shape=[jax.ShapeDtypeStruct(x.shape, F32)] + [jax.ShapeDtypeStruct(p.shape, F32) for p in pars],
            scratch_shapes=[pltpu.VMEM((tm + HALO, wd), F32), pltpu.VMEM((CONV_K - 1, tm + HALO, wd), F32),
                            pltpu.VMEM((CONV_K - 1, HALO, wd), F32)],
            compiler_params=_params(("arbitrary",)),
        )(x, x, *pars, ct)
        return tuple(outs)

    @jax.custom_vjp
    def op(x, *pars):
        return fwd_call(x, *pars)

    def fwd(x, *pars):
        return fwd_call(x, *pars), (x, pars)

    def bwd(res, ct):
        x, pars = res
        return bwd_call(x, pars, ct)

    op.defvjp(fwd, bwd)
    return op


STEP_CHUNKS = 5


def _make_chunk_scan(name, chunk_fn, state_shape, n_seq, n_par, out_widths):
    n_out = len(out_widths)
    zeros_idx = (0,) * len(state_shape)

    def plan(t):
        per_step = _any_tile(t // CHUNK, STEP_CHUNKS)
        return per_step, per_step * CHUNK, t // (per_step * CHUNK)

    def fn(state, *args):
        per_step = args[0].shape[0] // CHUNK
        outs = []
        for c in range(per_step):
            rows = slice(c * CHUNK, (c + 1) * CHUNK)
            res = chunk_fn(state, *[a[rows, :] for a in args[:n_seq]], *args[n_seq:])
            state = res[0]
            outs.append(res[1:])
        return (state,) + tuple(jnp.concatenate([o[i] for o in outs], axis=0) for i in range(n_out))

    def fwd_call(*args):
        t = args[0].shape[0]
        _, rows_per_step, nc = plan(t)
        seq_specs = [pl.BlockSpec((rows_per_step, a.shape[1]), lambda c: (c, 0)) for a in args[:n_seq]]
        par_specs = [pl.BlockSpec(a.shape, lambda c: (0, 0)) for a in args[n_seq:]]

        def body(*refs):
            ins = refs[:n_seq + n_par]
            out_refs = refs[n_seq + n_par:n_seq + n_par + n_out]
            states_ref = refs[n_seq + n_par + n_out]
            st = refs[-1]

            @pl.when(pl.program_id(0) == 0)
            def _():
                st[...] = jnp.zeros_like(st)

            s0 = st[...]
            states_ref[0] = s0
            res = fn(s0, *[r[...] for r in ins])
            st[...] = res[0]
            for o_ref, r in zip(out_refs, res[1:]):
                o_ref[...] = r

        outs = pl.pallas_call(
            body, name=name + "_fwd", grid=(nc,), in_specs=seq_specs + par_specs,
            out_specs=[pl.BlockSpec((rows_per_step, w), lambda c: (c, 0)) for w in out_widths]
            + [pl.BlockSpec((1,) + state_shape, lambda c: (c,) + zeros_idx)],
            out_shape=[jax.ShapeDtypeStruct((t, w), F32) for w in out_widths]
            + [jax.ShapeDtypeStruct((nc,) + state_shape, F32)],
            scratch_shapes=[pltpu.VMEM(state_shape, F32)],
            compiler_params=_params(("arbitrary",)),
        )(*args)
        return tuple(outs[:n_out]), outs[n_out]

    def bwd_call(args, states, cts):
        t = args[0].shape[0]
        _, rows_per_step, nc = plan(t)
        rev = lambda c: (nc - 1 - c, 0)
        seq_specs = [pl.BlockSpec((rows_per_step, a.shape[1]), rev) for a in args[:n_seq]]
        par_specs = [pl.BlockSpec(a.shape, lambda c: (0, 0)) for a in args[n_seq:]]
        ct_specs = [pl.BlockSpec((rows_per_step, w), rev) for w in out_widths]
        st_spec = pl.BlockSpec((1,) + state_shape, lambda c: (nc - 1 - c,) + zeros_idx)
        n_in = n_seq + n_par

        def body(*refs):
            vals = [r[...] for r in refs[:n_in]]
            s0 = refs[n_in][0]
            ct_vals = tuple(r[...] for r in refs[n_in + 1:n_in + 1 + n_out])
            d_refs = refs[n_in + 1 + n_out:-1]
            dst = refs[-1]

            @pl.when(pl.program_id(0) == 0)
            def _():
                dst[...] = jnp.zeros_like(dst)
                for j in range(n_par):
                    d_refs[n_seq + j][...] = jnp.zeros_like(d_refs[n_seq + j])

            _, vjp = jax.vjp(lambda *a: tuple(fn(*a)), s0, *vals)
            grads = vjp((dst[...],) + ct_vals)
            dst[...] = grads[0]
            for i in range(n_seq):
                d_refs[i][...] = grads[1 + i]
            for j in range(n_par):
                d_refs[n_seq + j][...] += grads[1 + n_seq + j]

        outs = pl.pallas_call(
            body, name=name + "_bwd", grid=(nc,), in_specs=seq_specs + par_specs + [st_spec] + ct_specs,
            out_specs=seq_specs + par_specs,
            out_shape=[jax.ShapeDtypeStruct(a.shape, F32) for a in args],
            scratch_shapes=[pltpu.VMEM(state_shape, F32)],
            compiler_params=_params(("arbitrary",)),
        )(*args, states, *cts)
        return tuple(outs)

    @jax.custom_vjp
    def op(*args):
        return fwd_call(*args)[0]

    def fwd(*args):
        outs, states = fwd_call(*args)
        return outs, (args, states)

    def bwd(res, cts):
        args, states = res
        return bwd_call(args, states, cts)

    op.defvjp(fwd, bwd)
    return op


def _s5_scan_fwd(bre, bim, are, aim, name):
    t, r, _ = bre.shape
    tb = _any_tile(t, 208)
    blk = pl.BlockSpec((tb, r, LANES), lambda i: (i, 0, 0))
    par = pl.BlockSpec((r, LANES), lambda i: (0, 0))

    def body(bre_ref, bim_ref, are_ref, aim_ref, sre_ref, sim_ref, st):
        @pl.when(pl.program_id(0) == 0)
        def _():
            st[...] = jnp.zeros_like(st)

        ar, ai = are_ref[...], aim_ref[...]

        def step(k, carry):
            sr, si = carry
            nr = ar * sr - ai * si + bre_ref[k]
            ni = ar * si + ai * sr + bim_ref[k]
            sre_ref[k] = nr
            sim_ref[k] = ni
            return nr, ni

        sr, si = lax.fori_loop(0, tb, step, (st[0], st[1]), unroll=4)
        st[0] = sr
        st[1] = si

    return pl.pallas_call(
        body, name=name, grid=(t // tb,), in_specs=[blk, blk, par, par], out_specs=[blk, blk],
        out_shape=[jax.ShapeDtypeStruct(bre.shape, F32)] * 2,
        scratch_shapes=[pltpu.VMEM((2, r, LANES), F32)],
        compiler_params=_params(("arbitrary",)),
    )(bre, bim, are, aim)


def _s5_scan_bwd(dsr, dsi, sre, sim, are, aim, name):
    t, r, _ = sre.shape
    tb = _any_tile(t, 208)
    nb = t // tb
    blk = pl.BlockSpec((tb, r, LANES), lambda i: (nb - 1 - i, 0, 0))
    par = pl.BlockSpec((r, LANES), lambda i: (0, 0))

    def body(dsr_ref, dsi_ref, sre_ref, sim_ref, are_ref, aim_ref, gre_ref, gim_ref, dar_ref, dai_ref, st):
        @pl.when(pl.program_id(0) == 0)
        def _():
            st[...] = jnp.zeros_like(st)
            dar_ref[...] = jnp.zeros_like(dar_ref)
            dai_ref[...] = jnp.zeros_like(dai_ref)

        ar, ai = are_ref[...], aim_ref[...]

        def step(k, carry):
            gr, gi, dar, dai = carry
            q = tb - 1 - k
            s_r, s_i = sre_ref[q], sim_ref[q]
            dar = dar + gr * s_r + gi * s_i
            dai = dai + gi * s_r - gr * s_i
            ngr = dsr_ref[q] + ar * gr + ai * gi
            ngi = dsi_ref[q] + ar * gi - ai * gr
            gre_ref[q] = ngr
            gim_ref[q] = ngi
            return ngr, ngi, dar, dai

        gr, gi, dar, dai = lax.fori_loop(0, tb, step, (st[0], st[1], dar_ref[...], dai_ref[...]), unroll=4)
        st[0] = gr
        st[1] = gi
        dar_ref[...] = dar
        dai_ref[...] = dai

    return pl.pallas_call(
        body, name=name, grid=(nb,), in_specs=[blk, blk, blk, blk, par, par], out_specs=[blk, blk, par, par],
        out_shape=[jax.ShapeDtypeStruct(sre.shape, F32)] * 2 + [jax.ShapeDtypeStruct(are.shape, F32)] * 2,
        scratch_shapes=[pltpu.VMEM((2, r, LANES), F32)],
        compiler_params=_params(("arbitrary",)),
    )(dsr, dsi, sre, sim, are, aim)


def _make_s5_scan(name):
    @jax.custom_vjp
    def scan(bre, bim, are, aim):
        return tuple(_s5_scan_fwd(bre, bim, are, aim, name + "_fwd"))

    def fwd(bre, bim, are, aim):
        sre, sim = _s5_scan_fwd(bre, bim, are, aim, name + "_fwd")
        return (sre, sim), (sre, sim, are, aim)

    def bwd(res, cts):
        sre, sim, are, aim = res
        return tuple(_s5_scan_bwd(cts[0], cts[1], sre, sim, are, aim, name + "_bwd"))

    scan.defvjp(fwd, bwd)
    return scan


def _dot(a, b):
    return jnp.dot(a.astype(BF16), b.astype(BF16), preferred_element_type=F32)


def _dot_nt(a, b):
    return lax.dot_general(a.astype(BF16), b.astype(BF16), (((1,), (1,)), ((), ())), preferred_element_type=F32)


def _dot_tn(a, b):
    return lax.dot_general(a.astype(BF16), b.astype(BF16), (((0,), (0,)), ((), ())), preferred_element_type=F32)


def _split3(x):
    x1 = x.astype(BF16)
    rest = x - x1.astype(F32)
    x2 = rest.astype(BF16)
    return x1, x2, (rest - x2.astype(F32)).astype(BF16)


def _sel_dot(dims, a, b, a_is_sel):
    sel = (a if a_is_sel else b).astype(BF16)
    acc = None
    for piece in _split3(b if a_is_sel else a):
        pair = (sel, piece) if a_is_sel else (piece, sel)
        term = lax.dot_general(*pair, (dims, ((), ())), preferred_element_type=F32)
        acc = term if acc is None else acc + term
    return acc


@jax.custom_vjp
def _running_sum(low, y):
    return _sel_dot(((1,), (0,)), low, y, True)


_running_sum.defvjp(lambda low, y: (_running_sum(low, y), low),
                    lambda low, ct: (jnp.zeros_like(low), _sel_dot(((0,), (0,)), low, ct, True)))


@jax.custom_vjp
def _spread(x, sel):
    return _sel_dot(((1,), (0,)), x, sel, False)


_spread.defvjp(lambda x, sel: (_spread(x, sel), sel),
               lambda sel, ct: (_sel_dot(((1,), (1,)), ct, sel, False), jnp.zeros_like(sel)))


@jax.custom_vjp
def _transposed(x, eye):
    return _sel_dot(((0,), (0,)), x, eye, False)


_transposed.defvjp(lambda x, eye: (_transposed(x, eye), eye),
                   lambda eye, ct: (_sel_dot(((1,), (1,)), eye, ct, True), jnp.zeros_like(eye)))


def _iota(shape, dim):
    return lax.broadcasted_iota(jnp.int32, shape, dim)


def _tri(strict=False):
    r, c = _iota((CHUNK, CHUNK), 0), _iota((CHUNK, CHUNK), 1)
    return (r > c) if strict else (r >= c)


def _silu(x):
    return x * jax.nn.sigmoid(x)


def _layer_norm(z, g, b):
    mu = jnp.mean(z, axis=-1, keepdims=True)
    var = jnp.mean(jnp.square(z - mu), axis=-1, keepdims=True)
    return (z - mu) * lax.rsqrt(var + LN_EPS) * g + b


def _ssd_chunk(state, xbc, sm, a_c, d_exp):
    width = SSD_HEADS * SSD_HEAD
    x = xbc[:, :width]
    lane = _iota((CHUNK, LANES), 1)
    dtc = jnp.where(lane < SSD_HEADS, sm, 0.0)
    low = _tri().astype(F32)
    eye = (_iota((CHUNK, CHUNK), 0) == _iota((CHUNK, CHUNK), 1)).astype(F32)
    head_col, head_row = _iota((LANES, width), 1), _iota((LANES, width), 0) * SSD_HEAD
    expand = ((head_col >= head_row) & (head_col < head_row + SSD_HEAD)).astype(F32)
    acum_c = _running_sum(low, dtc * a_c)
    acum_ct = _transposed(acum_c, eye)
    dt_exp = _spread(dtc, expand)
    acum = _spread(acum_c, expand)
    xd = x * dt_exp
    last = acum[CHUNK - 1:CHUNK, :]
    to_end = jnp.exp(last - acum)
    eac = jnp.exp(acum)
    causal = _tri()
    first_half = _iota((CHUNK, LANES), 1) < SSD_HEAD
    top_rows = _iota((LANES, LANES), 0) < SSD_HEAD
    pairs = range(SSD_HEADS // 2)
    grp = [(2 * p) // (SSD_HEADS // SSD_GROUPS) for p in pairs]
    cols = [slice(p * LANES, (p + 1) * LANES) for p in pairs]
    bg = [xbc[:, width + g * SSD_STATE: width + (g + 1) * SSD_STATE] for g in range(SSD_GROUPS)]
    cg = [xbc[:, width + (SSD_GROUPS + g) * SSD_STATE: width + (SSD_GROUPS + g + 1) * SSD_STATE]
          for g in range(SSD_GROUPS)]
    scores = [_dot_nt(cg[g], bg[g]) for g in range(SSD_GROUPS)]
    dec = [jnp.where(causal, jnp.exp(jnp.minimum(acum_c[:, h:h + 1] - acum_ct[h:h + 1, :], 0.0)), 0.0)
           for h in range(SSD_HEADS)]
    y_lo = [_dot(scores[grp[p]] * dec[2 * p], jnp.where(first_half, xd[:, cols[p]], 0.0)) for p in pairs]
    y_hi = [_dot(scores[grp[p]] * dec[2 * p + 1], jnp.where(first_half, 0.0, xd[:, cols[p]])) for p in pairs]
    s_prev = [state[p * LANES:(p + 1) * LANES, :] for p in pairs]
    y_off = [_dot_nt(cg[grp[p]], s_prev[p]) for p in pairs]
    s_add = [_dot_tn(xd[:, cols[p]] * to_end[:, cols[p]], bg[grp[p]]) for p in pairs]
    ys = [y_lo[p] + y_hi[p] + y_off[p] * eac[:, cols[p]] + x[:, cols[p]] * d_exp[:, cols[p]] for p in pairs]
    cd = [jnp.where(top_rows, jnp.exp(acum_c[CHUNK - 1:CHUNK, 2 * p:2 * p + 1]),
                    jnp.exp(acum_c[CHUNK - 1:CHUNK, 2 * p + 1:2 * p + 2])) for p in pairs]
    new_states = [s_prev[p] * cd[p] + s_add[p] for p in pairs]
    return jnp.concatenate(new_states, axis=0), jnp.concatenate(ys, axis=1)


def _neumann(a_mats, rhs, transposed):
    dot = _dot_tn if transposed else _dot
    nmats, sols = [-a for a in a_mats], list(rhs)
    for i in range(6):
        upd = [dot(n, s) for n, s in zip(nmats, sols)]
        if i < 5:
            nmats = [_dot(n, n) for n in nmats]
        sols = [s + u for s, u in zip(sols, upd)]
    return tuple(sols)


@jax.custom_vjp
def _solve_unit_lower(a_mats, rhs):
    return _neumann(a_mats, rhs, False)


def _solve_unit_lower_fwd(a_mats, rhs):
    sols = _neumann(a_mats, rhs, False)
    return sols, (a_mats, sols)


def _solve_unit_lower_bwd(res, d_sols):
    a_mats, sols = res
    d_rhs = _neumann(a_mats, d_sols, True)
    return tuple(-_dot_nt(dr, x) for dr, x in zip(d_rhs, sols)), d_rhs


_solve_unit_lower.defvjp(_solve_unit_lower_fwd, _solve_unit_lower_bwd)


def _gdn_chunk(state, qkv, sm):
    width = GDN_HEADS * GDN_HEAD
    g0, b0 = SSD_HEADS, SSD_HEADS + GDN_HEADS
    lane = _iota((CHUNK, LANES), 1)
    gc = jnp.where((lane >= g0) & (lane < b0), sm, 0.0)
    low = _tri().astype(F32)
    eye = (_iota((CHUNK, CHUNK), 0) == _iota((CHUNK, CHUNK), 1)).astype(F32)
    gcum = _running_sum(low, gc)
    gcum_t = _transposed(gcum, eye)
    causal, strict = _tri(), _tri(strict=True)
    heads = range(GDN_HEADS)
    q = [qkv[:, h * GDN_HEAD:(h + 1) * GDN_HEAD] for h in heads]
    k = [qkv[:, width + h * GDN_HEAD: width + (h + 1) * GDN_HEAD] for h in heads]
    v = [qkv[:, 2 * width + h * GDN_HEAD: 2 * width + (h + 1) * GDN_HEAD] for h in heads]
    beta = [sm[:, b0 + h:b0 + h + 1] for h in heads]
    gcol = [gcum[:, g0 + h:g0 + h + 1] for h in heads]
    glast = [gcum[CHUNK - 1:CHUNK, g0 + h:g0 + h + 1] for h in heads]
    gamma = [jnp.where(causal, jnp.exp(jnp.minimum(gcol[h] - gcum_t[g0 + h:g0 + h + 1, :], 0.0)), 0.0) for h in heads]
    kk = [_dot_nt(k[h], k[h]) for h in heads]
    qk = [_dot_nt(q[h], k[h]) for h in heads]
    egc = [jnp.exp(gcol[h]) for h in heads]
    a_mat = tuple(jnp.where(strict, kk[h] * gamma[h] * beta[h], 0.0) for h in heads)
    sol = _solve_unit_lower(a_mat, tuple(jnp.concatenate([v[h] * beta[h], k[h] * (beta[h] * egc[h])], axis=1)
                                         for h in heads))
    s_prev = [state[h * GDN_HEAD:(h + 1) * GDN_HEAD, :] for h in heads]
    w_s = [_dot(sol[h][:, GDN_HEAD:], s_prev[h]) for h in heads]
    q_s = [_dot(q[h] * egc[h], s_prev[h]) for h in heads]
    v_new = [sol[h][:, :GDN_HEAD] - w_s[h] for h in heads]
    a_v = [_dot(qk[h] * gamma[h], v_new[h]) for h in heads]
    k_v = [_dot_tn(k[h] * jnp.exp(glast[h] - gcol[h]), v_new[h]) for h in heads]
    outs = [q_s[h] + a_v[h] for h in heads]
    new_states = [s_prev[h] * jnp.exp(glast[h]) + k_v[h] for h in heads]
    return jnp.concatenate(new_states, axis=0), jnp.concatenate(outs, axis=1)


def _row_fns(d_model, pad_rows, loss_rows, alpha):
    def keep(ridx, v):
        return jnp.where(ridx >= pad_rows, v, 0.0)

    def both(h):
        return h, h.astype(BF16)

    def ln_in(ridx, h, g, b):
        return both(keep(ridx, _layer_norm(h, g, b)))

    def s5_tail(y_re, y_im, u, z, d, bias, glu, branch):
        v0 = jax.nn.gelu(y_re + y_im + d * u)
        return (branch(v0 * jax.nn.sigmoid(glu(v0) + bias) * _silu(z)),)

    def small_act(ridx, raw, bias, scale):
        lane = _iota(raw.shape, 1)
        sp = jax.nn.softplus(raw + bias)
        g0, b0 = SSD_HEADS, SSD_HEADS + GDN_HEADS
        out = jnp.where(lane < g0, sp, jnp.where(lane < b0, scale * sp,
                                                 jnp.where(lane < b0 + GDN_HEADS, jax.nn.sigmoid(raw), 0.0)))
        return (keep(ridx, out),)

    def conv(xs, w):
        acc = xs[0] * w[0:1, :]
        for j in range(1, CONV_K):
            acc = acc + xs[j] * w[j:j + 1, :]
        return acc

    def ssd_conv(ridx, x0, x1, x2, x3, w, b):
        return (keep(ridx, _silu(conv((x0, x1, x2, x3), w) + b)),)

    def ssd_tail(y, z, g, branch):
        v = y * _silu(z)
        return (branch(v * lax.rsqrt(jnp.mean(v * v, axis=-1, keepdims=True) + LN_EPS) * g),)

    def gdn_conv(x0, x1, x2, x3, w):
        a = _silu(conv((x0, x1, x2, x3), w))
        width = GDN_HEADS * GDN_HEAD
        parts = []
        for h in range(2 * GDN_HEADS):
            z = a[:, h * GDN_HEAD:(h + 1) * GDN_HEAD]
            z = z * lax.rsqrt(jnp.sum(z * z, axis=-1, keepdims=True) + 1e-6)
            parts.append(z * GDN_HEAD ** -0.5 if h < GDN_HEADS else z)
        parts.append(a[:, 2 * width:])
        return (jnp.concatenate(parts, axis=1),)

    def gdn_tail(o, z, g, branch):
        parts = []
        for h in range(GDN_HEADS):
            cols = slice(h * GDN_HEAD, (h + 1) * GDN_HEAD)
            oh = o[:, cols]
            oh = oh * lax.rsqrt(jnp.mean(oh * oh, axis=-1, keepdims=True) + LN_EPS) * g
            parts.append(oh * _silu(z[:, cols]))
        return (branch(jnp.concatenate(parts, axis=1)),)

    def merge_out(ridx, oa, ob, oc, gate, h, bias, g, b, w_out):
        acc = None
        for k, o in enumerate((oa, ob, oc)):
            cols = slice(k * d_model, (k + 1) * d_model)
            term = jax.nn.sigmoid(gate[:, cols] + bias[:, cols]) * o
            acc = term if acc is None else acc + term
        return both(keep(ridx, _layer_norm(alpha * h + w_out(acc), g, b)))

    def loss_rows_fn(ridx, h, tgt):
        row = 0.5 * jnp.mean(jnp.square(h - tgt), axis=-1, keepdims=True)
        row = jnp.where(ridx >= loss_rows, row, 0.0)
        lane = _iota((h.shape[0], LANES), 1)
        return (jnp.where(lane == 0, row, 0.0),)

    return dict(ln_in=ln_in, s5_tail=s5_tail, small_act=small_act, ssd_conv=ssd_conv, ssd_tail=ssd_tail,
                gdn_conv=gdn_conv, gdn_tail=gdn_tail, merge_out=merge_out, loss=loss_rows_fn)


def _pair_exchange(srcs, whole_srcs, name):
    n_arr, n_whole = len(srcs), len(whole_srcs)
    n_chip = N_DEV // 2
    base = n_arr * n_chip

    def body(*refs):
        src_refs, recv_refs = refs[:n_arr + n_whole], refs[n_arr + n_whole:2 * (n_arr + n_whole)]
        send_sems, recv_sems = refs[2 * (n_arr + n_whole):]
        x, y, c = lax.axis_index("x"), lax.axis_index("y"), lax.axis_index("c")

        def to_sibling(src, dst, k):
            return pltpu.make_async_remote_copy(src_ref=src, dst_ref=dst, send_sem=send_sems.at[k],
                                                recv_sem=recv_sems.at[k], device_id=(x, y, 1 - c),
                                                device_id_type=pl.DeviceIdType.MESH)

        copies = [to_sibling(src_refs[a].at[q, 1 - c], recv_refs[a].at[q], a * n_chip + q)
                  for a in range(n_arr) for q in range(n_chip)]
        copies += [to_sibling(src_refs[n_arr + b], recv_refs[n_arr + b], base + b) for b in range(n_whole)]
        for cp in copies:
            cp.start()
        for cp in copies:
            cp.wait()

    n_sem = base + n_whole
    return pl.pallas_call(
        body, name=name,
        in_specs=[pl.BlockSpec(memory_space=pl.ANY)] * (n_arr + n_whole),
        out_specs=[pl.BlockSpec(memory_space=pl.ANY)] * (n_arr + n_whole),
        out_shape=[jax.ShapeDtypeStruct((n_chip,) + tuple(s.shape[2:]), s.dtype) for s in srcs]
        + [jax.ShapeDtypeStruct(s.shape, s.dtype) for s in whole_srcs],
        scratch_shapes=[pltpu.SemaphoreType.DMA((n_sem,)), pltpu.SemaphoreType.DMA((n_sem,))],
    )(*srcs, *whole_srcs)


def _pair_sum(src, recv, core, name):
    n_chip, rows, cols = recv.shape
    tr = _row_tile(rows, 1024 if cols <= LANES else 512)

    def body(core_ref, a_ref, b_ref, o_ref):
        o_ref[...] = (a_ref[0].astype(F32) + b_ref[...].astype(F32)).astype(o_ref.dtype)

    blk = pl.BlockSpec((1, tr, cols), lambda q, i, core_ref: (q, i, 0))
    return pl.pallas_call(
        body, name=name,
        grid_spec=pltpu.PrefetchScalarGridSpec(
            num_scalar_prefetch=1, grid=(n_chip, rows // tr),
            in_specs=[pl.BlockSpec((1, 1, tr, cols), lambda q, i, core_ref: (q, core_ref[0], i, 0)), blk],
            out_specs=blk),
        out_shape=jax.ShapeDtypeStruct(recv.shape, recv.dtype),
        compiler_params=_params(("arbitrary", "arbitrary")),
    )(core, src, recv)


def _add2(a, b, name):
    rows = a.shape[0]
    tr = _row_tile(rows, FLAT_ROWS)
    blk = pl.BlockSpec((tr, LANES), lambda i: (i, 0))

    def body(a_ref, b_ref, o_ref):
        o_ref[...] = a_ref[...] + b_ref[...]

    return pl.pallas_call(body, name=name, grid=(rows // tr,), in_specs=[blk, blk], out_specs=blk,
                          out_shape=jax.ShapeDtypeStruct(a.shape, F32), compiler_params=_params(("arbitrary",)))(a, b)


def _add_parts(parts, name):
    n, rows, _ = parts.shape
    tr = _row_tile(rows, FLAT_ROWS)

    def body(p_ref, o_ref):
        acc = p_ref[0]
        for k in range(1, n):
            acc = acc + p_ref[k]
        o_ref[...] = acc

    return pl.pallas_call(body, name=name, grid=(rows // tr,),
                          in_specs=[pl.BlockSpec((n, tr, LANES), lambda i: (0, i, 0))],
                          out_specs=pl.BlockSpec((tr, LANES), lambda i: (i, 0)),
                          out_shape=jax.ShapeDtypeStruct((rows, LANES), F32),
                          compiler_params=_params(("arbitrary",)))(parts)


def _chip_exchange(chip_srcs, whole_srcs, name):
    n_chip_arr, n_whole = len(chip_srcs), len(whole_srcs)
    n_arr = n_chip_arr + n_whole
    n_chip = N_DEV // 2
    chip_flips = [(1, 0), (0, 1), (1, 1)]

    def body(*refs):
        src_refs, out_refs = refs[:n_arr], refs[n_arr:2 * n_arr]
        send_sems, recv_sems = refs[2 * n_arr:]
        x, y, c = lax.axis_index("x"), lax.axis_index("y"), lax.axis_index("c")
        my_chip = 2 * x + y
        copies = []
        for a in range(n_arr):
            for k, (fx, fy) in enumerate(chip_flips):
                px = 1 - x if fx else x
                py = 1 - y if fy else y
                src = src_refs[a].at[2 * px + py] if a < n_chip_arr else src_refs[a]
                copies.append(pltpu.make_async_remote_copy(
                    src_ref=src, dst_ref=out_refs[a].at[my_chip],
                    send_sem=send_sems.at[a * 3 + k], recv_sem=recv_sems.at[a * 3 + k],
                    device_id=(px, py, c), device_id_type=pl.DeviceIdType.MESH))
        for cp in copies:
            cp.start()
        for cp in copies:
            cp.wait()

    n_sem = n_arr * len(chip_flips)
    return pl.pallas_call(
        body, name=name,
        in_specs=[pl.BlockSpec(memory_space=pl.ANY)] * n_arr, out_specs=[pl.BlockSpec(memory_space=pl.ANY)] * n_arr,
        out_shape=[jax.ShapeDtypeStruct(s.shape, s.dtype) for s in chip_srcs]
        + [jax.ShapeDtypeStruct((n_chip,) + tuple(s.shape), s.dtype) for s in whole_srcs],
        scratch_shapes=[pltpu.SemaphoreType.DMA((n_sem,)), pltpu.SemaphoreType.DMA((n_sem,))],
    )(*chip_srcs, *whole_srcs)


def _gather(srcs, name):
    n_arr = len(srcs)
    n_sem = N_DEV - 1

    def body(*refs):
        src_refs, out_refs = refs[:n_arr], refs[n_arr:2 * n_arr]
        send_sems, recv_sems = refs[2 * n_arr:]
        x, y, c = lax.axis_index("x"), lax.axis_index("y"), lax.axis_index("c")
        me, sibling = (x, y, c), (x, y, 1 - c)
        chips = [(1 - x, y), (x, 1 - y), (1 - x, 1 - y)]

        def slot(a, dev):
            return out_refs[a].at[4 * dev[0] + 2 * dev[1] + dev[2]]

        def copy(a, k, block, to, own=False):
            return pltpu.make_async_remote_copy(
                src_ref=src_refs[a] if own else slot(a, block), dst_ref=slot(a, block),
                send_sem=send_sems.at[a * n_sem + k], recv_sem=recv_sems.at[a * n_sem + k],
                device_id=to, device_id_type=pl.DeviceIdType.MESH)

        arrays = range(n_arr)
        first = [copy(a, 0, me, sibling, own=True) for a in arrays]
        first += [copy(a, 1 + j, me, (*chip, c), own=True) for j, chip in enumerate(chips) for a in arrays]
        for cp in first:
            cp.start()
        passed = []
        for j, chip in enumerate(chips):
            for a in arrays:
                copy(a, 1 + j, (*chip, c), me).wait_recv()
                fwd = copy(a, 4 + j, (*chip, c), sibling)
                fwd.start()
                passed.append(fwd)
        for a in arrays:
            copy(a, 0, sibling, me).wait_recv()
        for j, chip in enumerate(chips):
            for a in arrays:
                copy(a, 4 + j, (*chip, 1 - c), me).wait_recv()
        for cp in first + passed:
            cp.wait_send()

    return pl.pallas_call(
        body, name=name,
        in_specs=[pl.BlockSpec(memory_space=pl.ANY)] * n_arr, out_specs=[pl.BlockSpec(memory_space=pl.ANY)] * n_arr,
        out_shape=[jax.ShapeDtypeStruct((N_DEV,) + tuple(s.shape), s.dtype) for s in srcs],
        scratch_shapes=[pltpu.SemaphoreType.DMA((n_arr * n_sem,)), pltpu.SemaphoreType.DMA((n_arr * n_sem,))],
    )(*srcs)


def _adamw_body(p_ref, w_ref, m_ref, v_ref, g_ref, d_ref, nm_ref, nv_ref):
    bc1 = 1.0 - ADAM_B1 ** ADAM_STEP
    bc2 = 1.0 - ADAM_B2 ** ADAM_STEP
    g = p_ref[0].astype(F32)
    for k in range(1, p_ref.shape[0]):
        g = g + p_ref[k].astype(F32)
    nm = ADAM_B1 * m_ref[...] + (1.0 - ADAM_B1) * g
    nv = ADAM_B2 * v_ref[...] + (1.0 - ADAM_B2) * jnp.square(g)
    m_hat = nm / bc1
    v_hat = nv / bc2
    g_ref[...] = g
    d_ref[...] = -ADAM_LR * (m_hat / (jnp.sqrt(v_hat) + ADAM_EPS) + ADAM_WD * w_ref[...])
    nm_ref[...] = nm
    nv_ref[...] = nv


def _adamw_rows(parts, w, m, v, name):
    rows, cols = w.shape
    tr = _row_tile(rows, max(128, (1 << 18) // cols))
    blk = pl.BlockSpec((tr, cols), lambda i: (i, 0))
    return pl.pallas_call(
        functools.partial(_adamw_body), name=name, grid=(rows // tr,),
        in_specs=[pl.BlockSpec((parts.shape[0], tr, cols), lambda i: (0, i, 0)), blk, blk, blk],
        out_specs=[blk] * 4, out_shape=[jax.ShapeDtypeStruct(w.shape, F32)] * 4,
        compiler_params=_params(("arbitrary",)),
    )(parts, w, m, v)


def _pad_flat(vec, rows):
    return jnp.pad(vec, (0, rows * LANES - vec.shape[0])).reshape(rows, LANES)


def _rows_for(n):
    return -(-n // (FLAT_ROWS * LANES)) * FLAT_ROWS


def _join_shards(rows, local_shape, dim):
    parts = jnp.moveaxis(rows.reshape((N_DEV,) + tuple(local_shape)), 0, dim)
    shp = tuple(local_shape)
    return parts.reshape(shp[:dim] + (N_DEV * shp[dim],) + shp[dim + 1:])


def _s5_tables(a_re, a_im, log_step, b_re, b_im, c_re, c_im):
    lam_re = jnp.minimum(a_re, -1e-4)
    lam_im = a_im
    step = jnp.exp(log_step)[:, None]
    mag = jnp.exp(lam_re * step)
    abar_re, abar_im = mag * jnp.cos(lam_im * step), mag * jnp.sin(lam_im * step)
    den = lam_re * lam_re + lam_im * lam_im
    nr, ni = abar_re - 1.0, abar_im
    coef_re = (nr * lam_re + ni * lam_im) / den
    coef_im = (ni * lam_re - nr * lam_im) / den
    bbar_re = coef_re[..., None] * b_re - coef_im[..., None] * b_im
    bbar_im = coef_re[..., None] * b_im + coef_im[..., None] * b_re
    groups = a_re.shape[0]
    nblk = groups // S5_BLOCK_GROUPS
    eye = jnp.eye(S5_BLOCK_GROUPS, dtype=F32)

    def in_blocks(bb):
        t = jnp.swapaxes(bb, 1, 2).reshape(nblk, S5_BLOCK_GROUPS, S5_GROUP, S5_STATE)
        blk = jnp.einsum('ab,jacp->jacbp', eye, t)
        return blk.reshape(nblk, S5_BLOCK_GROUPS * S5_GROUP, S5_BLOCK_GROUPS * S5_STATE)

    def out_blocks(cc):
        t = jnp.swapaxes(cc, 1, 2).reshape(nblk, S5_BLOCK_GROUPS, S5_STATE, S5_GROUP)
        blk = jnp.einsum('ab,japc->japbc', eye, t)
        return blk.reshape(nblk, S5_BLOCK_GROUPS * S5_STATE, S5_BLOCK_GROUPS * S5_GROUP)

    rows = groups * S5_STATE // LANES
    return dict(b_re=in_blocks(bbar_re), b_im=in_blocks(bbar_im), c_re=out_blocks(c_re), c_im=out_blocks(-c_im),
                a_re=abar_re.reshape(rows, LANES), a_im=abar_im.reshape(rows, LANES))


def _in_widths(d_model):
    return [BRANCH, BRANCH, SSD_HEADS * SSD_HEAD + 2 * SSD_GROUPS * SSD_STATE, SSD_HEADS, BRANCH,
            3 * BRANCH, GDN_HEADS, GDN_HEADS, BRANCH, 3 * d_model]


def _local_loss(w, mats, x, target):
    n_meta, d_model = w['meta'].shape
    depth = len(mats['in'])
    seq = x.shape[0]
    pad_rows = CHUNK - n_meta
    first = pad_rows + n_meta
    t_all = first + seq
    alpha = (2 * depth) ** 0.25
    fns = _row_fns(d_model, pad_rows, first, alpha)
    row = lambda nm, key, n_row, n_par, widths, cap, ridx=False: _make_rowwise(nm, fns[key], n_row, n_par, widths, cap, ridx)

    h = jnp.concatenate([jnp.zeros((pad_rows, d_model), F32), w['meta'], x], axis=0)
    h, h16 = _make_rowwise("ln_in", fns['ln_in'], 1, 2, [d_model, d_model], 416, True, out_dtypes=[F32, BF16])(
        h, w['ln_in_g'][None], w['ln_in_b'][None])

    n_small = SSD_HEADS + 2 * GDN_HEADS

    def small_cols(ps):
        return jnp.pad(jnp.concatenate([ps[3], ps[6], ps[7]], axis=1), ((0, 0), (0, LANES - n_small)))

    for l in range(depth):
        pw, pc = mats['in'][l], w['c_in'][l]
        used = (0, 1, 2, 4, 5, 8, 9)
        s5_u, s5_z, ssd_xbc, ssd_z, gdn_qkv, gdn_z, gate, small = _in_proj(
            ["in_s5u", "in_s5z", "in_ssdx", "in_ssdz", "in_gdnq", "in_gdnz", "in_gate", "in_small"], h, h16,
            [pw[i] for i in used] + [small_cols(pw)], [pc[i] for i in used] + [small_cols(pc)])

        zeros_tail = jnp.zeros((LANES - n_small,), F32)
        bias = jnp.concatenate([w['ssd_dt_bias'][l], w['gdn_dt_bias'][l], jnp.zeros((GDN_HEADS,), F32), zeros_tail])[None]
        scale = jnp.concatenate([jnp.ones((SSD_HEADS,), F32), -jnp.exp(w['gdn_a_log'][l]),
                                 jnp.zeros((GDN_HEADS,), F32), zeros_tail])[None]
        (sm,) = row("small_act", 'small_act', 1, 2, [LANES], 832, True)(small, bias, scale)

        tb = _s5_tables(w['s5_a_re'][l], w['s5_a_im'][l], w['s5_log_step'][l], w['s5_b_re'][l], w['s5_b_im'][l],
                        w['s5_c_re'][l], w['s5_c_im'][l])
        srows = tb['a_re'].shape[0]
        bu_re = _make_gmm("s5_bre")(s5_u, tb['b_re']).reshape(t_all, srows, LANES)
        bu_im = _make_gmm("s5_bim")(s5_u, tb['b_im']).reshape(t_all, srows, LANES)
        s_re, s_im = _make_s5_scan("s5_scan")(bu_re, bu_im, tb['a_re'], tb['a_im'])
        y_re = _make_gmm("s5_cre")(s_re.reshape(t_all, srows * LANES), tb['c_re'])
        y_im = _make_gmm("s5_cim")(s_im.reshape(t_all, srows * LANES), tb['c_im'])
        (out_a,) = _make_rowwise("s5_tail", fns['s5_tail'], 4, 2, [d_model], 208, n_wt=2)(
            y_re, y_im, s5_u, s5_z, w['s5_d'][l][None], w['s5_b_glu'][l][None],
            mats['glu'][l], mats['branch'][l, 0], w['c_glu'][l], w['c_branch'][l, 0])

        xbc = _make_conv_rowwise("ssd_conv", fns['ssd_conv'], 2, ssd_xbc.shape[1], 416, True)(
            ssd_xbc, w['ssd_conv_w'][l], w['ssd_conv_b'][l][None])
        a_c = jnp.pad(-jnp.exp(w['ssd_a_log'][l]), (0, LANES - SSD_HEADS))[None]
        d_exp = jnp.repeat(w['ssd_d'][l], SSD_HEAD)[None]
        (y_ssd,) = _make_chunk_scan("ssd_scan", _ssd_chunk, (SSD_HEADS // 2 * LANES, SSD_STATE), 2, 2, [BRANCH])(
            xbc, sm, a_c, d_exp)
        (out_b,) = _make_rowwise("ssd_tail", fns['ssd_tail'], 2, 1, [d_model], 416, n_wt=1)(
            y_ssd, ssd_z, w['ssd_norm_g'][l][None], mats['branch'][l, 1], w['c_branch'][l, 1])

        qkv = _make_conv_rowwise("gdn_conv", fns['gdn_conv'], 1, 3 * BRANCH, 208)(gdn_qkv, w['gdn_conv_w'][l])
        (o_gdn,) = _make_chunk_scan("gdn_scan", _gdn_chunk, (GDN_HEADS * GDN_HEAD, GDN_HEAD), 2, 0, [BRANCH])(qkv, sm)
        (out_c,) = _make_rowwise("gdn_tail", fns['gdn_tail'], 2, 1, [d_model], 416, n_wt=1)(
            o_gdn, gdn_z, w['gdn_norm_g'][l][None], mats['branch'][l, 2], w['c_branch'][l, 2])

        h, h16 = _make_rowwise("merge_out", fns['merge_out'], 5, 3, [d_model, d_model], 208, True, n_wt=1,
                               out_dtypes=[F32, BF16])(
            out_a, out_b, out_c, gate, h, w['b_gate'][l].reshape(1, 3 * d_model), w['ln_g'][l][None],
            w['ln_b'][l][None], mats['out'][l], w['c_out'][l])

    tgt = jnp.concatenate([jnp.zeros((first, d_model), F32), target], axis=0)
    (rows_loss,) = row("loss", 'loss', 2, 0, [LANES], 416, True)(h, tgt)
    return jnp.sum(rows_loss)


def _in_overlaps(d_model, n_loc):
    offs = [0]
    for wd in _in_widths(d_model):
        offs.append(offs[-1] + wd)
    out = []
    for i in range(len(offs) - 1):
        c0, c1 = offs[i], offs[i + 1]
        segs = []
        for k in range(N_DEV):
            g0, g1 = max(c0, k * n_loc), min(c1, (k + 1) * n_loc)
            if g0 < g1:
                segs.append((k, g0 - k * n_loc, g1 - k * n_loc, g0 - c0))
        out.append(segs)
    return out


STACKED = ['w_in', 's5_w_glu', 'w_branch', 'w_out']


def _shard_blocks(full, dim):
    shp = full.shape
    parts = full.reshape(shp[:dim] + (N_DEV, shp[dim] // N_DEV) + shp[dim + 1:])
    return jnp.moveaxis(parts, dim, 0)


def _join_blocks(blocks, dim):
    shp = blocks.shape[1:]
    return jnp.moveaxis(blocks, 0, dim).reshape(shp[:dim] + (N_DEV * shp[dim],) + shp[dim + 1:])


def _step(x, target, w_loc, m_loc, v_loc):
    small = [n for n in WEIGHTS if n in SHARD_DIM and n not in STACKED]
    repl = [n for n in WEIGHTS if n not in SHARD_DIM]
    size = lambda names: sum(int(w_loc[n].size) for n in names)
    depth, d_model, n_loc = w_loc['w_in'].shape
    overlaps = _in_overlaps(d_model, n_loc)
    n_chip = N_DEV // 2
    me = 4 * lax.axis_index("x") + 2 * lax.axis_index("y") + lax.axis_index("c")
    my_chip = 2 * lax.axis_index("x") + lax.axis_index("y")
    core = lax.axis_index("c").astype(jnp.int32).reshape(1)
    put = lambda buf, blk, idx: lax.dynamic_update_index_in_dim(buf, blk, idx, 0)
    take = lambda buf, idx: lax.dynamic_index_in_dim(buf, idx, 0, keepdims=False)

    rows_small = _rows_for(size(small))
    small_flat = _pad_flat(jnp.concatenate([w_loc[n].reshape(-1) for n in small]), rows_small)
    own_blocks = [w_loc[n].astype(BF16) for n in STACKED] + [small_flat]
    gathered = [put(got, blk, me) for got, blk in zip(_gather(own_blocks, "gather_weights"), own_blocks)]
    g_in = gathered[0]
    full = {n: _join_blocks(g, SHARD_DIM[n]) for n, g in zip(STACKED[1:], gathered[1:])}
    buf, off = gathered[-1].reshape(N_DEV, -1), 0
    for n in small:
        sz = int(w_loc[n].size)
        full[n] = _join_shards(buf[:, off:off + sz], w_loc[n].shape, SHARD_DIM[n])
        off += sz
    mats = dict(glu=full['s5_w_glu'], branch=full['w_branch'], out=full['w_out'], **{
        'in': [[jnp.concatenate([g_in[k, l, :, lo:hi] for k, lo, hi, _ in segs], axis=1) for segs in overlaps]
               for l in range(depth)]})
    w_diff = {n: w_loc[n] for n in repl}
    w_diff.update({n: full[n] for n in small})
    w_diff['c_in'] = [[jnp.zeros((d_model, wd), F32) for wd in _in_widths(d_model)] for _ in range(depth)]
    w_diff['c_glu'] = jnp.zeros(full['s5_w_glu'].shape, F32)
    w_diff['c_branch'] = jnp.zeros(full['w_branch'].shape, F32)
    w_diff['c_out'] = jnp.zeros(full['w_out'].shape, F32)

    loss, (g_w, g_x) = jax.value_and_grad(_local_loss, argnums=(0, 2))(w_diff, mats, x[0], target[0])
    g_w['s5_w_glu'], g_w['w_branch'], g_w['w_out'] = g_w['c_glu'], g_w['c_branch'], g_w['c_out']

    blocks = {'w_in': jnp.stack([jnp.stack([
        jnp.concatenate([g_w['c_in'][l][i][:, plo:plo + hi - lo]
                         for i, segs in enumerate(overlaps) for (kk, lo, hi, plo) in segs if kk == k], axis=1)
        for l in range(depth)]) for k in range(N_DEV)])}
    for n in STACKED[1:]:
        blocks[n] = _shard_blocks(g_w[n], SHARD_DIM[n])
    cols = {n: w_loc[n].shape[-1] for n in STACKED}
    send = [blocks[n].astype(BF16).reshape(n_chip, 2, -1, cols[n]) for n in STACKED]
    red_names = small + repl
    n_red = sum(int(g_w[n].size) for n in red_names) + 1
    red = _pad_flat(jnp.concatenate([g_w[n].reshape(-1) for n in red_names] + [loss.reshape(1)]), _rows_for(n_red))
    *recv, red_sib = _pair_exchange(send, [red], "pair_exchange")
    sums = [_pair_sum(s, r, core, "pair_sum_" + n) for s, r, n in zip(send, recv, STACKED)]
    red_chip = _add2(red, red_sib, "pair_sum_rest")
    *parts, red_parts = _chip_exchange(sums, [red_chip], "exchange_grads")
    parts = [put(p, take(s, my_chip), my_chip) for p, s in zip(parts, sums)]
    red_total = _add_parts(put(red_parts, red_chip, my_chip), "reduce_rest").reshape(-1)

    outs = {}
    for n, p in zip(STACKED, parts):
        as_rows = lambda a: a.reshape(-1, cols[n])
        res = _adamw_rows(p, as_rows(w_loc[n]), as_rows(m_loc[n]), as_rows(v_loc[n]), "adamw_" + n)
        outs[n] = [r.reshape(w_loc[n].shape) for r in res]
    g_red, off = {}, 0
    for n in red_names:
        sz = int(g_w[n].size)
        g_red[n] = red_total[off:off + sz].reshape(g_w[n].shape)
        off += sz
    loss_total = red_total[off]
    for n in small:
        g_red[n] = take(_shard_blocks(g_red[n], SHARD_DIM[n]), me)
    rows_rest = _rows_for(size(red_names))
    flat = lambda src: _pad_flat(jnp.concatenate([src[n].reshape(-1) for n in red_names]), rows_rest)
    res = _adamw_rows(flat(g_red)[None], flat(w_loc), flat(m_loc), flat(v_loc), "adamw_rest")
    off = 0
    for n in red_names:
        sz = int(w_loc[n].size)
        outs[n] = [r.reshape(-1)[off:off + sz].reshape(w_loc[n].shape) for r in res]
        off += sz
    return (loss_total, g_x[None], *[outs[n][k] for k in range(4) for n in WEIGHTS])


def kernel(x, meta, ln_in_g, ln_in_b, w_in, s5_a_re, s5_a_im, s5_log_step, s5_b_re, s5_b_im, s5_c_re, s5_c_im, s5_d, s5_w_glu, s5_b_glu, ssd_conv_w, ssd_conv_b, ssd_dt_bias, ssd_a_log, ssd_d, ssd_norm_g, gdn_conv_w, gdn_dt_bias, gdn_a_log, gdn_norm_g, w_branch, b_gate, w_out, ln_g, ln_b, loss_target, m_meta, m_ln_in_g, m_ln_in_b, m_w_in, m_s5_a_re, m_s5_a_im, m_s5_log_step, m_s5_b_re, m_s5_b_im, m_s5_c_re, m_s5_c_im, m_s5_d, m_s5_w_glu, m_s5_b_glu, m_ssd_conv_w, m_ssd_conv_b, m_ssd_dt_bias, m_ssd_a_log, m_ssd_d, m_ssd_norm_g, m_gdn_conv_w, m_gdn_dt_bias, m_gdn_a_log, m_gdn_norm_g, m_w_branch, m_b_gate, m_w_out, m_ln_g, m_ln_b, v_meta, v_ln_in_g, v_ln_in_b, v_w_in, v_s5_a_re, v_s5_a_im, v_s5_log_step, v_s5_b_re, v_s5_b_im, v_s5_c_re, v_s5_c_im, v_s5_d, v_s5_w_glu, v_s5_b_glu, v_ssd_conv_w, v_ssd_conv_b, v_ssd_dt_bias, v_ssd_a_log, v_ssd_d, v_ssd_norm_g, v_gdn_conv_w, v_gdn_dt_bias, v_gdn_a_log, v_gdn_norm_g, v_w_branch, v_b_gate, v_w_out, v_ln_g, v_ln_b):
    w_loc = dict(zip(WEIGHTS, (meta, ln_in_g, ln_in_b, w_in, s5_a_re, s5_a_im, s5_log_step, s5_b_re, s5_b_im, s5_c_re, s5_c_im, s5_d, s5_w_glu, s5_b_glu, ssd_conv_w, ssd_conv_b, ssd_dt_bias, ssd_a_log, ssd_d, ssd_norm_g, gdn_conv_w, gdn_dt_bias, gdn_a_log, gdn_norm_g, w_branch, b_gate, w_out, ln_g, ln_b)))
    m_loc = dict(zip(WEIGHTS, (m_meta, m_ln_in_g, m_ln_in_b, m_w_in, m_s5_a_re, m_s5_a_im, m_s5_log_step, m_s5_b_re, m_s5_b_im, m_s5_c_re, m_s5_c_im, m_s5_d, m_s5_w_glu, m_s5_b_glu, m_ssd_conv_w, m_ssd_conv_b, m_ssd_dt_bias, m_ssd_a_log, m_ssd_d, m_ssd_norm_g, m_gdn_conv_w, m_gdn_dt_bias, m_gdn_a_log, m_gdn_norm_g, m_w_branch, m_b_gate, m_w_out, m_ln_g, m_ln_b)))
    v_loc = dict(zip(WEIGHTS, (v_meta, v_ln_in_g, v_ln_in_b, v_w_in, v_s5_a_re, v_s5_a_im, v_s5_log_step, v_s5_b_re, v_s5_b_im, v_s5_c_re, v_s5_c_im, v_s5_d, v_s5_w_glu, v_s5_b_glu, v_ssd_conv_w, v_ssd_conv_b, v_ssd_dt_bias, v_ssd_a_log, v_ssd_d, v_ssd_norm_g, v_gdn_conv_w, v_gdn_dt_bias, v_gdn_a_log, v_gdn_norm_g, v_w_branch, v_b_gate, v_w_out, v_ln_g, v_ln_b)))
    return _step(x, loss_target, w_loc, m_loc, v_loc)
```

```python
import functools

import jax
import jax.numpy as jnp
from jax import lax
from jax.experimental import pallas as pl
from jax.experimental.pallas import tpu as pltpu

F32 = jnp.float32
BF16 = jnp.bfloat16

N_DEV = 8
LANES = 128
SUBLANES = 8
VMEM_LIMIT = 56 * 1024 * 1024
FLAT_ROWS = 1024

CHUNK = 64
CONV_K = 4
S5_GROUP = 16
S5_STATE = 64
S5_BLOCK_GROUPS = 8
SSD_HEAD = 64
SSD_HEADS = 12
SSD_GROUPS = 2
SSD_STATE = 128
GDN_HEAD = 128
GDN_HEADS = 6
BRANCH = 768
LN_EPS = 1e-5

ADAM_LR = 0.001
ADAM_B1 = 0.9
ADAM_B2 = 0.999
ADAM_EPS = 1e-08
ADAM_WD = 0.01
ADAM_STEP = 10

WEIGHTS = ['meta', 'ln_in_g', 'ln_in_b', 'w_in', 's5_a_re', 's5_a_im', 's5_log_step', 's5_b_re', 's5_b_im',
           's5_c_re', 's5_c_im', 's5_d', 's5_w_glu', 's5_b_glu', 'ssd_conv_w', 'ssd_conv_b', 'ssd_dt_bias',
           'ssd_a_log', 'ssd_d', 'ssd_norm_g', 'gdn_conv_w', 'gdn_dt_bias', 'gdn_a_log', 'gdn_norm_g',
           'w_branch', 'b_gate', 'w_out', 'ln_g', 'ln_b']
SHARD_DIM = {'meta': 1, 'w_in': 2, 's5_w_glu': 1, 'ssd_conv_w': 2, 'gdn_conv_w': 2, 'w_branch': 3, 'b_gate': 2,
             'w_out': 1}


def _params(sem):
    return pltpu.CompilerParams(dimension_semantics=sem, vmem_limit_bytes=VMEM_LIMIT)


def _row_tile(m, cap):
    best = None
    for t in range(SUBLANES, min(m, cap) + 1, SUBLANES):
        if m % t == 0:
            best = t
    return best if best is not None else m


def _col_tile(n, cap):
    best = None
    for t in range(LANES, min(n, cap) + 1, LANES):
        if n % t == 0:
            best = t
    return best if best is not None else n


def _any_tile(m, cap):
    best = 1
    for t in range(1, min(m, cap) + 1):
        if m % t == 0:
            best = t
    return best


def _grouped_rows(kind, x, y, g, k, n, name):
    m = x.shape[0]
    tm = _row_tile(m, 832)
    rows = lambda arr: pl.BlockSpec((tm, arr.shape[1]), lambda i: (i, 0))
    whole = pl.BlockSpec((g, k, n), lambda i: (0, 0, 0))

    def body(x_ref, y_ref, o_ref):
        if kind == 'db':
            @pl.when(pl.program_id(0) == 0)
            def _():
                o_ref[...] = jnp.zeros_like(o_ref)

        for gi in range(g):
            if kind == 'fwd':
                o_ref[:, gi * n:(gi + 1) * n] = _dot(x_ref[:, gi * k:(gi + 1) * k], y_ref[gi])
            elif kind == 'da':
                o_ref[:, gi * k:(gi + 1) * k] = _dot_nt(x_ref[:, gi * n:(gi + 1) * n], y_ref[gi])
            else:
                o_ref[gi] += _dot_tn(x_ref[:, gi * k:(gi + 1) * k], y_ref[:, gi * n:(gi + 1) * n])

    if kind == 'db':
        in_specs, out_spec, out_shape = [rows(x), rows(y)], whole, (g, k, n)
    else:
        width = g * n if kind == 'fwd' else g * k
        in_specs, out_spec, out_shape = [rows(x), whole], pl.BlockSpec((tm, width), lambda i: (i, 0)), (m, width)
    return pl.pallas_call(
        body, name=name, grid=(m // tm,), in_specs=in_specs, out_specs=out_spec,
        out_shape=jax.ShapeDtypeStruct(out_shape, F32), compiler_params=_params(("arbitrary",)),
    )(x, y)


def _mm_fwd(a, b, name):
    m, _ = a.shape
    g, k, n = b.shape
    if g > 1:
        return _grouped_rows('fwd', a, b, g, k, n, name)
    tm, tn = _row_tile(m, 832), n
    nj = n // tn

    def body(a_ref, b_ref, o_ref):
        o_ref[...] = jnp.dot(a_ref[...].astype(BF16), b_ref[0].astype(BF16), preferred_element_type=F32)

    return pl.pallas_call(
        body, name=name, grid=(g, m // tm, nj),
        in_specs=[pl.BlockSpec((tm, k), lambda gi, i, j: (i, gi)),
                  pl.BlockSpec((1, k, tn), lambda gi, i, j: (gi, 0, j))],
        out_specs=pl.BlockSpec((tm, tn), lambda gi, i, j: (i, gi * nj + j)),
        out_shape=jax.ShapeDtypeStruct((m, g * n), F32),
        compiler_params=_params(("arbitrary", "arbitrary", "arbitrary")),
    )(a, b)


def _mm_da(ct, b, name):
    m, _ = ct.shape
    g, k, n = b.shape
    if g > 1:
        return _grouped_rows('da', ct, b, g, k, n, name)
    tm, tk = _row_tile(m, 832 if n <= 1536 else 416), _col_tile(k, 1024)
    nk = k // tk

    def body(c_ref, b_ref, o_ref):
        o_ref[...] = lax.dot_general(c_ref[...].astype(BF16), b_ref[0].astype(BF16), (((1,), (1,)), ((), ())),
                                     preferred_element_type=F32)

    return pl.pallas_call(
        body, name=name, grid=(g, m // tm, nk),
        in_specs=[pl.BlockSpec((tm, n), lambda gi, i, j: (i, gi)),
                  pl.BlockSpec((1, tk, n), lambda gi, i, j: (gi, j, 0))],
        out_specs=pl.BlockSpec((tm, tk), lambda gi, i, j: (i, gi * nk + j)),
        out_shape=jax.ShapeDtypeStruct((m, g * k), F32),
        compiler_params=_params(("arbitrary", "arbitrary", "arbitrary")),
    )(ct, b)


def _mm_db(a, ct, g, k, n, name):
    m = a.shape[0]
    if g > 1:
        return _grouped_rows('db', a, ct, g, k, n, name)
    tm, tk, tn = _row_tile(m, 832 if n <= 1536 else 416), _col_tile(k, 1024), n
    nk, nn = k // tk, n // tn

    def body(a_ref, c_ref, o_ref):
        @pl.when(pl.program_id(3) == 0)
        def _():
            o_ref[...] = jnp.zeros_like(o_ref)

        o_ref[0] += lax.dot_general(a_ref[...].astype(BF16), c_ref[...].astype(BF16), (((0,), (0,)), ((), ())),
                                    preferred_element_type=F32)

    return pl.pallas_call(
        body, name=name, grid=(g, nk, nn, m // tm),
        in_specs=[pl.BlockSpec((tm, tk), lambda gi, i, j, r: (r, gi * nk + i)),
                  pl.BlockSpec((tm, tn), lambda gi, i, j, r: (r, gi * nn + j))],
        out_specs=pl.BlockSpec((1, tk, tn), lambda gi, i, j, r: (gi, i, j)),
        out_shape=jax.ShapeDtypeStruct((g, k, n), F32),
        compiler_params=_params(("arbitrary", "arbitrary", "arbitrary", "arbitrary")),
    )(a, ct)


def _make_gmm(name):
    @jax.custom_vjp
    def gmm(a, b):
        return _mm_fwd(a, b, name + "_fwd")

    def fwd(a, b):
        return _mm_fwd(a, b, name + "_fwd"), (a, b)

    def bwd(res, ct):
        a, b = res
        g, k, n = b.shape
        return _mm_da(ct, b, name + "_da"), _mm_db(a, ct, g, k, n, name + "_db")

    gmm.defvjp(fwd, bwd)
    return gmm


def _mm_da_sum(cts, ws, name):
    m, k = cts[0].shape[0], ws[0].shape[0]
    tm = _row_tile(m, 208)
    n_p = len(cts)

    def body(*refs):
        acc = None
        for c_ref, w_ref in zip(refs[:n_p], refs[n_p:2 * n_p]):
            term = lax.dot_general(c_ref[...].astype(BF16), w_ref[...].astype(BF16), (((1,), (1,)), ((), ())),
                                   preferred_element_type=F32)
            acc = term if acc is None else acc + term
        refs[2 * n_p][...] = acc

    return pl.pallas_call(
        body, name=name, grid=(m // tm,),
        in_specs=[pl.BlockSpec((tm, c.shape[1]), lambda i: (i, 0)) for c in cts]
        + [pl.BlockSpec(w.shape, lambda i: (0, 0), pipeline_mode=pl.Buffered(1)) for w in ws],
        out_specs=pl.BlockSpec((tm, k), lambda i: (i, 0)),
        out_shape=jax.ShapeDtypeStruct((m, k), F32),
        compiler_params=_params(("arbitrary",)),
    )(*cts, *ws)


def _in_proj(names, a, a16, ws, carriers):
    def products(a16, ws):
        return tuple(_mm_fwd(a16, w[None], nm + "_fwd") for nm, w in zip(names, ws))

    @jax.custom_vjp
    def proj(a, a16, ws, carriers):
        return products(a16, ws)

    def fwd(a, a16, ws, carriers):
        return products(a16, ws), (a16, ws)

    def bwd(res, cts):
        a16, ws = res
        d_ws = tuple(_mm_db(a16, ct, 1, w.shape[0], w.shape[1], nm + "_db")[0] for nm, w, ct in zip(names, ws, cts))
        return (_mm_da_sum(list(cts), list(ws), "in_da"), jnp.zeros_like(a16),
                tuple(jnp.zeros_like(w) for w in ws), d_ws)

    proj.defvjp(fwd, bwd)
    return proj(a, a16, tuple(ws), tuple(carriers))


@jax.custom_vjp
def _wdot(x, w, carrier):
    return _dot(x, w)


def _wdot_fwd(x, w, carrier):
    return _dot(x, w), (x, w)


def _wdot_bwd(res, ct):
    x, w = res
    return _dot_nt(ct, w), jnp.zeros_like(w), _dot_tn(x, ct)


_wdot.defvjp(_wdot_fwd, _wdot_bwd)


def _make_rowwise(name, fn, n_row, n_par, out_widths, tm_cap, use_ridx=False, n_wt=0, out_dtypes=None):
    n_out = len(out_widths)
    out_dtypes = out_dtypes or [F32] * n_out
    n_in = n_row + n_par + n_wt

    def bind(tm):
        if not use_ridx:
            return fn
        ridx = pl.program_id(0) * tm + lax.broadcasted_iota(jnp.int32, (tm, 1), 0)
        return functools.partial(fn, ridx)

    def specs(args, tm):
        rows = [pl.BlockSpec((tm, a.shape[1]), lambda i: (i, 0)) for a in args[:n_row]]
        pars = [pl.BlockSpec(a.shape, lambda i: (0, 0)) for a in args[n_row:n_in]]
        return rows, pars

    def fwd_call(*args):
        t = args[0].shape[0]
        tm = _row_tile(t, tm_cap)
        rows, pars = specs(args, tm)

        def body(*refs):
            vals = [r[...] for r in refs[:n_row + n_par]]
            mats = [functools.partial(lambda x, w: _dot(x, w), w=r[...]) for r in refs[n_row + n_par:n_in]]
            res = bind(tm)(*vals, *mats)
            for o_ref, r in zip(refs[n_in:], res):
                o_ref[...] = r

        outs = pl.pallas_call(
            body, name=name + "_fwd", grid=(t // tm,), in_specs=rows + pars,
            out_specs=[pl.BlockSpec((tm, w), lambda i: (i, 0)) for w in out_widths],
            out_shape=[jax.ShapeDtypeStruct((t, w), dt) for w, dt in zip(out_widths, out_dtypes)],
            compiler_params=_params(("arbitrary",)),
        )(*args)
        return tuple(outs)

    def bwd_call(args, cts):
        t = args[0].shape[0]
        tm = _row_tile(t, tm_cap)
        rows, pars = specs(args, tm)
        ct_specs = [pl.BlockSpec((tm, w), lambda i: (i, 0)) for w in out_widths]
        n_diff = n_row + n_par

        def body(*refs):
            vals = [r[...] for r in refs[:n_diff]]
            wts = [r[...] for r in refs[n_diff:n_in]]
            ct_vals = tuple(r[...] for r in refs[n_in:n_in + n_out])
            d_refs = refs[n_in + n_out:]
            f = bind(tm)

            def g(*a):
                mats = [functools.partial(lambda x, w, c: _wdot(x, w, c), w=w, c=c)
                        for w, c in zip(wts, a[n_diff:])]
                return tuple(f(*a[:n_diff], *mats))

            _, vjp = jax.vjp(g, *vals, *[jnp.zeros(w.shape, F32) for w in wts])
            grads = vjp(ct_vals)
            for i in range(n_row):
                d_refs[i][...] = grads[i]
            if n_par + n_wt:
                @pl.when(pl.program_id(0) == 0)
                def _():
                    for r in d_refs[n_row:]:
                        r[...] = jnp.zeros_like(r)

                for r, gr in zip(d_refs[n_row:], grads[n_row:]):
                    r[...] += gr

        outs = pl.pallas_call(
            body, name=name + "_bwd", grid=(t // tm,), in_specs=rows + pars + ct_specs,
            out_specs=rows + pars,
            out_shape=[jax.ShapeDtypeStruct(a.shape, F32) for a in args],
            compiler_params=_params(("arbitrary",)),
        )(*args, *cts)
        return tuple(outs)

    @jax.custom_vjp
    def op(*args):
        return fwd_call(*args[:n_in])

    def fwd(*args):
        return fwd_call(*args[:n_in]), args[:n_in]

    def bwd(args, cts):
        grads = bwd_call(args, cts)
        return grads[:n_row + n_par] + tuple(jnp.zeros_like(a) for a in args[n_row + n_par:]) + grads[n_row + n_par:]

    op.defvjp(fwd, bwd)
    return op


HALO = SUBLANES


def _make_conv_rowwise(name, fn, n_par, out_width, tm_cap, use_ridx=False):
    def bind(ridx):
        return functools.partial(fn, ridx) if use_ridx else fn

    def stage(x_ref, halo_ref, xs, first):
        xs[0:HALO, :] = jnp.where(first, 0.0, halo_ref[...])
        xs[HALO:, :] = x_ref[...]

    def taps(xs, tm):
        return [xs[pl.ds(HALO - (CONV_K - 1) + j, tm), :] for j in range(CONV_K)]

    def fwd_call(x, *pars):
        t, wd = x.shape
        tm = _row_tile(t, tm_cap)
        per = tm // HALO

        def body(*refs):
            x_ref, halo_ref = refs[:2]
            par_refs, o_ref, xs = refs[2:2 + n_par], refs[2 + n_par], refs[-1]
            i = pl.program_id(0)
            stage(x_ref, halo_ref, xs, i == 0)
            ridx = i * tm + lax.broadcasted_iota(jnp.int32, (tm, 1), 0)
            (o_ref[...],) = bind(ridx)(*taps(xs, tm), *[r[...] for r in par_refs])

        return pl.pallas_call(
            body, name=name + "_fwd", grid=(t // tm,),
            in_specs=[pl.BlockSpec((tm, wd), lambda i: (i, 0)),
                      pl.BlockSpec((HALO, wd), lambda i: (jnp.maximum(i * per - 1, 0), 0))]
            + [pl.BlockSpec(p.shape, lambda i: (0, 0)) for p in pars],
            out_specs=pl.BlockSpec((tm, out_width), lambda i: (i, 0)),
            out_shape=jax.ShapeDtypeStruct((t, out_width), F32),
            scratch_shapes=[pltpu.VMEM((tm + HALO, wd), F32)],
            compiler_params=_params(("arbitrary",)),
        )(x, x, *pars)

    def bwd_call(x, pars, ct):
        t, wd = x.shape
        tm = _row_tile(t, tm_cap)
        per = tm // HALO
        nb = t // tm

        def body(*refs):
            x_ref, halo_ref = refs[:2]
            par_refs, ct_ref = refs[2:2 + n_par], refs[2 + n_par]
            dx_ref, dpar_refs = refs[3 + n_par], refs[4 + n_par:4 + 2 * n_par]
            xs, gs, carry = refs[-3:]
            step = pl.program_id(0)
            blk = nb - 1 - step

            @pl.when(step == 0)
            def _():
                carry[...] = jnp.zeros_like(carry)
                for r in dpar_refs:
                    r[...] = jnp.zeros_like(r)

            stage(x_ref, halo_ref, xs, blk == 0)
            ridx = blk * tm + lax.broadcasted_iota(jnp.int32, (tm, 1), 0)
            f = bind(ridx)
            _, vjp = jax.vjp(lambda *a: tuple(f(*a)), *taps(xs, tm), *[r[...] for r in par_refs])
            grads = vjp((ct_ref[...],))
            dx = grads[CONV_K - 1]
            for j in range(CONV_K - 1):
                gs[j, 0:tm, :] = grads[j]
                gs[j, tm:, :] = carry[j]
                dx = dx + gs[j, pl.ds(CONV_K - 1 - j, tm), :]
                carry[j] = grads[j][0:HALO, :]
            dx_ref[...] = dx
            for r, g in zip(dpar_refs, grads[CONV_K:]):
                r[...] += g

        rev = lambda i: (nb - 1 - i, 0)
        outs = pl.pallas_call(
            body, name=name + "_bwd", grid=(nb,),
            in_specs=[pl.BlockSpec((tm, wd), rev),
                      pl.BlockSpec((HALO, wd), lambda i: (jnp.maximum((nb - 1 - i) * per - 1, 0), 0))]
            + [pl.BlockSpec(p.shape, lambda i: (0, 0)) for p in pars]
            + [pl.BlockSpec((tm, out_width), rev)],
            out_specs=[pl.BlockSpec((tm, wd), rev)] + [pl.BlockSpec(p.shape, lambda i: (0, 0)) for p in pars],
            out_shape=[jax.ShapeDtypeStruct(x.shape, F32)] + [jax.ShapeDtypeStruct(p.shape, F32) for p in pars],
            scratch_shapes=[pltpu.VMEM((tm + HALO, wd), F32), pltpu.VMEM((CONV_K - 1, tm + HALO, wd), F32),
                            pltpu.VMEM((CONV_K - 1, HALO, wd), F32)],
            compiler_params=_params(("arbitrary",)),
        )(x, x, *pars, ct)
        return tuple(outs)

    @jax.custom_vjp
    def op(x, *pars):
        return fwd_call(x, *pars)

    def fwd(x, *pars):
        return fwd_call(x, *pars), (x, pars)

    def bwd(res, ct):
        x, pars = res
        return bwd_call(x, pars, ct)

    op.defvjp(fwd, bwd)
    return op


STEP_CHUNKS = 5


def _make_chunk_scan(name, chunk_fn, state_shape, n_seq, n_par, out_widths):
    n_out = len(out_widths)
    zeros_idx = (0,) * len(state_shape)

    def plan(t):
        per_step = _any_tile(t // CHUNK, STEP_CHUNKS)
        return per_step, per_step * CHUNK, t // (per_step * CHUNK)

    def fn(state, *args):
        per_step = args[0].shape[0] // CHUNK
        outs = []
        for c in range(per_step):
            rows = slice(c * CHUNK, (c + 1) * CHUNK)
            res = chunk_fn(state, *[a[rows, :] for a in args[:n_seq]], *args[n_seq:])
            state = res[0]
            outs.append(res[1:])
        return (state,) + tuple(jnp.concatenate([o[i] for o in outs], axis=0) for i in range(n_out))

    def fwd_call(*args):
        t = args[0].shape[0]
        _, rows_per_step, nc = plan(t)
        seq_specs = [pl.BlockSpec((rows_per_step, a.shape[1]), lambda c: (c, 0)) for a in args[:n_seq]]
        par_specs = [pl.BlockSpec(a.shape, lambda c: (0, 0)) for a in args[n_seq:]]

        def body(*refs):
            ins = refs[:n_seq + n_par]
            out_refs = refs[n_seq + n_par:n_seq + n_par + n_out]
            states_ref = refs[n_seq + n_par + n_out]
            st = refs[-1]

            @pl.when(pl.program_id(0) == 0)
            def _():
                st[...] = jnp.zeros_like(st)

            s0 = st[...]
            states_ref[0] = s0
            res = fn(s0, *[r[...] for r in ins])
            st[...] = res[0]
            for o_ref, r in zip(out_refs, res[1:]):
                o_ref[...] = r

        outs = pl.pallas_call(
            body, name=name + "_fwd", grid=(nc,), in_specs=seq_specs + par_specs,
            out_specs=[pl.BlockSpec((rows_per_step, w), lambda c: (c, 0)) for w in out_widths]
            + [pl.BlockSpec((1,) + state_shape, lambda c: (c,) + zeros_idx)],
            out_shape=[jax.ShapeDtypeStruct((t, w), F32) for w in out_widths]
            + [jax.ShapeDtypeStruct((nc,) + state_shape, F32)],
            scratch_shapes=[pltpu.VMEM(state_shape, F32)],
            compiler_params=_params(("arbitrary",)),
        )(*args)
        return tuple(outs[:n_out]), outs[n_out]

    def bwd_call(args, states, cts):
        t = args[0].shape[0]
        _, rows_per_step, nc = plan(t)
        rev = lambda c: (nc - 1 - c, 0)
        seq_specs = [pl.BlockSpec((rows_per_step, a.shape[1]), rev) for a in args[:n_seq]]
        par_specs = [pl.BlockSpec(a.shape, lambda c: (0, 0)) for a in args[n_seq:]]
        ct_specs = [pl.BlockSpec((rows_per_step, w), rev) for w in out_widths]
        st_spec = pl.BlockSpec((1,) + state_shape, lambda c: (nc - 1 - c,) + zeros_idx)
        n_in = n_seq + n_par

        def body(*refs):
            vals = [r[...] for r in refs[:n_in]]
            s0 = refs[n_in][0]
            ct_vals = tuple(r[...] for r in refs[n_in + 1:n_in + 1 + n_out])
            d_refs = refs[n_in + 1 + n_out:-1]
            dst = refs[-1]

            @pl.when(pl.program_id(0) == 0)
            def _():
                dst[...] = jnp.zeros_like(dst)
                for j in range(n_par):
                    d_refs[n_seq + j][...] = jnp.zeros_like(d_refs[n_seq + j])

            _, vjp = jax.vjp(lambda *a: tuple(fn(*a)), s0, *vals)
            grads = vjp((dst[...],) + ct_vals)
            dst[...] = grads[0]
            for i in range(n_seq):
                d_refs[i][...] = grads[1 + i]
            for j in range(n_par):
                d_refs[n_seq + j][...] += grads[1 + n_seq + j]

        outs = pl.pallas_call(
            body, name=name + "_bwd", grid=(nc,), in_specs=seq_specs + par_specs + [st_spec] + ct_specs,
            out_specs=seq_specs + par_specs,
            out_shape=[jax.ShapeDtypeStruct(a.shape, F32) for a in args],
            scratch_shapes=[pltpu.VMEM(state_shape, F32)],
            compiler_params=_params(("arbitrary",)),
        )(*args, states, *cts)
        return tuple(outs)

    @jax.custom_vjp
    def op(*args):
        return fwd_call(*args)[0]

    def fwd(*args):
        outs, states = fwd_call(*args)
        return outs, (args, states)

    def bwd(res, cts):
        args, states = res
        return bwd_call(args, states, cts)

    op.defvjp(fwd, bwd)
    return op


def _s5_scan_fwd(bre, bim, are, aim, name):
    t, r, _ = bre.shape
    tb = _any_tile(t, 208)
    blk = pl.BlockSpec((tb, r, LANES), lambda i: (i, 0, 0))
    par = pl.BlockSpec((r, LANES), lambda i: (0, 0))

    def body(bre_ref, bim_ref, are_ref, aim_ref, sre_ref, sim_ref, st):
        @pl.when(pl.program_id(0) == 0)
        def _():
            st[...] = jnp.zeros_like(st)

        ar, ai = are_ref[...], aim_ref[...]

        def step(k, carry):
            sr, si = carry
            nr = ar * sr - ai * si + bre_ref[k]
            ni = ar * si + ai * sr + bim_ref[k]
            sre_ref[k] = nr
            sim_ref[k] = ni
            return nr, ni

        sr, si = lax.fori_loop(0, tb, step, (st[0], st[1]), unroll=4)
        st[0] = sr
        st[1] = si

    return pl.pallas_call(
        body, name=name, grid=(t // tb,), in_specs=[blk, blk, par, par], out_specs=[blk, blk],
        out_shape=[jax.ShapeDtypeStruct(bre.shape, F32)] * 2,
        scratch_shapes=[pltpu.VMEM((2, r, LANES), F32)],
        compiler_params=_params(("arbitrary",)),
    )(bre, bim, are, aim)


def _s5_scan_bwd(dsr, dsi, sre, sim, are, aim, name):
    t, r, _ = sre.shape
    tb = _any_tile(t, 208)
    nb = t // tb
    blk = pl.BlockSpec((tb, r, LANES), lambda i: (nb - 1 - i, 0, 0))
    par = pl.BlockSpec((r, LANES), lambda i: (0, 0))

    def body(dsr_ref, dsi_ref, sre_ref, sim_ref, are_ref, aim_ref, gre_ref, gim_ref, dar_ref, dai_ref, st):
        @pl.when(pl.program_id(0) == 0)
        def _():
            st[...] = jnp.zeros_like(st)
            dar_ref[...] = jnp.zeros_like(dar_ref)
            dai_ref[...] = jnp.zeros_like(dai_ref)

        ar, ai = are_ref[...], aim_ref[...]

        def step(k, carry):
            gr, gi, dar, dai = carry
            q = tb - 1 - k
            s_r, s_i = sre_ref[q], sim_ref[q]
            dar = dar + gr * s_r + gi * s_i
            dai = dai + gi * s_r - gr * s_i
            ngr = dsr_ref[q] + ar * gr + ai * gi
            ngi = dsi_ref[q] + ar * gi - ai * gr
            gre_ref[q] = ngr
            gim_ref[q] = ngi
            return ngr, ngi, dar, dai

        gr, gi, dar, dai = lax.fori_loop(0, tb, step, (st[0], st[1], dar_ref[...], dai_ref[...]), unroll=4)
        st[0] = gr
        st[1] = gi
        dar_ref[...] = dar
        dai_ref[...] = dai

    return pl.pallas_call(
        body, name=name, grid=(nb,), in_specs=[blk, blk, blk, blk, par, par], out_specs=[blk, blk, par, par],
        out_shape=[jax.ShapeDtypeStruct(sre.shape, F32)] * 2 + [jax.ShapeDtypeStruct(are.shape, F32)] * 2,
        scratch_shapes=[pltpu.VMEM((2, r, LANES), F32)],
        compiler_params=_params(("arbitrary",)),
    )(dsr, dsi, sre, sim, are, aim)


def _make_s5_scan(name):
    @jax.custom_vjp
    def scan(bre, bim, are, aim):
        return tuple(_s5_scan_fwd(bre, bim, are, aim, name + "_fwd"))

    def fwd(bre, bim, are, aim):
        sre, sim = _s5_scan_fwd(bre, bim, are, aim, name + "_fwd")
        return (sre, sim), (sre, sim, are, aim)

    def bwd(res, cts):
        sre, sim, are, aim = res
        return tuple(_s5_scan_bwd(cts[0], cts[1], sre, sim, are, aim, name + "_bwd"))

    scan.defvjp(fwd, bwd)
    return scan


def _dot(a, b):
    return jnp.dot(a.astype(BF16), b.astype(BF16), preferred_element_type=F32)


def _dot_nt(a, b):
    return lax.dot_general(a.astype(BF16), b.astype(BF16), (((1,), (1,)), ((), ())), preferred_element_type=F32)


def _dot_tn(a, b):
    return lax.dot_general(a.astype(BF16), b.astype(BF16), (((0,), (0,)), ((), ())), preferred_element_type=F32)


def _split3(x):
    x1 = x.astype(BF16)
    rest = x - x1.astype(F32)
    x2 = rest.astype(BF16)
    return x1, x2, (rest - x2.astype(F32)).astype(BF16)


def _sel_dot(dims, a, b, a_is_sel):
    sel = (a if a_is_sel else b).astype(BF16)
    acc = None
    for piece in _split3(b if a_is_sel else a):
        pair = (sel, piece) if a_is_sel else (piece, sel)
        term = lax.dot_general(*pair, (dims, ((), ())), preferred_element_type=F32)
        acc = term if acc is None else acc + term
    return acc


@jax.custom_vjp
def _running_sum(low, y):
    return _sel_dot(((1,), (0,)), low, y, True)


_running_sum.defvjp(lambda low, y: (_running_sum(low, y), low),
                    lambda low, ct: (jnp.zeros_like(low), _sel_dot(((0,), (0,)), low, ct, True)))


@jax.custom_vjp
def _spread(x, sel):
    return _sel_dot(((1,), (0,)), x, sel, False)


_spread.defvjp(lambda x, sel: (_spread(x, sel), sel),
               lambda sel, ct: (_sel_dot(((1,), (1,)), ct, sel, False), jnp.zeros_like(sel)))


@jax.custom_vjp
def _transposed(x, eye):
    return _sel_dot(((0,), (0,)), x, eye, False)


_transposed.defvjp(lambda x, eye: (_transposed(x, eye), eye),
                   lambda eye, ct: (_sel_dot(((1,), (1,)), eye, ct, True), jnp.zeros_like(eye)))


def _iota(shape, dim):
    return lax.broadcasted_iota(jnp.int32, shape, dim)


def _tri(strict=False):
    r, c = _iota((CHUNK, CHUNK), 0), _iota((CHUNK, CHUNK), 1)
    return (r > c) if strict else (r >= c)


def _silu(x):
    return x * jax.nn.sigmoid(x)


def _layer_norm(z, g, b):
    mu = jnp.mean(z, axis=-1, keepdims=True)
    var = jnp.mean(jnp.square(z - mu), axis=-1, keepdims=True)
    return (z - mu) * lax.rsqrt(var + LN_EPS) * g + b


def _ssd_chunk(state, xbc, sm, a_c, d_exp):
    width = SSD_HEADS * SSD_HEAD
    x = xbc[:, :width]
    lane = _iota((CHUNK, LANES), 1)
    dtc = jnp.where(lane < SSD_HEADS, sm, 0.0)
    low = _tri().astype(F32)
    eye = (_iota((CHUNK, CHUNK), 0) == _iota((CHUNK, CHUNK), 1)).astype(F32)
    head_col, head_row = _iota((LANES, width), 1), _iota((LANES, width), 0) * SSD_HEAD
    expand = ((head_col >= head_row) & (head_col < head_row + SSD_HEAD)).astype(F32)
    acum_c = _running_sum(low, dtc * a_c)
    acum_ct = _transposed(acum_c, eye)
    dt_exp = _spread(dtc, expand)
    acum = _spread(acum_c, expand)
    xd = x * dt_exp
    last = acum[CHUNK - 1:CHUNK, :]
    to_end = jnp.exp(last - acum)
    eac = jnp.exp(acum)
    causal = _tri()
    first_half = _iota((CHUNK, LANES), 1) < SSD_HEAD
    top_rows = _iota((LANES, LANES), 0) < SSD_HEAD
    pairs = range(SSD_HEADS // 2)
    grp = [(2 * p) // (SSD_HEADS // SSD_GROUPS) for p in pairs]
    cols = [slice(p * LANES, (p + 1) * LANES) for p in pairs]
    bg = [xbc[:, width + g * SSD_STATE: width + (g + 1) * SSD_STATE] for g in range(SSD_GROUPS)]
    cg = [xbc[:, width + (SSD_GROUPS + g) * SSD_STATE: width + (SSD_GROUPS + g + 1) * SSD_STATE]
          for g in range(SSD_GROUPS)]
    scores = [_dot_nt(cg[g], bg[g]) for g in range(SSD_GROUPS)]
    dec = [jnp.where(causal, jnp.exp(jnp.minimum(acum_c[:, h:h + 1] - acum_ct[h:h + 1, :], 0.0)), 0.0)
           for h in range(SSD_HEADS)]
    y_lo = [_dot(scores[grp[p]] * dec[2 * p], jnp.where(first_half, xd[:, cols[p]], 0.0)) for p in pairs]
    y_hi = [_dot(scores[grp[p]] * dec[2 * p + 1], jnp.where(first_half, 0.0, xd[:, cols[p]])) for p in pairs]
    s_prev = [state[p * LANES:(p + 1) * LANES, :] for p in pairs]
    y_off = [_dot_nt(cg[grp[p]], s_prev[p]) for p in pairs]
    s_add = [_dot_tn(xd[:, cols[p]] * to_end[:, cols[p]], bg[grp[p]]) for p in pairs]
    ys = [y_lo[p] + y_hi[p] + y_off[p] * eac[:, cols[p]] + x[:, cols[p]] * d_exp[:, cols[p]] for p in pairs]
    cd = [jnp.where(top_rows, jnp.exp(acum_c[CHUNK - 1:CHUNK, 2 * p:2 * p + 1]),
                    jnp.exp(acum_c[CHUNK - 1:CHUNK, 2 * p + 1:2 * p + 2])) for p in pairs]
    new_states = [s_prev[p] * cd[p] + s_add[p] for p in pairs]
    return jnp.concatenate(new_states, axis=0), jnp.concatenate(ys, axis=1)


def _neumann(a_mats, rhs, transposed):
    dot = _dot_tn if transposed else _dot
    nmats, sols = [-a for a in a_mats], list(rhs)
    for i in range(6):
        upd = [dot(n, s) for n, s in zip(nmats, sols)]
        if i < 5:
            nmats = [_dot(n, n) for n in nmats]
        sols = [s + u for s, u in zip(sols, upd)]
    return tuple(sols)


@jax.custom_vjp
def _solve_unit_lower(a_mats, rhs):
    return _neumann(a_mats, rhs, False)


def _solve_unit_lower_fwd(a_mats, rhs):
    sols = _neumann(a_mats, rhs, False)
    return sols, (a_mats, sols)


def _solve_unit_lower_bwd(res, d_sols):
    a_mats, sols = res
    d_rhs = _neumann(a_mats, d_sols, True)
    return tuple(-_dot_nt(dr, x) for dr, x in zip(d_rhs, sols)), d_rhs


_solve_unit_lower.defvjp(_solve_unit_lower_fwd, _solve_unit_lower_bwd)


def _gdn_chunk(state, qkv, sm):
    width = GDN_HEADS * GDN_HEAD
    g0, b0 = SSD_HEADS, SSD_HEADS + GDN_HEADS
    lane = _iota((CHUNK, LANES), 1)
    gc = jnp.where((lane >= g0) & (lane < b0), sm, 0.0)
    low = _tri().astype(F32)
    eye = (_iota((CHUNK, CHUNK), 0) == _iota((CHUNK, CHUNK), 1)).astype(F32)
    gcum = _running_sum(low, gc)
    gcum_t = _transposed(gcum, eye)
    causal, strict = _tri(), _tri(strict=True)
    heads = range(GDN_HEADS)
    q = [qkv[:, h * GDN_HEAD:(h + 1) * GDN_HEAD] for h in heads]
    k = [qkv[:, width + h * GDN_HEAD: width + (h + 1) * GDN_HEAD] for h in heads]
    v = [qkv[:, 2 * width + h * GDN_HEAD: 2 * width + (h + 1) * GDN_HEAD] for h in heads]
    beta = [sm[:, b0 + h:b0 + h + 1] for h in heads]
    gcol = [gcum[:, g0 + h:g0 + h + 1] for h in heads]
    glast = [gcum[CHUNK - 1:CHUNK, g0 + h:g0 + h + 1] for h in heads]
    gamma = [jnp.where(causal, jnp.exp(jnp.minimum(gcol[h] - gcum_t[g0 + h:g0 + h + 1, :], 0.0)), 0.0) for h in heads]
    kk = [_dot_nt(k[h], k[h]) for h in heads]
    qk = [_dot_nt(q[h], k[h]) for h in heads]
    egc = [jnp.exp(gcol[h]) for h in heads]
    a_mat = tuple(jnp.where(strict, kk[h] * gamma[h] * beta[h], 0.0) for h in heads)
    sol = _solve_unit_lower(a_mat, tuple(jnp.concatenate([v[h] * beta[h], k[h] * (beta[h] * egc[h])], axis=1)
                                         for h in heads))
    s_prev = [state[h * GDN_HEAD:(h + 1) * GDN_HEAD, :] for h in heads]
    w_s = [_dot(sol[h][:, GDN_HEAD:], s_prev[h]) for h in heads]
    q_s = [_dot(q[h] * egc[h], s_prev[h]) for h in heads]
    v_new = [sol[h][:, :GDN_HEAD] - w_s[h] for h in heads]
    a_v = [_dot(qk[h] * gamma[h], v_new[h]) for h in heads]
    k_v = [_dot_tn(k[h] * jnp.exp(glast[h] - gcol[h]), v_new[h]) for h in heads]
    outs = [q_s[h] + a_v[h] for h in heads]
    new_states = [s_prev[h] * jnp.exp(glast[h]) + k_v[h] for h in heads]
    return jnp.concatenate(new_states, axis=0), jnp.concatenate(outs, axis=1)


def _row_fns(d_model, pad_rows, loss_rows, alpha):
    def keep(ridx, v):
        return jnp.where(ridx >= pad_rows, v, 0.0)

    def both(h):
        return h, h.astype(BF16)

    def ln_in(ridx, h, g, b):
        return both(keep(ridx, _layer_norm(h, g, b)))

    def s5_tail(y_re, y_im, u, z, d, bias, glu, branch):
        v0 = jax.nn.gelu(y_re + y_im + d * u)
        return (branch(v0 * jax.nn.sigmoid(glu(v0) + bias) * _silu(z)),)

    def small_act(ridx, raw, bias, scale):
        lane = _iota(raw.shape, 1)
        sp = jax.nn.softplus(raw + bias)
        g0, b0 = SSD_HEADS, SSD_HEADS + GDN_HEADS
        out = jnp.where(lane < g0, sp, jnp.where(lane < b0, scale * sp,
                                                 jnp.where(lane < b0 + GDN_HEADS, jax.nn.sigmoid(raw), 0.0)))
        return (keep(ridx, out),)

    def conv(xs, w):
        acc = xs[0] * w[0:1, :]
        for j in range(1, CONV_K):
            acc = acc + xs[j] * w[j:j + 1, :]
        return acc

    def ssd_conv(ridx, x0, x1, x2, x3, w, b):
        return (keep(ridx, _silu(conv((x0, x1, x2, x3), w) + b)),)

    def ssd_tail(y, z, g, branch):
        v = y * _silu(z)
        return (branch(v * lax.rsqrt(jnp.mean(v * v, axis=-1, keepdims=True) + LN_EPS) * g),)

    def gdn_conv(x0, x1, x2, x3, w):
        a = _silu(conv((x0, x1, x2, x3), w))
        width = GDN_HEADS * GDN_HEAD
        parts = []
        for h in range(2 * GDN_HEADS):
            z = a[:, h * GDN_HEAD:(h + 1) * GDN_HEAD]
            z = z * lax.rsqrt(jnp.sum(z * z, axis=-1, keepdims=True) + 1e-6)
            parts.append(z * GDN_HEAD ** -0.5 if h < GDN_HEADS else z)
        parts.append(a[:, 2 * width:])
        return (jnp.concatenate(parts, axis=1),)

    def gdn_tail(o, z, g, branch):
        parts = []
        for h in range(GDN_HEADS):
            cols = slice(h * GDN_HEAD, (h + 1) * GDN_HEAD)
            oh = o[:, cols]
            oh = oh * lax.rsqrt(jnp.mean(oh * oh, axis=-1, keepdims=True) + LN_EPS) * g
            parts.append(oh * _silu(z[:, cols]))
        return (branch(jnp.concatenate(parts, axis=1)),)

    def merge_out(ridx, oa, ob, oc, gate, h, bias, g, b, w_out):
        acc = None
        for k, o in enumerate((oa, ob, oc)):
            cols = slice(k * d_model, (k + 1) * d_model)
            term = jax.nn.sigmoid(gate[:, cols] + bias[:, cols]) * o
            acc = term if acc is None else acc + term
        return both(keep(ridx, _layer_norm(alpha * h + w_out(acc), g, b)))

    def loss_rows_fn(ridx, h, tgt):
        row = 0.5 * jnp.mean(jnp.square(h - tgt), axis=-1, keepdims=True)
        row = jnp.where(ridx >= loss_rows, row, 0.0)
        lane = _iota((h.shape[0], LANES), 1)
        return (jnp.where(lane == 0, row, 0.0),)

    return dict(ln_in=ln_in, s5_tail=s5_tail, small_act=small_act, ssd_conv=ssd_conv, ssd_tail=ssd_tail,
                gdn_conv=gdn_conv, gdn_tail=gdn_tail, merge_out=merge_out, loss=loss_rows_fn)


def _pair_exchange(srcs, whole_srcs, name):
    n_arr, n_whole = len(srcs), len(whole_srcs)
    n_chip = N_DEV // 2
    base = n_arr * n_chip

    def body(*refs):
        src_refs, recv_refs = refs[:n_arr + n_whole], refs[n_arr + n_whole:2 * (n_arr + n_whole)]
        send_sems, recv_sems = refs[2 * (n_arr + n_whole):]
        x, y, c = lax.axis_index("x"), lax.axis_index("y"), lax.axis_index("c")

        def to_sibling(src, dst, k):
            return pltpu.make_async_remote_copy(src_ref=src, dst_ref=dst, send_sem=send_sems.at[k],
                                                recv_sem=recv_sems.at[k], device_id=(x, y, 1 - c),
                                                device_id_type=pl.DeviceIdType.MESH)

        copies = [to_sibling(src_refs[a].at[q, 1 - c], recv_refs[a].at[q], a * n_chip + q)
                  for a in range(n_arr) for q in range(n_chip)]
        copies += [to_sibling(src_refs[n_arr + b], recv_refs[n_arr + b], base + b) for b in range(n_whole)]
        for cp in copies:
            cp.start()
        for cp in copies:
            cp.wait()

    n_sem = base + n_whole
    return pl.pallas_call(
        body, name=name,
        in_specs=[pl.BlockSpec(memory_space=pl.ANY)] * (n_arr + n_whole),
        out_specs=[pl.BlockSpec(memory_space=pl.ANY)] * (n_arr + n_whole),
        out_shape=[jax.ShapeDtypeStruct((n_chip,) + tuple(s.shape[2:]), s.dtype) for s in srcs]
        + [jax.ShapeDtypeStruct(s.shape, s.dtype) for s in whole_srcs],
        scratch_shapes=[pltpu.SemaphoreType.DMA((n_sem,)), pltpu.SemaphoreType.DMA((n_sem,))],
    )(*srcs, *whole_srcs)


def _pair_sum(src, recv, core, name):
    n_chip, rows, cols = recv.shape
    tr = _row_tile(rows, 1024 if cols <= LANES else 512)

    def body(core_ref, a_ref, b_ref, o_ref):
        o_ref[...] = (a_ref[0].astype(F32) + b_ref[...].astype(F32)).astype(o_ref.dtype)

    blk = pl.BlockSpec((1, tr, cols), lambda q, i, core_ref: (q, i, 0))
    return pl.pallas_call(
        body, name=name,
        grid_spec=pltpu.PrefetchScalarGridSpec(
            num_scalar_prefetch=1, grid=(n_chip, rows // tr),
            in_specs=[pl.BlockSpec((1, 1, tr, cols), lambda q, i, core_ref: (q, core_ref[0], i, 0)), blk],
            out_specs=blk),
        out_shape=jax.ShapeDtypeStruct(recv.shape, recv.dtype),
        compiler_params=_params(("arbitrary", "arbitrary")),
    )(core, src, recv)


def _add2(a, b, name):
    rows = a.shape[0]
    tr = _row_tile(rows, FLAT_ROWS)
    blk = pl.BlockSpec((tr, LANES), lambda i: (i, 0))

    def body(a_ref, b_ref, o_ref):
        o_ref[...] = a_ref[...] + b_ref[...]

    return pl.pallas_call(body, name=name, grid=(rows // tr,), in_specs=[blk, blk], out_specs=blk,
                          out_shape=jax.ShapeDtypeStruct(a.shape, F32), compiler_params=_params(("arbitrary",)))(a, b)


def _add_parts(parts, name):
    n, rows, _ = parts.shape
    tr = _row_tile(rows, FLAT_ROWS)

    def body(p_ref, o_ref):
        acc = p_ref[0]
        for k in range(1, n):
            acc = acc + p_ref[k]
        o_ref[...] = acc

    return pl.pallas_call(body, name=name, grid=(rows // tr,),
                          in_specs=[pl.BlockSpec((n, tr, LANES), lambda i: (0, i, 0))],
                          out_specs=pl.BlockSpec((tr, LANES), lambda i: (i, 0)),
                          out_shape=jax.ShapeDtypeStruct((rows, LANES), F32),
                          compiler_params=_params(("arbitrary",)))(parts)


def _chip_exchange(chip_srcs, whole_srcs, name):
    n_chip_arr, n_whole = len(chip_srcs), len(whole_srcs)
    n_arr = n_chip_arr + n_whole
    n_chip = N_DEV // 2
    chip_flips = [(1, 0), (0, 1), (1, 1)]

    def body(*refs):
        src_refs, out_refs = refs[:n_arr], refs[n_arr:2 * n_arr]
        send_sems, recv_sems = refs[2 * n_arr:]
        x, y, c = lax.axis_index("x"), lax.axis_index("y"), lax.axis_index("c")
        my_chip = 2 * x + y
        copies = []
        for a in range(n_arr):
            for k, (fx, fy) in enumerate(chip_flips):
                px = 1 - x if fx else x
                py = 1 - y if fy else y
                src = src_refs[a].at[2 * px + py] if a < n_chip_arr else src_refs[a]
                copies.append(pltpu.make_async_remote_copy(
                    src_ref=src, dst_ref=out_refs[a].at[my_chip],
                    send_sem=send_sems.at[a * 3 + k], recv_sem=recv_sems.at[a * 3 + k],
                    device_id=(px, py, c), device_id_type=pl.DeviceIdType.MESH))
        for cp in copies:
            cp.start()
        for cp in copies:
            cp.wait()

    n_sem = n_arr * len(chip_flips)
    return pl.pallas_call(
        body, name=name,
        in_specs=[pl.BlockSpec(memory_space=pl.ANY)] * n_arr, out_specs=[pl.BlockSpec(memory_space=pl.ANY)] * n_arr,
        out_shape=[jax.ShapeDtypeStruct(s.shape, s.dtype) for s in chip_srcs]
        + [jax.ShapeDtypeStruct((n_chip,) + tuple(s.shape), s.dtype) for s in whole_srcs],
        scratch_shapes=[pltpu.SemaphoreType.DMA((n_sem,)), pltpu.SemaphoreType.DMA((n_sem,))],
    )(*chip_srcs, *whole_srcs)


def _gather(srcs, name):
    n_arr = len(srcs)
    n_sem = N_DEV - 1

    def body(*refs):
        src_refs, out_refs = refs[:n_arr], refs[n_arr:2 * n_arr]
        send_sems, recv_sems = refs[2 * n_arr:]
        x, y, c = lax.axis_index("x"), lax.axis_index("y"), lax.axis_index("c")
        me, sibling = (x, y, c), (x, y, 1 - c)
        chips = [(1 - x, y), (x, 1 - y), (1 - x, 1 - y)]

        def slot(a, dev):
            return out_refs[a].at[4 * dev[0] + 2 * dev[1] + dev[2]]

        def copy(a, k, block, to, own=False):
            return pltpu.make_async_remote_copy(
                src_ref=src_refs[a] if own else slot(a, block), dst_ref=slot(a, block),
                send_sem=send_sems.at[a * n_sem + k], recv_sem=recv_sems.at[a * n_sem + k],
                device_id=to, device_id_type=pl.DeviceIdType.MESH)

        arrays = range(n_arr)
        first = [copy(a, 0, me, sibling, own=True) for a in arrays]
        first += [copy(a, 1 + j, me, (*chip, c), own=True) for j, chip in enumerate(chips) for a in arrays]
        for cp in first:
            cp.start()
        passed = []
        for j, chip in enumerate(chips):
            for a in arrays:
                copy(a, 1 + j, (*chip, c), me).wait_recv()
                fwd = copy(a, 4 + j, (*chip, c), sibling)
                fwd.start()
                passed.append(fwd)
        for a in arrays:
            copy(a, 0, sibling, me).wait_recv()
        for j, chip in enumerate(chips):
            for a in arrays:
                copy(a, 4 + j, (*chip, 1 - c), me).wait_recv()
        for cp in first + passed:
            cp.wait_send()

    return pl.pallas_call(
        body, name=name,
        in_specs=[pl.BlockSpec(memory_space=pl.ANY)] * n_arr, out_specs=[pl.BlockSpec(memory_space=pl.ANY)] * n_arr,
        out_shape=[jax.ShapeDtypeStruct((N_DEV,) + tuple(s.shape), s.dtype) for s in srcs],
        scratch_shapes=[pltpu.SemaphoreType.DMA((n_arr * n_sem,)), pltpu.SemaphoreType.DMA((n_arr * n_sem,))],
    )(*srcs)


def _adamw_body(p_ref, w_ref, m_ref, v_ref, g_ref, d_ref, nm_ref, nv_ref):
    bc1 = 1.0 - ADAM_B1 ** ADAM_STEP
    bc2 = 1.0 - ADAM_B2 ** ADAM_STEP
    g = p_ref[0].astype(F32)
    for k in range(1, p_ref.shape[0]):
        g = g + p_ref[k].astype(F32)
    nm = ADAM_B1 * m_ref[...] + (1.0 - ADAM_B1) * g
    nv = ADAM_B2 * v_ref[...] + (1.0 - ADAM_B2) * jnp.square(g)
    m_hat = nm / bc1
    v_hat = nv / bc2
    g_ref[...] = g
    d_ref[...] = -ADAM_LR * (m_hat / (jnp.sqrt(v_hat) + ADAM_EPS) + ADAM_WD * w_ref[...])
    nm_ref[...] = nm
    nv_ref[...] = nv


def _adamw_rows(parts, w, m, v, name):
    rows, cols = w.shape
    tr = _row_tile(rows, max(128, (1 << 18) // cols))
    blk = pl.BlockSpec((tr, cols), lambda i: (i, 0))
    return pl.pallas_call(
        functools.partial(_adamw_body), name=name, grid=(rows // tr,),
        in_specs=[pl.BlockSpec((parts.shape[0], tr, cols), lambda i: (0, i, 0)), blk, blk, blk],
        out_specs=[blk] * 4, out_shape=[jax.ShapeDtypeStruct(w.shape, F32)] * 4,
        compiler_params=_params(("arbitrary",)),
    )(parts, w, m, v)


def _pad_flat(vec, rows):
    return jnp.pad(vec, (0, rows * LANES - vec.shape[0])).reshape(rows, LANES)


def _rows_for(n):
    return -(-n // (FLAT_ROWS * LANES)) * FLAT_ROWS


def _join_shards(rows, local_shape, dim):
    parts = jnp.moveaxis(rows.reshape((N_DEV,) + tuple(local_shape)), 0, dim)
    shp = tuple(local_shape)
    return parts.reshape(shp[:dim] + (N_DEV * shp[dim],) + shp[dim + 1:])


def _s5_tables(a_re, a_im, log_step, b_re, b_im, c_re, c_im):
    lam_re = jnp.minimum(a_re, -1e-4)
    lam_im = a_im
    step = jnp.exp(log_step)[:, None]
    mag = jnp.exp(lam_re * step)
    abar_re, abar_im = mag * jnp.cos(lam_im * step), mag * jnp.sin(lam_im * step)
    den = lam_re * lam_re + lam_im * lam_im
    nr, ni = abar_re - 1.0, abar_im
    coef_re = (nr * lam_re + ni * lam_im) / den
    coef_im = (ni * lam_re - nr * lam_im) / den
    bbar_re = coef_re[..., None] * b_re - coef_im[..., None] * b_im
    bbar_im = coef_re[..., None] * b_im + coef_im[..., None] * b_re
    groups = a_re.shape[0]
    nblk = groups // S5_BLOCK_GROUPS
    eye = jnp.eye(S5_BLOCK_GROUPS, dtype=F32)

    def in_blocks(bb):
        t = jnp.swapaxes(bb, 1, 2).reshape(nblk, S5_BLOCK_GROUPS, S5_GROUP, S5_STATE)
        blk = jnp.einsum('ab,jacp->jacbp', eye, t)
        return blk.reshape(nblk, S5_BLOCK_GROUPS * S5_GROUP, S5_BLOCK_GROUPS * S5_STATE)

    def out_blocks(cc):
        t = jnp.swapaxes(cc, 1, 2).reshape(nblk, S5_BLOCK_GROUPS, S5_STATE, S5_GROUP)
        blk = jnp.einsum('ab,japc->japbc', eye, t)
        return blk.reshape(nblk, S5_BLOCK_GROUPS * S5_STATE, S5_BLOCK_GROUPS * S5_GROUP)

    rows = groups * S5_STATE // LANES
    return dict(b_re=in_blocks(bbar_re), b_im=in_blocks(bbar_im), c_re=out_blocks(c_re), c_im=out_blocks(-c_im),
                a_re=abar_re.reshape(rows, LANES), a_im=abar_im.reshape(rows, LANES))


def _in_widths(d_model):
    return [BRANCH, BRANCH, SSD_HEADS * SSD_HEAD + 2 * SSD_GROUPS * SSD_STATE, SSD_HEADS, BRANCH,
            3 * BRANCH, GDN_HEADS, GDN_HEADS, BRANCH, 3 * d_model]


def _local_loss(w, mats, x, target):
    n_meta, d_model = w['meta'].shape
    depth = len(mats['in'])
    seq = x.shape[0]
    pad_rows = CHUNK - n_meta
    first = pad_rows + n_meta
    t_all = first + seq
    alpha = (2 * depth) ** 0.25
    fns = _row_fns(d_model, pad_rows, first, alpha)
    row = lambda nm, key, n_row, n_par, widths, cap, ridx=False: _make_rowwise(nm, fns[key], n_row, n_par, widths, cap, ridx)

    h = jnp.concatenate([jnp.zeros((pad_rows, d_model), F32), w['meta'], x], axis=0)
    h, h16 = _make_rowwise("ln_in", fns['ln_in'], 1, 2, [d_model, d_model], 416, True, out_dtypes=[F32, BF16])(
        h, w['ln_in_g'][None], w['ln_in_b'][None])

    n_small = SSD_HEADS + 2 * GDN_HEADS

    def small_cols(ps):
        return jnp.pad(jnp.concatenate([ps[3], ps[6], ps[7]], axis=1), ((0, 0), (0, LANES - n_small)))

    for l in range(depth):
        pw, pc = mats['in'][l], w['c_in'][l]
        used = (0, 1, 2, 4, 5, 8, 9)
        s5_u, s5_z, ssd_xbc, ssd_z, gdn_qkv, gdn_z, gate, small = _in_proj(
            ["in_s5u", "in_s5z", "in_ssdx", "in_ssdz", "in_gdnq", "in_gdnz", "in_gate", "in_small"], h, h16,
            [pw[i] for i in used] + [small_cols(pw)], [pc[i] for i in used] + [small_cols(pc)])

        zeros_tail = jnp.zeros((LANES - n_small,), F32)
        bias = jnp.concatenate([w['ssd_dt_bias'][l], w['gdn_dt_bias'][l], jnp.zeros((GDN_HEADS,), F32), zeros_tail])[None]
        scale = jnp.concatenate([jnp.ones((SSD_HEADS,), F32), -jnp.exp(w['gdn_a_log'][l]),
                                 jnp.zeros((GDN_HEADS,), F32), zeros_tail])[None]
        (sm,) = row("small_act", 'small_act', 1, 2, [LANES], 832, True)(small, bias, scale)

        tb = _s5_tables(w['s5_a_re'][l], w['s5_a_im'][l], w['s5_log_step'][l], w['s5_b_re'][l], w['s5_b_im'][l],
                        w['s5_c_re'][l], w['s5_c_im'][l])
        srows = tb['a_re'].shape[0]
        bu_re = _make_gmm("s5_bre")(s5_u, tb['b_re']).reshape(t_all, srows, LANES)
        bu_im = _make_gmm("s5_bim")(s5_u, tb['b_im']).reshape(t_all, srows, LANES)
        s_re, s_im = _make_s5_scan("s5_scan")(bu_re, bu_im, tb['a_re'], tb['a_im'])
        y_re = _make_gmm("s5_cre")(s_re.reshape(t_all, srows * LANES), tb['c_re'])
        y_im = _make_gmm("s5_cim")(s_im.reshape(t_all, srows * LANES), tb['c_im'])
        (out_a,) = _make_rowwise("s5_tail", fns['s5_tail'], 4, 2, [d_model], 208, n_wt=2)(
            y_re, y_im, s5_u, s5_z, w['s5_d'][l][None], w['s5_b_glu'][l][None],
            mats['glu'][l], mats['branch'][l, 0], w['c_glu'][l], w['c_branch'][l, 0])

        xbc = _make_conv_rowwise("ssd_conv", fns['ssd_conv'], 2, ssd_xbc.shape[1], 416, True)(
            ssd_xbc, w['ssd_conv_w'][l], w['ssd_conv_b'][l][None])
        a_c = jnp.pad(-jnp.exp(w['ssd_a_log'][l]), (0, LANES - SSD_HEADS))[None]
        d_exp = jnp.repeat(w['ssd_d'][l], SSD_HEAD)[None]
        (y_ssd,) = _make_chunk_scan("ssd_scan", _ssd_chunk, (SSD_HEADS // 2 * LANES, SSD_STATE), 2, 2, [BRANCH])(
            xbc, sm, a_c, d_exp)
        (out_b,) = _make_rowwise("ssd_tail", fns['ssd_tail'], 2, 1, [d_model], 416, n_wt=1)(
            y_ssd, ssd_z, w['ssd_norm_g'][l][None], mats['branch'][l, 1], w['c_branch'][l, 1])

        qkv = _make_conv_rowwise("gdn_conv", fns['gdn_conv'], 1, 3 * BRANCH, 208)(gdn_qkv, w['gdn_conv_w'][l])
        (o_gdn,) = _make_chunk_scan("gdn_scan", _gdn_chunk, (GDN_HEADS * GDN_HEAD, GDN_HEAD), 2, 0, [BRANCH])(qkv, sm)
        (out_c,) = _make_rowwise("gdn_tail", fns['gdn_tail'], 2, 1, [d_model], 416, n_wt=1)(
            o_gdn, gdn_z, w['gdn_norm_g'][l][None], mats['branch'][l, 2], w['c_branch'][l, 2])

        h, h16 = _make_rowwise("merge_out", fns['merge_out'], 5, 3, [d_model, d_model], 208, True, n_wt=1,
                               out_dtypes=[F32, BF16])(
            out_a, out_b, out_c, gate, h, w['b_gate'][l].reshape(1, 3 * d_model), w['ln_g'][l][None],
            w['ln_b'][l][None], mats['out'][l], w['c_out'][l])

    tgt = jnp.concatenate([jnp.zeros((first, d_model), F32), target], axis=0)
    (rows_loss,) = row("loss", 'loss', 2, 0, [LANES], 416, True)(h, tgt)
    return jnp.sum(rows_loss)


def _in_overlaps(d_model, n_loc):
    offs = [0]
    for wd in _in_widths(d_model):
        offs.append(offs[-1] + wd)
    out = []
    for i in range(len(offs) - 1):
        c0, c1 = offs[i], offs[i + 1]
        segs = []
        for k in range(N_DEV):
            g0, g1 = max(c0, k * n_loc), min(c1, (k + 1) * n_loc)
            if g0 < g1:
                segs.append((k, g0 - k * n_loc, g1 - k * n_loc, g0 - c0))
        out.append(segs)
    return out


STACKED = ['w_in', 's5_w_glu', 'w_branch', 'w_out']


def _shard_blocks(full, dim):
    shp = full.shape
    parts = full.reshape(shp[:dim] + (N_DEV, shp[dim] // N_DEV) + shp[dim + 1:])
    return jnp.moveaxis(parts, dim, 0)


def _join_blocks(blocks, dim):
    shp = blocks.shape[1:]
    return jnp.moveaxis(blocks, 0, dim).reshape(shp[:dim] + (N_DEV * shp[dim],) + shp[dim + 1:])


def _step(x, target, w_loc, m_loc, v_loc):
    small = [n for n in WEIGHTS if n in SHARD_DIM and n not in STACKED]
    repl = [n for n in WEIGHTS if n not in SHARD_DIM]
    size = lambda names: sum(int(w_loc[n].size) for n in names)
    depth, d_model, n_loc = w_loc['w_in'].shape
    overlaps = _in_overlaps(d_model, n_loc)
    n_chip = N_DEV // 2
    me = 4 * lax.axis_index("x") + 2 * lax.axis_index("y") + lax.axis_index("c")
    my_chip = 2 * lax.axis_index("x") + lax.axis_index("y")
    core = lax.axis_index("c").astype(jnp.int32).reshape(1)
    put = lambda buf, blk, idx: lax.dynamic_update_index_in_dim(buf, blk, idx, 0)
    take = lambda buf, idx: lax.dynamic_index_in_dim(buf, idx, 0, keepdims=False)

    rows_small = _rows_for(size(small))
    small_flat = _pad_flat(jnp.concatenate([w_loc[n].reshape(-1) for n in small]), rows_small)
    own_blocks = [w_loc[n].astype(BF16) for n in STACKED] + [small_flat]
    gathered = [put(got, blk, me) for got, blk in zip(_gather(own_blocks, "gather_weights"), own_blocks)]
    g_in = gathered[0]
    full = {n: _join_blocks(g, SHARD_DIM[n]) for n, g in zip(STACKED[1:], gathered[1:])}
    buf, off = gathered[-1].reshape(N_DEV, -1), 0
    for n in small:
        sz = int(w_loc[n].size)
        full[n] = _join_shards(buf[:, off:off + sz], w_loc[n].shape, SHARD_DIM[n])
        off += sz
    mats = dict(glu=full['s5_w_glu'], branch=full['w_branch'], out=full['w_out'], **{
        'in': [[jnp.concatenate([g_in[k, l, :, lo:hi] for k, lo, hi, _ in segs], axis=1) for segs in overlaps]
               for l in range(depth)]})
    w_diff = {n: w_loc[n] for n in repl}
    w_diff.update({n: full[n] for n in small})
    w_diff['c_in'] = [[jnp.zeros((d_model, wd), F32) for wd in _in_widths(d_model)] for _ in range(depth)]
    w_diff['c_glu'] = jnp.zeros(full['s5_w_glu'].shape, F32)
    w_diff['c_branch'] = jnp.zeros(full['w_branch'].shape, F32)
    w_diff['c_out'] = jnp.zeros(full['w_out'].shape, F32)

    loss, (g_w, g_x) = jax.value_and_grad(_local_loss, argnums=(0, 2))(w_diff, mats, x[0], target[0])
    g_w['s5_w_glu'], g_w['w_branch'], g_w['w_out'] = g_w['c_glu'], g_w['c_branch'], g_w['c_out']

    blocks = {'w_in': jnp.stack([jnp.stack([
        jnp.concatenate([g_w['c_in'][l][i][:, plo:plo + hi - lo]
                         for i, segs in enumerate(overlaps) for (kk, lo, hi, plo) in segs if kk == k], axis=1)
        for l in range(depth)]) for k in range(N_DEV)])}
    for n in STACKED[1:]:
        blocks[n] = _shard_blocks(g_w[n], SHARD_DIM[n])
    cols = {n: w_loc[n].shape[-1] for n in STACKED}
    send = [blocks[n].astype(BF16).reshape(n_chip, 2, -1, cols[n]) for n in STACKED]
    red_names = small + repl
    n_red = sum(int(g_w[n].size) for n in red_names) + 1
    red = _pad_flat(jnp.concatenate([g_w[n].reshape(-1) for n in red_names] + [loss.reshape(1)]), _rows_for(n_red))
    *recv, red_sib = _pair_exchange(send, [red], "pair_exchange")
    sums = [_pair_sum(s, r, core, "pair_sum_" + n) for s, r, n in zip(send, recv, STACKED)]
    red_chip = _add2(red, red_sib, "pair_sum_rest")
    *parts, red_parts = _chip_exchange(sums, [red_chip], "exchange_grads")
    parts = [put(p, take(s, my_chip), my_chip) for p, s in zip(parts, sums)]
    red_total = _add_parts(put(red_parts, red_chip, my_chip), "reduce_rest").reshape(-1)

    outs = {}
    for n, p in zip(STACKED, parts):
        as_rows = lambda a: a.reshape(-1, cols[n])
        res = _adamw_rows(p, as_rows(w_loc[n]), as_rows(m_loc[n]), as_rows(v_loc[n]), "adamw_" + n)
        outs[n] = [r.reshape(w_loc[n].shape) for r in res]
    g_red, off = {}, 0
    for n in red_names:
        sz = int(g_w[n].size)
        g_red[n] = red_total[off:off + sz].reshape(g_w[n].shape)
        off += sz
    loss_total = red_total[off]
    for n in small:
        g_red[n] = take(_shard_blocks(g_red[n], SHARD_DIM[n]), me)
    rows_rest = _rows_for(size(red_names))
    flat = lambda src: _pad_flat(jnp.concatenate([src[n].reshape(-1) for n in red_names]), rows_rest)
    res = _adamw_rows(flat(g_red)[None], flat(w_loc), flat(m_loc), flat(v_loc), "adamw_rest")
    off = 0
    for n in red_names:
        sz = int(w_loc[n].size)
        outs[n] = [r.reshape(-1)[off:off + sz].reshape(w_loc[n].shape) for r in res]
        off += sz
    return (loss_total, g_x[None], *[outs[n][k] for k in range(4) for n in WEIGHTS])


def kernel(x, meta, ln_in_g, ln_in_b, w_in, s5_a_re, s5_a_im, s5_log_step, s5_b_re, s5_b_im, s5_c_re, s5_c_im, s5_d, s5_w_glu, s5_b_glu, ssd_conv_w, ssd_conv_b, ssd_dt_bias, ssd_a_log, ssd_d, ssd_norm_g, gdn_conv_w, gdn_dt_bias, gdn_a_log, gdn_norm_g, w_branch, b_gate, w_out, ln_g, ln_b, loss_target, m_meta, m_ln_in_g, m_ln_in_b, m_w_in, m_s5_a_re, m_s5_a_im, m_s5_log_step, m_s5_b_re, m_s5_b_im, m_s5_c_re, m_s5_c_im, m_s5_d, m_s5_w_glu, m_s5_b_glu, m_ssd_conv_w, m_ssd_conv_b, m_ssd_dt_bias, m_ssd_a_log, m_ssd_d, m_ssd_norm_g, m_gdn_conv_w, m_gdn_dt_bias, m_gdn_a_log, m_gdn_norm_g, m_w_branch, m_b_gate, m_w_out, m_ln_g, m_ln_b, v_meta, v_ln_in_g, v_ln_in_b, v_w_in, v_s5_a_re, v_s5_a_im, v_s5_log_step, v_s5_b_re, v_s5_b_im, v_s5_c_re, v_s5_c_im, v_s5_d, v_s5_w_glu, v_s5_b_glu, v_ssd_conv_w, v_ssd_conv_b, v_ssd_dt_bias, v_ssd_a_log, v_ssd_d, v_ssd_norm_g, v_gdn_conv_w, v_gdn_dt_bias, v_gdn_a_log, v_gdn_norm_g, v_w_branch, v_b_gate, v_w_out, v_ln_g, v_ln_b):
    w_loc = dict(zip(WEIGHTS, (meta, ln_in_g, ln_in_b, w_in, s5_a_re, s5_a_im, s5_log_step, s5_b_re, s5_b_im, s5_c_re, s5_c_im, s5_d, s5_w_glu, s5_b_glu, ssd_conv_w, ssd_conv_b, ssd_dt_bias, ssd_a_log, ssd_d, ssd_norm_g, gdn_conv_w, gdn_dt_bias, gdn_a_log, gdn_norm_g, w_branch, b_gate, w_out, ln_g, ln_b)))
    m_loc = dict(zip(WEIGHTS, (m_meta, m_ln_in_g, m_ln_in_b, m_w_in, m_s5_a_re, m_s5_a_im, m_s5_log_step, m_s5_b_re, m_s5_b_im, m_s5_c_re, m_s5_c_im, m_s5_d, m_s5_w_glu, m_s5_b_glu, m_ssd_conv_w, m_ssd_conv_b, m_ssd_dt_bias, m_ssd_a_log, m_ssd_d, m_ssd_norm_g, m_gdn_conv_w, m_gdn_dt_bias, m_gdn_a_log, m_gdn_norm_g, m_w_branch, m_b_gate, m_w_out, m_ln_g, m_ln_b)))
    v_loc = dict(zip(WEIGHTS, (v_meta, v_ln_in_g, v_ln_in_b, v_w_in, v_s5_a_re, v_s5_a_im, v_s5_log_step, v_s5_b_re, v_s5_b_im, v_s5_c_re, v_s5_c_im, v_s5_d, v_s5_w_glu, v_s5_b_glu, v_ssd_conv_w, v_ssd_conv_b, v_ssd_dt_bias, v_ssd_a_log, v_ssd_d, v_ssd_norm_g, v_gdn_conv_w, v_gdn_dt_bias, v_gdn_a_log, v_gdn_norm_g, v_w_branch, v_b_gate, v_w_out, v_ln_g, v_ln_b)))
    return _step(x, loss_target, w_loc, m_loc, v_loc)
```

```python
import functools

import jax
import jax.numpy as jnp
from jax import lax
from jax.experimental import pallas as pl
from jax.experimental.pallas import tpu as pltpu

F32 = jnp.float32
BF16 = jnp.bfloat16

N_DEV = 8
LANES = 128
SUBLANES = 8
VMEM_LIMIT = 56 * 1024 * 1024
FLAT_ROWS = 1024

CHUNK = 64
CONV_K = 4
S5_GROUP = 16
S5_STATE = 64
S5_BLOCK_GROUPS = 8
SSD_HEAD = 64
SSD_HEADS = 12
SSD_GROUPS = 2
SSD_STATE = 128
GDN_HEAD = 128
GDN_HEADS = 6
BRANCH = 768
LN_EPS = 1e-5

ADAM_LR = 0.001
ADAM_B1 = 0.9
ADAM_B2 = 0.999
ADAM_EPS = 1e-08
ADAM_WD = 0.01
ADAM_STEP = 10

WEIGHTS = ['meta', 'ln_in_g', 'ln_in_b', 'w_in', 's5_a_re', 's5_a_im', 's5_log_step', 's5_b_re', 's5_b_im',
           's5_c_re', 's5_c_im', 's5_d', 's5_w_glu', 's5_b_glu', 'ssd_conv_w', 'ssd_conv_b', 'ssd_dt_bias',
           'ssd_a_log', 'ssd_d', 'ssd_norm_g', 'gdn_conv_w', 'gdn_dt_bias', 'gdn_a_log', 'gdn_norm_g',
           'w_branch', 'b_gate', 'w_out', 'ln_g', 'ln_b']
SHARD_DIM = {'meta': 1, 'w_in': 2, 's5_w_glu': 1, 'ssd_conv_w': 2, 'gdn_conv_w': 2, 'w_branch': 3, 'b_gate': 2,
             'w_out': 1}


def _params(sem):
    return pltpu.CompilerParams(dimension_semantics=sem, vmem_limit_bytes=VMEM_LIMIT)


def _row_tile(m, cap):
    best = None
    for t in range(SUBLANES, min(m, cap) + 1, SUBLANES):
        if m % t == 0:
            best = t
    return best if best is not None else m


def _col_tile(n, cap):
    best = None
    for t in range(LANES, min(n, cap) + 1, LANES):
        if n % t == 0:
            best = t
    return best if best is not None else n


def _any_tile(m, cap):
    best = 1
    for t in range(1, min(m, cap) + 1):
        if m % t == 0:
            best = t
    return best


def _grouped_multi(op, row_ins, whole_ins, outs, g, k, n, name):
    m = row_ins[0].shape[0]
    tm = _row_tile(m, 416)
    n_row, n_whole, n_out = len(row_ins), len(whole_ins), len(outs)
    rows = lambda width: pl.BlockSpec((tm, width), lambda i: (i, 0))
    whole = pl.BlockSpec((g, k, n), lambda i: (0, 0, 0))

    def body(*refs):
        r_refs, w_refs, o_refs = refs[:n_row], refs[n_row:n_row + n_whole], refs[n_row + n_whole:]
        if op == 'tn':
            @pl.when(pl.program_id(0) == 0)
            def _():
                for o_ref in o_refs:
                    o_ref[...] = jnp.zeros_like(o_ref)

        for gi in range(g):
            kc, nc = slice(gi * k, (gi + 1) * k), slice(gi * n, (gi + 1) * n)
            for o_ref, terms in zip(o_refs, outs):
                acc = None
                for i, j in terms:
                    if op == 'nn':
                        t = _dot(r_refs[i][:, kc], w_refs[j][gi])
                    elif op == 'nt':
                        t = _dot_nt(r_refs[i][:, nc], w_refs[j][gi])
                    else:
                        t = _dot_tn(r_refs[i][:, kc], r_refs[j][:, nc])
                    acc = t if acc is None else acc + t
                if op == 'nn':
                    o_ref[:, nc] = acc
                elif op == 'nt':
                    o_ref[:, kc] = acc
                else:
                    o_ref[gi] += acc

    if op == 'tn':
        out_specs, out_shapes = [whole] * n_out, [(g, k, n)] * n_out
    else:
        width = g * n if op == 'nn' else g * k
        out_specs, out_shapes = [rows(width)] * n_out, [(m, width)] * n_out
    res = pl.pallas_call(
        body, name=name, grid=(m // tm,),
        in_specs=[rows(r.shape[1]) for r in row_ins] + [whole] * n_whole, out_specs=out_specs,
        out_shape=[jax.ShapeDtypeStruct(s, F32) for s in out_shapes], compiler_params=_params(("arbitrary",)),
    )(*row_ins, *whole_ins)
    return tuple(res)


def _s5_in(u, b_re, b_im):
    g, k, n = b_re.shape

    @jax.custom_vjp
    def op(u, b_re, b_im):
        return _grouped_multi('nn', [u], [b_re, b_im], [[(0, 0)], [(0, 1)]], g, k, n, "s5_in_fwd")

    def fwd(u, b_re, b_im):
        return (_grouped_multi('nn', [u], [b_re, b_im], [[(0, 0)], [(0, 1)]], g, k, n, "s5_in_fwd"),
                (u, b_re, b_im))

    def bwd(res, cts):
        u, b_re, b_im = res
        (du,) = _grouped_multi('nt', list(cts), [b_re, b_im], [[(0, 0), (1, 1)]], g, k, n, "s5_in_da")
        d_re, d_im = _grouped_multi('tn', [u, cts[0], cts[1]], [], [[(0, 1)], [(0, 2)]], g, k, n, "s5_in_db")
        return du, d_re, d_im

    op.defvjp(fwd, bwd)
    return op(u, b_re, b_im)


def _s5_out(s_re, s_im, c_re, c_im):
    g, k, n = c_re.shape

    @jax.custom_vjp
    def op(s_re, s_im, c_re, c_im):
        return _grouped_multi('nn', [s_re, s_im], [c_re, c_im], [[(0, 0), (1, 1)]], g, k, n, "s5_out_fwd")[0]

    def fwd(s_re, s_im, c_re, c_im):
        y = _grouped_multi('nn', [s_re, s_im], [c_re, c_im], [[(0, 0), (1, 1)]], g, k, n, "s5_out_fwd")[0]
        return y, (s_re, s_im, c_re, c_im)

    def bwd(res, ct):
        s_re, s_im, c_re, c_im = res
        d_sre, d_sim = _grouped_multi('nt', [ct], [c_re, c_im], [[(0, 0)], [(0, 1)]], g, k, n, "s5_out_da")
        d_cre, d_cim = _grouped_multi('tn', [s_re, s_im, ct], [], [[(0, 2)], [(1, 2)]], g, k, n, "s5_out_db")
        return d_sre, d_sim, d_cre, d_cim

    op.defvjp(fwd, bwd)
    return op(s_re, s_im, c_re, c_im)


def _mm_fwd(a, b, name):
    m, _ = a.shape
    g, k, n = b.shape
    tm, tn = _row_tile(m, 832), n
    nj = n // tn

    def body(a_ref, b_ref, o_ref):
        o_ref[...] = jnp.dot(a_ref[...].astype(BF16), b_ref[0].astype(BF16), preferred_element_type=F32)

    return pl.pallas_call(
        body, name=name, grid=(g, m // tm, nj),
        in_specs=[pl.BlockSpec((tm, k), lambda gi, i, j: (i, gi)),
                  pl.BlockSpec((1, k, tn), lambda gi, i, j: (gi, 0, j))],
        out_specs=pl.BlockSpec((tm, tn), lambda gi, i, j: (i, gi * nj + j)),
        out_shape=jax.ShapeDtypeStruct((m, g * n), F32),
        compiler_params=_params(("arbitrary", "arbitrary", "arbitrary")),
    )(a, b)


def _mm_db(a, ct, g, k, n, name):
    m = a.shape[0]
    tm, tk, tn = _row_tile(m, 832 if n <= 1536 else 416), _col_tile(k, 1024), n
    nk, nn = k // tk, n // tn

    def body(a_ref, c_ref, o_ref):
        @pl.when(pl.program_id(3) == 0)
        def _():
            o_ref[...] = jnp.zeros_like(o_ref)

        o_ref[0] += lax.dot_general(a_ref[...].astype(BF16), c_ref[...].astype(BF16), (((0,), (0,)), ((), ())),
                                    preferred_element_type=F32)

    return pl.pallas_call(
        body, name=name, grid=(g, nk, nn, m // tm),
        in_specs=[pl.BlockSpec((tm, tk), lambda gi, i, j, r: (r, gi * nk + i)),
                  pl.BlockSpec((tm, tn), lambda gi, i, j, r: (r, gi * nn + j))],
        out_specs=pl.BlockSpec((1, tk, tn), lambda gi, i, j, r: (gi, i, j)),
        out_shape=jax.ShapeDtypeStruct((g, k, n), F32),
        compiler_params=_params(("arbitrary", "arbitrary", "arbitrary", "arbitrary")),
    )(a, ct)


def _mm_da_sum(cts, ws, name):
    m, k = cts[0].shape[0], ws[0].shape[0]
    tm = _row_tile(m, 208)
    n_p = len(cts)

    def body(*refs):
        acc = None
        for c_ref, w_ref in zip(refs[:n_p], refs[n_p:2 * n_p]):
            term = lax.dot_general(c_ref[...].astype(BF16), w_ref[...].astype(BF16), (((1,), (1,)), ((), ())),
                                   preferred_element_type=F32)
            acc = term if acc is None else acc + term
        refs[2 * n_p][...] = acc

    return pl.pallas_call(
        body, name=name, grid=(m // tm,),
        in_specs=[pl.BlockSpec((tm, c.shape[1]), lambda i: (i, 0)) for c in cts]
        + [pl.BlockSpec(w.shape, lambda i: (0, 0), pipeline_mode=pl.Buffered(1)) for w in ws],
        out_specs=pl.BlockSpec((tm, k), lambda i: (i, 0)),
        out_shape=jax.ShapeDtypeStruct((m, k), F32),
        compiler_params=_params(("arbitrary",)),
    )(*cts, *ws)


def _in_proj(names, a, a16, ws, carriers):
    def products(a16, ws):
        return tuple(_mm_fwd(a16, w[None], nm + "_fwd") for nm, w in zip(names, ws))

    @jax.custom_vjp
    def proj(a, a16, ws, carriers):
        return products(a16, ws)

    def fwd(a, a16, ws, carriers):
        return products(a16, ws), (a16, ws)

    def bwd(res, cts):
        a16, ws = res
        d_ws = tuple(_mm_db(a16, ct, 1, w.shape[0], w.shape[1], nm + "_db")[0] for nm, w, ct in zip(names, ws, cts))
        return (_mm_da_sum(list(cts), list(ws), "in_da"), jnp.zeros_like(a16),
                tuple(jnp.zeros_like(w) for w in ws), d_ws)

    proj.defvjp(fwd, bwd)
    return proj(a, a16, tuple(ws), tuple(carriers))


@jax.custom_vjp
def _wdot(x, w, carrier):
    return _dot(x, w)


def _wdot_fwd(x, w, carrier):
    return _dot(x, w), (x, w)


def _wdot_bwd(res, ct):
    x, w = res
    return _dot_nt(ct, w), jnp.zeros_like(w), _dot_tn(x, ct)


_wdot.defvjp(_wdot_fwd, _wdot_bwd)


def _make_rowwise(name, fn, n_row, n_par, out_widths, tm_cap, use_ridx=False, n_wt=0, out_dtypes=None):
    n_out = len(out_widths)
    out_dtypes = out_dtypes or [F32] * n_out
    n_in = n_row + n_par + n_wt

    def bind(tm):
        if not use_ridx:
            return fn
        ridx = pl.program_id(0) * tm + lax.broadcasted_iota(jnp.int32, (tm, 1), 0)
        return functools.partial(fn, ridx)

    def specs(args, tm):
        rows = [pl.BlockSpec((tm, a.shape[1]), lambda i: (i, 0)) for a in args[:n_row]]
        pars = [pl.BlockSpec(a.shape, lambda i: (0, 0)) for a in args[n_row:n_in]]
        return rows, pars

    def fwd_call(*args):
        t = args[0].shape[0]
        tm = _row_tile(t, tm_cap)
        rows, pars = specs(args, tm)

        def body(*refs):
            vals = [r[...] for r in refs[:n_row + n_par]]
            mats = [functools.partial(lambda x, w: _dot(x, w), w=r[...]) for r in refs[n_row + n_par:n_in]]
            res = bind(tm)(*vals, *mats)
            for o_ref, r in zip(refs[n_in:], res):
                o_ref[...] = r

        outs = pl.pallas_call(
            body, name=name + "_fwd", grid=(t // tm,), in_specs=rows + pars,
            out_specs=[pl.BlockSpec((tm, w), lambda i: (i, 0)) for w in out_widths],
            out_shape=[jax.ShapeDtypeStruct((t, w), dt) for w, dt in zip(out_widths, out_dtypes)],
            compiler_params=_params(("arbitrary",)),
        )(*args)
        return tuple(outs)

    def bwd_call(args, cts):
        t = args[0].shape[0]
        tm = _row_tile(t, tm_cap)
        rows, pars = specs(args, tm)
        ct_specs = [pl.BlockSpec((tm, w), lambda i: (i, 0)) for w in out_widths]
        n_diff = n_row + n_par

        def body(*refs):
            vals = [r[...] for r in refs[:n_diff]]
            wts = [r[...] for r in refs[n_diff:n_in]]
            ct_vals = tuple(r[...] for r in refs[n_in:n_in + n_out])
            d_refs = refs[n_in + n_out:]
            f = bind(tm)

            def g(*a):
                mats = [functools.partial(lambda x, w, c: _wdot(x, w, c), w=w, c=c)
                        for w, c in zip(wts, a[n_diff:])]
                return tuple(f(*a[:n_diff], *mats))

            _, vjp = jax.vjp(g, *vals, *[jnp.zeros(w.shape, F32) for w in wts])
            grads = vjp(ct_vals)
            for i in range(n_row):
                d_refs[i][...] = grads[i]
            if n_par + n_wt:
                @pl.when(pl.program_id(0) == 0)
                def _():
                    for r in d_refs[n_row:]:
                        r[...] = jnp.zeros_like(r)

                for r, gr in zip(d_refs[n_row:], grads[n_row:]):
                    r[...] += gr

        outs = pl.pallas_call(
            body, name=name + "_bwd", grid=(t // tm,), in_specs=rows + pars + ct_specs,
            out_specs=rows + pars,
            out_shape=[jax.ShapeDtypeStruct(a.shape, F32) for a in args],
            compiler_params=_params(("arbitrary",)),
        )(*args, *cts)
        return tuple(outs)

    @jax.custom_vjp
    def op(*args):
        return fwd_call(*args[:n_in])

    def fwd(*args):
        return fwd_call(*args[:n_in]), args[:n_in]

    def bwd(args, cts):
        grads = bwd_call(args, cts)
        return grads[:n_row + n_par] + tuple(jnp.zeros_like(a) for a in args[n_row + n_par:]) + grads[n_row + n_par:]

    op.defvjp(fwd, bwd)
    return op


HALO = SUBLANES


def _make_conv_rowwise(name, fn, n_par, out_width, tm_cap, use_ridx=False):
    def bind(ridx):
        return functools.partial(fn, ridx) if use_ridx else fn

    def stage(x_ref, halo_ref, xs, first):
        xs[0:HALO, :] = jnp.where(first, 0.0, halo_ref[...])
        xs[HALO:, :] = x_ref[...]

    def taps(xs, tm):
        return [xs[pl.ds(HALO - (CONV_K - 1) + j, tm), :] for j in range(CONV_K)]

    def fwd_call(x, *pars):
        t, wd = x.shape
        tm = _row_tile(t, tm_cap)
        per = tm // HALO

        def body(*refs):
            x_ref, halo_ref = refs[:2]
            par_refs, o_ref, xs = refs[2:2 + n_par], refs[2 + n_par], refs[-1]
            i = pl.program_id(0)
            stage(x_ref, halo_ref, xs, i == 0)
            ridx = i * tm + lax.broadcasted_iota(jnp.int32, (tm, 1), 0)
            (o_ref[...],) = bind(ridx)(*taps(xs, tm), *[r[...] for r in par_refs])

        return pl.pallas_call(
            body, name=name + "_fwd", grid=(t // tm,),
            in_specs=[pl.BlockSpec((tm, wd), lambda i: (i, 0)),
                      pl.BlockSpec((HALO, wd), lambda i: (jnp.maximum(i * per - 1, 0), 0))]
            + [pl.BlockSpec(p.shape, lambda i: (0, 0)) for p in pars],
            out_specs=pl.BlockSpec((tm, out_width), lambda i: (i, 0)),
            out_shape=jax.ShapeDtypeStruct((t, out_width), F32),
            scratch_shapes=[pltpu.VMEM((tm + HALO, wd), F32)],
            compiler_params=_params(("arbitrary",)),
        )(x, x, *pars)

    def bwd_call(x, pars, ct):
        t, wd = x.shape
        tm = _row_tile(t, tm_cap)
        per = tm // HALO
        nb = t // tm

        def body(*refs):
            x_ref, halo_ref = refs[:2]
            par_refs, ct_ref = refs[2:2 + n_par], refs[2 + n_par]
            dx_ref, dpar_refs = refs[3 + n_par], refs[4 + n_par:4 + 2 * n_par]
            xs, gs, carry = refs[-3:]
            step = pl.program_id(0)
            blk = nb - 1 - step

            @pl.when(step == 0)
            def _():
                carry[...] = jnp.zeros_like(carry)
                for r in dpar_refs:
                    r[...] = jnp.zeros_like(r)

            stage(x_ref, halo_ref, xs, blk == 0)
            ridx = blk * tm + lax.broadcasted_iota(jnp.int32, (tm, 1), 0)
            f = bind(ridx)
            _, vjp = jax.vjp(lambda *a: tuple(f(*a)), *taps(xs, tm), *[r[...] for r in par_refs])
            grads = vjp((ct_ref[...],))
            dx = grads[CONV_K - 1]
            for j in range(CONV_K - 1):
                gs[j, 0:tm, :] = grads[j]
                gs[j, tm:, :] = carry[j]
                dx = dx + gs[j, pl.ds(CONV_K - 1 - j, tm), :]
                carry[j] = grads[j][0:HALO, :]
            dx_ref[...] = dx
            for r, g in zip(dpar_refs, grads[CONV_K:]):
                r[...] += g

        rev = lambda i: (nb - 1 - i, 0)
        outs = pl.pallas_call(
            body, name=name + "_bwd", grid=(nb,),
            in_specs=[pl.BlockSpec((tm, wd), rev),
                      pl.BlockSpec((HALO, wd), lambda i: (jnp.maximum((nb - 1 - i) * per - 1, 0), 0))]
            + [pl.BlockSpec(p.shape, lambda i: (0, 0)) for p in pars]
            + [pl.BlockSpec((tm, out_width), rev)],
            out_specs=[pl.BlockSpec((tm, wd), rev)] + [pl.BlockSpec(p.shape, lambda i: (0, 0)) for p in pars],
            out_shape=[jax.ShapeDtypeStruct(x.shape, F32)] + [jax.ShapeDtypeStruct(p.shape, F32) for p in pars],
            scratch_shapes=[pltpu.VMEM((tm + HALO, wd), F32), pltpu.VMEM((CONV_K - 1, tm + HALO, wd), F32),
                            pltpu.VMEM((CONV_K - 1, HALO, wd), F32)],
            compiler_params=_params(("arbitrary",)),
        )(x, x, *pars, ct)
        return tuple(outs)

    @jax.custom_vjp
    def op(x, *pars):
        return fwd_call(x, *pars)

    def fwd(x, *pars):
        return fwd_call(x, *pars), (x, pars)

    def bwd(res, ct):
        x, pars = res
        return bwd_call(x, pars, ct)

    op.defvjp(fwd, bwd)
    return op


STEP_CHUNKS = 5


def _make_chunk_scan(name, chunk_fn, state_shape, n_seq, n_par, out_widths):
    n_out = len(out_widths)
    zeros_idx = (0,) * len(state_shape)

    def plan(t):
        per_step = _any_tile(t // CHUNK, STEP_CHUNKS)
        return per_step, per_step * CHUNK, t // (per_step * CHUNK)

    def fn(state, *args):
        per_step = args[0].shape[0] // CHUNK
        outs = []
        for c in range(per_step):
            rows = slice(c * CHUNK, (c + 1) * CHUNK)
            res = chunk_fn(state, *[a[rows, :] for a in args[:n_seq]], *args[n_seq:])
            state = res[0]
            outs.append(res[1:])
        return (state,) + tuple(jnp.concatenate([o[i] for o in outs], axis=0) for i in range(n_out))

    def fwd_call(*args):
        t = args[0].shape[0]
        _, rows_per_step, nc = plan(t)
        seq_specs = [pl.BlockSpec((rows_per_step, a.shape[1]), lambda c: (c, 0)) for a in args[:n_seq]]
        par_specs = [pl.BlockSpec(a.shape, lambda c: (0, 0)) for a in args[n_seq:]]

        def body(*refs):
            ins = refs[:n_seq + n_par]
            out_refs = refs[n_seq + n_par:n_seq + n_par + n_out]
            states_ref = refs[n_seq + n_par + n_out]
            st = refs[-1]

            @pl.when(pl.program_id(0) == 0)
            def _():
                st[...] = jnp.zeros_like(st)

            s0 = st[...]
            states_ref[0] = s0
            res = fn(s0, *[r[...] for r in ins])
            st[...] = res[0]
            for o_ref, r in zip(out_refs, res[1:]):
                o_ref[...] = r

        outs = pl.pallas_call(
            body, name=name + "_fwd", grid=(nc,), in_specs=seq_specs + par_specs,
            out_specs=[pl.BlockSpec((rows_per_step, w), lambda c: (c, 0)) for w in out_widths]
            + [pl.BlockSpec((1,) + state_shape, lambda c: (c,) + zeros_idx)],
            out_shape=[jax.ShapeDtypeStruct((t, w), F32) for w in out_widths]
            + [jax.ShapeDtypeStruct((nc,) + state_shape, F32)],
            scratch_shapes=[pltpu.VMEM(state_shape, F32)],
            compiler_params=_params(("arbitrary",)),
        )(*args)
        return tuple(outs[:n_out]), outs[n_out]

    def bwd_call(args, states, cts):
        t = args[0].shape[0]
        _, rows_per_step, nc = plan(t)
        rev = lambda c: (nc - 1 - c, 0)
        seq_specs = [pl.BlockSpec((rows_per_step, a.shape[1]), rev) for a in args[:n_seq]]
        par_specs = [pl.BlockSpec(a.shape, lambda c: (0, 0)) for a in args[n_seq:]]
        ct_specs = [pl.BlockSpec((rows_per_step, w), rev) for w in out_widths]
        st_spec = pl.BlockSpec((1,) + state_shape, lambda c: (nc - 1 - c,) + zeros_idx)
        n_in = n_seq + n_par

        def body(*refs):
            vals = [r[...] for r in refs[:n_in]]
            s0 = refs[n_in][0]
            ct_vals = tuple(r[...] for r in refs[n_in + 1:n_in + 1 + n_out])
            d_refs = refs[n_in + 1 + n_out:-1]
            dst = refs[-1]

            @pl.when(pl.program_id(0) == 0)
            def _():
                dst[...] = jnp.zeros_like(dst)
                for j in range(n_par):
                    d_refs[n_seq + j][...] = jnp.zeros_like(d_refs[n_seq + j])

            _, vjp = jax.vjp(lambda *a: tuple(fn(*a)), s0, *vals)
            grads = vjp((dst[...],) + ct_vals)
            dst[...] = grads[0]
            for i in range(n_seq):
                d_refs[i][...] = grads[1 + i]
            for j in range(n_par):
                d_refs[n_seq + j][...] += grads[1 + n_seq + j]

        outs = pl.pallas_call(
            body, name=name + "_bwd", grid=(nc,), in_specs=seq_specs + par_specs + [st_spec] + ct_specs,
            out_specs=seq_specs + par_specs,
            out_shape=[jax.ShapeDtypeStruct(a.shape, F32) for a in args],
            scratch_shapes=[pltpu.VMEM(state_shape, F32)],
            compiler_params=_params(("arbitrary",)),
        )(*args, states, *cts)
        return tuple(outs)

    @jax.custom_vjp
    def op(*args):
        return fwd_call(*args)[0]

    def fwd(*args):
        outs, states = fwd_call(*args)
        return outs, (args, states)

    def bwd(res, cts):
        args, states = res
        return bwd_call(args, states, cts)

    op.defvjp(fwd, bwd)
    return op


def _s5_scan_fwd(bre, bim, are, aim, name):
    t, r, _ = bre.shape
    tb = _any_tile(t, 208)
    blk = pl.BlockSpec((tb, r, LANES), lambda i: (i, 0, 0))
    par = pl.BlockSpec((r, LANES), lambda i: (0, 0))

    def body(bre_ref, bim_ref, are_ref, aim_ref, sre_ref, sim_ref, st):
        @pl.when(pl.program_id(0) == 0)
        def _():
            st[...] = jnp.zeros_like(st)

        ar, ai = are_ref[...], aim_ref[...]

        def step(k, carry):
            sr, si = carry
            nr = ar * sr - ai * si + bre_ref[k]
            ni = ar * si + ai * sr + bim_ref[k]
            sre_ref[k] = nr
            sim_ref[k] = ni
            return nr, ni

        sr, si = lax.fori_loop(0, tb, step, (st[0], st[1]), unroll=4)
        st[0] = sr
        st[1] = si

    return pl.pallas_call(
        body, name=name, grid=(t // tb,), in_specs=[blk, blk, par, par], out_specs=[blk, blk],
        out_shape=[jax.ShapeDtypeStruct(bre.shape, F32)] * 2,
        scratch_shapes=[pltpu.VMEM((2, r, LANES), F32)],
        compiler_params=_params(("arbitrary",)),
    )(bre, bim, are, aim)


def _s5_scan_bwd(dsr, dsi, sre, sim, are, aim, name):
    t, r, _ = sre.shape
    tb = _any_tile(t, 208)
    nb = t // tb
    blk = pl.BlockSpec((tb, r, LANES), lambda i: (nb - 1 - i, 0, 0))
    par = pl.BlockSpec((r, LANES), lambda i: (0, 0))

    def body(dsr_ref, dsi_ref, sre_ref, sim_ref, are_ref, aim_ref, gre_ref, gim_ref, dar_ref, dai_ref, st):
        @pl.when(pl.program_id(0) == 0)
        def _():
            st[...] = jnp.zeros_like(st)
            dar_ref[...] = jnp.zeros_like(dar_ref)
            dai_ref[...] = jnp.zeros_like(dai_ref)

        ar, ai = are_ref[...], aim_ref[...]

        def step(k, carry):
            gr, gi, dar, dai = carry
            q = tb - 1 - k
            s_r, s_i = sre_ref[q], sim_ref[q]
            dar = dar + gr * s_r + gi * s_i
            dai = dai + gi * s_r - gr * s_i
            ngr = dsr_ref[q] + ar * gr + ai * gi
            ngi = dsi_ref[q] + ar * gi - ai * gr
            gre_ref[q] = ngr
            gim_ref[q] = ngi
            return ngr, ngi, dar, dai

        gr, gi, dar, dai = lax.fori_loop(0, tb, step, (st[0], st[1], dar_ref[...], dai_ref[...]), unroll=4)
        st[0] = gr
        st[1] = gi
        dar_ref[...] = dar
        dai_ref[...] = dai

    return pl.pallas_call(
        body, name=name, grid=(nb,), in_specs=[blk, blk, blk, blk, par, par], out_specs=[blk, blk, par, par],
        out_shape=[jax.ShapeDtypeStruct(sre.shape, F32)] * 2 + [jax.ShapeDtypeStruct(are.shape, F32)] * 2,
        scratch_shapes=[pltpu.VMEM((2, r, LANES), F32)],
        compiler_params=_params(("arbitrary",)),
    )(dsr, dsi, sre, sim, are, aim)


def _make_s5_scan(name):
    @jax.custom_vjp
    def scan(bre, bim, are, aim):
        return tuple(_s5_scan_fwd(bre, bim, are, aim, name + "_fwd"))

    def fwd(bre, bim, are, aim):
        sre, sim = _s5_scan_fwd(bre, bim, are, aim, name + "_fwd")
        return (sre, sim), (sre, sim, are, aim)

    def bwd(res, cts):
        sre, sim, are, aim = res
        return tuple(_s5_scan_bwd(cts[0], cts[1], sre, sim, are, aim, name + "_bwd"))

    scan.defvjp(fwd, bwd)
    return scan


def _dot(a, b):
    return jnp.dot(a.astype(BF16), b.astype(BF16), preferred_element_type=F32)


def _dot_nt(a, b):
    return lax.dot_general(a.astype(BF16), b.astype(BF16), (((1,), (1,)), ((), ())), preferred_element_type=F32)


def _dot_tn(a, b):
    return lax.dot_general(a.astype(BF16), b.astype(BF16), (((0,), (0,)), ((), ())), preferred_element_type=F32)


def _split3(x):
    x1 = x.astype(BF16)
    rest = x - x1.astype(F32)
    x2 = rest.astype(BF16)
    return x1, x2, (rest - x2.astype(F32)).astype(BF16)


def _sel_dot(dims, a, b, a_is_sel):
    sel = (a if a_is_sel else b).astype(BF16)
    acc = None
    for piece in _split3(b if a_is_sel else a):
        pair = (sel, piece) if a_is_sel else (piece, sel)
        term = lax.dot_general(*pair, (dims, ((), ())), preferred_element_type=F32)
        acc = term if acc is None else acc + term
    return acc


@jax.custom_vjp
def _running_sum(low, y):
    return _sel_dot(((1,), (0,)), low, y, True)


_running_sum.defvjp(lambda low, y: (_running_sum(low, y), low),
                    lambda low, ct: (jnp.zeros_like(low), _sel_dot(((0,), (0,)), low, ct, True)))


@jax.custom_vjp
def _spread(x, sel):
    return _sel_dot(((1,), (0,)), x, sel, False)


_spread.defvjp(lambda x, sel: (_spread(x, sel), sel),
               lambda sel, ct: (_sel_dot(((1,), (1,)), ct, sel, False), jnp.zeros_like(sel)))


@jax.custom_vjp
def _transposed(x, eye):
    return _sel_dot(((0,), (0,)), x, eye, False)


_transposed.defvjp(lambda x, eye: (_transposed(x, eye), eye),
                   lambda eye, ct: (_sel_dot(((1,), (1,)), eye, ct, True), jnp.zeros_like(eye)))


def _iota(shape, dim):
    return lax.broadcasted_iota(jnp.int32, shape, dim)


def _tri(strict=False):
    r, c = _iota((CHUNK, CHUNK), 0), _iota((CHUNK, CHUNK), 1)
    return (r > c) if strict else (r >= c)


def _silu(x):
    return x * jax.nn.sigmoid(x)


def _layer_norm(z, g, b):
    mu = jnp.mean(z, axis=-1, keepdims=True)
    var = jnp.mean(jnp.square(z - mu), axis=-1, keepdims=True)
    return (z - mu) * lax.rsqrt(var + LN_EPS) * g + b


def _ssd_chunk(state, xbc, sm, a_c, d_exp):
    width = SSD_HEADS * SSD_HEAD
    x = xbc[:, :width]
    lane = _iota((CHUNK, LANES), 1)
    dtc = jnp.where(lane < SSD_HEADS, sm, 0.0)
    low = _tri().astype(F32)
    eye = (_iota((CHUNK, CHUNK), 0) == _iota((CHUNK, CHUNK), 1)).astype(F32)
    head_col, head_row = _iota((LANES, width), 1), _iota((LANES, width), 0) * SSD_HEAD
    expand = ((head_col >= head_row) & (head_col < head_row + SSD_HEAD)).astype(F32)
    acum_c = _running_sum(low, dtc * a_c)
    acum_ct = _transposed(acum_c, eye)
    dt_exp = _spread(dtc, expand)
    acum = _spread(acum_c, expand)
    xd = x * dt_exp
    last = acum[CHUNK - 1:CHUNK, :]
    to_end = jnp.exp(last - acum)
    eac = jnp.exp(acum)
    causal = _tri()
    first_half = _iota((CHUNK, LANES), 1) < SSD_HEAD
    top_rows = _iota((LANES, LANES), 0) < SSD_HEAD
    pairs = range(SSD_HEADS // 2)
    grp = [(2 * p) // (SSD_HEADS // SSD_GROUPS) for p in pairs]
    cols = [slice(p * LANES, (p + 1) * LANES) for p in pairs]
    bg = [xbc[:, width + g * SSD_STATE: width + (g + 1) * SSD_STATE] for g in range(SSD_GROUPS)]
    cg = [xbc[:, width + (SSD_GROUPS + g) * SSD_STATE: width + (SSD_GROUPS + g + 1) * SSD_STATE]
          for g in range(SSD_GROUPS)]
    scores = [_dot_nt(cg[g], bg[g]) for g in range(SSD_GROUPS)]
    dec = [jnp.where(causal, jnp.exp(jnp.minimum(acum_c[:, h:h + 1] - acum_ct[h:h + 1, :], 0.0)), 0.0)
           for h in range(SSD_HEADS)]
    y_lo = [_dot(scores[grp[p]] * dec[2 * p], jnp.where(first_half, xd[:, cols[p]], 0.0)) for p in pairs]
    y_hi = [_dot(scores[grp[p]] * dec[2 * p + 1], jnp.where(first_half, 0.0, xd[:, cols[p]])) for p in pairs]
    s_prev = [state[p * LANES:(p + 1) * LANES, :] for p in pairs]
    y_off = [_dot_nt(cg[grp[p]], s_prev[p]) for p in pairs]
    s_add = [_dot_tn(xd[:, cols[p]] * to_end[:, cols[p]], bg[grp[p]]) for p in pairs]
    ys = [y_lo[p] + y_hi[p] + y_off[p] * eac[:, cols[p]] + x[:, cols[p]] * d_exp[:, cols[p]] for p in pairs]
    cd = [jnp.where(top_rows, jnp.exp(acum_c[CHUNK - 1:CHUNK, 2 * p:2 * p + 1]),
                    jnp.exp(acum_c[CHUNK - 1:CHUNK, 2 * p + 1:2 * p + 2])) for p in pairs]
    new_states = [s_prev[p] * cd[p] + s_add[p] for p in pairs]
    return jnp.concatenate(new_states, axis=0), jnp.concatenate(ys, axis=1)


def _neumann(a_mats, rhs, transposed):
    dot = _dot_tn if transposed else _dot
    nmats, sols = [-a for a in a_mats], list(rhs)
    for i in range(6):
        upd = [dot(n, s) for n, s in zip(nmats, sols)]
        if i < 5:
            nmats = [_dot(n, n) for n in nmats]
        sols = [s + u for s, u in zip(sols, upd)]
    return tuple(sols)


@jax.custom_vjp
def _solve_unit_lower(a_mats, rhs):
    return _neumann(a_mats, rhs, False)


def _solve_unit_lower_fwd(a_mats, rhs):
    sols = _neumann(a_mats, rhs, False)
    return sols, (a_mats, sols)


def _solve_unit_lower_bwd(res, d_sols):
    a_mats, sols = res
    d_rhs = _neumann(a_mats, d_sols, True)
    return tuple(-_dot_nt(dr, x) for dr, x in zip(d_rhs, sols)), d_rhs


_solve_unit_lower.defvjp(_solve_unit_lower_fwd, _solve_unit_lower_bwd)


def _gdn_chunk(state, qkv, sm):
    width = GDN_HEADS * GDN_HEAD
    g0, b0 = SSD_HEADS, SSD_HEADS + GDN_HEADS
    lane = _iota((CHUNK, LANES), 1)
    gc = jnp.where((lane >= g0) & (lane < b0), sm, 0.0)
    low = _tri().astype(F32)
    eye = (_iota((CHUNK, CHUNK), 0) == _iota((CHUNK, CHUNK), 1)).astype(F32)
    gcum = _running_sum(low, gc)
    gcum_t = _transposed(gcum, eye)
    causal, strict = _tri(), _tri(strict=True)
    heads = range(GDN_HEADS)
    q = [qkv[:, h * GDN_HEAD:(h + 1) * GDN_HEAD] for h in heads]
    k = [qkv[:, width + h * GDN_HEAD: width + (h + 1) * GDN_HEAD] for h in heads]
    v = [qkv[:, 2 * width + h * GDN_HEAD: 2 * width + (h + 1) * GDN_HEAD] for h in heads]
    beta = [sm[:, b0 + h:b0 + h + 1] for h in heads]
    gcol = [gcum[:, g0 + h:g0 + h + 1] for h in heads]
    glast = [gcum[CHUNK - 1:CHUNK, g0 + h:g0 + h + 1] for h in heads]
    gamma = [jnp.where(causal, jnp.exp(jnp.minimum(gcol[h] - gcum_t[g0 + h:g0 + h + 1, :], 0.0)), 0.0) for h in heads]
    kk = [_dot_nt(k[h], k[h]) for h in heads]
    qk = [_dot_nt(q[h], k[h]) for h in heads]
    egc = [jnp.exp(gcol[h]) for h in heads]
    a_mat = tuple(jnp.where(strict, kk[h] * gamma[h] * beta[h], 0.0) for h in heads)
    sol = _solve_unit_lower(a_mat, tuple(jnp.concatenate([v[h] * beta[h], k[h] * (beta[h] * egc[h])], axis=1)
                                         for h in heads))
    s_prev = [state[h * GDN_HEAD:(h + 1) * GDN_HEAD, :] for h in heads]
    w_s = [_dot(sol[h][:, GDN_HEAD:], s_prev[h]) for h in heads]
    q_s = [_dot(q[h] * egc[h], s_prev[h]) for h in heads]
    v_new = [sol[h][:, :GDN_HEAD] - w_s[h] for h in heads]
    a_v = [_dot(qk[h] * gamma[h], v_new[h]) for h in heads]
    k_v = [_dot_tn(k[h] * jnp.exp(glast[h] - gcol[h]), v_new[h]) for h in heads]
    outs = [q_s[h] + a_v[h] for h in heads]
    new_states = [s_prev[h] * jnp.exp(glast[h]) + k_v[h] for h in heads]
    return jnp.concatenate(new_states, axis=0), jnp.concatenate(outs, axis=1)


def _row_fns(d_model, pad_rows, loss_rows, alpha):
    def keep(ridx, v):
        return jnp.where(ridx >= pad_rows, v, 0.0)

    def both(h):
        return h, h.astype(BF16)

    def ln_in(ridx, h, g, b):
        return both(keep(ridx, _layer_norm(h, g, b)))

    def s5_tail(y, u, z, d, bias, glu, branch):
        v0 = jax.nn.gelu(y + d * u)
        return (branch(v0 * jax.nn.sigmoid(glu(v0) + bias) * _silu(z)),)

    def small_act(ridx, raw, bias, scale):
        lane = _iota(raw.shape, 1)
        sp = jax.nn.softplus(raw + bias)
        g0, b0 = SSD_HEADS, SSD_HEADS + GDN_HEADS
        out = jnp.where(lane < g0, sp, jnp.where(lane < b0, scale * sp,
                                                 jnp.where(lane < b0 + GDN_HEADS, jax.nn.sigmoid(raw), 0.0)))
        return (keep(ridx, out),)

    def conv(xs, w):
        acc = xs[0] * w[0:1, :]
        for j in range(1, CONV_K):
            acc = acc + xs[j] * w[j:j + 1, :]
        return acc

    def ssd_conv(ridx, x0, x1, x2, x3, w, b):
        return (keep(ridx, _silu(conv((x0, x1, x2, x3), w) + b)),)

    def ssd_tail(y, z, g, branch):
        v = y * _silu(z)
        return (branch(v * lax.rsqrt(jnp.mean(v * v, axis=-1, keepdims=True) + LN_EPS) * g),)

    def gdn_conv(x0, x1, x2, x3, w):
        a = _silu(conv((x0, x1, x2, x3), w))
        width = GDN_HEADS * GDN_HEAD
        parts = []
        for h in range(2 * GDN_HEADS):
            z = a[:, h * GDN_HEAD:(h + 1) * GDN_HEAD]
            z = z * lax.rsqrt(jnp.sum(z * z, axis=-1, keepdims=True) + 1e-6)
            parts.append(z * GDN_HEAD ** -0.5 if h < GDN_HEADS else z)
        parts.append(a[:, 2 * width:])
        return (jnp.concatenate(parts, axis=1),)

    def gdn_tail(o, z, g, branch):
        parts = []
        for h in range(GDN_HEADS):
            cols = slice(h * GDN_HEAD, (h + 1) * GDN_HEAD)
            oh = o[:, cols]
            oh = oh * lax.rsqrt(jnp.mean(oh * oh, axis=-1, keepdims=True) + LN_EPS) * g
            parts.append(oh * _silu(z[:, cols]))
        return (branch(jnp.concatenate(parts, axis=1)),)

    def merge_out(ridx, oa, ob, oc, gate, h, bias, g, b, w_out):
        acc = None
        for k, o in enumerate((oa, ob, oc)):
            cols = slice(k * d_model, (k + 1) * d_model)
            term = jax.nn.sigmoid(gate[:, cols] + bias[:, cols]) * o
            acc = term if acc is None else acc + term
        return both(keep(ridx, _layer_norm(alpha * h + w_out(acc), g, b)))

    def loss_rows_fn(ridx, h, tgt):
        row = 0.5 * jnp.mean(jnp.square(h - tgt), axis=-1, keepdims=True)
        row = jnp.where(ridx >= loss_rows, row, 0.0)
        lane = _iota((h.shape[0], LANES), 1)
        return (jnp.where(lane == 0, row, 0.0),)

    return dict(ln_in=ln_in, s5_tail=s5_tail, small_act=small_act, ssd_conv=ssd_conv, ssd_tail=ssd_tail,
                gdn_conv=gdn_conv, gdn_tail=gdn_tail, merge_out=merge_out, loss=loss_rows_fn)


def _pair_exchange(srcs, whole_srcs, name):
    n_arr, n_whole = len(srcs), len(whole_srcs)
    n_chip = N_DEV // 2
    base = n_arr * n_chip

    def body(*refs):
        src_refs, recv_refs = refs[:n_arr + n_whole], refs[n_arr + n_whole:2 * (n_arr + n_whole)]
        send_sems, recv_sems = refs[2 * (n_arr + n_whole):]
        x, y, c = lax.axis_index("x"), lax.axis_index("y"), lax.axis_index("c")

        def to_sibling(src, dst, k):
            return pltpu.make_async_remote_copy(src_ref=src, dst_ref=dst, send_sem=send_sems.at[k],
                                                recv_sem=recv_sems.at[k], device_id=(x, y, 1 - c),
                                                device_id_type=pl.DeviceIdType.MESH)

        copies = [to_sibling(src_refs[a].at[q, 1 - c], recv_refs[a].at[q], a * n_chip + q)
                  for a in range(n_arr) for q in range(n_chip)]
        copies += [to_sibling(src_refs[n_arr + b], recv_refs[n_arr + b], base + b) for b in range(n_whole)]
        for cp in copies:
            cp.start()
        for cp in copies:
            cp.wait()

    n_sem = base + n_whole
    return pl.pallas_call(
        body, name=name,
        in_specs=[pl.BlockSpec(memory_space=pl.ANY)] * (n_arr + n_whole),
        out_specs=[pl.BlockSpec(memory_space=pl.ANY)] * (n_arr + n_whole),
        out_shape=[jax.ShapeDtypeStruct((n_chip,) + tuple(s.shape[2:]), s.dtype) for s in srcs]
        + [jax.ShapeDtypeStruct(s.shape, s.dtype) for s in whole_srcs],
        scratch_shapes=[pltpu.SemaphoreType.DMA((n_sem,)), pltpu.SemaphoreType.DMA((n_sem,))],
    )(*srcs, *whole_srcs)


def _pair_sum(src, recv, core, name):
    n_chip, rows, cols = recv.shape
    tr = _row_tile(rows, 1024 if cols <= LANES else 512)

    def body(core_ref, a_ref, b_ref, o_ref):
        o_ref[...] = (a_ref[0].astype(F32) + b_ref[...].astype(F32)).astype(o_ref.dtype)

    blk = pl.BlockSpec((1, tr, cols), lambda q, i, core_ref: (q, i, 0))
    return pl.pallas_call(
        body, name=name,
        grid_spec=pltpu.PrefetchScalarGridSpec(
            num_scalar_prefetch=1, grid=(n_chip, rows // tr),
            in_specs=[pl.BlockSpec((1, 1, tr, cols), lambda q, i, core_ref: (q, core_ref[0], i, 0)), blk],
            out_specs=blk),
        out_shape=jax.ShapeDtypeStruct(recv.shape, recv.dtype),
        compiler_params=_params(("arbitrary", "arbitrary")),
    )(core, src, recv)


def _add2(a, b, name):
    rows = a.shape[0]
    tr = _row_tile(rows, FLAT_ROWS)
    blk = pl.BlockSpec((tr, LANES), lambda i: (i, 0))

    def body(a_ref, b_ref, o_ref):
        o_ref[...] = a_ref[...] + b_ref[...]

    return pl.pallas_call(body, name=name, grid=(rows // tr,), in_specs=[blk, blk], out_specs=blk,
                          out_shape=jax.ShapeDtypeStruct(a.shape, F32), compiler_params=_params(("arbitrary",)))(a, b)


def _add_parts(parts, name):
    n, rows, _ = parts.shape
    tr = _row_tile(rows, FLAT_ROWS)

    def body(p_ref, o_ref):
        acc = p_ref[0]
        for k in range(1, n):
            acc = acc + p_ref[k]
        o_ref[...] = acc

    return pl.pallas_call(body, name=name, grid=(rows // tr,),
                          in_specs=[pl.BlockSpec((n, tr, LANES), lambda i: (0, i, 0))],
                          out_specs=pl.BlockSpec((tr, LANES), lambda i: (i, 0)),
                          out_shape=jax.ShapeDtypeStruct((rows, LANES), F32),
                          compiler_params=_params(("arbitrary",)))(parts)


def _chip_exchange(chip_srcs, whole_srcs, name):
    n_chip_arr, n_whole = len(chip_srcs), len(whole_srcs)
    n_arr = n_chip_arr + n_whole
    n_chip = N_DEV // 2
    chip_flips = [(1, 0), (0, 1), (1, 1)]

    def body(*refs):
        src_refs, out_refs = refs[:n_arr], refs[n_arr:2 * n_arr]
        send_sems, recv_sems = refs[2 * n_arr:]
        x, y, c = lax.axis_index("x"), lax.axis_index("y"), lax.axis_index("c")
        my_chip = 2 * x + y
        copies = []
        for a in range(n_arr):
            for k, (fx, fy) in enumerate(chip_flips):
                px = 1 - x if fx else x
                py = 1 - y if fy else y
                src = src_refs[a].at[2 * px + py] if a < n_chip_arr else src_refs[a]
                copies.append(pltpu.make_async_remote_copy(
                    src_ref=src, dst_ref=out_refs[a].at[my_chip],
                    send_sem=send_sems.at[a * 3 + k], recv_sem=recv_sems.at[a * 3 + k],
                    device_id=(px, py, c), device_id_type=pl.DeviceIdType.MESH))
        for cp in copies:
            cp.start()
        for cp in copies:
            cp.wait()

    n_sem = n_arr * len(chip_flips)
    return pl.pallas_call(
        body, name=name,
        in_specs=[pl.BlockSpec(memory_space=pl.ANY)] * n_arr, out_specs=[pl.BlockSpec(memory_space=pl.ANY)] * n_arr,
        out_shape=[jax.ShapeDtypeStruct(s.shape, s.dtype) for s in chip_srcs]
        + [jax.ShapeDtypeStruct((n_chip,) + tuple(s.shape), s.dtype) for s in whole_srcs],
        scratch_shapes=[pltpu.SemaphoreType.DMA((n_sem,)), pltpu.SemaphoreType.DMA((n_sem,))],
    )(*chip_srcs, *whole_srcs)


def _gather(srcs, name):
    n_arr = len(srcs)
    n_sem = N_DEV - 1

    def body(*refs):
        src_refs, out_refs = refs[:n_arr], refs[n_arr:2 * n_arr]
        send_sems, recv_sems = refs[2 * n_arr:]
        x, y, c = lax.axis_index("x"), lax.axis_index("y"), lax.axis_index("c")
        me, sibling = (x, y, c), (x, y, 1 - c)
        chips = [(1 - x, y), (x, 1 - y), (1 - x, 1 - y)]

        def slot(a, dev):
            return out_refs[a].at[4 * dev[0] + 2 * dev[1] + dev[2]]

        def copy(a, k, block, to, own=False):
            return pltpu.make_async_remote_copy(
                src_ref=src_refs[a] if own else slot(a, block), dst_ref=slot(a, block),
                send_sem=send_sems.at[a * n_sem + k], recv_sem=recv_sems.at[a * n_sem + k],
                device_id=to, device_id_type=pl.DeviceIdType.MESH)

        arrays = range(n_arr)
        first = [copy(a, 0, me, sibling, own=True) for a in arrays]
        first += [copy(a, 1 + j, me, (*chip, c), own=True) for j, chip in enumerate(chips) for a in arrays]
        for cp in first:
            cp.start()
        passed = []
        for j, chip in enumerate(chips):
            for a in arrays:
                copy(a, 1 + j, (*chip, c), me).wait_recv()
                fwd = copy(a, 4 + j, (*chip, c), sibling)
                fwd.start()
                passed.append(fwd)
        for a in arrays:
            copy(a, 0, sibling, me).wait_recv()
        for j, chip in enumerate(chips):
            for a in arrays:
                copy(a, 4 + j, (*chip, 1 - c), me).wait_recv()
        for cp in first + passed:
            cp.wait_send()

    return pl.pallas_call(
        body, name=name,
        in_specs=[pl.BlockSpec(memory_space=pl.ANY)] * n_arr, out_specs=[pl.BlockSpec(memory_space=pl.ANY)] * n_arr,
        out_shape=[jax.ShapeDtypeStruct((N_DEV,) + tuple(s.shape), s.dtype) for s in srcs],
        scratch_shapes=[pltpu.SemaphoreType.DMA((n_arr * n_sem,)), pltpu.SemaphoreType.DMA((n_arr * n_sem,))],
    )(*srcs)


def _adamw_body(p_ref, w_ref, m_ref, v_ref, g_ref, d_ref, nm_ref, nv_ref):
    bc1 = 1.0 - ADAM_B1 ** ADAM_STEP
    bc2 = 1.0 - ADAM_B2 ** ADAM_STEP
    g = p_ref[0].astype(F32)
    for k in range(1, p_ref.shape[0]):
        g = g + p_ref[k].astype(F32)
    nm = ADAM_B1 * m_ref[...] + (1.0 - ADAM_B1) * g
    nv = ADAM_B2 * v_ref[...] + (1.0 - ADAM_B2) * jnp.square(g)
    m_hat = nm / bc1
    v_hat = nv / bc2
    g_ref[...] = g
    d_ref[...] = -ADAM_LR * (m_hat / (jnp.sqrt(v_hat) + ADAM_EPS) + ADAM_WD * w_ref[...])
    nm_ref[...] = nm
    nv_ref[...] = nv


def _adamw_rows(parts, w, m, v, name):
    rows, cols = w.shape
    tr = _row_tile(rows, max(128, (1 << 18) // cols))
    blk = pl.BlockSpec((tr, cols), lambda i: (i, 0))
    return pl.pallas_call(
        functools.partial(_adamw_body), name=name, grid=(rows // tr,),
        in_specs=[pl.BlockSpec((parts.shape[0], tr, cols), lambda i: (0, i, 0)), blk, blk, blk],
        out_specs=[blk] * 4, out_shape=[jax.ShapeDtypeStruct(w.shape, F32)] * 4,
        compiler_params=_params(("arbitrary",)),
    )(parts, w, m, v)


def _pad_flat(vec, rows):
    return jnp.pad(vec, (0, rows * LANES - vec.shape[0])).reshape(rows, LANES)


def _rows_for(n):
    return -(-n // (FLAT_ROWS * LANES)) * FLAT_ROWS


def _join_shards(rows, local_shape, dim):
    parts = jnp.moveaxis(rows.reshape((N_DEV,) + tuple(local_shape)), 0, dim)
    shp = tuple(local_shape)
    return parts.reshape(shp[:dim] + (N_DEV * shp[dim],) + shp[dim + 1:])


def _s5_tables(a_re, a_im, log_step, b_re, b_im, c_re, c_im):
    lam_re = jnp.minimum(a_re, -1e-4)
    lam_im = a_im
    step = jnp.exp(log_step)[:, None]
    mag = jnp.exp(lam_re * step)
    abar_re, abar_im = mag * jnp.cos(lam_im * step), mag * jnp.sin(lam_im * step)
    den = lam_re * lam_re + lam_im * lam_im
    nr, ni = abar_re - 1.0, abar_im
    coef_re = (nr * lam_re + ni * lam_im) / den
    coef_im = (ni * lam_re - nr * lam_im) / den
    bbar_re = coef_re[..., None] * b_re - coef_im[..., None] * b_im
    bbar_im = coef_re[..., None] * b_im + coef_im[..., None] * b_re
    groups = a_re.shape[0]
    nblk = groups // S5_BLOCK_GROUPS
    eye = jnp.eye(S5_BLOCK_GROUPS, dtype=F32)

    def in_blocks(bb):
        t = jnp.swapaxes(bb, 1, 2).reshape(nblk, S5_BLOCK_GROUPS, S5_GROUP, S5_STATE)
        blk = jnp.einsum('ab,jacp->jacbp', eye, t)
        return blk.reshape(nblk, S5_BLOCK_GROUPS * S5_GROUP, S5_BLOCK_GROUPS * S5_STATE)

    def out_blocks(cc):
        t = jnp.swapaxes(cc, 1, 2).reshape(nblk, S5_BLOCK_GROUPS, S5_STATE, S5_GROUP)
        blk = jnp.einsum('ab,japc->japbc', eye, t)
        return blk.reshape(nblk, S5_BLOCK_GROUPS * S5_STATE, S5_BLOCK_GROUPS * S5_GROUP)

    rows = groups * S5_STATE // LANES
    return dict(b_re=in_blocks(bbar_re), b_im=in_blocks(bbar_im), c_re=out_blocks(c_re), c_im=out_blocks(-c_im),
                a_re=abar_re.reshape(rows, LANES), a_im=abar_im.reshape(rows, LANES))


def _in_widths(d_model):
    return [BRANCH, BRANCH, SSD_HEADS * SSD_HEAD + 2 * SSD_GROUPS * SSD_STATE, SSD_HEADS, BRANCH,
            3 * BRANCH, GDN_HEADS, GDN_HEADS, BRANCH, 3 * d_model]


def _local_loss(w, mats, x, target):
    n_meta, d_model = w['meta'].shape
    depth = len(mats['in'])
    seq = x.shape[0]
    pad_rows = CHUNK - n_meta
    first = pad_rows + n_meta
    t_all = first + seq
    alpha = (2 * depth) ** 0.25
    fns = _row_fns(d_model, pad_rows, first, alpha)
    row = lambda nm, key, n_row, n_par, widths, cap, ridx=False: _make_rowwise(nm, fns[key], n_row, n_par, widths, cap, ridx)

    h = jnp.concatenate([jnp.zeros((pad_rows, d_model), F32), w['meta'], x], axis=0)
    h, h16 = _make_rowwise("ln_in", fns['ln_in'], 1, 2, [d_model, d_model], 416, True, out_dtypes=[F32, BF16])(
        h, w['ln_in_g'][None], w['ln_in_b'][None])

    n_small = SSD_HEADS + 2 * GDN_HEADS

    def small_cols(ps):
        return jnp.pad(jnp.concatenate([ps[3], ps[6], ps[7]], axis=1), ((0, 0), (0, LANES - n_small)))

    for l in range(depth):
        pw, pc = mats['in'][l], w['c_in'][l]
        used = (0, 1, 2, 4, 5, 8, 9)
        s5_u, s5_z, ssd_xbc, ssd_z, gdn_qkv, gdn_z, gate, small = _in_proj(
            ["in_s5u", "in_s5z", "in_ssdx", "in_ssdz", "in_gdnq", "in_gdnz", "in_gate", "in_small"], h, h16,
            [pw[i] for i in used] + [small_cols(pw)], [pc[i] for i in used] + [small_cols(pc)])

        zeros_tail = jnp.zeros((LANES - n_small,), F32)
        bias = jnp.concatenate([w['ssd_dt_bias'][l], w['gdn_dt_bias'][l], jnp.zeros((GDN_HEADS,), F32), zeros_tail])[None]
        scale = jnp.concatenate([jnp.ones((SSD_HEADS,), F32), -jnp.exp(w['gdn_a_log'][l]),
                                 jnp.zeros((GDN_HEADS,), F32), zeros_tail])[None]
        (sm,) = row("small_act", 'small_act', 1, 2, [LANES], 832, True)(small, bias, scale)

        tb = _s5_tables(w['s5_a_re'][l], w['s5_a_im'][l], w['s5_log_step'][l], w['s5_b_re'][l], w['s5_b_im'][l],
                        w['s5_c_re'][l], w['s5_c_im'][l])
        srows = tb['a_re'].shape[0]
        bu_re, bu_im = _s5_in(s5_u, tb['b_re'], tb['b_im'])
        s_re, s_im = _make_s5_scan("s5_scan")(bu_re.reshape(t_all, srows, LANES), bu_im.reshape(t_all, srows, LANES),
                                              tb['a_re'], tb['a_im'])
        y_s5 = _s5_out(s_re.reshape(t_all, srows * LANES), s_im.reshape(t_all, srows * LANES), tb['c_re'], tb['c_im'])
        (out_a,) = _make_rowwise("s5_tail", fns['s5_tail'], 3, 2, [d_model], 208, n_wt=2)(
            y_s5, s5_u, s5_z, w['s5_d'][l][None], w['s5_b_glu'][l][None],
            mats['glu'][l], mats['branch'][l, 0], w['c_glu'][l], w['c_branch'][l, 0])

        xbc = _make_conv_rowwise("ssd_conv", fns['ssd_conv'], 2, ssd_xbc.shape[1], 416, True)(
            ssd_xbc, w['ssd_conv_w'][l], w['ssd_conv_b'][l][None])
        a_c = jnp.pad(-jnp.exp(w['ssd_a_log'][l]), (0, LANES - SSD_HEADS))[None]
        d_exp = jnp.repeat(w['ssd_d'][l], SSD_HEAD)[None]
        (y_ssd,) = _make_chunk_scan("ssd_scan", _ssd_chunk, (SSD_HEADS // 2 * LANES, SSD_STATE), 2, 2, [BRANCH])(
            xbc, sm, a_c, d_exp)
        (out_b,) = _make_rowwise("ssd_tail", fns['ssd_tail'], 2, 1, [d_model], 416, n_wt=1)(
            y_ssd, ssd_z, w['ssd_norm_g'][l][None], mats['branch'][l, 1], w['c_branch'][l, 1])

        qkv = _make_conv_rowwise("gdn_conv", fns['gdn_conv'], 1, 3 * BRANCH, 208)(gdn_qkv, w['gdn_conv_w'][l])
        (o_gdn,) = _make_chunk_scan("gdn_scan", _gdn_chunk, (GDN_HEADS * GDN_HEAD, GDN_HEAD), 2, 0, [BRANCH])(qkv, sm)
        (out_c,) = _make_rowwise("gdn_tail", fns['gdn_tail'], 2, 1, [d_model], 416, n_wt=1)(
            o_gdn, gdn_z, w['gdn_norm_g'][l][None], mats['branch'][l, 2], w['c_branch'][l, 2])

        h, h16 = _make_rowwise("merge_out", fns['merge_out'], 5, 3, [d_model, d_model], 208, True, n_wt=1,
                               out_dtypes=[F32, BF16])(
            out_a, out_b, out_c, gate, h, w['b_gate'][l].reshape(1, 3 * d_model), w['ln_g'][l][None],
            w['ln_b'][l][None], mats['out'][l], w['c_out'][l])

    tgt = jnp.concatenate([jnp.zeros((first, d_model), F32), target], axis=0)
    (rows_loss,) = row("loss", 'loss', 2, 0, [LANES], 416, True)(h, tgt)
    return jnp.sum(rows_loss)


def _in_overlaps(d_model, n_loc):
    offs = [0]
    for wd in _in_widths(d_model):
        offs.append(offs[-1] + wd)
    out = []
    for i in range(len(offs) - 1):
        c0, c1 = offs[i], offs[i + 1]
        segs = []
        for k in range(N_DEV):
            g0, g1 = max(c0, k * n_loc), min(c1, (k + 1) * n_loc)
            if g0 < g1:
                segs.append((k, g0 - k * n_loc, g1 - k * n_loc, g0 - c0))
        out.append(segs)
    return out


STACKED = ['w_in', 's5_w_glu', 'w_branch', 'w_out']


def _shard_blocks(full, dim):
    shp = full.shape
    parts = full.reshape(shp[:dim] + (N_DEV, shp[dim] // N_DEV) + shp[dim + 1:])
    return jnp.moveaxis(parts, dim, 0)


def _join_blocks(blocks, dim):
    shp = blocks.shape[1:]
    return jnp.moveaxis(blocks, 0, dim).reshape(shp[:dim] + (N_DEV * shp[dim],) + shp[dim + 1:])


def _step(x, target, w_loc, m_loc, v_loc):
    small = [n for n in WEIGHTS if n in SHARD_DIM and n not in STACKED]
    repl = [n for n in WEIGHTS if n not in SHARD_DIM]
    size = lambda names: sum(int(w_loc[n].size) for n in names)
    depth, d_model, n_loc = w_loc['w_in'].shape
    overlaps = _in_overlaps(d_model, n_loc)
    n_chip = N_DEV // 2
    me = 4 * lax.axis_index("x") + 2 * lax.axis_index("y") + lax.axis_index("c")
    my_chip = 2 * lax.axis_index("x") + lax.axis_index("y")
    core = lax.axis_index("c").astype(jnp.int32).reshape(1)
    put = lambda buf, blk, idx: lax.dynamic_update_index_in_dim(buf, blk, idx, 0)
    take = lambda buf, idx: lax.dynamic_index_in_dim(buf, idx, 0, keepdims=False)

    rows_small = _rows_for(size(small))
    small_flat = _pad_flat(jnp.concatenate([w_loc[n].reshape(-1) for n in small]), rows_small)
    own_blocks = [w_loc[n].astype(BF16) for n in STACKED] + [small_flat]
    gathered = [put(got, blk, me) for got, blk in zip(_gather(own_blocks, "gather_weights"), own_blocks)]
    g_in = gathered[0]
    full = {n: _join_blocks(g, SHARD_DIM[n]) for n, g in zip(STACKED[1:], gathered[1:])}
    buf, off = gathered[-1].reshape(N_DEV, -1), 0
    for n in small:
        sz = int(w_loc[n].size)
        full[n] = _join_shards(buf[:, off:off + sz], w_loc[n].shape, SHARD_DIM[n])
        off += sz
    mats = dict(glu=full['s5_w_glu'], branch=full['w_branch'], out=full['w_out'], **{
        'in': [[jnp.concatenate([g_in[k, l, :, lo:hi] for k, lo, hi, _ in segs], axis=1) for segs in overlaps]
               for l in range(depth)]})
    w_diff = {n: w_loc[n] for n in repl}
    w_diff.update({n: full[n] for n in small})
    w_diff['c_in'] = [[jnp.zeros((d_model, wd), F32) for wd in _in_widths(d_model)] for _ in range(depth)]
    w_diff['c_glu'] = jnp.zeros(full['s5_w_glu'].shape, F32)
    w_diff['c_branch'] = jnp.zeros(full['w_branch'].shape, F32)
    w_diff['c_out'] = jnp.zeros(full['w_out'].shape, F32)

    loss, (g_w, g_x) = jax.value_and_grad(_local_loss, argnums=(0, 2))(w_diff, mats, x[0], target[0])
    g_w['s5_w_glu'], g_w['w_branch'], g_w['w_out'] = g_w['c_glu'], g_w['c_branch'], g_w['c_out']

    blocks = {'w_in': jnp.stack([jnp.stack([
        jnp.concatenate([g_w['c_in'][l][i][:, plo:plo + hi - lo]
                         for i, segs in enumerate(overlaps) for (kk, lo, hi, plo) in segs if kk == k], axis=1)
        for l in range(depth)]) for k in range(N_DEV)])}
    for n in STACKED[1:]:
        blocks[n] = _shard_blocks(g_w[n], SHARD_DIM[n])
    cols = {n: w_loc[n].shape[-1] for n in STACKED}
    send = [blocks[n].astype(BF16).reshape(n_chip, 2, -1, cols[n]) for n in STACKED]
    red_names = small + repl
    n_red = sum(int(g_w[n].size) for n in red_names) + 1
    red = _pad_flat(jnp.concatenate([g_w[n].reshape(-1) for n in red_names] + [loss.reshape(1)]), _rows_for(n_red))
    *recv, red_sib = _pair_exchange(send, [red], "pair_exchange")
    sums = [_pair_sum(s, r, core, "pair_sum_" + n) for s, r, n in zip(send, recv, STACKED)]
    red_chip = _add2(red, red_sib, "pair_sum_rest")
    *parts, red_parts = _chip_exchange(sums, [red_chip], "exchange_grads")
    parts = [put(p, take(s, my_chip), my_chip) for p, s in zip(parts, sums)]
    red_total = _add_parts(put(red_parts, red_chip, my_chip), "reduce_rest").reshape(-1)

    outs = {}
    for n, p in zip(STACKED, parts):
        as_rows = lambda a: a.reshape(-1, cols[n])
        res = _adamw_rows(p, as_rows(w_loc[n]), as_rows(m_loc[n]), as_rows(v_loc[n]), "adamw_" + n)
        outs[n] = [r.reshape(w_loc[n].shape) for r in res]
    g_red, off = {}, 0
    for n in red_names:
        sz = int(g_w[n].size)
        g_red[n] = red_total[off:off + sz].reshape(g_w[n].shape)
        off += sz
    loss_total = red_total[off]
    for n in small:
        g_red[n] = take(_shard_blocks(g_red[n], SHARD_DIM[n]), me)
    rows_rest = _rows_for(size(red_names))
    flat = lambda src: _pad_flat(jnp.concatenate([src[n].reshape(-1) for n in red_names]), rows_rest)
    res = _adamw_rows(flat(g_red)[None], flat(w_loc), flat(m_loc), flat(v_loc), "adamw_rest")
    off = 0
    for n in red_names:
        sz = int(w_loc[n].size)
        outs[n] = [r.reshape(-1)[off:off + sz].reshape(w_loc[n].shape) for r in res]
        off += sz
    return (loss_total, g_x[None], *[outs[n][k] for k in range(4) for n in WEIGHTS])


def kernel(x, meta, ln_in_g, ln_in_b, w_in, s5_a_re, s5_a_im, s5_log_step, s5_b_re, s5_b_im, s5_c_re, s5_c_im, s5_d, s5_w_glu, s5_b_glu, ssd_conv_w, ssd_conv_b, ssd_dt_bias, ssd_a_log, ssd_d, ssd_norm_g, gdn_conv_w, gdn_dt_bias, gdn_a_log, gdn_norm_g, w_branch, b_gate, w_out, ln_g, ln_b, loss_target, m_meta, m_ln_in_g, m_ln_in_b, m_w_in, m_s5_a_re, m_s5_a_im, m_s5_log_step, m_s5_b_re, m_s5_b_im, m_s5_c_re, m_s5_c_im, m_s5_d, m_s5_w_glu, m_s5_b_glu, m_ssd_conv_w, m_ssd_conv_b, m_ssd_dt_bias, m_ssd_a_log, m_ssd_d, m_ssd_norm_g, m_gdn_conv_w, m_gdn_dt_bias, m_gdn_a_log, m_gdn_norm_g, m_w_branch, m_b_gate, m_w_out, m_ln_g, m_ln_b, v_meta, v_ln_in_g, v_ln_in_b, v_w_in, v_s5_a_re, v_s5_a_im, v_s5_log_step, v_s5_b_re, v_s5_b_im, v_s5_c_re, v_s5_c_im, v_s5_d, v_s5_w_glu, v_s5_b_glu, v_ssd_conv_w, v_ssd_conv_b, v_ssd_dt_bias, v_ssd_a_log, v_ssd_d, v_ssd_norm_g, v_gdn_conv_w, v_gdn_dt_bias, v_gdn_a_log, v_gdn_norm_g, v_w_branch, v_b_gate, v_w_out, v_ln_g, v_ln_b):
    w_loc = dict(zip(WEIGHTS, (meta, ln_in_g, ln_in_b, w_in, s5_a_re, s5_a_im, s5_log_step, s5_b_re, s5_b_im, s5_c_re, s5_c_im, s5_d, s5_w_glu, s5_b_glu, ssd_conv_w, ssd_conv_b, ssd_dt_bias, ssd_a_log, ssd_d, ssd_norm_g, gdn_conv_w, gdn_dt_bias, gdn_a_log, gdn_norm_g, w_branch, b_gate, w_out, ln_g, ln_b)))
    m_loc = dict(zip(WEIGHTS, (m_meta, m_ln_in_g, m_ln_in_b, m_w_in, m_s5_a_re, m_s5_a_im, m_s5_log_step, m_s5_b_re, m_s5_b_im, m_s5_c_re, m_s5_c_im, m_s5_d, m_s5_w_glu, m_s5_b_glu, m_ssd_conv_w, m_ssd_conv_b, m_ssd_dt_bias, m_ssd_a_log, m_ssd_d, m_ssd_norm_g, m_gdn_conv_w, m_gdn_dt_bias, m_gdn_a_log, m_gdn_norm_g, m_w_branch, m_b_gate, m_w_out, m_ln_g, m_ln_b)))
    v_loc = dict(zip(WEIGHTS, (v_meta, v_ln_in_g, v_ln_in_b, v_w_in, v_s5_a_re, v_s5_a_im, v_s5_log_step, v_s5_b_re, v_s5_b_im, v_s5_c_re, v_s5_c_im, v_s5_d, v_s5_w_glu, v_s5_b_glu, v_ssd_conv_w, v_ssd_conv_b, v_ssd_dt_bias, v_ssd_a_log, v_ssd_d, v_ssd_norm_g, v_gdn_conv_w, v_gdn_dt_bias, v_gdn_a_log, v_gdn_norm_g, v_w_branch, v_b_gate, v_w_out, v_ln_g, v_ln_b)))
    return _step(x, loss_target, w_loc, m_loc, v_loc)
```

```python
import functools

import jax
import jax.numpy as jnp
from jax import lax
from jax.experimental import pallas as pl
from jax.experimental.pallas import tpu as pltpu

F32 = jnp.float32
BF16 = jnp.bfloat16

N_DEV = 8
LANES = 128
SUBLANES = 8
VMEM_LIMIT = 56 * 1024 * 1024
FLAT_ROWS = 1024

CHUNK = 64
CONV_K = 4
S5_GROUP = 16
S5_STATE = 64
S5_BLOCK_GROUPS = 8
SSD_HEAD = 64
SSD_HEADS = 12
SSD_GROUPS = 2
SSD_STATE = 128
GDN_HEAD = 128
GDN_HEADS = 6
BRANCH = 768
LN_EPS = 1e-5

ADAM_LR = 0.001
ADAM_B1 = 0.9
ADAM_B2 = 0.999
ADAM_EPS = 1e-08
ADAM_WD = 0.01
ADAM_STEP = 10

WEIGHTS = ['meta', 'ln_in_g', 'ln_in_b', 'w_in', 's5_a_re', 's5_a_im', 's5_log_step', 's5_b_re', 's5_b_im',
           's5_c_re', 's5_c_im', 's5_d', 's5_w_glu', 's5_b_glu', 'ssd_conv_w', 'ssd_conv_b', 'ssd_dt_bias',
           'ssd_a_log', 'ssd_d', 'ssd_norm_g', 'gdn_conv_w', 'gdn_dt_bias', 'gdn_a_log', 'gdn_norm_g',
           'w_branch', 'b_gate', 'w_out', 'ln_g', 'ln_b']
SHARD_DIM = {'meta': 1, 'w_in': 2, 's5_w_glu': 1, 'ssd_conv_w': 2, 'gdn_conv_w': 2, 'w_branch': 3, 'b_gate': 2,
             'w_out': 1}


def _params(sem):
    return pltpu.CompilerParams(dimension_semantics=sem, vmem_limit_bytes=VMEM_LIMIT)


def _row_tile(m, cap):
    best = None
    for t in range(SUBLANES, min(m, cap) + 1, SUBLANES):
        if m % t == 0:
            best = t
    return best if best is not None else m


def _col_tile(n, cap):
    best = None
    for t in range(LANES, min(n, cap) + 1, LANES):
        if n % t == 0:
            best = t
    return best if best is not None else n


def _any_tile(m, cap):
    best = 1
    for t in range(1, min(m, cap) + 1):
        if m % t == 0:
            best = t
    return best


def _grouped_multi(op, row_ins, whole_ins, outs, g, k, n, name):
    m = row_ins[0].shape[0]
    tm = _row_tile(m, 416)
    n_row, n_whole, n_out = len(row_ins), len(whole_ins), len(outs)
    rows = lambda width: pl.BlockSpec((tm, width), lambda i: (i, 0))
    whole = pl.BlockSpec((g, k, n), lambda i: (0, 0, 0))

    def body(*refs):
        r_refs, w_refs, o_refs = refs[:n_row], refs[n_row:n_row + n_whole], refs[n_row + n_whole:]
        if op == 'tn':
            @pl.when(pl.program_id(0) == 0)
            def _():
                for o_ref in o_refs:
                    o_ref[...] = jnp.zeros_like(o_ref)

        for gi in range(g):
            kc, nc = slice(gi * k, (gi + 1) * k), slice(gi * n, (gi + 1) * n)
            for o_ref, terms in zip(o_refs, outs):
                acc = None
                for i, j in terms:
                    if op == 'nn':
                        t = _dot(r_refs[i][:, kc], w_refs[j][gi])
                    elif op == 'nt':
                        t = _dot_nt(r_refs[i][:, nc], w_refs[j][gi])
                    else:
                        t = _dot_tn(r_refs[i][:, kc], r_refs[j][:, nc])
                    acc = t if acc is None else acc + t
                if op == 'nn':
                    o_ref[:, nc] = acc
                elif op == 'nt':
                    o_ref[:, kc] = acc
                else:
                    o_ref[gi] += acc

    if op == 'tn':
        out_specs, out_shapes = [whole] * n_out, [(g, k, n)] * n_out
    else:
        width = g * n if op == 'nn' else g * k
        out_specs, out_shapes = [rows(width)] * n_out, [(m, width)] * n_out
    res = pl.pallas_call(
        body, name=name, grid=(m // tm,),
        in_specs=[rows(r.shape[1]) for r in row_ins] + [whole] * n_whole, out_specs=out_specs,
        out_shape=[jax.ShapeDtypeStruct(s, F32) for s in out_shapes], compiler_params=_params(("arbitrary",)),
    )(*row_ins, *whole_ins)
    return tuple(res)


def _s5_in(u, b_re, b_im):
    g, k, n = b_re.shape

    @jax.custom_vjp
    def op(u, b_re, b_im):
        return _grouped_multi('nn', [u], [b_re, b_im], [[(0, 0)], [(0, 1)]], g, k, n, "s5_in_fwd")

    def fwd(u, b_re, b_im):
        return (_grouped_multi('nn', [u], [b_re, b_im], [[(0, 0)], [(0, 1)]], g, k, n, "s5_in_fwd"),
                (u, b_re, b_im))

    def bwd(res, cts):
        u, b_re, b_im = res
        (du,) = _grouped_multi('nt', list(cts), [b_re, b_im], [[(0, 0), (1, 1)]], g, k, n, "s5_in_da")
        d_re, d_im = _grouped_multi('tn', [u, cts[0], cts[1]], [], [[(0, 1)], [(0, 2)]], g, k, n, "s5_in_db")
        return du, d_re, d_im

    op.defvjp(fwd, bwd)
    return op(u, b_re, b_im)


def _s5_out(s_re, s_im, c_re, c_im):
    g, k, n = c_re.shape

    @jax.custom_vjp
    def op(s_re, s_im, c_re, c_im):
        return _grouped_multi('nn', [s_re, s_im], [c_re, c_im], [[(0, 0), (1, 1)]], g, k, n, "s5_out_fwd")[0]

    def fwd(s_re, s_im, c_re, c_im):
        y = _grouped_multi('nn', [s_re, s_im], [c_re, c_im], [[(0, 0), (1, 1)]], g, k, n, "s5_out_fwd")[0]
        return y, (s_re, s_im, c_re, c_im)

    def bwd(res, ct):
        s_re, s_im, c_re, c_im = res
        d_sre, d_sim = _grouped_multi('nt', [ct], [c_re, c_im], [[(0, 0)], [(0, 1)]], g, k, n, "s5_out_da")
        d_cre, d_cim = _grouped_multi('tn', [s_re, s_im, ct], [], [[(0, 2)], [(1, 2)]], g, k, n, "s5_out_db")
        return d_sre, d_sim, d_cre, d_cim

    op.defvjp(fwd, bwd)
    return op(s_re, s_im, c_re, c_im)


def _mm_fwd(a, b, name):
    m, _ = a.shape
    g, k, n = b.shape
    tm, tn = _row_tile(m, 832), n
    nj = n // tn

    def body(a_ref, b_ref, o_ref):
        o_ref[...] = jnp.dot(a_ref[...].astype(BF16), b_ref[0].astype(BF16), preferred_element_type=F32)

    return pl.pallas_call(
        body, name=name, grid=(g, m // tm, nj),
        in_specs=[pl.BlockSpec((tm, k), lambda gi, i, j: (i, gi)),
                  pl.BlockSpec((1, k, tn), lambda gi, i, j: (gi, 0, j))],
        out_specs=pl.BlockSpec((tm, tn), lambda gi, i, j: (i, gi * nj + j)),
        out_shape=jax.ShapeDtypeStruct((m, g * n), F32),
        compiler_params=_params(("arbitrary", "arbitrary", "arbitrary")),
    )(a, b)


def _mm_db(a, ct, g, k, n, name):
    m = a.shape[0]
    tm, tk, tn = _row_tile(m, 832 if n <= 1536 else 416), _col_tile(k, 1024), n
    nk, nn = k // tk, n // tn

    def body(a_ref, c_ref, o_ref):
        @pl.when(pl.program_id(3) == 0)
        def _():
            o_ref[...] = jnp.zeros_like(o_ref)

        o_ref[0] += lax.dot_general(a_ref[...].astype(BF16), c_ref[...].astype(BF16), (((0,), (0,)), ((), ())),
                                    preferred_element_type=F32)

    return pl.pallas_call(
        body, name=name, grid=(g, nk, nn, m // tm),
        in_specs=[pl.BlockSpec((tm, tk), lambda gi, i, j, r: (r, gi * nk + i)),
                  pl.BlockSpec((tm, tn), lambda gi, i, j, r: (r, gi * nn + j))],
        out_specs=pl.BlockSpec((1, tk, tn), lambda gi, i, j, r: (gi, i, j)),
        out_shape=jax.ShapeDtypeStruct((g, k, n), F32),
        compiler_params=_params(("arbitrary", "arbitrary", "arbitrary", "arbitrary")),
    )(a, ct)


def _mm_da_sum(cts, ws, name):
    m, k = cts[0].shape[0], ws[0].shape[0]
    tm = _row_tile(m, 208)
    n_p = len(cts)

    def body(*refs):
        acc = None
        for c_ref, w_ref in zip(refs[:n_p], refs[n_p:2 * n_p]):
            term = lax.dot_general(c_ref[...].astype(BF16), w_ref[...].astype(BF16), (((1,), (1,)), ((), ())),
                                   preferred_element_type=F32)
            acc = term if acc is None else acc + term
        refs[2 * n_p][...] = acc

    return pl.pallas_call(
        body, name=name, grid=(m // tm,),
        in_specs=[pl.BlockSpec((tm, c.shape[1]), lambda i: (i, 0)) for c in cts]
        + [pl.BlockSpec(w.shape, lambda i: (0, 0), pipeline_mode=pl.Buffered(1)) for w in ws],
        out_specs=pl.BlockSpec((tm, k), lambda i: (i, 0)),
        out_shape=jax.ShapeDtypeStruct((m, k), F32),
        compiler_params=_params(("arbitrary",)),
    )(*cts, *ws)


def _in_proj(names, a, a16, ws, carriers):
    def products(a16, ws):
        m, tm = a16.shape[0], _row_tile(a16.shape[0], 208)

        def body(*refs):
            a = refs[0][...]
            for w_ref, o_ref in zip(refs[1:1 + len(ws)], refs[1 + len(ws):]):
                o_ref[...] = jnp.dot(a, w_ref[...], preferred_element_type=F32)

        return tuple(pl.pallas_call(
            body, name="in_fwd", grid=(m // tm,),
            in_specs=[pl.BlockSpec((tm, a16.shape[1]), lambda i: (i, 0))]
            + [pl.BlockSpec(w.shape, lambda i: (0, 0), pipeline_mode=pl.Buffered(1)) for w in ws],
            out_specs=[pl.BlockSpec((tm, w.shape[1]), lambda i: (i, 0)) for w in ws],
            out_shape=[jax.ShapeDtypeStruct((m, w.shape[1]), F32) for w in ws],
            compiler_params=_params(("arbitrary",)),
        )(a16, *ws))

    @jax.custom_vjp
    def proj(a, a16, ws, carriers):
        return products(a16, ws)

    def fwd(a, a16, ws, carriers):
        return products(a16, ws), (a16, ws)

    def bwd(res, cts):
        a16, ws = res
        d_ws = tuple(_mm_db(a16, ct, 1, w.shape[0], w.shape[1], nm + "_db")[0] for nm, w, ct in zip(names, ws, cts))
        return (_mm_da_sum(list(cts), list(ws), "in_da"), jnp.zeros_like(a16),
                tuple(jnp.zeros_like(w) for w in ws), d_ws)

    proj.defvjp(fwd, bwd)
    return proj(a, a16, tuple(ws), tuple(carriers))


@jax.custom_vjp
def _wdot(x, w, carrier):
    return _dot(x, w)


def _wdot_fwd(x, w, carrier):
    return _dot(x, w), (x, w)


def _wdot_bwd(res, ct):
    x, w = res
    return _dot_nt(ct, w), jnp.zeros_like(w), _dot_tn(x, ct)


_wdot.defvjp(_wdot_fwd, _wdot_bwd)


def _make_rowwise(name, fn, n_row, n_par, out_widths, tm_cap, use_ridx=False, n_wt=0, out_dtypes=None):
    n_out = len(out_widths)
    out_dtypes = out_dtypes or [F32] * n_out
    n_in = n_row + n_par + n_wt

    def bind(tm):
        if not use_ridx:
            return fn
        ridx = pl.program_id(0) * tm + lax.broadcasted_iota(jnp.int32, (tm, 1), 0)
        return functools.partial(fn, ridx)

    def specs(args, tm):
        rows = [pl.BlockSpec((tm, a.shape[1]), lambda i: (i, 0)) for a in args[:n_row]]
        pars = [pl.BlockSpec(a.shape, lambda i: (0, 0)) for a in args[n_row:n_in]]
        return rows, pars

    def fwd_call(*args):
        t = args[0].shape[0]
        tm = _row_tile(t, tm_cap)
        rows, pars = specs(args, tm)

        def body(*refs):
            vals = [r[...] for r in refs[:n_row + n_par]]
            mats = [functools.partial(lambda x, w: _dot(x, w), w=r[...]) for r in refs[n_row + n_par:n_in]]
            res = bind(tm)(*vals, *mats)
            for o_ref, r in zip(refs[n_in:], res):
                o_ref[...] = r

        outs = pl.pallas_call(
            body, name=name + "_fwd", grid=(t // tm,), in_specs=rows + pars,
            out_specs=[pl.BlockSpec((tm, w), lambda i: (i, 0)) for w in out_widths],
            out_shape=[jax.ShapeDtypeStruct((t, w), dt) for w, dt in zip(out_widths, out_dtypes)],
            compiler_params=_params(("arbitrary",)),
        )(*args)
        return tuple(outs)

    def bwd_call(args, cts):
        t = args[0].shape[0]
        tm = _row_tile(t, tm_cap)
        rows, pars = specs(args, tm)
        ct_specs = [pl.BlockSpec((tm, w), lambda i: (i, 0)) for w in out_widths]
        n_diff = n_row + n_par

        def body(*refs):
            vals = [r[...] for r in refs[:n_diff]]
            wts = [r[...] for r in refs[n_diff:n_in]]
            ct_vals = tuple(r[...] for r in refs[n_in:n_in + n_out])
            d_refs = refs[n_in + n_out:]
            f = bind(tm)

            def g(*a):
                mats = [functools.partial(lambda x, w, c: _wdot(x, w, c), w=w, c=c)
                        for w, c in zip(wts, a[n_diff:])]
                return tuple(f(*a[:n_diff], *mats))

            _, vjp = jax.vjp(g, *vals, *[jnp.zeros(w.shape, F32) for w in wts])
            grads = vjp(ct_vals)
            for i in range(n_row):
                d_refs[i][...] = grads[i]
            if n_par + n_wt:
                @pl.when(pl.program_id(0) == 0)
                def _():
                    for r in d_refs[n_row:]:
                        r[...] = jnp.zeros_like(r)

                for r, gr in zip(d_refs[n_row:], grads[n_row:]):
                    r[...] += gr

        outs = pl.pallas_call(
            body, name=name + "_bwd", grid=(t // tm,), in_specs=rows + pars + ct_specs,
            out_specs=rows + pars,
            out_shape=[jax.ShapeDtypeStruct(a.shape, F32) for a in args],
            compiler_params=_params(("arbitrary",)),
        )(*args, *cts)
        return tuple(outs)

    @jax.custom_vjp
    def op(*args):
        return fwd_call(*args[:n_in])

    def fwd(*args):
        return fwd_call(*args[:n_in]), args[:n_in]

    def bwd(args, cts):
        grads = bwd_call(args, cts)
        return grads[:n_row + n_par] + tuple(jnp.zeros_like(a) for a in args[n_row + n_par:]) + grads[n_row + n_par:]

    op.defvjp(fwd, bwd)
    return op


HALO = SUBLANES


def _make_conv_rowwise(name, fn, n_par, out_width, tm_cap, use_ridx=False):
    def bind(ridx):
        return functools.partial(fn, ridx) if use_ridx else fn

    def stage(x_ref, halo_ref, xs, first):
        xs[0:HALO, :] = jnp.where(first, 0.0, halo_ref[...])
        xs[HALO:, :] = x_ref[...]

    def taps(xs, tm):
        return [xs[pl.ds(HALO - (CONV_K - 1) + j, tm), :] for j in range(CONV_K)]

    def fwd_call(x, *pars):
        t, wd = x.shape
        tm = _row_tile(t, tm_cap)
        per = tm // HALO

        def body(*refs):
            x_ref, halo_ref = refs[:2]
            par_refs, o_ref, xs = refs[2:2 + n_par], refs[2 + n_par], refs[-1]
            i = pl.program_id(0)
            stage(x_ref, halo_ref, xs, i == 0)
            ridx = i * tm + lax.broadcasted_iota(jnp.int32, (tm, 1), 0)
            (o_ref[...],) = bind(ridx)(*taps(xs, tm), *[r[...] for r in par_refs])

        return pl.pallas_call(
            body, name=name + "_fwd", grid=(t // tm,),
            in_specs=[pl.BlockSpec((tm, wd), lambda i: (i, 0)),
                      pl.BlockSpec((HALO, wd), lambda i: (jnp.maximum(i * per - 1, 0), 0))]
            + [pl.BlockSpec(p.shape, lambda i: (0, 0)) for p in pars],
            out_specs=pl.BlockSpec((tm, out_width), lambda i: (i, 0)),
            out_shape=jax.ShapeDtypeStruct((t, out_width), F32),
            scratch_shapes=[pltpu.VMEM((tm + HALO, wd), F32)],
            compiler_params=_params(("arbitrary",)),
        )(x, x, *pars)

    def bwd_call(x, pars, ct):
        t, wd = x.shape
        tm = _row_tile(t, tm_cap)
        per = tm // HALO
        nb = t // tm

        def body(*refs):
            x_ref, halo_ref = refs[:2]
            par_refs, ct_ref = refs[2:2 + n_par], refs[2 + n_par]
            dx_ref, dpar_refs = refs[3 + n_par], refs[4 + n_par:4 + 2 * n_par]
            xs, gs, carry = refs[-3:]
            step = pl.program_id(0)
            blk = nb - 1 - step

            @pl.when(step == 0)
            def _():
                carry[...] = jnp.zeros_like(carry)
                for r in dpar_refs:
                    r[...] = jnp.zeros_like(r)

            stage(x_ref, halo_ref, xs, blk == 0)
            ridx = blk * tm + lax.broadcasted_iota(jnp.int32, (tm, 1), 0)
            f = bind(ridx)
            _, vjp = jax.vjp(lambda *a: tuple(f(*a)), *taps(xs, tm), *[r[...] for r in par_refs])
            grads = vjp((ct_ref[...],))
            dx = grads[CONV_K - 1]
            for j in range(CONV_K - 1):
                gs[j, 0:tm, :] = grads[j]
                gs[j, tm:, :] = carry[j]
                dx = dx + gs[j, pl.ds(CONV_K - 1 - j, tm), :]
                carry[j] = grads[j][0:HALO, :]
            dx_ref[...] = dx
            for r, g in zip(dpar_refs, grads[CONV_K:]):
                r[...] += g

        rev = lambda i: (nb - 1 - i, 0)
        outs = pl.pallas_call(
            body, name=name + "_bwd", grid=(nb,),
            in_specs=[pl.BlockSpec((tm, wd), rev),
                      pl.BlockSpec((HALO, wd), lambda i: (jnp.maximum((nb - 1 - i) * per - 1, 0), 0))]
            + [pl.BlockSpec(p.shape, lambda i: (0, 0)) for p in pars]
            + [pl.BlockSpec((tm, out_width), rev)],
            out_specs=[pl.BlockSpec((tm, wd), rev)] + [pl.BlockSpec(p.shape, lambda i: (0, 0)) for p in pars],
            out_shape=[jax.ShapeDtypeStruct(x.shape, F32)] + [jax.ShapeDtypeStruct(p.shape, F32) for p in pars],
            scratch_shapes=[pltpu.VMEM((tm + HALO, wd), F32), pltpu.VMEM((CONV_K - 1, tm + HALO, wd), F32),
                            pltpu.VMEM((CONV_K - 1, HALO, wd), F32)],
            compiler_params=_params(("arbitrary",)),
        )(x, x, *pars, ct)
        return tuple(outs)

    @jax.custom_vjp
    def op(x, *pars):
        return fwd_call(x, *pars)

    def fwd(x, *pars):
        return fwd_call(x, *pars), (x, pars)

    def bwd(res, ct):
        x, pars = res
        return bwd_call(x, pars, ct)

    op.defvjp(fwd, bwd)
    return op


STEP_CHUNKS = 5


def _make_chunk_scan(name, chunk_fn, state_shape, n_seq, n_par, out_widths):
    n_out = len(out_widths)
    zeros_idx = (0,) * len(state_shape)

    def plan(t):
        per_step = _any_tile(t // CHUNK, STEP_CHUNKS)
        return per_step, per_step * CHUNK, t // (per_step * CHUNK)

    def fn(state, *args):
        per_step = args[0].shape[0] // CHUNK
        outs = []
        for c in range(per_step):
            rows = slice(c * CHUNK, (c + 1) * CHUNK)
            res = chunk_fn(state, *[a[rows, :] for a in args[:n_seq]], *args[n_seq:])
            state = res[0]
            outs.append(res[1:])
        return (state,) + tuple(jnp.concatenate([o[i] for o in outs], axis=0) for i in range(n_out))

    def fwd_call(*args):
        t = args[0].shape[0]
        _, rows_per_step, nc = plan(t)
        seq_specs = [pl.BlockSpec((rows_per_step, a.shape[1]), lambda c: (c, 0)) for a in args[:n_seq]]
        par_specs = [pl.BlockSpec(a.shape, lambda c: (0, 0)) for a in args[n_seq:]]

        def body(*refs):
            ins = refs[:n_seq + n_par]
            out_refs = refs[n_seq + n_par:n_seq + n_par + n_out]
            states_ref = refs[n_seq + n_par + n_out]
            st = refs[-1]

            @pl.when(pl.program_id(0) == 0)
            def _():
                st[...] = jnp.zeros_like(st)

            s0 = st[...]
            states_ref[0] = s0
            res = fn(s0, *[r[...] for r in ins])
            st[...] = res[0]
            for o_ref, r in zip(out_refs, res[1:]):
                o_ref[...] = r

        outs = pl.pallas_call(
            body, name=name + "_fwd", grid=(nc,), in_specs=seq_specs + par_specs,
            out_specs=[pl.BlockSpec((rows_per_step, w), lambda c: (c, 0)) for w in out_widths]
            + [pl.BlockSpec((1,) + state_shape, lambda c: (c,) + zeros_idx)],
            out_shape=[jax.ShapeDtypeStruct((t, w), F32) for w in out_widths]
            + [jax.ShapeDtypeStruct((nc,) + state_shape, F32)],
            scratch_shapes=[pltpu.VMEM(state_shape, F32)],
            compiler_params=_params(("arbitrary",)),
        )(*args)
        return tuple(outs[:n_out]), outs[n_out]

    def bwd_call(args, states, cts):
        t = args[0].shape[0]
        _, rows_per_step, nc = plan(t)
        rev = lambda c: (nc - 1 - c, 0)
        seq_specs = [pl.BlockSpec((rows_per_step, a.shape[1]), rev) for a in args[:n_seq]]
        par_specs = [pl.BlockSpec(a.shape, lambda c: (0, 0)) for a in args[n_seq:]]
        ct_specs = [pl.BlockSpec((rows_per_step, w), rev) for w in out_widths]
        st_spec = pl.BlockSpec((1,) + state_shape, lambda c: (nc - 1 - c,) + zeros_idx)
        n_in = n_seq + n_par

        def body(*refs):
            vals = [r[...] for r in refs[:n_in]]
            s0 = refs[n_in][0]
            ct_vals = tuple(r[...] for r in refs[n_in + 1:n_in + 1 + n_out])
            d_refs = refs[n_in + 1 + n_out:-1]
            dst = refs[-1]

            @pl.when(pl.program_id(0) == 0)
            def _():
                dst[...] = jnp.zeros_like(dst)
                for j in range(n_par):
                    d_refs[n_seq + j][...] = jnp.zeros_like(d_refs[n_seq + j])

            _, vjp = jax.vjp(lambda *a: tuple(fn(*a)), s0, *vals)
            grads = vjp((dst[...],) + ct_vals)
            dst[...] = grads[0]
            for i in range(n_seq):
                d_refs[i][...] = grads[1 + i]
            for j in range(n_par):
                d_refs[n_seq + j][...] += grads[1 + n_seq + j]

        outs = pl.pallas_call(
            body, name=name + "_bwd", grid=(nc,), in_specs=seq_specs + par_specs + [st_spec] + ct_specs,
            out_specs=seq_specs + par_specs,
            out_shape=[jax.ShapeDtypeStruct(a.shape, F32) for a in args],
            scratch_shapes=[pltpu.VMEM(state_shape, F32)],
            compiler_params=_params(("arbitrary",)),
        )(*args, states, *cts)
        return tuple(outs)

    @jax.custom_vjp
    def op(*args):
        return fwd_call(*args)[0]

    def fwd(*args):
        outs, states = fwd_call(*args)
        return outs, (args, states)

    def bwd(res, cts):
        args, states = res
        return bwd_call(args, states, cts)

    op.defvjp(fwd, bwd)
    return op


def _s5_scan_fwd(bre, bim, are, aim, name):
    t, r, _ = bre.shape
    tb = _any_tile(t, 208)
    blk = pl.BlockSpec((tb, r, LANES), lambda i: (i, 0, 0))
    par = pl.BlockSpec((r, LANES), lambda i: (0, 0))

    def body(bre_ref, bim_ref, are_ref, aim_ref, sre_ref, sim_ref, st):
        @pl.when(pl.program_id(0) == 0)
        def _():
            st[...] = jnp.zeros_like(st)

        ar, ai = are_ref[...], aim_ref[...]

        def step(k, carry):
            sr, si = carry
            nr = ar * sr - ai * si + bre_ref[k]
            ni = ar * si + ai * sr + bim_ref[k]
            sre_ref[k] = nr
            sim_ref[k] = ni
            return nr, ni

        sr, si = lax.fori_loop(0, tb, step, (st[0], st[1]), unroll=4)
        st[0] = sr
        st[1] = si

    return pl.pallas_call(
        body, name=name, grid=(t // tb,), in_specs=[blk, blk, par, par], out_specs=[blk, blk],
        out_shape=[jax.ShapeDtypeStruct(bre.shape, F32)] * 2,
        scratch_shapes=[pltpu.VMEM((2, r, LANES), F32)],
        compiler_params=_params(("arbitrary",)),
    )(bre, bim, are, aim)


def _s5_scan_bwd(dsr, dsi, sre, sim, are, aim, name):
    t, r, _ = sre.shape
    tb = _any_tile(t, 208)
    nb = t // tb
    blk = pl.BlockSpec((tb, r, LANES), lambda i: (nb - 1 - i, 0, 0))
    par = pl.BlockSpec((r, LANES), lambda i: (0, 0))

    def body(dsr_ref, dsi_ref, sre_ref, sim_ref, are_ref, aim_ref, gre_ref, gim_ref, dar_ref, dai_ref, st):
        @pl.when(pl.program_id(0) == 0)
        def _():
            st[...] = jnp.zeros_like(st)
            dar_ref[...] = jnp.zeros_like(dar_ref)
            dai_ref[...] = jnp.zeros_like(dai_ref)

        ar, ai = are_ref[...], aim_ref[...]

        def step(k, carry):
            gr, gi, dar, dai = carry
            q = tb - 1 - k
            s_r, s_i = sre_ref[q], sim_ref[q]
            dar = dar + gr * s_r + gi * s_i
            dai = dai + gi * s_r - gr * s_i
            ngr = dsr_ref[q] + ar * gr + ai * gi
            ngi = dsi_ref[q] + ar * gi - ai * gr
            gre_ref[q] = ngr
            gim_ref[q] = ngi
            return ngr, ngi, dar, dai

        gr, gi, dar, dai = lax.fori_loop(0, tb, step, (st[0], st[1], dar_ref[...], dai_ref[...]), unroll=4)
        st[0] = gr
        st[1] = gi
        dar_ref[...] = dar
        dai_ref[...] = dai

    return pl.pallas_call(
        body, name=name, grid=(nb,), in_specs=[blk, blk, blk, blk, par, par], out_specs=[blk, blk, par, par],
        out_shape=[jax.ShapeDtypeStruct(sre.shape, F32)] * 2 + [jax.ShapeDtypeStruct(are.shape, F32)] * 2,
        scratch_shapes=[pltpu.VMEM((2, r, LANES), F32)],
        compiler_params=_params(("arbitrary",)),
    )(dsr, dsi, sre, sim, are, aim)


def _make_s5_scan(name):
    @jax.custom_vjp
    def scan(bre, bim, are, aim):
        return tuple(_s5_scan_fwd(bre, bim, are, aim, name + "_fwd"))

    def fwd(bre, bim, are, aim):
        sre, sim = _s5_scan_fwd(bre, bim, are, aim, name + "_fwd")
        return (sre, sim), (sre, sim, are, aim)

    def bwd(res, cts):
        sre, sim, are, aim = res
        return tuple(_s5_scan_bwd(cts[0], cts[1], sre, sim, are, aim, name + "_bwd"))

    scan.defvjp(fwd, bwd)
    return scan


def _dot(a, b):
    return jnp.dot(a.astype(BF16), b.astype(BF16), preferred_element_type=F32)


def _dot_nt(a, b):
    return lax.dot_general(a.astype(BF16), b.astype(BF16), (((1,), (1,)), ((), ())), preferred_element_type=F32)


def _dot_tn(a, b):
    return lax.dot_general(a.astype(BF16), b.astype(BF16), (((0,), (0,)), ((), ())), preferred_element_type=F32)


def _split3(x):
    x1 = x.astype(BF16)
    rest = x - x1.astype(F32)
    x2 = rest.astype(BF16)
    return x1, x2, (rest - x2.astype(F32)).astype(BF16)


def _sel_dot(dims, a, b, a_is_sel):
    sel = (a if a_is_sel else b).astype(BF16)
    acc = None
    for piece in _split3(b if a_is_sel else a):
        pair = (sel, piece) if a_is_sel else (piece, sel)
        term = lax.dot_general(*pair, (dims, ((), ())), preferred_element_type=F32)
        acc = term if acc is None else acc + term
    return acc


@jax.custom_vjp
def _running_sum(low, y):
    return _sel_dot(((1,), (0,)), low, y, True)


_running_sum.defvjp(lambda low, y: (_running_sum(low, y), low),
                    lambda low, ct: (jnp.zeros_like(low), _sel_dot(((0,), (0,)), low, ct, True)))


@jax.custom_vjp
def _spread(x, sel):
    return _sel_dot(((1,), (0,)), x, sel, False)


_spread.defvjp(lambda x, sel: (_spread(x, sel), sel),
               lambda sel, ct: (_sel_dot(((1,), (1,)), ct, sel, False), jnp.zeros_like(sel)))


@jax.custom_vjp
def _transposed(x, eye):
    return _sel_dot(((0,), (0,)), x, eye, False)


_transposed.defvjp(lambda x, eye: (_transposed(x, eye), eye),
                   lambda eye, ct: (_sel_dot(((1,), (1,)), eye, ct, True), jnp.zeros_like(eye)))


def _iota(shape, dim):
    return lax.broadcasted_iota(jnp.int32, shape, dim)


def _tri(strict=False):
    r, c = _iota((CHUNK, CHUNK), 0), _iota((CHUNK, CHUNK), 1)
    return (r > c) if strict else (r >= c)


def _silu(x):
    return x * jax.nn.sigmoid(x)


def _layer_norm(z, g, b):
    mu = jnp.mean(z, axis=-1, keepdims=True)
    var = jnp.mean(jnp.square(z - mu), axis=-1, keepdims=True)
    return (z - mu) * lax.rsqrt(var + LN_EPS) * g + b


def _ssd_chunk(state, xbc, sm, a_c, d_exp):
    width = SSD_HEADS * SSD_HEAD
    x = xbc[:, :width]
    lane = _iota((CHUNK, LANES), 1)
    dtc = jnp.where(lane < SSD_HEADS, sm, 0.0)
    low = _tri().astype(F32)
    eye = (_iota((CHUNK, CHUNK), 0) == _iota((CHUNK, CHUNK), 1)).astype(F32)
    head_col, head_row = _iota((LANES, width), 1), _iota((LANES, width), 0) * SSD_HEAD
    expand = ((head_col >= head_row) & (head_col < head_row + SSD_HEAD)).astype(F32)
    acum_c = _running_sum(low, dtc * a_c)
    acum_ct = _transposed(acum_c, eye)
    dt_exp = _spread(dtc, expand)
    acum = _spread(acum_c, expand)
    xd = x * dt_exp
    last = acum[CHUNK - 1:CHUNK, :]
    to_end = jnp.exp(last - acum)
    eac = jnp.exp(acum)
    causal = _tri()
    first_half = _iota((CHUNK, LANES), 1) < SSD_HEAD
    top_rows = _iota((LANES, LANES), 0) < SSD_HEAD
    pairs = range(SSD_HEADS // 2)
    grp = [(2 * p) // (SSD_HEADS // SSD_GROUPS) for p in pairs]
    cols = [slice(p * LANES, (p + 1) * LANES) for p in pairs]
    bg = [xbc[:, width + g * SSD_STATE: width + (g + 1) * SSD_STATE] for g in range(SSD_GROUPS)]
    cg = [xbc[:, width + (SSD_GROUPS + g) * SSD_STATE: width + (SSD_GROUPS + g + 1) * SSD_STATE]
          for g in range(SSD_GROUPS)]
    scores = [_dot_nt(cg[g], bg[g]) for g in range(SSD_GROUPS)]
    dec = [jnp.where(causal, jnp.exp(jnp.minimum(acum_c[:, h:h + 1] - acum_ct[h:h + 1, :], 0.0)), 0.0)
           for h in range(SSD_HEADS)]
    y_lo = [_dot(scores[grp[p]] * dec[2 * p], jnp.where(first_half, xd[:, cols[p]], 0.0)) for p in pairs]
    y_hi = [_dot(scores[grp[p]] * dec[2 * p + 1], jnp.where(first_half, 0.0, xd[:, cols[p]])) for p in pairs]
    s_prev = [state[p * LANES:(p + 1) * LANES, :] for p in pairs]
    y_off = [_dot_nt(cg[grp[p]], s_prev[p]) for p in pairs]
    s_add = [_dot_tn(xd[:, cols[p]] * to_end[:, cols[p]], bg[grp[p]]) for p in pairs]
    ys = [y_lo[p] + y_hi[p] + y_off[p] * eac[:, cols[p]] + x[:, cols[p]] * d_exp[:, cols[p]] for p in pairs]
    cd = [jnp.where(top_rows, jnp.exp(acum_c[CHUNK - 1:CHUNK, 2 * p:2 * p + 1]),
                    jnp.exp(acum_c[CHUNK - 1:CHUNK, 2 * p + 1:2 * p + 2])) for p in pairs]
    new_states = [s_prev[p] * cd[p] + s_add[p] for p in pairs]
    return jnp.concatenate(new_states, axis=0), jnp.concatenate(ys, axis=1)


def _neumann(a_mats, rhs, transposed):
    dot = _dot_tn if transposed else _dot
    nmats, sols = [-a for a in a_mats], list(rhs)
    for i in range(6):
        upd = [dot(n, s) for n, s in zip(nmats, sols)]
        if i < 5:
            nmats = [_dot(n, n) for n in nmats]
        sols = [s + u for s, u in zip(sols, upd)]
    return tuple(sols)


@jax.custom_vjp
def _solve_unit_lower(a_mats, rhs):
    return _neumann(a_mats, rhs, False)


def _solve_unit_lower_fwd(a_mats, rhs):
    sols = _neumann(a_mats, rhs, False)
    return sols, (a_mats, sols)


def _solve_unit_lower_bwd(res, d_sols):
    a_mats, sols = res
    d_rhs = _neumann(a_mats, d_sols, True)
    return tuple(-_dot_nt(dr, x) for dr, x in zip(d_rhs, sols)), d_rhs


_solve_unit_lower.defvjp(_solve_unit_lower_fwd, _solve_unit_lower_bwd)


def _gdn_chunk(state, qkv, sm):
    width = GDN_HEADS * GDN_HEAD
    g0, b0 = SSD_HEADS, SSD_HEADS + GDN_HEADS
    lane = _iota((CHUNK, LANES), 1)
    gc = jnp.where((lane >= g0) & (lane < b0), sm, 0.0)
    low = _tri().astype(F32)
    eye = (_iota((CHUNK, CHUNK), 0) == _iota((CHUNK, CHUNK), 1)).astype(F32)
    gcum = _running_sum(low, gc)
    gcum_t = _transposed(gcum, eye)
    causal, strict = _tri(), _tri(strict=True)
    heads = range(GDN_HEADS)
    q = [qkv[:, h * GDN_HEAD:(h + 1) * GDN_HEAD] for h in heads]
    k = [qkv[:, width + h * GDN_HEAD: width + (h + 1) * GDN_HEAD] for h in heads]
    v = [qkv[:, 2 * width + h * GDN_HEAD: 2 * width + (h + 1) * GDN_HEAD] for h in heads]
    beta = [sm[:, b0 + h:b0 + h + 1] for h in heads]
    gcol = [gcum[:, g0 + h:g0 + h + 1] for h in heads]
    glast = [gcum[CHUNK - 1:CHUNK, g0 + h:g0 + h + 1] for h in heads]
    gamma = [jnp.where(causal, jnp.exp(jnp.minimum(gcol[h] - gcum_t[g0 + h:g0 + h + 1, :], 0.0)), 0.0) for h in heads]
    kk = [_dot_nt(k[h], k[h]) for h in heads]
    qk = [_dot_nt(q[h], k[h]) for h in heads]
    egc = [jnp.exp(gcol[h]) for h in heads]
    a_mat = tuple(jnp.where(strict, kk[h] * gamma[h] * beta[h], 0.0) for h in heads)
    sol = _solve_unit_lower(a_mat, tuple(jnp.concatenate([v[h] * beta[h], k[h] * (beta[h] * egc[h])], axis=1)
                                         for h in heads))
    s_prev = [state[h * GDN_HEAD:(h + 1) * GDN_HEAD, :] for h in heads]
    w_s = [_dot(sol[h][:, GDN_HEAD:], s_prev[h]) for h in heads]
    q_s = [_dot(q[h] * egc[h], s_prev[h]) for h in heads]
    v_new = [sol[h][:, :GDN_HEAD] - w_s[h] for h in heads]
    a_v = [_dot(qk[h] * gamma[h], v_new[h]) for h in heads]
    k_v = [_dot_tn(k[h] * jnp.exp(glast[h] - gcol[h]), v_new[h]) for h in heads]
    outs = [q_s[h] + a_v[h] for h in heads]
    new_states = [s_prev[h] * jnp.exp(glast[h]) + k_v[h] for h in heads]
    return jnp.concatenate(new_states, axis=0), jnp.concatenate(outs, axis=1)


def _row_fns(d_model, pad_rows, loss_rows, alpha):
    def keep(ridx, v):
        return jnp.where(ridx >= pad_rows, v, 0.0)

    def both(h):
        return h, h.astype(BF16)

    def ln_in(ridx, h, g, b):
        return both(keep(ridx, _layer_norm(h, g, b)))

    def s5_tail(y, u, z, d, bias, glu, branch):
        v0 = jax.nn.gelu(y + d * u)
        return (branch(v0 * jax.nn.sigmoid(glu(v0) + bias) * _silu(z)),)

    def small_act(ridx, raw, bias, scale):
        lane = _iota(raw.shape, 1)
        sp = jax.nn.softplus(raw + bias)
        g0, b0 = SSD_HEADS, SSD_HEADS + GDN_HEADS
        out = jnp.where(lane < g0, sp, jnp.where(lane < b0, scale * sp,
                                                 jnp.where(lane < b0 + GDN_HEADS, jax.nn.sigmoid(raw), 0.0)))
        return (keep(ridx, out),)

    def conv(xs, w):
        acc = xs[0] * w[0:1, :]
        for j in range(1, CONV_K):
            acc = acc + xs[j] * w[j:j + 1, :]
        return acc

    def ssd_conv(ridx, x0, x1, x2, x3, w, b):
        return (keep(ridx, _silu(conv((x0, x1, x2, x3), w) + b)),)

    def ssd_tail(y, z, g, branch):
        v = y * _silu(z)
        return (branch(v * lax.rsqrt(jnp.mean(v * v, axis=-1, keepdims=True) + LN_EPS) * g),)

    def gdn_conv(x0, x1, x2, x3, w):
        a = _silu(conv((x0, x1, x2, x3), w))
        width = GDN_HEADS * GDN_HEAD
        parts = []
        for h in range(2 * GDN_HEADS):
            z = a[:, h * GDN_HEAD:(h + 1) * GDN_HEAD]
            z = z * lax.rsqrt(jnp.sum(z * z, axis=-1, keepdims=True) + 1e-6)
            parts.append(z * GDN_HEAD ** -0.5 if h < GDN_HEADS else z)
        parts.append(a[:, 2 * width:])
        return (jnp.concatenate(parts, axis=1),)

    def gdn_tail(o, z, g, branch):
        parts = []
        for h in range(GDN_HEADS):
            cols = slice(h * GDN_HEAD, (h + 1) * GDN_HEAD)
            oh = o[:, cols]
            oh = oh * lax.rsqrt(jnp.mean(oh * oh, axis=-1, keepdims=True) + LN_EPS) * g
            parts.append(oh * _silu(z[:, cols]))
        return (branch(jnp.concatenate(parts, axis=1)),)

    def merge_out(ridx, oa, ob, oc, gate, h, bias, g, b, w_out):
        acc = None
        for k, o in enumerate((oa, ob, oc)):
            cols = slice(k * d_model, (k + 1) * d_model)
            term = jax.nn.sigmoid(gate[:, cols] + bias[:, cols]) * o
            acc = term if acc is None else acc + term
        return both(keep(ridx, _layer_norm(alpha * h + w_out(acc), g, b)))

    def loss_rows_fn(ridx, h, tgt):
        row = 0.5 * jnp.mean(jnp.square(h - tgt), axis=-1, keepdims=True)
        row = jnp.where(ridx >= loss_rows, row, 0.0)
        lane = _iota((h.shape[0], LANES), 1)
        return (jnp.where(lane == 0, row, 0.0),)

    return dict(ln_in=ln_in, s5_tail=s5_tail, small_act=small_act, ssd_conv=ssd_conv, ssd_tail=ssd_tail,
                gdn_conv=gdn_conv, gdn_tail=gdn_tail, merge_out=merge_out, loss=loss_rows_fn)


def _pair_exchange(srcs, whole_srcs, name):
    n_arr, n_whole = len(srcs), len(whole_srcs)
    n_chip = N_DEV // 2
    base = n_arr * n_chip

    def body(*refs):
        src_refs, recv_refs = refs[:n_arr + n_whole], refs[n_arr + n_whole:2 * (n_arr + n_whole)]
        send_sems, recv_sems = refs[2 * (n_arr + n_whole):]
        x, y, c = lax.axis_index("x"), lax.axis_index("y"), lax.axis_index("c")

        def to_sibling(src, dst, k):
            return pltpu.make_async_remote_copy(src_ref=src, dst_ref=dst, send_sem=send_sems.at[k],
                                                recv_sem=recv_sems.at[k], device_id=(x, y, 1 - c),
                                                device_id_type=pl.DeviceIdType.MESH)

        copies = [to_sibling(src_refs[a].at[q, 1 - c], recv_refs[a].at[q], a * n_chip + q)
                  for a in range(n_arr) for q in range(n_chip)]
        copies += [to_sibling(src_refs[n_arr + b], recv_refs[n_arr + b], base + b) for b in range(n_whole)]
        for cp in copies:
            cp.start()
        for cp in copies:
            cp.wait()

    n_sem = base + n_whole
    return pl.pallas_call(
        body, name=name,
        in_specs=[pl.BlockSpec(memory_space=pl.ANY)] * (n_arr + n_whole),
        out_specs=[pl.BlockSpec(memory_space=pl.ANY)] * (n_arr + n_whole),
        out_shape=[jax.ShapeDtypeStruct((n_chip,) + tuple(s.shape[2:]), s.dtype) for s in srcs]
        + [jax.ShapeDtypeStruct(s.shape, s.dtype) for s in whole_srcs],
        scratch_shapes=[pltpu.SemaphoreType.DMA((n_sem,)), pltpu.SemaphoreType.DMA((n_sem,))],
    )(*srcs, *whole_srcs)


def _pair_sum(src, recv, core, name):
    n_chip, rows, cols = recv.shape
    tr = _row_tile(rows, 1024 if cols <= LANES else 512)

    def body(core_ref, a_ref, b_ref, o_ref):
        o_ref[...] = (a_ref[0].astype(F32) + b_ref[...].astype(F32)).astype(o_ref.dtype)

    blk = pl.BlockSpec((1, tr, cols), lambda q, i, core_ref: (q, i, 0))
    return pl.pallas_call(
        body, name=name,
        grid_spec=pltpu.PrefetchScalarGridSpec(
            num_scalar_prefetch=1, grid=(n_chip, rows // tr),
            in_specs=[pl.BlockSpec((1, 1, tr, cols), lambda q, i, core_ref: (q, core_ref[0], i, 0)), blk],
            out_specs=blk),
        out_shape=jax.ShapeDtypeStruct(recv.shape, recv.dtype),
        compiler_params=_params(("arbitrary", "arbitrary")),
    )(core, src, recv)


def _add2(a, b, name):
    rows = a.shape[0]
    tr = _row_tile(rows, FLAT_ROWS)
    blk = pl.BlockSpec((tr, LANES), lambda i: (i, 0))

    def body(a_ref, b_ref, o_ref):
        o_ref[...] = a_ref[...] + b_ref[...]

    return pl.pallas_call(body, name=name, grid=(rows // tr,), in_specs=[blk, blk], out_specs=blk,
                          out_shape=jax.ShapeDtypeStruct(a.shape, F32), compiler_params=_params(("arbitrary",)))(a, b)


def _add_parts(parts, name):
    n, rows, _ = parts.shape
    tr = _row_tile(rows, FLAT_ROWS)

    def body(p_ref, o_ref):
        acc = p_ref[0]
        for k in range(1, n):
            acc = acc + p_ref[k]
        o_ref[...] = acc

    return pl.pallas_call(body, name=name, grid=(rows // tr,),
                          in_specs=[pl.BlockSpec((n, tr, LANES), lambda i: (0, i, 0))],
                          out_specs=pl.BlockSpec((tr, LANES), lambda i: (i, 0)),
                          out_shape=jax.ShapeDtypeStruct((rows, LANES), F32),
                          compiler_params=_params(("arbitrary",)))(parts)


def _chip_exchange(chip_srcs, whole_srcs, name):
    n_chip_arr, n_whole = len(chip_srcs), len(whole_srcs)
    n_arr = n_chip_arr + n_whole
    n_chip = N_DEV // 2
    chip_flips = [(1, 0), (0, 1), (1, 1)]

    def body(*refs):
        src_refs, out_refs = refs[:n_arr], refs[n_arr:2 * n_arr]
        send_sems, recv_sems = refs[2 * n_arr:]
        x, y, c = lax.axis_index("x"), lax.axis_index("y"), lax.axis_index("c")
        my_chip = 2 * x + y
        copies = []
        for a in range(n_arr):
            for k, (fx, fy) in enumerate(chip_flips):
                px = 1 - x if fx else x
                py = 1 - y if fy else y
                src = src_refs[a].at[2 * px + py] if a < n_chip_arr else src_refs[a]
                copies.append(pltpu.make_async_remote_copy(
                    src_ref=src, dst_ref=out_refs[a].at[my_chip],
                    send_sem=send_sems.at[a * 3 + k], recv_sem=recv_sems.at[a * 3 + k],
                    device_id=(px, py, c), device_id_type=pl.DeviceIdType.MESH))
        for cp in copies:
            cp.start()
        for cp in copies:
            cp.wait()

    n_sem = n_arr * len(chip_flips)
    return pl.pallas_call(
        body, name=name,
        in_specs=[pl.BlockSpec(memory_space=pl.ANY)] * n_arr, out_specs=[pl.BlockSpec(memory_space=pl.ANY)] * n_arr,
        out_shape=[jax.ShapeDtypeStruct(s.shape, s.dtype) for s in chip_srcs]
        + [jax.ShapeDtypeStruct((n_chip,) + tuple(s.shape), s.dtype) for s in whole_srcs],
        scratch_shapes=[pltpu.SemaphoreType.DMA((n_sem,)), pltpu.SemaphoreType.DMA((n_sem,))],
    )(*chip_srcs, *whole_srcs)


def _gather(srcs, name):
    n_arr = len(srcs)
    n_sem = N_DEV - 1

    def body(*refs):
        src_refs, out_refs = refs[:n_arr], refs[n_arr:2 * n_arr]
        send_sems, recv_sems = refs[2 * n_arr:]
        x, y, c = lax.axis_index("x"), lax.axis_index("y"), lax.axis_index("c")
        me, sibling = (x, y, c), (x, y, 1 - c)
        chips = [(1 - x, y), (x, 1 - y), (1 - x, 1 - y)]

        def slot(a, dev):
            return out_refs[a].at[4 * dev[0] + 2 * dev[1] + dev[2]]

        def copy(a, k, block, to, own=False):
            return pltpu.make_async_remote_copy(
                src_ref=src_refs[a] if own else slot(a, block), dst_ref=slot(a, block),
                send_sem=send_sems.at[a * n_sem + k], recv_sem=recv_sems.at[a * n_sem + k],
                device_id=to, device_id_type=pl.DeviceIdType.MESH)

        arrays = range(n_arr)
        first = [copy(a, 0, me, sibling, own=True) for a in arrays]
        first += [copy(a, 1 + j, me, (*chip, c), own=True) for j, chip in enumerate(chips) for a in arrays]
        for cp in first:
            cp.start()
        passed = []
        for j, chip in enumerate(chips):
            for a in arrays:
                copy(a, 1 + j, (*chip, c), me).wait_recv()
                fwd = copy(a, 4 + j, (*chip, c), sibling)
                fwd.start()
                passed.append(fwd)
        for a in arrays:
            copy(a, 0, sibling, me).wait_recv()
        for j, chip in enumerate(chips):
            for a in arrays:
                copy(a, 4 + j, (*chip, 1 - c), me).wait_recv()
        for cp in first + passed:
            cp.wait_send()

    return pl.pallas_call(
        body, name=name,
        in_specs=[pl.BlockSpec(memory_space=pl.ANY)] * n_arr, out_specs=[pl.BlockSpec(memory_space=pl.ANY)] * n_arr,
        out_shape=[jax.ShapeDtypeStruct((N_DEV,) + tuple(s.shape), s.dtype) for s in srcs],
        scratch_shapes=[pltpu.SemaphoreType.DMA((n_arr * n_sem,)), pltpu.SemaphoreType.DMA((n_arr * n_sem,))],
    )(*srcs)


def _adamw_body(p_ref, w_ref, m_ref, v_ref, g_ref, d_ref, nm_ref, nv_ref):
    bc1 = 1.0 - ADAM_B1 ** ADAM_STEP
    bc2 = 1.0 - ADAM_B2 ** ADAM_STEP
    g = p_ref[0].astype(F32)
    for k in range(1, p_ref.shape[0]):
        g = g + p_ref[k].astype(F32)
    nm = ADAM_B1 * m_ref[...] + (1.0 - ADAM_B1) * g
    nv = ADAM_B2 * v_ref[...] + (1.0 - ADAM_B2) * jnp.square(g)
    m_hat = nm / bc1
    v_hat = nv / bc2
    g_ref[...] = g
    d_ref[...] = -ADAM_LR * (m_hat / (jnp.sqrt(v_hat) + ADAM_EPS) + ADAM_WD * w_ref[...])
    nm_ref[...] = nm
    nv_ref[...] = nv


def _adamw_rows(parts, w, m, v, name):
    rows, cols = w.shape
    tr = _row_tile(rows, max(128, (1 << 18) // cols))
    blk = pl.BlockSpec((tr, cols), lambda i: (i, 0))
    return pl.pallas_call(
        functools.partial(_adamw_body), name=name, grid=(rows // tr,),
        in_specs=[pl.BlockSpec((parts.shape[0], tr, cols), lambda i: (0, i, 0)), blk, blk, blk],
        out_specs=[blk] * 4, out_shape=[jax.ShapeDtypeStruct(w.shape, F32)] * 4,
        compiler_params=_params(("arbitrary",)),
    )(parts, w, m, v)


def _pad_flat(vec, rows):
    return jnp.pad(vec, (0, rows * LANES - vec.shape[0])).reshape(rows, LANES)


def _rows_for(n):
    return -(-n // (FLAT_ROWS * LANES)) * FLAT_ROWS


def _join_shards(rows, local_shape, dim):
    parts = jnp.moveaxis(rows.reshape((N_DEV,) + tuple(local_shape)), 0, dim)
    shp = tuple(local_shape)
    return parts.reshape(shp[:dim] + (N_DEV * shp[dim],) + shp[dim + 1:])


def _s5_tables(a_re, a_im, log_step, b_re, b_im, c_re, c_im):
    lam_re = jnp.minimum(a_re, -1e-4)
    lam_im = a_im
    step = jnp.exp(log_step)[:, None]
    mag = jnp.exp(lam_re * step)
    abar_re, abar_im = mag * jnp.cos(lam_im * step), mag * jnp.sin(lam_im * step)
    den = lam_re * lam_re + lam_im * lam_im
    nr, ni = abar_re - 1.0, abar_im
    coef_re = (nr * lam_re + ni * lam_im) / den
    coef_im = (ni * lam_re - nr * lam_im) / den
    bbar_re = coef_re[..., None] * b_re - coef_im[..., None] * b_im
    bbar_im = coef_re[..., None] * b_im + coef_im[..., None] * b_re
    groups = a_re.shape[0]
    nblk = groups // S5_BLOCK_GROUPS
    eye = jnp.eye(S5_BLOCK_GROUPS, dtype=F32)

    def in_blocks(bb):
        t = jnp.swapaxes(bb, 1, 2).reshape(nblk, S5_BLOCK_GROUPS, S5_GROUP, S5_STATE)
        blk = jnp.einsum('ab,jacp->jacbp', eye, t)
        return blk.reshape(nblk, S5_BLOCK_GROUPS * S5_GROUP, S5_BLOCK_GROUPS * S5_STATE)

    def out_blocks(cc):
        t = jnp.swapaxes(cc, 1, 2).reshape(nblk, S5_BLOCK_GROUPS, S5_STATE, S5_GROUP)
        blk = jnp.einsum('ab,japc->japbc', eye, t)
        return blk.reshape(nblk, S5_BLOCK_GROUPS * S5_STATE, S5_BLOCK_GROUPS * S5_GROUP)

    rows = groups * S5_STATE // LANES
    return dict(b_re=in_blocks(bbar_re), b_im=in_blocks(bbar_im), c_re=out_blocks(c_re), c_im=out_blocks(-c_im),
                a_re=abar_re.reshape(rows, LANES), a_im=abar_im.reshape(rows, LANES))


def _in_widths(d_model):
    return [BRANCH, BRANCH, SSD_HEADS * SSD_HEAD + 2 * SSD_GROUPS * SSD_STATE, SSD_HEADS, BRANCH,
            3 * BRANCH, GDN_HEADS, GDN_HEADS, BRANCH, 3 * d_model]


def _local_loss(w, mats, x, target):
    n_meta, d_model = w['meta'].shape
    depth = len(mats['in'])
    seq = x.shape[0]
    pad_rows = CHUNK - n_meta
    first = pad_rows + n_meta
    t_all = first + seq
    alpha = (2 * depth) ** 0.25
    fns = _row_fns(d_model, pad_rows, first, alpha)
    row = lambda nm, key, n_row, n_par, widths, cap, ridx=False: _make_rowwise(nm, fns[key], n_row, n_par, widths, cap, ridx)

    h = jnp.concatenate([jnp.zeros((pad_rows, d_model), F32), w['meta'], x], axis=0)
    h, h16 = _make_rowwise("ln_in", fns['ln_in'], 1, 2, [d_model, d_model], 416, True, out_dtypes=[F32, BF16])(
        h, w['ln_in_g'][None], w['ln_in_b'][None])

    n_small = SSD_HEADS + 2 * GDN_HEADS

    def small_cols(ps):
        return jnp.pad(jnp.concatenate([ps[3], ps[6], ps[7]], axis=1), ((0, 0), (0, LANES - n_small)))

    for l in range(depth):
        pw, pc = mats['in'][l], w['c_in'][l]
        used = (0, 1, 2, 4, 5, 8, 9)
        s5_u, s5_z, ssd_xbc, ssd_z, gdn_qkv, gdn_z, gate, small = _in_proj(
            ["in_s5u", "in_s5z", "in_ssdx", "in_ssdz", "in_gdnq", "in_gdnz", "in_gate", "in_small"], h, h16,
            [pw[i] for i in used] + [small_cols(pw)], [pc[i] for i in used] + [small_cols(pc)])

        zeros_tail = jnp.zeros((LANES - n_small,), F32)
        bias = jnp.concatenate([w['ssd_dt_bias'][l], w['gdn_dt_bias'][l], jnp.zeros((GDN_HEADS,), F32), zeros_tail])[None]
        scale = jnp.concatenate([jnp.ones((SSD_HEADS,), F32), -jnp.exp(w['gdn_a_log'][l]),
                                 jnp.zeros((GDN_HEADS,), F32), zeros_tail])[None]
        (sm,) = row("small_act", 'small_act', 1, 2, [LANES], 832, True)(small, bias, scale)

        tb = _s5_tables(w['s5_a_re'][l], w['s5_a_im'][l], w['s5_log_step'][l], w['s5_b_re'][l], w['s5_b_im'][l],
                        w['s5_c_re'][l], w['s5_c_im'][l])
        srows = tb['a_re'].shape[0]
        bu_re, bu_im = _s5_in(s5_u, tb['b_re'], tb['b_im'])
        s_re, s_im = _make_s5_scan("s5_scan")(bu_re.reshape(t_all, srows, LANES), bu_im.reshape(t_all, srows, LANES),
                                              tb['a_re'], tb['a_im'])
        y_s5 = _s5_out(s_re.reshape(t_all, srows * LANES), s_im.reshape(t_all, srows * LANES), tb['c_re'], tb['c_im'])
        (out_a,) = _make_rowwise("s5_tail", fns['s5_tail'], 3, 2, [d_model], 208, n_wt=2)(
            y_s5, s5_u, s5_z, w['s5_d'][l][None], w['s5_b_glu'][l][None],
            mats['glu'][l], mats['branch'][l, 0], w['c_glu'][l], w['c_branch'][l, 0])

        xbc = _make_conv_rowwise("ssd_conv", fns['ssd_conv'], 2, ssd_xbc.shape[1], 416, True)(
            ssd_xbc, w['ssd_conv_w'][l], w['ssd_conv_b'][l][None])
        a_c = jnp.pad(-jnp.exp(w['ssd_a_log'][l]), (0, LANES - SSD_HEADS))[None]
        d_exp = jnp.repeat(w['ssd_d'][l], SSD_HEAD)[None]
        (y_ssd,) = _make_chunk_scan("ssd_scan", _ssd_chunk, (SSD_HEADS // 2 * LANES, SSD_STATE), 2, 2, [BRANCH])(
            xbc, sm, a_c, d_exp)
        (out_b,) = _make_rowwise("ssd_tail", fns['ssd_tail'], 2, 1, [d_model], 416, n_wt=1)(
            y_ssd, ssd_z, w['ssd_norm_g'][l][None], mats['branch'][l, 1], w['c_branch'][l, 1])

        qkv = _make_conv_rowwise("gdn_conv", fns['gdn_conv'], 1, 3 * BRANCH, 208)(gdn_qkv, w['gdn_conv_w'][l])
        (o_gdn,) = _make_chunk_scan("gdn_scan", _gdn_chunk, (GDN_HEADS * GDN_HEAD, GDN_HEAD), 2, 0, [BRANCH])(qkv, sm)
        (out_c,) = _make_rowwise("gdn_tail", fns['gdn_tail'], 2, 1, [d_model], 416, n_wt=1)(
            o_gdn, gdn_z, w['gdn_norm_g'][l][None], mats['branch'][l, 2], w['c_branch'][l, 2])

        h, h16 = _make_rowwise("merge_out", fns['merge_out'], 5, 3, [d_model, d_model], 208, True, n_wt=1,
                               out_dtypes=[F32, BF16])(
            out_a, out_b, out_c, gate, h, w['b_gate'][l].reshape(1, 3 * d_model), w['ln_g'][l][None],
            w['ln_b'][l][None], mats['out'][l], w['c_out'][l])

    tgt = jnp.concatenate([jnp.zeros((first, d_model), F32), target], axis=0)
    (rows_loss,) = row("loss", 'loss', 2, 0, [LANES], 416, True)(h, tgt)
    return jnp.sum(rows_loss)


def _in_overlaps(d_model, n_loc):
    offs = [0]
    for wd in _in_widths(d_model):
        offs.append(offs[-1] + wd)
    out = []
    for i in range(len(offs) - 1):
        c0, c1 = offs[i], offs[i + 1]
        segs = []
        for k in range(N_DEV):
            g0, g1 = max(c0, k * n_loc), min(c1, (k + 1) * n_loc)
            if g0 < g1:
                segs.append((k, g0 - k * n_loc, g1 - k * n_loc, g0 - c0))
        out.append(segs)
    return out


STACKED = ['w_in', 's5_w_glu', 'w_branch', 'w_out']


def _shard_blocks(full, dim):
    shp = full.shape
    parts = full.reshape(shp[:dim] + (N_DEV, shp[dim] // N_DEV) + shp[dim + 1:])
    return jnp.moveaxis(parts, dim, 0)


def _join_blocks(blocks, dim):
    shp = blocks.shape[1:]
    return jnp.moveaxis(blocks, 0, dim).reshape(shp[:dim] + (N_DEV * shp[dim],) + shp[dim + 1:])


def _step(x, target, w_loc, m_loc, v_loc):
    small = [n for n in WEIGHTS if n in SHARD_DIM and n not in STACKED]
    repl = [n for n in WEIGHTS if n not in SHARD_DIM]
    size = lambda names: sum(int(w_loc[n].size) for n in names)
    depth, d_model, n_loc = w_loc['w_in'].shape
    overlaps = _in_overlaps(d_model, n_loc)
    n_chip = N_DEV // 2
    me = 4 * lax.axis_index("x") + 2 * lax.axis_index("y") + lax.axis_index("c")
    my_chip = 2 * lax.axis_index("x") + lax.axis_index("y")
    core = lax.axis_index("c").astype(jnp.int32).reshape(1)
    put = lambda buf, blk, idx: lax.dynamic_update_index_in_dim(buf, blk, idx, 0)
    take = lambda buf, idx: lax.dynamic_index_in_dim(buf, idx, 0, keepdims=False)

    rows_small = _rows_for(size(small))
    small_flat = _pad_flat(jnp.concatenate([w_loc[n].reshape(-1) for n in small]), rows_small)
    own_blocks = [w_loc[n].astype(BF16) for n in STACKED] + [small_flat]
    gathered = [put(got, blk, me) for got, blk in zip(_gather(own_blocks, "gather_weights"), own_blocks)]
    g_in = gathered[0]
    full = {n: _join_blocks(g, SHARD_DIM[n]) for n, g in zip(STACKED[1:], gathered[1:])}
    buf, off = gathered[-1].reshape(N_DEV, -1), 0
    for n in small:
        sz = int(w_loc[n].size)
        full[n] = _join_shards(buf[:, off:off + sz], w_loc[n].shape, SHARD_DIM[n])
        off += sz
    mats = dict(glu=full['s5_w_glu'], branch=full['w_branch'], out=full['w_out'], **{
        'in': [[jnp.concatenate([g_in[k, l, :, lo:hi] for k, lo, hi, _ in segs], axis=1) for segs in overlaps]
               for l in range(depth)]})
    w_diff = {n: w_loc[n] for n in repl}
    w_diff.update({n: full[n] for n in small})
    w_diff['c_in'] = [[jnp.zeros((d_model, wd), F32) for wd in _in_widths(d_model)] for _ in range(depth)]
    w_diff['c_glu'] = jnp.zeros(full['s5_w_glu'].shape, F32)
    w_diff['c_branch'] = jnp.zeros(full['w_branch'].shape, F32)
    w_diff['c_out'] = jnp.zeros(full['w_out'].shape, F32)

    loss, (g_w, g_x) = jax.value_and_grad(_local_loss, argnums=(0, 2))(w_diff, mats, x[0], target[0])
    g_w['s5_w_glu'], g_w['w_branch'], g_w['w_out'] = g_w['c_glu'], g_w['c_branch'], g_w['c_out']

    blocks = {'w_in': jnp.stack([jnp.stack([
        jnp.concatenate([g_w['c_in'][l][i][:, plo:plo + hi - lo]
                         for i, segs in enumerate(overlaps) for (kk, lo, hi, plo) in segs if kk == k], axis=1)
        for l in range(depth)]) for k in range(N_DEV)])}
    for n in STACKED[1:]:
        blocks[n] = _shard_blocks(g_w[n], SHARD_DIM[n])
    cols = {n: w_loc[n].shape[-1] for n in STACKED}
    send = [blocks[n].astype(BF16).reshape(n_chip, 2, -1, cols[n]) for n in STACKED]
    red_names = small + repl
    n_red = sum(int(g_w[n].size) for n in red_names) + 1
    red = _pad_flat(jnp.concatenate([g_w[n].reshape(-1) for n in red_names] + [loss.reshape(1)]), _rows_for(n_red))
    *recv, red_sib = _pair_exchange(send, [red], "pair_exchange")
    sums = [_pair_sum(s, r, core, "pair_sum_" + n) for s, r, n in zip(send, recv, STACKED)]
    red_chip = _add2(red, red_sib, "pair_sum_rest")
    *parts, red_parts = _chip_exchange(sums, [red_chip], "exchange_grads")
    parts = [put(p, take(s, my_chip), my_chip) for p, s in zip(parts, sums)]
    red_total = _add_parts(put(red_parts, red_chip, my_chip), "reduce_rest").reshape(-1)

    outs = {}
    for n, p in zip(STACKED, parts):
        as_rows = lambda a: a.reshape(-1, cols[n])
        res = _adamw_rows(p, as_rows(w_loc[n]), as_rows(m_loc[n]), as_rows(v_loc[n]), "adamw_" + n)
        outs[n] = [r.reshape(w_loc[n].shape) for r in res]
    g_red, off = {}, 0
    for n in red_names:
        sz = int(g_w[n].size)
        g_red[n] = red_total[off:off + sz].reshape(g_w[n].shape)
        off += sz
    loss_total = red_total[off]
    for n in small:
        g_red[n] = take(_shard_blocks(g_red[n], SHARD_DIM[n]), me)
    rows_rest = _rows_for(size(red_names))
    flat = lambda src: _pad_flat(jnp.concatenate([src[n].reshape(-1) for n in red_names]), rows_rest)
    res = _adamw_rows(flat(g_red)[None], flat(w_loc), flat(m_loc), flat(v_loc), "adamw_rest")
    off = 0
    for n in red_names:
        sz = int(w_loc[n].size)
        outs[n] = [r.reshape(-1)[off:off + sz].reshape(w_loc[n].shape) for r in res]
        off += sz
    return (loss_total, g_x[None], *[outs[n][k] for k in range(4) for n in WEIGHTS])


def kernel(x, meta, ln_in_g, ln_in_b, w_in, s5_a_re, s5_a_im, s5_log_step, s5_b_re, s5_b_im, s5_c_re, s5_c_im, s5_d, s5_w_glu, s5_b_glu, ssd_conv_w, ssd_conv_b, ssd_dt_bias, ssd_a_log, ssd_d, ssd_norm_g, gdn_conv_w, gdn_dt_bias, gdn_a_log, gdn_norm_g, w_branch, b_gate, w_out, ln_g, ln_b, loss_target, m_meta, m_ln_in_g, m_ln_in_b, m_w_in, m_s5_a_re, m_s5_a_im, m_s5_log_step, m_s5_b_re, m_s5_b_im, m_s5_c_re, m_s5_c_im, m_s5_d, m_s5_w_glu, m_s5_b_glu, m_ssd_conv_w, m_ssd_conv_b, m_ssd_dt_bias, m_ssd_a_log, m_ssd_d, m_ssd_norm_g, m_gdn_conv_w, m_gdn_dt_bias, m_gdn_a_log, m_gdn_norm_g, m_w_branch, m_b_gate, m_w_out, m_ln_g, m_ln_b, v_meta, v_ln_in_g, v_ln_in_b, v_w_in, v_s5_a_re, v_s5_a_im, v_s5_log_step, v_s5_b_re, v_s5_b_im, v_s5_c_re, v_s5_c_im, v_s5_d, v_s5_w_glu, v_s5_b_glu, v_ssd_conv_w, v_ssd_conv_b, v_ssd_dt_bias, v_ssd_a_log, v_ssd_d, v_ssd_norm_g, v_gdn_conv_w, v_gdn_dt_bias, v_gdn_a_log, v_gdn_norm_g, v_w_branch, v_b_gate, v_w_out, v_ln_g, v_ln_b):
    w_loc = dict(zip(WEIGHTS, (meta, ln_in_g, ln_in_b, w_in, s5_a_re, s5_a_im, s5_log_step, s5_b_re, s5_b_im, s5_c_re, s5_c_im, s5_d, s5_w_glu, s5_b_glu, ssd_conv_w, ssd_conv_b, ssd_dt_bias, ssd_a_log, ssd_d, ssd_norm_g, gdn_conv_w, gdn_dt_bias, gdn_a_log, gdn_norm_g, w_branch, b_gate, w_out, ln_g, ln_b)))
    m_loc = dict(zip(WEIGHTS, (m_meta, m_ln_in_g, m_ln_in_b, m_w_in, m_s5_a_re, m_s5_a_im, m_s5_log_step, m_s5_b_re, m_s5_b_im, m_s5_c_re, m_s5_c_im, m_s5_d, m_s5_w_glu, m_s5_b_glu, m_ssd_conv_w, m_ssd_conv_b, m_ssd_dt_bias, m_ssd_a_log, m_ssd_d, m_ssd_norm_g, m_gdn_conv_w, m_gdn_dt_bias, m_gdn_a_log, m_gdn_norm_g, m_w_branch, m_b_gate, m_w_out, m_ln_g, m_ln_b)))
    v_loc = dict(zip(WEIGHTS, (v_meta, v_ln_in_g, v_ln_in_b, v_w_in, v_s5_a_re, v_s5_a_im, v_s5_log_step, v_s5_b_re, v_s5_b_im, v_s5_c_re, v_s5_c_im, v_s5_d, v_s5_w_glu, v_s5_b_glu, v_ssd_conv_w, v_ssd_conv_b, v_ssd_dt_bias, v_ssd_a_log, v_ssd_d, v_ssd_norm_g, v_gdn_conv_w, v_gdn_dt_bias, v_gdn_a_log, v_gdn_norm_g, v_w_branch, v_b_gate, v_w_out, v_ln_g, v_ln_b)))
    return _step(x, loss_target, w_loc, m_loc, v_loc)
```
